```python
import jax, jax.numpy as jnp
from jax import lax
import numpy as np

D_MODEL = 1024
BATCH = 8
SEQ = 2048
DEPTH = 2
DEC_BATCH = 128
DEC_SEQ = 1
PAST_LEN = 16384
PAGE_SIZE = 128

MIX_WIDTH = D_MODEL
HG_WIDTH = MIX_WIDTH // 2
HG_HEADS = 4
HG_DK = HG_WIDTH // HG_HEADS
HG_DV = HG_WIDTH // HG_HEADS
HG_CHUNK = 64
LRU_WIDTH = MIX_WIDTH - HG_WIDTH
LRU_BLOCKS = 8
LRU_BW = LRU_WIDTH // LRU_BLOCKS
LRU_C = 8.0
CONV_K = 4
IN_WIDTH = 4 * HG_WIDTH + 2 * LRU_WIDTH
D_FF = 2816
N_EXPERTS = 8
TOP_K = 2
D_FF_EXPERT = 3584
N_DENSE = (DEPTH + 1) // 2
N_MOE = DEPTH // 2
EPS = 1e-6

kernel_name = 'hymba_style_hgrn2_rglru_decoder_step'


def rmsnorm(x, g):
    xf = x.astype(jnp.float32)
    y = xf * lax.rsqrt(jnp.mean(xf * xf, axis=-1, keepdims=True) + EPS)
    return (y * g.astype(jnp.float32)).astype(x.dtype)


def hgrn2_chunked(q, k, v, log_f, s0):
    B, L = q.shape[0], q.shape[1]
    C = min(HG_CHUNK, L)
    n_chunks = -(-L // C)
    pad = n_chunks * C - L
    if pad:
        pw = ((0, 0), (0, pad), (0, 0), (0, 0))
        q, k, v, log_f = [jnp.pad(t, pw) for t in (q, k, v, log_f)]

    def to_chunks(t):
        return t.reshape(B, n_chunks, C, t.shape[2], t.shape[3]).transpose(1, 0, 2, 3, 4)

    mask = jnp.tril(jnp.ones((C, C), dtype=bool))[None, :, :, None, None]

    def step(S, inp):
        qc, kc, vc, gc = inp
        b = jnp.cumsum(gc, axis=1)
        diff = b[:, :, None] - b[:, None, :]
        decay = jnp.exp(jnp.where(mask, diff, -jnp.inf))
        scores = jnp.sum(qc[:, :, None] * kc[:, None] * decay, axis=-1)
        o_intra = jnp.einsum('btsh,bshv->bthv', scores, vc)
        o_inter = jnp.einsum('bthk,bhkv->bthv', qc * jnp.exp(b), S)
        b_last = b[:, -1]
        k_dec = kc * jnp.exp(b_last[:, None] - b)
        S_new = jnp.exp(b_last)[..., None] * S + jnp.einsum('bshk,bshv->bhkv', k_dec, vc)
        return S_new, o_intra + o_inter

    S_fin, o = lax.scan(step, s0, (to_chunks(q), to_chunks(k), to_chunks(v), to_chunks(log_f)))
    o = o.transpose(1, 0, 2, 3, 4).reshape(B, n_chunks * C, HG_HEADS, HG_DV)[:, :L]
    return o, S_fin


def causal_conv(x, buf, w, b):
    L = x.shape[1]
    xx = jnp.concatenate([buf.astype(x.dtype), x], axis=1)
    y = b.astype(jnp.float32)
    for j in range(CONV_K):
        y = y + xx[:, j:j + L].astype(jnp.float32) * w[j].astype(jnp.float32)
    return y, xx[:, L:]


def lru_combine(e1, e2):
    a1, u1 = e1
    a2, u2 = e2
    return a1 * a2, a2 * u1 + u2


def mixer(h, s_hg, s_lru, s_conv, lb, w_in, hg_norm_g, conv_w, conv_b,
          w_ra, b_ra, w_ix, b_ix, lru_lambda, w_out):
    f32 = jnp.float32
    B, L, _ = h.shape
    z = h @ w_in
    splits = [HG_WIDTH, 2 * HG_WIDTH, 3 * HG_WIDTH, 4 * HG_WIDTH, 4 * HG_WIDTH + LRU_WIDTH]
    q_pre, f_pre, i_pre, g_pre, xr, gr = jnp.split(z, splits, axis=-1)

    shp = (B, L, HG_HEADS, HG_DK)
    q = jax.nn.silu(q_pre.astype(f32)).reshape(shp)
    fz = f_pre.astype(f32).reshape(shp)
    lbh = lb.reshape(HG_HEADS, HG_DK)
    log_f = jnp.logaddexp(jnp.log(lbh), jnp.log1p(-lbh) + jax.nn.log_sigmoid(fz))
    k = (1.0 - lbh) * jax.nn.sigmoid(-fz)
    v = i_pre.astype(f32).reshape(B, L, HG_HEADS, HG_DV)
    o, s_hg_new = hgrn2_chunked(q, k, v, log_f, s_hg.astype(f32))
    o = o * lax.rsqrt(jnp.mean(o * o, axis=-1, keepdims=True) + EPS)
    o = o * hg_norm_g.astype(f32).reshape(HG_HEADS, HG_DV)
    o = o.reshape(B, L, HG_WIDTH) * jax.nn.silu(g_pre.astype(f32))

    xc, conv_new = causal_conv(xr, s_conv, conv_w, conv_b)
    xcb = xc.reshape(B, L, LRU_BLOCKS, LRU_BW)
    r = jax.nn.sigmoid(jnp.einsum('blnc,ncd->blnd', xcb, w_ra.astype(f32))
                       + b_ra.astype(f32).reshape(LRU_BLOCKS, LRU_BW)).reshape(B, L, LRU_WIDTH)
    ig = jax.nn.sigmoid(jnp.einsum('blnc,ncd->blnd', xcb, w_ix.astype(f32))
                        + b_ix.astype(f32).reshape(LRU_BLOCKS, LRU_BW)).reshape(B, L, LRU_WIDTH)
    log_a = -LRU_C * jax.nn.softplus(-lru_lambda.astype(f32)) * r
    a = jnp.exp(log_a)
    u = jnp.sqrt(-jnp.expm1(2.0 * log_a)) * (ig * xc)
    u = u.at[:, 0].add(a[:, 0] * s_lru.astype(f32))
    _, hs = lax.associative_scan(lru_combine, (a, u), axis=1)
    lru_out = hs * jax.nn.gelu(gr.astype(f32))

    merged = jnp.concatenate([o, lru_out], axis=-1).astype(h.dtype)
    out = merged @ w_out
    return (out.astype(h.dtype), s_hg_new.astype(s_hg.dtype),
            hs[:, -1].astype(s_lru.dtype), conv_new.astype(s_conv.dtype))


def swiglu(h, w1, w3, w2):
    return (jax.nn.silu(h @ w1) * (h @ w3)) @ w2


def moe_ffn(h, router_w, w1, w3, w2):
    f32 = jnp.float32
    B, L, D = h.shape
    t = h.reshape(B * L, D)
    logits = (t @ router_w).astype(f32)
    top_v, top_i = lax.top_k(logits, TOP_K)
    probs = jax.nn.softmax(top_v, axis=-1)
    gates = jnp.sum(jax.nn.one_hot(top_i, N_EXPERTS, dtype=f32) * probs[..., None], axis=1)
    out = jnp.zeros((B * L, D), f32)
    for e in range(N_EXPERTS):
        y = swiglu(t, w1[e], w3[e], w2[e]).astype(f32)
        out = out + gates[:, e:e + 1] * y
    return out.astype(h.dtype).reshape(B, L, D)


def trunk(x, s_hg, s_lru, s_conv, norm1_g, w_in, lower_bounds, hg_norm_g, conv_w, conv_b,
          w_ra, b_ra, w_ix, b_ix, lru_lambda, w_out, norm2_g, ffn_w1, ffn_w3, ffn_w2,
          router_w, moe_w1, moe_w3, moe_w2, final_norm_g):
    lb_all = jnp.cumsum(jax.nn.softmax(lower_bounds.astype(jnp.float32), axis=0), axis=0)
    lb_all = lb_all - lb_all[0]
    new_hg, new_lru, new_conv = [], [], []
    for l in range(DEPTH):
        h = rmsnorm(x, norm1_g[l])
        m, s1, s2, s3 = mixer(h, s_hg[l], s_lru[l], s_conv[l], lb_all[l], w_in[l], hg_norm_g[l],
                              conv_w[l], conv_b[l], w_ra[l], b_ra[l], w_ix[l], b_ix[l],
                              lru_lambda[l], w_out[l])
        x = x + m
        new_hg.append(s1)
        new_lru.append(s2)
        new_conv.append(s3)
        h = rmsnorm(x, norm2_g[l])
        if l % 2 == 0:
            j = l // 2
            x = x + swiglu(h, ffn_w1[j], ffn_w3[j], ffn_w2[j]).astype(x.dtype)
        else:
            j = l // 2
            x = x + moe_ffn(h, router_w[j], moe_w1[j], moe_w3[j], moe_w2[j])
    y = rmsnorm(x, final_norm_g)
    return y, jnp.stack(new_hg), jnp.stack(new_lru), jnp.stack(new_conv)


def setup_inputs(seed: int = 0) -> dict:
    key = jax.random.key(seed)
    ks = jax.random.split(key, 32)
    f32 = jnp.float32

    def nrm(k, shape, scale):
        return jax.random.normal(k, shape, f32) * scale

    u = jax.random.uniform(ks[15], (DEPTH, LRU_WIDTH), f32, 0.9, 0.999)
    a0 = u ** (1.0 / LRU_C)
    return {
        'x_prompt': nrm(ks[0], (BATCH, SEQ, D_MODEL), 1.0),
        'x_sample': nrm(ks[1], (DEC_BATCH, DEC_SEQ, D_MODEL), 1.0),
        'state_hgrn': nrm(ks[2], (DEPTH, DEC_BATCH, HG_HEADS, HG_DK, HG_DV), 0.3),
        'state_lru': nrm(ks[3], (DEPTH, DEC_BATCH, LRU_WIDTH), 0.5),
        'state_conv': nrm(ks[4], (DEPTH, DEC_BATCH, CONV_K - 1, LRU_WIDTH), 1.0),
        'norm1_g': 1.0 + nrm(ks[5], (DEPTH, D_MODEL), 0.02),
        'w_in': nrm(ks[6], (DEPTH, D_MODEL, IN_WIDTH), D_MODEL ** -0.5),
        'lower_bounds': nrm(ks[7], (DEPTH, HG_WIDTH), 1.0),
        'hg_norm_g': 1.0 + nrm(ks[8], (DEPTH, HG_WIDTH), 0.02),
        'conv_w': nrm(ks[9], (DEPTH, CONV_K, LRU_WIDTH), CONV_K ** -0.5),
        'conv_b': nrm(ks[10], (DEPTH, LRU_WIDTH), 0.01),
        'w_ra': nrm(ks[11], (DEPTH, LRU_BLOCKS, LRU_BW, LRU_BW), LRU_BW ** -0.5),
        'b_ra': nrm(ks[12], (DEPTH, LRU_WIDTH), 0.01),
        'w_ix': nrm(ks[13], (DEPTH, LRU_BLOCKS, LRU_BW, LRU_BW), LRU_BW ** -0.5),
        'b_ix': nrm(ks[14], (DEPTH, LRU_WIDTH), 0.01),
        'lru_lambda': jnp.log(a0) - jnp.log1p(-a0),
        'w_out': nrm(ks[16], (DEPTH, MIX_WIDTH, D_MODEL), MIX_WIDTH ** -0.5),
        'norm2_g': 1.0 + nrm(ks[17], (DEPTH, D_MODEL), 0.02),
        'ffn_w1': nrm(ks[18], (N_DENSE, D_MODEL, D_FF), D_MODEL ** -0.5),
        'ffn_w3': nrm(ks[19], (N_DENSE, D_MODEL, D_FF), D_MODEL ** -0.5),
        'ffn_w2': nrm(ks[20], (N_DENSE, D_FF, D_MODEL), D_FF ** -0.5),
        'router_w': nrm(ks[21], (N_MOE, D_MODEL, N_EXPERTS), D_MODEL ** -0.5),
        'moe_w1': nrm(ks[22], (N_MOE, N_EXPERTS, D_MODEL, D_FF_EXPERT), D_MODEL ** -0.5),
        'moe_w3': nrm(ks[23], (N_MOE, N_EXPERTS, D_MODEL, D_FF_EXPERT), D_MODEL ** -0.5),
        'moe_w2': nrm(ks[24], (N_MOE, N_EXPERTS, D_FF_EXPERT, D_MODEL), D_FF_EXPERT ** -0.5),
        'final_norm_g': 1.0 + nrm(ks[25], (D_MODEL,), 0.02),
    }


def reference(x_prompt, x_sample, state_hgrn, state_lru, state_conv, norm1_g, w_in, lower_bounds,
              hg_norm_g, conv_w, conv_b, w_ra, b_ra, w_ix, b_ix, lru_lambda, w_out, norm2_g,
              ffn_w1, ffn_w3, ffn_w2, router_w, moe_w1, moe_w3, moe_w2, final_norm_g):
    params = (norm1_g, w_in, lower_bounds, hg_norm_g, conv_w, conv_b, w_ra, b_ra, w_ix, b_ix,
              lru_lambda, w_out, norm2_g, ffn_w1, ffn_w3, ffn_w2, router_w, moe_w1, moe_w3,
              moe_w2, final_norm_g)
    bp = x_prompt.shape[0]
    z_hg = jnp.zeros((DEPTH, bp, HG_HEADS, HG_DK, HG_DV), state_hgrn.dtype)
    z_lru = jnp.zeros((DEPTH, bp, LRU_WIDTH), state_lru.dtype)
    z_conv = jnp.zeros((DEPTH, bp, CONV_K - 1, LRU_WIDTH), state_conv.dtype)
    y_prompt, hg_p, lru_p, conv_p = trunk(x_prompt, z_hg, z_lru, z_conv, *params)
    y_sample, hg_s, lru_s, conv_s = trunk(x_sample, state_hgrn, state_lru, state_conv, *params)
    return (y_prompt, y_sample, hg_p, lru_p, conv_p, hg_s, lru_s, conv_s)
```

```python
import functools

import jax
import jax.numpy as jnp
from jax import lax
from jax.experimental import pallas as pl
from jax.experimental.pallas import tpu as pltpu

F32 = jnp.float32
BF16 = jnp.bfloat16
I32 = jnp.int32

D_MODEL = 1024
HG_WIDTH = 512
HG_HEADS = 4
HG_D = 128
LRU_WIDTH = 512
LRU_BLOCKS = 8
LRU_C = 8.0
CONV_K = 4
IN_WIDTH = 4 * HG_WIDTH + 2 * LRU_WIDTH
N_EXPERTS = 8
EPS = 1e-6

CHUNK = 64
DIAG = 8
LANES = 128
VMEM_LIMIT_BYTES = 56 * 1024 * 1024


def _cparams(sem):
    return pltpu.CompilerParams(dimension_semantics=sem, vmem_limit_bytes=VMEM_LIMIT_BYTES)


def _sigmoid(x):
    return 1.0 / (1.0 + jnp.exp(-x))


def _silu(x):
    return x * _sigmoid(x)


def _gelu_tanh(x):
    c = 0.7978845608028654
    return 0.5 * x * (1.0 + jnp.tanh(c * (x + 0.044715 * (x * x * x))))


def _rms(x, g):
    return x * lax.rsqrt(jnp.mean(x * x, axis=-1, keepdims=True) + EPS) * g


def _dot(a, b):
    return jnp.dot(a, b, preferred_element_type=F32)


def _dot_nt(a, b):
    return lax.dot_general(a, b, (((1,), (1,)), ((), ())), preferred_element_type=F32)


def _dot_tn(a, b):
    return lax.dot_general(a, b, (((0,), (0,)), ((), ())), preferred_element_type=F32)


def _split3(x):
    x1 = x.astype(BF16)
    r1 = x - x1.astype(F32)
    x2 = r1.astype(BF16)
    x3 = (r1 - x2.astype(F32)).astype(BF16)
    return x1, x2, x3


def _linear_kernel(*refs, has_gain, has_res):
    it = iter(refs)
    x_ref = next(it)
    g_ref = next(it) if has_gain else None
    w_ref = next(it)
    r_ref = next(it) if has_res else None
    o_ref = next(it)
    h_scr = next(it)

    @pl.when(pl.program_id(1) == 0)
    def _():
        x = x_ref[...].astype(F32)
        if has_gain:
            x = _rms(x, g_ref[...])
        h_scr[...] = x.astype(BF16)

    acc = _dot(h_scr[...], w_ref[...])
    if has_res:
        acc = acc + r_ref[...]
    o_ref[...] = acc.astype(o_ref.dtype)


def _linear(x, w, gain=None, res=None, tm=1024, tn=1024):
    m, k = x.shape
    n = w.shape[1]
    tm = min(tm, m)
    tn = min(tn, n)
    in_specs = [pl.BlockSpec((tm, k), lambda i, j: (i, 0))]
    args = [x]
    if gain is not None:
        in_specs.append(pl.BlockSpec((1, k), lambda i, j: (0, 0)))
        args.append(gain.reshape(1, k))
    in_specs.append(pl.BlockSpec((k, tn), lambda i, j: (0, j)))
    args.append(w)
    if res is not None:
        in_specs.append(pl.BlockSpec((tm, tn), lambda i, j: (i, j)))
        args.append(res)
    return pl.pallas_call(
        functools.partial(_linear_kernel, has_gain=gain is not None, has_res=res is not None),
        out_shape=jax.ShapeDtypeStruct((m, n), F32),
        grid=(m // tm, n // tn),
        in_specs=in_specs,
        out_specs=pl.BlockSpec((tm, tn), lambda i, j: (i, j)),
        scratch_shapes=[pltpu.VMEM((tm, k), BF16)],
        compiler_params=_cparams(("parallel", "arbitrary")),
        name="linear",
    )(*args)


def _ffn_kernel(te_ref, nv_ref, *refs, has_gain, has_res, tf, ck):
    del te_ref
    it = iter(refs)
    x_ref = next(it)
    g_ref = next(it) if has_gain else None
    w1_ref, w3_ref, w2_ref, o_ref, xb_scr, a_scr, acc_scr = it
    i = pl.program_id(0)
    j = pl.program_id(1)
    nj = pl.num_programs(1)
    valid = i < nv_ref[0]

    @pl.when(valid)
    def _():
        @pl.when(j == 0)
        def _():
            x = x_ref[...]
            if has_gain:
                x = _rms(x, g_ref[...])
            xb_scr[...] = x.astype(BF16)

        xb = xb_scr[...]
        for c in range(tf // ck):
            h1 = _dot(xb, w1_ref[0, :, c * ck:(c + 1) * ck])
            h3 = _dot(xb, w3_ref[0, :, c * ck:(c + 1) * ck])
            a_scr[:, c * ck:(c + 1) * ck] = (_silu(h1) * h3).astype(BF16)
        part = _dot(a_scr[...], w2_ref[0])

        @pl.when(j == 0)
        def _():
            acc_scr[...] = part

        @pl.when(j > 0)
        def _():
            acc_scr[...] += part

        @pl.when(j == nj - 1)
        def _():
            out = acc_scr[...]
            if has_res:
                out = x_ref[...] + out
            o_ref[...] = out

    @pl.when(jnp.logical_and(jnp.logical_not(valid), j == nj - 1))
    def _():
        o_ref[...] = jnp.zeros_like(o_ref)


def _ffn(x, w1, w3, w2, tile_expert, n_valid, gain=None, res=False, tm=512, n_f=2, ck=256):
    p, d = x.shape
    f = w1.shape[2]
    tm = min(tm, p)
    tf = f // n_f
    n_tiles = p // tm
    in_specs = [pl.BlockSpec((tm, d), lambda i, j, te, nv: (i, 0))]
    args = [x]
    if gain is not None:
        in_specs.append(pl.BlockSpec((1, d), lambda i, j, te, nv: (0, 0)))
        args.append(gain.reshape(1, d))
    in_specs += [
        pl.BlockSpec((1, d, tf), lambda i, j, te, nv: (te[i], 0, j)),
        pl.BlockSpec((1, d, tf), lambda i, j, te, nv: (te[i], 0, j)),
        pl.BlockSpec((1, tf, d), lambda i, j, te, nv: (te[i], j, 0)),
    ]
    args += [w1, w3, w2]
    grid_spec = pltpu.PrefetchScalarGridSpec(
        num_scalar_prefetch=2,
        grid=(n_tiles, n_f),
        in_specs=in_specs,
        out_specs=pl.BlockSpec((tm, d), lambda i, j, te, nv: (i, 0)),
        scratch_shapes=[pltpu.VMEM((tm, d), BF16), pltpu.VMEM((tm, tf), BF16),
                        pltpu.VMEM((tm, d), F32)],
    )
    return pl.pallas_call(
        functools.partial(_ffn_kernel, has_gain=gain is not None, has_res=res, tf=tf, ck=ck),
        out_shape=jax.ShapeDtypeStruct((p, d), F32),
        grid_spec=grid_spec,
        compiler_params=_cparams(("arbitrary", "arbitrary")),
        name="swiglu",
    )(tile_expert, n_valid, *args)


def _hgrn_gates(fz, loglb, l1mlb, oneml):
    e = jnp.exp(-jnp.abs(fz))
    log_sig = jnp.minimum(fz, 0.0) - jnp.log1p(e)
    cc = l1mlb + log_sig
    log_f = jnp.maximum(loglb, cc) + jnp.log1p(jnp.exp(-jnp.abs(loglb - cc)))
    k = oneml * _sigmoid(-fz)
    return log_f, k


def _lru_gates(xc, wra, bra, wix, bix, c8):
    xb = xc.astype(BF16)
    r = _sigmoid(_dot(xb, wra) + bra)
    ig = _sigmoid(_dot(xb, wix) + bix)
    log_a = c8 * r
    a = jnp.exp(log_a)
    one_m_a2 = -jnp.tanh(log_a) * (a * a + 1.0)
    u = jnp.sqrt(one_m_a2) * (ig * xc)
    return a, u


def _mixer_prompt_kernel(z_ref, lbp_ref, hgg_ref, cw_ref, cb_ref, wra_ref, bra_ref, wix_ref,
                         bix_ref, c8_ref, cm_ref, m_ref, hs_ref, ls_ref, cs_ref,
                         st_scr, h_scr, xpad_scr, xc_scr, *, tl):
    l = pl.program_id(1)
    nl = pl.num_programs(1)

    @pl.when(l == 0)
    def _():
        st_scr[...] = jnp.zeros_like(st_scr)
        h_scr[...] = jnp.zeros_like(h_scr)
        xpad_scr[0:8, :] = jnp.zeros((8, LRU_WIDTH), F32)

    xpad_scr[8:8 + tl, :] = z_ref[0, :, 4 * HG_WIDTH:4 * HG_WIDTH + LRU_WIDTH]
    xc = cb_ref[...]
    for j in range(CONV_K):
        xc = xc + xpad_scr[5 + j:5 + j + tl, :] * cw_ref[j:j + 1, :]
    xc_scr[...] = xc
    tail = xpad_scr[tl + 5:tl + 8, :]
    xpad_scr[5:8, :] = tail

    loglb = lbp_ref[0:1, :]
    l1mlb = lbp_ref[1:2, :]
    oneml = lbp_ref[2:3, :]
    row = lax.broadcasted_iota(I32, (CHUNK, 1), 0)
    ti = lax.broadcasted_iota(I32, (CHUNK, CHUNK), 0)
    si = lax.broadcasted_iota(I32, (CHUNK, CHUNK), 1)
    t3 = lax.broadcasted_iota(I32, (1, DIAG, 1), 1)
    nblk = CHUNK // DIAG

    def chunk_body(c, carry):
        r0 = pl.multiple_of(c * CHUNK, CHUNK)
        rows = pl.ds(r0, CHUNK)

        xc_c = xc_scr[rows, :]
        a, u = _lru_gates(xc_c, wra_ref[...], bra_ref[...], wix_ref[...], bix_ref[...], c8_ref[...])
        gr = z_ref[0, rows, 4 * HG_WIDTH + LRU_WIDTH:IN_WIDTH]
        for gi in range(LRU_WIDTH // LANES):
            cs = slice(gi * LANES, (gi + 1) * LANES)
            aa = a[:, cs]
            uu = u[:, cs]
            d = 1
            while d < CHUNK:
                keep = row >= d
                a_sh = pltpu.roll(aa, d, 0)
                u_sh = pltpu.roll(uu, d, 0)
                uu = jnp.where(keep, aa * u_sh + uu, uu)
                aa = jnp.where(keep, aa * a_sh, aa)
                d *= 2
            hs = uu + aa * h_scr[:, cs]
            h_scr[:, cs] = hs[CHUNK - 1:CHUNK, :]
            m_ref[0, rows, HG_WIDTH + gi * LANES:HG_WIDTH + (gi + 1) * LANES] = (
                hs * _gelu_tanh(gr[:, cs])).astype(BF16)

        q = _silu(z_ref[0, rows, 0:HG_WIDTH])
        log_f, k = _hgrn_gates(z_ref[0, rows, HG_WIDTH:2 * HG_WIDTH], loglb, l1mlb, oneml)
        v = z_ref[0, rows, 2 * HG_WIDTH:3 * HG_WIDTH]
        gate = _silu(z_ref[0, rows, 3 * HG_WIDTH:4 * HG_WIDTH])
        f1, f2, f3 = _split3(log_f)
        cm = cm_ref[...]
        b4 = _dot(cm, f1) + _dot(cm, f2) + _dot(cm, f3)
        b = b4[0:CHUNK]

        for hh in range(HG_HEADS):
            cs = slice(hh * HG_D, (hh + 1) * HG_D)
            qh, kh, vh, bh = q[:, cs], k[:, cs], v[:, cs], b[:, cs]
            vb = vh.astype(BF16)
            amat = jnp.zeros((CHUNK, CHUNK), F32)
            lvl = 1
            w = CHUNK // 2
            while w >= DIAG:
                rw = b4[lvl * CHUNK:(lvl + 1) * CHUNK, cs]
                upper = (row & (2 * w - 1)) >= w
                qw = jnp.where(upper, qh * jnp.exp(jnp.minimum(bh - rw, 0.0)), 0.0).astype(BF16)
                kw = jnp.where(upper, 0.0, kh * jnp.exp(jnp.minimum(rw - bh, 0.0))).astype(BF16)
                aw = _dot_nt(qw, kw)
                if 2 * w < CHUNK:
                    sh = (2 * w).bit_length() - 1
                    aw = jnp.where((ti >> sh) == (si >> sh), aw, 0.0)
                amat = amat + aw
                lvl += 1
                w //= 2
            o = _dot(amat.astype(BF16), vb)
            st = st_scr[hh]
            o = o + _dot_nt((qh * jnp.exp(bh)).astype(BF16), st.astype(BF16))
            q3 = qh.reshape(nblk, DIAG, HG_D)
            k3 = kh.reshape(nblk, DIAG, HG_D)
            v3 = vh.reshape(nblk, DIAG, HG_D)
            b3 = bh.reshape(nblk, DIAG, HG_D)
            od = jnp.zeros((nblk, DIAG, HG_D), F32)
            for s in range(DIAG):
                dec = jnp.exp(jnp.minimum(b3 - b3[:, s:s + 1, :], 0.0))
                p = q3 * (k3[:, s:s + 1, :] * dec)
                rs = jnp.sum(p, axis=-1, keepdims=True)
                rs = jnp.where(t3 >= s, rs, 0.0)
                od = od + rs * v3[:, s:s + 1, :]
            o = o + od.reshape(CHUNK, HG_D)
            bl = bh[CHUNK - 1:CHUNK, :]
            kdec = (kh * jnp.exp(bl - bh)).astype(BF16)
            st_scr[hh] = st * jnp.exp(bl) + _dot_tn(vb, kdec)
            o = o * lax.rsqrt(jnp.mean(o * o, axis=-1, keepdims=True) + EPS)
            o = o * hgg_ref[:, cs]
            o = o * gate[:, cs]
            m_ref[0, rows, cs] = o.astype(BF16)
        return carry

    lax.fori_loop(0, tl // CHUNK, chunk_body, 0)

    @pl.when(l == nl - 1)
    def _():
        for hh in range(HG_HEADS):
            hs_ref[0, hh] = st_scr[hh].T
        ls_ref[0] = h_scr[...]
        cs_ref[0] = xpad_scr[5:8, :]


def _mixer_prompt(z, mp, tl=512):
    bsz, seq, _ = z.shape
    tl = min(tl, seq)
    full = lambda shape: pl.BlockSpec(shape, lambda b, l: (0,) * len(shape))
    out_shapes = (
        jax.ShapeDtypeStruct((bsz, seq, D_MODEL), BF16),
        jax.ShapeDtypeStruct((bsz, HG_HEADS, HG_D, HG_D), F32),
        jax.ShapeDtypeStruct((bsz, 1, LRU_WIDTH), F32),
        jax.ShapeDtypeStruct((bsz, CONV_K - 1, LRU_WIDTH), F32),
    )
    merged, hg, lru, conv = pl.pallas_call(
        functools.partial(_mixer_prompt_kernel, tl=tl),
        out_shape=out_shapes,
        grid=(bsz, seq // tl),
        in_specs=[
            pl.BlockSpec((1, tl, IN_WIDTH), lambda b, l: (b, l, 0)),
            full((3, HG_WIDTH)), full((1, HG_WIDTH)), full((CONV_K, LRU_WIDTH)),
            full((1, LRU_WIDTH)), full((LRU_WIDTH, LRU_WIDTH)), full((1, LRU_WIDTH)),
            full((LRU_WIDTH, LRU_WIDTH)), full((1, LRU_WIDTH)), full((1, LRU_WIDTH)),
            full((4 * CHUNK, CHUNK)),
        ],
        out_specs=(
            pl.BlockSpec((1, tl, D_MODEL), lambda b, l: (b, l, 0)),
            pl.BlockSpec((1, HG_HEADS, HG_D, HG_D), lambda b, l: (b, 0, 0, 0)),
            pl.BlockSpec((1, 1, LRU_WIDTH), lambda b, l: (b, 0, 0)),
            pl.BlockSpec((1, CONV_K - 1, LRU_WIDTH), lambda b, l: (b, 0, 0)),
        ),
        scratch_shapes=[
            pltpu.VMEM((HG_HEADS, HG_D, HG_D), F32),
            pltpu.VMEM((1, LRU_WIDTH), F32),
            pltpu.VMEM((tl + 8, LRU_WIDTH), F32),
            pltpu.VMEM((tl, LRU_WIDTH), F32),
        ],
        compiler_params=_cparams(("parallel", "arbitrary")),
        name="mixer_prompt",
    )(z, mp["lbp"], mp["hgg"], mp["cw"], mp["cb"], mp["wra"], mp["bra"], mp["wix"], mp["bix"],
      mp["c8"], mp["cm"])
    return merged, hg, lru.reshape(bsz, LRU_WIDTH), conv


def _mixer_step_kernel(z_ref, sh_ref, sl_ref, sc_ref, lbp_ref, hgg_ref, cw_ref, cb_ref, wra_ref,
                       bra_ref, wix_ref, bix_ref, c8_ref, m_ref, hn_ref, ln_ref, cn_ref, o_scr,
                       *, bb):
    z = z_ref[...]
    xr = z[:, 4 * HG_WIDTH:4 * HG_WIDTH + LRU_WIDTH]
    gr = z[:, 4 * HG_WIDTH + LRU_WIDTH:IN_WIDTH]
    buf = sc_ref[...]
    xc = cb_ref[...]
    for j in range(CONV_K - 1):
        xc = xc + buf[:, j * LRU_WIDTH:(j + 1) * LRU_WIDTH] * cw_ref[j:j + 1, :]
    xc = xc + xr * cw_ref[CONV_K - 1:CONV_K, :]
    cn_ref[:, 0:(CONV_K - 2) * LRU_WIDTH] = buf[:, LRU_WIDTH:(CONV_K - 1) * LRU_WIDTH]
    cn_ref[:, (CONV_K - 2) * LRU_WIDTH:(CONV_K - 1) * LRU_WIDTH] = xr
    a, u = _lru_gates(xc, wra_ref[...], bra_ref[...], wix_ref[...], bix_ref[...], c8_ref[...])
    hnew = u + a * sl_ref[...]
    ln_ref[...] = hnew
    m_ref[:, HG_WIDTH:D_MODEL] = (hnew * _gelu_tanh(gr)).astype(BF16)

    q = _silu(z[:, 0:HG_WIDTH])
    log_f, k = _hgrn_gates(z[:, HG_WIDTH:2 * HG_WIDTH], lbp_ref[0:1, :], lbp_ref[1:2, :],
                           lbp_ref[2:3, :])
    f = jnp.exp(log_f)
    v = z[:, 2 * HG_WIDTH:3 * HG_WIDTH]
    gate = _silu(z[:, 3 * HG_WIDTH:4 * HG_WIDTH])
    zpad = jnp.zeros((HG_D - bb, HG_D), F32)
    for hh in range(HG_HEADS):
        cs = slice(hh * HG_D, (hh + 1) * HG_D)
        ft = jnp.concatenate([f[:, cs], zpad], axis=0).T
        kt = jnp.concatenate([k[:, cs], zpad], axis=0).T
        qt = jnp.concatenate([q[:, cs], zpad], axis=0).T
        for j in range(bb):
            fcol = jnp.broadcast_to(ft[:, j:j + 1], (HG_D, HG_D))
            kcol = jnp.broadcast_to(kt[:, j:j + 1], (HG_D, HG_D))
            qcol = jnp.broadcast_to(qt[:, j:j + 1], (HG_D, HG_D))
            vrow = jnp.broadcast_to(v[j:j + 1, cs], (HG_D, HG_D))
            sn = fcol * sh_ref[j, hh] + kcol * vrow
            hn_ref[j, hh] = sn
            o_scr[j:j + 1, cs] = jnp.sum(qcol * sn, axis=0, keepdims=True)
    for hh in range(HG_HEADS):
        cs = slice(hh * HG_D, (hh + 1) * HG_D)
        o = o_scr[:, cs]
        o = o * lax.rsqrt(jnp.mean(o * o, axis=-1, keepdims=True) + EPS)
        o = o * hgg_ref[:, cs]
        o = o * gate[:, cs]
        m_ref[:, cs] = o.astype(BF16)


def _mixer_step(z, s_hg, s_lru, s_conv, mp, bb=16):
    bsz = z.shape[0]
    bb = min(bb, bsz)
    cw3 = (CONV_K - 1) * LRU_WIDTH
    full = lambda shape: pl.BlockSpec(shape, lambda i: (0,) * len(shape))
    out_shapes = (
        jax.ShapeDtypeStruct((bsz, D_MODEL), BF16),
        jax.ShapeDtypeStruct((bsz, HG_HEADS, HG_D, HG_D), F32),
        jax.ShapeDtypeStruct((bsz, LRU_WIDTH), F32),
        jax.ShapeDtypeStruct((bsz, cw3), F32),
    )
    merged, hg, lru, conv = pl.pallas_call(
        functools.partial(_mixer_step_kernel, bb=bb),
        out_shape=out_shapes,
        grid=(bsz // bb,),
        in_specs=[
            pl.BlockSpec((bb, IN_WIDTH), lambda i: (i, 0)),
            pl.BlockSpec((bb, HG_HEADS, HG_D, HG_D), lambda i: (i, 0, 0, 0)),
            pl.BlockSpec((bb, LRU_WIDTH), lambda i: (i, 0)),
            pl.BlockSpec((bb, cw3), lambda i: (i, 0)),
            full((3, HG_WIDTH)), full((1, HG_WIDTH)), full((CONV_K, LRU_WIDTH)),
            full((1, LRU_WIDTH)), full((LRU_WIDTH, LRU_WIDTH)), full((1, LRU_WIDTH)),
            full((LRU_WIDTH, LRU_WIDTH)), full((1, LRU_WIDTH)), full((1, LRU_WIDTH)),
        ],
        out_specs=(
            pl.BlockSpec((bb, D_MODEL), lambda i: (i, 0)),
            pl.BlockSpec((bb, HG_HEADS, HG_D, HG_D), lambda i: (i, 0, 0, 0)),
            pl.BlockSpec((bb, LRU_WIDTH), lambda i: (i, 0)),
            pl.BlockSpec((bb, cw3), lambda i: (i, 0)),
        ),
        scratch_shapes=[pltpu.VMEM((bb, HG_WIDTH), F32)],
        compiler_params=_cparams(("parallel",)),
        name="mixer_step",
    )(z, s_hg, s_lru, s_conv.reshape(bsz, cw3), mp["lbp"], mp["hgg"], mp["cw"], mp["cb"],
      mp["wra"], mp["bra"], mp["wix"], mp["bix"], mp["c8"])
    return merged, hg, lru, conv.reshape(bsz, CONV_K - 1, LRU_WIDTH)


def _router_kernel(x_ref, g_ref, rw_ref, tril_ref, h_ref, info_ref, cnt_ref, carry_scr):
    i = pl.program_id(0)

    @pl.when(i == 0)
    def _():
        carry_scr[...] = jnp.zeros_like(carry_scr)

    h = _rms(x_ref[...], g_ref[...])
    h_ref[...] = h
    logits = jnp.dot(h, rw_ref[...], preferred_element_type=F32, precision=lax.Precision.HIGHEST)
    lane = lax.broadcasted_iota(I32, logits.shape, 1)
    neg = jnp.float32(-jnp.inf)
    logits = jnp.where(lane < N_EXPERTS, logits, neg)
    m1 = jnp.max(logits, axis=-1, keepdims=True)
    i1 = jnp.min(jnp.where(logits == m1, lane, LANES), axis=-1, keepdims=True)
    l2 = jnp.where(lane == i1, neg, logits)
    m2 = jnp.max(l2, axis=-1, keepdims=True)
    i2 = jnp.min(jnp.where(l2 == m2, lane, LANES), axis=-1, keepdims=True)
    e = jnp.exp(m2 - m1)
    p1 = 1.0 / (1.0 + e)
    p2 = e / (1.0 + e)
    oh1 = (lane == i1).astype(F32)
    oh2 = (lane == i2).astype(F32)
    sel = oh1 + oh2
    before = _dot(tril_ref[...], sel.astype(BF16)) + carry_scr[...]
    r1 = jnp.sum(oh1 * before, axis=-1, keepdims=True)
    r2 = jnp.sum(oh2 * before, axis=-1, keepdims=True)
    carry_scr[...] += jnp.sum(sel, axis=0, keepdims=True)
    info = jnp.where(lane == 0, i1.astype(F32), 0.0)
    info = jnp.where(lane == 1, i2.astype(F32), info)
    info = jnp.where(lane == 2, r1, info)
    info = jnp.where(lane == 3, r2, info)
    info = jnp.where(lane == 4, p1, info)
    info = jnp.where(lane == 5, p2, info)
    info_ref[...] = info
    cnt_ref[...] = carry_scr[...]


def _router(x, gain, rw_pad, tm=512):
    t, d = x.shape
    tm = min(tm, t)
    tril = jnp.tril(jnp.ones((tm, tm), BF16), -1)
    return pl.pallas_call(
        _router_kernel,
        out_shape=(jax.ShapeDtypeStruct((t, d), F32),
                   jax.ShapeDtypeStruct((t, LANES), F32),
                   jax.ShapeDtypeStruct((1, LANES), F32)),
        grid=(t // tm,),
        in_specs=[
            pl.BlockSpec((tm, d), lambda i: (i, 0)),
            pl.BlockSpec((1, d), lambda i: (0, 0)),
            pl.BlockSpec((d, LANES), lambda i: (0, 0)),
            pl.BlockSpec((tm, tm), lambda i: (0, 0)),
        ],
        out_specs=(pl.BlockSpec((tm, d), lambda i: (i, 0)),
                   pl.BlockSpec((tm, LANES), lambda i: (i, 0)),
                   pl.BlockSpec((1, LANES), lambda i: (0, 0))),
        scratch_shapes=[pltpu.VMEM((1, LANES), F32)],
        compiler_params=_cparams(("arbitrary",)),
        name="router",
    )(x, gain.reshape(1, d), rw_pad, tril)


def _row_copy(src, dst, s, d, sem):
    return pltpu.make_async_copy(src.at[pl.ds(s, 1)], dst.at[pl.ds(d, 1)], sem)


def _dispatch_kernel(dest_ref, h_ref, xs_in_ref, xs_ref, sem, *, td):
    del xs_in_ref
    base = pl.program_id(0) * td

    def issue(r, c):
        _row_copy(h_ref, xs_ref, base + r, dest_ref[0, 0, r], sem).start()
        _row_copy(h_ref, xs_ref, base + r, dest_ref[0, 0, td + r], sem).start()
        return c

    lax.fori_loop(0, td, issue, 0)

    def drain(r, c):
        _row_copy(h_ref, xs_ref, 0, 0, sem).wait()
        return c

    lax.fori_loop(0, 2 * td, drain, 0)


def _dispatch(h, dest, n_rows, td=1024):
    t, d = h.shape
    td = min(td, t)
    nb = t // td
    dest3 = jnp.concatenate([dest[0].reshape(nb, td), dest[1].reshape(nb, td)], axis=1)
    dest3 = dest3.reshape(nb, 1, 2 * td)
    xs0 = jnp.zeros((n_rows, d), F32)
    return pl.pallas_call(
        functools.partial(_dispatch_kernel, td=td),
        out_shape=jax.ShapeDtypeStruct((n_rows, d), F32),
        grid=(nb,),
        in_specs=[
            pl.BlockSpec((1, 1, 2 * td), lambda i: (i, 0, 0), memory_space=pltpu.SMEM),
            pl.BlockSpec(memory_space=pl.ANY),
            pl.BlockSpec(memory_space=pl.ANY),
        ],
        out_specs=pl.BlockSpec(memory_space=pl.ANY),
        scratch_shapes=[pltpu.SemaphoreType.DMA(())],
        input_output_aliases={2: 0},
        compiler_params=pltpu.CompilerParams(dimension_semantics=("arbitrary",),
                                             has_side_effects=True),
        name="dispatch",
    )(dest3, h, xs0)


def _collect_kernel(dest_ref, ys_ref, y1_ref, y2_ref, sem, *, td):
    base = pl.program_id(0) * td

    def issue(r, c):
        _row_copy(ys_ref, y1_ref, dest_ref[0, 0, r], base + r, sem).start()
        _row_copy(ys_ref, y2_ref, dest_ref[0, 0, td + r], base + r, sem).start()
        return c

    lax.fori_loop(0, td, issue, 0)

    def drain(r, c):
        _row_copy(ys_ref, y1_ref, 0, 0, sem).wait()
        return c

    lax.fori_loop(0, 2 * td, drain, 0)


def _collect(ys, dest, t, td=1024):
    d = ys.shape[1]
    td = min(td, t)
    nb = t // td
    dest3 = jnp.concatenate([dest[0].reshape(nb, td), dest[1].reshape(nb, td)], axis=1)
    dest3 = dest3.reshape(nb, 1, 2 * td)
    return pl.pallas_call(
        functools.partial(_collect_kernel, td=td),
        out_shape=(jax.ShapeDtypeStruct((t, d), F32), jax.ShapeDtypeStruct((t, d), F32)),
        grid=(nb,),
        in_specs=[
            pl.BlockSpec((1, 1, 2 * td), lambda i: (i, 0, 0), memory_space=pltpu.SMEM),
            pl.BlockSpec(memory_space=pl.ANY),
        ],
        out_specs=(pl.BlockSpec(memory_space=pl.ANY), pl.BlockSpec(memory_space=pl.ANY)),
        scratch_shapes=[pltpu.SemaphoreType.DMA(())],
        compiler_params=pltpu.CompilerParams(dimension_semantics=("arbitrary",),
                                             has_side_effects=True),
        name="collect",
    )(dest3, ys)


def _combine_kernel(x_ref, y1_ref, y2_ref, info_ref, g_ref, o_ref):
    p1 = info_ref[:, 4:5]
    p2 = info_ref[:, 5:6]
    x = x_ref[...] + (p1 * y1_ref[...] + p2 * y2_ref[...])
    o_ref[...] = _rms(x, g_ref[...])


def _combine(x, y1, y2, info, gain, tm=512):
    t, d = x.shape
    tm = min(tm, t)
    row = pl.BlockSpec((tm, d), lambda i: (i, 0))
    return pl.pallas_call(
        _combine_kernel,
        out_shape=jax.ShapeDtypeStruct((t, d), F32),
        grid=(t // tm,),
        in_specs=[row, row, row, pl.BlockSpec((tm, LANES), lambda i: (i, 0)),
                  pl.BlockSpec((1, d), lambda i: (0, 0))],
        out_specs=row,
        compiler_params=_cparams(("parallel",)),
        name="combine",
    )(x, y1, y2, info, gain.reshape(1, d))


def _moe(x, norm_g, rw_pad, w1, w3, w2, final_g, tm_e):
    t, d = x.shape
    h, info, cnt = _router(x, norm_g, rw_pad)
    e1 = info[:, 0].astype(I32)
    e2 = info[:, 1].astype(I32)
    r1 = info[:, 2].astype(I32)
    r2 = info[:, 3].astype(I32)
    counts = cnt[0, :N_EXPERTS].astype(I32)
    padded = ((counts + tm_e - 1) // tm_e) * tm_e
    ends = jnp.cumsum(padded)
    starts = ends - padded
    dest = jnp.stack([starts[e1] + r1, starts[e2] + r2])
    n_tiles = (2 * t + N_EXPERTS * (tm_e - 1)) // tm_e
    n_rows = n_tiles * tm_e
    tile_start = jnp.arange(n_tiles, dtype=I32) * tm_e
    tile_expert = jnp.minimum(
        jnp.sum((tile_start[:, None] >= ends[None, :]).astype(I32), axis=1), N_EXPERTS - 1)
    n_valid = (ends[-1] // tm_e).reshape(1).astype(I32)
    xs = _dispatch(h, dest, n_rows)
    ys = _ffn(xs, w1, w3, w2, tile_expert.astype(I32), n_valid, tm=tm_e, n_f=2)
    y1, y2 = _collect(ys, dest, t)
    return _combine(x, y1, y2, info, final_g)


def _block_diag(w):
    n, c, d = w.shape
    eye = jnp.eye(n, dtype=w.dtype)
    return (w[:, :, None, :] * eye[:, None, :, None]).reshape(n * c, n * d)


def _cum_matrix():
    t = jnp.arange(CHUNK)
    tril = (t[None, :] <= t[:, None]).astype(F32)
    mats = [tril]
    w = CHUNK // 2
    while w >= DIAG:
        ref_row = (t // (2 * w)) * (2 * w) + w
        mats.append(tril[ref_row])
        w //= 2
    while len(mats) < 4:
        mats.append(jnp.zeros_like(tril))
    return jnp.concatenate(mats, axis=0).astype(BF16)


def _mixer_params(l, lb, hg_norm_g, conv_w, conv_b, w_ra, b_ra, w_ix, b_ix, lru_lambda):
    row = lambda a: a.reshape(1, -1).astype(F32)
    return {
        "lbp": jnp.stack([jnp.log(lb), jnp.log1p(-lb), 1.0 - lb]).astype(F32),
        "hgg": row(hg_norm_g[l]),
        "cw": conv_w[l].astype(F32),
        "cb": row(conv_b[l]),
        "wra": _block_diag(w_ra[l].astype(F32)).astype(BF16),
        "bra": row(b_ra[l]),
        "wix": _block_diag(w_ix[l].astype(F32)).astype(BF16),
        "bix": row(b_ix[l]),
        "c8": row(-LRU_C * jax.nn.softplus(-lru_lambda[l].astype(F32))),
        "cm": _cum_matrix(),
    }


def _trunk(x, seq_shape, states, prm):
    bsz, seq = seq_shape
    one = jnp.ones((1,), I32)
    new_hg, new_lru, new_conv = [], [], []
    for l in range(2):
        mp = prm["mixer"][l]
        z = _linear(x, prm["w_in"][l], gain=prm["norm1_g"][l])
        if states is None:
            merged, s1, s2, s3 = _mixer_prompt(z.reshape(bsz, seq, IN_WIDTH), mp)
            merged = merged.reshape(bsz * seq, D_MODEL)
        else:
            merged, s1, s2, s3 = _mixer_step(z, states[0][l], states[1][l], states[2][l], mp)
        new_hg.append(s1)
        new_lru.append(s2)
        new_conv.append(s3)
        x = _linear(merged, prm["w_out"][l], res=x)
        if l == 0:
            n_tiles = max(x.shape[0] // 512, 1)
            x = _ffn(x, prm["ffn_w1"], prm["ffn_w3"], prm["ffn_w2"], jnp.zeros((n_tiles,), I32),
                     one * n_tiles, gain=prm["norm2_g"][0], res=True, tm=512, n_f=1)
        else:
            x = _moe(x, prm["norm2_g"][1], prm["rw_pad"], prm["moe_w1"], prm["moe_w3"],
                     prm["moe_w2"], prm["final_norm_g"], tm_e=512 if states is None else 128)
    return x, jnp.stack(new_hg), jnp.stack(new_lru), jnp.stack(new_conv)


def kernel(x_prompt, x_sample, state_hgrn, state_lru, state_conv, norm1_g, w_in, lower_bounds,
           hg_norm_g, conv_w, conv_b, w_ra, b_ra, w_ix, b_ix, lru_lambda, w_out, norm2_g,
           ffn_w1, ffn_w3, ffn_w2, router_w, moe_w1, moe_w3, moe_w2, final_norm_g):
    lb_all = jnp.cumsum(jax.nn.softmax(lower_bounds.astype(F32), axis=0), axis=0)
    lb_all = lb_all - lb_all[0]
    prm = {
        "mixer": [_mixer_params(l, lb_all[l], hg_norm_g, conv_w, conv_b, w_ra, b_ra, w_ix, b_ix,
                                lru_lambda) for l in range(2)],
        "norm1_g": norm1_g, "norm2_g": norm2_g, "final_norm_g": final_norm_g,
        "w_in": w_in.astype(BF16), "w_out": w_out.astype(BF16),
        "ffn_w1": ffn_w1.astype(BF16), "ffn_w3": ffn_w3.astype(BF16),
        "ffn_w2": ffn_w2.astype(BF16),
        "moe_w1": moe_w1[0].astype(BF16), "moe_w3": moe_w3[0].astype(BF16),
        "moe_w2": moe_w2[0].astype(BF16),
        "rw_pad": jnp.pad(router_w[0].astype(F32), ((0, 0), (0, LANES - N_EXPERTS))),
    }
    bp, sp, d = x_prompt.shape
    bs = x_sample.shape[0]
    y_p, hg_p, lru_p, conv_p = _trunk(x_prompt.reshape(bp * sp, d), (bp, sp), None, prm)
    y_s, hg_s, lru_s, conv_s = _trunk(x_sample.reshape(bs, d), (bs, 1),
                                      (state_hgrn, state_lru, state_conv), prm)
    return (y_p.reshape(bp, sp, d), y_s.reshape(bs, 1, d), hg_p, lru_p, conv_p, hg_s, lru_s,
            conv_s)
```

```python
import functools

import jax
import jax.numpy as jnp
from jax import lax
from jax.experimental import pallas as pl
from jax.experimental.pallas import tpu as pltpu

F32 = jnp.float32
BF16 = jnp.bfloat16
I32 = jnp.int32

D_MODEL = 1024
HG_WIDTH = 512
HG_HEADS = 4
HG_D = 128
LRU_WIDTH = 512
LRU_BLOCKS = 8
LRU_C = 8.0
CONV_K = 4
IN_WIDTH = 4 * HG_WIDTH + 2 * LRU_WIDTH
N_EXPERTS = 8
EPS = 1e-6

CHUNK = 64
DIAG = 8
LANES = 128
VMEM_LIMIT_BYTES = 56 * 1024 * 1024


def _cparams(sem):
    return pltpu.CompilerParams(dimension_semantics=sem, vmem_limit_bytes=VMEM_LIMIT_BYTES)


def _sigmoid(x):
    return 1.0 / (1.0 + jnp.exp(-x))


def _silu(x):
    return x * _sigmoid(x)


def _gelu_tanh(x):
    c = 0.7978845608028654
    return 0.5 * x * (1.0 + jnp.tanh(c * (x + 0.044715 * (x * x * x))))


def _rms(x, g):
    return x * lax.rsqrt(jnp.mean(x * x, axis=-1, keepdims=True) + EPS) * g


def _dot(a, b):
    if b.dtype == F32:
        return jnp.dot(a.astype(F32), b, preferred_element_type=F32,
                       precision=lax.Precision.HIGHEST)
    return jnp.dot(a.astype(BF16), b, preferred_element_type=F32)


def _dot_nt(a, b):
    return lax.dot_general(a, b, (((1,), (1,)), ((), ())), preferred_element_type=F32)


def _dot_tn(a, b):
    return lax.dot_general(a, b, (((0,), (0,)), ((), ())), preferred_element_type=F32)


def _split3(x):
    x1 = x.astype(BF16)
    r1 = x - x1.astype(F32)
    x2 = r1.astype(BF16)
    x3 = (r1 - x2.astype(F32)).astype(BF16)
    return x1, x2, x3


def _linear_kernel(*refs, has_gain, has_res):
    it = iter(refs)
    x_ref = next(it)
    g_ref = next(it) if has_gain else None
    w_ref = next(it)
    r_ref = next(it) if has_res else None
    o_ref = next(it)
    h_scr = next(it)

    @pl.when(pl.program_id(1) == 0)
    def _():
        x = x_ref[...].astype(F32)
        if has_gain:
            x = _rms(x, g_ref[...])
        h_scr[...] = x.astype(h_scr.dtype)

    acc = _dot(h_scr[...], w_ref[...])
    if has_res:
        acc = acc + r_ref[...]
    o_ref[...] = acc.astype(o_ref.dtype)


def _linear(x, w, gain=None, res=None, tm=1024, tn=1024):
    m, k = x.shape
    n = w.shape[1]
    tm = min(tm, m)
    tn = min(tn, n)
    in_specs = [pl.BlockSpec((tm, k), lambda i, j: (i, 0))]
    args = [x]
    if gain is not None:
        in_specs.append(pl.BlockSpec((1, k), lambda i, j: (0, 0)))
        args.append(gain.reshape(1, k))
    in_specs.append(pl.BlockSpec((k, tn), lambda i, j: (0, j)))
    args.append(w)
    if res is not None:
        in_specs.append(pl.BlockSpec((tm, tn), lambda i, j: (i, j)))
        args.append(res)
    return pl.pallas_call(
        functools.partial(_linear_kernel, has_gain=gain is not None, has_res=res is not None),
        out_shape=jax.ShapeDtypeStruct((m, n), F32),
        grid=(m // tm, n // tn),
        in_specs=in_specs,
        out_specs=pl.BlockSpec((tm, tn), lambda i, j: (i, j)),
        scratch_shapes=[pltpu.VMEM((tm, k), w.dtype)],
        compiler_params=_cparams(("parallel", "arbitrary")),
        name="linear",
    )(*args)


def _unpack_rows(xu):
    lo = pltpu.bitcast(xu << 16, F32)
    hi = pltpu.bitcast(xu & jnp.uint32(0xFFFF0000), F32)
    return lo, hi


def _pack_rows(h):
    hb = h.astype(BF16).astype(F32)
    half = D_MODEL // 2
    lo = pltpu.bitcast(hb[:, :half], jnp.uint32) >> 16
    hi = pltpu.bitcast(hb[:, half:], jnp.uint32) & jnp.uint32(0xFFFF0000)
    return lo | hi


def _ffn_kernel(te_ref, nv_ref, *refs, has_gain, has_res, packed, tf, ck):
    del te_ref
    it = iter(refs)
    x_ref = next(it)
    g_ref = next(it) if has_gain else None
    w1_ref, w3_ref, w2_ref, o_ref, xb_scr, a_scr, acc_scr = it
    i = pl.program_id(0)
    j = pl.program_id(1)
    nj = pl.num_programs(1)
    valid = i < nv_ref[0]

    @pl.when(valid)
    def _():
        @pl.when(j == 0)
        def _():
            if packed:
                lo, hi = _unpack_rows(x_ref[...])
                xb_scr[:, :D_MODEL // 2] = lo.astype(xb_scr.dtype)
                xb_scr[:, D_MODEL // 2:] = hi.astype(xb_scr.dtype)
            else:
                x = x_ref[...]
                if has_gain:
                    x = _rms(x, g_ref[...])
                xb_scr[...] = x.astype(xb_scr.dtype)

        xb = xb_scr[...]
        for c in range(tf // ck):
            h1 = _dot(xb, w1_ref[0, :, c * ck:(c + 1) * ck])
            h3 = _dot(xb, w3_ref[0, :, c * ck:(c + 1) * ck])
            a_scr[:, c * ck:(c + 1) * ck] = (_silu(h1) * h3).astype(a_scr.dtype)
        part = _dot(a_scr[...], w2_ref[0])

        @pl.when(j == 0)
        def _():
            acc_scr[...] = part

        @pl.when(j > 0)
        def _():
            acc_scr[...] += part

        @pl.when(j == nj - 1)
        def _():
            out = acc_scr[...]
            if has_res:
                out = x_ref[...] + out
            o_ref[...] = out

    @pl.when(jnp.logical_and(jnp.logical_not(valid), j == nj - 1))
    def _():
        o_ref[...] = jnp.zeros_like(o_ref)


def _ffn(x, w1, w3, w2, tile_expert, n_valid, gain=None, res=False, packed=False, tm=512,
         n_f=2, ck=256):
    p = x.shape[0]
    d = D_MODEL
    f = w1.shape[2]
    tm = min(tm, p)
    tf = f // n_f
    n_tiles = p // tm
    in_specs = [pl.BlockSpec((tm, x.shape[1]), lambda i, j, te, nv: (i, 0))]
    args = [x]
    if gain is not None:
        in_specs.append(pl.BlockSpec((1, d), lambda i, j, te, nv: (0, 0)))
        args.append(gain.reshape(1, d))
    in_specs += [
        pl.BlockSpec((1, d, tf), lambda i, j, te, nv: (te[i], 0, j)),
        pl.BlockSpec((1, d, tf), lambda i, j, te, nv: (te[i], 0, j)),
        pl.BlockSpec((1, tf, d), lambda i, j, te, nv: (te[i], j, 0)),
    ]
    args += [w1, w3, w2]
    grid_spec = pltpu.PrefetchScalarGridSpec(
        num_scalar_prefetch=2,
        grid=(n_tiles, n_f),
        in_specs=in_specs,
        out_specs=pl.BlockSpec((tm, d), lambda i, j, te, nv: (i, 0)),
        scratch_shapes=[pltpu.VMEM((tm, d), w1.dtype), pltpu.VMEM((tm, tf), w1.dtype),
                        pltpu.VMEM((tm, d), F32)],
    )
    return pl.pallas_call(
        functools.partial(_ffn_kernel, has_gain=gain is not None, has_res=res, packed=packed,
                          tf=tf, ck=ck),
        out_shape=jax.ShapeDtypeStruct((p, d), F32),
        grid_spec=grid_spec,
        compiler_params=_cparams(("arbitrary", "arbitrary")),
        name="swiglu",
    )(tile_expert, n_valid, *args)


def _hgrn_gates(fz, loglb, l1mlb, oneml):
    e = jnp.exp(-jnp.abs(fz))
    log_sig = jnp.minimum(fz, 0.0) - jnp.log1p(e)
    cc = l1mlb + log_sig
    log_f = jnp.maximum(loglb, cc) + jnp.log1p(jnp.exp(-jnp.abs(loglb - cc)))
    k = oneml * _sigmoid(-fz)
    return log_f, k


def _lru_gates(xc, wra, bra, wix, bix, c8):
    r = _sigmoid(_dot(xc, wra) + bra)
    ig = _sigmoid(_dot(xc, wix) + bix)
    log_a = c8 * r
    a = jnp.exp(log_a)
    one_m_a2 = -jnp.tanh(log_a) * (a * a + 1.0)
    u = jnp.sqrt(one_m_a2) * (ig * xc)
    return a, u


def _mixer_prompt_kernel(z_ref, lbp_ref, hgg_ref, cw_ref, cb_ref, wra_ref, bra_ref, wix_ref,
                         bix_ref, c8_ref, cm_ref, m_ref, hs_ref, ls_ref, cs_ref,
                         st_scr, h_scr, xpad_scr, xc_scr, *, tl):
    l = pl.program_id(1)
    nl = pl.num_programs(1)

    @pl.when(l == 0)
    def _():
        st_scr[...] = jnp.zeros_like(st_scr)
        h_scr[...] = jnp.zeros_like(h_scr)
        xpad_scr[0:8, :] = jnp.zeros((8, LRU_WIDTH), F32)

    xpad_scr[8:8 + tl, :] = z_ref[0, :, 4 * HG_WIDTH:4 * HG_WIDTH + LRU_WIDTH]
    xc = cb_ref[...]
    for j in range(CONV_K):
        xc = xc + xpad_scr[5 + j:5 + j + tl, :] * cw_ref[j:j + 1, :]
    xc_scr[...] = xc
    tail = xpad_scr[tl + 5:tl + 8, :]
    xpad_scr[5:8, :] = tail

    loglb = lbp_ref[0:1, :]
    l1mlb = lbp_ref[1:2, :]
    oneml = lbp_ref[2:3, :]
    row = lax.broadcasted_iota(I32, (CHUNK, 1), 0)
    ti = lax.broadcasted_iota(I32, (CHUNK, CHUNK), 0)
    si = lax.broadcasted_iota(I32, (CHUNK, CHUNK), 1)
    t3 = lax.broadcasted_iota(I32, (1, DIAG, 1), 1)
    nblk = CHUNK // DIAG

    def chunk_body(c, carry):
        r0 = pl.multiple_of(c * CHUNK, CHUNK)
        rows = pl.ds(r0, CHUNK)

        xc_c = xc_scr[rows, :]
        a, u = _lru_gates(xc_c, wra_ref[...], bra_ref[...], wix_ref[...], bix_ref[...], c8_ref[...])
        gr = z_ref[0, rows, 4 * HG_WIDTH + LRU_WIDTH:IN_WIDTH]
        for gi in range(LRU_WIDTH // LANES):
            cs = slice(gi * LANES, (gi + 1) * LANES)
            aa = a[:, cs]
            uu = u[:, cs]
            d = 1
            while d < CHUNK:
                keep = row >= d
                a_sh = pltpu.roll(aa, d, 0)
                u_sh = pltpu.roll(uu, d, 0)
                uu = jnp.where(keep, aa * u_sh + uu, uu)
                aa = jnp.where(keep, aa * a_sh, aa)
                d *= 2
            hs = uu + aa * h_scr[:, cs]
            h_scr[:, cs] = hs[CHUNK - 1:CHUNK, :]
            m_ref[0, rows, HG_WIDTH + gi * LANES:HG_WIDTH + (gi + 1) * LANES] = (
                hs * _gelu_tanh(gr[:, cs])).astype(BF16)

        q = _silu(z_ref[0, rows, 0:HG_WIDTH])
        log_f, k = _hgrn_gates(z_ref[0, rows, HG_WIDTH:2 * HG_WIDTH], loglb, l1mlb, oneml)
        v = z_ref[0, rows, 2 * HG_WIDTH:3 * HG_WIDTH]
        gate = _silu(z_ref[0, rows, 3 * HG_WIDTH:4 * HG_WIDTH])
        f1, f2, f3 = _split3(log_f)
        cm = cm_ref[...]
        b4 = _dot(cm, f1) + _dot(cm, f2) + _dot(cm, f3)
        b = b4[0:CHUNK]

        for hh in range(HG_HEADS):
            cs = slice(hh * HG_D, (hh + 1) * HG_D)
            qh, kh, vh, bh = q[:, cs], k[:, cs], v[:, cs], b[:, cs]
            vb = vh.astype(BF16)
            amat = jnp.zeros((CHUNK, CHUNK), F32)
            lvl = 1
            w = CHUNK // 2
            while w >= DIAG:
                rw = b4[lvl * CHUNK:(lvl + 1) * CHUNK, cs]
                upper = (row & (2 * w - 1)) >= w
                qw = jnp.where(upper, qh * jnp.exp(jnp.minimum(bh - rw, 0.0)), 0.0).astype(BF16)
                kw = jnp.where(upper, 0.0, kh * jnp.exp(jnp.minimum(rw - bh, 0.0))).astype(BF16)
                aw = _dot_nt(qw, kw)
                if 2 * w < CHUNK:
                    sh = (2 * w).bit_length() - 1
                    aw = jnp.where((ti >> sh) == (si >> sh), aw, 0.0)
                amat = amat + aw
                lvl += 1
                w //= 2
            o = _dot(amat.astype(BF16), vb)
            st = st_scr[hh]
            o = o + _dot_nt((qh * jnp.exp(bh)).astype(BF16), st.astype(BF16))
            q3 = qh.reshape(nblk, DIAG, HG_D)
            k3 = kh.reshape(nblk, DIAG, HG_D)
            v3 = vh.reshape(nblk, DIAG, HG_D)
            b3 = bh.reshape(nblk, DIAG, HG_D)
            od = jnp.zeros((nblk, DIAG, HG_D), F32)
            for s in range(DIAG):
                dec = jnp.exp(jnp.minimum(b3 - b3[:, s:s + 1, :], 0.0))
                p = q3 * (k3[:, s:s + 1, :] * dec)
                rs = jnp.sum(p, axis=-1, keepdims=True)
                rs = jnp.where(t3 >= s, rs, 0.0)
                od = od + rs * v3[:, s:s + 1, :]
            o = o + od.reshape(CHUNK, HG_D)
            bl = bh[CHUNK - 1:CHUNK, :]
            kdec = (kh * jnp.exp(bl - bh)).astype(BF16)
            st_scr[hh] = st * jnp.exp(bl) + _dot_tn(vb, kdec)
            o = o * lax.rsqrt(jnp.mean(o * o, axis=-1, keepdims=True) + EPS)
            o = o * hgg_ref[:, cs]
            o = o * gate[:, cs]
            m_ref[0, rows, cs] = o.astype(BF16)
        return carry

    lax.fori_loop(0, tl // CHUNK, chunk_body, 0)

    @pl.when(l == nl - 1)
    def _():
        for hh in range(HG_HEADS):
            hs_ref[0, hh] = st_scr[hh].T
        ls_ref[0] = h_scr[...]
        cs_ref[0] = xpad_scr[5:8, :]


def _mixer_prompt(z, mp, tl=512):
    bsz, seq, _ = z.shape
    tl = min(tl, seq)
    full = lambda shape: pl.BlockSpec(shape, lambda b, l: (0,) * len(shape))
    out_shapes = (
        jax.ShapeDtypeStruct((bsz, seq, D_MODEL), BF16),
        jax.ShapeDtypeStruct((bsz, HG_HEADS, HG_D, HG_D), F32),
        jax.ShapeDtypeStruct((bsz, 1, LRU_WIDTH), F32),
        jax.ShapeDtypeStruct((bsz, CONV_K - 1, LRU_WIDTH), F32),
    )
    merged, hg, lru, conv = pl.pallas_call(
        functools.partial(_mixer_prompt_kernel, tl=tl),
        out_shape=out_shapes,
        grid=(bsz, seq // tl),
        in_specs=[
            pl.BlockSpec((1, tl, IN_WIDTH), lambda b, l: (b, l, 0)),
            full((3, HG_WIDTH)), full((1, HG_WIDTH)), full((CONV_K, LRU_WIDTH)),
            full((1, LRU_WIDTH)), full((LRU_WIDTH, LRU_WIDTH)), full((1, LRU_WIDTH)),
            full((LRU_WIDTH, LRU_WIDTH)), full((1, LRU_WIDTH)), full((1, LRU_WIDTH)),
            full((4 * CHUNK, CHUNK)),
        ],
        out_specs=(
            pl.BlockSpec((1, tl, D_MODEL), lambda b, l: (b, l, 0)),
            pl.BlockSpec((1, HG_HEADS, HG_D, HG_D), lambda b, l: (b, 0, 0, 0)),
            pl.BlockSpec((1, 1, LRU_WIDTH), lambda b, l: (b, 0, 0)),
            pl.BlockSpec((1, CONV_K - 1, LRU_WIDTH), lambda b, l: (b, 0, 0)),
        ),
        scratch_shapes=[
            pltpu.VMEM((HG_HEADS, HG_D, HG_D), F32),
            pltpu.VMEM((1, LRU_WIDTH), F32),
            pltpu.VMEM((tl + 8, LRU_WIDTH), F32),
            pltpu.VMEM((tl, LRU_WIDTH), F32),
        ],
        compiler_params=_cparams(("parallel", "arbitrary")),
        name="mixer_prompt",
    )(z, mp["lbp"], mp["hgg"], mp["cw"], mp["cb"], mp["wra"], mp["bra"], mp["wix"], mp["bix"],
      mp["c8"], mp["cm"])
    return merged, hg, lru.reshape(bsz, LRU_WIDTH), conv


def _mixer_step_kernel(z_ref, sh_ref, sl_ref, sc_ref, lbp_ref, hgg_ref, cw_ref, cb_ref, wra_ref,
                       bra_ref, wix_ref, bix_ref, c8_ref, m_ref, hn_ref, ln_ref, cn_ref, o_scr,
                       *, bb):
    z = z_ref[...]
    xr = z[:, 4 * HG_WIDTH:4 * HG_WIDTH + LRU_WIDTH]
    gr = z[:, 4 * HG_WIDTH + LRU_WIDTH:IN_WIDTH]
    buf = sc_ref[...]
    xc = cb_ref[...]
    for j in range(CONV_K - 1):
        xc = xc + buf[:, j * LRU_WIDTH:(j + 1) * LRU_WIDTH] * cw_ref[j:j + 1, :]
    xc = xc + xr * cw_ref[CONV_K - 1:CONV_K, :]
    cn_ref[:, 0:(CONV_K - 2) * LRU_WIDTH] = buf[:, LRU_WIDTH:(CONV_K - 1) * LRU_WIDTH]
    cn_ref[:, (CONV_K - 2) * LRU_WIDTH:(CONV_K - 1) * LRU_WIDTH] = xr
    a, u = _lru_gates(xc, wra_ref[...], bra_ref[...], wix_ref[...], bix_ref[...], c8_ref[...])
    hnew = u + a * sl_ref[...]
    ln_ref[...] = hnew
    m_ref[:, HG_WIDTH:D_MODEL] = (hnew * _gelu_tanh(gr)).astype(m_ref.dtype)

    q = _silu(z[:, 0:HG_WIDTH])
    log_f, k = _hgrn_gates(z[:, HG_WIDTH:2 * HG_WIDTH], lbp_ref[0:1, :], lbp_ref[1:2, :],
                           lbp_ref[2:3, :])
    f = jnp.exp(log_f)
    v = z[:, 2 * HG_WIDTH:3 * HG_WIDTH]
    gate = _silu(z[:, 3 * HG_WIDTH:4 * HG_WIDTH])
    zpad = jnp.zeros((HG_D - bb, HG_D), F32)
    for hh in range(HG_HEADS):
        cs = slice(hh * HG_D, (hh + 1) * HG_D)
        ft = jnp.concatenate([f[:, cs], zpad], axis=0).T
        kt = jnp.concatenate([k[:, cs], zpad], axis=0).T
        qt = jnp.concatenate([q[:, cs], zpad], axis=0).T
        for j in range(bb):
            fcol = jnp.broadcast_to(ft[:, j:j + 1], (HG_D, HG_D))
            kcol = jnp.broadcast_to(kt[:, j:j + 1], (HG_D, HG_D))
            qcol = jnp.broadcast_to(qt[:, j:j + 1], (HG_D, HG_D))
            vrow = jnp.broadcast_to(v[j:j + 1, cs], (HG_D, HG_D))
            sn = fcol * sh_ref[j, hh] + kcol * vrow
            hn_ref[j, hh] = sn
            o_scr[j:j + 1, cs] = jnp.sum(qcol * sn, axis=0, keepdims=True)
    for hh in range(HG_HEADS):
        cs = slice(hh * HG_D, (hh + 1) * HG_D)
        o = o_scr[:, cs]
        o = o * lax.rsqrt(jnp.mean(o * o, axis=-1, keepdims=True) + EPS)
        o = o * hgg_ref[:, cs]
        o = o * gate[:, cs]
        m_ref[:, cs] = o.astype(m_ref.dtype)


def _mixer_step(z, s_hg, s_lru, s_conv, mp, bb=16):
    bsz = z.shape[0]
    bb = min(bb, bsz)
    cw3 = (CONV_K - 1) * LRU_WIDTH
    full = lambda shape: pl.BlockSpec(shape, lambda i: (0,) * len(shape))
    out_shapes = (
        jax.ShapeDtypeStruct((bsz, D_MODEL), mp["wra"].dtype),
        jax.ShapeDtypeStruct((bsz, HG_HEADS, HG_D, HG_D), F32),
        jax.ShapeDtypeStruct((bsz, LRU_WIDTH), F32),
        jax.ShapeDtypeStruct((bsz, cw3), F32),
    )
    merged, hg, lru, conv = pl.pallas_call(
        functools.partial(_mixer_step_kernel, bb=bb),
        out_shape=out_shapes,
        grid=(bsz // bb,),
        in_specs=[
            pl.BlockSpec((bb, IN_WIDTH), lambda i: (i, 0)),
            pl.BlockSpec((bb, HG_HEADS, HG_D, HG_D), lambda i: (i, 0, 0, 0)),
            pl.BlockSpec((bb, LRU_WIDTH), lambda i: (i, 0)),
            pl.BlockSpec((bb, cw3), lambda i: (i, 0)),
            full((3, HG_WIDTH)), full((1, HG_WIDTH)), full((CONV_K, LRU_WIDTH)),
            full((1, LRU_WIDTH)), full((LRU_WIDTH, LRU_WIDTH)), full((1, LRU_WIDTH)),
            full((LRU_WIDTH, LRU_WIDTH)), full((1, LRU_WIDTH)), full((1, LRU_WIDTH)),
        ],
        out_specs=(
            pl.BlockSpec((bb, D_MODEL), lambda i: (i, 0)),
            pl.BlockSpec((bb, HG_HEADS, HG_D, HG_D), lambda i: (i, 0, 0, 0)),
            pl.BlockSpec((bb, LRU_WIDTH), lambda i: (i, 0)),
            pl.BlockSpec((bb, cw3), lambda i: (i, 0)),
        ),
        scratch_shapes=[pltpu.VMEM((bb, HG_WIDTH), F32)],
        compiler_params=_cparams(("parallel",)),
        name="mixer_step",
    )(z, s_hg, s_lru, s_conv.reshape(bsz, cw3), mp["lbp"], mp["hgg"], mp["cw"], mp["cb"],
      mp["wra"], mp["bra"], mp["wix"], mp["bix"], mp["c8"])
    return merged, hg, lru, conv.reshape(bsz, CONV_K - 1, LRU_WIDTH)


def _router_kernel(x_ref, g_ref, rw_ref, tril_ref, info_ref, cnt_ref, carry_scr):
    i = pl.program_id(0)

    @pl.when(i == 0)
    def _():
        carry_scr[...] = jnp.zeros_like(carry_scr)

    h = _rms(x_ref[...], g_ref[...])
    logits = jnp.dot(h, rw_ref[...], preferred_element_type=F32, precision=lax.Precision.HIGHEST)
    lane = lax.broadcasted_iota(I32, logits.shape, 1)
    neg = jnp.float32(-jnp.inf)
    logits = jnp.where(lane < N_EXPERTS, logits, neg)
    m1 = jnp.max(logits, axis=-1, keepdims=True)
    i1 = jnp.min(jnp.where(logits == m1, lane, LANES), axis=-1, keepdims=True)
    l2 = jnp.where(lane == i1, neg, logits)
    m2 = jnp.max(l2, axis=-1, keepdims=True)
    i2 = jnp.min(jnp.where(l2 == m2, lane, LANES), axis=-1, keepdims=True)
    e = jnp.exp(m2 - m1)
    p1 = 1.0 / (1.0 + e)
    p2 = e / (1.0 + e)
    oh1 = (lane == i1).astype(F32)
    oh2 = (lane == i2).astype(F32)
    sel = oh1 + oh2
    before = _dot(tril_ref[...], sel.astype(BF16)) + carry_scr[...]
    r1 = jnp.sum(oh1 * before, axis=-1, keepdims=True)
    r2 = jnp.sum(oh2 * before, axis=-1, keepdims=True)
    carry_scr[...] += jnp.sum(sel, axis=0, keepdims=True)
    info = jnp.where(lane == 0, i1.astype(F32), 0.0)
    info = jnp.where(lane == 1, i2.astype(F32), info)
    info = jnp.where(lane == 2, r1, info)
    info = jnp.where(lane == 3, r2, info)
    info = jnp.where(lane == 4, p1, info)
    info = jnp.where(lane == 5, p2, info)
    info_ref[...] = info
    cnt_ref[...] = carry_scr[...]


def _router(x, gain, rw_pad, tm=512):
    t, d = x.shape
    tm = min(tm, t)
    tril = jnp.tril(jnp.ones((tm, tm), BF16), -1)
    return pl.pallas_call(
        _router_kernel,
        out_shape=(jax.ShapeDtypeStruct((t, LANES), F32),
                   jax.ShapeDtypeStruct((1, LANES), F32)),
        grid=(t // tm,),
        in_specs=[
            pl.BlockSpec((tm, d), lambda i: (i, 0)),
            pl.BlockSpec((1, d), lambda i: (0, 0)),
            pl.BlockSpec((d, LANES), lambda i: (0, 0)),
            pl.BlockSpec((tm, tm), lambda i: (0, 0)),
        ],
        out_specs=(pl.BlockSpec((tm, LANES), lambda i: (i, 0)),
                   pl.BlockSpec((1, LANES), lambda i: (0, 0))),
        scratch_shapes=[pltpu.VMEM((1, LANES), F32)],
        compiler_params=_cparams(("arbitrary",)),
        name="router",
    )(x, gain.reshape(1, d), rw_pad, tril)


def _row_copy(src, dst, s, d, sem):
    return pltpu.make_async_copy(src.at[pl.ds(s, 1)], dst.at[pl.ds(d, 1)], sem)


def _tile_dest(dest, t, td):
    nb = t // td
    dest3 = jnp.concatenate([dest[0].reshape(nb, td), dest[1].reshape(nb, td)], axis=1)
    return dest3.reshape(nb, 1, 2 * td)


def _dispatch_kernel(dest_ref, x_ref, g_ref, xs_in_ref, xs_ref, pk_scr, sem, *, td):
    del xs_in_ref
    pk_scr[...] = _pack_rows(_rms(x_ref[...], g_ref[...]))

    def issue(r, c):
        _row_copy(pk_scr, xs_ref, r, dest_ref[0, 0, r], sem).start()
        _row_copy(pk_scr, xs_ref, r, dest_ref[0, 0, td + r], sem).start()
        return c

    lax.fori_loop(0, td, issue, 0, unroll=8)

    def drain(r, c):
        _row_copy(pk_scr, xs_ref, 0, 0, sem).wait()
        return c

    lax.fori_loop(0, 2 * td, drain, 0)


def _dispatch(x, gain, dest, n_rows, td=512):
    t, d = x.shape
    td = min(td, t)
    xs0 = jnp.zeros((n_rows, d // 2), jnp.uint32)
    return pl.pallas_call(
        functools.partial(_dispatch_kernel, td=td),
        out_shape=jax.ShapeDtypeStruct((n_rows, d // 2), jnp.uint32),
        grid=(t // td,),
        in_specs=[
            pl.BlockSpec((1, 1, 2 * td), lambda i: (i, 0, 0), memory_space=pltpu.SMEM),
            pl.BlockSpec((td, d), lambda i: (i, 0)),
            pl.BlockSpec((1, d), lambda i: (0, 0)),
            pl.BlockSpec(memory_space=pl.ANY),
        ],
        out_specs=pl.BlockSpec(memory_space=pl.ANY),
        scratch_shapes=[pltpu.VMEM((td, d // 2), jnp.uint32), pltpu.SemaphoreType.DMA(())],
        input_output_aliases={3: 0},
        compiler_params=pltpu.CompilerParams(dimension_semantics=("arbitrary",),
                                             vmem_limit_bytes=VMEM_LIMIT_BYTES,
                                             has_side_effects=True),
        name="dispatch",
    )(_tile_dest(dest, t, td), x, gain.reshape(1, d), xs0)


def _combine_kernel(dest_ref, x_ref, info_ref, g_ref, ys_ref, o_ref, y1_scr, y2_scr, sem, *, td):
    def issue(r, c):
        _row_copy(ys_ref, y1_scr, dest_ref[0, 0, r], r, sem).start()
        _row_copy(ys_ref, y2_scr, dest_ref[0, 0, td + r], r, sem).start()
        return c

    lax.fori_loop(0, td, issue, 0, unroll=8)

    def drain(r, c):
        _row_copy(ys_ref, y1_scr, 0, 0, sem).wait()
        return c

    lax.fori_loop(0, 2 * td, drain, 0)
    p1 = info_ref[:, 4:5]
    p2 = info_ref[:, 5:6]
    x = x_ref[...] + (p1 * y1_scr[...] + p2 * y2_scr[...])
    o_ref[...] = _rms(x, g_ref[...])


def _combine(x, ys, dest, info, gain, td=256):
    t, d = x.shape
    td = min(td, t)
    row = pl.BlockSpec((td, d), lambda i: (i, 0))
    return pl.pallas_call(
        functools.partial(_combine_kernel, td=td),
        out_shape=jax.ShapeDtypeStruct((t, d), F32),
        grid=(t // td,),
        in_specs=[
            pl.BlockSpec((1, 1, 2 * td), lambda i: (i, 0, 0), memory_space=pltpu.SMEM),
            row,
            pl.BlockSpec((td, LANES), lambda i: (i, 0)),
            pl.BlockSpec((1, d), lambda i: (0, 0)),
            pl.BlockSpec(memory_space=pl.ANY),
        ],
        out_specs=row,
        scratch_shapes=[pltpu.VMEM((td, d), F32), pltpu.VMEM((td, d), F32),
                        pltpu.SemaphoreType.DMA(())],
        compiler_params=_cparams(("arbitrary",)),
        name="combine",
    )(_tile_dest(dest, t, td), x, info, gain.reshape(1, d), ys)


def _moe(x, norm_g, rw_pad, w1, w3, w2, final_g, tm_e):
    t, d = x.shape
    info, cnt = _router(x, norm_g, rw_pad)
    e1 = info[:, 0].astype(I32)
    e2 = info[:, 1].astype(I32)
    r1 = info[:, 2].astype(I32)
    r2 = info[:, 3].astype(I32)
    counts = cnt[0, :N_EXPERTS].astype(I32)
    padded = ((counts + tm_e - 1) // tm_e) * tm_e
    ends = jnp.cumsum(padded)
    starts = ends - padded
    dest = jnp.stack([starts[e1] + r1, starts[e2] + r2])
    n_tiles = (2 * t + N_EXPERTS * (tm_e - 1)) // tm_e
    n_rows = n_tiles * tm_e
    tile_start = jnp.arange(n_tiles, dtype=I32) * tm_e
    tile_expert = jnp.minimum(
        jnp.sum((tile_start[:, None] >= ends[None, :]).astype(I32), axis=1), N_EXPERTS - 1)
    n_valid = (ends[-1] // tm_e).reshape(1).astype(I32)
    xs = _dispatch(x, norm_g, dest, n_rows)
    ys = _ffn(xs, w1, w3, w2, tile_expert.astype(I32), n_valid, packed=True, tm=tm_e, n_f=2)
    return _combine(x, ys, dest, info, final_g)


def _block_diag(w):
    n, c, d = w.shape
    eye = jnp.eye(n, dtype=w.dtype)
    return (w[:, :, None, :] * eye[:, None, :, None]).reshape(n * c, n * d)


def _cum_matrix():
    t = jnp.arange(CHUNK)
    tril = (t[None, :] <= t[:, None]).astype(F32)
    mats = [tril]
    w = CHUNK // 2
    while w >= DIAG:
        ref_row = (t // (2 * w)) * (2 * w) + w
        mats.append(tril[ref_row])
        w //= 2
    while len(mats) < 4:
        mats.append(jnp.zeros_like(tril))
    return jnp.concatenate(mats, axis=0).astype(BF16)


def _mixer_params(l, lb, hg_norm_g, conv_w, conv_b, w_ra, b_ra, w_ix, b_ix, lru_lambda, wdtype):
    row = lambda a: a.reshape(1, -1).astype(F32)
    return {
        "lbp": jnp.stack([jnp.log(lb), jnp.log1p(-lb), 1.0 - lb]).astype(F32),
        "hgg": row(hg_norm_g[l]),
        "cw": conv_w[l].astype(F32),
        "cb": row(conv_b[l]),
        "wra": _block_diag(w_ra[l].astype(F32)).astype(wdtype),
        "bra": row(b_ra[l]),
        "wix": _block_diag(w_ix[l].astype(F32)).astype(wdtype),
        "bix": row(b_ix[l]),
        "c8": row(-LRU_C * jax.nn.softplus(-lru_lambda[l].astype(F32))),
        "cm": _cum_matrix(),
    }


def _trunk(x, seq_shape, states, prm):
    bsz, seq = seq_shape
    one = jnp.ones((1,), I32)
    new_hg, new_lru, new_conv = [], [], []
    for l in range(2):
        mp = prm["mixer"][l]
        z = _linear(x, prm["w_in"][l], gain=prm["norm1_g"][l])
        if states is None:
            merged, s1, s2, s3 = _mixer_prompt(z.reshape(bsz, seq, IN_WIDTH), mp)
            merged = merged.reshape(bsz * seq, D_MODEL)
        else:
            merged, s1, s2, s3 = _mixer_step(z, states[0][l], states[1][l], states[2][l], mp)
        new_hg.append(s1)
        new_lru.append(s2)
        new_conv.append(s3)
        x = _linear(merged, prm["w_out"][l], res=x)
        if l == 0:
            n_tiles = max(x.shape[0] // 512, 1)
            n_f = 1 if prm["ffn_w1"].dtype == BF16 else prm["ffn_w1"].shape[2] // 256
            x = _ffn(x, prm["ffn_w1"], prm["ffn_w3"], prm["ffn_w2"], jnp.zeros((n_tiles,), I32),
                     one * n_tiles, gain=prm["norm2_g"][0], res=True, tm=512, n_f=n_f)
        else:
            x = _moe(x, prm["norm2_g"][1], prm["rw_pad"], prm["moe_w1"], prm["moe_w3"],
                     prm["moe_w2"], prm["final_norm_g"], tm_e=512 if states is None else 128)
    return x, jnp.stack(new_hg), jnp.stack(new_lru), jnp.stack(new_conv)


def kernel(x_prompt, x_sample, state_hgrn, state_lru, state_conv, norm1_g, w_in, lower_bounds,
           hg_norm_g, conv_w, conv_b, w_ra, b_ra, w_ix, b_ix, lru_lambda, w_out, norm2_g,
           ffn_w1, ffn_w3, ffn_w2, router_w, moe_w1, moe_w3, moe_w2, final_norm_g):
    lb_all = jnp.cumsum(jax.nn.softmax(lower_bounds.astype(F32), axis=0), axis=0)
    lb_all = lb_all - lb_all[0]
    experts = {
        "moe_w1": moe_w1[0].astype(BF16), "moe_w3": moe_w3[0].astype(BF16),
        "moe_w2": moe_w2[0].astype(BF16),
        "rw_pad": jnp.pad(router_w[0].astype(F32), ((0, 0), (0, LANES - N_EXPERTS))),
        "norm1_g": norm1_g, "norm2_g": norm2_g, "final_norm_g": final_norm_g,
    }

    def params(wdtype):
        return dict(
            experts,
            mixer=[_mixer_params(l, lb_all[l], hg_norm_g, conv_w, conv_b, w_ra, b_ra, w_ix, b_ix,
                                 lru_lambda, wdtype) for l in range(2)],
            w_in=w_in.astype(wdtype), w_out=w_out.astype(wdtype),
            ffn_w1=ffn_w1.astype(wdtype), ffn_w3=ffn_w3.astype(wdtype),
            ffn_w2=ffn_w2.astype(wdtype))

    bp, sp, d = x_prompt.shape
    bs = x_sample.shape[0]
    y_p, hg_p, lru_p, conv_p = _trunk(x_prompt.reshape(bp * sp, d), (bp, sp), None, params(BF16))
    y_s, hg_s, lru_s, conv_s = _trunk(x_sample.reshape(bs, d), (bs, 1),
                                      (state_hgrn, state_lru, state_conv), params(F32))
    return (y_p.reshape(bp, sp, d), y_s.reshape(bs, 1, d), hg_p, lru_p, conv_p, hg_s, lru_s,
            conv_s)
```

```python
import functools

import jax
import jax.numpy as jnp
from jax import lax
from jax.experimental import pallas as pl
from jax.experimental.pallas import tpu as pltpu

F32 = jnp.float32
BF16 = jnp.bfloat16
I32 = jnp.int32

D_MODEL = 1024
HG_WIDTH = 512
HG_HEADS = 4
HG_D = 128
LRU_WIDTH = 512
LRU_BLOCKS = 8
LRU_C = 8.0
CONV_K = 4
IN_WIDTH = 4 * HG_WIDTH + 2 * LRU_WIDTH
N_EXPERTS = 8
EPS = 1e-6

CHUNK = 64
DIAG = 1
LANES = 128
VMEM_LIMIT_BYTES = 56 * 1024 * 1024


def _cparams(sem):
    return pltpu.CompilerParams(dimension_semantics=sem, vmem_limit_bytes=VMEM_LIMIT_BYTES)


def _sigmoid(x):
    return 1.0 / (1.0 + jnp.exp(-x))


def _silu(x):
    return x * _sigmoid(x)


def _gelu_tanh(x):
    c = 0.7978845608028654
    return 0.5 * x * (1.0 + jnp.tanh(c * (x + 0.044715 * (x * x * x))))


def _rms(x, g):
    return x * lax.rsqrt(jnp.mean(x * x, axis=-1, keepdims=True) + EPS) * g


def _dot(a, b):
    if b.dtype == F32:
        return jnp.dot(a.astype(F32), b, preferred_element_type=F32,
                       precision=lax.Precision.HIGHEST)
    return jnp.dot(a.astype(BF16), b, preferred_element_type=F32)


def _dot_nt(a, b):
    return lax.dot_general(a, b, (((1,), (1,)), ((), ())), preferred_element_type=F32)


def _dot_tn(a, b):
    return lax.dot_general(a, b, (((0,), (0,)), ((), ())), preferred_element_type=F32)


def _split3(x):
    x1 = x.astype(BF16)
    r1 = x - x1.astype(F32)
    x2 = r1.astype(BF16)
    x3 = (r1 - x2.astype(F32)).astype(BF16)
    return x1, x2, x3


def _linear_kernel(*refs, has_gain, has_res):
    it = iter(refs)
    x_ref = next(it)
    g_ref = next(it) if has_gain else None
    w_ref = next(it)
    r_ref = next(it) if has_res else None
    o_ref = next(it)
    h_scr = next(it)

    @pl.when(pl.program_id(1) == 0)
    def _():
        x = x_ref[...].astype(F32)
        if has_gain:
            x = _rms(x, g_ref[...])
        h_scr[...] = x.astype(h_scr.dtype)

    acc = _dot(h_scr[...], w_ref[...])
    if has_res:
        acc = acc + r_ref[...]
    o_ref[...] = acc.astype(o_ref.dtype)


def _linear(x, w, gain=None, res=None, tm=1024, tn=1024):
    m, k = x.shape
    n = w.shape[1]
    tm = min(tm, m)
    tn = min(tn, n)
    in_specs = [pl.BlockSpec((tm, k), lambda i, j: (i, 0))]
    args = [x]
    if gain is not None:
        in_specs.append(pl.BlockSpec((1, k), lambda i, j: (0, 0)))
        args.append(gain.reshape(1, k))
    in_specs.append(pl.BlockSpec((k, tn), lambda i, j: (0, j)))
    args.append(w)
    if res is not None:
        in_specs.append(pl.BlockSpec((tm, tn), lambda i, j: (i, j)))
        args.append(res)
    return pl.pallas_call(
        functools.partial(_linear_kernel, has_gain=gain is not None, has_res=res is not None),
        out_shape=jax.ShapeDtypeStruct((m, n), F32),
        grid=(m // tm, n // tn),
        in_specs=in_specs,
        out_specs=pl.BlockSpec((tm, tn), lambda i, j: (i, j)),
        scratch_shapes=[pltpu.VMEM((tm, k), w.dtype)],
        compiler_params=_cparams(("parallel", "arbitrary")),
        name="linear",
    )(*args)


def _ffn_kernel(te_ref, nv_ref, *refs, has_gain, has_res, tf, ck):
    del te_ref
    it = iter(refs)
    x_ref = next(it)
    g_ref = next(it) if has_gain else None
    w1_ref, w3_ref, w2_ref, o_ref, xb_scr, a_scr, acc_scr = it
    i = pl.program_id(0)
    j = pl.program_id(1)
    nj = pl.num_programs(1)
    valid = i < nv_ref[0]

    @pl.when(valid)
    def _():
        @pl.when(j == 0)
        def _():
            x = x_ref[...]
            if has_gain:
                x = _rms(x, g_ref[...])
            xb_scr[...] = x.astype(xb_scr.dtype)

        xb = xb_scr[...]
        for c in range(tf // ck):
            h1 = _dot(xb, w1_ref[0, :, c * ck:(c + 1) * ck])
            h3 = _dot(xb, w3_ref[0, :, c * ck:(c + 1) * ck])
            a_scr[:, c * ck:(c + 1) * ck] = (_silu(h1) * h3).astype(a_scr.dtype)
        part = _dot(a_scr[...], w2_ref[0])

        @pl.when(j == 0)
        def _():
            acc_scr[...] = part

        @pl.when(j > 0)
        def _():
            acc_scr[...] += part

        @pl.when(j == nj - 1)
        def _():
            out = acc_scr[...]
            if has_res:
                out = x_ref[...] + out
            o_ref[...] = out

    @pl.when(jnp.logical_and(jnp.logical_not(valid), j == nj - 1))
    def _():
        o_ref[...] = jnp.zeros_like(o_ref)


def _ffn(x, w1, w3, w2, tile_expert, n_valid, gain=None, res=False, tm=512, n_f=2, ck=256):
    p, d = x.shape
    f = w1.shape[2]
    tm = min(tm, p)
    tf = f // n_f
    n_tiles = p // tm
    in_specs = [pl.BlockSpec((tm, x.shape[1]), lambda i, j, te, nv: (i, 0))]
    args = [x]
    if gain is not None:
        in_specs.append(pl.BlockSpec((1, d), lambda i, j, te, nv: (0, 0)))
        args.append(gain.reshape(1, d))
    in_specs += [
        pl.BlockSpec((1, d, tf), lambda i, j, te, nv: (te[i], 0, j)),
        pl.BlockSpec((1, d, tf), lambda i, j, te, nv: (te[i], 0, j)),
        pl.BlockSpec((1, tf, d), lambda i, j, te, nv: (te[i], j, 0)),
    ]
    args += [w1, w3, w2]
    grid_spec = pltpu.PrefetchScalarGridSpec(
        num_scalar_prefetch=2,
        grid=(n_tiles, n_f),
        in_specs=in_specs,
        out_specs=pl.BlockSpec((tm, d), lambda i, j, te, nv: (i, 0)),
        scratch_shapes=[pltpu.VMEM((tm, d), w1.dtype), pltpu.VMEM((tm, tf), w1.dtype),
                        pltpu.VMEM((tm, d), F32)],
    )
    return pl.pallas_call(
        functools.partial(_ffn_kernel, has_gain=gain is not None, has_res=res, tf=tf, ck=ck),
        out_shape=jax.ShapeDtypeStruct((p, d), F32),
        grid_spec=grid_spec,
        compiler_params=_cparams(("arbitrary", "arbitrary")),
        name="swiglu",
    )(tile_expert, n_valid, *args)


def _hgrn_gates(fz, loglb, l1mlb, oneml):
    e = jnp.exp(-jnp.abs(fz))
    log_sig = jnp.minimum(fz, 0.0) - jnp.log1p(e)
    cc = l1mlb + log_sig
    log_f = jnp.maximum(loglb, cc) + jnp.log1p(jnp.exp(-jnp.abs(loglb - cc)))
    k = oneml * _sigmoid(-fz)
    return log_f, k


def _level_reference(b, w, row):
    n = b.shape[0]
    if w >= 4:
        return jnp.concatenate(
            [jnp.broadcast_to(b[j + w:j + w + 1, :], (2 * w, b.shape[1]))
             for j in range(0, n, 2 * w)], axis=0)
    ahead = lambda s: pltpu.roll(b, n - s, 0)
    pos = row & (2 * w - 1)
    if w == 2:
        return jnp.where(pos == 0, ahead(2),
                         jnp.where(pos == 1, ahead(1),
                                   jnp.where(pos == 2, b, pltpu.roll(b, 1, 0))))
    return jnp.where(pos == 0, ahead(1), b)


def _lru_gates(xc, wra, bra, wix, bix, c8):
    r = _sigmoid(_dot(xc, wra) + bra)
    ig = _sigmoid(_dot(xc, wix) + bix)
    log_a = c8 * r
    a = jnp.exp(log_a)
    one_m_a2 = -jnp.tanh(log_a) * (a * a + 1.0)
    u = jnp.sqrt(one_m_a2) * (ig * xc)
    return a, u


def _mixer_block_kernel(x_ref, g1_ref, win_ref, lbp_ref, hgg_ref, cw_ref, cb_ref, wra_ref,
                        bra_ref, wix_ref, bix_ref, c8_ref, cm_ref, wout_ref,
                        xo_ref, hs_ref, ls_ref, cs_ref,
                        st_scr, h_scr, xpad_scr, hb_scr, q_scr, lf_scr, k_scr, v_scr, g_scr,
                        gl_scr, a_scr, u_scr, m_scr, *, tl):
    l = pl.program_id(1)
    nl = pl.num_programs(1)

    @pl.when(l == 0)
    def _():
        st_scr[...] = jnp.zeros_like(st_scr)
        h_scr[...] = jnp.zeros_like(h_scr)
        xpad_scr[0:8, :] = jnp.zeros((8, LRU_WIDTH), F32)

    hb_scr[...] = _rms(x_ref[0], g1_ref[...]).astype(BF16)
    hb = hb_scr[...]
    zg = lambda j: _dot(hb, win_ref[:, j * HG_WIDTH:(j + 1) * HG_WIDTH])
    q_scr[...] = _silu(zg(0))
    log_f, kk = _hgrn_gates(zg(1), lbp_ref[0:1, :], lbp_ref[1:2, :], lbp_ref[2:3, :])
    lf_scr[...] = log_f
    k_scr[...] = kk
    v_scr[...] = zg(2)
    g_scr[...] = _silu(zg(3))
    xpad_scr[8:8 + tl, :] = zg(4)
    gl_scr[...] = _gelu_tanh(zg(5))
    xc = cb_ref[...]
    for j in range(CONV_K):
        xc = xc + xpad_scr[5 + j:5 + j + tl, :] * cw_ref[j:j + 1, :]
    tail = xpad_scr[tl + 5:tl + 8, :]
    xpad_scr[5:8, :] = tail
    a, u = _lru_gates(xc, wra_ref[...], bra_ref[...], wix_ref[...], bix_ref[...], c8_ref[...])
    a_scr[...] = a
    u_scr[...] = u

    row = lax.broadcasted_iota(I32, (CHUNK, 1), 0)
    ti = lax.broadcasted_iota(I32, (CHUNK, CHUNK), 0)
    si = lax.broadcasted_iota(I32, (CHUNK, CHUNK), 1)
    t3 = lax.broadcasted_iota(I32, (1, DIAG, 1), 1)
    nblk = CHUNK // DIAG

    def chunk_body(c, carry):
        r0 = pl.multiple_of(c * CHUNK, CHUNK)
        rows = pl.ds(r0, CHUNK)

        for gi in range(LRU_WIDTH // LANES):
            cs = slice(gi * LANES, (gi + 1) * LANES)
            aa = a_scr[rows, cs]
            uu = u_scr[rows, cs]
            d = 1
            while d < CHUNK:
                keep = row >= d
                a_sh = pltpu.roll(aa, d, 0)
                u_sh = pltpu.roll(uu, d, 0)
                uu = jnp.where(keep, aa * u_sh + uu, uu)
                aa = jnp.where(keep, aa * a_sh, aa)
                d *= 2
            hs = uu + aa * h_scr[:, cs]
            h_scr[:, cs] = hs[CHUNK - 1:CHUNK, :]
            m_scr[rows, HG_WIDTH + gi * LANES:HG_WIDTH + (gi + 1) * LANES] = (
                hs * gl_scr[rows, cs]).astype(BF16)

        q = q_scr[rows, :]
        k = k_scr[rows, :]
        v = v_scr[rows, :]
        f1, f2, f3 = _split3(lf_scr[rows, :])
        cm = cm_ref[...]
        b = _dot(cm, f1) + _dot(cm, f2) + _dot(cm, f3)

        for hh in range(HG_HEADS):
            cs = slice(hh * HG_D, (hh + 1) * HG_D)
            qh, kh, vh, bh = q[:, cs], k[:, cs], v[:, cs], b[:, cs]
            vb = vh.astype(BF16)
            amat = jnp.zeros((CHUNK, CHUNK), F32)
            w = CHUNK // 2
            while w >= DIAG:
                rw = _level_reference(bh, w, row)
                upper = (row & (2 * w - 1)) >= w
                dec = jnp.exp(-jnp.abs(bh - rw))
                qw = jnp.where(upper, qh * dec, 0.0).astype(BF16)
                kw = jnp.where(upper, 0.0, kh * dec).astype(BF16)
                aw = _dot_nt(qw, kw)
                if 2 * w < CHUNK:
                    sh = (2 * w).bit_length() - 1
                    aw = jnp.where((ti >> sh) == (si >> sh), aw, 0.0)
                amat = amat + aw
                w //= 2
            o = _dot(amat.astype(BF16), vb)
            st = st_scr[hh]
            o = o + _dot_nt((qh * jnp.exp(bh)).astype(BF16), st.astype(BF16))
            if DIAG == 1:
                o = o + jnp.sum(qh * kh, axis=-1, keepdims=True) * vh
            else:
                q3 = qh.reshape(nblk, DIAG, HG_D)
                k3 = kh.reshape(nblk, DIAG, HG_D)
                v3 = vh.reshape(nblk, DIAG, HG_D)
                b3 = bh.reshape(nblk, DIAG, HG_D)
                od = jnp.zeros((nblk, DIAG, HG_D), F32)
                for s in range(DIAG):
                    dec = jnp.exp(jnp.minimum(b3 - b3[:, s:s + 1, :], 0.0))
                    p = q3 * (k3[:, s:s + 1, :] * dec)
                    rs = jnp.sum(p, axis=-1, keepdims=True)
                    rs = jnp.where(t3 >= s, rs, 0.0)
                    od = od + rs * v3[:, s:s + 1, :]
                o = o + od.reshape(CHUNK, HG_D)
            bl = bh[CHUNK - 1:CHUNK, :]
            kdec = (kh * jnp.exp(bl - bh)).astype(BF16)
            st_scr[hh] = st * jnp.exp(bl) + _dot_tn(vb, kdec)
            o = o * lax.rsqrt(jnp.mean(o * o, axis=-1, keepdims=True) + EPS)
            o = o * hgg_ref[:, cs]
            o = o * g_scr[rows, cs]
            m_scr[rows, cs] = o.astype(BF16)
        return carry

    lax.fori_loop(0, tl // CHUNK, chunk_body, 0, unroll=4)
    xo_ref[0] = x_ref[0] + _dot(m_scr[...], wout_ref[...])

    @pl.when(l == nl - 1)
    def _():
        for hh in range(HG_HEADS):
            hs_ref[0, hh] = st_scr[hh].T
        ls_ref[0] = h_scr[...]
        cs_ref[0] = xpad_scr[5:8, :]


def _mixer_block(x, gain, w_in, w_out, mp, tl=512):
    bsz, seq, d = x.shape
    tl = min(tl, seq)
    full = lambda shape: pl.BlockSpec(shape, lambda b, l: (0,) * len(shape))
    out_shapes = (
        jax.ShapeDtypeStruct((bsz, seq, d), F32),
        jax.ShapeDtypeStruct((bsz, HG_HEADS, HG_D, HG_D), F32),
        jax.ShapeDtypeStruct((bsz, 1, LRU_WIDTH), F32),
        jax.ShapeDtypeStruct((bsz, CONV_K - 1, LRU_WIDTH), F32),
    )
    half = lambda dt: pltpu.VMEM((tl, HG_WIDTH), dt)
    xo, hg, lru, conv = pl.pallas_call(
        functools.partial(_mixer_block_kernel, tl=tl),
        out_shape=out_shapes,
        grid=(bsz, seq // tl),
        in_specs=[
            pl.BlockSpec((1, tl, d), lambda b, l: (b, l, 0)),
            full((1, d)), full((d, IN_WIDTH)),
            full((3, HG_WIDTH)), full((1, HG_WIDTH)), full((CONV_K, LRU_WIDTH)),
            full((1, LRU_WIDTH)), full((LRU_WIDTH, LRU_WIDTH)), full((1, LRU_WIDTH)),
            full((LRU_WIDTH, LRU_WIDTH)), full((1, LRU_WIDTH)), full((1, LRU_WIDTH)),
            full(mp["cm"].shape), full((d, d)),
        ],
        out_specs=(
            pl.BlockSpec((1, tl, d), lambda b, l: (b, l, 0)),
            pl.BlockSpec((1, HG_HEADS, HG_D, HG_D), lambda b, l: (b, 0, 0, 0)),
            pl.BlockSpec((1, 1, LRU_WIDTH), lambda b, l: (b, 0, 0)),
            pl.BlockSpec((1, CONV_K - 1, LRU_WIDTH), lambda b, l: (b, 0, 0)),
        ),
        scratch_shapes=[
            pltpu.VMEM((HG_HEADS, HG_D, HG_D), F32),
            pltpu.VMEM((1, LRU_WIDTH), F32),
            pltpu.VMEM((tl + 8, LRU_WIDTH), F32),
            pltpu.VMEM((tl, d), BF16),
            half(F32), half(F32), half(F32), half(F32), half(F32), half(F32), half(F32),
            half(F32),
            pltpu.VMEM((tl, d), BF16),
        ],
        compiler_params=_cparams(("parallel", "arbitrary")),
        name="mixer_block",
    )(x, gain.reshape(1, d), w_in, mp["lbp"], mp["hgg"], mp["cw"], mp["cb"], mp["wra"],
      mp["bra"], mp["wix"], mp["bix"], mp["c8"], mp["cm"], w_out)
    return xo, hg, lru.reshape(bsz, LRU_WIDTH), conv


def _mixer_step_kernel(z_ref, sh_ref, sl_ref, sc_ref, lbp_ref, hgg_ref, cw_ref, cb_ref, wra_ref,
                       bra_ref, wix_ref, bix_ref, c8_ref, m_ref, hn_ref, ln_ref, cn_ref, o_scr,
                       *, bb):
    z = z_ref[...]
    xr = z[:, 4 * HG_WIDTH:4 * HG_WIDTH + LRU_WIDTH]
    gr = z[:, 4 * HG_WIDTH + LRU_WIDTH:IN_WIDTH]
    buf = sc_ref[...]
    xc = cb_ref[...]
    for j in range(CONV_K - 1):
        xc = xc + buf[:, j * LRU_WIDTH:(j + 1) * LRU_WIDTH] * cw_ref[j:j + 1, :]
    xc = xc + xr * cw_ref[CONV_K - 1:CONV_K, :]
    cn_ref[:, 0:(CONV_K - 2) * LRU_WIDTH] = buf[:, LRU_WIDTH:(CONV_K - 1) * LRU_WIDTH]
    cn_ref[:, (CONV_K - 2) * LRU_WIDTH:(CONV_K - 1) * LRU_WIDTH] = xr
    a, u = _lru_gates(xc, wra_ref[...], bra_ref[...], wix_ref[...], bix_ref[...], c8_ref[...])
    hnew = u + a * sl_ref[...]
    ln_ref[...] = hnew
    m_ref[:, HG_WIDTH:D_MODEL] = (hnew * _gelu_tanh(gr)).astype(m_ref.dtype)

    q = _silu(z[:, 0:HG_WIDTH])
    log_f, k = _hgrn_gates(z[:, HG_WIDTH:2 * HG_WIDTH], lbp_ref[0:1, :], lbp_ref[1:2, :],
                           lbp_ref[2:3, :])
    f = jnp.exp(log_f)
    v = z[:, 2 * HG_WIDTH:3 * HG_WIDTH]
    gate = _silu(z[:, 3 * HG_WIDTH:4 * HG_WIDTH])
    zpad = jnp.zeros((HG_D - bb, HG_D), F32)
    for hh in range(HG_HEADS):
        cs = slice(hh * HG_D, (hh + 1) * HG_D)
        ft = jnp.concatenate([f[:, cs], zpad], axis=0).T
        kt = jnp.concatenate([k[:, cs], zpad], axis=0).T
        qt = jnp.concatenate([q[:, cs], zpad], axis=0).T
        for j in range(bb):
            fcol = jnp.broadcast_to(ft[:, j:j + 1], (HG_D, HG_D))
            kcol = jnp.broadcast_to(kt[:, j:j + 1], (HG_D, HG_D))
            qcol = jnp.broadcast_to(qt[:, j:j + 1], (HG_D, HG_D))
            vrow = jnp.broadcast_to(v[j:j + 1, cs], (HG_D, HG_D))
            sn = fcol * sh_ref[j, hh] + kcol * vrow
            hn_ref[j, hh] = sn
            o_scr[j:j + 1, cs] = jnp.sum(qcol * sn, axis=0, keepdims=True)
    for hh in range(HG_HEADS):
        cs = slice(hh * HG_D, (hh + 1) * HG_D)
        o = o_scr[:, cs]
        o = o * lax.rsqrt(jnp.mean(o * o, axis=-1, keepdims=True) + EPS)
        o = o * hgg_ref[:, cs]
        o = o * gate[:, cs]
        m_ref[:, cs] = o.astype(m_ref.dtype)


def _mixer_step(z, s_hg, s_lru, s_conv, mp, bb=16):
    bsz = z.shape[0]
    bb = min(bb, bsz)
    cw3 = (CONV_K - 1) * LRU_WIDTH
    full = lambda shape: pl.BlockSpec(shape, lambda i: (0,) * len(shape))
    out_shapes = (
        jax.ShapeDtypeStruct((bsz, D_MODEL), mp["wra"].dtype),
        jax.ShapeDtypeStruct((bsz, HG_HEADS, HG_D, HG_D), F32),
        jax.ShapeDtypeStruct((bsz, LRU_WIDTH), F32),
        jax.ShapeDtypeStruct((bsz, cw3), F32),
    )
    merged, hg, lru, conv = pl.pallas_call(
        functools.partial(_mixer_step_kernel, bb=bb),
        out_shape=out_shapes,
        grid=(bsz // bb,),
        in_specs=[
            pl.BlockSpec((bb, IN_WIDTH), lambda i: (i, 0)),
            pl.BlockSpec((bb, HG_HEADS, HG_D, HG_D), lambda i: (i, 0, 0, 0)),
            pl.BlockSpec((bb, LRU_WIDTH), lambda i: (i, 0)),
            pl.BlockSpec((bb, cw3), lambda i: (i, 0)),
            full((3, HG_WIDTH)), full((1, HG_WIDTH)), full((CONV_K, LRU_WIDTH)),
            full((1, LRU_WIDTH)), full((LRU_WIDTH, LRU_WIDTH)), full((1, LRU_WIDTH)),
            full((LRU_WIDTH, LRU_WIDTH)), full((1, LRU_WIDTH)), full((1, LRU_WIDTH)),
        ],
        out_specs=(
            pl.BlockSpec((bb, D_MODEL), lambda i: (i, 0)),
            pl.BlockSpec((bb, HG_HEADS, HG_D, HG_D), lambda i: (i, 0, 0, 0)),
            pl.BlockSpec((bb, LRU_WIDTH), lambda i: (i, 0)),
            pl.BlockSpec((bb, cw3), lambda i: (i, 0)),
        ),
        scratch_shapes=[pltpu.VMEM((bb, HG_WIDTH), F32)],
        compiler_params=_cparams(("parallel",)),
        name="mixer_step",
    )(z, s_hg, s_lru, s_conv.reshape(bsz, cw3), mp["lbp"], mp["hgg"], mp["cw"], mp["cb"],
      mp["wra"], mp["bra"], mp["wix"], mp["bix"], mp["c8"])
    return merged, hg, lru, conv.reshape(bsz, CONV_K - 1, LRU_WIDTH)


def _router_kernel(x_ref, g_ref, rw_ref, tril_ref, info_ref, cnt_ref, carry_scr):
    i = pl.program_id(0)

    @pl.when(i == 0)
    def _():
        carry_scr[...] = jnp.zeros_like(carry_scr)

    h = _rms(x_ref[...], g_ref[...])
    logits = jnp.dot(h, rw_ref[...], preferred_element_type=F32, precision=lax.Precision.HIGHEST)
    lane = lax.broadcasted_iota(I32, logits.shape, 1)
    neg = jnp.float32(-jnp.inf)
    logits = jnp.where(lane < N_EXPERTS, logits, neg)
    m1 = jnp.max(logits, axis=-1, keepdims=True)
    i1 = jnp.min(jnp.where(logits == m1, lane, LANES), axis=-1, keepdims=True)
    l2 = jnp.where(lane == i1, neg, logits)
    m2 = jnp.max(l2, axis=-1, keepdims=True)
    i2 = jnp.min(jnp.where(l2 == m2, lane, LANES), axis=-1, keepdims=True)
    e = jnp.exp(m2 - m1)
    p1 = 1.0 / (1.0 + e)
    p2 = e / (1.0 + e)
    oh1 = (lane == i1).astype(F32)
    oh2 = (lane == i2).astype(F32)
    sel = oh1 + oh2
    before = _dot(tril_ref[...], sel.astype(BF16)) + carry_scr[...]
    r1 = jnp.sum(oh1 * before, axis=-1, keepdims=True)
    r2 = jnp.sum(oh2 * before, axis=-1, keepdims=True)
    carry_scr[...] += jnp.sum(sel, axis=0, keepdims=True)
    info = jnp.where(lane == 0, i1.astype(F32), 0.0)
    info = jnp.where(lane == 1, i2.astype(F32), info)
    info = jnp.where(lane == 2, r1, info)
    info = jnp.where(lane == 3, r2, info)
    info = jnp.where(lane == 4, p1, info)
    info = jnp.where(lane == 5, p2, info)
    info_ref[...] = info
    cnt_ref[...] = carry_scr[...]


def _router(x, gain, rw_pad, tm=512):
    t, d = x.shape
    tm = min(tm, t)
    tril = jnp.tril(jnp.ones((tm, tm), BF16), -1)
    return pl.pallas_call(
        _router_kernel,
        out_shape=(jax.ShapeDtypeStruct((t, LANES), F32),
                   jax.ShapeDtypeStruct((1, LANES), F32)),
        grid=(t // tm,),
        in_specs=[
            pl.BlockSpec((tm, d), lambda i: (i, 0)),
            pl.BlockSpec((1, d), lambda i: (0, 0)),
            pl.BlockSpec((d, LANES), lambda i: (0, 0)),
            pl.BlockSpec((tm, tm), lambda i: (0, 0)),
        ],
        out_specs=(pl.BlockSpec((tm, LANES), lambda i: (i, 0)),
                   pl.BlockSpec((1, LANES), lambda i: (0, 0))),
        scratch_shapes=[pltpu.VMEM((1, LANES), F32)],
        compiler_params=_cparams(("arbitrary",)),
        name="router",
    )(x, gain.reshape(1, d), rw_pad, tril)


def _row_copy(src, dst, s, d, sem):
    return pltpu.make_async_copy(src.at[pl.ds(s, 1)], dst.at[pl.ds(d, 1)], sem)


def _tile_dest(dest, t, td):
    nb = t // td
    dest3 = jnp.concatenate([dest[0].reshape(nb, td), dest[1].reshape(nb, td)], axis=1)
    return dest3.reshape(nb, 1, 2 * td)


def _dispatch_kernel(dest_ref, x_ref, g_ref, xs_in_ref, xs_ref, pk_scr, sem, *, td):
    del xs_in_ref
    pk_scr[...] = _rms(x_ref[...], g_ref[...])

    def issue(r, c):
        _row_copy(pk_scr, xs_ref, r, dest_ref[0, 0, r], sem).start(priority=0)
        _row_copy(pk_scr, xs_ref, r, dest_ref[0, 0, td + r], sem).start(priority=1)
        return c

    lax.fori_loop(0, td, issue, 0, unroll=8)
    for _ in range(2):
        pltpu.make_async_copy(pk_scr, xs_ref.at[pl.ds(0, td)], sem).wait()


def _dispatch(x, gain, dest, n_rows, td=512):
    t, d = x.shape
    td = min(td, t)
    xs0 = jnp.zeros((n_rows, d), F32)
    return pl.pallas_call(
        functools.partial(_dispatch_kernel, td=td),
        out_shape=jax.ShapeDtypeStruct((n_rows, d), F32),
        grid=(t // td,),
        in_specs=[
            pl.BlockSpec((1, 1, 2 * td), lambda i: (i, 0, 0), memory_space=pltpu.SMEM),
            pl.BlockSpec((td, d), lambda i: (i, 0)),
            pl.BlockSpec((1, d), lambda i: (0, 0)),
            pl.BlockSpec(memory_space=pl.ANY),
        ],
        out_specs=pl.BlockSpec(memory_space=pl.ANY),
        scratch_shapes=[pltpu.VMEM((td, d), F32), pltpu.SemaphoreType.DMA(())],
        input_output_aliases={3: 0},
        compiler_params=pltpu.CompilerParams(dimension_semantics=("arbitrary",),
                                             vmem_limit_bytes=VMEM_LIMIT_BYTES,
                                             has_side_effects=True),
        name="dispatch",
    )(_tile_dest(dest, t, td), x, gain.reshape(1, d), xs0)


def _combine_kernel(dest_ref, x_ref, info_ref, g_ref, ys_ref, o_ref, y1_scr, y2_scr, sem, *, td):
    def issue(r, c):
        _row_copy(ys_ref, y1_scr, dest_ref[0, 0, r], r, sem).start(priority=0)
        _row_copy(ys_ref, y2_scr, dest_ref[0, 0, td + r], r, sem).start(priority=1)
        return c

    lax.fori_loop(0, td, issue, 0, unroll=8)
    pltpu.make_async_copy(ys_ref.at[pl.ds(0, td)], y1_scr, sem).wait()
    pltpu.make_async_copy(ys_ref.at[pl.ds(0, td)], y2_scr, sem).wait()
    p1 = info_ref[:, 4:5]
    p2 = info_ref[:, 5:6]
    x = x_ref[...] + (p1 * y1_scr[...] + p2 * y2_scr[...])
    o_ref[...] = _rms(x, g_ref[...])


def _combine(x, ys, dest, info, gain, td=256):
    t, d = x.shape
    td = min(td, t)
    row = pl.BlockSpec((td, d), lambda i: (i, 0))
    return pl.pallas_call(
        functools.partial(_combine_kernel, td=td),
        out_shape=jax.ShapeDtypeStruct((t, d), F32),
        grid=(t // td,),
        in_specs=[
            pl.BlockSpec((1, 1, 2 * td), lambda i: (i, 0, 0), memory_space=pltpu.SMEM),
            row,
            pl.BlockSpec((td, LANES), lambda i: (i, 0)),
            pl.BlockSpec((1, d), lambda i: (0, 0)),
            pl.BlockSpec(memory_space=pl.ANY),
        ],
        out_specs=row,
        scratch_shapes=[pltpu.VMEM((td, d), F32), pltpu.VMEM((td, d), F32),
                        pltpu.SemaphoreType.DMA(())],
        compiler_params=_cparams(("arbitrary",)),
        name="combine",
    )(_tile_dest(dest, t, td), x, info, gain.reshape(1, d), ys)


def _moe(x, norm_g, rw_pad, w1, w3, w2, final_g, tm_e):
    t, d = x.shape
    info, cnt = _router(x, norm_g, rw_pad)
    e1 = info[:, 0].astype(I32)
    e2 = info[:, 1].astype(I32)
    r1 = info[:, 2].astype(I32)
    r2 = info[:, 3].astype(I32)
    counts = cnt[0, :N_EXPERTS].astype(I32)
    padded = ((counts + tm_e - 1) // tm_e) * tm_e
    ends = jnp.cumsum(padded)
    starts = ends - padded
    dest = jnp.stack([starts[e1] + r1, starts[e2] + r2])
    n_tiles = (2 * t + N_EXPERTS * (tm_e - 1)) // tm_e
    n_rows = n_tiles * tm_e
    tile_start = jnp.arange(n_tiles, dtype=I32) * tm_e
    tile_expert = jnp.minimum(
        jnp.sum((tile_start[:, None] >= ends[None, :]).astype(I32), axis=1), N_EXPERTS - 1)
    n_valid = (ends[-1] // tm_e).reshape(1).astype(I32)
    xs = _dispatch(x, norm_g, dest, n_rows)
    ys = _ffn(xs, w1, w3, w2, tile_expert.astype(I32), n_valid, tm=tm_e, n_f=2)
    return _combine(x, ys, dest, info, final_g)


def _block_diag(w):
    n, c, d = w.shape
    eye = jnp.eye(n, dtype=w.dtype)
    return (w[:, :, None, :] * eye[:, None, :, None]).reshape(n * c, n * d)


def _cum_matrix():
    t = jnp.arange(CHUNK)
    return (t[None, :] <= t[:, None]).astype(BF16)


def _mixer_params(l, lb, hg_norm_g, conv_w, conv_b, w_ra, b_ra, w_ix, b_ix, lru_lambda, wdtype):
    row = lambda a: a.reshape(1, -1).astype(F32)
    return {
        "lbp": jnp.stack([jnp.log(lb), jnp.log1p(-lb), 1.0 - lb]).astype(F32),
        "hgg": row(hg_norm_g[l]),
        "cw": conv_w[l].astype(F32),
        "cb": row(conv_b[l]),
        "wra": _block_diag(w_ra[l].astype(F32)).astype(wdtype),
        "bra": row(b_ra[l]),
        "wix": _block_diag(w_ix[l].astype(F32)).astype(wdtype),
        "bix": row(b_ix[l]),
        "c8": row(-LRU_C * jax.nn.softplus(-lru_lambda[l].astype(F32))),
        "cm": _cum_matrix(),
    }


def _trunk(x, seq_shape, states, prm):
    bsz, seq = seq_shape
    one = jnp.ones((1,), I32)
    new_hg, new_lru, new_conv = [], [], []
    for l in range(2):
        mp = prm["mixer"][l]
        if states is None:
            x, s1, s2, s3 = _mixer_block(x.reshape(bsz, seq, D_MODEL), prm["norm1_g"][l],
                                         prm["w_in"][l], prm["w_out"][l], mp)
            x = x.reshape(bsz * seq, D_MODEL)
        else:
            z = _linear(x, prm["w_in"][l], gain=prm["norm1_g"][l])
            merged, s1, s2, s3 = _mixer_step(z, states[0][l], states[1][l], states[2][l], mp)
            x = _linear(merged, prm["w_out"][l], res=x)
        new_hg.append(s1)
        new_lru.append(s2)
        new_conv.append(s3)
        if l == 0:
            n_tiles = max(x.shape[0] // 512, 1)
            n_f = 1 if prm["ffn_w1"].dtype == BF16 else prm["ffn_w1"].shape[2] // 256
            x = _ffn(x, prm["ffn_w1"], prm["ffn_w3"], prm["ffn_w2"], jnp.zeros((n_tiles,), I32),
                     one * n_tiles, gain=prm["norm2_g"][0], res=True, tm=512, n_f=n_f)
        else:
            x = _moe(x, prm["norm2_g"][1], prm["rw_pad"], prm["moe_w1"], prm["moe_w3"],
                     prm["moe_w2"], prm["final_norm_g"], tm_e=512 if states is None else 128)
    return x, jnp.stack(new_hg), jnp.stack(new_lru), jnp.stack(new_conv)


def kernel(x_prompt, x_sample, state_hgrn, state_lru, state_conv, norm1_g, w_in, lower_bounds,
           hg_norm_g, conv_w, conv_b, w_ra, b_ra, w_ix, b_ix, lru_lambda, w_out, norm2_g,
           ffn_w1, ffn_w3, ffn_w2, router_w, moe_w1, moe_w3, moe_w2, final_norm_g):
    lb_all = jnp.cumsum(jax.nn.softmax(lower_bounds.astype(F32), axis=0), axis=0)
    lb_all = lb_all - lb_all[0]
    experts = {
        "moe_w1": moe_w1[0].astype(BF16), "moe_w3": moe_w3[0].astype(BF16),
        "moe_w2": moe_w2[0].astype(BF16),
        "rw_pad": jnp.pad(router_w[0].astype(F32), ((0, 0), (0, LANES - N_EXPERTS))),
        "norm1_g": norm1_g, "norm2_g": norm2_g, "final_norm_g": final_norm_g,
    }

    def params(wdtype):
        return dict(
            experts,
            mixer=[_mixer_params(l, lb_all[l], hg_norm_g, conv_w, conv_b, w_ra, b_ra, w_ix, b_ix,
                                 lru_lambda, wdtype) for l in range(2)],
            w_in=w_in.astype(wdtype), w_out=w_out.astype(wdtype),
            ffn_w1=ffn_w1.astype(wdtype), ffn_w3=ffn_w3.astype(wdtype),
            ffn_w2=ffn_w2.astype(wdtype))

    bp, sp, d = x_prompt.shape
    bs = x_sample.shape[0]
    y_p, hg_p, lru_p, conv_p = _trunk(x_prompt.reshape(bp * sp, d), (bp, sp), None, params(BF16))
    y_s, hg_s, lru_s, conv_s = _trunk(x_sample.reshape(bs, d), (bs, 1),
                                      (state_hgrn, state_lru, state_conv), params(F32))
    return (y_p.reshape(bp, sp, d), y_s.reshape(bs, 1, d), hg_p, lru_p, conv_p, hg_s, lru_s,
            conv_s)
```

```python
import functools

import jax
import jax.numpy as jnp
from jax import lax
from jax.experimental import pallas as pl
from jax.experimental.pallas import tpu as pltpu

F32 = jnp.float32
BF16 = jnp.bfloat16
I32 = jnp.int32

D_MODEL = 1024
HG_WIDTH = 512
HG_HEADS = 4
HG_D = 128
LRU_WIDTH = 512
LRU_BLOCKS = 8
LRU_C = 8.0
CONV_K = 4
IN_WIDTH = 4 * HG_WIDTH + 2 * LRU_WIDTH
N_EXPERTS = 8
EPS = 1e-6

CHUNK = 64
DIAG = 1
LANES = 128
SUBLANES = 8
VMEM_LIMIT_BYTES = 56 * 1024 * 1024


def _cparams(sem):
    return pltpu.CompilerParams(dimension_semantics=sem, vmem_limit_bytes=VMEM_LIMIT_BYTES)


def _sigmoid(x):
    return 1.0 / (1.0 + jnp.exp(-x))


def _silu(x):
    return x * _sigmoid(x)


def _gelu_tanh(x):
    c = 0.7978845608028654
    return 0.5 * x * (1.0 + jnp.tanh(c * (x + 0.044715 * (x * x * x))))


def _rms(x, g):
    return x * lax.rsqrt(jnp.mean(x * x, axis=-1, keepdims=True) + EPS) * g


def _dot(a, b):
    if b.dtype == F32:
        return jnp.dot(a.astype(F32), b, preferred_element_type=F32,
                       precision=lax.Precision.HIGHEST)
    return jnp.dot(a.astype(BF16), b, preferred_element_type=F32)


def _dot_nt(a, b):
    return lax.dot_general(a, b, (((1,), (1,)), ((), ())), preferred_element_type=F32)


def _dot_tn(a, b):
    return lax.dot_general(a, b, (((0,), (0,)), ((), ())), preferred_element_type=F32)


def _split3(x):
    x1 = x.astype(BF16)
    r1 = x - x1.astype(F32)
    x2 = r1.astype(BF16)
    x3 = (r1 - x2.astype(F32)).astype(BF16)
    return x1, x2, x3


def _linear_kernel(*refs, has_gain, has_res):
    it = iter(refs)
    x_ref = next(it)
    g_ref = next(it) if has_gain else None
    w_ref = next(it)
    r_ref = next(it) if has_res else None
    o_ref = next(it)
    h_scr = next(it)

    @pl.when(pl.program_id(1) == 0)
    def _():
        x = x_ref[...].astype(F32)
        if has_gain:
            x = _rms(x, g_ref[...])
        h_scr[...] = x.astype(h_scr.dtype)

    acc = _dot(h_scr[...], w_ref[...])
    if has_res:
        acc = acc + r_ref[...]
    o_ref[...] = acc.astype(o_ref.dtype)


def _linear(x, w, layer, gain=None, res=None, tm=1024, tn=1024):
    m, k = x.shape
    n = w.shape[2]
    tm = min(tm, m)
    tn = min(tn, n)
    in_specs = [pl.BlockSpec((tm, k), lambda i, j: (i, 0))]
    args = [x]
    if gain is not None:
        in_specs.append(pl.BlockSpec((1, k), lambda i, j: (0, 0)))
        args.append(gain.reshape(1, k))
    in_specs.append(pl.BlockSpec((None, k, tn), lambda i, j: (layer, 0, j)))
    args.append(w)
    if res is not None:
        in_specs.append(pl.BlockSpec((tm, tn), lambda i, j: (i, j)))
        args.append(res)
    return pl.pallas_call(
        functools.partial(_linear_kernel, has_gain=gain is not None, has_res=res is not None),
        out_shape=jax.ShapeDtypeStruct((m, n), F32),
        grid=(m // tm, n // tn),
        in_specs=in_specs,
        out_specs=pl.BlockSpec((tm, tn), lambda i, j: (i, j)),
        scratch_shapes=[pltpu.VMEM((tm, k), w.dtype)],
        compiler_params=_cparams(("parallel", "arbitrary")),
        name="linear",
    )(*args)


def _ffn_kernel(te_ref, nv_ref, *refs, has_gain, has_res, tf, ck):
    del te_ref
    it = iter(refs)
    x_ref = next(it)
    g_ref = next(it) if has_gain else None
    w1_ref, w3_ref, w2_ref, o_ref, xb_scr, a_scr, acc_scr = it
    i = pl.program_id(0)
    j = pl.program_id(1)
    nj = pl.num_programs(1)
    valid = i < nv_ref[0]

    @pl.when(valid)
    def _():
        @pl.when(j == 0)
        def _():
            x = x_ref[...]
            if has_gain:
                x = _rms(x, g_ref[...])
            xb_scr[...] = x.astype(xb_scr.dtype)

        xb = xb_scr[...]
        for c in range(tf // ck):
            h1 = _dot(xb, w1_ref[0, :, c * ck:(c + 1) * ck])
            h3 = _dot(xb, w3_ref[0, :, c * ck:(c + 1) * ck])
            a_scr[:, c * ck:(c + 1) * ck] = (_silu(h1) * h3).astype(a_scr.dtype)
        part = _dot(a_scr[...], w2_ref[0])

        @pl.when(j == 0)
        def _():
            acc_scr[...] = part

        @pl.when(j > 0)
        def _():
            acc_scr[...] += part

        @pl.when(j == nj - 1)
        def _():
            out = acc_scr[...]
            if has_res:
                out = x_ref[...] + out
            o_ref[...] = out

    @pl.when(jnp.logical_and(jnp.logical_not(valid), j == nj - 1))
    def _():
        o_ref[...] = jnp.zeros_like(o_ref)


def _ffn(x, w1, w3, w2, tile_expert, n_valid, gain=None, res=False, tm=512, n_f=2, ck=256):
    p, d = x.shape
    f = w1.shape[2]
    tm = min(tm, p)
    tf = f // n_f
    n_tiles = p // tm
    in_specs = [pl.BlockSpec((tm, x.shape[1]), lambda i, j, te, nv: (i, 0))]
    args = [x]
    if gain is not None:
        in_specs.append(pl.BlockSpec((1, d), lambda i, j, te, nv: (0, 0)))
        args.append(gain.reshape(1, d))
    in_specs += [
        pl.BlockSpec((1, d, tf), lambda i, j, te, nv: (te[i], 0, j)),
        pl.BlockSpec((1, d, tf), lambda i, j, te, nv: (te[i], 0, j)),
        pl.BlockSpec((1, tf, d), lambda i, j, te, nv: (te[i], j, 0)),
    ]
    args += [w1, w3, w2]
    grid_spec = pltpu.PrefetchScalarGridSpec(
        num_scalar_prefetch=2,
        grid=(n_tiles, n_f),
        in_specs=in_specs,
        out_specs=pl.BlockSpec((tm, d), lambda i, j, te, nv: (i, 0)),
        scratch_shapes=[pltpu.VMEM((tm, d), w1.dtype), pltpu.VMEM((tm, tf), w1.dtype),
                        pltpu.VMEM((tm, d), F32)],
    )
    return pl.pallas_call(
        functools.partial(_ffn_kernel, has_gain=gain is not None, has_res=res, tf=tf, ck=ck),
        out_shape=jax.ShapeDtypeStruct((p, d), F32),
        grid_spec=grid_spec,
        compiler_params=_cparams(("arbitrary", "arbitrary")),
        name="swiglu",
    )(tile_expert, n_valid, *args)


def _hgrn_gates(fz, loglb, l1mlb, oneml):
    e = jnp.exp(-jnp.abs(fz))
    ope = 1.0 + e
    log_sig = jnp.minimum(fz, 0.0) - jnp.log(ope)
    cc = l1mlb + log_sig
    log_f = jnp.maximum(loglb, cc) + jnp.log(1.0 + jnp.exp(-jnp.abs(loglb - cc)))
    k = oneml * (jnp.where(fz > 0.0, e, 1.0) / ope)
    return log_f, k


def _level_reference(b, w):
    n = b.shape[0]
    if w >= 4:
        return jnp.concatenate(
            [jnp.broadcast_to(b[j + w:j + w + 1, :], (2 * w, b.shape[1]))
             for j in range(0, n, 2 * w)], axis=0)
    b3 = b.reshape(n // SUBLANES, SUBLANES, b.shape[1])
    ahead = lambda s: pltpu.roll(b3, SUBLANES - s, 1)
    pos = lax.broadcasted_iota(I32, (1, SUBLANES, 1), 1) & (2 * w - 1)
    if w == 2:
        r3 = jnp.where(pos == 0, ahead(2),
                       jnp.where(pos == 1, ahead(1),
                                 jnp.where(pos == 2, b3, pltpu.roll(b3, 1, 1))))
    else:
        r3 = jnp.where(pos == 0, ahead(1), b3)
    return r3.reshape(b.shape)


def _lru_gates(xc, wra, bra, wix, bix, c8):
    r = _sigmoid(_dot(xc, wra) + bra)
    ig = _sigmoid(_dot(xc, wix) + bix)
    log_a = c8 * r
    a = jnp.exp(log_a)
    one_m_a2 = -jnp.tanh(log_a) * (a * a + 1.0)
    u = jnp.sqrt(one_m_a2) * (ig * xc)
    return a, u


def _mixer_block_kernel(x_ref, g1_ref, win_ref, lbp_ref, hgg_ref, cw_ref, cb_ref, wra_ref,
                        bra_ref, wix_ref, bix_ref, c8_ref, cm_ref, wout_ref,
                        xo_ref, hs_ref, ls_ref, cs_ref,
                        st_scr, h_scr, xpad_scr, hb_scr, q_scr, lf_scr, k_scr, v_scr, g_scr,
                        gl_scr, a_scr, u_scr, m_scr, *, tl):
    l = pl.program_id(1)
    nl = pl.num_programs(1)

    @pl.when(l == 0)
    def _():
        st_scr[...] = jnp.zeros_like(st_scr)
        h_scr[...] = jnp.zeros_like(h_scr)
        xpad_scr[0:8, :] = jnp.zeros((8, LRU_WIDTH), F32)

    hb_scr[...] = _rms(x_ref[0], g1_ref[...]).astype(BF16)
    hb = hb_scr[...]
    zg = lambda j: _dot(hb, win_ref[:, j * HG_WIDTH:(j + 1) * HG_WIDTH])
    q_scr[...] = _silu(zg(0))
    log_f, kk = _hgrn_gates(zg(1), lbp_ref[0:1, :], lbp_ref[1:2, :], lbp_ref[2:3, :])
    lf_scr[...] = log_f
    k_scr[...] = kk
    v_scr[...] = zg(2)
    g_scr[...] = _silu(zg(3))
    xpad_scr[8:8 + tl, :] = zg(4)
    gl_scr[...] = _gelu_tanh(zg(5))
    xc = cb_ref[...]
    for j in range(CONV_K):
        xc = xc + xpad_scr[5 + j:5 + j + tl, :] * cw_ref[j:j + 1, :]
    tail = xpad_scr[tl + 5:tl + 8, :]
    xpad_scr[5:8, :] = tail
    a, u = _lru_gates(xc, wra_ref[...], bra_ref[...], wix_ref[...], bix_ref[...], c8_ref[...])
    a_scr[...] = a
    u_scr[...] = u

    row = lax.broadcasted_iota(I32, (CHUNK, 1), 0)
    ti = lax.broadcasted_iota(I32, (CHUNK, CHUNK), 0)
    si = lax.broadcasted_iota(I32, (CHUNK, CHUNK), 1)
    t3 = lax.broadcasted_iota(I32, (1, DIAG, 1), 1)
    sub3 = lax.broadcasted_iota(I32, (1, SUBLANES, 1), 1)
    nblk = CHUNK // DIAG
    ngrp = CHUNK // SUBLANES

    def chunk_body(c, carry):
        r0 = pl.multiple_of(c * CHUNK, CHUNK)
        rows = pl.ds(r0, CHUNK)

        for gi in range(LRU_WIDTH // LANES):
            cs = slice(gi * LANES, (gi + 1) * LANES)
            aa = a_scr[rows, cs].reshape(ngrp, SUBLANES, LANES)
            uu = u_scr[rows, cs].reshape(ngrp, SUBLANES, LANES)
            d = 1
            while d < SUBLANES:
                keep = sub3 >= d
                a_sh = pltpu.roll(aa, d, 1)
                u_sh = pltpu.roll(uu, d, 1)
                uu = jnp.where(keep, aa * u_sh + uu, uu)
                aa = jnp.where(keep, aa * a_sh, aa)
                d *= 2
            hprev = h_scr[:, cs]
            groups = []
            for j in range(ngrp):
                hj = uu[j] + aa[j] * hprev
                groups.append(hj)
                hprev = hj[SUBLANES - 1:SUBLANES, :]
            hs = jnp.concatenate(groups, axis=0)
            h_scr[:, cs] = hprev
            m_scr[rows, HG_WIDTH + gi * LANES:HG_WIDTH + (gi + 1) * LANES] = (
                hs * gl_scr[rows, cs]).astype(BF16)

        q = q_scr[rows, :]
        k = k_scr[rows, :]
        v = v_scr[rows, :]
        f1, f2, f3 = _split3(lf_scr[rows, :])
        cm = cm_ref[...]
        b = _dot(cm, f1) + _dot(cm, f2) + _dot(cm, f3)

        for hh in range(HG_HEADS):
            cs = slice(hh * HG_D, (hh + 1) * HG_D)
            qh, kh, vh, bh = q[:, cs], k[:, cs], v[:, cs], b[:, cs]
            vb = vh.astype(BF16)
            amat = jnp.zeros((CHUNK, CHUNK), F32)
            w = CHUNK // 2
            while w >= DIAG:
                rw = _level_reference(bh, w)
                upper = (row & (2 * w - 1)) >= w
                dec = jnp.exp(-jnp.abs(bh - rw))
                qw = jnp.where(upper, qh * dec, 0.0).astype(BF16)
                kw = jnp.where(upper, 0.0, kh * dec).astype(BF16)
                aw = _dot_nt(qw, kw)
                if 2 * w < CHUNK:
                    sh = (2 * w).bit_length() - 1
                    aw = jnp.where((ti >> sh) == (si >> sh), aw, 0.0)
                amat = amat + aw
                w //= 2
            o = _dot(amat.astype(BF16), vb)
            st = st_scr[hh]
            o = o + _dot_nt((qh * jnp.exp(bh)).astype(BF16), st.astype(BF16))
            if DIAG == 1:
                o = o + jnp.sum(qh * kh, axis=-1, keepdims=True) * vh
            else:
                q3 = qh.reshape(nblk, DIAG, HG_D)
                k3 = kh.reshape(nblk, DIAG, HG_D)
                v3 = vh.reshape(nblk, DIAG, HG_D)
                b3 = bh.reshape(nblk, DIAG, HG_D)
                od = jnp.zeros((nblk, DIAG, HG_D), F32)
                for s in range(DIAG):
                    dec = jnp.exp(jnp.minimum(b3 - b3[:, s:s + 1, :], 0.0))
                    p = q3 * (k3[:, s:s + 1, :] * dec)
                    rs = jnp.sum(p, axis=-1, keepdims=True)
                    rs = jnp.where(t3 >= s, rs, 0.0)
                    od = od + rs * v3[:, s:s + 1, :]
                o = o + od.reshape(CHUNK, HG_D)
            bl = bh[CHUNK - 1:CHUNK, :]
            kdec = (kh * jnp.exp(bl - bh)).astype(BF16)
            st_scr[hh] = st * jnp.exp(bl) + _dot_tn(vb, kdec)
            o = o * lax.rsqrt(jnp.mean(o * o, axis=-1, keepdims=True) + EPS)
            o = o * hgg_ref[:, cs]
            o = o * g_scr[rows, cs]
            m_scr[rows, cs] = o.astype(BF16)
        return carry

    lax.fori_loop(0, tl // CHUNK, chunk_body, 0, unroll=4)
    xo_ref[0] = x_ref[0] + _dot(m_scr[...], wout_ref[...])

    @pl.when(l == nl - 1)
    def _():
        for hh in range(HG_HEADS):
            hs_ref[0, hh] = st_scr[hh].T
        ls_ref[0] = h_scr[...]
        cs_ref[0] = xpad_scr[5:8, :]


def _mixer_block(x, gain, w_in, w_out, layer, mp, tl=512):
    bsz, seq, d = x.shape
    tl = min(tl, seq)
    full = lambda shape: pl.BlockSpec(shape, lambda b, l: (0,) * len(shape))
    stacked = lambda shape: pl.BlockSpec((None,) + shape, lambda b, l: (layer, 0, 0))
    out_shapes = (
        jax.ShapeDtypeStruct((bsz, seq, d), F32),
        jax.ShapeDtypeStruct((bsz, HG_HEADS, HG_D, HG_D), F32),
        jax.ShapeDtypeStruct((bsz, 1, LRU_WIDTH), F32),
        jax.ShapeDtypeStruct((bsz, CONV_K - 1, LRU_WIDTH), F32),
    )
    half = lambda dt: pltpu.VMEM((tl, HG_WIDTH), dt)
    xo, hg, lru, conv = pl.pallas_call(
        functools.partial(_mixer_block_kernel, tl=tl),
        out_shape=out_shapes,
        grid=(bsz, seq // tl),
        in_specs=[
            pl.BlockSpec((1, tl, d), lambda b, l: (b, l, 0)),
            full((1, d)), stacked((d, IN_WIDTH)),
            full((3, HG_WIDTH)), full((1, HG_WIDTH)), full((CONV_K, LRU_WIDTH)),
            full((1, LRU_WIDTH)), full((LRU_WIDTH, LRU_WIDTH)), full((1, LRU_WIDTH)),
            full((LRU_WIDTH, LRU_WIDTH)), full((1, LRU_WIDTH)), full((1, LRU_WIDTH)),
            full(mp["cm"].shape), stacked((d, d)),
        ],
        out_specs=(
            pl.BlockSpec((1, tl, d), lambda b, l: (b, l, 0)),
            pl.BlockSpec((1, HG_HEADS, HG_D, HG_D), lambda b, l: (b, 0, 0, 0)),
            pl.BlockSpec((1, 1, LRU_WIDTH), lambda b, l: (b, 0, 0)),
            pl.BlockSpec((1, CONV_K - 1, LRU_WIDTH), lambda b, l: (b, 0, 0)),
        ),
        scratch_shapes=[
            pltpu.VMEM((HG_HEADS, HG_D, HG_D), F32),
            pltpu.VMEM((1, LRU_WIDTH), F32),
            pltpu.VMEM((tl + 8, LRU_WIDTH), F32),
            pltpu.VMEM((tl, d), BF16),
            half(F32), half(F32), half(F32), half(F32), half(F32), half(F32), half(F32),
            half(F32),
            pltpu.VMEM((tl, d), BF16),
        ],
        compiler_params=_cparams(("parallel", "arbitrary")),
        name="mixer_block",
    )(x, gain.reshape(1, d), w_in, mp["lbp"], mp["hgg"], mp["cw"], mp["cb"], mp["wra"],
      mp["bra"], mp["wix"], mp["bix"], mp["c8"], mp["cm"], w_out)
    return xo, hg, lru.reshape(bsz, LRU_WIDTH), conv


def _mixer_step_kernel(z_ref, sh_ref, sl_ref, sc_ref, lbp_ref, hgg_ref, cw_ref, cb_ref, wra_ref,
                       bra_ref, wix_ref, bix_ref, c8_ref, *rest, bb, aliased):
    m_ref, hn_ref, ln_ref, cn_ref, o_scr = rest[1:] if aliased else rest
    z = z_ref[...]
    xr = z[:, 4 * HG_WIDTH:4 * HG_WIDTH + LRU_WIDTH]
    gr = z[:, 4 * HG_WIDTH + LRU_WIDTH:IN_WIDTH]
    buf = sc_ref[...]
    xc = cb_ref[...]
    for j in range(CONV_K - 1):
        xc = xc + buf[:, j * LRU_WIDTH:(j + 1) * LRU_WIDTH] * cw_ref[j:j + 1, :]
    xc = xc + xr * cw_ref[CONV_K - 1:CONV_K, :]
    cn_ref[:, 0:(CONV_K - 2) * LRU_WIDTH] = buf[:, LRU_WIDTH:(CONV_K - 1) * LRU_WIDTH]
    cn_ref[:, (CONV_K - 2) * LRU_WIDTH:(CONV_K - 1) * LRU_WIDTH] = xr
    a, u = _lru_gates(xc, wra_ref[...], bra_ref[...], wix_ref[...], bix_ref[...], c8_ref[...])
    hnew = u + a * sl_ref[...]
    ln_ref[...] = hnew
    m_ref[:, HG_WIDTH:D_MODEL] = (hnew * _gelu_tanh(gr)).astype(m_ref.dtype)

    q = _silu(z[:, 0:HG_WIDTH])
    log_f, k = _hgrn_gates(z[:, HG_WIDTH:2 * HG_WIDTH], lbp_ref[0:1, :], lbp_ref[1:2, :],
                           lbp_ref[2:3, :])
    f = jnp.exp(log_f)
    v = z[:, 2 * HG_WIDTH:3 * HG_WIDTH]
    gate = _silu(z[:, 3 * HG_WIDTH:4 * HG_WIDTH])
    zpad = jnp.zeros((HG_D - bb, HG_D), F32)
    for hh in range(HG_HEADS):
        cs = slice(hh * HG_D, (hh + 1) * HG_D)
        ft = jnp.concatenate([f[:, cs], zpad], axis=0).T
        kt = jnp.concatenate([k[:, cs], zpad], axis=0).T
        qt = jnp.concatenate([q[:, cs], zpad], axis=0).T
        for j in range(bb):
            fcol = jnp.broadcast_to(ft[:, j:j + 1], (HG_D, HG_D))
            kcol = jnp.broadcast_to(kt[:, j:j + 1], (HG_D, HG_D))
            qcol = jnp.broadcast_to(qt[:, j:j + 1], (HG_D, HG_D))
            vrow = jnp.broadcast_to(v[j:j + 1, cs], (HG_D, HG_D))
            sn = fcol * sh_ref[j, hh] + kcol * vrow
            hn_ref[j, hh] = sn
            o_scr[j:j + 1, cs] = jnp.sum(qcol * sn, axis=0, keepdims=True)
    for hh in range(HG_HEADS):
        cs = slice(hh * HG_D, (hh + 1) * HG_D)
        o = o_scr[:, cs]
        o = o * lax.rsqrt(jnp.mean(o * o, axis=-1, keepdims=True) + EPS)
        o = o * hgg_ref[:, cs]
        o = o * gate[:, cs]
        m_ref[:, cs] = o.astype(m_ref.dtype)


def _mixer_step(z, s_hg_all, layer, s_lru, s_conv, mp, hg_buf=None, bb=16):
    bsz = z.shape[0]
    bb = min(bb, bsz)
    cw3 = (CONV_K - 1) * LRU_WIDTH
    full = lambda shape: pl.BlockSpec(shape, lambda i: (0,) * len(shape))
    state_spec = pl.BlockSpec((None, bb, HG_HEADS, HG_D, HG_D), lambda i: (layer, i, 0, 0, 0))
    out_shapes = (
        jax.ShapeDtypeStruct((bsz, D_MODEL), mp["wra"].dtype),
        jax.ShapeDtypeStruct(s_hg_all.shape, F32),
        jax.ShapeDtypeStruct((bsz, LRU_WIDTH), F32),
        jax.ShapeDtypeStruct((bsz, cw3), F32),
    )
    in_specs = [
        pl.BlockSpec((bb, IN_WIDTH), lambda i: (i, 0)),
        state_spec,
        pl.BlockSpec((bb, LRU_WIDTH), lambda i: (i, 0)),
        pl.BlockSpec((bb, cw3), lambda i: (i, 0)),
        full((3, HG_WIDTH)), full((1, HG_WIDTH)), full((CONV_K, LRU_WIDTH)),
        full((1, LRU_WIDTH)), full((LRU_WIDTH, LRU_WIDTH)), full((1, LRU_WIDTH)),
        full((LRU_WIDTH, LRU_WIDTH)), full((1, LRU_WIDTH)), full((1, LRU_WIDTH)),
    ]
    args = [z, s_hg_all, s_lru, s_conv.reshape(bsz, cw3), mp["lbp"], mp["hgg"], mp["cw"],
            mp["cb"], mp["wra"], mp["bra"], mp["wix"], mp["bix"], mp["c8"]]
    aliases = {}
    if hg_buf is not None:
        aliases = {len(args): 1}
        in_specs.append(pl.BlockSpec(memory_space=pl.ANY))
        args.append(hg_buf)
    merged, hg, lru, conv = pl.pallas_call(
        functools.partial(_mixer_step_kernel, bb=bb, aliased=hg_buf is not None),
        out_shape=out_shapes,
        grid=(bsz // bb,),
        in_specs=in_specs,
        out_specs=(
            pl.BlockSpec((bb, D_MODEL), lambda i: (i, 0)),
            state_spec,
            pl.BlockSpec((bb, LRU_WIDTH), lambda i: (i, 0)),
            pl.BlockSpec((bb, cw3), lambda i: (i, 0)),
        ),
        scratch_shapes=[pltpu.VMEM((bb, HG_WIDTH), F32)],
        input_output_aliases=aliases,
        compiler_params=_cparams(("parallel",)),
        name="mixer_step",
    )(*args)
    return merged, hg, lru, conv.reshape(bsz, CONV_K - 1, LRU_WIDTH)


def _router_kernel(x_ref, g_ref, rw_ref, tril_ref, info_ref, cnt_ref, carry_scr):
    i = pl.program_id(0)

    @pl.when(i == 0)
    def _():
        carry_scr[...] = jnp.zeros_like(carry_scr)

    h = _rms(x_ref[...], g_ref[...])
    logits = jnp.dot(h, rw_ref[...], preferred_element_type=F32, precision=lax.Precision.HIGHEST)
    lane = lax.broadcasted_iota(I32, logits.shape, 1)
    neg = jnp.float32(-jnp.inf)
    logits = jnp.where(lane < N_EXPERTS, logits, neg)
    m1 = jnp.max(logits, axis=-1, keepdims=True)
    i1 = jnp.min(jnp.where(logits == m1, lane, LANES), axis=-1, keepdims=True)
    l2 = jnp.where(lane == i1, neg, logits)
    m2 = jnp.max(l2, axis=-1, keepdims=True)
    i2 = jnp.min(jnp.where(l2 == m2, lane, LANES), axis=-1, keepdims=True)
    e = jnp.exp(m2 - m1)
    p1 = 1.0 / (1.0 + e)
    p2 = e / (1.0 + e)
    oh1 = (lane == i1).astype(F32)
    oh2 = (lane == i2).astype(F32)
    sel = oh1 + oh2
    before = _dot(tril_ref[...], sel.astype(BF16)) + carry_scr[...]
    r1 = jnp.sum(oh1 * before, axis=-1, keepdims=True)
    r2 = jnp.sum(oh2 * before, axis=-1, keepdims=True)
    carry_scr[...] += jnp.sum(sel, axis=0, keepdims=True)
    info = jnp.where(lane == 0, i1.astype(F32), 0.0)
    info = jnp.where(lane == 1, i2.astype(F32), info)
    info = jnp.where(lane == 2, r1, info)
    info = jnp.where(lane == 3, r2, info)
    info = jnp.where(lane == 4, p1, info)
    info = jnp.where(lane == 5, p2, info)
    info_ref[...] = info
    cnt_ref[...] = carry_scr[...]


def _router(x, gain, rw_pad, tm=512):
    t, d = x.shape
    tm = min(tm, t)
    tril = jnp.tril(jnp.ones((tm, tm), BF16), -1)
    return pl.pallas_call(
        _router_kernel,
        out_shape=(jax.ShapeDtypeStruct((t, LANES), F32),
                   jax.ShapeDtypeStruct((1, LANES), F32)),
        grid=(t // tm,),
        in_specs=[
            pl.BlockSpec((tm, d), lambda i: (i, 0)),
            pl.BlockSpec((1, d), lambda i: (0, 0)),
            pl.BlockSpec((d, LANES), lambda i: (0, 0)),
            pl.BlockSpec((tm, tm), lambda i: (0, 0)),
        ],
        out_specs=(pl.BlockSpec((tm, LANES), lambda i: (i, 0)),
                   pl.BlockSpec((1, LANES), lambda i: (0, 0))),
        scratch_shapes=[pltpu.VMEM((1, LANES), F32)],
        compiler_params=_cparams(("arbitrary",)),
        name="router",
    )(x, gain.reshape(1, d), rw_pad, tril)


def _row_copy(src, dst, s, d, sem):
    return pltpu.make_async_copy(src.at[pl.ds(s, 1)], dst.at[pl.ds(d, 1)], sem)


def _tile_dest(dest, t, td):
    nb = t // td
    dest3 = jnp.concatenate([dest[0].reshape(nb, td), dest[1].reshape(nb, td)], axis=1)
    return dest3.reshape(nb, 1, 2 * td)


def _dispatch_kernel(dest_ref, x_ref, g_ref, xs_in_ref, xs_ref, pk_scr, sem, *, td):
    del xs_in_ref
    pk_scr[...] = _rms(x_ref[...], g_ref[...])

    def issue(r, c):
        _row_copy(pk_scr, xs_ref, r, dest_ref[0, 0, r], sem).start(priority=0)
        _row_copy(pk_scr, xs_ref, r, dest_ref[0, 0, td + r], sem).start(priority=1)
        return c

    lax.fori_loop(0, td, issue, 0, unroll=8)
    for _ in range(2):
        pltpu.make_async_copy(pk_scr, xs_ref.at[pl.ds(0, td)], sem).wait()


def _dispatch(x, gain, dest, n_rows, td=512):
    t, d = x.shape
    td = min(td, t)
    xs0 = jnp.zeros((n_rows, d), F32)
    return pl.pallas_call(
        functools.partial(_dispatch_kernel, td=td),
        out_shape=jax.ShapeDtypeStruct((n_rows, d), F32),
        grid=(t // td,),
        in_specs=[
            pl.BlockSpec((1, 1, 2 * td), lambda i: (i, 0, 0), memory_space=pltpu.SMEM),
            pl.BlockSpec((td, d), lambda i: (i, 0)),
            pl.BlockSpec((1, d), lambda i: (0, 0)),
            pl.BlockSpec(memory_space=pl.ANY),
        ],
        out_specs=pl.BlockSpec(memory_space=pl.ANY),
        scratch_shapes=[pltpu.VMEM((td, d), F32), pltpu.SemaphoreType.DMA(())],
        input_output_aliases={3: 0},
        compiler_params=pltpu.CompilerParams(dimension_semantics=("arbitrary",),
                                             vmem_limit_bytes=VMEM_LIMIT_BYTES,
                                             has_side_effects=True),
        name="dispatch",
    )(_tile_dest(dest, t, td), x, gain.reshape(1, d), xs0)


def _combine_kernel(dest_ref, x_ref, info_ref, g_ref, ys_ref, o_ref, y1_scr, y2_scr, sem, *, td):
    def issue(r, c):
        _row_copy(ys_ref, y1_scr, dest_ref[0, 0, r], r, sem).start(priority=0)
        _row_copy(ys_ref, y2_scr, dest_ref[0, 0, td + r], r, sem).start(priority=1)
        return c

    lax.fori_loop(0, td, issue, 0, unroll=8)
    pltpu.make_async_copy(ys_ref.at[pl.ds(0, td)], y1_scr, sem).wait()
    pltpu.make_async_copy(ys_ref.at[pl.ds(0, td)], y2_scr, sem).wait()
    p1 = info_ref[:, 4:5]
    p2 = info_ref[:, 5:6]
    x = x_ref[...] + (p1 * y1_scr[...] + p2 * y2_scr[...])
    o_ref[...] = _rms(x, g_ref[...])


def _combine(x, ys, dest, info, gain, td=256):
    t, d = x.shape
    td = min(td, t)
    row = pl.BlockSpec((td, d), lambda i: (i, 0))
    return pl.pallas_call(
        functools.partial(_combine_kernel, td=td),
        out_shape=jax.ShapeDtypeStruct((t, d), F32),
        grid=(t // td,),
        in_specs=[
            pl.BlockSpec((1, 1, 2 * td), lambda i: (i, 0, 0), memory_space=pltpu.SMEM),
            row,
            pl.BlockSpec((td, LANES), lambda i: (i, 0)),
            pl.BlockSpec((1, d), lambda i: (0, 0)),
            pl.BlockSpec(memory_space=pl.ANY),
        ],
        out_specs=row,
        scratch_shapes=[pltpu.VMEM((td, d), F32), pltpu.VMEM((td, d), F32),
                        pltpu.SemaphoreType.DMA(())],
        compiler_params=_cparams(("arbitrary",)),
        name="combine",
    )(_tile_dest(dest, t, td), x, info, gain.reshape(1, d), ys)


def _moe(x, norm_g, rw_pad, w1, w3, w2, final_g, tm_e):
    t, d = x.shape
    info, cnt = _router(x, norm_g, rw_pad)
    e1 = info[:, 0].astype(I32)
    e2 = info[:, 1].astype(I32)
    r1 = info[:, 2].astype(I32)
    r2 = info[:, 3].astype(I32)
    counts = cnt[0, :N_EXPERTS].astype(I32)
    padded = ((counts + tm_e - 1) // tm_e) * tm_e
    ends = jnp.cumsum(padded)
    starts = ends - padded
    dest = jnp.stack([starts[e1] + r1, starts[e2] + r2])
    n_tiles = (2 * t + N_EXPERTS * (tm_e - 1)) // tm_e
    n_rows = n_tiles * tm_e
    tile_start = jnp.arange(n_tiles, dtype=I32) * tm_e
    tile_expert = jnp.minimum(
        jnp.sum((tile_start[:, None] >= ends[None, :]).astype(I32), axis=1), N_EXPERTS - 1)
    n_valid = (ends[-1] // tm_e).reshape(1).astype(I32)
    xs = _dispatch(x, norm_g, dest, n_rows)
    ys = _ffn(xs, w1, w3, w2, tile_expert.astype(I32), n_valid, tm=tm_e, n_f=2)
    return _combine(x, ys, dest, info, final_g)


def _block_diag(w):
    n, c, d = w.shape
    eye = jnp.eye(n, dtype=w.dtype)
    return (w[:, :, None, :] * eye[:, None, :, None]).reshape(n * c, n * d)


def _cum_matrix():
    t = jnp.arange(CHUNK)
    return (t[None, :] <= t[:, None]).astype(BF16)


def _mixer_params(l, lb, hg_norm_g, conv_w, conv_b, w_ra, b_ra, w_ix, b_ix, lru_lambda, wdtype):
    row = lambda a: a.reshape(1, -1).astype(F32)
    return {
        "lbp": jnp.stack([jnp.log(lb), jnp.log1p(-lb), 1.0 - lb]).astype(F32),
        "hgg": row(hg_norm_g[l]),
        "cw": conv_w[l].astype(F32),
        "cb": row(conv_b[l]),
        "wra": _block_diag(w_ra[l].astype(F32)).astype(wdtype),
        "bra": row(b_ra[l]),
        "wix": _block_diag(w_ix[l].astype(F32)).astype(wdtype),
        "bix": row(b_ix[l]),
        "c8": row(-LRU_C * jax.nn.softplus(-lru_lambda[l].astype(F32))),
        "cm": _cum_matrix(),
    }


def _trunk(x, seq_shape, states, prm):
    bsz, seq = seq_shape
    one = jnp.ones((1,), I32)
    new_hg, new_lru, new_conv = [], [], []
    hg_stack = None if states is None else states[0]
    for l in range(2):
        mp = prm["mixer"][l]
        if states is None:
            x, s1, s2, s3 = _mixer_block(x.reshape(bsz, seq, D_MODEL), prm["norm1_g"][l],
                                         prm["w_in"], prm["w_out"], l, mp)
            x = x.reshape(bsz * seq, D_MODEL)
            new_hg.append(s1)
        else:
            z = _linear(x, prm["w_in"], l, gain=prm["norm1_g"][l])
            merged, hg_stack, s2, s3 = _mixer_step(z, states[0], l, states[1][l], states[2][l],
                                                   mp, hg_buf=hg_stack)
            x = _linear(merged, prm["w_out"], l, res=x)
        new_lru.append(s2)
        new_conv.append(s3)
        if l == 0:
            n_tiles = max(x.shape[0] // 512, 1)
            n_f = 1 if prm["ffn_w1"].dtype == BF16 else prm["ffn_w1"].shape[2] // 256
            x = _ffn(x, prm["ffn_w1"], prm["ffn_w3"], prm["ffn_w2"], jnp.zeros((n_tiles,), I32),
                     one * n_tiles, gain=prm["norm2_g"][0], res=True, tm=512, n_f=n_f)
        else:
            x = _moe(x, prm["norm2_g"][1], prm["rw_pad"], prm["moe_w1"], prm["moe_w3"],
                     prm["moe_w2"], prm["final_norm_g"], tm_e=512 if states is None else 128)
    hg_all = jnp.stack(new_hg) if states is None else hg_stack
    return x, hg_all, jnp.stack(new_lru), jnp.stack(new_conv)


def kernel(x_prompt, x_sample, state_hgrn, state_lru, state_conv, norm1_g, w_in, lower_bounds,
           hg_norm_g, conv_w, conv_b, w_ra, b_ra, w_ix, b_ix, lru_lambda, w_out, norm2_g,
           ffn_w1, ffn_w3, ffn_w2, router_w, moe_w1, moe_w3, moe_w2, final_norm_g):
    lb_all = jnp.cumsum(jax.nn.softmax(lower_bounds.astype(F32), axis=0), axis=0)
    lb_all = lb_all - lb_all[0]
    experts = {
        "moe_w1": moe_w1[0].astype(BF16), "moe_w3": moe_w3[0].astype(BF16),
        "moe_w2": moe_w2[0].astype(BF16),
        "rw_pad": jnp.pad(router_w[0].astype(F32), ((0, 0), (0, LANES - N_EXPERTS))),
        "norm1_g": norm1_g, "norm2_g": norm2_g, "final_norm_g": final_norm_g,
    }

    def params(wdtype):
        return dict(
            experts,
            mixer=[_mixer_params(l, lb_all[l], hg_norm_g, conv_w, conv_b, w_ra, b_ra, w_ix, b_ix,
                                 lru_lambda, wdtype) for l in range(2)],
            w_in=w_in.astype(wdtype), w_out=w_out.astype(wdtype),
            ffn_w1=ffn_w1.astype(wdtype), ffn_w3=ffn_w3.astype(wdtype),
            ffn_w2=ffn_w2.astype(wdtype))

    bp, sp, d = x_prompt.shape
    bs = x_sample.shape[0]
    y_p, hg_p, lru_p, conv_p = _trunk(x_prompt.reshape(bp * sp, d), (bp, sp), None, params(BF16))
    y_s, hg_s, lru_s, conv_s = _trunk(x_sample.reshape(bs, d), (bs, 1),
                                      (state_hgrn, state_lru, state_conv), params(F32))
    return (y_p.reshape(bp, sp, d), y_s.reshape(bs, 1, d), hg_p, lru_p, conv_p, hg_s, lru_s,
            conv_s)
```

```python
import functools

import jax
import jax.numpy as jnp
from jax import lax
from jax.experimental import pallas as pl
from jax.experimental.pallas import tpu as pltpu

F32 = jnp.float32
BF16 = jnp.bfloat16
I32 = jnp.int32

D_MODEL = 1024
HG_WIDTH = 512
HG_HEADS = 4
HG_D = 128
LRU_WIDTH = 512
LRU_BLOCKS = 8
LRU_C = 8.0
CONV_K = 4
IN_WIDTH = 4 * HG_WIDTH + 2 * LRU_WIDTH
N_EXPERTS = 8
EPS = 1e-6

CHUNK = 64
DIAG = 1
LANES = 128
SUBLANES = 8
VMEM_LIMIT_BYTES = 56 * 1024 * 1024


def _cparams(sem):
    return pltpu.CompilerParams(dimension_semantics=sem, vmem_limit_bytes=VMEM_LIMIT_BYTES)


def _sigmoid(x):
    return 1.0 / (1.0 + jnp.exp(-x))


def _silu(x):
    return x * _sigmoid(x)


def _gelu_tanh(x):
    c = 0.7978845608028654
    return 0.5 * x * (1.0 + jnp.tanh(c * (x + 0.044715 * (x * x * x))))


def _rms(x, g):
    return x * lax.rsqrt(jnp.mean(x * x, axis=-1, keepdims=True) + EPS) * g


def _dot(a, b):
    if b.dtype == F32:
        return jnp.dot(a.astype(F32), b, preferred_element_type=F32,
                       precision=lax.Precision.HIGHEST)
    return jnp.dot(a.astype(BF16), b, preferred_element_type=F32)


def _dot_nt(a, b):
    return lax.dot_general(a, b, (((1,), (1,)), ((), ())), preferred_element_type=F32)


def _dot_tn(a, b):
    return lax.dot_general(a, b, (((0,), (0,)), ((), ())), preferred_element_type=F32)


def _split3(x):
    x1 = x.astype(BF16)
    r1 = x - x1.astype(F32)
    x2 = r1.astype(BF16)
    x3 = (r1 - x2.astype(F32)).astype(BF16)
    return x1, x2, x3


def _linear_kernel(*refs, has_gain, has_res):
    it = iter(refs)
    x_ref = next(it)
    g_ref = next(it) if has_gain else None
    w_ref = next(it)
    r_ref = next(it) if has_res else None
    o_ref = next(it)
    h_scr = next(it)

    @pl.when(pl.program_id(1) == 0)
    def _():
        x = x_ref[...].astype(F32)
        if has_gain:
            x = _rms(x, g_ref[...])
        h_scr[...] = x.astype(h_scr.dtype)

    acc = _dot(h_scr[...], w_ref[...])
    if has_res:
        acc = acc + r_ref[...]
    o_ref[...] = acc.astype(o_ref.dtype)


def _linear(x, w, layer, gain=None, res=None, tm=1024, tn=1024):
    m, k = x.shape
    n = w.shape[2]
    tm = min(tm, m)
    tn = min(tn, n)
    in_specs = [pl.BlockSpec((tm, k), lambda i, j: (i, 0))]
    args = [x]
    if gain is not None:
        in_specs.append(pl.BlockSpec((1, k), lambda i, j: (0, 0)))
        args.append(gain.reshape(1, k))
    in_specs.append(pl.BlockSpec((None, k, tn), lambda i, j: (layer, 0, j)))
    args.append(w)
    if res is not None:
        in_specs.append(pl.BlockSpec((tm, tn), lambda i, j: (i, j)))
        args.append(res)
    return pl.pallas_call(
        functools.partial(_linear_kernel, has_gain=gain is not None, has_res=res is not None),
        out_shape=jax.ShapeDtypeStruct((m, n), F32),
        grid=(m // tm, n // tn),
        in_specs=in_specs,
        out_specs=pl.BlockSpec((tm, tn), lambda i, j: (i, j)),
        scratch_shapes=[pltpu.VMEM((tm, k), w.dtype)],
        compiler_params=_cparams(("parallel", "arbitrary")),
        name="linear",
    )(*args)


def _ffn_kernel(te_ref, nv_ref, *refs, has_gain, has_res, tf, ck):
    del te_ref
    it = iter(refs)
    x_ref = next(it)
    g_ref = next(it) if has_gain else None
    w1_ref, w3_ref, w2_ref, o_ref, xb_scr, a_scr, acc_scr = it
    i = pl.program_id(0)
    j = pl.program_id(1)
    nj = pl.num_programs(1)
    valid = i < nv_ref[0]

    @pl.when(valid)
    def _():
        @pl.when(j == 0)
        def _():
            if len(x_ref.shape) == 3:
                for c in range(SUBLANES):
                    xb_scr[:, c * LANES:(c + 1) * LANES] = x_ref[:, c, :].astype(xb_scr.dtype)
            else:
                x = x_ref[...]
                if has_gain:
                    x = _rms(x, g_ref[...])
                xb_scr[...] = x.astype(xb_scr.dtype)

        xb = xb_scr[...]
        for c in range(tf // ck):
            h1 = _dot(xb, w1_ref[0, :, c * ck:(c + 1) * ck])
            h3 = _dot(xb, w3_ref[0, :, c * ck:(c + 1) * ck])
            a_scr[:, c * ck:(c + 1) * ck] = (_silu(h1) * h3).astype(a_scr.dtype)
        part = _dot(a_scr[...], w2_ref[0])

        @pl.when(j == 0)
        def _():
            acc_scr[...] = part

        @pl.when(j > 0)
        def _():
            acc_scr[...] += part

        @pl.when(j == nj - 1)
        def _():
            out = acc_scr[...]
            if has_res:
                out = x_ref[...] + out
            if len(o_ref.shape) == 3:
                for c in range(SUBLANES):
                    o_ref[:, c, :] = out[:, c * LANES:(c + 1) * LANES]
            else:
                o_ref[...] = out

    @pl.when(jnp.logical_and(jnp.logical_not(valid), j == nj - 1))
    def _():
        o_ref[...] = jnp.zeros_like(o_ref)


def _ffn(x, w1, w3, w2, tile_expert, n_valid, gain=None, res=False, tm=512, n_f=2, ck=256):
    p = x.shape[0]
    d = w1.shape[1]
    f = w1.shape[2]
    tm = min(tm, p)
    tf = f // n_f
    n_tiles = p // tm
    act_spec = pl.BlockSpec((tm,) + x.shape[1:],
                            lambda i, j, te, nv: (i,) + (0,) * (len(x.shape) - 1))
    in_specs = [act_spec]
    args = [x]
    if gain is not None:
        in_specs.append(pl.BlockSpec((1, d), lambda i, j, te, nv: (0, 0)))
        args.append(gain.reshape(1, d))
    in_specs += [
        pl.BlockSpec((1, d, tf), lambda i, j, te, nv: (te[i], 0, j)),
        pl.BlockSpec((1, d, tf), lambda i, j, te, nv: (te[i], 0, j)),
        pl.BlockSpec((1, tf, d), lambda i, j, te, nv: (te[i], j, 0)),
    ]
    args += [w1, w3, w2]
    grid_spec = pltpu.PrefetchScalarGridSpec(
        num_scalar_prefetch=2,
        grid=(n_tiles, n_f),
        in_specs=in_specs,
        out_specs=act_spec,
        scratch_shapes=[pltpu.VMEM((tm, d), w1.dtype), pltpu.VMEM((tm, tf), w1.dtype),
                        pltpu.VMEM((tm, d), F32)],
    )
    return pl.pallas_call(
        functools.partial(_ffn_kernel, has_gain=gain is not None, has_res=res, tf=tf, ck=ck),
        out_shape=jax.ShapeDtypeStruct(x.shape, F32),
        grid_spec=grid_spec,
        compiler_params=_cparams(("arbitrary", "arbitrary")),
        name="swiglu",
    )(tile_expert, n_valid, *args)


def _hgrn_gates(fz, loglb, l1mlb, oneml):
    e = jnp.exp(-jnp.abs(fz))
    ope = 1.0 + e
    log_sig = jnp.minimum(fz, 0.0) - jnp.log(ope)
    cc = l1mlb + log_sig
    log_f = jnp.maximum(loglb, cc) + jnp.log(1.0 + jnp.exp(-jnp.abs(loglb - cc)))
    k = oneml * (jnp.where(fz > 0.0, e, 1.0) / ope)
    return log_f, k


def _level_reference(b, w):
    n = b.shape[0]
    if w >= 4:
        return jnp.concatenate(
            [jnp.broadcast_to(b[j + w:j + w + 1, :], (2 * w, b.shape[1]))
             for j in range(0, n, 2 * w)], axis=0)
    b3 = b.reshape(n // SUBLANES, SUBLANES, b.shape[1])
    ahead = lambda s: pltpu.roll(b3, SUBLANES - s, 1)
    pos = lax.broadcasted_iota(I32, (1, SUBLANES, 1), 1) & (2 * w - 1)
    if w == 2:
        r3 = jnp.where(pos == 0, ahead(2),
                       jnp.where(pos == 1, ahead(1),
                                 jnp.where(pos == 2, b3, pltpu.roll(b3, 1, 1))))
    else:
        r3 = jnp.where(pos == 0, ahead(1), b3)
    return r3.reshape(b.shape)


def _lru_gates(xc, wra, bra, wix, bix, c8):
    r = _sigmoid(_dot(xc, wra) + bra)
    ig = _sigmoid(_dot(xc, wix) + bix)
    log_a = c8 * r
    a = jnp.exp(log_a)
    one_m_a2 = -jnp.tanh(log_a) * (a * a + 1.0)
    u = jnp.sqrt(one_m_a2) * (ig * xc)
    return a, u


def _mixer_block_kernel(x_ref, g1_ref, win_ref, lbp_ref, hgg_ref, cw_ref, cb_ref, wra_ref,
                        bra_ref, wix_ref, bix_ref, c8_ref, cm_ref, wout_ref,
                        xo_ref, hs_ref, ls_ref, cs_ref,
                        st_scr, h_scr, xpad_scr, hb_scr, q_scr, lf_scr, k_scr, v_scr, g_scr,
                        gl_scr, a_scr, u_scr, m_scr, *, tl):
    l = pl.program_id(1)
    nl = pl.num_programs(1)

    @pl.when(l == 0)
    def _():
        st_scr[...] = jnp.zeros_like(st_scr)
        h_scr[...] = jnp.zeros_like(h_scr)
        xpad_scr[0:8, :] = jnp.zeros((8, LRU_WIDTH), F32)

    hb_scr[...] = _rms(x_ref[0], g1_ref[...]).astype(BF16)
    hb = hb_scr[...]
    zg = lambda j: _dot(hb, win_ref[:, j * HG_WIDTH:(j + 1) * HG_WIDTH])
    q_scr[...] = _silu(zg(0))
    log_f, kk = _hgrn_gates(zg(1), lbp_ref[0:1, :], lbp_ref[1:2, :], lbp_ref[2:3, :])
    lf_scr[...] = log_f
    k_scr[...] = kk
    v_scr[...] = zg(2)
    g_scr[...] = _silu(zg(3))
    xpad_scr[8:8 + tl, :] = zg(4)
    gl_scr[...] = _gelu_tanh(zg(5))
    xc = cb_ref[...]
    for j in range(CONV_K):
        xc = xc + xpad_scr[5 + j:5 + j + tl, :] * cw_ref[j:j + 1, :]
    tail = xpad_scr[tl + 5:tl + 8, :]
    xpad_scr[5:8, :] = tail
    a, u = _lru_gates(xc, wra_ref[...], bra_ref[...], wix_ref[...], bix_ref[...], c8_ref[...])
    a_scr[...] = a
    u_scr[...] = u

    row = lax.broadcasted_iota(I32, (CHUNK, 1), 0)
    ti = lax.broadcasted_iota(I32, (CHUNK, CHUNK), 0)
    si = lax.broadcasted_iota(I32, (CHUNK, CHUNK), 1)
    t3 = lax.broadcasted_iota(I32, (1, DIAG, 1), 1)
    sub3 = lax.broadcasted_iota(I32, (1, SUBLANES, 1), 1)
    nblk = CHUNK // DIAG
    ngrp = CHUNK // SUBLANES

    def chunk_body(c, carry):
        r0 = pl.multiple_of(c * CHUNK, CHUNK)
        rows = pl.ds(r0, CHUNK)

        for gi in range(LRU_WIDTH // LANES):
            cs = slice(gi * LANES, (gi + 1) * LANES)
            aa = a_scr[rows, cs].reshape(ngrp, SUBLANES, LANES)
            uu = u_scr[rows, cs].reshape(ngrp, SUBLANES, LANES)
            d = 1
            while d < SUBLANES:
                keep = sub3 >= d
                a_sh = pltpu.roll(aa, d, 1)
                u_sh = pltpu.roll(uu, d, 1)
                uu = jnp.where(keep, aa * u_sh + uu, uu)
                aa = jnp.where(keep, aa * a_sh, aa)
                d *= 2
            hprev = h_scr[:, cs]
            groups = []
            for j in range(ngrp):
                hj = uu[j] + aa[j] * hprev
                groups.append(hj)
                hprev = hj[SUBLANES - 1:SUBLANES, :]
            hs = jnp.concatenate(groups, axis=0)
            h_scr[:, cs] = hprev
            m_scr[rows, HG_WIDTH + gi * LANES:HG_WIDTH + (gi + 1) * LANES] = (
                hs * gl_scr[rows, cs]).astype(BF16)

        q = q_scr[rows, :]
        k = k_scr[rows, :]
        v = v_scr[rows, :]
        f1, f2, f3 = _split3(lf_scr[rows, :])
        cm = cm_ref[...]
        b = _dot(cm, f1) + _dot(cm, f2) + _dot(cm, f3)

        for hh in range(HG_HEADS):
            cs = slice(hh * HG_D, (hh + 1) * HG_D)
            qh, kh, vh, bh = q[:, cs], k[:, cs], v[:, cs], b[:, cs]
            vb = vh.astype(BF16)
            amat = jnp.zeros((CHUNK, CHUNK), F32)
            w = CHUNK // 2
            while w >= DIAG:
                rw = _level_reference(bh, w)
                upper = (row & (2 * w - 1)) >= w
                dec = jnp.exp(-jnp.abs(bh - rw))
                qw = jnp.where(upper, qh * dec, 0.0).astype(BF16)
                kw = jnp.where(upper, 0.0, kh * dec).astype(BF16)
                aw = _dot_nt(qw, kw)
                if 2 * w < CHUNK:
                    sh = (2 * w).bit_length() - 1
                    aw = jnp.where((ti >> sh) == (si >> sh), aw, 0.0)
                amat = amat + aw
                w //= 2
            o = _dot(amat.astype(BF16), vb)
            st = st_scr[hh]
            o = o + _dot_nt((qh * jnp.exp(bh)).astype(BF16), st.astype(BF16))
            if DIAG == 1:
                o = o + jnp.sum(qh * kh, axis=-1, keepdims=True) * vh
            else:
                q3 = qh.reshape(nblk, DIAG, HG_D)
                k3 = kh.reshape(nblk, DIAG, HG_D)
                v3 = vh.reshape(nblk, DIAG, HG_D)
                b3 = bh.reshape(nblk, DIAG, HG_D)
                od = jnp.zeros((nblk, DIAG, HG_D), F32)
                for s in range(DIAG):
                    dec = jnp.exp(jnp.minimum(b3 - b3[:, s:s + 1, :], 0.0))
                    p = q3 * (k3[:, s:s + 1, :] * dec)
                    rs = jnp.sum(p, axis=-1, keepdims=True)
                    rs = jnp.where(t3 >= s, rs, 0.0)
                    od = od + rs * v3[:, s:s + 1, :]
                o = o + od.reshape(CHUNK, HG_D)
            bl = bh[CHUNK - 1:CHUNK, :]
            kdec = (kh * jnp.exp(bl - bh)).astype(BF16)
            st_scr[hh] = st * jnp.exp(bl) + _dot_tn(vb, kdec)
            o = o * lax.rsqrt(jnp.mean(o * o, axis=-1, keepdims=True) + EPS)
            o = o * hgg_ref[:, cs]
            o = o * g_scr[rows, cs]
            m_scr[rows, cs] = o.astype(BF16)
        return carry

    lax.fori_loop(0, tl // CHUNK, chunk_body, 0, unroll=4)
    xo_ref[0] = x_ref[0] + _dot(m_scr[...], wout_ref[...])

    @pl.when(l == nl - 1)
    def _():
        for hh in range(HG_HEADS):
            hs_ref[0, hh] = st_scr[hh].T
        ls_ref[0] = h_scr[...]
        cs_ref[0] = xpad_scr[5:8, :]


def _mixer_block(x, gain, w_in, w_out, layer, mp, tl=512):
    bsz, seq, d = x.shape
    tl = min(tl, seq)
    full = lambda shape: pl.BlockSpec(shape, lambda b, l: (0,) * len(shape))
    stacked = lambda shape: pl.BlockSpec((None,) + shape, lambda b, l: (layer, 0, 0))
    out_shapes = (
        jax.ShapeDtypeStruct((bsz, seq, d), F32),
        jax.ShapeDtypeStruct((bsz, HG_HEADS, HG_D, HG_D), F32),
        jax.ShapeDtypeStruct((bsz, 1, LRU_WIDTH), F32),
        jax.ShapeDtypeStruct((bsz, CONV_K - 1, LRU_WIDTH), F32),
    )
    half = lambda dt: pltpu.VMEM((tl, HG_WIDTH), dt)
    xo, hg, lru, conv = pl.pallas_call(
        functools.partial(_mixer_block_kernel, tl=tl),
        out_shape=out_shapes,
        grid=(bsz, seq // tl),
        in_specs=[
            pl.BlockSpec((1, tl, d), lambda b, l: (b, l, 0)),
            full((1, d)), stacked((d, IN_WIDTH)),
            full((3, HG_WIDTH)), full((1, HG_WIDTH)), full((CONV_K, LRU_WIDTH)),
            full((1, LRU_WIDTH)), full((LRU_WIDTH, LRU_WIDTH)), full((1, LRU_WIDTH)),
            full((LRU_WIDTH, LRU_WIDTH)), full((1, LRU_WIDTH)), full((1, LRU_WIDTH)),
            full(mp["cm"].shape), stacked((d, d)),
        ],
        out_specs=(
            pl.BlockSpec((1, tl, d), lambda b, l: (b, l, 0)),
            pl.BlockSpec((1, HG_HEADS, HG_D, HG_D), lambda b, l: (b, 0, 0, 0)),
            pl.BlockSpec((1, 1, LRU_WIDTH), lambda b, l: (b, 0, 0)),
            pl.BlockSpec((1, CONV_K - 1, LRU_WIDTH), lambda b, l: (b, 0, 0)),
        ),
        scratch_shapes=[
            pltpu.VMEM((HG_HEADS, HG_D, HG_D), F32),
            pltpu.VMEM((1, LRU_WIDTH), F32),
            pltpu.VMEM((tl + 8, LRU_WIDTH), F32),
            pltpu.VMEM((tl, d), BF16),
            half(F32), half(F32), half(F32), half(F32), half(F32), half(F32), half(F32),
            half(F32),
            pltpu.VMEM((tl, d), BF16),
        ],
        compiler_params=_cparams(("parallel", "arbitrary")),
        name="mixer_block",
    )(x, gain.reshape(1, d), w_in, mp["lbp"], mp["hgg"], mp["cw"], mp["cb"], mp["wra"],
      mp["bra"], mp["wix"], mp["bix"], mp["c8"], mp["cm"], w_out)
    return xo, hg, lru.reshape(bsz, LRU_WIDTH), conv


def _mixer_step_kernel(z_ref, sh_ref, sl_ref, sc_ref, lbp_ref, hgg_ref, cw_ref, cb_ref, wra_ref,
                       bra_ref, wix_ref, bix_ref, c8_ref, *rest, bb, aliased):
    m_ref, hn_ref, ln_ref, cn_ref, o_scr = rest[1:] if aliased else rest
    z = z_ref[...]
    xr = z[:, 4 * HG_WIDTH:4 * HG_WIDTH + LRU_WIDTH]
    gr = z[:, 4 * HG_WIDTH + LRU_WIDTH:IN_WIDTH]
    buf = sc_ref[...]
    xc = cb_ref[...]
    for j in range(CONV_K - 1):
        xc = xc + buf[:, j * LRU_WIDTH:(j + 1) * LRU_WIDTH] * cw_ref[j:j + 1, :]
    xc = xc + xr * cw_ref[CONV_K - 1:CONV_K, :]
    cn_ref[:, 0:(CONV_K - 2) * LRU_WIDTH] = buf[:, LRU_WIDTH:(CONV_K - 1) * LRU_WIDTH]
    cn_ref[:, (CONV_K - 2) * LRU_WIDTH:(CONV_K - 1) * LRU_WIDTH] = xr
    a, u = _lru_gates(xc, wra_ref[...], bra_ref[...], wix_ref[...], bix_ref[...], c8_ref[...])
    hnew = u + a * sl_ref[...]
    ln_ref[...] = hnew
    m_ref[:, HG_WIDTH:D_MODEL] = (hnew * _gelu_tanh(gr)).astype(m_ref.dtype)

    q = _silu(z[:, 0:HG_WIDTH])
    log_f, k = _hgrn_gates(z[:, HG_WIDTH:2 * HG_WIDTH], lbp_ref[0:1, :], lbp_ref[1:2, :],
                           lbp_ref[2:3, :])
    f = jnp.exp(log_f)
    v = z[:, 2 * HG_WIDTH:3 * HG_WIDTH]
    gate = _silu(z[:, 3 * HG_WIDTH:4 * HG_WIDTH])
    zpad = jnp.zeros((HG_D - bb, HG_D), F32)
    for hh in range(HG_HEADS):
        cs = slice(hh * HG_D, (hh + 1) * HG_D)
        ft = jnp.concatenate([f[:, cs], zpad], axis=0).T
        kt = jnp.concatenate([k[:, cs], zpad], axis=0).T
        qt = jnp.concatenate([q[:, cs], zpad], axis=0).T
        for j in range(bb):
            fcol = jnp.broadcast_to(ft[:, j:j + 1], (HG_D, HG_D))
            kcol = jnp.broadcast_to(kt[:, j:j + 1], (HG_D, HG_D))
            qcol = jnp.broadcast_to(qt[:, j:j + 1], (HG_D, HG_D))
            vrow = jnp.broadcast_to(v[j:j + 1, cs], (HG_D, HG_D))
            sn = fcol * sh_ref[j, hh] + kcol * vrow
            hn_ref[j, hh] = sn
            o_scr[j:j + 1, cs] = jnp.sum(qcol * sn, axis=0, keepdims=True)
    for hh in range(HG_HEADS):
        cs = slice(hh * HG_D, (hh + 1) * HG_D)
        o = o_scr[:, cs]
        o = o * lax.rsqrt(jnp.mean(o * o, axis=-1, keepdims=True) + EPS)
        o = o * hgg_ref[:, cs]
        o = o * gate[:, cs]
        m_ref[:, cs] = o.astype(m_ref.dtype)


def _mixer_step(z, s_hg_all, layer, s_lru, s_conv, mp, hg_buf=None, bb=16):
    bsz = z.shape[0]
    bb = min(bb, bsz)
    cw3 = (CONV_K - 1) * LRU_WIDTH
    full = lambda shape: pl.BlockSpec(shape, lambda i: (0,) * len(shape))
    state_spec = pl.BlockSpec((None, bb, HG_HEADS, HG_D, HG_D), lambda i: (layer, i, 0, 0, 0))
    out_shapes = (
        jax.ShapeDtypeStruct((bsz, D_MODEL), mp["wra"].dtype),
        jax.ShapeDtypeStruct(s_hg_all.shape, F32),
        jax.ShapeDtypeStruct((bsz, LRU_WIDTH), F32),
        jax.ShapeDtypeStruct((bsz, cw3), F32),
    )
    in_specs = [
        pl.BlockSpec((bb, IN_WIDTH), lambda i: (i, 0)),
        state_spec,
        pl.BlockSpec((bb, LRU_WIDTH), lambda i: (i, 0)),
        pl.BlockSpec((bb, cw3), lambda i: (i, 0)),
        full((3, HG_WIDTH)), full((1, HG_WIDTH)), full((CONV_K, LRU_WIDTH)),
        full((1, LRU_WIDTH)), full((LRU_WIDTH, LRU_WIDTH)), full((1, LRU_WIDTH)),
        full((LRU_WIDTH, LRU_WIDTH)), full((1, LRU_WIDTH)), full((1, LRU_WIDTH)),
    ]
    args = [z, s_hg_all, s_lru, s_conv.reshape(bsz, cw3), mp["lbp"], mp["hgg"], mp["cw"],
            mp["cb"], mp["wra"], mp["bra"], mp["wix"], mp["bix"], mp["c8"]]
    aliases = {}
    if hg_buf is not None:
        aliases = {len(args): 1}
        in_specs.append(pl.BlockSpec(memory_space=pl.ANY))
        args.append(hg_buf)
    merged, hg, lru, conv = pl.pallas_call(
        functools.partial(_mixer_step_kernel, bb=bb, aliased=hg_buf is not None),
        out_shape=out_shapes,
        grid=(bsz // bb,),
        in_specs=in_specs,
        out_specs=(
            pl.BlockSpec((bb, D_MODEL), lambda i: (i, 0)),
            state_spec,
            pl.BlockSpec((bb, LRU_WIDTH), lambda i: (i, 0)),
            pl.BlockSpec((bb, cw3), lambda i: (i, 0)),
        ),
        scratch_shapes=[pltpu.VMEM((bb, HG_WIDTH), F32)],
        input_output_aliases=aliases,
        compiler_params=_cparams(("parallel",)),
        name="mixer_step",
    )(*args)
    return merged, hg, lru, conv.reshape(bsz, CONV_K - 1, LRU_WIDTH)


def _router_kernel(x_ref, g_ref, rw_ref, tril_ref, info_ref, cnt_ref, carry_scr):
    i = pl.program_id(0)

    @pl.when(i == 0)
    def _():
        carry_scr[...] = jnp.zeros_like(carry_scr)

    h = _rms(x_ref[...], g_ref[...])
    logits = jnp.dot(h, rw_ref[...], preferred_element_type=F32, precision=lax.Precision.HIGHEST)
    lane = lax.broadcasted_iota(I32, logits.shape, 1)
    neg = jnp.float32(-jnp.inf)
    logits = jnp.where(lane < N_EXPERTS, logits, neg)
    m1 = jnp.max(logits, axis=-1, keepdims=True)
    i1 = jnp.min(jnp.where(logits == m1, lane, LANES), axis=-1, keepdims=True)
    l2 = jnp.where(lane == i1, neg, logits)
    m2 = jnp.max(l2, axis=-1, keepdims=True)
    i2 = jnp.min(jnp.where(l2 == m2, lane, LANES), axis=-1, keepdims=True)
    e = jnp.exp(m2 - m1)
    p1 = 1.0 / (1.0 + e)
    p2 = e / (1.0 + e)
    oh1 = (lane == i1).astype(F32)
    oh2 = (lane == i2).astype(F32)
    sel = oh1 + oh2
    before = _dot(tril_ref[...], sel.astype(BF16)) + carry_scr[...]
    r1 = jnp.sum(oh1 * before, axis=-1, keepdims=True)
    r2 = jnp.sum(oh2 * before, axis=-1, keepdims=True)
    carry_scr[...] += jnp.sum(sel, axis=0, keepdims=True)
    info = jnp.where(lane == 0, i1.astype(F32), 0.0)
    info = jnp.where(lane == 1, i2.astype(F32), info)
    info = jnp.where(lane == 2, r1, info)
    info = jnp.where(lane == 3, r2, info)
    info = jnp.where(lane == 4, p1, info)
    info = jnp.where(lane == 5, p2, info)
    info_ref[...] = info
    cnt_ref[...] = carry_scr[...]


def _router(x, gain, rw_pad, tm=512):
    t, d = x.shape
    tm = min(tm, t)
    tril = jnp.tril(jnp.ones((tm, tm), BF16), -1)
    return pl.pallas_call(
        _router_kernel,
        out_shape=(jax.ShapeDtypeStruct((t, LANES), F32),
                   jax.ShapeDtypeStruct((1, LANES), F32)),
        grid=(t // tm,),
        in_specs=[
            pl.BlockSpec((tm, d), lambda i: (i, 0)),
            pl.BlockSpec((1, d), lambda i: (0, 0)),
            pl.BlockSpec((d, LANES), lambda i: (0, 0)),
            pl.BlockSpec((tm, tm), lambda i: (0, 0)),
        ],
        out_specs=(pl.BlockSpec((tm, LANES), lambda i: (i, 0)),
                   pl.BlockSpec((1, LANES), lambda i: (0, 0))),
        scratch_shapes=[pltpu.VMEM((1, LANES), F32)],
        compiler_params=_cparams(("arbitrary",)),
        name="router",
    )(x, gain.reshape(1, d), rw_pad, tril)


def _row_copy(src, dst, s, d, sem):
    return pltpu.make_async_copy(src.at[pl.ds(s, 1)], dst.at[pl.ds(d, 1)], sem)


def _tile_dest(dest, t, td):
    nb = t // td
    dest3 = jnp.concatenate([dest[0].reshape(nb, td), dest[1].reshape(nb, td)], axis=1)
    return dest3.reshape(nb, 1, 2 * td)


def _dispatch_kernel(dest_ref, x_ref, g_ref, xs_in_ref, xs_ref, pk_scr, sem, *, td):
    del xs_in_ref
    h = _rms(x_ref[...], g_ref[...])
    for c in range(SUBLANES):
        pk_scr[:, c, :] = h[:, c * LANES:(c + 1) * LANES]

    def issue(r, c):
        _row_copy(pk_scr, xs_ref, r, dest_ref[0, 0, r], sem).start(priority=0)
        _row_copy(pk_scr, xs_ref, r, dest_ref[0, 0, td + r], sem).start(priority=1)
        return c

    lax.fori_loop(0, td, issue, 0, unroll=8)
    for _ in range(2):
        pltpu.make_async_copy(pk_scr, xs_ref.at[pl.ds(0, td)], sem).wait()


def _dispatch(x, gain, dest, n_rows, td=512):
    t, d = x.shape
    td = min(td, t)
    row_tile = (d // LANES, LANES)
    xs0 = jnp.zeros((n_rows,) + row_tile, F32)
    return pl.pallas_call(
        functools.partial(_dispatch_kernel, td=td),
        out_shape=jax.ShapeDtypeStruct((n_rows,) + row_tile, F32),
        grid=(t // td,),
        in_specs=[
            pl.BlockSpec((1, 1, 2 * td), lambda i: (i, 0, 0), memory_space=pltpu.SMEM),
            pl.BlockSpec((td, d), lambda i: (i, 0)),
            pl.BlockSpec((1, d), lambda i: (0, 0)),
            pl.BlockSpec(memory_space=pl.ANY),
        ],
        out_specs=pl.BlockSpec(memory_space=pl.ANY),
        scratch_shapes=[pltpu.VMEM((td,) + row_tile, F32), pltpu.SemaphoreType.DMA(())],
        input_output_aliases={3: 0},
        compiler_params=pltpu.CompilerParams(dimension_semantics=("arbitrary",),
                                             vmem_limit_bytes=VMEM_LIMIT_BYTES,
                                             has_side_effects=True),
        name="dispatch",
    )(_tile_dest(dest, t, td), x, gain.reshape(1, d), xs0)


def _combine_kernel(dest_ref, x_ref, info_ref, g_ref, ys_ref, o_ref, y1_scr, y2_scr, sem, *, td):
    def issue(r, c):
        _row_copy(ys_ref, y1_scr, dest_ref[0, 0, r], r, sem).start(priority=0)
        _row_copy(ys_ref, y2_scr, dest_ref[0, 0, td + r], r, sem).start(priority=1)
        return c

    lax.fori_loop(0, td, issue, 0, unroll=8)
    pltpu.make_async_copy(ys_ref.at[pl.ds(0, td)], y1_scr, sem).wait()
    pltpu.make_async_copy(ys_ref.at[pl.ds(0, td)], y2_scr, sem).wait()
    p1 = info_ref[:, 4:5]
    p2 = info_ref[:, 5:6]
    moe = jnp.concatenate([p1 * y1_scr[:, c, :] + p2 * y2_scr[:, c, :] for c in range(SUBLANES)],
                          axis=-1)
    o_ref[...] = _rms(x_ref[...] + moe, g_ref[...])


def _combine(x, ys, dest, info, gain, td=256):
    t, d = x.shape
    td = min(td, t)
    row = pl.BlockSpec((td, d), lambda i: (i, 0))
    return pl.pallas_call(
        functools.partial(_combine_kernel, td=td),
        out_shape=jax.ShapeDtypeStruct((t, d), F32),
        grid=(t // td,),
        in_specs=[
            pl.BlockSpec((1, 1, 2 * td), lambda i: (i, 0, 0), memory_space=pltpu.SMEM),
            row,
            pl.BlockSpec((td, LANES), lambda i: (i, 0)),
            pl.BlockSpec((1, d), lambda i: (0, 0)),
            pl.BlockSpec(memory_space=pl.ANY),
        ],
        out_specs=row,
        scratch_shapes=[pltpu.VMEM((td,) + ys.shape[1:], F32), pltpu.VMEM((td,) + ys.shape[1:], F32),
                        pltpu.SemaphoreType.DMA(())],
        compiler_params=_cparams(("arbitrary",)),
        name="combine",
    )(_tile_dest(dest, t, td), x, info, gain.reshape(1, d), ys)


def _moe(x, norm_g, rw_pad, w1, w3, w2, final_g, tm_e):
    t, d = x.shape
    info, cnt = _router(x, norm_g, rw_pad)
    e1 = info[:, 0].astype(I32)
    e2 = info[:, 1].astype(I32)
    r1 = info[:, 2].astype(I32)
    r2 = info[:, 3].astype(I32)
    counts = cnt[0, :N_EXPERTS].astype(I32)
    padded = ((counts + tm_e - 1) // tm_e) * tm_e
    ends = jnp.cumsum(padded)
    starts = ends - padded
    dest = jnp.stack([starts[e1] + r1, starts[e2] + r2])
    n_tiles = (2 * t + N_EXPERTS * (tm_e - 1)) // tm_e
    n_rows = n_tiles * tm_e
    tile_start = jnp.arange(n_tiles, dtype=I32) * tm_e
    tile_expert = jnp.minimum(
        jnp.sum((tile_start[:, None] >= ends[None, :]).astype(I32), axis=1), N_EXPERTS - 1)
    n_valid = (ends[-1] // tm_e).reshape(1).astype(I32)
    xs = _dispatch(x, norm_g, dest, n_rows)
    ys = _ffn(xs, w1, w3, w2, tile_expert.astype(I32), n_valid, tm=tm_e, n_f=2)
    return _combine(x, ys, dest, info, final_g)


def _block_diag(w):
    n, c, d = w.shape
    eye = jnp.eye(n, dtype=w.dtype)
    return (w[:, :, None, :] * eye[:, None, :, None]).reshape(n * c, n * d)


def _cum_matrix():
    t = jnp.arange(CHUNK)
    return (t[None, :] <= t[:, None]).astype(BF16)


def _mixer_params(l, lb, hg_norm_g, conv_w, conv_b, w_ra, b_ra, w_ix, b_ix, lru_lambda, wdtype):
    row = lambda a: a.reshape(1, -1).astype(F32)
    return {
        "lbp": jnp.stack([jnp.log(lb), jnp.log1p(-lb), 1.0 - lb]).astype(F32),
        "hgg": row(hg_norm_g[l]),
        "cw": conv_w[l].astype(F32),
        "cb": row(conv_b[l]),
        "wra": _block_diag(w_ra[l].astype(F32)).astype(wdtype),
        "bra": row(b_ra[l]),
        "wix": _block_diag(w_ix[l].astype(F32)).astype(wdtype),
        "bix": row(b_ix[l]),
        "c8": row(-LRU_C * jax.nn.softplus(-lru_lambda[l].astype(F32))),
        "cm": _cum_matrix(),
    }


def _trunk(x, seq_shape, states, prm):
    bsz, seq = seq_shape
    one = jnp.ones((1,), I32)
    new_hg, new_lru, new_conv = [], [], []
    hg_stack = None if states is None else states[0]
    for l in range(2):
        mp = prm["mixer"][l]
        if states is None:
            x, s1, s2, s3 = _mixer_block(x.reshape(bsz, seq, D_MODEL), prm["norm1_g"][l],
                                         prm["w_in"], prm["w_out"], l, mp)
            x = x.reshape(bsz * seq, D_MODEL)
            new_hg.append(s1)
        else:
            z = _linear(x, prm["w_in"], l, gain=prm["norm1_g"][l])
            merged, hg_stack, s2, s3 = _mixer_step(z, states[0], l, states[1][l], states[2][l],
                                                   mp, hg_buf=hg_stack)
            x = _linear(merged, prm["w_out"], l, res=x)
        new_lru.append(s2)
        new_conv.append(s3)
        if l == 0:
            n_tiles = max(x.shape[0] // 512, 1)
            n_f = 1 if prm["ffn_w1"].dtype == BF16 else prm["ffn_w1"].shape[2] // 256
            x = _ffn(x, prm["ffn_w1"], prm["ffn_w3"], prm["ffn_w2"], jnp.zeros((n_tiles,), I32),
                     one * n_tiles, gain=prm["norm2_g"][0], res=True, tm=512, n_f=n_f)
        else:
            x = _moe(x, prm["norm2_g"][1], prm["rw_pad"], prm["moe_w1"], prm["moe_w3"],
                     prm["moe_w2"], prm["final_norm_g"], tm_e=512 if states is None else 128)
    hg_all = jnp.stack(new_hg) if states is None else hg_stack
    return x, hg_all, jnp.stack(new_lru), jnp.stack(new_conv)


def kernel(x_prompt, x_sample, state_hgrn, state_lru, state_conv, norm1_g, w_in, lower_bounds,
           hg_norm_g, conv_w, conv_b, w_ra, b_ra, w_ix, b_ix, lru_lambda, w_out, norm2_g,
           ffn_w1, ffn_w3, ffn_w2, router_w, moe_w1, moe_w3, moe_w2, final_norm_g):
    lb_all = jnp.cumsum(jax.nn.softmax(lower_bounds.astype(F32), axis=0), axis=0)
    lb_all = lb_all - lb_all[0]
    experts = {
        "moe_w1": moe_w1[0].astype(BF16), "moe_w3": moe_w3[0].astype(BF16),
        "moe_w2": moe_w2[0].astype(BF16),
        "rw_pad": jnp.pad(router_w[0].astype(F32), ((0, 0), (0, LANES - N_EXPERTS))),
        "norm1_g": norm1_g, "norm2_g": norm2_g, "final_norm_g": final_norm_g,
    }

    def params(wdtype):
        return dict(
            experts,
            mixer=[_mixer_params(l, lb_all[l], hg_norm_g, conv_w, conv_b, w_ra, b_ra, w_ix, b_ix,
                                 lru_lambda, wdtype) for l in range(2)],
            w_in=w_in.astype(wdtype), w_out=w_out.astype(wdtype),
            ffn_w1=ffn_w1.astype(wdtype), ffn_w3=ffn_w3.astype(wdtype),
            ffn_w2=ffn_w2.astype(wdtype))

    bp, sp, d = x_prompt.shape
    bs = x_sample.shape[0]
    y_p, hg_p, lru_p, conv_p = _trunk(x_prompt.reshape(bp * sp, d), (bp, sp), None, params(BF16))
    y_s, hg_s, lru_s, conv_s = _trunk(x_sample.reshape(bs, d), (bs, 1),
                                      (state_hgrn, state_lru, state_conv), params(F32))
    return (y_p.reshape(bp, sp, d), y_s.reshape(bs, 1, d), hg_p, lru_p, conv_p, hg_s, lru_s,
            conv_s)
```

```python
import functools

import jax
import jax.numpy as jnp
from jax import lax
from jax.experimental import pallas as pl
from jax.experimental.pallas import tpu as pltpu

F32 = jnp.float32
BF16 = jnp.bfloat16
I32 = jnp.int32

D_MODEL = 1024
HG_WIDTH = 512
HG_HEADS = 4
HG_D = 128
LRU_WIDTH = 512
LRU_BLOCKS = 8
LRU_C = 8.0
CONV_K = 4
IN_WIDTH = 4 * HG_WIDTH + 2 * LRU_WIDTH
N_EXPERTS = 8
EPS = 1e-6

CHUNK = 64
DIAG = 1
LANES = 128
SUBLANES = 8
VMEM_LIMIT_BYTES = 56 * 1024 * 1024


def _cparams(sem):
    return pltpu.CompilerParams(dimension_semantics=sem, vmem_limit_bytes=VMEM_LIMIT_BYTES)


def _sigmoid(x):
    return 1.0 / (1.0 + jnp.exp(-x))


def _silu(x):
    return x * _sigmoid(x)


def _gelu_tanh(x):
    c = 0.7978845608028654
    return 0.5 * x * (1.0 + jnp.tanh(c * (x + 0.044715 * (x * x * x))))


def _rms(x, g):
    return x * lax.rsqrt(jnp.mean(x * x, axis=-1, keepdims=True) + EPS) * g


def _dot(a, b):
    if b.dtype == F32:
        return jnp.dot(a.astype(F32), b, preferred_element_type=F32,
                       precision=lax.Precision.HIGHEST)
    return jnp.dot(a.astype(BF16), b, preferred_element_type=F32)


def _dot_nt(a, b):
    return lax.dot_general(a, b, (((1,), (1,)), ((), ())), preferred_element_type=F32)


def _dot_tn(a, b):
    return lax.dot_general(a, b, (((0,), (0,)), ((), ())), preferred_element_type=F32)


def _split3(x):
    x1 = x.astype(BF16)
    r1 = x - x1.astype(F32)
    x2 = r1.astype(BF16)
    x3 = (r1 - x2.astype(F32)).astype(BF16)
    return x1, x2, x3


def _linear_kernel(*refs, has_gain, has_res):
    it = iter(refs)
    x_ref = next(it)
    g_ref = next(it) if has_gain else None
    w_ref = next(it)
    r_ref = next(it) if has_res else None
    o_ref = next(it)
    h_scr = next(it)

    @pl.when(pl.program_id(1) == 0)
    def _():
        x = x_ref[...].astype(F32)
        if has_gain:
            x = _rms(x, g_ref[...])
        h_scr[...] = x.astype(h_scr.dtype)

    acc = _dot(h_scr[...], w_ref[...])
    if has_res:
        acc = acc + r_ref[...]
    o_ref[...] = acc.astype(o_ref.dtype)


def _linear(x, w, layer, gain=None, res=None, tm=1024, tn=1024):
    m, k = x.shape
    n = w.shape[2]
    tm = min(tm, m)
    tn = min(tn, n)
    in_specs = [pl.BlockSpec((tm, k), lambda i, j: (i, 0))]
    args = [x]
    if gain is not None:
        in_specs.append(pl.BlockSpec((1, k), lambda i, j: (0, 0)))
        args.append(gain.reshape(1, k))
    in_specs.append(pl.BlockSpec((None, k, tn), lambda i, j: (layer, 0, j)))
    args.append(w)
    if res is not None:
        in_specs.append(pl.BlockSpec((tm, tn), lambda i, j: (i, j)))
        args.append(res)
    return pl.pallas_call(
        functools.partial(_linear_kernel, has_gain=gain is not None, has_res=res is not None),
        out_shape=jax.ShapeDtypeStruct((m, n), F32),
        grid=(m // tm, n // tn),
        in_specs=in_specs,
        out_specs=pl.BlockSpec((tm, tn), lambda i, j: (i, j)),
        scratch_shapes=[pltpu.VMEM((tm, k), w.dtype)],
        compiler_params=_cparams(("parallel", "arbitrary")),
        name="linear",
    )(*args)


def _ffn_kernel(te_ref, nv_ref, *refs, has_gain, has_res, tf, ck):
    del te_ref
    it = iter(refs)
    x_ref = next(it)
    g_ref = next(it) if has_gain else None
    w1_ref, w3_ref, w2_ref, o_ref, xb_scr, a_scr, acc_scr = it
    i = pl.program_id(0)
    j = pl.program_id(1)
    nj = pl.num_programs(1)
    valid = i < nv_ref[0]

    @pl.when(valid)
    def _():
        @pl.when(j == 0)
        def _():
            x = x_ref[...]
            if has_gain:
                x = _rms(x, g_ref[...])
            xb_scr[...] = x.astype(xb_scr.dtype)

        xb = xb_scr[...]
        for c in range(tf // ck):
            h1 = _dot(xb, w1_ref[0, :, c * ck:(c + 1) * ck])
            h3 = _dot(xb, w3_ref[0, :, c * ck:(c + 1) * ck])
            a_scr[:, c * ck:(c + 1) * ck] = (_silu(h1) * h3).astype(a_scr.dtype)
        part = _dot(a_scr[...], w2_ref[0])

        @pl.when(j == 0)
        def _():
            acc_scr[...] = part

        @pl.when(j > 0)
        def _():
            acc_scr[...] += part

        @pl.when(j == nj - 1)
        def _():
            out = acc_scr[...]
            if has_res:
                out = x_ref[...] + out
            o_ref[...] = out

    @pl.when(jnp.logical_and(jnp.logical_not(valid), j == nj - 1))
    def _():
        o_ref[...] = jnp.zeros_like(o_ref)


def _ffn(x, w1, w3, w2, tile_expert, n_valid, gain=None, res=False, tm=512, n_f=2, ck=256):
    p, d = x.shape
    f = w1.shape[2]
    tm = min(tm, p)
    tf = f // n_f
    n_tiles = p // tm
    in_specs = [pl.BlockSpec((tm, d), lambda i, j, te, nv: (i, 0))]
    args = [x]
    if gain is not None:
        in_specs.append(pl.BlockSpec((1, d), lambda i, j, te, nv: (0, 0)))
        args.append(gain.reshape(1, d))
    in_specs += [
        pl.BlockSpec((1, d, tf), lambda i, j, te, nv: (te[i], 0, j)),
        pl.BlockSpec((1, d, tf), lambda i, j, te, nv: (te[i], 0, j)),
        pl.BlockSpec((1, tf, d), lambda i, j, te, nv: (te[i], j, 0)),
    ]
    args += [w1, w3, w2]
    grid_spec = pltpu.PrefetchScalarGridSpec(
        num_scalar_prefetch=2,
        grid=(n_tiles, n_f),
        in_specs=in_specs,
        out_specs=pl.BlockSpec((tm, d), lambda i, j, te, nv: (i, 0)),
        scratch_shapes=[pltpu.VMEM((tm, d), w1.dtype), pltpu.VMEM((tm, tf), w1.dtype),
                        pltpu.VMEM((tm, d), F32)],
    )
    return pl.pallas_call(
        functools.partial(_ffn_kernel, has_gain=gain is not None, has_res=res, tf=tf, ck=ck),
        out_shape=jax.ShapeDtypeStruct((p, d), F32),
        grid_spec=grid_spec,
        compiler_params=_cparams(("arbitrary", "arbitrary")),
        name="swiglu",
    )(tile_expert, n_valid, *args)


def _hgrn_gates(fz, loglb, l1mlb, oneml):
    e = jnp.exp(-jnp.abs(fz))
    ope = 1.0 + e
    log_sig = jnp.minimum(fz, 0.0) - jnp.log(ope)
    cc = l1mlb + log_sig
    log_f = jnp.maximum(loglb, cc) + jnp.log(1.0 + jnp.exp(-jnp.abs(loglb - cc)))
    k = oneml * (jnp.where(fz > 0.0, e, 1.0) / ope)
    return log_f, k


def _level_reference(b, w):
    n = b.shape[0]
    if w >= 4:
        return jnp.concatenate(
            [jnp.broadcast_to(b[j + w:j + w + 1, :], (2 * w, b.shape[1]))
             for j in range(0, n, 2 * w)], axis=0)
    b3 = b.reshape(n // SUBLANES, SUBLANES, b.shape[1])
    ahead = lambda s: pltpu.roll(b3, SUBLANES - s, 1)
    pos = lax.broadcasted_iota(I32, (1, SUBLANES, 1), 1) & (2 * w - 1)
    if w == 2:
        r3 = jnp.where(pos == 0, ahead(2),
                       jnp.where(pos == 1, ahead(1),
                                 jnp.where(pos == 2, b3, pltpu.roll(b3, 1, 1))))
    else:
        r3 = jnp.where(pos == 0, ahead(1), b3)
    return r3.reshape(b.shape)


def _lru_gates(xc, wra, bra, wix, bix, c8):
    r = _sigmoid(_dot(xc, wra) + bra)
    ig = _sigmoid(_dot(xc, wix) + bix)
    log_a = c8 * r
    a = jnp.exp(log_a)
    one_m_a2 = -jnp.tanh(log_a) * (a * a + 1.0)
    u = jnp.sqrt(one_m_a2) * (ig * xc)
    return a, u


def _mixer_block_kernel(x_ref, g1_ref, win_ref, lbp_ref, hgg_ref, cw_ref, cb_ref, wra_ref,
                        bra_ref, wix_ref, bix_ref, c8_ref, cm_ref, wout_ref,
                        xo_ref, hs_ref, ls_ref, cs_ref,
                        st_scr, h_scr, xpad_scr, hb_scr, q_scr, lf_scr, k_scr, v_scr, g_scr,
                        gl_scr, a_scr, u_scr, m_scr, *, tl):
    l = pl.program_id(1)
    nl = pl.num_programs(1)

    @pl.when(l == 0)
    def _():
        st_scr[...] = jnp.zeros_like(st_scr)
        h_scr[...] = jnp.zeros_like(h_scr)
        xpad_scr[0:8, :] = jnp.zeros((8, LRU_WIDTH), F32)

    hb_scr[...] = _rms(x_ref[0], g1_ref[...]).astype(BF16)
    hb = hb_scr[...]
    zg = lambda j: _dot(hb, win_ref[:, j * HG_WIDTH:(j + 1) * HG_WIDTH])
    q_scr[...] = _silu(zg(0))
    log_f, kk = _hgrn_gates(zg(1), lbp_ref[0:1, :], lbp_ref[1:2, :], lbp_ref[2:3, :])
    lf_scr[...] = log_f
    k_scr[...] = kk
    v_scr[...] = zg(2)
    g_scr[...] = _silu(zg(3))
    xpad_scr[8:8 + tl, :] = zg(4)
    gl_scr[...] = _gelu_tanh(zg(5))
    xc = cb_ref[...]
    for j in range(CONV_K):
        xc = xc + xpad_scr[5 + j:5 + j + tl, :] * cw_ref[j:j + 1, :]
    tail = xpad_scr[tl + 5:tl + 8, :]
    xpad_scr[5:8, :] = tail
    a, u = _lru_gates(xc, wra_ref[...], bra_ref[...], wix_ref[...], bix_ref[...], c8_ref[...])
    a_scr[...] = a
    u_scr[...] = u

    row = lax.broadcasted_iota(I32, (CHUNK, 1), 0)
    ti = lax.broadcasted_iota(I32, (CHUNK, CHUNK), 0)
    si = lax.broadcasted_iota(I32, (CHUNK, CHUNK), 1)
    t3 = lax.broadcasted_iota(I32, (1, DIAG, 1), 1)
    sub3 = lax.broadcasted_iota(I32, (1, SUBLANES, 1), 1)
    nblk = CHUNK // DIAG
    ngrp = CHUNK // SUBLANES

    def chunk_body(c, carry):
        r0 = pl.multiple_of(c * CHUNK, CHUNK)
        rows = pl.ds(r0, CHUNK)

        for gi in range(LRU_WIDTH // LANES):
            cs = slice(gi * LANES, (gi + 1) * LANES)
            aa = a_scr[rows, cs].reshape(ngrp, SUBLANES, LANES)
            uu = u_scr[rows, cs].reshape(ngrp, SUBLANES, LANES)
            d = 1
            while d < SUBLANES:
                keep = sub3 >= d
                a_sh = pltpu.roll(aa, d, 1)
                u_sh = pltpu.roll(uu, d, 1)
                uu = jnp.where(keep, aa * u_sh + uu, uu)
                aa = jnp.where(keep, aa * a_sh, aa)
                d *= 2
            hprev = h_scr[:, cs]
            groups = []
            for j in range(ngrp):
                hj = uu[j] + aa[j] * hprev
                groups.append(hj)
                hprev = hj[SUBLANES - 1:SUBLANES, :]
            hs = jnp.concatenate(groups, axis=0)
            h_scr[:, cs] = hprev
            m_scr[rows, HG_WIDTH + gi * LANES:HG_WIDTH + (gi + 1) * LANES] = (
                hs * gl_scr[rows, cs]).astype(BF16)

        q = q_scr[rows, :]
        k = k_scr[rows, :]
        v = v_scr[rows, :]
        f1, f2, f3 = _split3(lf_scr[rows, :])
        cm = cm_ref[...]
        b = _dot(cm, f1) + _dot(cm, f2) + _dot(cm, f3)

        for hh in range(HG_HEADS):
            cs = slice(hh * HG_D, (hh + 1) * HG_D)
            qh, kh, vh, bh = q[:, cs], k[:, cs], v[:, cs], b[:, cs]
            vb = vh.astype(BF16)
            amat = jnp.zeros((CHUNK, CHUNK), F32)
            w = CHUNK // 2
            while w >= DIAG:
                rw = _level_reference(bh, w)
                upper = (row & (2 * w - 1)) >= w
                dec = jnp.exp(-jnp.abs(bh - rw))
                qw = jnp.where(upper, qh * dec, 0.0).astype(BF16)
                kw = jnp.where(upper, 0.0, kh * dec).astype(BF16)
                aw = _dot_nt(qw, kw)
                if 2 * w < CHUNK:
                    sh = (2 * w).bit_length() - 1
                    aw = jnp.where((ti >> sh) == (si >> sh), aw, 0.0)
                amat = amat + aw
                w //= 2
            o = _dot(amat.astype(BF16), vb)
            st = st_scr[hh]
            o = o + _dot_nt((qh * jnp.exp(bh)).astype(BF16), st.astype(BF16))
            if DIAG == 1:
                o = o + jnp.sum(qh * kh, axis=-1, keepdims=True) * vh
            else:
                q3 = qh.reshape(nblk, DIAG, HG_D)
                k3 = kh.reshape(nblk, DIAG, HG_D)
                v3 = vh.reshape(nblk, DIAG, HG_D)
                b3 = bh.reshape(nblk, DIAG, HG_D)
                od = jnp.zeros((nblk, DIAG, HG_D), F32)
                for s in range(DIAG):
                    dec = jnp.exp(jnp.minimum(b3 - b3[:, s:s + 1, :], 0.0))
                    p = q3 * (k3[:, s:s + 1, :] * dec)
                    rs = jnp.sum(p, axis=-1, keepdims=True)
                    rs = jnp.where(t3 >= s, rs, 0.0)
                    od = od + rs * v3[:, s:s + 1, :]
                o = o + od.reshape(CHUNK, HG_D)
            bl = bh[CHUNK - 1:CHUNK, :]
            kdec = (kh * jnp.exp(bl - bh)).astype(BF16)
            st_scr[hh] = st * jnp.exp(bl) + _dot_tn(vb, kdec)
            o = o * lax.rsqrt(jnp.mean(o * o, axis=-1, keepdims=True) + EPS)
            o = o * hgg_ref[:, cs]
            o = o * g_scr[rows, cs]
            m_scr[rows, cs] = o.astype(BF16)
        return carry

    lax.fori_loop(0, tl // CHUNK, chunk_body, 0, unroll=4)
    xo_ref[0] = x_ref[0] + _dot(m_scr[...], wout_ref[...])

    @pl.when(l == nl - 1)
    def _():
        for hh in range(HG_HEADS):
            hs_ref[0, hh] = st_scr[hh].T
        ls_ref[0] = h_scr[...]
        cs_ref[0] = xpad_scr[5:8, :]


def _mixer_block(x, gain, w_in, w_out, layer, mp, tl=512):
    bsz, seq, d = x.shape
    tl = min(tl, seq)
    full = lambda shape: pl.BlockSpec(shape, lambda b, l: (0,) * len(shape))
    stacked = lambda shape: pl.BlockSpec((None,) + shape, lambda b, l: (layer, 0, 0))
    out_shapes = (
        jax.ShapeDtypeStruct((bsz, seq, d), F32),
        jax.ShapeDtypeStruct((bsz, HG_HEADS, HG_D, HG_D), F32),
        jax.ShapeDtypeStruct((bsz, 1, LRU_WIDTH), F32),
        jax.ShapeDtypeStruct((bsz, CONV_K - 1, LRU_WIDTH), F32),
    )
    half = lambda dt: pltpu.VMEM((tl, HG_WIDTH), dt)
    xo, hg, lru, conv = pl.pallas_call(
        functools.partial(_mixer_block_kernel, tl=tl),
        out_shape=out_shapes,
        grid=(bsz, seq // tl),
        in_specs=[
            pl.BlockSpec((1, tl, d), lambda b, l: (b, l, 0)),
            full((1, d)), stacked((d, IN_WIDTH)),
            full((3, HG_WIDTH)), full((1, HG_WIDTH)), full((CONV_K, LRU_WIDTH)),
            full((1, LRU_WIDTH)), full((LRU_WIDTH, LRU_WIDTH)), full((1, LRU_WIDTH)),
            full((LRU_WIDTH, LRU_WIDTH)), full((1, LRU_WIDTH)), full((1, LRU_WIDTH)),
            full(mp["cm"].shape), stacked((d, d)),
        ],
        out_specs=(
            pl.BlockSpec((1, tl, d), lambda b, l: (b, l, 0)),
            pl.BlockSpec((1, HG_HEADS, HG_D, HG_D), lambda b, l: (b, 0, 0, 0)),
            pl.BlockSpec((1, 1, LRU_WIDTH), lambda b, l: (b, 0, 0)),
            pl.BlockSpec((1, CONV_K - 1, LRU_WIDTH), lambda b, l: (b, 0, 0)),
        ),
        scratch_shapes=[
            pltpu.VMEM((HG_HEADS, HG_D, HG_D), F32),
            pltpu.VMEM((1, LRU_WIDTH), F32),
            pltpu.VMEM((tl + 8, LRU_WIDTH), F32),
            pltpu.VMEM((tl, d), BF16),
            half(F32), half(F32), half(F32), half(F32), half(F32), half(F32), half(F32),
            half(F32),
            pltpu.VMEM((tl, d), BF16),
        ],
        compiler_params=_cparams(("parallel", "arbitrary")),
        name="mixer_block",
    )(x, gain.reshape(1, d), w_in, mp["lbp"], mp["hgg"], mp["cw"], mp["cb"], mp["wra"],
      mp["bra"], mp["wix"], mp["bix"], mp["c8"], mp["cm"], w_out)
    return xo, hg, lru.reshape(bsz, LRU_WIDTH), conv


def _mixer_step_kernel(z_ref, sh_ref, sl_ref, sc_ref, lbp_ref, hgg_ref, cw_ref, cb_ref, wra_ref,
                       bra_ref, wix_ref, bix_ref, c8_ref, *rest, bb, aliased):
    m_ref, hn_ref, ln_ref, cn_ref, o_scr = rest[1:] if aliased else rest
    z = z_ref[...]
    xr = z[:, 4 * HG_WIDTH:4 * HG_WIDTH + LRU_WIDTH]
    gr = z[:, 4 * HG_WIDTH + LRU_WIDTH:IN_WIDTH]
    buf = sc_ref[...]
    xc = cb_ref[...]
    for j in range(CONV_K - 1):
        xc = xc + buf[:, j * LRU_WIDTH:(j + 1) * LRU_WIDTH] * cw_ref[j:j + 1, :]
    xc = xc + xr * cw_ref[CONV_K - 1:CONV_K, :]
    cn_ref[:, 0:(CONV_K - 2) * LRU_WIDTH] = buf[:, LRU_WIDTH:(CONV_K - 1) * LRU_WIDTH]
    cn_ref[:, (CONV_K - 2) * LRU_WIDTH:(CONV_K - 1) * LRU_WIDTH] = xr
    a, u = _lru_gates(xc, wra_ref[...], bra_ref[...], wix_ref[...], bix_ref[...], c8_ref[...])
    hnew = u + a * sl_ref[...]
    ln_ref[...] = hnew
    m_ref[:, HG_WIDTH:D_MODEL] = (hnew * _gelu_tanh(gr)).astype(m_ref.dtype)

    q = _silu(z[:, 0:HG_WIDTH])
    log_f, k = _hgrn_gates(z[:, HG_WIDTH:2 * HG_WIDTH], lbp_ref[0:1, :], lbp_ref[1:2, :],
                           lbp_ref[2:3, :])
    f = jnp.exp(log_f)
    v = z[:, 2 * HG_WIDTH:3 * HG_WIDTH]
    gate = _silu(z[:, 3 * HG_WIDTH:4 * HG_WIDTH])
    zpad = jnp.zeros((HG_D - bb, HG_D), F32)
    for hh in range(HG_HEADS):
        cs = slice(hh * HG_D, (hh + 1) * HG_D)
        ft = jnp.concatenate([f[:, cs], zpad], axis=0).T
        kt = jnp.concatenate([k[:, cs], zpad], axis=0).T
        qt = jnp.concatenate([q[:, cs], zpad], axis=0).T
        for j in range(bb):
            fcol = jnp.broadcast_to(ft[:, j:j + 1], (HG_D, HG_D))
            kcol = jnp.broadcast_to(kt[:, j:j + 1], (HG_D, HG_D))
            qcol = jnp.broadcast_to(qt[:, j:j + 1], (HG_D, HG_D))
            vrow = jnp.broadcast_to(v[j:j + 1, cs], (HG_D, HG_D))
            sn = fcol * sh_ref[j, hh] + kcol * vrow
            hn_ref[j, hh] = sn
            o_scr[j:j + 1, cs] = jnp.sum(qcol * sn, axis=0, keepdims=True)
    for hh in range(HG_HEADS):
        cs = slice(hh * HG_D, (hh + 1) * HG_D)
        o = o_scr[:, cs]
        o = o * lax.rsqrt(jnp.mean(o * o, axis=-1, keepdims=True) + EPS)
        o = o * hgg_ref[:, cs]
        o = o * gate[:, cs]
        m_ref[:, cs] = o.astype(m_ref.dtype)


def _mixer_step(z, s_hg_all, layer, s_lru, s_conv, mp, hg_buf=None, bb=16):
    bsz = z.shape[0]
    bb = min(bb, bsz)
    cw3 = (CONV_K - 1) * LRU_WIDTH
    full = lambda shape: pl.BlockSpec(shape, lambda i: (0,) * len(shape))
    state_spec = pl.BlockSpec((None, bb, HG_HEADS, HG_D, HG_D), lambda i: (layer, i, 0, 0, 0))
    out_shapes = (
        jax.ShapeDtypeStruct((bsz, D_MODEL), mp["wra"].dtype),
        jax.ShapeDtypeStruct(s_hg_all.shape, F32),
        jax.ShapeDtypeStruct((bsz, LRU_WIDTH), F32),
        jax.ShapeDtypeStruct((bsz, cw3), F32),
    )
    in_specs = [
        pl.BlockSpec((bb, IN_WIDTH), lambda i: (i, 0)),
        state_spec,
        pl.BlockSpec((bb, LRU_WIDTH), lambda i: (i, 0)),
        pl.BlockSpec((bb, cw3), lambda i: (i, 0)),
        full((3, HG_WIDTH)), full((1, HG_WIDTH)), full((CONV_K, LRU_WIDTH)),
        full((1, LRU_WIDTH)), full((LRU_WIDTH, LRU_WIDTH)), full((1, LRU_WIDTH)),
        full((LRU_WIDTH, LRU_WIDTH)), full((1, LRU_WIDTH)), full((1, LRU_WIDTH)),
    ]
    args = [z, s_hg_all, s_lru, s_conv.reshape(bsz, cw3), mp["lbp"], mp["hgg"], mp["cw"],
            mp["cb"], mp["wra"], mp["bra"], mp["wix"], mp["bix"], mp["c8"]]
    aliases = {}
    if hg_buf is not None:
        aliases = {len(args): 1}
        in_specs.append(pl.BlockSpec(memory_space=pl.ANY))
        args.append(hg_buf)
    merged, hg, lru, conv = pl.pallas_call(
        functools.partial(_mixer_step_kernel, bb=bb, aliased=hg_buf is not None),
        out_shape=out_shapes,
        grid=(bsz // bb,),
        in_specs=in_specs,
        out_specs=(
            pl.BlockSpec((bb, D_MODEL), lambda i: (i, 0)),
            state_spec,
            pl.BlockSpec((bb, LRU_WIDTH), lambda i: (i, 0)),
            pl.BlockSpec((bb, cw3), lambda i: (i, 0)),
        ),
        scratch_shapes=[pltpu.VMEM((bb, HG_WIDTH), F32)],
        input_output_aliases=aliases,
        compiler_params=_cparams(("parallel",)),
        name="mixer_step",
    )(*args)
    return merged, hg, lru, conv.reshape(bsz, CONV_K - 1, LRU_WIDTH)


def _router_kernel(x_ref, g_ref, rw_ref, tril_ref, info_ref, idx_ref, cnt_ref, carry_scr):
    i = pl.program_id(0)

    @pl.when(i == 0)
    def _():
        carry_scr[...] = jnp.zeros_like(carry_scr)

    h = _rms(x_ref[...], g_ref[...])
    logits = jnp.dot(h, rw_ref[...], preferred_element_type=F32, precision=lax.Precision.HIGHEST)
    lane = lax.broadcasted_iota(I32, logits.shape, 1)
    neg = jnp.float32(-jnp.inf)
    logits = jnp.where(lane < N_EXPERTS, logits, neg)
    m1 = jnp.max(logits, axis=-1, keepdims=True)
    i1 = jnp.min(jnp.where(logits == m1, lane, LANES), axis=-1, keepdims=True)
    l2 = jnp.where(lane == i1, neg, logits)
    m2 = jnp.max(l2, axis=-1, keepdims=True)
    i2 = jnp.min(jnp.where(l2 == m2, lane, LANES), axis=-1, keepdims=True)
    e = jnp.exp(m2 - m1)
    p1 = 1.0 / (1.0 + e)
    p2 = e / (1.0 + e)
    oh1 = (lane == i1).astype(F32)
    oh2 = (lane == i2).astype(F32)
    sel = oh1 + oh2
    before = _dot(tril_ref[...], sel.astype(BF16)) + carry_scr[...]
    r1 = jnp.sum(oh1 * before, axis=-1, keepdims=True)
    r2 = jnp.sum(oh2 * before, axis=-1, keepdims=True)
    carry_scr[...] += jnp.sum(sel, axis=0, keepdims=True)
    info = jnp.where(lane == 0, i1.astype(F32), 0.0)
    info = jnp.where(lane == 1, i2.astype(F32), info)
    info = jnp.where(lane == 2, r1, info)
    info = jnp.where(lane == 3, r2, info)
    info = jnp.where(lane == 4, p1, info)
    info = jnp.where(lane == 5, p2, info)
    info_ref[...] = info
    idx_ref[...] = info.T[0:SUBLANES, :].astype(I32)
    cnt_ref[...] = carry_scr[...]


def _router(x, gain, rw_pad, tm=512):
    t, d = x.shape
    tm = min(tm, t)
    tril = jnp.tril(jnp.ones((tm, tm), BF16), -1)
    return pl.pallas_call(
        _router_kernel,
        out_shape=(jax.ShapeDtypeStruct((t, LANES), F32),
                   jax.ShapeDtypeStruct((SUBLANES, t), I32),
                   jax.ShapeDtypeStruct((1, LANES), F32)),
        grid=(t // tm,),
        in_specs=[
            pl.BlockSpec((tm, d), lambda i: (i, 0)),
            pl.BlockSpec((1, d), lambda i: (0, 0)),
            pl.BlockSpec((d, LANES), lambda i: (0, 0)),
            pl.BlockSpec((tm, tm), lambda i: (0, 0)),
        ],
        out_specs=(pl.BlockSpec((tm, LANES), lambda i: (i, 0)),
                   pl.BlockSpec((SUBLANES, tm), lambda i: (0, i)),
                   pl.BlockSpec((1, LANES), lambda i: (0, 0))),
        scratch_shapes=[pltpu.VMEM((1, LANES), F32)],
        compiler_params=_cparams(("arbitrary",)),
        name="router",
    )(x, gain.reshape(1, d), rw_pad, tril)


def _row_copy(src, dst, s, d, sem):
    return pltpu.make_async_copy(src.at[pl.ds(s, 1)], dst.at[pl.ds(d, 1)], sem)


def _tile_dest(dest, t, td):
    nb = t // td
    dest3 = jnp.concatenate([dest[0].reshape(nb, td), dest[1].reshape(nb, td)], axis=1)
    return dest3.reshape(nb, 1, 2 * td)


def _dispatch_kernel(dest_ref, x_ref, g_ref, xs_in_ref, xs_ref, pk_scr, sem, *, td):
    del xs_in_ref
    pk_scr[...] = _rms(x_ref[...], g_ref[...])

    def issue(r, c):
        _row_copy(pk_scr, xs_ref, r, dest_ref[0, 0, r], sem).start(priority=0)
        _row_copy(pk_scr, xs_ref, r, dest_ref[0, 0, td + r], sem).start(priority=1)
        return c

    lax.fori_loop(0, td, issue, 0, unroll=8)
    for _ in range(2):
        pltpu.make_async_copy(pk_scr, xs_ref.at[pl.ds(0, td)], sem).wait()


def _dispatch(x, gain, dest, n_rows, td=512):
    t, d = x.shape
    td = min(td, t)
    xs0 = jnp.zeros((n_rows, d), F32)
    return pl.pallas_call(
        functools.partial(_dispatch_kernel, td=td),
        out_shape=jax.ShapeDtypeStruct((n_rows, d), F32),
        grid=(t // td,),
        in_specs=[
            pl.BlockSpec((1, 1, 2 * td), lambda i: (i, 0, 0), memory_space=pltpu.SMEM),
            pl.BlockSpec((td, d), lambda i: (i, 0)),
            pl.BlockSpec((1, d), lambda i: (0, 0)),
            pl.BlockSpec(memory_space=pl.ANY),
        ],
        out_specs=pl.BlockSpec(memory_space=pl.ANY),
        scratch_shapes=[pltpu.VMEM((td, d), F32), pltpu.SemaphoreType.DMA(())],
        input_output_aliases={3: 0},
        compiler_params=pltpu.CompilerParams(dimension_semantics=("arbitrary",),
                                             vmem_limit_bytes=VMEM_LIMIT_BYTES,
                                             has_side_effects=True),
        name="dispatch",
    )(_tile_dest(dest, t, td), x, gain.reshape(1, d), xs0)


def _combine_kernel(dest_ref, nxt_ref, x_ref, info_ref, g_ref, ys_ref, o_ref, y_scr, sems, *, td):
    i = pl.program_id(0)
    nb = pl.num_programs(0)

    def fetch(idx_ref, slot):
        def issue(r, c):
            _row_copy(ys_ref, y_scr.at[slot, 0], idx_ref[0, 0, r], r,
                      sems.at[slot]).start(priority=0)
            _row_copy(ys_ref, y_scr.at[slot, 1], idx_ref[0, 0, td + r], r,
                      sems.at[slot]).start(priority=1)
            return c

        lax.fori_loop(0, td, issue, 0, unroll=8)

    def finish(slot):
        for j in range(2):
            pltpu.make_async_copy(ys_ref.at[pl.ds(0, td)], y_scr.at[slot, j],
                                  sems.at[slot]).wait()
        p1 = info_ref[:, 4:5]
        p2 = info_ref[:, 5:6]
        x = x_ref[...] + (p1 * y_scr[slot, 0] + p2 * y_scr[slot, 1])
        o_ref[...] = _rms(x, g_ref[...])

    @pl.when(i == 0)
    def _():
        fetch(dest_ref, 0)

    for slot in range(2):
        @pl.when(i % 2 == slot)
        def _(slot=slot):
            @pl.when(i + 1 < nb)
            def _():
                fetch(nxt_ref, 1 - slot)

            finish(slot)


def _combine(x, ys, dest, info, gain, td=256):
    t, d = x.shape
    td = min(td, t)
    nb = t // td
    row = pl.BlockSpec((td, d), lambda i: (i, 0))
    tiled = _tile_dest(dest, t, td)
    return pl.pallas_call(
        functools.partial(_combine_kernel, td=td),
        out_shape=jax.ShapeDtypeStruct((t, d), F32),
        grid=(nb,),
        in_specs=[
            pl.BlockSpec((1, 1, 2 * td), lambda i: (i, 0, 0), memory_space=pltpu.SMEM),
            pl.BlockSpec((1, 1, 2 * td), lambda i: (jnp.minimum(i + 1, nb - 1), 0, 0),
                         memory_space=pltpu.SMEM),
            row,
            pl.BlockSpec((td, LANES), lambda i: (i, 0)),
            pl.BlockSpec((1, d), lambda i: (0, 0)),
            pl.BlockSpec(memory_space=pl.ANY),
        ],
        out_specs=row,
        scratch_shapes=[pltpu.VMEM((2, 2, td, d), F32), pltpu.SemaphoreType.DMA((2,))],
        compiler_params=_cparams(("arbitrary",)),
        name="combine",
    )(tiled, tiled, x, info, gain.reshape(1, d), ys)


def _moe(x, norm_g, rw_pad, w1, w3, w2, final_g, tm_e):
    t, d = x.shape
    info, idx, cnt = _router(x, norm_g, rw_pad)
    e1, e2, r1, r2 = idx[0], idx[1], idx[2], idx[3]
    counts = cnt[0, :N_EXPERTS].astype(I32)
    padded = ((counts + tm_e - 1) // tm_e) * tm_e
    ends = jnp.cumsum(padded)
    starts = ends - padded
    dest = jnp.stack([starts[e1] + r1, starts[e2] + r2])
    n_tiles = (2 * t + N_EXPERTS * (tm_e - 1)) // tm_e
    n_rows = n_tiles * tm_e
    tile_start = jnp.arange(n_tiles, dtype=I32) * tm_e
    tile_expert = jnp.minimum(
        jnp.sum((tile_start[:, None] >= ends[None, :]).astype(I32), axis=1), N_EXPERTS - 1)
    n_valid = (ends[-1] // tm_e).reshape(1).astype(I32)
    xs = _dispatch(x, norm_g, dest, n_rows)
    ys = _ffn(xs, w1, w3, w2, tile_expert.astype(I32), n_valid, tm=tm_e, n_f=2)
    return _combine(x, ys, dest, info, final_g)


def _block_diag(w):
    n, c, d = w.shape
    eye = jnp.eye(n, dtype=w.dtype)
    return (w[:, :, None, :] * eye[:, None, :, None]).reshape(n * c, n * d)


def _cum_matrix():
    t = jnp.arange(CHUNK)
    return (t[None, :] <= t[:, None]).astype(BF16)


def _mixer_params(l, lb, hg_norm_g, conv_w, conv_b, w_ra, b_ra, w_ix, b_ix, lru_lambda, wdtype):
    row = lambda a: a.reshape(1, -1).astype(F32)
    return {
        "lbp": jnp.stack([jnp.log(lb), jnp.log1p(-lb), 1.0 - lb]).astype(F32),
        "hgg": row(hg_norm_g[l]),
        "cw": conv_w[l].astype(F32),
        "cb": row(conv_b[l]),
        "wra": _block_diag(w_ra[l].astype(F32)).astype(wdtype),
        "bra": row(b_ra[l]),
        "wix": _block_diag(w_ix[l].astype(F32)).astype(wdtype),
        "bix": row(b_ix[l]),
        "c8": row(-LRU_C * jax.nn.softplus(-lru_lambda[l].astype(F32))),
        "cm": _cum_matrix(),
    }


def _trunk(x, seq_shape, states, prm):
    bsz, seq = seq_shape
    one = jnp.ones((1,), I32)
    new_hg, new_lru, new_conv = [], [], []
    hg_stack = None if states is None else states[0]
    for l in range(2):
        mp = prm["mixer"][l]
        if states is None:
            x, s1, s2, s3 = _mixer_block(x.reshape(bsz, seq, D_MODEL), prm["norm1_g"][l],
                                         prm["w_in"], prm["w_out"], l, mp)
            x = x.reshape(bsz * seq, D_MODEL)
            new_hg.append(s1)
        else:
            z = _linear(x, prm["w_in"], l, gain=prm["norm1_g"][l])
            merged, hg_stack, s2, s3 = _mixer_step(z, states[0], l, states[1][l], states[2][l],
                                                   mp, hg_buf=hg_stack)
            x = _linear(merged, prm["w_out"], l, res=x)
        new_lru.append(s2)
        new_conv.append(s3)
        if l == 0:
            n_tiles = max(x.shape[0] // 512, 1)
            n_f = 1 if prm["ffn_w1"].dtype == BF16 else prm["ffn_w1"].shape[2] // 256
            x = _ffn(x, prm["ffn_w1"], prm["ffn_w3"], prm["ffn_w2"], jnp.zeros((n_tiles,), I32),
                     one * n_tiles, gain=prm["norm2_g"][0], res=True, tm=512, n_f=n_f)
        else:
            x = _moe(x, prm["norm2_g"][1], prm["rw_pad"], prm["moe_w1"], prm["moe_w3"],
                     prm["moe_w2"], prm["final_norm_g"], tm_e=512 if states is None else 128)
    hg_all = jnp.stack(new_hg) if states is None else hg_stack
    return x, hg_all, jnp.stack(new_lru), jnp.stack(new_conv)


def kernel(x_prompt, x_sample, state_hgrn, state_lru, state_conv, norm1_g, w_in, lower_bounds,
           hg_norm_g, conv_w, conv_b, w_ra, b_ra, w_ix, b_ix, lru_lambda, w_out, norm2_g,
           ffn_w1, ffn_w3, ffn_w2, router_w, moe_w1, moe_w3, moe_w2, final_norm_g):
    lb_all = jnp.cumsum(jax.nn.softmax(lower_bounds.astype(F32), axis=0), axis=0)
    lb_all = lb_all - lb_all[0]
    experts = {
        "moe_w1": moe_w1[0].astype(BF16), "moe_w3": moe_w3[0].astype(BF16),
        "moe_w2": moe_w2[0].astype(BF16),
        "rw_pad": jnp.pad(router_w[0].astype(F32), ((0, 0), (0, LANES - N_EXPERTS))),
        "norm1_g": norm1_g, "norm2_g": norm2_g, "final_norm_g": final_norm_g,
    }

    def params(wdtype):
        return dict(
            experts,
            mixer=[_mixer_params(l, lb_all[l], hg_norm_g, conv_w, conv_b, w_ra, b_ra, w_ix, b_ix,
                                 lru_lambda, wdtype) for l in range(2)],
            w_in=w_in.astype(wdtype), w_out=w_out.astype(wdtype),
            ffn_w1=ffn_w1.astype(wdtype), ffn_w3=ffn_w3.astype(wdtype),
            ffn_w2=ffn_w2.astype(wdtype))

    bp, sp, d = x_prompt.shape
    bs = x_sample.shape[0]
    y_p, hg_p, lru_p, conv_p = _trunk(x_prompt.reshape(bp * sp, d), (bp, sp), None, params(BF16))
    y_s, hg_s, lru_s, conv_s = _trunk(x_sample.reshape(bs, d), (bs, 1),
                                      (state_hgrn, state_lru, state_conv), params(F32))
    return (y_p.reshape(bp, sp, d), y_s.reshape(bs, 1, d), hg_p, lru_p, conv_p, hg_s, lru_s,
            conv_s)
```

```python
import functools

import jax
import jax.numpy as jnp
from jax import lax
from jax.experimental import pallas as pl
from jax.experimental.pallas import tpu as pltpu

F32 = jnp.float32
BF16 = jnp.bfloat16
I32 = jnp.int32

D_MODEL = 1024
HG_WIDTH = 512
HG_HEADS = 4
HG_D = 128
LRU_WIDTH = 512
LRU_BLOCKS = 8
LRU_C = 8.0
CONV_K = 4
IN_WIDTH = 4 * HG_WIDTH + 2 * LRU_WIDTH
N_EXPERTS = 8
EPS = 1e-6

CHUNK = 64
DIAG = 1
LANES = 128
SUBLANES = 8
VMEM_LIMIT_BYTES = 56 * 1024 * 1024


def _cparams(sem):
    return pltpu.CompilerParams(dimension_semantics=sem, vmem_limit_bytes=VMEM_LIMIT_BYTES)


def _sigmoid(x):
    return 1.0 / (1.0 + jnp.exp(-x))


def _silu(x):
    return x * _sigmoid(x)


def _gelu_tanh(x):
    c = 0.7978845608028654
    return 0.5 * x * (1.0 + jnp.tanh(c * (x + 0.044715 * (x * x * x))))


def _rms(x, g):
    return x * lax.rsqrt(jnp.mean(x * x, axis=-1, keepdims=True) + EPS) * g


def _dot(a, b):
    if b.dtype == F32:
        return jnp.dot(a.astype(F32), b, preferred_element_type=F32,
                       precision=lax.Precision.HIGHEST)
    return jnp.dot(a.astype(BF16), b, preferred_element_type=F32)


def _dot_nt(a, b):
    return lax.dot_general(a, b, (((1,), (1,)), ((), ())), preferred_element_type=F32)


def _dot_tn(a, b):
    return lax.dot_general(a, b, (((0,), (0,)), ((), ())), preferred_element_type=F32)


def _split3(x):
    x1 = x.astype(BF16)
    r1 = x - x1.astype(F32)
    x2 = r1.astype(BF16)
    x3 = (r1 - x2.astype(F32)).astype(BF16)
    return x1, x2, x3


def _linear_kernel(*refs, has_gain, has_res):
    it = iter(refs)
    x_ref = next(it)
    g_ref = next(it) if has_gain else None
    w_ref = next(it)
    r_ref = next(it) if has_res else None
    o_ref = next(it)
    h_scr = next(it)

    @pl.when(pl.program_id(1) == 0)
    def _():
        x = x_ref[...].astype(F32)
        if has_gain:
            x = _rms(x, g_ref[...])
        h_scr[...] = x.astype(h_scr.dtype)

    acc = _dot(h_scr[...], w_ref[...])
    if has_res:
        acc = acc + r_ref[...]
    o_ref[...] = acc.astype(o_ref.dtype)


def _linear(x, w, layer, gain=None, res=None, tm=1024, tn=1024):
    m, k = x.shape
    n = w.shape[2]
    tm = min(tm, m)
    tn = min(tn, n)
    in_specs = [pl.BlockSpec((tm, k), lambda i, j: (i, 0))]
    args = [x]
    if gain is not None:
        in_specs.append(pl.BlockSpec((1, k), lambda i, j: (0, 0)))
        args.append(gain.reshape(1, k))
    in_specs.append(pl.BlockSpec((None, k, tn), lambda i, j: (layer, 0, j)))
    args.append(w)
    if res is not None:
        in_specs.append(pl.BlockSpec((tm, tn), lambda i, j: (i, j)))
        args.append(res)
    return pl.pallas_call(
        functools.partial(_linear_kernel, has_gain=gain is not None, has_res=res is not None),
        out_shape=jax.ShapeDtypeStruct((m, n), F32),
        grid=(m // tm, n // tn),
        in_specs=in_specs,
        out_specs=pl.BlockSpec((tm, tn), lambda i, j: (i, j)),
        scratch_shapes=[pltpu.VMEM((tm, k), w.dtype)],
        compiler_params=_cparams(("parallel", "arbitrary")),
        name="linear",
    )(*args)


def _ffn_kernel(te_ref, nv_ref, *refs, has_gain, has_res, tf, ck):
    del te_ref
    it = iter(refs)
    x_ref = next(it)
    g_ref = next(it) if has_gain else None
    w1_ref, w3_ref, w2_ref, o_ref, xb_scr, a_scr, acc_scr = it
    i = pl.program_id(0)
    j = pl.program_id(1)
    nj = pl.num_programs(1)
    valid = i < nv_ref[0]

    @pl.when(valid)
    def _():
        @pl.when(j == 0)
        def _():
            x = x_ref[...]
            if has_gain:
                x = _rms(x, g_ref[...])
            xb_scr[...] = x.astype(xb_scr.dtype)

        xb = xb_scr[...]
        for c in range(tf // ck):
            h1 = _dot(xb, w1_ref[0, :, c * ck:(c + 1) * ck])
            h3 = _dot(xb, w3_ref[0, :, c * ck:(c + 1) * ck])
            a_scr[:, c * ck:(c + 1) * ck] = (_silu(h1) * h3).astype(a_scr.dtype)
        part = _dot(a_scr[...], w2_ref[0])

        @pl.when(j == 0)
        def _():
            acc_scr[...] = part

        @pl.when(j > 0)
        def _():
            acc_scr[...] += part

        @pl.when(j == nj - 1)
        def _():
            out = acc_scr[...]
            if has_res:
                out = x_ref[...] + out
            o_ref[...] = out

    @pl.when(jnp.logical_and(jnp.logical_not(valid), j == nj - 1))
    def _():
        o_ref[...] = jnp.zeros_like(o_ref)


def _ffn(x, w1, w3, w2, tile_expert, n_valid, gain=None, res=False, tm=512, n_f=2, ck=256):
    p, d = x.shape
    f = w1.shape[2]
    tm = min(tm, p)
    tf = f // n_f
    n_tiles = p // tm
    in_specs = [pl.BlockSpec((tm, d), lambda i, j, te, nv: (i, 0))]
    args = [x]
    if gain is not None:
        in_specs.append(pl.BlockSpec((1, d), lambda i, j, te, nv: (0, 0)))
        args.append(gain.reshape(1, d))
    in_specs += [
        pl.BlockSpec((1, d, tf), lambda i, j, te, nv: (te[i], 0, j)),
        pl.BlockSpec((1, d, tf), lambda i, j, te, nv: (te[i], 0, j)),
        pl.BlockSpec((1, tf, d), lambda i, j, te, nv: (te[i], j, 0)),
    ]
    args += [w1, w3, w2]
    grid_spec = pltpu.PrefetchScalarGridSpec(
        num_scalar_prefetch=2,
        grid=(n_tiles, n_f),
        in_specs=in_specs,
        out_specs=pl.BlockSpec((tm, d), lambda i, j, te, nv: (i, 0)),
        scratch_shapes=[pltpu.VMEM((tm, d), w1.dtype), pltpu.VMEM((tm, tf), w1.dtype),
                        pltpu.VMEM((tm, d), F32)],
    )
    return pl.pallas_call(
        functools.partial(_ffn_kernel, has_gain=gain is not None, has_res=res, tf=tf, ck=ck),
        out_shape=jax.ShapeDtypeStruct((p, d), F32),
        grid_spec=grid_spec,
        compiler_params=_cparams(("arbitrary", "arbitrary")),
        name="swiglu",
    )(tile_expert, n_valid, *args)


def _hgrn_gates(fz, loglb, l1mlb, oneml):
    e = jnp.exp(-jnp.abs(fz))
    ope = 1.0 + e
    log_sig = jnp.minimum(fz, 0.0) - jnp.log(ope)
    cc = l1mlb + log_sig
    log_f = jnp.maximum(loglb, cc) + jnp.log(1.0 + jnp.exp(-jnp.abs(loglb - cc)))
    k = oneml * (jnp.where(fz > 0.0, e, 1.0) / ope)
    return log_f, k


def _level_reference(b, w):
    n = b.shape[0]
    if w >= 4:
        return jnp.concatenate(
            [jnp.broadcast_to(b[j + w:j + w + 1, :], (2 * w, b.shape[1]))
             for j in range(0, n, 2 * w)], axis=0)
    b3 = b.reshape(n // SUBLANES, SUBLANES, b.shape[1])
    ahead = lambda s: pltpu.roll(b3, SUBLANES - s, 1)
    pos = lax.broadcasted_iota(I32, (1, SUBLANES, 1), 1) & (2 * w - 1)
    if w == 2:
        r3 = jnp.where(pos == 0, ahead(2),
                       jnp.where(pos == 1, ahead(1),
                                 jnp.where(pos == 2, b3, pltpu.roll(b3, 1, 1))))
    else:
        r3 = jnp.where(pos == 0, ahead(1), b3)
    return r3.reshape(b.shape)


def _lru_gates(xc, wra, bra, wix, bix, c8):
    r = _sigmoid(_dot(xc, wra) + bra)
    ig = _sigmoid(_dot(xc, wix) + bix)
    log_a = c8 * r
    a = jnp.exp(log_a)
    one_m_a2 = -jnp.tanh(log_a) * (a * a + 1.0)
    u = jnp.sqrt(one_m_a2) * (ig * xc)
    return a, u


def _mixer_block_kernel(x_ref, g1_ref, win_ref, lbp_ref, hgg_ref, cw_ref, cb_ref, wra_ref,
                        bra_ref, wix_ref, bix_ref, c8_ref, cm_ref, wout_ref,
                        xo_ref, hs_ref, ls_ref, cs_ref,
                        st_scr, h_scr, xpad_scr, hb_scr, q_scr, lf_scr, k_scr, v_scr, g_scr,
                        gl_scr, a_scr, u_scr, m_scr, *, tl):
    l = pl.program_id(1)
    nl = pl.num_programs(1)

    @pl.when(l == 0)
    def _():
        st_scr[...] = jnp.zeros_like(st_scr)
        h_scr[...] = jnp.zeros_like(h_scr)
        xpad_scr[0:8, :] = jnp.zeros((8, LRU_WIDTH), F32)

    hb_scr[...] = _rms(x_ref[0], g1_ref[...]).astype(BF16)
    hb = hb_scr[...]
    zg = lambda j: _dot(hb, win_ref[:, j * HG_WIDTH:(j + 1) * HG_WIDTH])
    q_scr[...] = _silu(zg(0))
    log_f, kk = _hgrn_gates(zg(1), lbp_ref[0:1, :], lbp_ref[1:2, :], lbp_ref[2:3, :])
    lf_scr[...] = log_f
    k_scr[...] = kk
    v_scr[...] = zg(2)
    g_scr[...] = _silu(zg(3))
    xpad_scr[8:8 + tl, :] = zg(4)
    gl_scr[...] = _gelu_tanh(zg(5))
    xc = cb_ref[...]
    for j in range(CONV_K):
        xc = xc + xpad_scr[5 + j:5 + j + tl, :] * cw_ref[j:j + 1, :]
    tail = xpad_scr[tl + 5:tl + 8, :]
    xpad_scr[5:8, :] = tail
    a, u = _lru_gates(xc, wra_ref[...], bra_ref[...], wix_ref[...], bix_ref[...], c8_ref[...])
    a_scr[...] = a
    u_scr[...] = u

    row = lax.broadcasted_iota(I32, (CHUNK, 1), 0)
    ti = lax.broadcasted_iota(I32, (CHUNK, CHUNK), 0)
    si = lax.broadcasted_iota(I32, (CHUNK, CHUNK), 1)
    t3 = lax.broadcasted_iota(I32, (1, DIAG, 1), 1)
    sub3 = lax.broadcasted_iota(I32, (1, SUBLANES, 1), 1)
    nblk = CHUNK // DIAG
    ngrp = CHUNK // SUBLANES
    levels = []
    w = CHUNK // 2
    while w >= DIAG:
        sh = (2 * w).bit_length() - 1
        pairs = ((ti >> sh) == (si >> sh)) & ((ti & (2 * w - 1)) >= w) & ((si & (2 * w - 1)) < w)
        levels.append((w, (row & (2 * w - 1)) >= w, pairs))
        w //= 2

    def chunk_body(c, carry):
        r0 = pl.multiple_of(c * CHUNK, CHUNK)
        rows = pl.ds(r0, CHUNK)

        for gi in range(LRU_WIDTH // LANES):
            cs = slice(gi * LANES, (gi + 1) * LANES)
            aa = a_scr[rows, cs].reshape(ngrp, SUBLANES, LANES)
            uu = u_scr[rows, cs].reshape(ngrp, SUBLANES, LANES)
            d = 1
            while d < SUBLANES:
                keep = sub3 >= d
                a_sh = pltpu.roll(aa, d, 1)
                u_sh = pltpu.roll(uu, d, 1)
                uu = jnp.where(keep, aa * u_sh + uu, uu)
                aa = jnp.where(keep, aa * a_sh, aa)
                d *= 2
            hprev = h_scr[:, cs]
            groups = []
            for j in range(ngrp):
                hj = uu[j] + aa[j] * hprev
                groups.append(hj)
                hprev = hj[SUBLANES - 1:SUBLANES, :]
            hs = jnp.concatenate(groups, axis=0)
            h_scr[:, cs] = hprev
            m_scr[rows, HG_WIDTH + gi * LANES:HG_WIDTH + (gi + 1) * LANES] = (
                hs * gl_scr[rows, cs]).astype(BF16)

        q = q_scr[rows, :]
        k = k_scr[rows, :]
        v = v_scr[rows, :]
        f1, f2, f3 = _split3(lf_scr[rows, :])
        cm = cm_ref[...]
        b = _dot(cm, f1) + _dot(cm, f2) + _dot(cm, f3)

        for hh in range(HG_HEADS):
            cs = slice(hh * HG_D, (hh + 1) * HG_D)
            qh, kh, vh, bh = q[:, cs], k[:, cs], v[:, cs], b[:, cs]
            vb = vh.astype(BF16)
            amat = jnp.zeros((CHUNK, CHUNK), F32)
            for w, upper, pairs in levels:
                dec = jnp.exp(-jnp.abs(bh - _level_reference(bh, w)))
                y = (jnp.where(upper, qh, kh) * dec).astype(BF16)
                amat = jnp.where(pairs, _dot_nt(y, y), amat)
            o = _dot(amat.astype(BF16), vb)
            st = st_scr[hh]
            o = o + _dot_nt((qh * jnp.exp(bh)).astype(BF16), st.astype(BF16))
            if DIAG == 1:
                o = o + jnp.sum(qh * kh, axis=-1, keepdims=True) * vh
            else:
                q3 = qh.reshape(nblk, DIAG, HG_D)
                k3 = kh.reshape(nblk, DIAG, HG_D)
                v3 = vh.reshape(nblk, DIAG, HG_D)
                b3 = bh.reshape(nblk, DIAG, HG_D)
                od = jnp.zeros((nblk, DIAG, HG_D), F32)
                for s in range(DIAG):
                    dec = jnp.exp(jnp.minimum(b3 - b3[:, s:s + 1, :], 0.0))
                    p = q3 * (k3[:, s:s + 1, :] * dec)
                    rs = jnp.sum(p, axis=-1, keepdims=True)
                    rs = jnp.where(t3 >= s, rs, 0.0)
                    od = od + rs * v3[:, s:s + 1, :]
                o = o + od.reshape(CHUNK, HG_D)
            bl = bh[CHUNK - 1:CHUNK, :]
            kdec = (kh * jnp.exp(bl - bh)).astype(BF16)
            st_scr[hh] = st * jnp.exp(bl) + _dot_tn(vb, kdec)
            o = o * lax.rsqrt(jnp.mean(o * o, axis=-1, keepdims=True) + EPS)
            o = o * hgg_ref[:, cs]
            o = o * g_scr[rows, cs]
            m_scr[rows, cs] = o.astype(BF16)
        return carry

    lax.fori_loop(0, tl // CHUNK, chunk_body, 0, unroll=4)
    xo_ref[0] = x_ref[0] + _dot(m_scr[...], wout_ref[...])

    @pl.when(l == nl - 1)
    def _():
        for hh in range(HG_HEADS):
            hs_ref[0, hh] = st_scr[hh].T
        ls_ref[0] = h_scr[...]
        cs_ref[0] = xpad_scr[5:8, :]


def _mixer_block(x, gain, w_in, w_out, layer, mp, tl=512):
    bsz, seq, d = x.shape
    tl = min(tl, seq)
    full = lambda shape: pl.BlockSpec(shape, lambda b, l: (0,) * len(shape))
    stacked = lambda shape: pl.BlockSpec((None,) + shape, lambda b, l: (layer, 0, 0))
    out_shapes = (
        jax.ShapeDtypeStruct((bsz, seq, d), F32),
        jax.ShapeDtypeStruct((bsz, HG_HEADS, HG_D, HG_D), F32),
        jax.ShapeDtypeStruct((bsz, 1, LRU_WIDTH), F32),
        jax.ShapeDtypeStruct((bsz, CONV_K - 1, LRU_WIDTH), F32),
    )
    half = lambda dt: pltpu.VMEM((tl, HG_WIDTH), dt)
    xo, hg, lru, conv = pl.pallas_call(
        functools.partial(_mixer_block_kernel, tl=tl),
        out_shape=out_shapes,
        grid=(bsz, seq // tl),
        in_specs=[
            pl.BlockSpec((1, tl, d), lambda b, l: (b, l, 0)),
            full((1, d)), stacked((d, IN_WIDTH)),
            full((3, HG_WIDTH)), full((1, HG_WIDTH)), full((CONV_K, LRU_WIDTH)),
            full((1, LRU_WIDTH)), full((LRU_WIDTH, LRU_WIDTH)), full((1, LRU_WIDTH)),
            full((LRU_WIDTH, LRU_WIDTH)), full((1, LRU_WIDTH)), full((1, LRU_WIDTH)),
            full(mp["cm"].shape), stacked((d, d)),
        ],
        out_specs=(
            pl.BlockSpec((1, tl, d), lambda b, l: (b, l, 0)),
            pl.BlockSpec((1, HG_HEADS, HG_D, HG_D), lambda b, l: (b, 0, 0, 0)),
            pl.BlockSpec((1, 1, LRU_WIDTH), lambda b, l: (b, 0, 0)),
            pl.BlockSpec((1, CONV_K - 1, LRU_WIDTH), lambda b, l: (b, 0, 0)),
        ),
        scratch_shapes=[
            pltpu.VMEM((HG_HEADS, HG_D, HG_D), F32),
            pltpu.VMEM((1, LRU_WIDTH), F32),
            pltpu.VMEM((tl + 8, LRU_WIDTH), F32),
            pltpu.VMEM((tl, d), BF16),
            half(F32), half(F32), half(F32), half(F32), half(F32), half(F32), half(F32),
            half(F32),
            pltpu.VMEM((tl, d), BF16),
        ],
        compiler_params=_cparams(("parallel", "arbitrary")),
        name="mixer_block",
    )(x, gain.reshape(1, d), w_in, mp["lbp"], mp["hgg"], mp["cw"], mp["cb"], mp["wra"],
      mp["bra"], mp["wix"], mp["bix"], mp["c8"], mp["cm"], w_out)
    return xo, hg, lru.reshape(bsz, LRU_WIDTH), conv


def _mixer_step_kernel(z_ref, sh_ref, sl_ref, sc_ref, lbp_ref, hgg_ref, cw_ref, cb_ref, wra_ref,
                       bra_ref, wix_ref, bix_ref, c8_ref, *rest, bb, aliased):
    m_ref, hn_ref, ln_ref, cn_ref, o_scr = rest[1:] if aliased else rest
    z = z_ref[...]
    xr = z[:, 4 * HG_WIDTH:4 * HG_WIDTH + LRU_WIDTH]
    gr = z[:, 4 * HG_WIDTH + LRU_WIDTH:IN_WIDTH]
    buf = sc_ref[...]
    xc = cb_ref[...]
    for j in range(CONV_K - 1):
        xc = xc + buf[:, j * LRU_WIDTH:(j + 1) * LRU_WIDTH] * cw_ref[j:j + 1, :]
    xc = xc + xr * cw_ref[CONV_K - 1:CONV_K, :]
    cn_ref[:, 0:(CONV_K - 2) * LRU_WIDTH] = buf[:, LRU_WIDTH:(CONV_K - 1) * LRU_WIDTH]
    cn_ref[:, (CONV_K - 2) * LRU_WIDTH:(CONV_K - 1) * LRU_WIDTH] = xr
    a, u = _lru_gates(xc, wra_ref[...], bra_ref[...], wix_ref[...], bix_ref[...], c8_ref[...])
    hnew = u + a * sl_ref[...]
    ln_ref[...] = hnew
    m_ref[:, HG_WIDTH:D_MODEL] = (hnew * _gelu_tanh(gr)).astype(m_ref.dtype)

    q = _silu(z[:, 0:HG_WIDTH])
    log_f, k = _hgrn_gates(z[:, HG_WIDTH:2 * HG_WIDTH], lbp_ref[0:1, :], lbp_ref[1:2, :],
                           lbp_ref[2:3, :])
    f = jnp.exp(log_f)
    v = z[:, 2 * HG_WIDTH:3 * HG_WIDTH]
    gate = _silu(z[:, 3 * HG_WIDTH:4 * HG_WIDTH])
    zpad = jnp.zeros((HG_D - bb, HG_D), F32)
    for hh in range(HG_HEADS):
        cs = slice(hh * HG_D, (hh + 1) * HG_D)
        ft = jnp.concatenate([f[:, cs], zpad], axis=0).T
        kt = jnp.concatenate([k[:, cs], zpad], axis=0).T
        qt = jnp.concatenate([q[:, cs], zpad], axis=0).T
        for j in range(bb):
            fcol = jnp.broadcast_to(ft[:, j:j + 1], (HG_D, HG_D))
            kcol = jnp.broadcast_to(kt[:, j:j + 1], (HG_D, HG_D))
            qcol = jnp.broadcast_to(qt[:, j:j + 1], (HG_D, HG_D))
            vrow = jnp.broadcast_to(v[j:j + 1, cs], (HG_D, HG_D))
            sn = fcol * sh_ref[j, hh] + kcol * vrow
            hn_ref[j, hh] = sn
            o_scr[j:j + 1, cs] = jnp.sum(qcol * sn, axis=0, keepdims=True)
    for hh in range(HG_HEADS):
        cs = slice(hh * HG_D, (hh + 1) * HG_D)
        o = o_scr[:, cs]
        o = o * lax.rsqrt(jnp.mean(o * o, axis=-1, keepdims=True) + EPS)
        o = o * hgg_ref[:, cs]
        o = o * gate[:, cs]
        m_ref[:, cs] = o.astype(m_ref.dtype)


def _mixer_step(z, s_hg_all, layer, s_lru, s_conv, mp, hg_buf=None, bb=16):
    bsz = z.shape[0]
    bb = min(bb, bsz)
    cw3 = (CONV_K - 1) * LRU_WIDTH
    full = lambda shape: pl.BlockSpec(shape, lambda i: (0,) * len(shape))
    state_spec = pl.BlockSpec((None, bb, HG_HEADS, HG_D, HG_D), lambda i: (layer, i, 0, 0, 0))
    out_shapes = (
        jax.ShapeDtypeStruct((bsz, D_MODEL), mp["wra"].dtype),
        jax.ShapeDtypeStruct(s_hg_all.shape, F32),
        jax.ShapeDtypeStruct((bsz, LRU_WIDTH), F32),
        jax.ShapeDtypeStruct((bsz, cw3), F32),
    )
    in_specs = [
        pl.BlockSpec((bb, IN_WIDTH), lambda i: (i, 0)),
        state_spec,
        pl.BlockSpec((bb, LRU_WIDTH), lambda i: (i, 0)),
        pl.BlockSpec((bb, cw3), lambda i: (i, 0)),
        full((3, HG_WIDTH)), full((1, HG_WIDTH)), full((CONV_K, LRU_WIDTH)),
        full((1, LRU_WIDTH)), full((LRU_WIDTH, LRU_WIDTH)), full((1, LRU_WIDTH)),
        full((LRU_WIDTH, LRU_WIDTH)), full((1, LRU_WIDTH)), full((1, LRU_WIDTH)),
    ]
    args = [z, s_hg_all, s_lru, s_conv.reshape(bsz, cw3), mp["lbp"], mp["hgg"], mp["cw"],
            mp["cb"], mp["wra"], mp["bra"], mp["wix"], mp["bix"], mp["c8"]]
    aliases = {}
    if hg_buf is not None:
        aliases = {len(args): 1}
        in_specs.append(pl.BlockSpec(memory_space=pl.ANY))
        args.append(hg_buf)
    merged, hg, lru, conv = pl.pallas_call(
        functools.partial(_mixer_step_kernel, bb=bb, aliased=hg_buf is not None),
        out_shape=out_shapes,
        grid=(bsz // bb,),
        in_specs=in_specs,
        out_specs=(
            pl.BlockSpec((bb, D_MODEL), lambda i: (i, 0)),
            state_spec,
            pl.BlockSpec((bb, LRU_WIDTH), lambda i: (i, 0)),
            pl.BlockSpec((bb, cw3), lambda i: (i, 0)),
        ),
        scratch_shapes=[pltpu.VMEM((bb, HG_WIDTH), F32)],
        input_output_aliases=aliases,
        compiler_params=_cparams(("parallel",)),
        name="mixer_step",
    )(*args)
    return merged, hg, lru, conv.reshape(bsz, CONV_K - 1, LRU_WIDTH)


def _router_kernel(x_ref, g_ref, rw_ref, tril_ref, info_ref, idx_ref, cnt_ref, carry_scr):
    i = pl.program_id(0)

    @pl.when(i == 0)
    def _():
        carry_scr[...] = jnp.zeros_like(carry_scr)

    h = _rms(x_ref[...], g_ref[...])
    logits = jnp.dot(h, rw_ref[...], preferred_element_type=F32, precision=lax.Precision.HIGHEST)
    lane = lax.broadcasted_iota(I32, logits.shape, 1)
    neg = jnp.float32(-jnp.inf)
    logits = jnp.where(lane < N_EXPERTS, logits, neg)
    m1 = jnp.max(logits, axis=-1, keepdims=True)
    i1 = jnp.min(jnp.where(logits == m1, lane, LANES), axis=-1, keepdims=True)
    l2 = jnp.where(lane == i1, neg, logits)
    m2 = jnp.max(l2, axis=-1, keepdims=True)
    i2 = jnp.min(jnp.where(l2 == m2, lane, LANES), axis=-1, keepdims=True)
    e = jnp.exp(m2 - m1)
    p1 = 1.0 / (1.0 + e)
    p2 = e / (1.0 + e)
    oh1 = (lane == i1).astype(F32)
    oh2 = (lane == i2).astype(F32)
    sel = oh1 + oh2
    before = _dot(tril_ref[...], sel.astype(BF16)) + carry_scr[...]
    r1 = jnp.sum(oh1 * before, axis=-1, keepdims=True)
    r2 = jnp.sum(oh2 * before, axis=-1, keepdims=True)
    carry_scr[...] += jnp.sum(sel, axis=0, keepdims=True)
    info = jnp.where(lane == 0, i1.astype(F32), 0.0)
    info = jnp.where(lane == 1, i2.astype(F32), info)
    info = jnp.where(lane == 2, r1, info)
    info = jnp.where(lane == 3, r2, info)
    info = jnp.where(lane == 4, p1, info)
    info = jnp.where(lane == 5, p2, info)
    info_ref[...] = info
    idx_ref[...] = info.T[0:SUBLANES, :].astype(I32)
    cnt_ref[...] = carry_scr[...]


def _router(x, gain, rw_pad, tm=512):
    t, d = x.shape
    tm = min(tm, t)
    tril = jnp.tril(jnp.ones((tm, tm), BF16), -1)
    return pl.pallas_call(
        _router_kernel,
        out_shape=(jax.ShapeDtypeStruct((t, LANES), F32),
                   jax.ShapeDtypeStruct((SUBLANES, t), I32),
                   jax.ShapeDtypeStruct((1, LANES), F32)),
        grid=(t // tm,),
        in_specs=[
            pl.BlockSpec((tm, d), lambda i: (i, 0)),
            pl.BlockSpec((1, d), lambda i: (0, 0)),
            pl.BlockSpec((d, LANES), lambda i: (0, 0)),
            pl.BlockSpec((tm, tm), lambda i: (0, 0)),
        ],
        out_specs=(pl.BlockSpec((tm, LANES), lambda i: (i, 0)),
                   pl.BlockSpec((SUBLANES, tm), lambda i: (0, i)),
                   pl.BlockSpec((1, LANES), lambda i: (0, 0))),
        scratch_shapes=[pltpu.VMEM((1, LANES), F32)],
        compiler_params=_cparams(("arbitrary",)),
        name="router",
    )(x, gain.reshape(1, d), rw_pad, tril)


def _row_copy(src, dst, s, d, sem):
    return pltpu.make_async_copy(src.at[pl.ds(s, 1)], dst.at[pl.ds(d, 1)], sem)


def _tile_dest(dest, t, td):
    nb = t // td
    dest3 = jnp.concatenate([dest[0].reshape(nb, td), dest[1].reshape(nb, td)], axis=1)
    return dest3.reshape(nb, 1, 2 * td)


def _dispatch_kernel(dest_ref, x_ref, g_ref, xs_in_ref, xs_ref, pk_scr, sem, *, td):
    del xs_in_ref
    pk_scr[...] = _rms(x_ref[...], g_ref[...])

    def issue(r, c):
        _row_copy(pk_scr, xs_ref, r, dest_ref[0, 0, r], sem).start(priority=0)
        _row_copy(pk_scr, xs_ref, r, dest_ref[0, 0, td + r], sem).start(priority=1)
        return c

    lax.fori_loop(0, td, issue, 0, unroll=8)
    for _ in range(2):
        pltpu.make_async_copy(pk_scr, xs_ref.at[pl.ds(0, td)], sem).wait()


def _dispatch(x, gain, dest, n_rows, td=512):
    t, d = x.shape
    td = min(td, t)
    xs0 = jnp.zeros((n_rows, d), F32)
    return pl.pallas_call(
        functools.partial(_dispatch_kernel, td=td),
        out_shape=jax.ShapeDtypeStruct((n_rows, d), F32),
        grid=(t // td,),
        in_specs=[
            pl.BlockSpec((1, 1, 2 * td), lambda i: (i, 0, 0), memory_space=pltpu.SMEM),
            pl.BlockSpec((td, d), lambda i: (i, 0)),
            pl.BlockSpec((1, d), lambda i: (0, 0)),
            pl.BlockSpec(memory_space=pl.ANY),
        ],
        out_specs=pl.BlockSpec(memory_space=pl.ANY),
        scratch_shapes=[pltpu.VMEM((td, d), F32), pltpu.SemaphoreType.DMA(())],
        input_output_aliases={3: 0},
        compiler_params=pltpu.CompilerParams(dimension_semantics=("arbitrary",),
                                             vmem_limit_bytes=VMEM_LIMIT_BYTES,
                                             has_side_effects=True),
        name="dispatch",
    )(_tile_dest(dest, t, td), x, gain.reshape(1, d), xs0)


def _combine_kernel(dest_ref, nxt_ref, x_ref, info_ref, g_ref, ys_ref, o_ref, y_scr, sems, *, td):
    i = pl.program_id(0)
    nb = pl.num_programs(0)

    def fetch(idx_ref, slot):
        def issue(r, c):
            _row_copy(ys_ref, y_scr.at[slot, 0], idx_ref[0, 0, r], r,
                      sems.at[slot]).start(priority=0)
            _row_copy(ys_ref, y_scr.at[slot, 1], idx_ref[0, 0, td + r], r,
                      sems.at[slot]).start(priority=1)
            return c

        lax.fori_loop(0, td, issue, 0, unroll=8)

    def finish(slot):
        for j in range(2):
            pltpu.make_async_copy(ys_ref.at[pl.ds(0, td)], y_scr.at[slot, j],
                                  sems.at[slot]).wait()
        p1 = info_ref[:, 4:5]
        p2 = info_ref[:, 5:6]
        x = x_ref[...] + (p1 * y_scr[slot, 0] + p2 * y_scr[slot, 1])
        o_ref[...] = _rms(x, g_ref[...])

    @pl.when(i == 0)
    def _():
        fetch(dest_ref, 0)

    for slot in range(2):
        @pl.when(i % 2 == slot)
        def _(slot=slot):
            @pl.when(i + 1 < nb)
            def _():
                fetch(nxt_ref, 1 - slot)

            finish(slot)


def _combine(x, ys, dest, info, gain, td=256):
    t, d = x.shape
    td = min(td, t)
    nb = t // td
    row = pl.BlockSpec((td, d), lambda i: (i, 0))
    tiled = _tile_dest(dest, t, td)
    return pl.pallas_call(
        functools.partial(_combine_kernel, td=td),
        out_shape=jax.ShapeDtypeStruct((t, d), F32),
        grid=(nb,),
        in_specs=[
            pl.BlockSpec((1, 1, 2 * td), lambda i: (i, 0, 0), memory_space=pltpu.SMEM),
            pl.BlockSpec((1, 1, 2 * td), lambda i: (jnp.minimum(i + 1, nb - 1), 0, 0),
                         memory_space=pltpu.SMEM),
            row,
            pl.BlockSpec((td, LANES), lambda i: (i, 0)),
            pl.BlockSpec((1, d), lambda i: (0, 0)),
            pl.BlockSpec(memory_space=pl.ANY),
        ],
        out_specs=row,
        scratch_shapes=[pltpu.VMEM((2, 2, td, d), F32), pltpu.SemaphoreType.DMA((2,))],
        compiler_params=_cparams(("arbitrary",)),
        name="combine",
    )(tiled, tiled, x, info, gain.reshape(1, d), ys)


def _moe(x, norm_g, rw_pad, w1, w3, w2, final_g, tm_e):
    t, d = x.shape
    info, idx, cnt = _router(x, norm_g, rw_pad)
    e1, e2, r1, r2 = idx[0], idx[1], idx[2], idx[3]
    counts = cnt[0, :N_EXPERTS].astype(I32)
    padded = ((counts + tm_e - 1) // tm_e) * tm_e
    ends = jnp.cumsum(padded)
    starts = ends - padded
    dest = jnp.stack([starts[e1] + r1, starts[e2] + r2])
    n_tiles = (2 * t + N_EXPERTS * (tm_e - 1)) // tm_e
    n_rows = n_tiles * tm_e
    tile_start = jnp.arange(n_tiles, dtype=I32) * tm_e
    tile_expert = jnp.minimum(
        jnp.sum((tile_start[:, None] >= ends[None, :]).astype(I32), axis=1), N_EXPERTS - 1)
    n_valid = (ends[-1] // tm_e).reshape(1).astype(I32)
    xs = _dispatch(x, norm_g, dest, n_rows)
    ys = _ffn(xs, w1, w3, w2, tile_expert.astype(I32), n_valid, tm=tm_e, n_f=2)
    return _combine(x, ys, dest, info, final_g)


def _block_diag(w):
    n, c, d = w.shape
    eye = jnp.eye(n, dtype=w.dtype)
    return (w[:, :, None, :] * eye[:, None, :, None]).reshape(n * c, n * d)


def _cum_matrix():
    t = jnp.arange(CHUNK)
    return (t[None, :] <= t[:, None]).astype(BF16)


def _mixer_params(l, lb, hg_norm_g, conv_w, conv_b, w_ra, b_ra, w_ix, b_ix, lru_lambda, wdtype):
    row = lambda a: a.reshape(1, -1).astype(F32)
    return {
        "lbp": jnp.stack([jnp.log(lb), jnp.log1p(-lb), 1.0 - lb]).astype(F32),
        "hgg": row(hg_norm_g[l]),
        "cw": conv_w[l].astype(F32),
        "cb": row(conv_b[l]),
        "wra": _block_diag(w_ra[l].astype(F32)).astype(wdtype),
        "bra": row(b_ra[l]),
        "wix": _block_diag(w_ix[l].astype(F32)).astype(wdtype),
        "bix": row(b_ix[l]),
        "c8": row(-LRU_C * jax.nn.softplus(-lru_lambda[l].astype(F32))),
        "cm": _cum_matrix(),
    }


def _trunk(x, seq_shape, states, prm):
    bsz, seq = seq_shape
    one = jnp.ones((1,), I32)
    new_hg, new_lru, new_conv = [], [], []
    hg_stack = None if states is None else states[0]
    for l in range(2):
        mp = prm["mixer"][l]
        if states is None:
            x, s1, s2, s3 = _mixer_block(x.reshape(bsz, seq, D_MODEL), prm["norm1_g"][l],
                                         prm["w_in"], prm["w_out"], l, mp)
            x = x.reshape(bsz * seq, D_MODEL)
            new_hg.append(s1)
        else:
            z = _linear(x, prm["w_in"], l, gain=prm["norm1_g"][l])
            merged, hg_stack, s2, s3 = _mixer_step(z, states[0], l, states[1][l], states[2][l],
                                                   mp, hg_buf=hg_stack)
            x = _linear(merged, prm["w_out"], l, res=x)
        new_lru.append(s2)
        new_conv.append(s3)
        if l == 0:
            n_tiles = max(x.shape[0] // 512, 1)
            n_f = 1 if prm["ffn_w1"].dtype == BF16 else prm["ffn_w1"].shape[2] // 256
            x = _ffn(x, prm["ffn_w1"], prm["ffn_w3"], prm["ffn_w2"], jnp.zeros((n_tiles,), I32),
                     one * n_tiles, gain=prm["norm2_g"][0], res=True, tm=512, n_f=n_f)
        else:
            x = _moe(x, prm["norm2_g"][1], prm["rw_pad"], prm["moe_w1"], prm["moe_w3"],
                     prm["moe_w2"], prm["final_norm_g"], tm_e=512 if states is None else 128)
    hg_all = jnp.stack(new_hg) if states is None else hg_stack
    return x, hg_all, jnp.stack(new_lru), jnp.stack(new_conv)


def kernel(x_prompt, x_sample, state_hgrn, state_lru, state_conv, norm1_g, w_in, lower_bounds,
           hg_norm_g, conv_w, conv_b, w_ra, b_ra, w_ix, b_ix, lru_lambda, w_out, norm2_g,
           ffn_w1, ffn_w3, ffn_w2, router_w, moe_w1, moe_w3, moe_w2, final_norm_g):
    lb_all = jnp.cumsum(jax.nn.softmax(lower_bounds.astype(F32), axis=0), axis=0)
    lb_all = lb_all - lb_all[0]
    experts = {
        "moe_w1": moe_w1[0].astype(BF16), "moe_w3": moe_w3[0].astype(BF16),
        "moe_w2": moe_w2[0].astype(BF16),
        "rw_pad": jnp.pad(router_w[0].astype(F32), ((0, 0), (0, LANES - N_EXPERTS))),
        "norm1_g": norm1_g, "norm2_g": norm2_g, "final_norm_g": final_norm_g,
    }

    def params(wdtype):
        return dict(
            experts,
            mixer=[_mixer_params(l, lb_all[l], hg_norm_g, conv_w, conv_b, w_ra, b_ra, w_ix, b_ix,
                                 lru_lambda, wdtype) for l in range(2)],
            w_in=w_in.astype(wdtype), w_out=w_out.astype(wdtype),
            ffn_w1=ffn_w1.astype(wdtype), ffn_w3=ffn_w3.astype(wdtype),
            ffn_w2=ffn_w2.astype(wdtype))

    bp, sp, d = x_prompt.shape
    bs = x_sample.shape[0]
    y_p, hg_p, lru_p, conv_p = _trunk(x_prompt.reshape(bp * sp, d), (bp, sp), None, params(BF16))
    y_s, hg_s, lru_s, conv_s = _trunk(x_sample.reshape(bs, d), (bs, 1),
                                      (state_hgrn, state_lru, state_conv), params(F32))
    return (y_p.reshape(bp, sp, d), y_s.reshape(bs, 1, d), hg_p, lru_p, conv_p, hg_s, lru_s,
            conv_s)
```

```python
import functools

import jax
import jax.numpy as jnp
from jax import lax
from jax.experimental import pallas as pl
from jax.experimental.pallas import tpu as pltpu

F32 = jnp.float32
BF16 = jnp.bfloat16
I32 = jnp.int32

D_MODEL = 1024
HG_WIDTH = 512
HG_HEADS = 4
HG_D = 128
LRU_WIDTH = 512
LRU_BLOCKS = 8
LRU_C = 8.0
CONV_K = 4
IN_WIDTH = 4 * HG_WIDTH + 2 * LRU_WIDTH
N_EXPERTS = 8
EPS = 1e-6

CHUNK = 64
DIAG = 1
LANES = 128
SUBLANES = 8
VMEM_LIMIT_BYTES = 56 * 1024 * 1024


def _cparams(sem):
    return pltpu.CompilerParams(dimension_semantics=sem, vmem_limit_bytes=VMEM_LIMIT_BYTES)


def _sigmoid(x):
    return 1.0 / (1.0 + jnp.exp(-x))


def _silu(x):
    return x * _sigmoid(x)


def _gelu_tanh(x):
    c = 0.7978845608028654
    return 0.5 * x * (1.0 + jnp.tanh(c * (x + 0.044715 * (x * x * x))))


def _rms(x, g):
    return x * lax.rsqrt(jnp.mean(x * x, axis=-1, keepdims=True) + EPS) * g


def _dot(a, b):
    if b.dtype == F32:
        return jnp.dot(a.astype(F32), b, preferred_element_type=F32,
                       precision=lax.Precision.HIGHEST)
    return jnp.dot(a.astype(BF16), b, preferred_element_type=F32)


def _dot_nt(a, b):
    return lax.dot_general(a, b, (((1,), (1,)), ((), ())), preferred_element_type=F32)


def _dot_tn(a, b):
    return lax.dot_general(a, b, (((0,), (0,)), ((), ())), preferred_element_type=F32)


def _split3(x):
    x1 = x.astype(BF16)
    r1 = x - x1.astype(F32)
    x2 = r1.astype(BF16)
    x3 = (r1 - x2.astype(F32)).astype(BF16)
    return x1, x2, x3


def _linear_kernel(*refs, has_gain, has_res):
    it = iter(refs)
    x_ref = next(it)
    g_ref = next(it) if has_gain else None
    w_ref = next(it)
    r_ref = next(it) if has_res else None
    o_ref = next(it)
    h_scr = next(it)

    @pl.when(pl.program_id(1) == 0)
    def _():
        x = x_ref[...].astype(F32)
        if has_gain:
            x = _rms(x, g_ref[...])
        h_scr[...] = x.astype(h_scr.dtype)

    acc = _dot(h_scr[...], w_ref[...])
    if has_res:
        acc = acc + r_ref[...]
    o_ref[...] = acc.astype(o_ref.dtype)


def _linear(x, w, layer, gain=None, res=None, tm=1024, tn=1024):
    m, k = x.shape
    n = w.shape[2]
    tm = min(tm, m)
    tn = min(tn, n)
    in_specs = [pl.BlockSpec((tm, k), lambda i, j: (i, 0))]
    args = [x]
    if gain is not None:
        in_specs.append(pl.BlockSpec((1, k), lambda i, j: (0, 0)))
        args.append(gain.reshape(1, k))
    in_specs.append(pl.BlockSpec((None, k, tn), lambda i, j: (layer, 0, j)))
    args.append(w)
    if res is not None:
        in_specs.append(pl.BlockSpec((tm, tn), lambda i, j: (i, j)))
        args.append(res)
    return pl.pallas_call(
        functools.partial(_linear_kernel, has_gain=gain is not None, has_res=res is not None),
        out_shape=jax.ShapeDtypeStruct((m, n), F32),
        grid=(m // tm, n // tn),
        in_specs=in_specs,
        out_specs=pl.BlockSpec((tm, tn), lambda i, j: (i, j)),
        scratch_shapes=[pltpu.VMEM((tm, k), w.dtype)],
        compiler_params=_cparams(("parallel", "arbitrary")),
        name="linear",
    )(*args)


class _CastStream:
    PER_STEP = 4

    def __init__(self, srcs, dsts, bufs, sems, step):
        self.jobs = [(src, dst, step * self.PER_STEP + k)
                     for src, dst in zip(srcs, dsts) for k in range(self.PER_STEP)]
        self.bufs, self.sems = bufs, sems
        self.pending = {}
        self.done = 0

    def _copies(self, s):
        src, dst, slab = self.jobs[s]
        kind = 0 if src.shape[1] == self.bufs[0].shape[2] else 1
        fbuf, bbuf = self.bufs[2 * kind], self.bufs[2 * kind + 1]
        slot = s % 2
        rows = fbuf.shape[1]
        window = pl.ds(pl.multiple_of(slab * rows, rows), rows)
        fetch = pltpu.make_async_copy(src.at[window], fbuf.at[slot], self.sems.at[0, kind, slot])
        store = pltpu.make_async_copy(bbuf.at[slot], dst.at[window], self.sems.at[1, kind, slot])
        return fetch, store, fbuf, bbuf, slot, kind

    GROUP = 2

    def _fetch_group(self, first):
        for s in range(first, min(first + self.GROUP, len(self.jobs))):
            self._copies(s)[0].start()

    def start(self):
        self._fetch_group(0)

    def advance(self):
        group = range(self.done, min(self.done + self.GROUP, len(self.jobs)))
        for s in group:
            self._copies(s)[0].wait()
        for s in group:
            _, store, fbuf, bbuf, slot, kind = self._copies(s)
            if (kind, slot) in self.pending:
                self.pending.pop((kind, slot)).wait()
            bbuf[slot] = fbuf[slot].astype(BF16)
            store.start()
            self.pending[(kind, slot)] = store
        self.done += len(group)
        self._fetch_group(self.done)

    def finish(self):
        while self.done < len(self.jobs):
            self.advance()
        for store in self.pending.values():
            store.wait()
        self.pending = {}


def _ffn_kernel(te_ref, nv_ref, *refs, has_gain, has_res, n_cast, tf, ck):
    del te_ref
    it = iter(refs)
    x_ref = next(it)
    g_ref = next(it) if has_gain else None
    w1_ref, w3_ref, w2_ref = next(it), next(it), next(it)
    cast_src = [next(it) for _ in range(n_cast)]
    o_ref = next(it)
    cast_dst = [next(it) for _ in range(n_cast)]
    xb_scr, a_scr, acc_scr = next(it), next(it), next(it)
    cast_scr = list(it)
    i = pl.program_id(0)
    j = pl.program_id(1)
    nj = pl.num_programs(1)
    valid = i < nv_ref[0]

    @pl.when(valid)
    def _():
        @pl.when(j == 0)
        def _():
            x = x_ref[...]
            if has_gain:
                x = _rms(x, g_ref[...])
            xb_scr[...] = x.astype(xb_scr.dtype)

        stream = None
        if n_cast:
            stream = _CastStream(cast_src, cast_dst, cast_scr[:-1], cast_scr[-1], i)
            stream.start()
        xb = xb_scr[...]
        for c in range(tf // ck):
            h1 = _dot(xb, w1_ref[0, :, c * ck:(c + 1) * ck])
            h3 = _dot(xb, w3_ref[0, :, c * ck:(c + 1) * ck])
            a_scr[:, c * ck:(c + 1) * ck] = (_silu(h1) * h3).astype(a_scr.dtype)
            if stream is not None and c % 2 == 1:
                stream.advance()
        part = _dot(a_scr[...], w2_ref[0])
        if stream is not None:
            stream.finish()

        @pl.when(j == 0)
        def _():
            acc_scr[...] = part

        @pl.when(j > 0)
        def _():
            acc_scr[...] += part

        @pl.when(j == nj - 1)
        def _():
            out = acc_scr[...]
            if has_res:
                out = x_ref[...] + out
            o_ref[...] = out

    @pl.when(jnp.logical_and(jnp.logical_not(valid), j == nj - 1))
    def _():
        o_ref[...] = jnp.zeros_like(o_ref)


def _ffn(x, w1, w3, w2, tile_expert, n_valid, gain=None, res=False, cast=(), tm=512, n_f=2,
         ck=256):
    p, d = x.shape
    f = w1.shape[2]
    tm = min(tm, p)
    tf = f // n_f
    n_tiles = p // tm
    slabs = n_tiles * _CastStream.PER_STEP
    cast_scratch = []
    if cast:
        assert n_f == 1 and len({c.shape[1] for c in cast}) <= 2
        for width in dict.fromkeys(c.shape[1] for c in cast):
            rows = {c.shape[0] // slabs for c in cast if c.shape[1] == width}
            assert len(rows) == 1 and all(c.shape[0] % slabs == 0 for c in cast)
            cast_scratch += [pltpu.VMEM((2, rows.pop(), width), F32)]
            cast_scratch += [pltpu.VMEM(cast_scratch[-1].shape, BF16)]
        cast_scratch += [pltpu.SemaphoreType.DMA((2, 2, 2))]
    hbm = pl.BlockSpec(memory_space=pl.ANY)
    in_specs = [pl.BlockSpec((tm, d), lambda i, j, te, nv: (i, 0))]
    args = [x]
    if gain is not None:
        in_specs.append(pl.BlockSpec((1, d), lambda i, j, te, nv: (0, 0)))
        args.append(gain.reshape(1, d))
    in_specs += [
        pl.BlockSpec((1, d, tf), lambda i, j, te, nv: (te[i], 0, j)),
        pl.BlockSpec((1, d, tf), lambda i, j, te, nv: (te[i], 0, j)),
        pl.BlockSpec((1, tf, d), lambda i, j, te, nv: (te[i], j, 0)),
    ]
    args += [w1, w3, w2] + list(cast)
    in_specs += [hbm] * len(cast)
    grid_spec = pltpu.PrefetchScalarGridSpec(
        num_scalar_prefetch=2,
        grid=(n_tiles, n_f),
        in_specs=in_specs,
        out_specs=[pl.BlockSpec((tm, d), lambda i, j, te, nv: (i, 0))] + [hbm] * len(cast),
        scratch_shapes=[pltpu.VMEM((tm, d), w1.dtype), pltpu.VMEM((tm, tf), w1.dtype),
                        pltpu.VMEM((tm, d), F32)] + cast_scratch,
    )
    out = pl.pallas_call(
        functools.partial(_ffn_kernel, has_gain=gain is not None, has_res=res,
                          n_cast=len(cast), tf=tf, ck=ck),
        out_shape=[jax.ShapeDtypeStruct((p, d), F32)]
        + [jax.ShapeDtypeStruct(c.shape, BF16) for c in cast],
        grid_spec=grid_spec,
        compiler_params=_cparams(("arbitrary", "arbitrary")),
        name="swiglu",
    )(tile_expert, n_valid, *args)
    return out if cast else out[0]


def _hgrn_gates(fz, loglb, l1mlb, oneml):
    e = jnp.exp(-jnp.abs(fz))
    ope = 1.0 + e
    log_sig = jnp.minimum(fz, 0.0) - jnp.log(ope)
    cc = l1mlb + log_sig
    log_f = jnp.maximum(loglb, cc) + jnp.log(1.0 + jnp.exp(-jnp.abs(loglb - cc)))
    k = oneml * (jnp.where(fz > 0.0, e, 1.0) / ope)
    return log_f, k


def _level_reference(b, w):
    n = b.shape[0]
    if w >= 4:
        return jnp.concatenate(
            [jnp.broadcast_to(b[j + w:j + w + 1, :], (2 * w, b.shape[1]))
             for j in range(0, n, 2 * w)], axis=0)
    b3 = b.reshape(n // SUBLANES, SUBLANES, b.shape[1])
    ahead = lambda s: pltpu.roll(b3, SUBLANES - s, 1)
    pos = lax.broadcasted_iota(I32, (1, SUBLANES, 1), 1) & (2 * w - 1)
    if w == 2:
        r3 = jnp.where(pos == 0, ahead(2),
                       jnp.where(pos == 1, ahead(1),
                                 jnp.where(pos == 2, b3, pltpu.roll(b3, 1, 1))))
    else:
        r3 = jnp.where(pos == 0, ahead(1), b3)
    return r3.reshape(b.shape)


def _lru_gates(xc, wra, bra, wix, bix, c8):
    r = _sigmoid(_dot(xc, wra) + bra)
    ig = _sigmoid(_dot(xc, wix) + bix)
    log_a = c8 * r
    a = jnp.exp(log_a)
    one_m_a2 = -jnp.tanh(log_a) * (a * a + 1.0)
    u = jnp.sqrt(one_m_a2) * (ig * xc)
    return a, u


def _mixer_block_kernel(x_ref, g1_ref, win_ref, lbp_ref, hgg_ref, cw_ref, cb_ref, wra_ref,
                        bra_ref, wix_ref, bix_ref, c8_ref, cm_ref, wout_ref,
                        xo_ref, hs_ref, ls_ref, cs_ref,
                        st_scr, h_scr, xpad_scr, hb_scr, q_scr, lf_scr, k_scr, v_scr, g_scr,
                        gl_scr, a_scr, u_scr, m_scr, *, tl):
    l = pl.program_id(1)
    nl = pl.num_programs(1)

    @pl.when(l == 0)
    def _():
        st_scr[...] = jnp.zeros_like(st_scr)
        h_scr[...] = jnp.zeros_like(h_scr)
        xpad_scr[0:8, :] = jnp.zeros((8, LRU_WIDTH), F32)

    hb_scr[...] = _rms(x_ref[0], g1_ref[...]).astype(BF16)
    hb = hb_scr[...]
    zg = lambda j: _dot(hb, win_ref[:, j * HG_WIDTH:(j + 1) * HG_WIDTH])
    q_scr[...] = _silu(zg(0))
    log_f, kk = _hgrn_gates(zg(1), lbp_ref[0:1, :], lbp_ref[1:2, :], lbp_ref[2:3, :])
    lf_scr[...] = log_f
    k_scr[...] = kk
    v_scr[...] = zg(2)
    g_scr[...] = _silu(zg(3))
    xpad_scr[8:8 + tl, :] = zg(4)
    gl_scr[...] = _gelu_tanh(zg(5))
    xc = cb_ref[...]
    for j in range(CONV_K):
        xc = xc + xpad_scr[5 + j:5 + j + tl, :] * cw_ref[j:j + 1, :]
    tail = xpad_scr[tl + 5:tl + 8, :]
    xpad_scr[5:8, :] = tail
    a, u = _lru_gates(xc, wra_ref[...], bra_ref[...], wix_ref[...], bix_ref[...], c8_ref[...])
    a_scr[...] = a
    u_scr[...] = u

    row = lax.broadcasted_iota(I32, (CHUNK, 1), 0)
    ti = lax.broadcasted_iota(I32, (CHUNK, CHUNK), 0)
    si = lax.broadcasted_iota(I32, (CHUNK, CHUNK), 1)
    t3 = lax.broadcasted_iota(I32, (1, DIAG, 1), 1)
    sub3 = lax.broadcasted_iota(I32, (1, SUBLANES, 1), 1)
    nblk = CHUNK // DIAG
    ngrp = CHUNK // SUBLANES
    levels = []
    w = CHUNK // 2
    while w >= DIAG:
        sh = (2 * w).bit_length() - 1
        pairs = ((ti >> sh) == (si >> sh)) & ((ti & (2 * w - 1)) >= w) & ((si & (2 * w - 1)) < w)
        levels.append((w, (row & (2 * w - 1)) >= w, pairs))
        w //= 2

    def chunk_body(c, carry):
        r0 = pl.multiple_of(c * CHUNK, CHUNK)
        rows = pl.ds(r0, CHUNK)

        for gi in range(LRU_WIDTH // LANES):
            cs = slice(gi * LANES, (gi + 1) * LANES)
            aa = a_scr[rows, cs].reshape(ngrp, SUBLANES, LANES)
            uu = u_scr[rows, cs].reshape(ngrp, SUBLANES, LANES)
            d = 1
            while d < SUBLANES:
                keep = sub3 >= d
                a_sh = pltpu.roll(aa, d, 1)
                u_sh = pltpu.roll(uu, d, 1)
                uu = jnp.where(keep, aa * u_sh + uu, uu)
                aa = jnp.where(keep, aa * a_sh, aa)
                d *= 2
            hprev = h_scr[:, cs]
            groups = []
            for j in range(ngrp):
                hj = uu[j] + aa[j] * hprev
                groups.append(hj)
                hprev = hj[SUBLANES - 1:SUBLANES, :]
            hs = jnp.concatenate(groups, axis=0)
            h_scr[:, cs] = hprev
            m_scr[rows, HG_WIDTH + gi * LANES:HG_WIDTH + (gi + 1) * LANES] = (
                hs * gl_scr[rows, cs]).astype(BF16)

        q = q_scr[rows, :]
        k = k_scr[rows, :]
        v = v_scr[rows, :]
        f1, f2, f3 = _split3(lf_scr[rows, :])
        cm = cm_ref[...]
        b = _dot(cm, f1) + _dot(cm, f2) + _dot(cm, f3)

        for hh in range(HG_HEADS):
            cs = slice(hh * HG_D, (hh + 1) * HG_D)
            qh, kh, vh, bh = q[:, cs], k[:, cs], v[:, cs], b[:, cs]
            vb = vh.astype(BF16)
            amat = jnp.zeros((CHUNK, CHUNK), F32)
            for w, upper, pairs in levels:
                dec = jnp.exp(-jnp.abs(bh - _level_reference(bh, w)))
                y = (jnp.where(upper, qh, kh) * dec).astype(BF16)
                amat = jnp.where(pairs, _dot_nt(y, y), amat)
            o = _dot(amat.astype(BF16), vb)
            st = st_scr[hh]
            o = o + _dot_nt((qh * jnp.exp(bh)).astype(BF16), st.astype(BF16))
            if DIAG == 1:
                o = o + jnp.sum(qh * kh, axis=-1, keepdims=True) * vh
            else:
                q3 = qh.reshape(nblk, DIAG, HG_D)
                k3 = kh.reshape(nblk, DIAG, HG_D)
                v3 = vh.reshape(nblk, DIAG, HG_D)
                b3 = bh.reshape(nblk, DIAG, HG_D)
                od = jnp.zeros((nblk, DIAG, HG_D), F32)
                for s in range(DIAG):
                    dec = jnp.exp(jnp.minimum(b3 - b3[:, s:s + 1, :], 0.0))
                    p = q3 * (k3[:, s:s + 1, :] * dec)
                    rs = jnp.sum(p, axis=-1, keepdims=True)
                    rs = jnp.where(t3 >= s, rs, 0.0)
                    od = od + rs * v3[:, s:s + 1, :]
                o = o + od.reshape(CHUNK, HG_D)
            bl = bh[CHUNK - 1:CHUNK, :]
            kdec = (kh * jnp.exp(bl - bh)).astype(BF16)
            st_scr[hh] = st * jnp.exp(bl) + _dot_tn(vb, kdec)
            o = o * lax.rsqrt(jnp.mean(o * o, axis=-1, keepdims=True) + EPS)
            o = o * hgg_ref[:, cs]
            o = o * g_scr[rows, cs]
            m_scr[rows, cs] = o.astype(BF16)
        return carry

    lax.fori_loop(0, tl // CHUNK, chunk_body, 0, unroll=4)
    xo_ref[0] = x_ref[0] + _dot(m_scr[...], wout_ref[...])

    @pl.when(l == nl - 1)
    def _():
        for hh in range(HG_HEADS):
            hs_ref[0, hh] = st_scr[hh].T
        ls_ref[0] = h_scr[...]
        cs_ref[0] = xpad_scr[5:8, :]


def _mixer_block(x, gain, w_in, w_out, layer, mp, tl=512):
    bsz, seq, d = x.shape
    tl = min(tl, seq)
    full = lambda shape: pl.BlockSpec(shape, lambda b, l: (0,) * len(shape))
    stacked = lambda shape: pl.BlockSpec((None,) + shape, lambda b, l: (layer, 0, 0))
    out_shapes = (
        jax.ShapeDtypeStruct((bsz, seq, d), F32),
        jax.ShapeDtypeStruct((bsz, HG_HEADS, HG_D, HG_D), F32),
        jax.ShapeDtypeStruct((bsz, 1, LRU_WIDTH), F32),
        jax.ShapeDtypeStruct((bsz, CONV_K - 1, LRU_WIDTH), F32),
    )
    half = lambda dt: pltpu.VMEM((tl, HG_WIDTH), dt)
    xo, hg, lru, conv = pl.pallas_call(
        functools.partial(_mixer_block_kernel, tl=tl),
        out_shape=out_shapes,
        grid=(bsz, seq // tl),
        in_specs=[
            pl.BlockSpec((1, tl, d), lambda b, l: (b, l, 0)),
            full((1, d)), stacked((d, IN_WIDTH)),
            full((3, HG_WIDTH)), full((1, HG_WIDTH)), full((CONV_K, LRU_WIDTH)),
            full((1, LRU_WIDTH)), full((LRU_WIDTH, LRU_WIDTH)), full((1, LRU_WIDTH)),
            full((LRU_WIDTH, LRU_WIDTH)), full((1, LRU_WIDTH)), full((1, LRU_WIDTH)),
            full(mp["cm"].shape), stacked((d, d)),
        ],
        out_specs=(
            pl.BlockSpec((1, tl, d), lambda b, l: (b, l, 0)),
            pl.BlockSpec((1, HG_HEADS, HG_D, HG_D), lambda b, l: (b, 0, 0, 0)),
            pl.BlockSpec((1, 1, LRU_WIDTH), lambda b, l: (b, 0, 0)),
            pl.BlockSpec((1, CONV_K - 1, LRU_WIDTH), lambda b, l: (b, 0, 0)),
        ),
        scratch_shapes=[
            pltpu.VMEM((HG_HEADS, HG_D, HG_D), F32),
            pltpu.VMEM((1, LRU_WIDTH), F32),
            pltpu.VMEM((tl + 8, LRU_WIDTH), F32),
            pltpu.VMEM((tl, d), BF16),
            half(F32), half(F32), half(F32), half(F32), half(F32), half(F32), half(F32),
            half(F32),
            pltpu.VMEM((tl, d), BF16),
        ],
        compiler_params=_cparams(("parallel", "arbitrary")),
        name="mixer_block",
    )(x, gain.reshape(1, d), w_in, mp["lbp"], mp["hgg"], mp["cw"], mp["cb"], mp["wra"],
      mp["bra"], mp["wix"], mp["bix"], mp["c8"], mp["cm"], w_out)
    return xo, hg, lru.reshape(bsz, LRU_WIDTH), conv


def _mixer_step_kernel(z_ref, sh_ref, sl_ref, sc_ref, lbp_ref, hgg_ref, cw_ref, cb_ref, wra_ref,
                       bra_ref, wix_ref, bix_ref, c8_ref, *rest, bb, aliased):
    m_ref, hn_ref, ln_ref, cn_ref, o_scr = rest[1:] if aliased else rest
    z = z_ref[...]
    xr = z[:, 4 * HG_WIDTH:4 * HG_WIDTH + LRU_WIDTH]
    gr = z[:, 4 * HG_WIDTH + LRU_WIDTH:IN_WIDTH]
    buf = sc_ref[...]
    xc = cb_ref[...]
    for j in range(CONV_K - 1):
        xc = xc + buf[:, j * LRU_WIDTH:(j + 1) * LRU_WIDTH] * cw_ref[j:j + 1, :]
    xc = xc + xr * cw_ref[CONV_K - 1:CONV_K, :]
    cn_ref[:, 0:(CONV_K - 2) * LRU_WIDTH] = buf[:, LRU_WIDTH:(CONV_K - 1) * LRU_WIDTH]
    cn_ref[:, (CONV_K - 2) * LRU_WIDTH:(CONV_K - 1) * LRU_WIDTH] = xr
    a, u = _lru_gates(xc, wra_ref[...], bra_ref[...], wix_ref[...], bix_ref[...], c8_ref[...])
    hnew = u + a * sl_ref[...]
    ln_ref[...] = hnew
    m_ref[:, HG_WIDTH:D_MODEL] = (hnew * _gelu_tanh(gr)).astype(m_ref.dtype)

    q = _silu(z[:, 0:HG_WIDTH])
    log_f, k = _hgrn_gates(z[:, HG_WIDTH:2 * HG_WIDTH], lbp_ref[0:1, :], lbp_ref[1:2, :],
                           lbp_ref[2:3, :])
    f = jnp.exp(log_f)
    v = z[:, 2 * HG_WIDTH:3 * HG_WIDTH]
    gate = _silu(z[:, 3 * HG_WIDTH:4 * HG_WIDTH])
    zpad = jnp.zeros((HG_D - bb, HG_D), F32)
    for hh in range(HG_HEADS):
        cs = slice(hh * HG_D, (hh + 1) * HG_D)
        ft = jnp.concatenate([f[:, cs], zpad], axis=0).T
        kt = jnp.concatenate([k[:, cs], zpad], axis=0).T
        qt = jnp.concatenate([q[:, cs], zpad], axis=0).T
        for j in range(bb):
            fcol = jnp.broadcast_to(ft[:, j:j + 1], (HG_D, HG_D))
            kcol = jnp.broadcast_to(kt[:, j:j + 1], (HG_D, HG_D))
            qcol = jnp.broadcast_to(qt[:, j:j + 1], (HG_D, HG_D))
            vrow = jnp.broadcast_to(v[j:j + 1, cs], (HG_D, HG_D))
            sn = fcol * sh_ref[j, hh] + kcol * vrow
            hn_ref[j, hh] = sn
            o_scr[j:j + 1, cs] = jnp.sum(qcol * sn, axis=0, keepdims=True)
    for hh in range(HG_HEADS):
        cs = slice(hh * HG_D, (hh + 1) * HG_D)
        o = o_scr[:, cs]
        o = o * lax.rsqrt(jnp.mean(o * o, axis=-1, keepdims=True) + EPS)
        o = o * hgg_ref[:, cs]
        o = o * gate[:, cs]
        m_ref[:, cs] = o.astype(m_ref.dtype)


def _mixer_step(z, s_hg_all, layer, s_lru, s_conv, mp, hg_buf=None, bb=16):
    bsz = z.shape[0]
    bb = min(bb, bsz)
    cw3 = (CONV_K - 1) * LRU_WIDTH
    full = lambda shape: pl.BlockSpec(shape, lambda i: (0,) * len(shape))
    state_spec = pl.BlockSpec((None, bb, HG_HEADS, HG_D, HG_D), lambda i: (layer, i, 0, 0, 0))
    out_shapes = (
        jax.ShapeDtypeStruct((bsz, D_MODEL), mp["wra"].dtype),
        jax.ShapeDtypeStruct(s_hg_all.shape, F32),
        jax.ShapeDtypeStruct((bsz, LRU_WIDTH), F32),
        jax.ShapeDtypeStruct((bsz, cw3), F32),
    )
    in_specs = [
        pl.BlockSpec((bb, IN_WIDTH), lambda i: (i, 0)),
        state_spec,
        pl.BlockSpec((bb, LRU_WIDTH), lambda i: (i, 0)),
        pl.BlockSpec((bb, cw3), lambda i: (i, 0)),
        full((3, HG_WIDTH)), full((1, HG_WIDTH)), full((CONV_K, LRU_WIDTH)),
        full((1, LRU_WIDTH)), full((LRU_WIDTH, LRU_WIDTH)), full((1, LRU_WIDTH)),
        full((LRU_WIDTH, LRU_WIDTH)), full((1, LRU_WIDTH)), full((1, LRU_WIDTH)),
    ]
    args = [z, s_hg_all, s_lru, s_conv.reshape(bsz, cw3), mp["lbp"], mp["hgg"], mp["cw"],
            mp["cb"], mp["wra"], mp["bra"], mp["wix"], mp["bix"], mp["c8"]]
    aliases = {}
    if hg_buf is not None:
        aliases = {len(args): 1}
        in_specs.append(pl.BlockSpec(memory_space=pl.ANY))
        args.append(hg_buf)
    merged, hg, lru, conv = pl.pallas_call(
        functools.partial(_mixer_step_kernel, bb=bb, aliased=hg_buf is not None),
        out_shape=out_shapes,
        grid=(bsz // bb,),
        in_specs=in_specs,
        out_specs=(
            pl.BlockSpec((bb, D_MODEL), lambda i: (i, 0)),
            state_spec,
            pl.BlockSpec((bb, LRU_WIDTH), lambda i: (i, 0)),
            pl.BlockSpec((bb, cw3), lambda i: (i, 0)),
        ),
        scratch_shapes=[pltpu.VMEM((bb, HG_WIDTH), F32)],
        input_output_aliases=aliases,
        compiler_params=_cparams(("parallel",)),
        name="mixer_step",
    )(*args)
    return merged, hg, lru, conv.reshape(bsz, CONV_K - 1, LRU_WIDTH)


def _router_kernel(x_ref, g_ref, rw_ref, tril_ref, info_ref, idx_ref, cnt_ref, carry_scr):
    i = pl.program_id(0)

    @pl.when(i == 0)
    def _():
        carry_scr[...] = jnp.zeros_like(carry_scr)

    h = _rms(x_ref[...], g_ref[...])
    logits = jnp.dot(h, rw_ref[...], preferred_element_type=F32, precision=lax.Precision.HIGHEST)
    lane = lax.broadcasted_iota(I32, logits.shape, 1)
    neg = jnp.float32(-jnp.inf)
    logits = jnp.where(lane < N_EXPERTS, logits, neg)
    m1 = jnp.max(logits, axis=-1, keepdims=True)
    i1 = jnp.min(jnp.where(logits == m1, lane, LANES), axis=-1, keepdims=True)
    l2 = jnp.where(lane == i1, neg, logits)
    m2 = jnp.max(l2, axis=-1, keepdims=True)
    i2 = jnp.min(jnp.where(l2 == m2, lane, LANES), axis=-1, keepdims=True)
    e = jnp.exp(m2 - m1)
    p1 = 1.0 / (1.0 + e)
    p2 = e / (1.0 + e)
    oh1 = (lane == i1).astype(F32)
    oh2 = (lane == i2).astype(F32)
    sel = oh1 + oh2
    before = _dot(tril_ref[...], sel.astype(BF16)) + carry_scr[...]
    r1 = jnp.sum(oh1 * before, axis=-1, keepdims=True)
    r2 = jnp.sum(oh2 * before, axis=-1, keepdims=True)
    carry_scr[...] += jnp.sum(sel, axis=0, keepdims=True)
    info = jnp.where(lane == 0, i1.astype(F32), 0.0)
    info = jnp.where(lane == 1, i2.astype(F32), info)
    info = jnp.where(lane == 2, r1, info)
    info = jnp.where(lane == 3, r2, info)
    info = jnp.where(lane == 4, p1, info)
    info = jnp.where(lane == 5, p2, info)
    info_ref[...] = info
    idx_ref[...] = info.T[0:SUBLANES, :].astype(I32)
    cnt_ref[...] = carry_scr[...]


def _router(x, gain, rw_pad, tm=512):
    t, d = x.shape
    tm = min(tm, t)
    tril = jnp.tril(jnp.ones((tm, tm), BF16), -1)
    return pl.pallas_call(
        _router_kernel,
        out_shape=(jax.ShapeDtypeStruct((t, LANES), F32),
                   jax.ShapeDtypeStruct((SUBLANES, t), I32),
                   jax.ShapeDtypeStruct((1, LANES), F32)),
        grid=(t // tm,),
        in_specs=[
            pl.BlockSpec((tm, d), lambda i: (i, 0)),
            pl.BlockSpec((1, d), lambda i: (0, 0)),
            pl.BlockSpec((d, LANES), lambda i: (0, 0)),
            pl.BlockSpec((tm, tm), lambda i: (0, 0)),
        ],
        out_specs=(pl.BlockSpec((tm, LANES), lambda i: (i, 0)),
                   pl.BlockSpec((SUBLANES, tm), lambda i: (0, i)),
                   pl.BlockSpec((1, LANES), lambda i: (0, 0))),
        scratch_shapes=[pltpu.VMEM((1, LANES), F32)],
        compiler_params=_cparams(("arbitrary",)),
        name="router",
    )(x, gain.reshape(1, d), rw_pad, tril)


def _row_copy(src, dst, s, d, sem):
    return pltpu.make_async_copy(src.at[pl.ds(s, 1)], dst.at[pl.ds(d, 1)], sem)


def _tile_dest(dest, t, td):
    nb = t // td
    dest3 = jnp.concatenate([dest[0].reshape(nb, td), dest[1].reshape(nb, td)], axis=1)
    return dest3.reshape(nb, 1, 2 * td)


def _dispatch_kernel(dest_ref, x_ref, g_ref, xs_in_ref, xs_ref, pk_scr, sem, *, td):
    del xs_in_ref
    pk_scr[...] = _rms(x_ref[...], g_ref[...])

    def issue(r, c):
        _row_copy(pk_scr, xs_ref, r, dest_ref[0, 0, r], sem).start(priority=0)
        _row_copy(pk_scr, xs_ref, r, dest_ref[0, 0, td + r], sem).start(priority=1)
        return c

    lax.fori_loop(0, td, issue, 0, unroll=8)
    for _ in range(2):
        pltpu.make_async_copy(pk_scr, xs_ref.at[pl.ds(0, td)], sem).wait()


def _dispatch(x, gain, dest, n_rows, td=512):
    t, d = x.shape
    td = min(td, t)
    xs0 = jnp.zeros((n_rows, d), F32)
    return pl.pallas_call(
        functools.partial(_dispatch_kernel, td=td),
        out_shape=jax.ShapeDtypeStruct((n_rows, d), F32),
        grid=(t // td,),
        in_specs=[
            pl.BlockSpec((1, 1, 2 * td), lambda i: (i, 0, 0), memory_space=pltpu.SMEM),
            pl.BlockSpec((td, d), lambda i: (i, 0)),
            pl.BlockSpec((1, d), lambda i: (0, 0)),
            pl.BlockSpec(memory_space=pl.ANY),
        ],
        out_specs=pl.BlockSpec(memory_space=pl.ANY),
        scratch_shapes=[pltpu.VMEM((td, d), F32), pltpu.SemaphoreType.DMA(())],
        input_output_aliases={3: 0},
        compiler_params=pltpu.CompilerParams(dimension_semantics=("arbitrary",),
                                             vmem_limit_bytes=VMEM_LIMIT_BYTES,
                                             has_side_effects=True),
        name="dispatch",
    )(_tile_dest(dest, t, td), x, gain.reshape(1, d), xs0)


def _combine_kernel(dest_ref, nxt_ref, x_ref, info_ref, g_ref, ys_ref, o_ref, y_scr, sems, *, td):
    i = pl.program_id(0)
    nb = pl.num_programs(0)

    def fetch(idx_ref, slot):
        def issue(r, c):
            _row_copy(ys_ref, y_scr.at[slot, 0], idx_ref[0, 0, r], r,
                      sems.at[slot]).start(priority=0)
            _row_copy(ys_ref, y_scr.at[slot, 1], idx_ref[0, 0, td + r], r,
                      sems.at[slot]).start(priority=1)
            return c

        lax.fori_loop(0, td, issue, 0, unroll=8)

    def finish(slot):
        for j in range(2):
            pltpu.make_async_copy(ys_ref.at[pl.ds(0, td)], y_scr.at[slot, j],
                                  sems.at[slot]).wait()
        p1 = info_ref[:, 4:5]
        p2 = info_ref[:, 5:6]
        x = x_ref[...] + (p1 * y_scr[slot, 0] + p2 * y_scr[slot, 1])
        o_ref[...] = _rms(x, g_ref[...])

    @pl.when(i == 0)
    def _():
        fetch(dest_ref, 0)

    for slot in range(2):
        @pl.when(i % 2 == slot)
        def _(slot=slot):
            @pl.when(i + 1 < nb)
            def _():
                fetch(nxt_ref, 1 - slot)

            finish(slot)


def _combine(x, ys, dest, info, gain, td=256):
    t, d = x.shape
    td = min(td, t)
    nb = t // td
    row = pl.BlockSpec((td, d), lambda i: (i, 0))
    tiled = _tile_dest(dest, t, td)
    return pl.pallas_call(
        functools.partial(_combine_kernel, td=td),
        out_shape=jax.ShapeDtypeStruct((t, d), F32),
        grid=(nb,),
        in_specs=[
            pl.BlockSpec((1, 1, 2 * td), lambda i: (i, 0, 0), memory_space=pltpu.SMEM),
            pl.BlockSpec((1, 1, 2 * td), lambda i: (jnp.minimum(i + 1, nb - 1), 0, 0),
                         memory_space=pltpu.SMEM),
            row,
            pl.BlockSpec((td, LANES), lambda i: (i, 0)),
            pl.BlockSpec((1, d), lambda i: (0, 0)),
            pl.BlockSpec(memory_space=pl.ANY),
        ],
        out_specs=row,
        scratch_shapes=[pltpu.VMEM((2, 2, td, d), F32), pltpu.SemaphoreType.DMA((2,))],
        compiler_params=_cparams(("arbitrary",)),
        name="combine",
    )(tiled, tiled, x, info, gain.reshape(1, d), ys)


def _moe(x, norm_g, rw_pad, w1, w3, w2, final_g, tm_e):
    t, d = x.shape
    info, idx, cnt = _router(x, norm_g, rw_pad)
    e1, e2, r1, r2 = idx[0], idx[1], idx[2], idx[3]
    counts = cnt[0, :N_EXPERTS].astype(I32)
    padded = ((counts + tm_e - 1) // tm_e) * tm_e
    ends = jnp.cumsum(padded)
    starts = ends - padded
    dest = jnp.stack([starts[e1] + r1, starts[e2] + r2])
    n_tiles = (2 * t + N_EXPERTS * (tm_e - 1)) // tm_e
    n_rows = n_tiles * tm_e
    tile_start = jnp.arange(n_tiles, dtype=I32) * tm_e
    tile_expert = jnp.minimum(
        jnp.sum((tile_start[:, None] >= ends[None, :]).astype(I32), axis=1), N_EXPERTS - 1)
    n_valid = (ends[-1] // tm_e).reshape(1).astype(I32)
    xs = _dispatch(x, norm_g, dest, n_rows)
    ys = _ffn(xs, w1, w3, w2, tile_expert.astype(I32), n_valid, tm=tm_e, n_f=2)
    return _combine(x, ys, dest, info, final_g)


def _block_diag(w):
    n, c, d = w.shape
    eye = jnp.eye(n, dtype=w.dtype)
    return (w[:, :, None, :] * eye[:, None, :, None]).reshape(n * c, n * d)


def _cum_matrix():
    t = jnp.arange(CHUNK)
    return (t[None, :] <= t[:, None]).astype(BF16)


def _mixer_params(l, lb, hg_norm_g, conv_w, conv_b, w_ra, b_ra, w_ix, b_ix, lru_lambda, wdtype):
    row = lambda a: a.reshape(1, -1).astype(F32)
    return {
        "lbp": jnp.stack([jnp.log(lb), jnp.log1p(-lb), 1.0 - lb]).astype(F32),
        "hgg": row(hg_norm_g[l]),
        "cw": conv_w[l].astype(F32),
        "cb": row(conv_b[l]),
        "wra": _block_diag(w_ra[l].astype(F32)).astype(wdtype),
        "bra": row(b_ra[l]),
        "wix": _block_diag(w_ix[l].astype(F32)).astype(wdtype),
        "bix": row(b_ix[l]),
        "c8": row(-LRU_C * jax.nn.softplus(-lru_lambda[l].astype(F32))),
        "cm": _cum_matrix(),
    }


def _trunk(x, seq_shape, states, prm):
    bsz, seq = seq_shape
    one = jnp.ones((1,), I32)
    new_hg, new_lru, new_conv = [], [], []
    hg_stack = None if states is None else states[0]
    for l in range(2):
        mp = prm["mixer"][l]
        if states is None:
            x, s1, s2, s3 = _mixer_block(x.reshape(bsz, seq, D_MODEL), prm["norm1_g"][l],
                                         prm["w_in"], prm["w_out"], l, mp)
            x = x.reshape(bsz * seq, D_MODEL)
            new_hg.append(s1)
        else:
            z = _linear(x, prm["w_in"], l, gain=prm["norm1_g"][l])
            merged, hg_stack, s2, s3 = _mixer_step(z, states[0], l, states[1][l], states[2][l],
                                                   mp, hg_buf=hg_stack)
            x = _linear(merged, prm["w_out"], l, res=x)
        new_lru.append(s2)
        new_conv.append(s3)
        if l == 0:
            n_tiles = max(x.shape[0] // 512, 1)
            n_f = 1 if prm["ffn_w1"].dtype == BF16 else prm["ffn_w1"].shape[2] // 256
            todo = () if "moe" in prm["shared"] else prm["moe_f32"]
            out = _ffn(x, prm["ffn_w1"], prm["ffn_w3"], prm["ffn_w2"], jnp.zeros((n_tiles,), I32),
                       one * n_tiles, gain=prm["norm2_g"][0], res=True, cast=todo, tm=512,
                       n_f=n_f)
            if todo:
                x = out[0]
                prm["shared"]["moe"] = [o.reshape(s) for o, s in zip(out[1:], prm["moe_shapes"])]
            else:
                x = out
        else:
            w1, w3, w2 = prm["shared"]["moe"]
            x = _moe(x, prm["norm2_g"][1], prm["rw_pad"], w1, w3, w2, prm["final_norm_g"],
                     tm_e=512 if states is None else 128)
    hg_all = jnp.stack(new_hg) if states is None else hg_stack
    return x, hg_all, jnp.stack(new_lru), jnp.stack(new_conv)


def kernel(x_prompt, x_sample, state_hgrn, state_lru, state_conv, norm1_g, w_in, lower_bounds,
           hg_norm_g, conv_w, conv_b, w_ra, b_ra, w_ix, b_ix, lru_lambda, w_out, norm2_g,
           ffn_w1, ffn_w3, ffn_w2, router_w, moe_w1, moe_w3, moe_w2, final_norm_g):
    lb_all = jnp.cumsum(jax.nn.softmax(lower_bounds.astype(F32), axis=0), axis=0)
    lb_all = lb_all - lb_all[0]
    moe = [moe_w1[0].astype(F32), moe_w3[0].astype(F32), moe_w2[0].astype(F32)]
    experts = {
        "moe_f32": [m.reshape(-1, m.shape[-1]) for m in moe],
        "moe_shapes": [m.shape for m in moe],
        "shared": {},
        "rw_pad": jnp.pad(router_w[0].astype(F32), ((0, 0), (0, LANES - N_EXPERTS))),
        "norm1_g": norm1_g, "norm2_g": norm2_g, "final_norm_g": final_norm_g,
    }

    def params(wdtype):
        return dict(
            experts,
            mixer=[_mixer_params(l, lb_all[l], hg_norm_g, conv_w, conv_b, w_ra, b_ra, w_ix, b_ix,
                                 lru_lambda, wdtype) for l in range(2)],
            w_in=w_in.astype(wdtype), w_out=w_out.astype(wdtype),
            ffn_w1=ffn_w1.astype(wdtype), ffn_w3=ffn_w3.astype(wdtype),
            ffn_w2=ffn_w2.astype(wdtype))

    bp, sp, d = x_prompt.shape
    bs = x_sample.shape[0]
    y_p, hg_p, lru_p, conv_p = _trunk(x_prompt.reshape(bp * sp, d), (bp, sp), None, params(BF16))
    y_s, hg_s, lru_s, conv_s = _trunk(x_sample.reshape(bs, d), (bs, 1),
                                      (state_hgrn, state_lru, state_conv), params(F32))
    return (y_p.reshape(bp, sp, d), y_s.reshape(bs, 1, d), hg_p, lru_p, conv_p, hg_s, lru_s,
            conv_s)
```

```python
import functools

import jax
import jax.numpy as jnp
from jax import lax
from jax.experimental import pallas as pl
from jax.experimental.pallas import tpu as pltpu

F32 = jnp.float32
BF16 = jnp.bfloat16
I32 = jnp.int32

D_MODEL = 1024
HG_WIDTH = 512
HG_HEADS = 4
HG_D = 128
LRU_WIDTH = 512
LRU_BLOCKS = 8
LRU_C = 8.0
CONV_K = 4
IN_WIDTH = 4 * HG_WIDTH + 2 * LRU_WIDTH
N_EXPERTS = 8
EPS = 1e-6

CHUNK = 64
DIAG = 1
LANES = 128
SUBLANES = 8
TILE_ROWS = 512
F_CHUNK = 256
VMEM_LIMIT_BYTES = 56 * 1024 * 1024


def _cparams(sem):
    return pltpu.CompilerParams(dimension_semantics=sem, vmem_limit_bytes=VMEM_LIMIT_BYTES)


def _sigmoid(x):
    return 1.0 / (1.0 + jnp.exp(-x))


def _silu(x):
    return x * _sigmoid(x)


def _gelu_tanh(x):
    c = 0.7978845608028654
    return 0.5 * x * (1.0 + jnp.tanh(c * (x + 0.044715 * (x * x * x))))


def _rms(x, g):
    return x * lax.rsqrt(jnp.mean(x * x, axis=-1, keepdims=True) + EPS) * g


def _dot(a, b):
    if b.dtype == F32:
        return jnp.dot(a.astype(F32), b, preferred_element_type=F32,
                       precision=lax.Precision.HIGHEST)
    return jnp.dot(a.astype(BF16), b, preferred_element_type=F32)


def _dot_nt(a, b):
    return lax.dot_general(a, b, (((1,), (1,)), ((), ())), preferred_element_type=F32)


def _dot_tn(a, b):
    return lax.dot_general(a, b, (((0,), (0,)), ((), ())), preferred_element_type=F32)


def _split3(x):
    x1 = x.astype(BF16)
    r1 = x - x1.astype(F32)
    x2 = r1.astype(BF16)
    x3 = (r1 - x2.astype(F32)).astype(BF16)
    return x1, x2, x3


def _linear_kernel(*refs, has_gain, has_res):
    it = iter(refs)
    x_ref = next(it)
    g_ref = next(it) if has_gain else None
    w_ref = next(it)
    r_ref = next(it) if has_res else None
    o_ref = next(it)
    h_scr = next(it)

    @pl.when(pl.program_id(1) == 0)
    def _():
        x = x_ref[...].astype(F32)
        if has_gain:
            x = _rms(x, g_ref[...])
        h_scr[...] = x.astype(h_scr.dtype)

    acc = _dot(h_scr[...], w_ref[...])
    if has_res:
        acc = acc + r_ref[...]
    o_ref[...] = acc.astype(o_ref.dtype)


def _linear(x, w, layer, gain=None, res=None, tm=1024, tn=1024):
    m, k = x.shape
    n = w.shape[2]
    tm = min(tm, m)
    tn = min(tn, n)
    in_specs = [pl.BlockSpec((tm, k), lambda i, j: (i, 0))]
    args = [x]
    if gain is not None:
        in_specs.append(pl.BlockSpec((1, k), lambda i, j: (0, 0)))
        args.append(gain.reshape(1, k))
    in_specs.append(pl.BlockSpec((None, k, tn), lambda i, j: (layer, 0, j)))
    args.append(w)
    if res is not None:
        in_specs.append(pl.BlockSpec((tm, tn), lambda i, j: (i, j)))
        args.append(res)
    return pl.pallas_call(
        functools.partial(_linear_kernel, has_gain=gain is not None, has_res=res is not None),
        out_shape=jax.ShapeDtypeStruct((m, n), F32),
        grid=(m // tm, n // tn),
        in_specs=in_specs,
        out_specs=pl.BlockSpec((tm, tn), lambda i, j: (i, j)),
        scratch_shapes=[pltpu.VMEM((tm, k), w.dtype)],
        compiler_params=_cparams(("parallel", "arbitrary")),
        name="linear",
    )(*args)


def _ffn_kernel(te_ref, nv_ref, *refs, has_gain, has_res, tf, ck):
    del te_ref
    it = iter(refs)
    x_ref = next(it)
    g_ref = next(it) if has_gain else None
    w1_ref, w3_ref, w2_ref, o_ref, xb_scr, a_scr, acc_scr = it
    i = pl.program_id(0)
    j = pl.program_id(1)
    nj = pl.num_programs(1)
    valid = i < nv_ref[0]

    @pl.when(valid)
    def _():
        @pl.when(j == 0)
        def _():
            x = x_ref[...]
            if has_gain:
                x = _rms(x, g_ref[...])
            xb_scr[...] = x.astype(xb_scr.dtype)

        xb = xb_scr[...]
        for c in range(tf // ck):
            h1 = _dot(xb, w1_ref[0, :, c * ck:(c + 1) * ck])
            h3 = _dot(xb, w3_ref[0, :, c * ck:(c + 1) * ck])
            a_scr[:, c * ck:(c + 1) * ck] = (_silu(h1) * h3).astype(a_scr.dtype)
        part = _dot(a_scr[...], w2_ref[0])

        @pl.when(j == 0)
        def _():
            acc_scr[...] = part

        @pl.when(j > 0)
        def _():
            acc_scr[...] += part

        @pl.when(j == nj - 1)
        def _():
            out = acc_scr[...]
            if has_res:
                out = x_ref[...] + out
            o_ref[...] = out

    @pl.when(jnp.logical_and(jnp.logical_not(valid), j == nj - 1))
    def _():
        o_ref[...] = jnp.zeros_like(o_ref)


def _ffn(x, w1, w3, w2, tile_expert, n_valid, gain=None, res=False, tm=TILE_ROWS, n_f=2,
         ck=F_CHUNK):
    p, d = x.shape
    f = w1.shape[2]
    tm = min(tm, p)
    tf = f // n_f
    n_tiles = p // tm
    in_specs = [pl.BlockSpec((tm, d), lambda i, j, te, nv: (i, 0))]
    args = [x]
    if gain is not None:
        in_specs.append(pl.BlockSpec((1, d), lambda i, j, te, nv: (0, 0)))
        args.append(gain.reshape(1, d))
    in_specs += [
        pl.BlockSpec((1, d, tf), lambda i, j, te, nv: (te[i], 0, j)),
        pl.BlockSpec((1, d, tf), lambda i, j, te, nv: (te[i], 0, j)),
        pl.BlockSpec((1, tf, d), lambda i, j, te, nv: (te[i], j, 0)),
    ]
    args += [w1, w3, w2]
    grid_spec = pltpu.PrefetchScalarGridSpec(
        num_scalar_prefetch=2,
        grid=(n_tiles, n_f),
        in_specs=in_specs,
        out_specs=pl.BlockSpec((tm, d), lambda i, j, te, nv: (i, 0)),
        scratch_shapes=[pltpu.VMEM((tm, d), w1.dtype), pltpu.VMEM((tm, tf), w1.dtype),
                        pltpu.VMEM((tm, d), F32)],
    )
    return pl.pallas_call(
        functools.partial(_ffn_kernel, has_gain=gain is not None, has_res=res, tf=tf, ck=ck),
        out_shape=jax.ShapeDtypeStruct((p, d), F32),
        grid_spec=grid_spec,
        compiler_params=_cparams(("arbitrary", "arbitrary")),
        name="swiglu",
    )(tile_expert, n_valid, *args)


def _hgrn_gates(fz, loglb, l1mlb, oneml):
    e = jnp.exp(-jnp.abs(fz))
    ope = 1.0 + e
    log_sig = jnp.minimum(fz, 0.0) - jnp.log(ope)
    cc = l1mlb + log_sig
    log_f = jnp.maximum(loglb, cc) + jnp.log(1.0 + jnp.exp(-jnp.abs(loglb - cc)))
    k = oneml * (jnp.where(fz > 0.0, e, 1.0) / ope)
    return log_f, k


def _level_reference(b, w):
    n = b.shape[0]
    if w >= 4:
        return jnp.concatenate(
            [jnp.broadcast_to(b[j + w:j + w + 1, :], (2 * w, b.shape[1]))
             for j in range(0, n, 2 * w)], axis=0)
    b3 = b.reshape(n // SUBLANES, SUBLANES, b.shape[1])
    ahead = lambda s: pltpu.roll(b3, SUBLANES - s, 1)
    pos = lax.broadcasted_iota(I32, (1, SUBLANES, 1), 1) & (2 * w - 1)
    if w == 2:
        r3 = jnp.where(pos == 0, ahead(2),
                       jnp.where(pos == 1, ahead(1),
                                 jnp.where(pos == 2, b3, pltpu.roll(b3, 1, 1))))
    else:
        r3 = jnp.where(pos == 0, ahead(1), b3)
    return r3.reshape(b.shape)


def _lru_gates(xc, wra, bra, wix, bix, c8):
    r = _sigmoid(_dot(xc, wra) + bra)
    ig = _sigmoid(_dot(xc, wix) + bix)
    log_a = c8 * r
    a = jnp.exp(log_a)
    one_m_a2 = -jnp.tanh(log_a) * (a * a + 1.0)
    u = jnp.sqrt(one_m_a2) * (ig * xc)
    return a, u


def _mixer_block_kernel(x_ref, g1_ref, win_ref, lbp_ref, hgg_ref, cw_ref, cb_ref, wra_ref,
                        bra_ref, wix_ref, bix_ref, c8_ref, cm_ref, wout_ref,
                        xo_ref, hs_ref, ls_ref, cs_ref,
                        st_scr, h_scr, xpad_scr, hb_scr, q_scr, lf_scr, k_scr, v_scr, g_scr,
                        gl_scr, a_scr, u_scr, m_scr, *, tl):
    l = pl.program_id(1)
    nl = pl.num_programs(1)

    @pl.when(l == 0)
    def _():
        st_scr[...] = jnp.zeros_like(st_scr)
        h_scr[...] = jnp.zeros_like(h_scr)
        xpad_scr[0:8, :] = jnp.zeros((8, LRU_WIDTH), F32)

    hb_scr[...] = _rms(x_ref[0], g1_ref[...]).astype(BF16)
    hb = hb_scr[...]
    zg = lambda j: _dot(hb, win_ref[:, j * HG_WIDTH:(j + 1) * HG_WIDTH])
    q_scr[...] = _silu(zg(0))
    log_f, kk = _hgrn_gates(zg(1), lbp_ref[0:1, :], lbp_ref[1:2, :], lbp_ref[2:3, :])
    lf_scr[...] = log_f
    k_scr[...] = kk
    v_scr[...] = zg(2)
    g_scr[...] = _silu(zg(3))
    xpad_scr[8:8 + tl, :] = zg(4)
    gl_scr[...] = _gelu_tanh(zg(5))
    xc = cb_ref[...]
    for j in range(CONV_K):
        xc = xc + xpad_scr[5 + j:5 + j + tl, :] * cw_ref[j:j + 1, :]
    tail = xpad_scr[tl + 5:tl + 8, :]
    xpad_scr[5:8, :] = tail
    a, u = _lru_gates(xc, wra_ref[...], bra_ref[...], wix_ref[...], bix_ref[...], c8_ref[...])
    a_scr[...] = a
    u_scr[...] = u

    row = lax.broadcasted_iota(I32, (CHUNK, 1), 0)
    ti = lax.broadcasted_iota(I32, (CHUNK, CHUNK), 0)
    si = lax.broadcasted_iota(I32, (CHUNK, CHUNK), 1)
    t3 = lax.broadcasted_iota(I32, (1, DIAG, 1), 1)
    sub3 = lax.broadcasted_iota(I32, (1, SUBLANES, 1), 1)
    nblk = CHUNK // DIAG
    ngrp = CHUNK // SUBLANES
    levels = []
    w = CHUNK // 2
    while w >= DIAG:
        sh = (2 * w).bit_length() - 1
        pairs = ((ti >> sh) == (si >> sh)) & ((ti & (2 * w - 1)) >= w) & ((si & (2 * w - 1)) < w)
        levels.append((w, (row & (2 * w - 1)) >= w, pairs))
        w //= 2

    def chunk_body(c, carry):
        r0 = pl.multiple_of(c * CHUNK, CHUNK)
        rows = pl.ds(r0, CHUNK)

        for gi in range(LRU_WIDTH // LANES):
            cs = slice(gi * LANES, (gi + 1) * LANES)
            aa = a_scr[rows, cs].reshape(ngrp, SUBLANES, LANES)
            uu = u_scr[rows, cs].reshape(ngrp, SUBLANES, LANES)
            d = 1
            while d < SUBLANES:
                keep = sub3 >= d
                a_sh = pltpu.roll(aa, d, 1)
                u_sh = pltpu.roll(uu, d, 1)
                uu = jnp.where(keep, aa * u_sh + uu, uu)
                aa = jnp.where(keep, aa * a_sh, aa)
                d *= 2
            hprev = h_scr[:, cs]
            groups = []
            for j in range(ngrp):
                hj = uu[j] + aa[j] * hprev
                groups.append(hj)
                hprev = hj[SUBLANES - 1:SUBLANES, :]
            hs = jnp.concatenate(groups, axis=0)
            h_scr[:, cs] = hprev
            m_scr[rows, HG_WIDTH + gi * LANES:HG_WIDTH + (gi + 1) * LANES] = (
                hs * gl_scr[rows, cs]).astype(BF16)

        q = q_scr[rows, :]
        k = k_scr[rows, :]
        v = v_scr[rows, :]
        f1, f2, f3 = _split3(lf_scr[rows, :])
        cm = cm_ref[...]
        b = _dot(cm, f1) + _dot(cm, f2) + _dot(cm, f3)

        for hh in range(HG_HEADS):
            cs = slice(hh * HG_D, (hh + 1) * HG_D)
            qh, kh, vh, bh = q[:, cs], k[:, cs], v[:, cs], b[:, cs]
            vb = vh.astype(BF16)
            amat = jnp.zeros((CHUNK, CHUNK), F32)
            for w, upper, pairs in levels:
                dec = jnp.exp(-jnp.abs(bh - _level_reference(bh, w)))
                y = (jnp.where(upper, qh, kh) * dec).astype(BF16)
                amat = jnp.where(pairs, _dot_nt(y, y), amat)
            o = _dot(amat.astype(BF16), vb)
            st = st_scr[hh]
            o = o + _dot_nt((qh * jnp.exp(bh)).astype(BF16), st.astype(BF16))
            if DIAG == 1:
                o = o + jnp.sum(qh * kh, axis=-1, keepdims=True) * vh
            else:
                q3 = qh.reshape(nblk, DIAG, HG_D)
                k3 = kh.reshape(nblk, DIAG, HG_D)
                v3 = vh.reshape(nblk, DIAG, HG_D)
                b3 = bh.reshape(nblk, DIAG, HG_D)
                od = jnp.zeros((nblk, DIAG, HG_D), F32)
                for s in range(DIAG):
                    dec = jnp.exp(jnp.minimum(b3 - b3[:, s:s + 1, :], 0.0))
                    p = q3 * (k3[:, s:s + 1, :] * dec)
                    rs = jnp.sum(p, axis=-1, keepdims=True)
                    rs = jnp.where(t3 >= s, rs, 0.0)
                    od = od + rs * v3[:, s:s + 1, :]
                o = o + od.reshape(CHUNK, HG_D)
            bl = bh[CHUNK - 1:CHUNK, :]
            kdec = (kh * jnp.exp(bl - bh)).astype(BF16)
            st_scr[hh] = st * jnp.exp(bl) + _dot_tn(vb, kdec)
            o = o * lax.rsqrt(jnp.mean(o * o, axis=-1, keepdims=True) + EPS)
            o = o * hgg_ref[:, cs]
            o = o * g_scr[rows, cs]
            m_scr[rows, cs] = o.astype(BF16)
        return carry

    lax.fori_loop(0, tl // CHUNK, chunk_body, 0, unroll=4)
    xo_ref[0] = x_ref[0] + _dot(m_scr[...], wout_ref[...])

    @pl.when(l == nl - 1)
    def _():
        for hh in range(HG_HEADS):
            hs_ref[0, hh] = st_scr[hh].T
        ls_ref[0] = h_scr[...]
        cs_ref[0] = xpad_scr[5:8, :]


def _mixer_block(x, gain, w_in, w_out, layer, mp, tl=512):
    bsz, seq, d = x.shape
    tl = min(tl, seq)
    full = lambda shape: pl.BlockSpec(shape, lambda b, l: (0,) * len(shape))
    stacked = lambda shape: pl.BlockSpec((None,) + shape, lambda b, l: (layer, 0, 0))
    out_shapes = (
        jax.ShapeDtypeStruct((bsz, seq, d), F32),
        jax.ShapeDtypeStruct((bsz, HG_HEADS, HG_D, HG_D), F32),
        jax.ShapeDtypeStruct((bsz, 1, LRU_WIDTH), F32),
        jax.ShapeDtypeStruct((bsz, CONV_K - 1, LRU_WIDTH), F32),
    )
    half = lambda dt: pltpu.VMEM((tl, HG_WIDTH), dt)
    xo, hg, lru, conv = pl.pallas_call(
        functools.partial(_mixer_block_kernel, tl=tl),
        out_shape=out_shapes,
        grid=(bsz, seq // tl),
        in_specs=[
            pl.BlockSpec((1, tl, d), lambda b, l: (b, l, 0)),
            full((1, d)), stacked((d, IN_WIDTH)),
            full((3, HG_WIDTH)), full((1, HG_WIDTH)), full((CONV_K, LRU_WIDTH)),
            full((1, LRU_WIDTH)), full((LRU_WIDTH, LRU_WIDTH)), full((1, LRU_WIDTH)),
            full((LRU_WIDTH, LRU_WIDTH)), full((1, LRU_WIDTH)), full((1, LRU_WIDTH)),
            full(mp["cm"].shape), stacked((d, d)),
        ],
        out_specs=(
            pl.BlockSpec((1, tl, d), lambda b, l: (b, l, 0)),
            pl.BlockSpec((1, HG_HEADS, HG_D, HG_D), lambda b, l: (b, 0, 0, 0)),
            pl.BlockSpec((1, 1, LRU_WIDTH), lambda b, l: (b, 0, 0)),
            pl.BlockSpec((1, CONV_K - 1, LRU_WIDTH), lambda b, l: (b, 0, 0)),
        ),
        scratch_shapes=[
            pltpu.VMEM((HG_HEADS, HG_D, HG_D), F32),
            pltpu.VMEM((1, LRU_WIDTH), F32),
            pltpu.VMEM((tl + 8, LRU_WIDTH), F32),
            pltpu.VMEM((tl, d), BF16),
            half(F32), half(F32), half(F32), half(F32), half(F32), half(F32), half(F32),
            half(F32),
            pltpu.VMEM((tl, d), BF16),
        ],
        compiler_params=_cparams(("parallel", "arbitrary")),
        name="mixer_block",
    )(x, gain.reshape(1, d), w_in, mp["lbp"], mp["hgg"], mp["cw"], mp["cb"], mp["wra"],
      mp["bra"], mp["wix"], mp["bix"], mp["c8"], mp["cm"], w_out)
    return xo, hg, lru.reshape(bsz, LRU_WIDTH), conv


def _mixer_step_kernel(z_ref, sh_ref, sl_ref, sc_ref, lbp_ref, hgg_ref, cw_ref, cb_ref, wra_ref,
                       bra_ref, wix_ref, bix_ref, c8_ref, *rest, bb, aliased):
    m_ref, hn_ref, ln_ref, cn_ref, o_scr = rest[1:] if aliased else rest
    z = z_ref[...]
    xr = z[:, 4 * HG_WIDTH:4 * HG_WIDTH + LRU_WIDTH]
    gr = z[:, 4 * HG_WIDTH + LRU_WIDTH:IN_WIDTH]
    buf = sc_ref[...]
    xc = cb_ref[...]
    for j in range(CONV_K - 1):
        xc = xc + buf[:, j * LRU_WIDTH:(j + 1) * LRU_WIDTH] * cw_ref[j:j + 1, :]
    xc = xc + xr * cw_ref[CONV_K - 1:CONV_K, :]
    cn_ref[:, 0:(CONV_K - 2) * LRU_WIDTH] = buf[:, LRU_WIDTH:(CONV_K - 1) * LRU_WIDTH]
    cn_ref[:, (CONV_K - 2) * LRU_WIDTH:(CONV_K - 1) * LRU_WIDTH] = xr
    a, u = _lru_gates(xc, wra_ref[...], bra_ref[...], wix_ref[...], bix_ref[...], c8_ref[...])
    hnew = u + a * sl_ref[...]
    ln_ref[...] = hnew
    m_ref[:, HG_WIDTH:D_MODEL] = (hnew * _gelu_tanh(gr)).astype(m_ref.dtype)

    q = _silu(z[:, 0:HG_WIDTH])
    log_f, k = _hgrn_gates(z[:, HG_WIDTH:2 * HG_WIDTH], lbp_ref[0:1, :], lbp_ref[1:2, :],
                           lbp_ref[2:3, :])
    f = jnp.exp(log_f)
    v = z[:, 2 * HG_WIDTH:3 * HG_WIDTH]
    gate = _silu(z[:, 3 * HG_WIDTH:4 * HG_WIDTH])
    zpad = jnp.zeros((HG_D - bb, HG_D), F32)
    for hh in range(HG_HEADS):
        cs = slice(hh * HG_D, (hh + 1) * HG_D)
        ft = jnp.concatenate([f[:, cs], zpad], axis=0).T
        kt = jnp.concatenate([k[:, cs], zpad], axis=0).T
        qt = jnp.concatenate([q[:, cs], zpad], axis=0).T
        for j in range(bb):
            fcol = jnp.broadcast_to(ft[:, j:j + 1], (HG_D, HG_D))
            kcol = jnp.broadcast_to(kt[:, j:j + 1], (HG_D, HG_D))
            qcol = jnp.broadcast_to(qt[:, j:j + 1], (HG_D, HG_D))
            vrow = jnp.broadcast_to(v[j:j + 1, cs], (HG_D, HG_D))
            sn = fcol * sh_ref[j, hh] + kcol * vrow
            hn_ref[j, hh] = sn
            o_scr[j:j + 1, cs] = jnp.sum(qcol * sn, axis=0, keepdims=True)
    for hh in range(HG_HEADS):
        cs = slice(hh * HG_D, (hh + 1) * HG_D)
        o = o_scr[:, cs]
        o = o * lax.rsqrt(jnp.mean(o * o, axis=-1, keepdims=True) + EPS)
        o = o * hgg_ref[:, cs]
        o = o * gate[:, cs]
        m_ref[:, cs] = o.astype(m_ref.dtype)


def _mixer_step(z, s_hg_all, layer, s_lru, s_conv, mp, hg_buf=None, bb=16):
    bsz = z.shape[0]
    bb = min(bb, bsz)
    cw3 = (CONV_K - 1) * LRU_WIDTH
    full = lambda shape: pl.BlockSpec(shape, lambda i: (0,) * len(shape))
    state_spec = pl.BlockSpec((None, bb, HG_HEADS, HG_D, HG_D), lambda i: (layer, i, 0, 0, 0))
    out_shapes = (
        jax.ShapeDtypeStruct((bsz, D_MODEL), mp["wra"].dtype),
        jax.ShapeDtypeStruct(s_hg_all.shape, F32),
        jax.ShapeDtypeStruct((bsz, LRU_WIDTH), F32),
        jax.ShapeDtypeStruct((bsz, cw3), F32),
    )
    in_specs = [
        pl.BlockSpec((bb, IN_WIDTH), lambda i: (i, 0)),
        state_spec,
        pl.BlockSpec((bb, LRU_WIDTH), lambda i: (i, 0)),
        pl.BlockSpec((bb, cw3), lambda i: (i, 0)),
        full((3, HG_WIDTH)), full((1, HG_WIDTH)), full((CONV_K, LRU_WIDTH)),
        full((1, LRU_WIDTH)), full((LRU_WIDTH, LRU_WIDTH)), full((1, LRU_WIDTH)),
        full((LRU_WIDTH, LRU_WIDTH)), full((1, LRU_WIDTH)), full((1, LRU_WIDTH)),
    ]
    args = [z, s_hg_all, s_lru, s_conv.reshape(bsz, cw3), mp["lbp"], mp["hgg"], mp["cw"],
            mp["cb"], mp["wra"], mp["bra"], mp["wix"], mp["bix"], mp["c8"]]
    aliases = {}
    if hg_buf is not None:
        aliases = {len(args): 1}
        in_specs.append(pl.BlockSpec(memory_space=pl.ANY))
        args.append(hg_buf)
    merged, hg, lru, conv = pl.pallas_call(
        functools.partial(_mixer_step_kernel, bb=bb, aliased=hg_buf is not None),
        out_shape=out_shapes,
        grid=(bsz // bb,),
        in_specs=in_specs,
        out_specs=(
            pl.BlockSpec((bb, D_MODEL), lambda i: (i, 0)),
            state_spec,
            pl.BlockSpec((bb, LRU_WIDTH), lambda i: (i, 0)),
            pl.BlockSpec((bb, cw3), lambda i: (i, 0)),
        ),
        scratch_shapes=[pltpu.VMEM((bb, HG_WIDTH), F32)],
        input_output_aliases=aliases,
        compiler_params=_cparams(("parallel",)),
        name="mixer_step",
    )(*args)
    return merged, hg, lru, conv.reshape(bsz, CONV_K - 1, LRU_WIDTH)


def _router_kernel(x_ref, g_ref, rw_ref, tril_ref, info_ref, idx_ref, cnt_ref, zero_ref,
                   carry_scr):
    i = pl.program_id(0)

    @pl.when(i == 0)
    def _():
        carry_scr[...] = jnp.zeros_like(carry_scr)

    zero_ref[...] = jnp.zeros_like(zero_ref)

    h = _rms(x_ref[...], g_ref[...])
    logits = jnp.dot(h, rw_ref[...], preferred_element_type=F32, precision=lax.Precision.HIGHEST)
    lane = lax.broadcasted_iota(I32, logits.shape, 1)
    neg = jnp.float32(-jnp.inf)
    logits = jnp.where(lane < N_EXPERTS, logits, neg)
    m1 = jnp.max(logits, axis=-1, keepdims=True)
    i1 = jnp.min(jnp.where(logits == m1, lane, LANES), axis=-1, keepdims=True)
    l2 = jnp.where(lane == i1, neg, logits)
    m2 = jnp.max(l2, axis=-1, keepdims=True)
    i2 = jnp.min(jnp.where(l2 == m2, lane, LANES), axis=-1, keepdims=True)
    e = jnp.exp(m2 - m1)
    p1 = 1.0 / (1.0 + e)
    p2 = e / (1.0 + e)
    oh1 = (lane == i1).astype(F32)
    oh2 = (lane == i2).astype(F32)
    sel = oh1 + oh2
    before = _dot(tril_ref[...], sel.astype(BF16)) + carry_scr[...]
    r1 = jnp.sum(oh1 * before, axis=-1, keepdims=True)
    r2 = jnp.sum(oh2 * before, axis=-1, keepdims=True)
    carry_scr[...] += jnp.sum(sel, axis=0, keepdims=True)
    info = jnp.where(lane == 0, i1.astype(F32), 0.0)
    info = jnp.where(lane == 1, i2.astype(F32), info)
    info = jnp.where(lane == 2, r1, info)
    info = jnp.where(lane == 3, r2, info)
    info = jnp.where(lane == 4, p1, info)
    info = jnp.where(lane == 5, p2, info)
    info_ref[...] = info
    idx_ref[...] = info.T[0:SUBLANES, :].astype(I32)
    cnt_ref[...] = carry_scr[...]


def _router(x, gain, rw_pad, n_rows, tm=512):
    t, d = x.shape
    tm = min(tm, t)
    zrows = n_rows // (t // tm)
    assert zrows * (t // tm) == n_rows and zrows % SUBLANES == 0
    tril = jnp.tril(jnp.ones((tm, tm), BF16), -1)
    return pl.pallas_call(
        _router_kernel,
        out_shape=(jax.ShapeDtypeStruct((t, LANES), F32),
                   jax.ShapeDtypeStruct((SUBLANES, t), I32),
                   jax.ShapeDtypeStruct((1, LANES), F32),
                   jax.ShapeDtypeStruct((n_rows, d), F32)),
        grid=(t // tm,),
        in_specs=[
            pl.BlockSpec((tm, d), lambda i: (i, 0)),
            pl.BlockSpec((1, d), lambda i: (0, 0)),
            pl.BlockSpec((d, LANES), lambda i: (0, 0)),
            pl.BlockSpec((tm, tm), lambda i: (0, 0)),
        ],
        out_specs=(pl.BlockSpec((tm, LANES), lambda i: (i, 0)),
                   pl.BlockSpec((SUBLANES, tm), lambda i: (0, i)),
                   pl.BlockSpec((1, LANES), lambda i: (0, 0)),
                   pl.BlockSpec((zrows, d), lambda i: (i, 0))),
        scratch_shapes=[pltpu.VMEM((1, LANES), F32)],
        compiler_params=_cparams(("arbitrary",)),
        name="router",
    )(x, gain.reshape(1, d), rw_pad, tril)


def _row_copy(src, dst, s, d, sem):
    return pltpu.make_async_copy(src.at[pl.ds(s, 1)], dst.at[pl.ds(d, 1)], sem)


def _tile_dest(dest, t, td):
    nb = t // td
    dest3 = jnp.concatenate([dest[0].reshape(nb, td), dest[1].reshape(nb, td)], axis=1)
    return dest3.reshape(nb, 1, 2 * td)


def _dispatch_kernel(dest_ref, x_ref, g_ref, xs_in_ref, xs_ref, pk_scr, sem, *, td):
    del xs_in_ref
    pk_scr[...] = _rms(x_ref[...], g_ref[...])

    def issue(r, c):
        _row_copy(pk_scr, xs_ref, r, dest_ref[0, 0, r], sem).start(priority=0)
        _row_copy(pk_scr, xs_ref, r, dest_ref[0, 0, td + r], sem).start(priority=1)
        return c

    lax.fori_loop(0, td, issue, 0, unroll=8)
    for _ in range(2):
        pltpu.make_async_copy(pk_scr, xs_ref.at[pl.ds(0, td)], sem).wait()


def _dispatch(x, gain, dest, xs0, td=512):
    t, d = x.shape
    td = min(td, t)
    n_rows = xs0.shape[0]
    return pl.pallas_call(
        functools.partial(_dispatch_kernel, td=td),
        out_shape=jax.ShapeDtypeStruct((n_rows, d), F32),
        grid=(t // td,),
        in_specs=[
            pl.BlockSpec((1, 1, 2 * td), lambda i: (i, 0, 0), memory_space=pltpu.SMEM),
            pl.BlockSpec((td, d), lambda i: (i, 0)),
            pl.BlockSpec((1, d), lambda i: (0, 0)),
            pl.BlockSpec(memory_space=pl.ANY),
        ],
        out_specs=pl.BlockSpec(memory_space=pl.ANY),
        scratch_shapes=[pltpu.VMEM((td, d), F32), pltpu.SemaphoreType.DMA(())],
        input_output_aliases={3: 0},
        compiler_params=pltpu.CompilerParams(dimension_semantics=("arbitrary",),
                                             vmem_limit_bytes=VMEM_LIMIT_BYTES,
                                             has_side_effects=True),
        name="dispatch",
    )(_tile_dest(dest, t, td), x, gain.reshape(1, d), xs0)


def _combine_kernel(dest_ref, nxt_ref, x_ref, info_ref, g_ref, ys_ref, o_ref, y_scr, sems, *, td):
    i = pl.program_id(0)
    nb = pl.num_programs(0)

    def fetch(idx_ref, slot):
        def issue(r, c):
            _row_copy(ys_ref, y_scr.at[slot, 0], idx_ref[0, 0, r], r,
                      sems.at[slot]).start(priority=0)
            _row_copy(ys_ref, y_scr.at[slot, 1], idx_ref[0, 0, td + r], r,
                      sems.at[slot]).start(priority=1)
            return c

        lax.fori_loop(0, td, issue, 0, unroll=8)

    def finish(slot):
        for j in range(2):
            pltpu.make_async_copy(ys_ref.at[pl.ds(0, td)], y_scr.at[slot, j],
                                  sems.at[slot]).wait()
        p1 = info_ref[:, 4:5]
        p2 = info_ref[:, 5:6]
        x = x_ref[...] + (p1 * y_scr[slot, 0] + p2 * y_scr[slot, 1])
        o_ref[...] = _rms(x, g_ref[...])

    @pl.when(i == 0)
    def _():
        fetch(dest_ref, 0)

    for slot in range(2):
        @pl.when(i % 2 == slot)
        def _(slot=slot):
            @pl.when(i + 1 < nb)
            def _():
                fetch(nxt_ref, 1 - slot)

            finish(slot)


def _combine(x, ys, dest, info, gain, td=256):
    t, d = x.shape
    td = min(td, t)
    nb = t // td
    row = pl.BlockSpec((td, d), lambda i: (i, 0))
    tiled = _tile_dest(dest, t, td)
    return pl.pallas_call(
        functools.partial(_combine_kernel, td=td),
        out_shape=jax.ShapeDtypeStruct((t, d), F32),
        grid=(nb,),
        in_specs=[
            pl.BlockSpec((1, 1, 2 * td), lambda i: (i, 0, 0), memory_space=pltpu.SMEM),
            pl.BlockSpec((1, 1, 2 * td), lambda i: (jnp.minimum(i + 1, nb - 1), 0, 0),
                         memory_space=pltpu.SMEM),
            row,
            pl.BlockSpec((td, LANES), lambda i: (i, 0)),
            pl.BlockSpec((1, d), lambda i: (0, 0)),
            pl.BlockSpec(memory_space=pl.ANY),
        ],
        out_specs=row,
        scratch_shapes=[pltpu.VMEM((2, 2, td, d), F32), pltpu.SemaphoreType.DMA((2,))],
        compiler_params=_cparams(("arbitrary",)),
        name="combine",
    )(tiled, tiled, x, info, gain.reshape(1, d), ys)


def _moe(x, norm_g, rw_pad, w1, w3, w2, final_g, tm_e):
    t, d = x.shape
    n_tiles = (2 * t + N_EXPERTS * (tm_e - 1)) // tm_e
    n_rows = n_tiles * tm_e
    info, idx, cnt, xs0 = _router(x, norm_g, rw_pad, n_rows)
    e1, e2, r1, r2 = idx[0], idx[1], idx[2], idx[3]
    counts = cnt[0, :N_EXPERTS].astype(I32)
    padded = ((counts + tm_e - 1) // tm_e) * tm_e
    ends = jnp.cumsum(padded)
    starts = ends - padded
    dest = jnp.stack([starts[e1] + r1, starts[e2] + r2])
    tile_start = jnp.arange(n_tiles, dtype=I32) * tm_e
    tile_expert = jnp.minimum(
        jnp.sum((tile_start[:, None] >= ends[None, :]).astype(I32), axis=1), N_EXPERTS - 1)
    n_valid = (ends[-1] // tm_e).reshape(1).astype(I32)
    xs = _dispatch(x, norm_g, dest, xs0)
    ys = _ffn(xs, w1, w3, w2, tile_expert.astype(I32), n_valid, tm=tm_e, n_f=2)
    return _combine(x, ys, dest, info, final_g)


def _block_diag(w):
    n, c, d = w.shape
    eye = jnp.eye(n, dtype=w.dtype)
    return (w[:, :, None, :] * eye[:, None, :, None]).reshape(n * c, n * d)


def _cum_matrix():
    t = jnp.arange(CHUNK)
    return (t[None, :] <= t[:, None]).astype(BF16)


def _mixer_params(l, lb, hg_norm_g, conv_w, conv_b, w_ra, b_ra, w_ix, b_ix, lru_lambda, wdtype):
    row = lambda a: a.reshape(1, -1).astype(F32)
    return {
        "lbp": jnp.stack([jnp.log(lb), jnp.log1p(-lb), 1.0 - lb]).astype(F32),
        "hgg": row(hg_norm_g[l]),
        "cw": conv_w[l].astype(F32),
        "cb": row(conv_b[l]),
        "wra": _block_diag(w_ra[l].astype(F32)).astype(wdtype),
        "bra": row(b_ra[l]),
        "wix": _block_diag(w_ix[l].astype(F32)).astype(wdtype),
        "bix": row(b_ix[l]),
        "c8": row(-LRU_C * jax.nn.softplus(-lru_lambda[l].astype(F32))),
        "cm": _cum_matrix(),
    }


def _trunk(x, seq_shape, states, prm):
    bsz, seq = seq_shape
    one = jnp.ones((1,), I32)
    new_hg, new_lru, new_conv = [], [], []
    hg_stack = None if states is None else states[0]
    for l in range(2):
        mp = prm["mixer"][l]
        if states is None:
            x, s1, s2, s3 = _mixer_block(x.reshape(bsz, seq, D_MODEL), prm["norm1_g"][l],
                                         prm["w_in"], prm["w_out"], l, mp)
            x = x.reshape(bsz * seq, D_MODEL)
            new_hg.append(s1)
        else:
            z = _linear(x, prm["w_in"], l, gain=prm["norm1_g"][l])
            merged, hg_stack, s2, s3 = _mixer_step(z, states[0], l, states[1][l], states[2][l],
                                                   mp, hg_buf=hg_stack)
            x = _linear(merged, prm["w_out"], l, res=x)
        new_lru.append(s2)
        new_conv.append(s3)
        if l == 0:
            n_tiles = max(x.shape[0] // TILE_ROWS, 1)
            n_f = 1 if prm["ffn_w1"].dtype == BF16 else prm["ffn_w1"].shape[2] // F_CHUNK
            x = _ffn(x, prm["ffn_w1"], prm["ffn_w3"], prm["ffn_w2"], jnp.zeros((n_tiles,), I32),
                     one * n_tiles, gain=prm["norm2_g"][0], res=True, tm=TILE_ROWS, n_f=n_f,
                     ck=F_CHUNK)
        else:
            x = _moe(x, prm["norm2_g"][1], prm["rw_pad"], prm["moe_w1"], prm["moe_w3"],
                     prm["moe_w2"], prm["final_norm_g"],
                     tm_e=TILE_ROWS if states is None else LANES)
    hg_all = jnp.stack(new_hg) if states is None else hg_stack
    return x, hg_all, jnp.stack(new_lru), jnp.stack(new_conv)


def kernel(x_prompt, x_sample, state_hgrn, state_lru, state_conv, norm1_g, w_in, lower_bounds,
           hg_norm_g, conv_w, conv_b, w_ra, b_ra, w_ix, b_ix, lru_lambda, w_out, norm2_g,
           ffn_w1, ffn_w3, ffn_w2, router_w, moe_w1, moe_w3, moe_w2, final_norm_g):
    lb_all = jnp.cumsum(jax.nn.softmax(lower_bounds.astype(F32), axis=0), axis=0)
    lb_all = lb_all - lb_all[0]
    experts = {
        "moe_w1": moe_w1[0].astype(BF16), "moe_w3": moe_w3[0].astype(BF16),
        "moe_w2": moe_w2[0].astype(BF16),
        "rw_pad": jnp.pad(router_w[0].astype(F32), ((0, 0), (0, LANES - N_EXPERTS))),
        "norm1_g": norm1_g, "norm2_g": norm2_g, "final_norm_g": final_norm_g,
    }

    def params(wdtype):
        return dict(
            experts,
            mixer=[_mixer_params(l, lb_all[l], hg_norm_g, conv_w, conv_b, w_ra, b_ra, w_ix, b_ix,
                                 lru_lambda, wdtype) for l in range(2)],
            w_in=w_in.astype(wdtype), w_out=w_out.astype(wdtype),
            ffn_w1=ffn_w1.astype(wdtype), ffn_w3=ffn_w3.astype(wdtype),
            ffn_w2=ffn_w2.astype(wdtype))

    bp, sp, d = x_prompt.shape
    bs = x_sample.shape[0]
    y_p, hg_p, lru_p, conv_p = _trunk(x_prompt.reshape(bp * sp, d), (bp, sp), None, params(BF16))
    y_s, hg_s, lru_s, conv_s = _trunk(x_sample.reshape(bs, d), (bs, 1),
                                      (state_hgrn, state_lru, state_conv), params(F32))
    return (y_p.reshape(bp, sp, d), y_s.reshape(bs, 1, d), hg_p, lru_p, conv_p, hg_s, lru_s,
            conv_s)
```

```python
import functools

import jax
import jax.numpy as jnp
from jax import lax
from jax.experimental import pallas as pl
from jax.experimental.pallas import tpu as pltpu

F32 = jnp.float32
BF16 = jnp.bfloat16
I32 = jnp.int32

D_MODEL = 1024
HG_WIDTH = 512
HG_HEADS = 4
HG_D = 128
LRU_WIDTH = 512
LRU_BLOCKS = 8
LRU_C = 8.0
CONV_K = 4
IN_WIDTH = 4 * HG_WIDTH + 2 * LRU_WIDTH
N_EXPERTS = 8
EPS = 1e-6

CHUNK = 64
DIAG = 1
LANES = 128
SUBLANES = 8
TILE_ROWS = 512
F_CHUNK = 256
VMEM_LIMIT_BYTES = 56 * 1024 * 1024


def _cparams(sem):
    return pltpu.CompilerParams(dimension_semantics=sem, vmem_limit_bytes=VMEM_LIMIT_BYTES)


def _sigmoid(x):
    return 1.0 / (1.0 + jnp.exp(-x))


def _silu(x):
    return x * _sigmoid(x)


def _gelu_tanh(x):
    c = 0.7978845608028654
    return 0.5 * x * (1.0 + jnp.tanh(c * (x + 0.044715 * (x * x * x))))


def _rms(x, g):
    return x * lax.rsqrt(jnp.mean(x * x, axis=-1, keepdims=True) + EPS) * g


def _dot(a, b):
    if b.dtype == F32:
        return jnp.dot(a.astype(F32), b, preferred_element_type=F32,
                       precision=lax.Precision.HIGHEST)
    return jnp.dot(a.astype(BF16), b, preferred_element_type=F32)


def _dot_nt(a, b):
    return lax.dot_general(a, b, (((1,), (1,)), ((), ())), preferred_element_type=F32)


def _dot_tn(a, b):
    return lax.dot_general(a, b, (((0,), (0,)), ((), ())), preferred_element_type=F32)


def _split3(x):
    x1 = x.astype(BF16)
    r1 = x - x1.astype(F32)
    x2 = r1.astype(BF16)
    x3 = (r1 - x2.astype(F32)).astype(BF16)
    return x1, x2, x3


def _linear_kernel(*refs, has_gain, has_res):
    it = iter(refs)
    x_ref = next(it)
    g_ref = next(it) if has_gain else None
    w_ref = next(it)
    r_ref = next(it) if has_res else None
    o_ref = next(it)
    h_scr = next(it)

    @pl.when(pl.program_id(1) == 0)
    def _():
        x = x_ref[...].astype(F32)
        if has_gain:
            x = _rms(x, g_ref[...])
        h_scr[...] = x.astype(h_scr.dtype)

    acc = _dot(h_scr[...], w_ref[...])
    if has_res:
        acc = acc + r_ref[...]
    o_ref[...] = acc.astype(o_ref.dtype)


def _linear(x, w, layer, gain=None, res=None, tm=1024, tn=1024):
    m, k = x.shape
    n = w.shape[2]
    tm = min(tm, m)
    tn = min(tn, n)
    in_specs = [pl.BlockSpec((tm, k), lambda i, j: (i, 0))]
    args = [x]
    if gain is not None:
        in_specs.append(pl.BlockSpec((1, k), lambda i, j: (0, 0)))
        args.append(gain.reshape(1, k))
    in_specs.append(pl.BlockSpec((None, k, tn), lambda i, j: (layer, 0, j)))
    args.append(w)
    if res is not None:
        in_specs.append(pl.BlockSpec((tm, tn), lambda i, j: (i, j)))
        args.append(res)
    return pl.pallas_call(
        functools.partial(_linear_kernel, has_gain=gain is not None, has_res=res is not None),
        out_shape=jax.ShapeDtypeStruct((m, n), F32),
        grid=(m // tm, n // tn),
        in_specs=in_specs,
        out_specs=pl.BlockSpec((tm, tn), lambda i, j: (i, j)),
        scratch_shapes=[pltpu.VMEM((tm, k), w.dtype)],
        compiler_params=_cparams(("parallel", "arbitrary")),
        name="linear",
    )(*args)


def _ffn_kernel(te_ref, nv_ref, *refs, has_gain, has_res, tf, ck):
    del te_ref
    it = iter(refs)
    x_ref = next(it)
    g_ref = next(it) if has_gain else None
    w1_ref, w3_ref, w2_ref, o_ref, xb_scr, a_scr, acc_scr = it
    i = pl.program_id(0)
    j = pl.program_id(1)
    nj = pl.num_programs(1)
    valid = i < nv_ref[0]

    @pl.when(valid)
    def _():
        @pl.when(j == 0)
        def _():
            x = x_ref[...]
            if has_gain:
                x = _rms(x, g_ref[...])
            xb_scr[...] = x.astype(xb_scr.dtype)

        xb = xb_scr[...]
        for c in range(tf // ck):
            h1 = _dot(xb, w1_ref[0, :, c * ck:(c + 1) * ck])
            h3 = _dot(xb, w3_ref[0, :, c * ck:(c + 1) * ck])
            a_scr[:, c * ck:(c + 1) * ck] = (_silu(h1) * h3).astype(a_scr.dtype)
        part = _dot(a_scr[...], w2_ref[0])

        @pl.when(j == 0)
        def _():
            acc_scr[...] = part

        @pl.when(j > 0)
        def _():
            acc_scr[...] += part

        @pl.when(j == nj - 1)
        def _():
            out = acc_scr[...]
            if has_res:
                out = x_ref[...] + out
            o_ref[...] = out

    @pl.when(jnp.logical_and(jnp.logical_not(valid), j == nj - 1))
    def _():
        o_ref[...] = jnp.zeros_like(o_ref)


def _ffn(x, w1, w3, w2, tile_expert, n_valid, gain=None, res=False, tm=TILE_ROWS, n_f=2,
         ck=F_CHUNK):
    p, d = x.shape
    f = w1.shape[2]
    tm = min(tm, p)
    tf = f // n_f
    n_tiles = p // tm
    in_specs = [pl.BlockSpec((tm, d), lambda i, j, te, nv: (i, 0))]
    args = [x]
    if gain is not None:
        in_specs.append(pl.BlockSpec((1, d), lambda i, j, te, nv: (0, 0)))
        args.append(gain.reshape(1, d))
    in_specs += [
        pl.BlockSpec((1, d, tf), lambda i, j, te, nv: (te[i], 0, j)),
        pl.BlockSpec((1, d, tf), lambda i, j, te, nv: (te[i], 0, j)),
        pl.BlockSpec((1, tf, d), lambda i, j, te, nv: (te[i], j, 0)),
    ]
    args += [w1, w3, w2]
    grid_spec = pltpu.PrefetchScalarGridSpec(
        num_scalar_prefetch=2,
        grid=(n_tiles, n_f),
        in_specs=in_specs,
        out_specs=pl.BlockSpec((tm, d), lambda i, j, te, nv: (i, 0)),
        scratch_shapes=[pltpu.VMEM((tm, d), w1.dtype), pltpu.VMEM((tm, tf), w1.dtype),
                        pltpu.VMEM((tm, d), F32)],
    )
    return pl.pallas_call(
        functools.partial(_ffn_kernel, has_gain=gain is not None, has_res=res, tf=tf, ck=ck),
        out_shape=jax.ShapeDtypeStruct((p, d), F32),
        grid_spec=grid_spec,
        compiler_params=_cparams(("arbitrary", "arbitrary")),
        name="swiglu",
    )(tile_expert, n_valid, *args)


def _hgrn_gates(fz, loglb, l1mlb, oneml):
    e = jnp.exp(-jnp.abs(fz))
    ope = 1.0 + e
    log_sig = jnp.minimum(fz, 0.0) - jnp.log(ope)
    cc = l1mlb + log_sig
    log_f = jnp.maximum(loglb, cc) + jnp.log(1.0 + jnp.exp(-jnp.abs(loglb - cc)))
    k = oneml * (jnp.where(fz > 0.0, e, 1.0) / ope)
    return log_f, k


def _level_reference(b, w):
    n = b.shape[0]
    if w >= 4:
        return jnp.concatenate(
            [jnp.broadcast_to(b[j + w:j + w + 1, :], (2 * w, b.shape[1]))
             for j in range(0, n, 2 * w)], axis=0)
    b3 = b.reshape(n // SUBLANES, SUBLANES, b.shape[1])
    ahead = lambda s: pltpu.roll(b3, SUBLANES - s, 1)
    pos = lax.broadcasted_iota(I32, (1, SUBLANES, 1), 1) & (2 * w - 1)
    if w == 2:
        r3 = jnp.where(pos == 0, ahead(2),
                       jnp.where(pos == 1, ahead(1),
                                 jnp.where(pos == 2, b3, pltpu.roll(b3, 1, 1))))
    else:
        r3 = jnp.where(pos == 0, ahead(1), b3)
    return r3.reshape(b.shape)


def _lru_gates(xc, wra, bra, wix, bix, c8):
    r = _sigmoid(_dot(xc, wra) + bra)
    ig = _sigmoid(_dot(xc, wix) + bix)
    log_a = c8 * r
    a = jnp.exp(log_a)
    one_m_a2 = -jnp.tanh(log_a) * (a * a + 1.0)
    u = jnp.sqrt(one_m_a2) * (ig * xc)
    return a, u


def _mixer_block_kernel(x_ref, g1_ref, win_ref, lbp_ref, hgg_ref, cw_ref, cb_ref, wra_ref,
                        bra_ref, wix_ref, bix_ref, c8_ref, cm_ref, wout_ref,
                        xo_ref, hs_ref, ls_ref, cs_ref,
                        st_scr, h_scr, xpad_scr, hb_scr, q_scr, lf_scr, k_scr, v_scr, g_scr,
                        gl_scr, a_scr, u_scr, m_scr, *, tl):
    l = pl.program_id(1)
    nl = pl.num_programs(1)

    @pl.when(l == 0)
    def _():
        st_scr[...] = jnp.zeros_like(st_scr)
        h_scr[...] = jnp.zeros_like(h_scr)
        xpad_scr[0:8, :] = jnp.zeros((8, LRU_WIDTH), F32)

    hb_scr[...] = _rms(x_ref[0], g1_ref[...]).astype(BF16)
    hb = hb_scr[...]
    zg = lambda j: _dot(hb, win_ref[:, j * HG_WIDTH:(j + 1) * HG_WIDTH])
    q_scr[...] = _silu(zg(0))
    log_f, kk = _hgrn_gates(zg(1), lbp_ref[0:1, :], lbp_ref[1:2, :], lbp_ref[2:3, :])
    lf_scr[...] = log_f
    k_scr[...] = kk
    v_scr[...] = zg(2)
    g_scr[...] = _silu(zg(3))
    xpad_scr[8:8 + tl, :] = zg(4)
    gl_scr[...] = _gelu_tanh(zg(5))
    xc = cb_ref[...]
    for j in range(CONV_K):
        xc = xc + xpad_scr[5 + j:5 + j + tl, :] * cw_ref[j:j + 1, :]
    tail = xpad_scr[tl + 5:tl + 8, :]
    xpad_scr[5:8, :] = tail
    a, u = _lru_gates(xc, wra_ref[...], bra_ref[...], wix_ref[...], bix_ref[...], c8_ref[...])
    a_scr[...] = a
    u_scr[...] = u

    row = lax.broadcasted_iota(I32, (CHUNK, 1), 0)
    ti = lax.broadcasted_iota(I32, (CHUNK, CHUNK), 0)
    si = lax.broadcasted_iota(I32, (CHUNK, CHUNK), 1)
    t3 = lax.broadcasted_iota(I32, (1, DIAG, 1), 1)
    sub3 = lax.broadcasted_iota(I32, (1, SUBLANES, 1), 1)
    nblk = CHUNK // DIAG
    ngrp = CHUNK // SUBLANES
    levels = []
    w = CHUNK // 2
    while w >= DIAG:
        sh = (2 * w).bit_length() - 1
        pairs = ((ti >> sh) == (si >> sh)) & ((ti & (2 * w - 1)) >= w) & ((si & (2 * w - 1)) < w)
        levels.append((w, (row & (2 * w - 1)) >= w, pairs))
        w //= 2

    def chunk_body(c, carry):
        r0 = pl.multiple_of(c * CHUNK, CHUNK)
        rows = pl.ds(r0, CHUNK)

        for gi in range(LRU_WIDTH // LANES):
            cs = slice(gi * LANES, (gi + 1) * LANES)
            aa = a_scr[rows, cs].reshape(ngrp, SUBLANES, LANES)
            uu = u_scr[rows, cs].reshape(ngrp, SUBLANES, LANES)
            d = 1
            while d < SUBLANES:
                keep = sub3 >= d
                a_sh = pltpu.roll(aa, d, 1)
                u_sh = pltpu.roll(uu, d, 1)
                uu = jnp.where(keep, aa * u_sh + uu, uu)
                aa = jnp.where(keep, aa * a_sh, aa)
                d *= 2
            hprev = h_scr[:, cs]
            groups = []
            for j in range(ngrp):
                hj = uu[j] + aa[j] * hprev
                groups.append(hj)
                hprev = hj[SUBLANES - 1:SUBLANES, :]
            hs = jnp.concatenate(groups, axis=0)
            h_scr[:, cs] = hprev
            m_scr[rows, HG_WIDTH + gi * LANES:HG_WIDTH + (gi + 1) * LANES] = (
                hs * gl_scr[rows, cs]).astype(BF16)

        q = q_scr[rows, :]
        k = k_scr[rows, :]
        v = v_scr[rows, :]
        f1, f2, f3 = _split3(lf_scr[rows, :])
        cm = cm_ref[...]
        b = _dot(cm, f1) + _dot(cm, f2) + _dot(cm, f3)

        for hh in range(HG_HEADS):
            cs = slice(hh * HG_D, (hh + 1) * HG_D)
            qh, kh, vh, bh = q[:, cs], k[:, cs], v[:, cs], b[:, cs]
            vb = vh.astype(BF16)
            amat = jnp.zeros((CHUNK, CHUNK), F32)
            for w, upper, pairs in levels:
                dec = jnp.exp(-jnp.abs(bh - _level_reference(bh, w)))
                y = (jnp.where(upper, qh, kh) * dec).astype(BF16)
                amat = jnp.where(pairs, _dot_nt(y, y), amat)
            o = _dot(amat.astype(BF16), vb)
            st = st_scr[hh]
            o = o + _dot_nt((qh * jnp.exp(bh)).astype(BF16), st.astype(BF16))
            if DIAG == 1:
                o = o + jnp.sum(qh * kh, axis=-1, keepdims=True) * vh
            else:
                q3 = qh.reshape(nblk, DIAG, HG_D)
                k3 = kh.reshape(nblk, DIAG, HG_D)
                v3 = vh.reshape(nblk, DIAG, HG_D)
                b3 = bh.reshape(nblk, DIAG, HG_D)
                od = jnp.zeros((nblk, DIAG, HG_D), F32)
                for s in range(DIAG):
                    dec = jnp.exp(jnp.minimum(b3 - b3[:, s:s + 1, :], 0.0))
                    p = q3 * (k3[:, s:s + 1, :] * dec)
                    rs = jnp.sum(p, axis=-1, keepdims=True)
                    rs = jnp.where(t3 >= s, rs, 0.0)
                    od = od + rs * v3[:, s:s + 1, :]
                o = o + od.reshape(CHUNK, HG_D)
            bl = bh[CHUNK - 1:CHUNK, :]
            kdec = (kh * jnp.exp(bl - bh)).astype(BF16)
            st_scr[hh] = st * jnp.exp(bl) + _dot_tn(vb, kdec)
            o = o * lax.rsqrt(jnp.mean(o * o, axis=-1, keepdims=True) + EPS)
            o = o * hgg_ref[:, cs]
            o = o * g_scr[rows, cs]
            m_scr[rows, cs] = o.astype(BF16)
        return carry

    lax.fori_loop(0, tl // CHUNK, chunk_body, 0, unroll=4)
    xo_ref[0] = x_ref[0] + _dot(m_scr[...], wout_ref[...])

    @pl.when(l == nl - 1)
    def _():
        for hh in range(HG_HEADS):
            hs_ref[0, hh] = st_scr[hh].T
        ls_ref[0] = h_scr[...]
        cs_ref[0] = xpad_scr[5:8, :]


def _mixer_block(x, gain, w_in, w_out, layer, mp, tl=512):
    bsz, seq, d = x.shape
    tl = min(tl, seq)
    full = lambda shape: pl.BlockSpec(shape, lambda b, l: (0,) * len(shape))
    stacked = lambda shape: pl.BlockSpec((None,) + shape, lambda b, l: (layer, 0, 0))
    out_shapes = (
        jax.ShapeDtypeStruct((bsz, seq, d), F32),
        jax.ShapeDtypeStruct((bsz, HG_HEADS, HG_D, HG_D), F32),
        jax.ShapeDtypeStruct((bsz, 1, LRU_WIDTH), F32),
        jax.ShapeDtypeStruct((bsz, CONV_K - 1, LRU_WIDTH), F32),
    )
    half = lambda dt: pltpu.VMEM((tl, HG_WIDTH), dt)
    xo, hg, lru, conv = pl.pallas_call(
        functools.partial(_mixer_block_kernel, tl=tl),
        out_shape=out_shapes,
        grid=(bsz, seq // tl),
        in_specs=[
            pl.BlockSpec((1, tl, d), lambda b, l: (b, l, 0)),
            full((1, d)), stacked((d, IN_WIDTH)),
            full((3, HG_WIDTH)), full((1, HG_WIDTH)), full((CONV_K, LRU_WIDTH)),
            full((1, LRU_WIDTH)), full((LRU_WIDTH, LRU_WIDTH)), full((1, LRU_WIDTH)),
            full((LRU_WIDTH, LRU_WIDTH)), full((1, LRU_WIDTH)), full((1, LRU_WIDTH)),
            full(mp["cm"].shape), stacked((d, d)),
        ],
        out_specs=(
            pl.BlockSpec((1, tl, d), lambda b, l: (b, l, 0)),
            pl.BlockSpec((1, HG_HEADS, HG_D, HG_D), lambda b, l: (b, 0, 0, 0)),
            pl.BlockSpec((1, 1, LRU_WIDTH), lambda b, l: (b, 0, 0)),
            pl.BlockSpec((1, CONV_K - 1, LRU_WIDTH), lambda b, l: (b, 0, 0)),
        ),
        scratch_shapes=[
            pltpu.VMEM((HG_HEADS, HG_D, HG_D), F32),
            pltpu.VMEM((1, LRU_WIDTH), F32),
            pltpu.VMEM((tl + 8, LRU_WIDTH), F32),
            pltpu.VMEM((tl, d), BF16),
            half(F32), half(F32), half(F32), half(F32), half(F32), half(F32), half(F32),
            half(F32),
            pltpu.VMEM((tl, d), BF16),
        ],
        compiler_params=_cparams(("parallel", "arbitrary")),
        name="mixer_block",
    )(x, gain.reshape(1, d), w_in, mp["lbp"], mp["hgg"], mp["cw"], mp["cb"], mp["wra"],
      mp["bra"], mp["wix"], mp["bix"], mp["c8"], mp["cm"], w_out)
    return xo, hg, lru.reshape(bsz, LRU_WIDTH), conv


def _mixer_step_kernel(z_ref, sh_ref, sl_ref, sc_ref, lbp_ref, hgg_ref, cw_ref, cb_ref, wra_ref,
                       bra_ref, wix_ref, bix_ref, c8_ref, *rest, bb, aliased):
    m_ref, hn_ref, ln_ref, cn_ref, o_scr = rest[1:] if aliased else rest
    z = z_ref[...]
    xr = z[:, 4 * HG_WIDTH:4 * HG_WIDTH + LRU_WIDTH]
    gr = z[:, 4 * HG_WIDTH + LRU_WIDTH:IN_WIDTH]
    buf = sc_ref[...]
    xc = cb_ref[...]
    for j in range(CONV_K - 1):
        xc = xc + buf[:, j * LRU_WIDTH:(j + 1) * LRU_WIDTH] * cw_ref[j:j + 1, :]
    xc = xc + xr * cw_ref[CONV_K - 1:CONV_K, :]
    cn_ref[:, 0:(CONV_K - 2) * LRU_WIDTH] = buf[:, LRU_WIDTH:(CONV_K - 1) * LRU_WIDTH]
    cn_ref[:, (CONV_K - 2) * LRU_WIDTH:(CONV_K - 1) * LRU_WIDTH] = xr
    a, u = _lru_gates(xc, wra_ref[...], bra_ref[...], wix_ref[...], bix_ref[...], c8_ref[...])
    hnew = u + a * sl_ref[...]
    ln_ref[...] = hnew
    m_ref[:, HG_WIDTH:D_MODEL] = (hnew * _gelu_tanh(gr)).astype(m_ref.dtype)

    q = _silu(z[:, 0:HG_WIDTH])
    log_f, k = _hgrn_gates(z[:, HG_WIDTH:2 * HG_WIDTH], lbp_ref[0:1, :], lbp_ref[1:2, :],
                           lbp_ref[2:3, :])
    f = jnp.exp(log_f)
    v = z[:, 2 * HG_WIDTH:3 * HG_WIDTH]
    gate = _silu(z[:, 3 * HG_WIDTH:4 * HG_WIDTH])
    zpad = jnp.zeros((HG_D - bb, HG_D), F32)
    for hh in range(HG_HEADS):
        cs = slice(hh * HG_D, (hh + 1) * HG_D)
        ft = jnp.concatenate([f[:, cs], zpad], axis=0).T
        kt = jnp.concatenate([k[:, cs], zpad], axis=0).T
        qt = jnp.concatenate([q[:, cs], zpad], axis=0).T
        for j in range(bb):
            fcol = jnp.broadcast_to(ft[:, j:j + 1], (HG_D, HG_D))
            kcol = jnp.broadcast_to(kt[:, j:j + 1], (HG_D, HG_D))
            qcol = jnp.broadcast_to(qt[:, j:j + 1], (HG_D, HG_D))
            vrow = jnp.broadcast_to(v[j:j + 1, cs], (HG_D, HG_D))
            sn = fcol * sh_ref[j, hh] + kcol * vrow
            hn_ref[j, hh] = sn
            o_scr[j:j + 1, cs] = jnp.sum(qcol * sn, axis=0, keepdims=True)
    for hh in range(HG_HEADS):
        cs = slice(hh * HG_D, (hh + 1) * HG_D)
        o = o_scr[:, cs]
        o = o * lax.rsqrt(jnp.mean(o * o, axis=-1, keepdims=True) + EPS)
        o = o * hgg_ref[:, cs]
        o = o * gate[:, cs]
        m_ref[:, cs] = o.astype(m_ref.dtype)


def _mixer_step(z, s_hg_all, layer, s_lru, s_conv, mp, hg_buf=None, bb=16):
    bsz = z.shape[0]
    bb = min(bb, bsz)
    cw3 = (CONV_K - 1) * LRU_WIDTH
    full = lambda shape: pl.BlockSpec(shape, lambda i: (0,) * len(shape))
    state_spec = pl.BlockSpec((None, bb, HG_HEADS, HG_D, HG_D), lambda i: (layer, i, 0, 0, 0))
    out_shapes = (
        jax.ShapeDtypeStruct((bsz, D_MODEL), mp["wra"].dtype),
        jax.ShapeDtypeStruct(s_hg_all.shape, F32),
        jax.ShapeDtypeStruct((bsz, LRU_WIDTH), F32),
        jax.ShapeDtypeStruct((bsz, cw3), F32),
    )
    in_specs = [
        pl.BlockSpec((bb, IN_WIDTH), lambda i: (i, 0)),
        state_spec,
        pl.BlockSpec((bb, LRU_WIDTH), lambda i: (i, 0)),
        pl.BlockSpec((bb, cw3), lambda i: (i, 0)),
        full((3, HG_WIDTH)), full((1, HG_WIDTH)), full((CONV_K, LRU_WIDTH)),
        full((1, LRU_WIDTH)), full((LRU_WIDTH, LRU_WIDTH)), full((1, LRU_WIDTH)),
        full((LRU_WIDTH, LRU_WIDTH)), full((1, LRU_WIDTH)), full((1, LRU_WIDTH)),
    ]
    args = [z, s_hg_all, s_lru, s_conv.reshape(bsz, cw3), mp["lbp"], mp["hgg"], mp["cw"],
            mp["cb"], mp["wra"], mp["bra"], mp["wix"], mp["bix"], mp["c8"]]
    aliases = {}
    if hg_buf is not None:
        aliases = {len(args): 1}
        in_specs.append(pl.BlockSpec(memory_space=pl.ANY))
        args.append(hg_buf)
    merged, hg, lru, conv = pl.pallas_call(
        functools.partial(_mixer_step_kernel, bb=bb, aliased=hg_buf is not None),
        out_shape=out_shapes,
        grid=(bsz // bb,),
        in_specs=in_specs,
        out_specs=(
            pl.BlockSpec((bb, D_MODEL), lambda i: (i, 0)),
            state_spec,
            pl.BlockSpec((bb, LRU_WIDTH), lambda i: (i, 0)),
            pl.BlockSpec((bb, cw3), lambda i: (i, 0)),
        ),
        scratch_shapes=[pltpu.VMEM((bb, HG_WIDTH), F32)],
        input_output_aliases=aliases,
        compiler_params=_cparams(("parallel",)),
        name="mixer_step",
    )(*args)
    return merged, hg, lru, conv.reshape(bsz, CONV_K - 1, LRU_WIDTH)


def _router_kernel(x_ref, g_ref, rw_ref, tril_ref, info_ref, idx_ref, cnt_ref, zero_ref,
                   carry_scr):
    i = pl.program_id(0)

    @pl.when(i == 0)
    def _():
        carry_scr[...] = jnp.zeros_like(carry_scr)

    zero_ref[...] = jnp.zeros_like(zero_ref)

    h = _rms(x_ref[...], g_ref[...])
    logits = jnp.dot(h, rw_ref[...], preferred_element_type=F32, precision=lax.Precision.HIGHEST)
    lane = lax.broadcasted_iota(I32, logits.shape, 1)
    neg = jnp.float32(-jnp.inf)
    logits = jnp.where(lane < N_EXPERTS, logits, neg)
    m1 = jnp.max(logits, axis=-1, keepdims=True)
    i1 = jnp.min(jnp.where(logits == m1, lane, LANES), axis=-1, keepdims=True)
    l2 = jnp.where(lane == i1, neg, logits)
    m2 = jnp.max(l2, axis=-1, keepdims=True)
    i2 = jnp.min(jnp.where(l2 == m2, lane, LANES), axis=-1, keepdims=True)
    e = jnp.exp(m2 - m1)
    p1 = 1.0 / (1.0 + e)
    p2 = e / (1.0 + e)
    oh1 = (lane == i1).astype(F32)
    oh2 = (lane == i2).astype(F32)
    sel = oh1 + oh2
    before = _dot(tril_ref[...], sel.astype(BF16)) + carry_scr[...]
    r1 = jnp.sum(oh1 * before, axis=-1, keepdims=True)
    r2 = jnp.sum(oh2 * before, axis=-1, keepdims=True)
    carry_scr[...] += jnp.sum(sel, axis=0, keepdims=True)
    info = jnp.where(lane == 0, i1.astype(F32), 0.0)
    info = jnp.where(lane == 1, i2.astype(F32), info)
    info = jnp.where(lane == 2, r1, info)
    info = jnp.where(lane == 3, r2, info)
    info = jnp.where(lane == 4, p1, info)
    info = jnp.where(lane == 5, p2, info)
    info_ref[...] = info
    idx_ref[...] = info.T[0:SUBLANES, :].astype(I32)
    cnt_ref[...] = carry_scr[...]


def _router(x, gain, rw_pad, n_rows, tm=512):
    t, d = x.shape
    tm = min(tm, t)
    zrows = n_rows // (t // tm)
    assert zrows * (t // tm) == n_rows and zrows % SUBLANES == 0
    tril = jnp.tril(jnp.ones((tm, tm), BF16), -1)
    return pl.pallas_call(
        _router_kernel,
        out_shape=(jax.ShapeDtypeStruct((t, LANES), F32),
                   jax.ShapeDtypeStruct((SUBLANES, t), I32),
                   jax.ShapeDtypeStruct((1, LANES), F32),
                   jax.ShapeDtypeStruct((n_rows, d), F32)),
        grid=(t // tm,),
        in_specs=[
            pl.BlockSpec((tm, d), lambda i: (i, 0)),
            pl.BlockSpec((1, d), lambda i: (0, 0)),
            pl.BlockSpec((d, LANES), lambda i: (0, 0)),
            pl.BlockSpec((tm, tm), lambda i: (0, 0)),
        ],
        out_specs=(pl.BlockSpec((tm, LANES), lambda i: (i, 0)),
                   pl.BlockSpec((SUBLANES, tm), lambda i: (0, i)),
                   pl.BlockSpec((1, LANES), lambda i: (0, 0)),
                   pl.BlockSpec((zrows, d), lambda i: (i, 0))),
        scratch_shapes=[pltpu.VMEM((1, LANES), F32)],
        compiler_params=_cparams(("arbitrary",)),
        name="router",
    )(x, gain.reshape(1, d), rw_pad, tril)


def _row_copy(src, dst, s, d, sem):
    return pltpu.make_async_copy(src.at[pl.ds(s, 1)], dst.at[pl.ds(d, 1)], sem)


def _tile_dest(dest, t, td):
    nb = t // td
    dest3 = jnp.concatenate([dest[0].reshape(nb, td), dest[1].reshape(nb, td)], axis=1)
    return dest3.reshape(nb, 1, 2 * td)


def _dispatch_kernel(dest_ref, x_ref, g_ref, xs_in_ref, xs_ref, pk_scr, sem, *, td):
    del xs_in_ref
    pk_scr[...] = _rms(x_ref[...], g_ref[...])

    def issue(r, c):
        _row_copy(pk_scr, xs_ref, r, dest_ref[0, 0, r], sem).start(priority=0)
        _row_copy(pk_scr, xs_ref, r, dest_ref[0, 0, td + r], sem).start(priority=1)
        return c

    lax.fori_loop(0, td, issue, 0, unroll=8)
    for _ in range(2):
        pltpu.make_async_copy(pk_scr, xs_ref.at[pl.ds(0, td)], sem).wait()


def _dispatch(x, gain, dest, xs0, td=512):
    t, d = x.shape
    td = min(td, t)
    n_rows = xs0.shape[0]
    return pl.pallas_call(
        functools.partial(_dispatch_kernel, td=td),
        out_shape=jax.ShapeDtypeStruct((n_rows, d), F32),
        grid=(t // td,),
        in_specs=[
            pl.BlockSpec((1, 1, 2 * td), lambda i: (i, 0, 0), memory_space=pltpu.SMEM),
            pl.BlockSpec((td, d), lambda i: (i, 0)),
            pl.BlockSpec((1, d), lambda i: (0, 0)),
            pl.BlockSpec(memory_space=pl.ANY),
        ],
        out_specs=pl.BlockSpec(memory_space=pl.ANY),
        scratch_shapes=[pltpu.VMEM((td, d), F32), pltpu.SemaphoreType.DMA(())],
        input_output_aliases={3: 0},
        compiler_params=pltpu.CompilerParams(dimension_semantics=("arbitrary",),
                                             vmem_limit_bytes=VMEM_LIMIT_BYTES,
                                             has_side_effects=True),
        name="dispatch",
    )(_tile_dest(dest, t, td), x, gain.reshape(1, d), xs0)


def _combine_kernel(dest_ref, nxt_ref, x_ref, info_ref, g_ref, ys_ref, o_ref, y_scr, sems, *, td):
    i = pl.program_id(0)
    nb = pl.num_programs(0)

    def fetch(idx_ref, slot):
        def issue(r, c):
            _row_copy(ys_ref, y_scr.at[slot, 0], idx_ref[0, 0, r], r,
                      sems.at[slot]).start(priority=0)
            _row_copy(ys_ref, y_scr.at[slot, 1], idx_ref[0, 0, td + r], r,
                      sems.at[slot]).start(priority=1)
            return c

        lax.fori_loop(0, td, issue, 0, unroll=8)

    def finish(slot):
        for j in range(2):
            pltpu.make_async_copy(ys_ref.at[pl.ds(0, td)], y_scr.at[slot, j],
                                  sems.at[slot]).wait()
        p1 = info_ref[:, 4:5]
        p2 = info_ref[:, 5:6]
        x = x_ref[...] + (p1 * y_scr[slot, 0] + p2 * y_scr[slot, 1])
        o_ref[...] = _rms(x, g_ref[...])

    @pl.when(i == 0)
    def _():
        fetch(dest_ref, 0)

    for slot in range(2):
        @pl.when(i % 2 == slot)
        def _(slot=slot):
            @pl.when(i + 1 < nb)
            def _():
                fetch(nxt_ref, 1 - slot)

            finish(slot)


def _combine(x, ys, dest, info, gain, td=256):
    t, d = x.shape
    td = min(td, t)
    nb = t // td
    row = pl.BlockSpec((td, d), lambda i: (i, 0))
    tiled = _tile_dest(dest, t, td)
    return pl.pallas_call(
        functools.partial(_combine_kernel, td=td),
        out_shape=jax.ShapeDtypeStruct((t, d), F32),
        grid=(nb,),
        in_specs=[
            pl.BlockSpec((1, 1, 2 * td), lambda i: (i, 0, 0), memory_space=pltpu.SMEM),
            pl.BlockSpec((1, 1, 2 * td), lambda i: (jnp.minimum(i + 1, nb - 1), 0, 0),
                         memory_space=pltpu.SMEM),
            row,
            pl.BlockSpec((td, LANES), lambda i: (i, 0)),
            pl.BlockSpec((1, d), lambda i: (0, 0)),
            pl.BlockSpec(memory_space=pl.ANY),
        ],
        out_specs=row,
        scratch_shapes=[pltpu.VMEM((2, 2, td, d), F32), pltpu.SemaphoreType.DMA((2,))],
        compiler_params=_cparams(("arbitrary",)),
        name="combine",
    )(tiled, tiled, x, info, gain.reshape(1, d), ys)


def _moe(x, norm_g, rw_pad, w1, w3, w2, final_g, tm_e):
    t, d = x.shape
    n_tiles = (2 * t + N_EXPERTS * (tm_e - 1)) // tm_e
    n_rows = n_tiles * tm_e
    info, idx, cnt, xs0 = _router(x, norm_g, rw_pad, n_rows)
    e1, e2, r1, r2 = idx[0], idx[1], idx[2], idx[3]
    counts = cnt[0, :N_EXPERTS].astype(I32)
    padded = ((counts + tm_e - 1) // tm_e) * tm_e
    ends = jnp.cumsum(padded)
    starts = ends - padded
    dest = jnp.stack([starts[e1] + r1, starts[e2] + r2])
    tile_start = jnp.arange(n_tiles, dtype=I32) * tm_e
    tile_expert = jnp.minimum(
        jnp.sum((tile_start[:, None] >= ends[None, :]).astype(I32), axis=1), N_EXPERTS - 1)
    n_valid = (ends[-1] // tm_e).reshape(1).astype(I32)
    xs = _dispatch(x, norm_g, dest, xs0)
    ys = _ffn(xs, w1, w3, w2, tile_expert.astype(I32), n_valid, tm=tm_e, n_f=1)
    return _combine(x, ys, dest, info, final_g)


def _block_diag(w):
    n, c, d = w.shape
    eye = jnp.eye(n, dtype=w.dtype)
    return (w[:, :, None, :] * eye[:, None, :, None]).reshape(n * c, n * d)


def _cum_matrix():
    t = jnp.arange(CHUNK)
    return (t[None, :] <= t[:, None]).astype(BF16)


def _mixer_params(l, lb, hg_norm_g, conv_w, conv_b, w_ra, b_ra, w_ix, b_ix, lru_lambda, wdtype):
    row = lambda a: a.reshape(1, -1).astype(F32)
    return {
        "lbp": jnp.stack([jnp.log(lb), jnp.log1p(-lb), 1.0 - lb]).astype(F32),
        "hgg": row(hg_norm_g[l]),
        "cw": conv_w[l].astype(F32),
        "cb": row(conv_b[l]),
        "wra": _block_diag(w_ra[l].astype(F32)).astype(wdtype),
        "bra": row(b_ra[l]),
        "wix": _block_diag(w_ix[l].astype(F32)).astype(wdtype),
        "bix": row(b_ix[l]),
        "c8": row(-LRU_C * jax.nn.softplus(-lru_lambda[l].astype(F32))),
        "cm": _cum_matrix(),
    }


def _trunk(x, seq_shape, states, prm):
    bsz, seq = seq_shape
    one = jnp.ones((1,), I32)
    new_hg, new_lru, new_conv = [], [], []
    hg_stack = None if states is None else states[0]
    for l in range(2):
        mp = prm["mixer"][l]
        if states is None:
            x, s1, s2, s3 = _mixer_block(x.reshape(bsz, seq, D_MODEL), prm["norm1_g"][l],
                                         prm["w_in"], prm["w_out"], l, mp)
            x = x.reshape(bsz * seq, D_MODEL)
            new_hg.append(s1)
        else:
            z = _linear(x, prm["w_in"], l, gain=prm["norm1_g"][l])
            merged, hg_stack, s2, s3 = _mixer_step(z, states[0], l, states[1][l], states[2][l],
                                                   mp, hg_buf=hg_stack)
            x = _linear(merged, prm["w_out"], l, res=x)
        new_lru.append(s2)
        new_conv.append(s3)
        if l == 0:
            n_tiles = max(x.shape[0] // TILE_ROWS, 1)
            n_f = 1 if prm["ffn_w1"].dtype == BF16 else prm["ffn_w1"].shape[2] // F_CHUNK
            x = _ffn(x, prm["ffn_w1"], prm["ffn_w3"], prm["ffn_w2"], jnp.zeros((n_tiles,), I32),
                     one * n_tiles, gain=prm["norm2_g"][0], res=True, tm=TILE_ROWS, n_f=n_f,
                     ck=F_CHUNK)
        else:
            x = _moe(x, prm["norm2_g"][1], prm["rw_pad"], prm["moe_w1"], prm["moe_w3"],
                     prm["moe_w2"], prm["final_norm_g"],
                     tm_e=TILE_ROWS if states is None else LANES)
    hg_all = jnp.stack(new_hg) if states is None else hg_stack
    return x, hg_all, jnp.stack(new_lru), jnp.stack(new_conv)


def kernel(x_prompt, x_sample, state_hgrn, state_lru, state_conv, norm1_g, w_in, lower_bounds,
           hg_norm_g, conv_w, conv_b, w_ra, b_ra, w_ix, b_ix, lru_lambda, w_out, norm2_g,
           ffn_w1, ffn_w3, ffn_w2, router_w, moe_w1, moe_w3, moe_w2, final_norm_g):
    lb_all = jnp.cumsum(jax.nn.softmax(lower_bounds.astype(F32), axis=0), axis=0)
    lb_all = lb_all - lb_all[0]
    experts = {
        "moe_w1": moe_w1[0].astype(BF16), "moe_w3": moe_w3[0].astype(BF16),
        "moe_w2": moe_w2[0].astype(BF16),
        "rw_pad": jnp.pad(router_w[0].astype(F32), ((0, 0), (0, LANES - N_EXPERTS))),
        "norm1_g": norm1_g, "norm2_g": norm2_g, "final_norm_g": final_norm_g,
    }

    def params(wdtype):
        return dict(
            experts,
            mixer=[_mixer_params(l, lb_all[l], hg_norm_g, conv_w, conv_b, w_ra, b_ra, w_ix, b_ix,
                                 lru_lambda, wdtype) for l in range(2)],
            w_in=w_in.astype(wdtype), w_out=w_out.astype(wdtype),
            ffn_w1=ffn_w1.astype(wdtype), ffn_w3=ffn_w3.astype(wdtype),
            ffn_w2=ffn_w2.astype(wdtype))

    bp, sp, d = x_prompt.shape
    bs = x_sample.shape[0]
    y_p, hg_p, lru_p, conv_p = _trunk(x_prompt.reshape(bp * sp, d), (bp, sp), None, params(BF16))
    y_s, hg_s, lru_s, conv_s = _trunk(x_sample.reshape(bs, d), (bs, 1),
                                      (state_hgrn, state_lru, state_conv), params(F32))
    return (y_p.reshape(bp, sp, d), y_s.reshape(bs, 1, d), hg_p, lru_p, conv_p, hg_s, lru_s,
            conv_s)
```

```python
import functools

import jax
import jax.numpy as jnp
from jax import lax
from jax.experimental import pallas as pl
from jax.experimental.pallas import tpu as pltpu

F32 = jnp.float32
BF16 = jnp.bfloat16
I32 = jnp.int32

D_MODEL = 1024
HG_WIDTH = 512
HG_HEADS = 4
HG_D = 128
LRU_WIDTH = 512
LRU_BLOCKS = 8
LRU_C = 8.0
CONV_K = 4
IN_WIDTH = 4 * HG_WIDTH + 2 * LRU_WIDTH
N_EXPERTS = 8
EPS = 1e-6

CHUNK = 64
DIAG = 1
LANES = 128
SUBLANES = 8
TILE_ROWS = 512
F_CHUNK = 256
VMEM_LIMIT_BYTES = 56 * 1024 * 1024


def _cparams(sem):
    return pltpu.CompilerParams(dimension_semantics=sem, vmem_limit_bytes=VMEM_LIMIT_BYTES)


def _sigmoid(x):
    return 1.0 / (1.0 + jnp.exp(-x))


def _silu(x):
    return x * _sigmoid(x)


def _gelu_tanh(x):
    c = 0.7978845608028654
    return 0.5 * x * (1.0 + jnp.tanh(c * (x + 0.044715 * (x * x * x))))


def _rms(x, g):
    return x * lax.rsqrt(jnp.mean(x * x, axis=-1, keepdims=True) + EPS) * g


def _dot(a, b):
    if b.dtype == F32:
        return jnp.dot(a.astype(F32), b, preferred_element_type=F32,
                       precision=lax.Precision.HIGHEST)
    return jnp.dot(a.astype(BF16), b, preferred_element_type=F32)


def _dot_nt(a, b):
    return lax.dot_general(a, b, (((1,), (1,)), ((), ())), preferred_element_type=F32)


def _dot_tn(a, b):
    return lax.dot_general(a, b, (((0,), (0,)), ((), ())), preferred_element_type=F32)


def _split3(x):
    x1 = x.astype(BF16)
    r1 = x - x1.astype(F32)
    x2 = r1.astype(BF16)
    x3 = (r1 - x2.astype(F32)).astype(BF16)
    return x1, x2, x3


def _linear_kernel(*refs, has_gain, has_res):
    it = iter(refs)
    x_ref = next(it)
    g_ref = next(it) if has_gain else None
    w_ref = next(it)
    r_ref = next(it) if has_res else None
    o_ref = next(it)
    h_scr = next(it)

    @pl.when(pl.program_id(1) == 0)
    def _():
        x = x_ref[...].astype(F32)
        if has_gain:
            x = _rms(x, g_ref[...])
        h_scr[...] = x.astype(h_scr.dtype)

    acc = _dot(h_scr[...], w_ref[...])
    if has_res:
        acc = acc + r_ref[...]
    o_ref[...] = acc.astype(o_ref.dtype)


def _linear(x, w, layer, gain=None, res=None, tm=1024, tn=1024):
    m, k = x.shape
    n = w.shape[2]
    tm = min(tm, m)
    tn = min(tn, n)
    in_specs = [pl.BlockSpec((tm, k), lambda i, j: (i, 0))]
    args = [x]
    if gain is not None:
        in_specs.append(pl.BlockSpec((1, k), lambda i, j: (0, 0)))
        args.append(gain.reshape(1, k))
    in_specs.append(pl.BlockSpec((None, k, tn), lambda i, j: (layer, 0, j)))
    args.append(w)
    if res is not None:
        in_specs.append(pl.BlockSpec((tm, tn), lambda i, j: (i, j)))
        args.append(res)
    return pl.pallas_call(
        functools.partial(_linear_kernel, has_gain=gain is not None, has_res=res is not None),
        out_shape=jax.ShapeDtypeStruct((m, n), F32),
        grid=(m // tm, n // tn),
        in_specs=in_specs,
        out_specs=pl.BlockSpec((tm, tn), lambda i, j: (i, j)),
        scratch_shapes=[pltpu.VMEM((tm, k), w.dtype)],
        compiler_params=_cparams(("parallel", "arbitrary")),
        name="linear",
    )(*args)


def _ffn_kernel(te_ref, nv_ref, *refs, has_gain, has_res, tf, ck):
    del te_ref
    it = iter(refs)
    x_ref = next(it)
    g_ref = next(it) if has_gain else None
    w1_ref, w3_ref, w2_ref, o_ref, xb_scr, a_scr, acc_scr = it
    i = pl.program_id(0)
    j = pl.program_id(1)
    nj = pl.num_programs(1)
    valid = i < nv_ref[0]

    @pl.when(valid)
    def _():
        @pl.when(j == 0)
        def _():
            x = x_ref[...]
            if has_gain:
                x = _rms(x, g_ref[...])
            xb_scr[...] = x.astype(xb_scr.dtype)

        xb = xb_scr[...]
        for c in range(tf // ck):
            h1 = _dot(xb, w1_ref[0, :, c * ck:(c + 1) * ck])
            h3 = _dot(xb, w3_ref[0, :, c * ck:(c + 1) * ck])
            a_scr[:, c * ck:(c + 1) * ck] = (_silu(h1) * h3).astype(a_scr.dtype)
        part = _dot(a_scr[...], w2_ref[0])

        @pl.when(j == 0)
        def _():
            acc_scr[...] = part

        @pl.when(j > 0)
        def _():
            acc_scr[...] += part

        @pl.when(j == nj - 1)
        def _():
            out = acc_scr[...]
            if has_res:
                out = x_ref[...] + out
            o_ref[...] = out

    @pl.when(jnp.logical_and(jnp.logical_not(valid), j == nj - 1))
    def _():
        o_ref[...] = jnp.zeros_like(o_ref)


def _ffn(x, w1, w3, w2, tile_expert, n_valid, gain=None, res=False, tm=TILE_ROWS, n_f=2,
         ck=F_CHUNK):
    p, d = x.shape
    f = w1.shape[2]
    tm = min(tm, p)
    tf = f // n_f
    n_tiles = p // tm
    in_specs = [pl.BlockSpec((tm, d), lambda i, j, te, nv: (i, 0))]
    args = [x]
    if gain is not None:
        in_specs.append(pl.BlockSpec((1, d), lambda i, j, te, nv: (0, 0)))
        args.append(gain.reshape(1, d))
    in_specs += [
        pl.BlockSpec((1, d, tf), lambda i, j, te, nv: (te[i], 0, j)),
        pl.BlockSpec((1, d, tf), lambda i, j, te, nv: (te[i], 0, j)),
        pl.BlockSpec((1, tf, d), lambda i, j, te, nv: (te[i], j, 0)),
    ]
    args += [w1, w3, w2]
    grid_spec = pltpu.PrefetchScalarGridSpec(
        num_scalar_prefetch=2,
        grid=(n_tiles, n_f),
        in_specs=in_specs,
        out_specs=pl.BlockSpec((tm, d), lambda i, j, te, nv: (i, 0)),
        scratch_shapes=[pltpu.VMEM((tm, d), w1.dtype), pltpu.VMEM((tm, tf), w1.dtype),
                        pltpu.VMEM((tm, d), F32)],
    )
    return pl.pallas_call(
        functools.partial(_ffn_kernel, has_gain=gain is not None, has_res=res, tf=tf, ck=ck),
        out_shape=jax.ShapeDtypeStruct((p, d), F32),
        grid_spec=grid_spec,
        compiler_params=_cparams(("arbitrary", "arbitrary")),
        name="swiglu",
    )(tile_expert, n_valid, *args)


def _hgrn_gates(fz, loglb, l1mlb, oneml):
    e = jnp.exp(-jnp.abs(fz))
    ope = 1.0 + e
    log_sig = jnp.minimum(fz, 0.0) - jnp.log(ope)
    cc = l1mlb + log_sig
    log_f = jnp.maximum(loglb, cc) + jnp.log(1.0 + jnp.exp(-jnp.abs(loglb - cc)))
    k = oneml * (jnp.where(fz > 0.0, e, 1.0) / ope)
    return log_f, k


def _level_reference(b, w):
    n = b.shape[0]
    if w >= 4:
        return jnp.concatenate(
            [jnp.broadcast_to(b[j + w:j + w + 1, :], (2 * w, b.shape[1]))
             for j in range(0, n, 2 * w)], axis=0)
    b3 = b.reshape(n // SUBLANES, SUBLANES, b.shape[1])
    ahead = lambda s: pltpu.roll(b3, SUBLANES - s, 1)
    pos = lax.broadcasted_iota(I32, (1, SUBLANES, 1), 1) & (2 * w - 1)
    if w == 2:
        r3 = jnp.where(pos == 0, ahead(2),
                       jnp.where(pos == 1, ahead(1),
                                 jnp.where(pos == 2, b3, pltpu.roll(b3, 1, 1))))
    else:
        r3 = jnp.where(pos == 0, ahead(1), b3)
    return r3.reshape(b.shape)


def _lru_gates(xc, wra, bra, wix, bix, c8):
    r = _sigmoid(_dot(xc, wra) + bra)
    ig = _sigmoid(_dot(xc, wix) + bix)
    log_a = c8 * r
    a = jnp.exp(log_a)
    one_m_a2 = -jnp.tanh(log_a) * (a * a + 1.0)
    u = jnp.sqrt(one_m_a2) * (ig * xc)
    return a, u


def _mixer_block_kernel(x_ref, g1_ref, win_ref, lbp_ref, hgg_ref, cw_ref, cb_ref, wra_ref,
                        bra_ref, wix_ref, bix_ref, c8_ref, cm_ref, wout_ref,
                        xo_ref, hs_ref, ls_ref, cs_ref,
                        st_scr, h_scr, xpad_scr, hb_scr, q_scr, lf_scr, k_scr, v_scr, g_scr,
                        gl_scr, a_scr, u_scr, m_scr, *, tl):
    l = pl.program_id(1)
    nl = pl.num_programs(1)

    @pl.when(l == 0)
    def _():
        st_scr[...] = jnp.zeros_like(st_scr)
        h_scr[...] = jnp.zeros_like(h_scr)
        xpad_scr[0:8, :] = jnp.zeros((8, LRU_WIDTH), F32)

    hb_scr[...] = _rms(x_ref[0], g1_ref[...]).astype(BF16)
    hb = hb_scr[...]
    zg = lambda j: _dot(hb, win_ref[:, j * HG_WIDTH:(j + 1) * HG_WIDTH])
    q_scr[...] = _silu(zg(0))
    log_f, kk = _hgrn_gates(zg(1), lbp_ref[0:1, :], lbp_ref[1:2, :], lbp_ref[2:3, :])
    lf_scr[...] = log_f
    k_scr[...] = kk
    v_scr[...] = zg(2)
    g_scr[...] = _silu(zg(3))
    xpad_scr[8:8 + tl, :] = zg(4)
    gl_scr[...] = _gelu_tanh(zg(5))
    xc = cb_ref[...]
    for j in range(CONV_K):
        xc = xc + xpad_scr[5 + j:5 + j + tl, :] * cw_ref[j:j + 1, :]
    tail = xpad_scr[tl + 5:tl + 8, :]
    xpad_scr[5:8, :] = tail
    a, u = _lru_gates(xc, wra_ref[...], bra_ref[...], wix_ref[...], bix_ref[...], c8_ref[...])
    a_scr[...] = a
    u_scr[...] = u

    row = lax.broadcasted_iota(I32, (CHUNK, 1), 0)
    ti = lax.broadcasted_iota(I32, (CHUNK, CHUNK), 0)
    si = lax.broadcasted_iota(I32, (CHUNK, CHUNK), 1)
    t3 = lax.broadcasted_iota(I32, (1, DIAG, 1), 1)
    sub3 = lax.broadcasted_iota(I32, (1, SUBLANES, 1), 1)
    nblk = CHUNK // DIAG
    ngrp = CHUNK // SUBLANES
    levels = []
    w = CHUNK // 2
    while w >= DIAG:
        sh = (2 * w).bit_length() - 1
        pairs = ((ti >> sh) == (si >> sh)) & ((ti & (2 * w - 1)) >= w) & ((si & (2 * w - 1)) < w)
        levels.append((w, (row & (2 * w - 1)) >= w, pairs))
        w //= 2

    def chunk_body(c, carry):
        r0 = pl.multiple_of(c * CHUNK, CHUNK)
        rows = pl.ds(r0, CHUNK)

        for gi in range(LRU_WIDTH // LANES):
            cs = slice(gi * LANES, (gi + 1) * LANES)
            aa = a_scr[rows, cs].reshape(ngrp, SUBLANES, LANES)
            uu = u_scr[rows, cs].reshape(ngrp, SUBLANES, LANES)
            d = 1
            while d < SUBLANES:
                keep = sub3 >= d
                a_sh = pltpu.roll(aa, d, 1)
                u_sh = pltpu.roll(uu, d, 1)
                uu = jnp.where(keep, aa * u_sh + uu, uu)
                aa = jnp.where(keep, aa * a_sh, aa)
                d *= 2
            hprev = h_scr[:, cs]
            groups = []
            for j in range(ngrp):
                hj = uu[j] + aa[j] * hprev
                groups.append(hj)
                hprev = hj[SUBLANES - 1:SUBLANES, :]
            hs = jnp.concatenate(groups, axis=0)
            h_scr[:, cs] = hprev
            m_scr[rows, HG_WIDTH + gi * LANES:HG_WIDTH + (gi + 1) * LANES] = (
                hs * gl_scr[rows, cs]).astype(BF16)

        q = q_scr[rows, :]
        k = k_scr[rows, :]
        v = v_scr[rows, :]
        f1, f2, f3 = _split3(lf_scr[rows, :])
        cm = cm_ref[...]
        b = _dot(cm, f1) + _dot(cm, f2) + _dot(cm, f3)

        for hh in range(HG_HEADS):
            cs = slice(hh * HG_D, (hh + 1) * HG_D)
            qh, kh, vh, bh = q[:, cs], k[:, cs], v[:, cs], b[:, cs]
            vb = vh.astype(BF16)
            amat = jnp.zeros((CHUNK, CHUNK), F32)
            for w, upper, pairs in levels:
                dec = jnp.exp(-jnp.abs(bh - _level_reference(bh, w)))
                y = (jnp.where(upper, qh, kh) * dec).astype(BF16)
                amat = jnp.where(pairs, _dot_nt(y, y), amat)
            o = _dot(amat.astype(BF16), vb)
            st = st_scr[hh]
            o = o + _dot_nt((qh * jnp.exp(bh)).astype(BF16), st.astype(BF16))
            if DIAG == 1:
                o = o + jnp.sum(qh * kh, axis=-1, keepdims=True) * vh
            else:
                q3 = qh.reshape(nblk, DIAG, HG_D)
                k3 = kh.reshape(nblk, DIAG, HG_D)
                v3 = vh.reshape(nblk, DIAG, HG_D)
                b3 = bh.reshape(nblk, DIAG, HG_D)
                od = jnp.zeros((nblk, DIAG, HG_D), F32)
                for s in range(DIAG):
                    dec = jnp.exp(jnp.minimum(b3 - b3[:, s:s + 1, :], 0.0))
                    p = q3 * (k3[:, s:s + 1, :] * dec)
                    rs = jnp.sum(p, axis=-1, keepdims=True)
                    rs = jnp.where(t3 >= s, rs, 0.0)
                    od = od + rs * v3[:, s:s + 1, :]
                o = o + od.reshape(CHUNK, HG_D)
            bl = bh[CHUNK - 1:CHUNK, :]
            kdec = (kh * jnp.exp(bl - bh)).astype(BF16)
            st_scr[hh] = st * jnp.exp(bl) + _dot_tn(vb, kdec)
            o = o * lax.rsqrt(jnp.mean(o * o, axis=-1, keepdims=True) + EPS)
            o = o * hgg_ref[:, cs]
            o = o * g_scr[rows, cs]
            m_scr[rows, cs] = o.astype(BF16)
        return carry

    lax.fori_loop(0, tl // CHUNK, chunk_body, 0, unroll=8)
    xo_ref[0] = x_ref[0] + _dot(m_scr[...], wout_ref[...])

    @pl.when(l == nl - 1)
    def _():
        for hh in range(HG_HEADS):
            hs_ref[0, hh] = st_scr[hh].T
        ls_ref[0] = h_scr[...]
        cs_ref[0] = xpad_scr[5:8, :]


def _mixer_block(x, gain, w_in, w_out, layer, mp, tl=512):
    bsz, seq, d = x.shape
    tl = min(tl, seq)
    full = lambda shape: pl.BlockSpec(shape, lambda b, l: (0,) * len(shape))
    stacked = lambda shape: pl.BlockSpec((None,) + shape, lambda b, l: (layer, 0, 0))
    out_shapes = (
        jax.ShapeDtypeStruct((bsz, seq, d), F32),
        jax.ShapeDtypeStruct((bsz, HG_HEADS, HG_D, HG_D), F32),
        jax.ShapeDtypeStruct((bsz, 1, LRU_WIDTH), F32),
        jax.ShapeDtypeStruct((bsz, CONV_K - 1, LRU_WIDTH), F32),
    )
    half = lambda dt: pltpu.VMEM((tl, HG_WIDTH), dt)
    xo, hg, lru, conv = pl.pallas_call(
        functools.partial(_mixer_block_kernel, tl=tl),
        out_shape=out_shapes,
        grid=(bsz, seq // tl),
        in_specs=[
            pl.BlockSpec((1, tl, d), lambda b, l: (b, l, 0)),
            full((1, d)), stacked((d, IN_WIDTH)),
            full((3, HG_WIDTH)), full((1, HG_WIDTH)), full((CONV_K, LRU_WIDTH)),
            full((1, LRU_WIDTH)), full((LRU_WIDTH, LRU_WIDTH)), full((1, LRU_WIDTH)),
            full((LRU_WIDTH, LRU_WIDTH)), full((1, LRU_WIDTH)), full((1, LRU_WIDTH)),
            full(mp["cm"].shape), stacked((d, d)),
        ],
        out_specs=(
            pl.BlockSpec((1, tl, d), lambda b, l: (b, l, 0)),
            pl.BlockSpec((1, HG_HEADS, HG_D, HG_D), lambda b, l: (b, 0, 0, 0)),
            pl.BlockSpec((1, 1, LRU_WIDTH), lambda b, l: (b, 0, 0)),
            pl.BlockSpec((1, CONV_K - 1, LRU_WIDTH), lambda b, l: (b, 0, 0)),
        ),
        scratch_shapes=[
            pltpu.VMEM((HG_HEADS, HG_D, HG_D), F32),
            pltpu.VMEM((1, LRU_WIDTH), F32),
            pltpu.VMEM((tl + 8, LRU_WIDTH), F32),
            pltpu.VMEM((tl, d), BF16),
            half(F32), half(F32), half(F32), half(F32), half(F32), half(F32), half(F32),
            half(F32),
            pltpu.VMEM((tl, d), BF16),
        ],
        compiler_params=_cparams(("parallel", "arbitrary")),
        name="mixer_block",
    )(x, gain.reshape(1, d), w_in, mp["lbp"], mp["hgg"], mp["cw"], mp["cb"], mp["wra"],
      mp["bra"], mp["wix"], mp["bix"], mp["c8"], mp["cm"], w_out)
    return xo, hg, lru.reshape(bsz, LRU_WIDTH), conv


def _mixer_step_kernel(z_ref, sh_ref, sl_ref, sc_ref, lbp_ref, hgg_ref, cw_ref, cb_ref, wra_ref,
                       bra_ref, wix_ref, bix_ref, c8_ref, *rest, bb, aliased):
    m_ref, hn_ref, ln_ref, cn_ref, o_scr = rest[1:] if aliased else rest
    z = z_ref[...]
    xr = z[:, 4 * HG_WIDTH:4 * HG_WIDTH + LRU_WIDTH]
    gr = z[:, 4 * HG_WIDTH + LRU_WIDTH:IN_WIDTH]
    buf = sc_ref[...]
    xc = cb_ref[...]
    for j in range(CONV_K - 1):
        xc = xc + buf[:, j * LRU_WIDTH:(j + 1) * LRU_WIDTH] * cw_ref[j:j + 1, :]
    xc = xc + xr * cw_ref[CONV_K - 1:CONV_K, :]
    cn_ref[:, 0:(CONV_K - 2) * LRU_WIDTH] = buf[:, LRU_WIDTH:(CONV_K - 1) * LRU_WIDTH]
    cn_ref[:, (CONV_K - 2) * LRU_WIDTH:(CONV_K - 1) * LRU_WIDTH] = xr
    a, u = _lru_gates(xc, wra_ref[...], bra_ref[...], wix_ref[...], bix_ref[...], c8_ref[...])
    hnew = u + a * sl_ref[...]
    ln_ref[...] = hnew
    m_ref[:, HG_WIDTH:D_MODEL] = (hnew * _gelu_tanh(gr)).astype(m_ref.dtype)

    q = _silu(z[:, 0:HG_WIDTH])
    log_f, k = _hgrn_gates(z[:, HG_WIDTH:2 * HG_WIDTH], lbp_ref[0:1, :], lbp_ref[1:2, :],
                           lbp_ref[2:3, :])
    f = jnp.exp(log_f)
    v = z[:, 2 * HG_WIDTH:3 * HG_WIDTH]
    gate = _silu(z[:, 3 * HG_WIDTH:4 * HG_WIDTH])
    zpad = jnp.zeros((HG_D - bb, HG_D), F32)
    for hh in range(HG_HEADS):
        cs = slice(hh * HG_D, (hh + 1) * HG_D)
        ft = jnp.concatenate([f[:, cs], zpad], axis=0).T
        kt = jnp.concatenate([k[:, cs], zpad], axis=0).T
        qt = jnp.concatenate([q[:, cs], zpad], axis=0).T
        for j in range(bb):
            fcol = jnp.broadcast_to(ft[:, j:j + 1], (HG_D, HG_D))
            kcol = jnp.broadcast_to(kt[:, j:j + 1], (HG_D, HG_D))
            qcol = jnp.broadcast_to(qt[:, j:j + 1], (HG_D, HG_D))
            vrow = jnp.broadcast_to(v[j:j + 1, cs], (HG_D, HG_D))
            sn = fcol * sh_ref[j, hh] + kcol * vrow
            hn_ref[j, hh] = sn
            o_scr[j:j + 1, cs] = jnp.sum(qcol * sn, axis=0, keepdims=True)
    for hh in range(HG_HEADS):
        cs = slice(hh * HG_D, (hh + 1) * HG_D)
        o = o_scr[:, cs]
        o = o * lax.rsqrt(jnp.mean(o * o, axis=-1, keepdims=True) + EPS)
        o = o * hgg_ref[:, cs]
        o = o * gate[:, cs]
        m_ref[:, cs] = o.astype(m_ref.dtype)


def _mixer_step(z, s_hg_all, layer, s_lru, s_conv, mp, hg_buf=None, bb=16):
    bsz = z.shape[0]
    bb = min(bb, bsz)
    cw3 = (CONV_K - 1) * LRU_WIDTH
    full = lambda shape: pl.BlockSpec(shape, lambda i: (0,) * len(shape))
    state_spec = pl.BlockSpec((None, bb, HG_HEADS, HG_D, HG_D), lambda i: (layer, i, 0, 0, 0))
    out_shapes = (
        jax.ShapeDtypeStruct((bsz, D_MODEL), mp["wra"].dtype),
        jax.ShapeDtypeStruct(s_hg_all.shape, F32),
        jax.ShapeDtypeStruct((bsz, LRU_WIDTH), F32),
        jax.ShapeDtypeStruct((bsz, cw3), F32),
    )
    in_specs = [
        pl.BlockSpec((bb, IN_WIDTH), lambda i: (i, 0)),
        state_spec,
        pl.BlockSpec((bb, LRU_WIDTH), lambda i: (i, 0)),
        pl.BlockSpec((bb, cw3), lambda i: (i, 0)),
        full((3, HG_WIDTH)), full((1, HG_WIDTH)), full((CONV_K, LRU_WIDTH)),
        full((1, LRU_WIDTH)), full((LRU_WIDTH, LRU_WIDTH)), full((1, LRU_WIDTH)),
        full((LRU_WIDTH, LRU_WIDTH)), full((1, LRU_WIDTH)), full((1, LRU_WIDTH)),
    ]
    args = [z, s_hg_all, s_lru, s_conv.reshape(bsz, cw3), mp["lbp"], mp["hgg"], mp["cw"],
            mp["cb"], mp["wra"], mp["bra"], mp["wix"], mp["bix"], mp["c8"]]
    aliases = {}
    if hg_buf is not None:
        aliases = {len(args): 1}
        in_specs.append(pl.BlockSpec(memory_space=pl.ANY))
        args.append(hg_buf)
    merged, hg, lru, conv = pl.pallas_call(
        functools.partial(_mixer_step_kernel, bb=bb, aliased=hg_buf is not None),
        out_shape=out_shapes,
        grid=(bsz // bb,),
        in_specs=in_specs,
        out_specs=(
            pl.BlockSpec((bb, D_MODEL), lambda i: (i, 0)),
            state_spec,
            pl.BlockSpec((bb, LRU_WIDTH), lambda i: (i, 0)),
            pl.BlockSpec((bb, cw3), lambda i: (i, 0)),
        ),
        scratch_shapes=[pltpu.VMEM((bb, HG_WIDTH), F32)],
        input_output_aliases=aliases,
        compiler_params=_cparams(("parallel",)),
        name="mixer_step",
    )(*args)
    return merged, hg, lru, conv.reshape(bsz, CONV_K - 1, LRU_WIDTH)


def _router_kernel(x_ref, g_ref, rw_ref, tril_ref, info_ref, idx_ref, cnt_ref, zero_ref,
                   carry_scr):
    i = pl.program_id(0)

    @pl.when(i == 0)
    def _():
        carry_scr[...] = jnp.zeros_like(carry_scr)

    zero_ref[...] = jnp.zeros_like(zero_ref)

    h = _rms(x_ref[...], g_ref[...])
    logits = jnp.dot(h, rw_ref[...], preferred_element_type=F32, precision=lax.Precision.HIGHEST)
    lane = lax.broadcasted_iota(I32, logits.shape, 1)
    neg = jnp.float32(-jnp.inf)
    logits = jnp.where(lane < N_EXPERTS, logits, neg)
    m1 = jnp.max(logits, axis=-1, keepdims=True)
    i1 = jnp.min(jnp.where(logits == m1, lane, LANES), axis=-1, keepdims=True)
    l2 = jnp.where(lane == i1, neg, logits)
    m2 = jnp.max(l2, axis=-1, keepdims=True)
    i2 = jnp.min(jnp.where(l2 == m2, lane, LANES), axis=-1, keepdims=True)
    e = jnp.exp(m2 - m1)
    p1 = 1.0 / (1.0 + e)
    p2 = e / (1.0 + e)
    oh1 = (lane == i1).astype(F32)
    oh2 = (lane == i2).astype(F32)
    sel = oh1 + oh2
    before = _dot(tril_ref[...], sel.astype(BF16)) + carry_scr[...]
    r1 = jnp.sum(oh1 * before, axis=-1, keepdims=True)
    r2 = jnp.sum(oh2 * before, axis=-1, keepdims=True)
    carry_scr[...] += jnp.sum(sel, axis=0, keepdims=True)
    info = jnp.where(lane == 0, i1.astype(F32), 0.0)
    info = jnp.where(lane == 1, i2.astype(F32), info)
    info = jnp.where(lane == 2, r1, info)
    info = jnp.where(lane == 3, r2, info)
    info = jnp.where(lane == 4, p1, info)
    info = jnp.where(lane == 5, p2, info)
    info_ref[...] = info
    idx_ref[...] = info.T[0:SUBLANES, :].astype(I32)
    cnt_ref[...] = carry_scr[...]


def _router(x, gain, rw_pad, n_rows, tm=512):
    t, d = x.shape
    tm = min(tm, t)
    zrows = n_rows // (t // tm)
    assert zrows * (t // tm) == n_rows and zrows % SUBLANES == 0
    tril = jnp.tril(jnp.ones((tm, tm), BF16), -1)
    return pl.pallas_call(
        _router_kernel,
        out_shape=(jax.ShapeDtypeStruct((t, LANES), F32),
                   jax.ShapeDtypeStruct((SUBLANES, t), I32),
                   jax.ShapeDtypeStruct((1, LANES), F32),
                   jax.ShapeDtypeStruct((n_rows, d), F32)),
        grid=(t // tm,),
        in_specs=[
            pl.BlockSpec((tm, d), lambda i: (i, 0)),
            pl.BlockSpec((1, d), lambda i: (0, 0)),
            pl.BlockSpec((d, LANES), lambda i: (0, 0)),
            pl.BlockSpec((tm, tm), lambda i: (0, 0)),
        ],
        out_specs=(pl.BlockSpec((tm, LANES), lambda i: (i, 0)),
                   pl.BlockSpec((SUBLANES, tm), lambda i: (0, i)),
                   pl.BlockSpec((1, LANES), lambda i: (0, 0)),
                   pl.BlockSpec((zrows, d), lambda i: (i, 0))),
        scratch_shapes=[pltpu.VMEM((1, LANES), F32)],
        compiler_params=_cparams(("arbitrary",)),
        name="router",
    )(x, gain.reshape(1, d), rw_pad, tril)


def _row_copy(src, dst, s, d, sem):
    return pltpu.make_async_copy(src.at[pl.ds(s, 1)], dst.at[pl.ds(d, 1)], sem)


def _tile_dest(dest, t, td):
    nb = t // td
    dest3 = jnp.concatenate([dest[0].reshape(nb, td), dest[1].reshape(nb, td)], axis=1)
    return dest3.reshape(nb, 1, 2 * td)


def _dispatch_kernel(dest_ref, x_ref, g_ref, xs_in_ref, xs_ref, pk_scr, sem, *, td):
    del xs_in_ref
    pk_scr[...] = _rms(x_ref[...], g_ref[...])

    def issue(r, c):
        _row_copy(pk_scr, xs_ref, r, dest_ref[0, 0, r], sem).start(priority=0)
        _row_copy(pk_scr, xs_ref, r, dest_ref[0, 0, td + r], sem).start(priority=1)
        return c

    lax.fori_loop(0, td, issue, 0, unroll=8)
    for _ in range(2):
        pltpu.make_async_copy(pk_scr, xs_ref.at[pl.ds(0, td)], sem).wait()


def _dispatch(x, gain, dest, xs0, td=512):
    t, d = x.shape
    td = min(td, t)
    n_rows = xs0.shape[0]
    return pl.pallas_call(
        functools.partial(_dispatch_kernel, td=td),
        out_shape=jax.ShapeDtypeStruct((n_rows, d), F32),
        grid=(t // td,),
        in_specs=[
            pl.BlockSpec((1, 1, 2 * td), lambda i: (i, 0, 0), memory_space=pltpu.SMEM),
            pl.BlockSpec((td, d), lambda i: (i, 0)),
            pl.BlockSpec((1, d), lambda i: (0, 0)),
            pl.BlockSpec(memory_space=pl.ANY),
        ],
        out_specs=pl.BlockSpec(memory_space=pl.ANY),
        scratch_shapes=[pltpu.VMEM((td, d), F32), pltpu.SemaphoreType.DMA(())],
        input_output_aliases={3: 0},
        compiler_params=pltpu.CompilerParams(dimension_semantics=("arbitrary",),
                                             vmem_limit_bytes=VMEM_LIMIT_BYTES,
                                             has_side_effects=True),
        name="dispatch",
    )(_tile_dest(dest, t, td), x, gain.reshape(1, d), xs0)


def _combine_kernel(dest_ref, nxt_ref, x_ref, info_ref, g_ref, ys_ref, o_ref, y_scr, sems, *, td):
    i = pl.program_id(0)
    nb = pl.num_programs(0)

    def fetch(idx_ref, slot):
        def issue(r, c):
            _row_copy(ys_ref, y_scr.at[slot, 0], idx_ref[0, 0, r], r,
                      sems.at[slot]).start(priority=0)
            _row_copy(ys_ref, y_scr.at[slot, 1], idx_ref[0, 0, td + r], r,
                      sems.at[slot]).start(priority=1)
            return c

        lax.fori_loop(0, td, issue, 0, unroll=8)

    def finish(slot):
        for j in range(2):
            pltpu.make_async_copy(ys_ref.at[pl.ds(0, td)], y_scr.at[slot, j],
                                  sems.at[slot]).wait()
        p1 = info_ref[:, 4:5]
        p2 = info_ref[:, 5:6]
        x = x_ref[...] + (p1 * y_scr[slot, 0] + p2 * y_scr[slot, 1])
        o_ref[...] = _rms(x, g_ref[...])

    @pl.when(i == 0)
    def _():
        fetch(dest_ref, 0)

    for slot in range(2):
        @pl.when(i % 2 == slot)
        def _(slot=slot):
            @pl.when(i + 1 < nb)
            def _():
                fetch(nxt_ref, 1 - slot)

            finish(slot)


def _combine(x, ys, dest, info, gain, td=256):
    t, d = x.shape
    td = min(td, t)
    nb = t // td
    row = pl.BlockSpec((td, d), lambda i: (i, 0))
    tiled = _tile_dest(dest, t, td)
    return pl.pallas_call(
        functools.partial(_combine_kernel, td=td),
        out_shape=jax.ShapeDtypeStruct((t, d), F32),
        grid=(nb,),
        in_specs=[
            pl.BlockSpec((1, 1, 2 * td), lambda i: (i, 0, 0), memory_space=pltpu.SMEM),
            pl.BlockSpec((1, 1, 2 * td), lambda i: (jnp.minimum(i + 1, nb - 1), 0, 0),
                         memory_space=pltpu.SMEM),
            row,
            pl.BlockSpec((td, LANES), lambda i: (i, 0)),
            pl.BlockSpec((1, d), lambda i: (0, 0)),
            pl.BlockSpec(memory_space=pl.ANY),
        ],
        out_specs=row,
        scratch_shapes=[pltpu.VMEM((2, 2, td, d), F32), pltpu.SemaphoreType.DMA((2,))],
        compiler_params=_cparams(("arbitrary",)),
        name="combine",
    )(tiled, tiled, x, info, gain.reshape(1, d), ys)


def _moe(x, norm_g, rw_pad, w1, w3, w2, final_g, tm_e):
    t, d = x.shape
    n_tiles = (2 * t + N_EXPERTS * (tm_e - 1)) // tm_e
    n_rows = n_tiles * tm_e
    info, idx, cnt, xs0 = _router(x, norm_g, rw_pad, n_rows)
    e1, e2, r1, r2 = idx[0], idx[1], idx[2], idx[3]
    counts = cnt[0, :N_EXPERTS].astype(I32)
    padded = ((counts + tm_e - 1) // tm_e) * tm_e
    ends = jnp.cumsum(padded)
    starts = ends - padded
    dest = jnp.stack([starts[e1] + r1, starts[e2] + r2])
    tile_start = jnp.arange(n_tiles, dtype=I32) * tm_e
    tile_expert = jnp.minimum(
        jnp.sum((tile_start[:, None] >= ends[None, :]).astype(I32), axis=1), N_EXPERTS - 1)
    n_valid = (ends[-1] // tm_e).reshape(1).astype(I32)
    xs = _dispatch(x, norm_g, dest, xs0)
    ys = _ffn(xs, w1, w3, w2, tile_expert.astype(I32), n_valid, tm=tm_e, n_f=1)
    return _combine(x, ys, dest, info, final_g)


def _block_diag(w):
    n, c, d = w.shape
    eye = jnp.eye(n, dtype=w.dtype)
    return (w[:, :, None, :] * eye[:, None, :, None]).reshape(n * c, n * d)


def _cum_matrix():
    t = jnp.arange(CHUNK)
    return (t[None, :] <= t[:, None]).astype(BF16)


def _mixer_params(l, lb, hg_norm_g, conv_w, conv_b, w_ra, b_ra, w_ix, b_ix, lru_lambda, wdtype):
    row = lambda a: a.reshape(1, -1).astype(F32)
    return {
        "lbp": jnp.stack([jnp.log(lb), jnp.log1p(-lb), 1.0 - lb]).astype(F32),
        "hgg": row(hg_norm_g[l]),
        "cw": conv_w[l].astype(F32),
        "cb": row(conv_b[l]),
        "wra": _block_diag(w_ra[l].astype(F32)).astype(wdtype),
        "bra": row(b_ra[l]),
        "wix": _block_diag(w_ix[l].astype(F32)).astype(wdtype),
        "bix": row(b_ix[l]),
        "c8": row(-LRU_C * jax.nn.softplus(-lru_lambda[l].astype(F32))),
        "cm": _cum_matrix(),
    }


def _trunk(x, seq_shape, states, prm):
    bsz, seq = seq_shape
    one = jnp.ones((1,), I32)
    new_hg, new_lru, new_conv = [], [], []
    hg_stack = None if states is None else states[0]
    for l in range(2):
        mp = prm["mixer"][l]
        if states is None:
            x, s1, s2, s3 = _mixer_block(x.reshape(bsz, seq, D_MODEL), prm["norm1_g"][l],
                                         prm["w_in"], prm["w_out"], l, mp)
            x = x.reshape(bsz * seq, D_MODEL)
            new_hg.append(s1)
        else:
            z = _linear(x, prm["w_in"], l, gain=prm["norm1_g"][l])
            merged, hg_stack, s2, s3 = _mixer_step(z, states[0], l, states[1][l], states[2][l],
                                                   mp, hg_buf=hg_stack)
            x = _linear(merged, prm["w_out"], l, res=x)
        new_lru.append(s2)
        new_conv.append(s3)
        if l == 0:
            n_tiles = max(x.shape[0] // TILE_ROWS, 1)
            n_f = 1 if prm["ffn_w1"].dtype == BF16 else prm["ffn_w1"].shape[2] // F_CHUNK
            x = _ffn(x, prm["ffn_w1"], prm["ffn_w3"], prm["ffn_w2"], jnp.zeros((n_tiles,), I32),
                     one * n_tiles, gain=prm["norm2_g"][0], res=True, tm=TILE_ROWS, n_f=n_f,
                     ck=F_CHUNK)
        else:
            x = _moe(x, prm["norm2_g"][1], prm["rw_pad"], prm["moe_w1"], prm["moe_w3"],
                     prm["moe_w2"], prm["final_norm_g"],
                     tm_e=TILE_ROWS if states is None else LANES)
    hg_all = jnp.stack(new_hg) if states is None else hg_stack
    return x, hg_all, jnp.stack(new_lru), jnp.stack(new_conv)


def kernel(x_prompt, x_sample, state_hgrn, state_lru, state_conv, norm1_g, w_in, lower_bounds,
           hg_norm_g, conv_w, conv_b, w_ra, b_ra, w_ix, b_ix, lru_lambda, w_out, norm2_g,
           ffn_w1, ffn_w3, ffn_w2, router_w, moe_w1, moe_w3, moe_w2, final_norm_g):
    lb_all = jnp.cumsum(jax.nn.softmax(lower_bounds.astype(F32), axis=0), axis=0)
    lb_all = lb_all - lb_all[0]
    experts = {
        "moe_w1": moe_w1[0].astype(BF16), "moe_w3": moe_w3[0].astype(BF16),
        "moe_w2": moe_w2[0].astype(BF16),
        "rw_pad": jnp.pad(router_w[0].astype(F32), ((0, 0), (0, LANES - N_EXPERTS))),
        "norm1_g": norm1_g, "norm2_g": norm2_g, "final_norm_g": final_norm_g,
    }

    def params(wdtype):
        return dict(
            experts,
            mixer=[_mixer_params(l, lb_all[l], hg_norm_g, conv_w, conv_b, w_ra, b_ra, w_ix, b_ix,
                                 lru_lambda, wdtype) for l in range(2)],
            w_in=w_in.astype(wdtype), w_out=w_out.astype(wdtype),
            ffn_w1=ffn_w1.astype(wdtype), ffn_w3=ffn_w3.astype(wdtype),
            ffn_w2=ffn_w2.astype(wdtype))

    bp, sp, d = x_prompt.shape
    bs = x_sample.shape[0]
    y_p, hg_p, lru_p, conv_p = _trunk(x_prompt.reshape(bp * sp, d), (bp, sp), None, params(BF16))
    y_s, hg_s, lru_s, conv_s = _trunk(x_sample.reshape(bs, d), (bs, 1),
                                      (state_hgrn, state_lru, state_conv), params(F32))
    return (y_p.reshape(bp, sp, d), y_s.reshape(bs, 1, d), hg_p, lru_p, conv_p, hg_s, lru_s,
            conv_s)
```

```python
import functools

import jax
import jax.numpy as jnp
from jax import lax
from jax.experimental import pallas as pl
from jax.experimental.pallas import tpu as pltpu

F32 = jnp.float32
BF16 = jnp.bfloat16
I32 = jnp.int32

D_MODEL = 1024
HG_WIDTH = 512
HG_HEADS = 4
HG_D = 128
LRU_WIDTH = 512
LRU_BLOCKS = 8
LRU_C = 8.0
CONV_K = 4
IN_WIDTH = 4 * HG_WIDTH + 2 * LRU_WIDTH
N_EXPERTS = 8
EPS = 1e-6

CHUNK = 64
DIAG = 1
LANES = 128
SUBLANES = 8
TILE_ROWS = 512
F_CHUNK = 256
VMEM_LIMIT_BYTES = 56 * 1024 * 1024


def _cparams(sem):
    return pltpu.CompilerParams(dimension_semantics=sem, vmem_limit_bytes=VMEM_LIMIT_BYTES)


def _exp_neg(x):
    return jnp.exp2(x * -1.4426950408889634)


def _sigmoid(x):
    return 1.0 / (1.0 + _exp_neg(x))


def _silu(x):
    return x * _sigmoid(x)


def _gelu_tanh(x):
    c = 0.7978845608028654
    return 0.5 * x * (1.0 + jnp.tanh(c * (x + 0.044715 * (x * x * x))))


def _rms(x, g):
    return x * lax.rsqrt(jnp.mean(x * x, axis=-1, keepdims=True) + EPS) * g


def _dot(a, b):
    if b.dtype == F32:
        return jnp.dot(a.astype(F32), b, preferred_element_type=F32,
                       precision=lax.Precision.HIGHEST)
    return jnp.dot(a.astype(BF16), b, preferred_element_type=F32)


def _dot_nt(a, b):
    return lax.dot_general(a, b, (((1,), (1,)), ((), ())), preferred_element_type=F32)


def _dot_tn(a, b):
    return lax.dot_general(a, b, (((0,), (0,)), ((), ())), preferred_element_type=F32)


def _split3(x):
    x1 = x.astype(BF16)
    r1 = x - x1.astype(F32)
    x2 = r1.astype(BF16)
    x3 = (r1 - x2.astype(F32)).astype(BF16)
    return x1, x2, x3


def _linear_kernel(*refs, has_gain, has_res):
    it = iter(refs)
    x_ref = next(it)
    g_ref = next(it) if has_gain else None
    w_ref = next(it)
    r_ref = next(it) if has_res else None
    o_ref = next(it)
    h_scr = next(it)

    @pl.when(pl.program_id(1) == 0)
    def _():
        x = x_ref[...].astype(F32)
        if has_gain:
            x = _rms(x, g_ref[...])
        h_scr[...] = x.astype(h_scr.dtype)

    acc = _dot(h_scr[...], w_ref[...])
    if has_res:
        acc = acc + r_ref[...]
    o_ref[...] = acc.astype(o_ref.dtype)


def _linear(x, w, layer, gain=None, res=None, tm=1024, tn=1024):
    m, k = x.shape
    n = w.shape[2]
    tm = min(tm, m)
    tn = min(tn, n)
    in_specs = [pl.BlockSpec((tm, k), lambda i, j: (i, 0))]
    args = [x]
    if gain is not None:
        in_specs.append(pl.BlockSpec((1, k), lambda i, j: (0, 0)))
        args.append(gain.reshape(1, k))
    in_specs.append(pl.BlockSpec((None, k, tn), lambda i, j: (layer, 0, j)))
    args.append(w)
    if res is not None:
        in_specs.append(pl.BlockSpec((tm, tn), lambda i, j: (i, j)))
        args.append(res)
    return pl.pallas_call(
        functools.partial(_linear_kernel, has_gain=gain is not None, has_res=res is not None),
        out_shape=jax.ShapeDtypeStruct((m, n), F32),
        grid=(m // tm, n // tn),
        in_specs=in_specs,
        out_specs=pl.BlockSpec((tm, tn), lambda i, j: (i, j)),
        scratch_shapes=[pltpu.VMEM((tm, k), w.dtype)],
        compiler_params=_cparams(("parallel", "arbitrary")),
        name="linear",
    )(*args)


def _ffn_kernel(te_ref, nv_ref, *refs, has_gain, has_res, tf, ck):
    del te_ref
    it = iter(refs)
    x_ref = next(it)
    g_ref = next(it) if has_gain else None
    w1_ref, w3_ref, w2_ref, o_ref, xb_scr, a_scr, acc_scr = it
    i = pl.program_id(0)
    j = pl.program_id(1)
    nj = pl.num_programs(1)
    valid = i < nv_ref[0]

    @pl.when(valid)
    def _():
        @pl.when(j == 0)
        def _():
            x = x_ref[...]
            if has_gain:
                x = _rms(x, g_ref[...])
            xb_scr[...] = x.astype(xb_scr.dtype)

        xb = xb_scr[...]
        for c in range(tf // ck):
            h1 = _dot(xb, w1_ref[0, :, c * ck:(c + 1) * ck])
            h3 = _dot(xb, w3_ref[0, :, c * ck:(c + 1) * ck])
            a_scr[:, c * ck:(c + 1) * ck] = (_silu(h1) * h3).astype(a_scr.dtype)
        part = _dot(a_scr[...], w2_ref[0])

        @pl.when(j == 0)
        def _():
            acc_scr[...] = part

        @pl.when(j > 0)
        def _():
            acc_scr[...] += part

        @pl.when(j == nj - 1)
        def _():
            out = acc_scr[...]
            if has_res:
                out = x_ref[...] + out
            o_ref[...] = out

    @pl.when(jnp.logical_and(jnp.logical_not(valid), j == nj - 1))
    def _():
        o_ref[...] = jnp.zeros_like(o_ref)


def _ffn(x, w1, w3, w2, tile_expert, n_valid, gain=None, res=False, tm=TILE_ROWS, n_f=2,
         ck=F_CHUNK):
    p, d = x.shape
    f = w1.shape[2]
    tm = min(tm, p)
    tf = f // n_f
    n_tiles = p // tm
    in_specs = [pl.BlockSpec((tm, d), lambda i, j, te, nv: (i, 0))]
    args = [x]
    if gain is not None:
        in_specs.append(pl.BlockSpec((1, d), lambda i, j, te, nv: (0, 0)))
        args.append(gain.reshape(1, d))
    in_specs += [
        pl.BlockSpec((1, d, tf), lambda i, j, te, nv: (te[i], 0, j)),
        pl.BlockSpec((1, d, tf), lambda i, j, te, nv: (te[i], 0, j)),
        pl.BlockSpec((1, tf, d), lambda i, j, te, nv: (te[i], j, 0)),
    ]
    args += [w1, w3, w2]
    grid_spec = pltpu.PrefetchScalarGridSpec(
        num_scalar_prefetch=2,
        grid=(n_tiles, n_f),
        in_specs=in_specs,
        out_specs=pl.BlockSpec((tm, d), lambda i, j, te, nv: (i, 0)),
        scratch_shapes=[pltpu.VMEM((tm, d), w1.dtype), pltpu.VMEM((tm, tf), w1.dtype),
                        pltpu.VMEM((tm, d), F32)],
    )
    return pl.pallas_call(
        functools.partial(_ffn_kernel, has_gain=gain is not None, has_res=res, tf=tf, ck=ck),
        out_shape=jax.ShapeDtypeStruct((p, d), F32),
        grid_spec=grid_spec,
        compiler_params=_cparams(("arbitrary", "arbitrary")),
        name="swiglu",
    )(tile_expert, n_valid, *args)


def _hgrn_gates(fz, loglb, l1mlb, oneml):
    e = _exp_neg(jnp.abs(fz))
    ope = 1.0 + e
    log_sig = jnp.minimum(fz, 0.0) - jnp.log(ope)
    cc = l1mlb + log_sig
    log_f = jnp.maximum(loglb, cc) + jnp.log(1.0 + _exp_neg(jnp.abs(loglb - cc)))
    k = oneml * (jnp.where(fz > 0.0, e, 1.0) / ope)
    return log_f, k


def _level_reference(b, w):
    n = b.shape[0]
    if w >= 4:
        return jnp.concatenate(
            [jnp.broadcast_to(b[j + w:j + w + 1, :], (2 * w, b.shape[1]))
             for j in range(0, n, 2 * w)], axis=0)
    b3 = b.reshape(n // SUBLANES, SUBLANES, b.shape[1])
    ahead = lambda s: pltpu.roll(b3, SUBLANES - s, 1)
    pos = lax.broadcasted_iota(I32, (1, SUBLANES, 1), 1) & (2 * w - 1)
    if w == 2:
        r3 = jnp.where(pos == 0, ahead(2),
                       jnp.where(pos == 1, ahead(1),
                                 jnp.where(pos == 2, b3, pltpu.roll(b3, 1, 1))))
    else:
        r3 = jnp.where(pos == 0, ahead(1), b3)
    return r3.reshape(b.shape)


def _lru_gates(xc, wra, bra, wix, bix, c8):
    r = _sigmoid(_dot(xc, wra) + bra)
    ig = _sigmoid(_dot(xc, wix) + bix)
    log_a = c8 * r
    a = jnp.exp(log_a)
    one_m_a2 = -jnp.tanh(log_a) * (a * a + 1.0)
    u = jnp.sqrt(one_m_a2) * (ig * xc)
    return a, u


def _mixer_block_kernel(x_ref, g1_ref, win_ref, lbp_ref, hgg_ref, cw_ref, cb_ref, wra_ref,
                        bra_ref, wix_ref, bix_ref, c8_ref, cm_ref, wout_ref,
                        xo_ref, hs_ref, ls_ref, cs_ref,
                        st_scr, h_scr, xpad_scr, hb_scr, q_scr, lf_scr, k_scr, v_scr, g_scr,
                        gl_scr, a_scr, u_scr, m_scr, *, tl):
    l = pl.program_id(1)
    nl = pl.num_programs(1)

    @pl.when(l == 0)
    def _():
        st_scr[...] = jnp.zeros_like(st_scr)
        h_scr[...] = jnp.zeros_like(h_scr)
        xpad_scr[0:8, :] = jnp.zeros((8, LRU_WIDTH), F32)

    hb_scr[...] = _rms(x_ref[0], g1_ref[...]).astype(BF16)
    hb = hb_scr[...]
    zg = lambda j: _dot(hb, win_ref[:, j * HG_WIDTH:(j + 1) * HG_WIDTH])
    q_scr[...] = _silu(zg(0))
    log_f, kk = _hgrn_gates(zg(1), lbp_ref[0:1, :], lbp_ref[1:2, :], lbp_ref[2:3, :])
    lf_scr[...] = log_f
    k_scr[...] = kk
    v_scr[...] = zg(2)
    g_scr[...] = _silu(zg(3))
    xpad_scr[8:8 + tl, :] = zg(4)
    gl_scr[...] = _gelu_tanh(zg(5))
    xc = cb_ref[...]
    for j in range(CONV_K):
        xc = xc + xpad_scr[5 + j:5 + j + tl, :] * cw_ref[j:j + 1, :]
    tail = xpad_scr[tl + 5:tl + 8, :]
    xpad_scr[5:8, :] = tail
    a, u = _lru_gates(xc, wra_ref[...], bra_ref[...], wix_ref[...], bix_ref[...], c8_ref[...])
    a_scr[...] = a
    u_scr[...] = u

    row = lax.broadcasted_iota(I32, (CHUNK, 1), 0)
    ti = lax.broadcasted_iota(I32, (CHUNK, CHUNK), 0)
    si = lax.broadcasted_iota(I32, (CHUNK, CHUNK), 1)
    t3 = lax.broadcasted_iota(I32, (1, DIAG, 1), 1)
    sub3 = lax.broadcasted_iota(I32, (1, SUBLANES, 1), 1)
    nblk = CHUNK // DIAG
    ngrp = CHUNK // SUBLANES
    levels = []
    w = CHUNK // 2
    while w >= DIAG:
        sh = (2 * w).bit_length() - 1
        pairs = ((ti >> sh) == (si >> sh)) & ((ti & (2 * w - 1)) >= w) & ((si & (2 * w - 1)) < w)
        levels.append((w, (row & (2 * w - 1)) >= w, pairs))
        w //= 2

    def chunk_body(c, carry):
        r0 = pl.multiple_of(c * CHUNK, CHUNK)
        rows = pl.ds(r0, CHUNK)

        for gi in range(LRU_WIDTH // LANES):
            cs = slice(gi * LANES, (gi + 1) * LANES)
            aa = a_scr[rows, cs].reshape(ngrp, SUBLANES, LANES)
            uu = u_scr[rows, cs].reshape(ngrp, SUBLANES, LANES)
            d = 1
            while d < SUBLANES:
                keep = sub3 >= d
                a_sh = pltpu.roll(aa, d, 1)
                u_sh = pltpu.roll(uu, d, 1)
                uu = jnp.where(keep, aa * u_sh + uu, uu)
                aa = jnp.where(keep, aa * a_sh, aa)
                d *= 2
            hprev = h_scr[:, cs]
            groups = []
            for j in range(ngrp):
                hj = uu[j] + aa[j] * hprev
                groups.append(hj)
                hprev = hj[SUBLANES - 1:SUBLANES, :]
            hs = jnp.concatenate(groups, axis=0)
            h_scr[:, cs] = hprev
            m_scr[rows, HG_WIDTH + gi * LANES:HG_WIDTH + (gi + 1) * LANES] = (
                hs * gl_scr[rows, cs]).astype(BF16)

        q = q_scr[rows, :]
        k = k_scr[rows, :]
        v = v_scr[rows, :]
        f1, f2, f3 = _split3(lf_scr[rows, :])
        cm = cm_ref[...]
        b = _dot(cm, f1) + _dot(cm, f2) + _dot(cm, f3)

        for hh in range(HG_HEADS):
            cs = slice(hh * HG_D, (hh + 1) * HG_D)
            qh, kh, vh, bh = q[:, cs], k[:, cs], v[:, cs], b[:, cs]
            vb = vh.astype(BF16)
            amat = jnp.zeros((CHUNK, CHUNK), F32)
            for w, upper, pairs in levels:
                dec = _exp_neg(jnp.abs(bh - _level_reference(bh, w)))
                y = (jnp.where(upper, qh, kh) * dec).astype(BF16)
                amat = jnp.where(pairs, _dot_nt(y, y), amat)
            o = _dot(amat.astype(BF16), vb)
            st = st_scr[hh]
            o = o + _dot_nt((qh * jnp.exp(bh)).astype(BF16), st.astype(BF16))
            if DIAG == 1:
                o = o + jnp.sum(qh * kh, axis=-1, keepdims=True) * vh
            else:
                q3 = qh.reshape(nblk, DIAG, HG_D)
                k3 = kh.reshape(nblk, DIAG, HG_D)
                v3 = vh.reshape(nblk, DIAG, HG_D)
                b3 = bh.reshape(nblk, DIAG, HG_D)
                od = jnp.zeros((nblk, DIAG, HG_D), F32)
                for s in range(DIAG):
                    dec = jnp.exp(jnp.minimum(b3 - b3[:, s:s + 1, :], 0.0))
                    p = q3 * (k3[:, s:s + 1, :] * dec)
                    rs = jnp.sum(p, axis=-1, keepdims=True)
                    rs = jnp.where(t3 >= s, rs, 0.0)
                    od = od + rs * v3[:, s:s + 1, :]
                o = o + od.reshape(CHUNK, HG_D)
            bl = bh[CHUNK - 1:CHUNK, :]
            kdec = (kh * jnp.exp(bl - bh)).astype(BF16)
            st_scr[hh] = st * jnp.exp(bl) + _dot_tn(vb, kdec)
            o = o * lax.rsqrt(jnp.mean(o * o, axis=-1, keepdims=True) + EPS)
            o = o * hgg_ref[:, cs]
            o = o * g_scr[rows, cs]
            m_scr[rows, cs] = o.astype(BF16)
        return carry

    lax.fori_loop(0, tl // CHUNK, chunk_body, 0, unroll=8)
    xo_ref[0] = x_ref[0] + _dot(m_scr[...], wout_ref[...])

    @pl.when(l == nl - 1)
    def _():
        for hh in range(HG_HEADS):
            hs_ref[0, hh] = st_scr[hh].T
        ls_ref[0] = h_scr[...]
        cs_ref[0] = xpad_scr[5:8, :]


def _mixer_block(x, gain, w_in, w_out, layer, mp, tl=512):
    bsz, seq, d = x.shape
    tl = min(tl, seq)
    full = lambda shape: pl.BlockSpec(shape, lambda b, l: (0,) * len(shape))
    stacked = lambda shape: pl.BlockSpec((None,) + shape, lambda b, l: (layer, 0, 0))
    out_shapes = (
        jax.ShapeDtypeStruct((bsz, seq, d), F32),
        jax.ShapeDtypeStruct((bsz, HG_HEADS, HG_D, HG_D), F32),
        jax.ShapeDtypeStruct((bsz, 1, LRU_WIDTH), F32),
        jax.ShapeDtypeStruct((bsz, CONV_K - 1, LRU_WIDTH), F32),
    )
    half = lambda dt: pltpu.VMEM((tl, HG_WIDTH), dt)
    xo, hg, lru, conv = pl.pallas_call(
        functools.partial(_mixer_block_kernel, tl=tl),
        out_shape=out_shapes,
        grid=(bsz, seq // tl),
        in_specs=[
            pl.BlockSpec((1, tl, d), lambda b, l: (b, l, 0)),
            full((1, d)), stacked((d, IN_WIDTH)),
            full((3, HG_WIDTH)), full((1, HG_WIDTH)), full((CONV_K, LRU_WIDTH)),
            full((1, LRU_WIDTH)), full((LRU_WIDTH, LRU_WIDTH)), full((1, LRU_WIDTH)),
            full((LRU_WIDTH, LRU_WIDTH)), full((1, LRU_WIDTH)), full((1, LRU_WIDTH)),
            full(mp["cm"].shape), stacked((d, d)),
        ],
        out_specs=(
            pl.BlockSpec((1, tl, d), lambda b, l: (b, l, 0)),
            pl.BlockSpec((1, HG_HEADS, HG_D, HG_D), lambda b, l: (b, 0, 0, 0)),
            pl.BlockSpec((1, 1, LRU_WIDTH), lambda b, l: (b, 0, 0)),
            pl.BlockSpec((1, CONV_K - 1, LRU_WIDTH), lambda b, l: (b, 0, 0)),
        ),
        scratch_shapes=[
            pltpu.VMEM((HG_HEADS, HG_D, HG_D), F32),
            pltpu.VMEM((1, LRU_WIDTH), F32),
            pltpu.VMEM((tl + 8, LRU_WIDTH), F32),
            pltpu.VMEM((tl, d), BF16),
            half(F32), half(F32), half(F32), half(F32), half(F32), half(F32), half(F32),
            half(F32),
            pltpu.VMEM((tl, d), BF16),
        ],
        compiler_params=_cparams(("parallel", "arbitrary")),
        name="mixer_block",
    )(x, gain.reshape(1, d), w_in, mp["lbp"], mp["hgg"], mp["cw"], mp["cb"], mp["wra"],
      mp["bra"], mp["wix"], mp["bix"], mp["c8"], mp["cm"], w_out)
    return xo, hg, lru.reshape(bsz, LRU_WIDTH), conv


def _mixer_step_kernel(z_ref, sh_ref, sl_ref, sc_ref, lbp_ref, hgg_ref, cw_ref, cb_ref, wra_ref,
                       bra_ref, wix_ref, bix_ref, c8_ref, *rest, bb, aliased):
    m_ref, hn_ref, ln_ref, cn_ref, o_scr = rest[1:] if aliased else rest
    z = z_ref[...]
    xr = z[:, 4 * HG_WIDTH:4 * HG_WIDTH + LRU_WIDTH]
    gr = z[:, 4 * HG_WIDTH + LRU_WIDTH:IN_WIDTH]
    buf = sc_ref[...]
    xc = cb_ref[...]
    for j in range(CONV_K - 1):
        xc = xc + buf[:, j * LRU_WIDTH:(j + 1) * LRU_WIDTH] * cw_ref[j:j + 1, :]
    xc = xc + xr * cw_ref[CONV_K - 1:CONV_K, :]
    cn_ref[:, 0:(CONV_K - 2) * LRU_WIDTH] = buf[:, LRU_WIDTH:(CONV_K - 1) * LRU_WIDTH]
    cn_ref[:, (CONV_K - 2) * LRU_WIDTH:(CONV_K - 1) * LRU_WIDTH] = xr
    a, u = _lru_gates(xc, wra_ref[...], bra_ref[...], wix_ref[...], bix_ref[...], c8_ref[...])
    hnew = u + a * sl_ref[...]
    ln_ref[...] = hnew
    m_ref[:, HG_WIDTH:D_MODEL] = (hnew * _gelu_tanh(gr)).astype(m_ref.dtype)

    q = _silu(z[:, 0:HG_WIDTH])
    log_f, k = _hgrn_gates(z[:, HG_WIDTH:2 * HG_WIDTH], lbp_ref[0:1, :], lbp_ref[1:2, :],
                           lbp_ref[2:3, :])
    f = jnp.exp(log_f)
    v = z[:, 2 * HG_WIDTH:3 * HG_WIDTH]
    gate = _silu(z[:, 3 * HG_WIDTH:4 * HG_WIDTH])
    zpad = jnp.zeros((HG_D - bb, HG_D), F32)
    for hh in range(HG_HEADS):
        cs = slice(hh * HG_D, (hh + 1) * HG_D)
        ft = jnp.concatenate([f[:, cs], zpad], axis=0).T
        kt = jnp.concatenate([k[:, cs], zpad], axis=0).T
        qt = jnp.concatenate([q[:, cs], zpad], axis=0).T
        for j in range(bb):
            fcol = jnp.broadcast_to(ft[:, j:j + 1], (HG_D, HG_D))
            kcol = jnp.broadcast_to(kt[:, j:j + 1], (HG_D, HG_D))
            qcol = jnp.broadcast_to(qt[:, j:j + 1], (HG_D, HG_D))
            vrow = jnp.broadcast_to(v[j:j + 1, cs], (HG_D, HG_D))
            sn = fcol * sh_ref[j, hh] + kcol * vrow
            hn_ref[j, hh] = sn
            o_scr[j:j + 1, cs] = jnp.sum(qcol * sn, axis=0, keepdims=True)
    for hh in range(HG_HEADS):
        cs = slice(hh * HG_D, (hh + 1) * HG_D)
        o = o_scr[:, cs]
        o = o * lax.rsqrt(jnp.mean(o * o, axis=-1, keepdims=True) + EPS)
        o = o * hgg_ref[:, cs]
        o = o * gate[:, cs]
        m_ref[:, cs] = o.astype(m_ref.dtype)


def _mixer_step(z, s_hg_all, layer, s_lru, s_conv, mp, hg_buf=None, bb=16):
    bsz = z.shape[0]
    bb = min(bb, bsz)
    cw3 = (CONV_K - 1) * LRU_WIDTH
    full = lambda shape: pl.BlockSpec(shape, lambda i: (0,) * len(shape))
    state_spec = pl.BlockSpec((None, bb, HG_HEADS, HG_D, HG_D), lambda i: (layer, i, 0, 0, 0))
    out_shapes = (
        jax.ShapeDtypeStruct((bsz, D_MODEL), mp["wra"].dtype),
        jax.ShapeDtypeStruct(s_hg_all.shape, F32),
        jax.ShapeDtypeStruct((bsz, LRU_WIDTH), F32),
        jax.ShapeDtypeStruct((bsz, cw3), F32),
    )
    in_specs = [
        pl.BlockSpec((bb, IN_WIDTH), lambda i: (i, 0)),
        state_spec,
        pl.BlockSpec((bb, LRU_WIDTH), lambda i: (i, 0)),
        pl.BlockSpec((bb, cw3), lambda i: (i, 0)),
        full((3, HG_WIDTH)), full((1, HG_WIDTH)), full((CONV_K, LRU_WIDTH)),
        full((1, LRU_WIDTH)), full((LRU_WIDTH, LRU_WIDTH)), full((1, LRU_WIDTH)),
        full((LRU_WIDTH, LRU_WIDTH)), full((1, LRU_WIDTH)), full((1, LRU_WIDTH)),
    ]
    args = [z, s_hg_all, s_lru, s_conv.reshape(bsz, cw3), mp["lbp"], mp["hgg"], mp["cw"],
            mp["cb"], mp["wra"], mp["bra"], mp["wix"], mp["bix"], mp["c8"]]
    aliases = {}
    if hg_buf is not None:
        aliases = {len(args): 1}
        in_specs.append(pl.BlockSpec(memory_space=pl.ANY))
        args.append(hg_buf)
    merged, hg, lru, conv = pl.pallas_call(
        functools.partial(_mixer_step_kernel, bb=bb, aliased=hg_buf is not None),
        out_shape=out_shapes,
        grid=(bsz // bb,),
        in_specs=in_specs,
        out_specs=(
            pl.BlockSpec((bb, D_MODEL), lambda i: (i, 0)),
            state_spec,
            pl.BlockSpec((bb, LRU_WIDTH), lambda i: (i, 0)),
            pl.BlockSpec((bb, cw3), lambda i: (i, 0)),
        ),
        scratch_shapes=[pltpu.VMEM((bb, HG_WIDTH), F32)],
        input_output_aliases=aliases,
        compiler_params=_cparams(("parallel",)),
        name="mixer_step",
    )(*args)
    return merged, hg, lru, conv.reshape(bsz, CONV_K - 1, LRU_WIDTH)


def _router_kernel(x_ref, g_ref, rw_ref, tril_ref, *refs, n_cast):
    cast_src = refs[:n_cast]
    info_ref, idx_ref, cnt_ref, zero_ref = refs[n_cast:n_cast + 4]
    cast_dst = refs[n_cast + 4:2 * n_cast + 4]
    carry_scr = refs[-1]
    i = pl.program_id(0)

    @pl.when(i == 0)
    def _():
        carry_scr[...] = jnp.zeros_like(carry_scr)

    zero_ref[...] = jnp.zeros_like(zero_ref)
    for src, dst in zip(cast_src, cast_dst):
        dst[...] = src[...].astype(dst.dtype)

    h = _rms(x_ref[...], g_ref[...])
    logits = jnp.dot(h, rw_ref[...], preferred_element_type=F32, precision=lax.Precision.HIGHEST)
    lane = lax.broadcasted_iota(I32, logits.shape, 1)
    neg = jnp.float32(-jnp.inf)
    logits = jnp.where(lane < N_EXPERTS, logits, neg)
    m1 = jnp.max(logits, axis=-1, keepdims=True)
    i1 = jnp.min(jnp.where(logits == m1, lane, LANES), axis=-1, keepdims=True)
    l2 = jnp.where(lane == i1, neg, logits)
    m2 = jnp.max(l2, axis=-1, keepdims=True)
    i2 = jnp.min(jnp.where(l2 == m2, lane, LANES), axis=-1, keepdims=True)
    e = jnp.exp(m2 - m1)
    p1 = 1.0 / (1.0 + e)
    p2 = e / (1.0 + e)
    oh1 = (lane == i1).astype(F32)
    oh2 = (lane == i2).astype(F32)
    sel = oh1 + oh2
    before = _dot(tril_ref[...], sel.astype(BF16)) + carry_scr[...]
    r1 = jnp.sum(oh1 * before, axis=-1, keepdims=True)
    r2 = jnp.sum(oh2 * before, axis=-1, keepdims=True)
    carry_scr[...] += jnp.sum(sel, axis=0, keepdims=True)
    info = jnp.where(lane == 0, i1.astype(F32), 0.0)
    info = jnp.where(lane == 1, i2.astype(F32), info)
    info = jnp.where(lane == 2, r1, info)
    info = jnp.where(lane == 3, r2, info)
    info = jnp.where(lane == 4, p1, info)
    info = jnp.where(lane == 5, p2, info)
    info_ref[...] = info
    idx_ref[...] = info.T[0:SUBLANES, :].astype(I32)
    cnt_ref[...] = carry_scr[...]


def _router(x, gain, rw_pad, n_rows, cast=(), tm=512):
    t, d = x.shape
    tm = min(tm, t)
    steps = t // tm
    zrows = n_rows // steps
    assert zrows * steps == n_rows and zrows % SUBLANES == 0
    assert all(c.shape[0] % (steps * 2 * SUBLANES) == 0 for c in cast)
    slab = lambda c: pl.BlockSpec((c.shape[0] // steps, c.shape[1]), lambda i: (i, 0))
    tril = jnp.tril(jnp.ones((tm, tm), BF16), -1)
    return pl.pallas_call(
        functools.partial(_router_kernel, n_cast=len(cast)),
        out_shape=[jax.ShapeDtypeStruct((t, LANES), F32),
                   jax.ShapeDtypeStruct((SUBLANES, t), I32),
                   jax.ShapeDtypeStruct((1, LANES), F32),
                   jax.ShapeDtypeStruct((n_rows, d), F32)]
        + [jax.ShapeDtypeStruct(c.shape, BF16) for c in cast],
        grid=(steps,),
        in_specs=[
            pl.BlockSpec((tm, d), lambda i: (i, 0)),
            pl.BlockSpec((1, d), lambda i: (0, 0)),
            pl.BlockSpec((d, LANES), lambda i: (0, 0)),
            pl.BlockSpec((tm, tm), lambda i: (0, 0)),
        ] + [slab(c) for c in cast],
        out_specs=[pl.BlockSpec((tm, LANES), lambda i: (i, 0)),
                   pl.BlockSpec((SUBLANES, tm), lambda i: (0, i)),
                   pl.BlockSpec((1, LANES), lambda i: (0, 0)),
                   pl.BlockSpec((zrows, d), lambda i: (i, 0))] + [slab(c) for c in cast],
        scratch_shapes=[pltpu.VMEM((1, LANES), F32)],
        compiler_params=_cparams(("arbitrary",)),
        name="router",
    )(x, gain.reshape(1, d), rw_pad, tril, *cast)


def _row_copy(src, dst, s, d, sem):
    return pltpu.make_async_copy(src.at[pl.ds(s, 1)], dst.at[pl.ds(d, 1)], sem)


def _tile_dest(dest, t, td):
    nb = t // td
    dest3 = jnp.concatenate([dest[0].reshape(nb, td), dest[1].reshape(nb, td)], axis=1)
    return dest3.reshape(nb, 1, 2 * td)


def _dispatch_kernel(dest_ref, x_ref, g_ref, xs_in_ref, xs_ref, pk_scr, sem, *, td):
    del xs_in_ref
    pk_scr[...] = _rms(x_ref[...], g_ref[...])

    def issue(r, c):
        _row_copy(pk_scr, xs_ref, r, dest_ref[0, 0, r], sem).start(priority=0)
        _row_copy(pk_scr, xs_ref, r, dest_ref[0, 0, td + r], sem).start(priority=1)
        return c

    lax.fori_loop(0, td, issue, 0, unroll=8)
    for _ in range(2):
        pltpu.make_async_copy(pk_scr, xs_ref.at[pl.ds(0, td)], sem).wait()


def _dispatch(x, gain, dest, xs0, td=512):
    t, d = x.shape
    td = min(td, t)
    n_rows = xs0.shape[0]
    return pl.pallas_call(
        functools.partial(_dispatch_kernel, td=td),
        out_shape=jax.ShapeDtypeStruct((n_rows, d), F32),
        grid=(t // td,),
        in_specs=[
            pl.BlockSpec((1, 1, 2 * td), lambda i: (i, 0, 0), memory_space=pltpu.SMEM),
            pl.BlockSpec((td, d), lambda i: (i, 0)),
            pl.BlockSpec((1, d), lambda i: (0, 0)),
            pl.BlockSpec(memory_space=pl.ANY),
        ],
        out_specs=pl.BlockSpec(memory_space=pl.ANY),
        scratch_shapes=[pltpu.VMEM((td, d), F32), pltpu.SemaphoreType.DMA(())],
        input_output_aliases={3: 0},
        compiler_params=pltpu.CompilerParams(dimension_semantics=("arbitrary",),
                                             vmem_limit_bytes=VMEM_LIMIT_BYTES,
                                             has_side_effects=True),
        name="dispatch",
    )(_tile_dest(dest, t, td), x, gain.reshape(1, d), xs0)


def _combine_kernel(dest_ref, nxt_ref, x_ref, info_ref, g_ref, ys_ref, o_ref, y_scr, sems, *, td):
    i = pl.program_id(0)
    nb = pl.num_programs(0)

    def fetch(idx_ref, slot):
        def issue(r, c):
            _row_copy(ys_ref, y_scr.at[slot, 0], idx_ref[0, 0, r], r,
                      sems.at[slot]).start(priority=0)
            _row_copy(ys_ref, y_scr.at[slot, 1], idx_ref[0, 0, td + r], r,
                      sems.at[slot]).start(priority=1)
            return c

        lax.fori_loop(0, td, issue, 0, unroll=8)

    def finish(slot):
        for j in range(2):
            pltpu.make_async_copy(ys_ref.at[pl.ds(0, td)], y_scr.at[slot, j],
                                  sems.at[slot]).wait()
        p1 = info_ref[:, 4:5]
        p2 = info_ref[:, 5:6]
        x = x_ref[...] + (p1 * y_scr[slot, 0] + p2 * y_scr[slot, 1])
        o_ref[...] = _rms(x, g_ref[...])

    @pl.when(i == 0)
    def _():
        fetch(dest_ref, 0)

    for slot in range(2):
        @pl.when(i % 2 == slot)
        def _(slot=slot):
            @pl.when(i + 1 < nb)
            def _():
                fetch(nxt_ref, 1 - slot)

            finish(slot)


def _combine(x, ys, dest, info, gain, td=256):
    t, d = x.shape
    td = min(td, t)
    nb = t // td
    row = pl.BlockSpec((td, d), lambda i: (i, 0))
    tiled = _tile_dest(dest, t, td)
    return pl.pallas_call(
        functools.partial(_combine_kernel, td=td),
        out_shape=jax.ShapeDtypeStruct((t, d), F32),
        grid=(nb,),
        in_specs=[
            pl.BlockSpec((1, 1, 2 * td), lambda i: (i, 0, 0), memory_space=pltpu.SMEM),
            pl.BlockSpec((1, 1, 2 * td), lambda i: (jnp.minimum(i + 1, nb - 1), 0, 0),
                         memory_space=pltpu.SMEM),
            row,
            pl.BlockSpec((td, LANES), lambda i: (i, 0)),
            pl.BlockSpec((1, d), lambda i: (0, 0)),
            pl.BlockSpec(memory_space=pl.ANY),
        ],
        out_specs=row,
        scratch_shapes=[pltpu.VMEM((2, 2, td, d), F32), pltpu.SemaphoreType.DMA((2,))],
        compiler_params=_cparams(("arbitrary",)),
        name="combine",
    )(tiled, tiled, x, info, gain.reshape(1, d), ys)


def _moe(x, norm_g, rw_pad, experts, final_g, tm_e):
    t, d = x.shape
    n_tiles = (2 * t + N_EXPERTS * (tm_e - 1)) // tm_e
    n_rows = n_tiles * tm_e
    todo = [] if "bf16" in experts else [w.reshape(-1, w.shape[-1]) for w in experts["f32"]]
    info, idx, cnt, xs0, *rounded = _router(x, norm_g, rw_pad, n_rows, cast=todo)
    if todo:
        experts["bf16"] = [r.reshape(w.shape) for r, w in zip(rounded, experts["f32"])]
    w1, w3, w2 = experts["bf16"]
    e1, e2, r1, r2 = idx[0], idx[1], idx[2], idx[3]
    counts = cnt[0, :N_EXPERTS].astype(I32)
    padded = ((counts + tm_e - 1) // tm_e) * tm_e
    ends = jnp.cumsum(padded)
    starts = ends - padded
    dest = jnp.stack([starts[e1] + r1, starts[e2] + r2])
    tile_start = jnp.arange(n_tiles, dtype=I32) * tm_e
    tile_expert = jnp.minimum(
        jnp.sum((tile_start[:, None] >= ends[None, :]).astype(I32), axis=1), N_EXPERTS - 1)
    n_valid = (ends[-1] // tm_e).reshape(1).astype(I32)
    xs = _dispatch(x, norm_g, dest, xs0)
    ys = _ffn(xs, w1, w3, w2, tile_expert.astype(I32), n_valid, tm=tm_e, n_f=1)
    return _combine(x, ys, dest, info, final_g)


def _block_diag(w):
    n, c, d = w.shape
    eye = jnp.eye(n, dtype=w.dtype)
    return (w[:, :, None, :] * eye[:, None, :, None]).reshape(n * c, n * d)


def _cum_matrix():
    t = jnp.arange(CHUNK)
    return (t[None, :] <= t[:, None]).astype(BF16)


def _mixer_params(l, lb, hg_norm_g, conv_w, conv_b, w_ra, b_ra, w_ix, b_ix, lru_lambda, wdtype):
    row = lambda a: a.reshape(1, -1).astype(F32)
    return {
        "lbp": jnp.stack([jnp.log(lb), jnp.log1p(-lb), 1.0 - lb]).astype(F32),
        "hgg": row(hg_norm_g[l]),
        "cw": conv_w[l].astype(F32),
        "cb": row(conv_b[l]),
        "wra": _block_diag(w_ra[l].astype(F32)).astype(wdtype),
        "bra": row(b_ra[l]),
        "wix": _block_diag(w_ix[l].astype(F32)).astype(wdtype),
        "bix": row(b_ix[l]),
        "c8": row(-LRU_C * jax.nn.softplus(-lru_lambda[l].astype(F32))),
        "cm": _cum_matrix(),
    }


def _trunk(x, seq_shape, states, prm):
    bsz, seq = seq_shape
    one = jnp.ones((1,), I32)
    new_hg, new_lru, new_conv = [], [], []
    hg_stack = None if states is None else states[0]
    for l in range(2):
        mp = prm["mixer"][l]
        if states is None:
            x, s1, s2, s3 = _mixer_block(x.reshape(bsz, seq, D_MODEL), prm["norm1_g"][l],
                                         prm["w_in"], prm["w_out"], l, mp)
            x = x.reshape(bsz * seq, D_MODEL)
            new_hg.append(s1)
        else:
            z = _linear(x, prm["w_in"], l, gain=prm["norm1_g"][l])
            merged, hg_stack, s2, s3 = _mixer_step(z, states[0], l, states[1][l], states[2][l],
                                                   mp, hg_buf=hg_stack)
            x = _linear(merged, prm["w_out"], l, res=x)
        new_lru.append(s2)
        new_conv.append(s3)
        if l == 0:
            n_tiles = max(x.shape[0] // TILE_ROWS, 1)
            n_f = 1 if prm["ffn_w1"].dtype == BF16 else prm["ffn_w1"].shape[2] // F_CHUNK
            x = _ffn(x, prm["ffn_w1"], prm["ffn_w3"], prm["ffn_w2"], jnp.zeros((n_tiles,), I32),
                     one * n_tiles, gain=prm["norm2_g"][0], res=True, tm=TILE_ROWS, n_f=n_f,
                     ck=F_CHUNK)
        else:
            x = _moe(x, prm["norm2_g"][1], prm["rw_pad"], prm["experts"], prm["final_norm_g"],
                     tm_e=TILE_ROWS if states is None else LANES)
    hg_all = jnp.stack(new_hg) if states is None else hg_stack
    return x, hg_all, jnp.stack(new_lru), jnp.stack(new_conv)


def kernel(x_prompt, x_sample, state_hgrn, state_lru, state_conv, norm1_g, w_in, lower_bounds,
           hg_norm_g, conv_w, conv_b, w_ra, b_ra, w_ix, b_ix, lru_lambda, w_out, norm2_g,
           ffn_w1, ffn_w3, ffn_w2, router_w, moe_w1, moe_w3, moe_w2, final_norm_g):
    lb_all = jnp.cumsum(jax.nn.softmax(lower_bounds.astype(F32), axis=0), axis=0)
    lb_all = lb_all - lb_all[0]
    experts = {
        "experts": {"f32": [moe_w1[0].astype(F32), moe_w3[0].astype(F32),
                            moe_w2[0].astype(F32)]},
        "rw_pad": jnp.pad(router_w[0].astype(F32), ((0, 0), (0, LANES - N_EXPERTS))),
        "norm1_g": norm1_g, "norm2_g": norm2_g, "final_norm_g": final_norm_g,
    }

    def params(wdtype):
        return dict(
            experts,
            mixer=[_mixer_params(l, lb_all[l], hg_norm_g, conv_w, conv_b, w_ra, b_ra, w_ix, b_ix,
                                 lru_lambda, wdtype) for l in range(2)],
            w_in=w_in.astype(wdtype), w_out=w_out.astype(wdtype),
            ffn_w1=ffn_w1.astype(wdtype), ffn_w3=ffn_w3.astype(wdtype),
            ffn_w2=ffn_w2.astype(wdtype))

    bp, sp, d = x_prompt.shape
    bs = x_sample.shape[0]
    y_p, hg_p, lru_p, conv_p = _trunk(x_prompt.reshape(bp * sp, d), (bp, sp), None, params(BF16))
    y_s, hg_s, lru_s, conv_s = _trunk(x_sample.reshape(bs, d), (bs, 1),
                                      (state_hgrn, state_lru, state_conv), params(F32))
    return (y_p.reshape(bp, sp, d), y_s.reshape(bs, 1, d), hg_p, lru_p, conv_p, hg_s, lru_s,
            conv_s)
```

```python
import functools

import jax
import jax.numpy as jnp
from jax import lax
from jax.experimental import pallas as pl
from jax.experimental.pallas import tpu as pltpu

F32 = jnp.float32
BF16 = jnp.bfloat16
I32 = jnp.int32

D_MODEL = 1024
HG_WIDTH = 512
HG_HEADS = 4
HG_D = 128
LRU_WIDTH = 512
LRU_BLOCKS = 8
LRU_C = 8.0
CONV_K = 4
IN_WIDTH = 4 * HG_WIDTH + 2 * LRU_WIDTH
N_EXPERTS = 8
EPS = 1e-6

CHUNK = 64
DIAG = 1
LANES = 128
SUBLANES = 8
TILE_ROWS = 512
F_CHUNK = 256
VMEM_LIMIT_BYTES = 56 * 1024 * 1024


def _cparams(sem):
    return pltpu.CompilerParams(dimension_semantics=sem, vmem_limit_bytes=VMEM_LIMIT_BYTES)


def _exp_neg(x):
    return jnp.exp2(x * -1.4426950408889634)


def _sigmoid(x):
    return 1.0 / (1.0 + _exp_neg(x))


def _silu(x):
    return x * _sigmoid(x)


def _gelu_tanh(x):
    c = 0.7978845608028654
    return 0.5 * x * (1.0 + jnp.tanh(c * (x + 0.044715 * (x * x * x))))


def _rms(x, g):
    return x * lax.rsqrt(jnp.mean(x * x, axis=-1, keepdims=True) + EPS) * g


def _dot(a, b):
    if b.dtype == F32:
        return jnp.dot(a.astype(F32), b, preferred_element_type=F32,
                       precision=lax.Precision.HIGHEST)
    return jnp.dot(a.astype(BF16), b, preferred_element_type=F32)


def _dot_nt(a, b):
    return lax.dot_general(a, b, (((1,), (1,)), ((), ())), preferred_element_type=F32)


def _dot_tn(a, b):
    return lax.dot_general(a, b, (((0,), (0,)), ((), ())), preferred_element_type=F32)


def _split3(x):
    x1 = x.astype(BF16)
    r1 = x - x1.astype(F32)
    x2 = r1.astype(BF16)
    x3 = (r1 - x2.astype(F32)).astype(BF16)
    return x1, x2, x3


def _linear_kernel(*refs, has_gain, has_res):
    it = iter(refs)
    x_ref = next(it)
    g_ref = next(it) if has_gain else None
    w_ref = next(it)
    r_ref = next(it) if has_res else None
    o_ref = next(it)
    h_scr = next(it)

    @pl.when(pl.program_id(1) == 0)
    def _():
        x = x_ref[...].astype(F32)
        if has_gain:
            x = _rms(x, g_ref[...])
        h_scr[...] = x.astype(h_scr.dtype)

    acc = _dot(h_scr[...], w_ref[...])
    if has_res:
        acc = acc + r_ref[...]
    o_ref[...] = acc.astype(o_ref.dtype)


def _linear(x, w, layer, gain=None, res=None, tm=1024, tn=1024):
    m, k = x.shape
    n = w.shape[2]
    tm = min(tm, m)
    tn = min(tn, n)
    in_specs = [pl.BlockSpec((tm, k), lambda i, j: (i, 0))]
    args = [x]
    if gain is not None:
        in_specs.append(pl.BlockSpec((1, k), lambda i, j: (0, 0)))
        args.append(gain.reshape(1, k))
    in_specs.append(pl.BlockSpec((None, k, tn), lambda i, j: (layer, 0, j)))
    args.append(w)
    if res is not None:
        in_specs.append(pl.BlockSpec((tm, tn), lambda i, j: (i, j)))
        args.append(res)
    return pl.pallas_call(
        functools.partial(_linear_kernel, has_gain=gain is not None, has_res=res is not None),
        out_shape=jax.ShapeDtypeStruct((m, n), F32),
        grid=(m // tm, n // tn),
        in_specs=in_specs,
        out_specs=pl.BlockSpec((tm, tn), lambda i, j: (i, j)),
        scratch_shapes=[pltpu.VMEM((tm, k), w.dtype)],
        compiler_params=_cparams(("parallel", "arbitrary")),
        name="linear",
    )(*args)


def _ffn_kernel(te_ref, nv_ref, *refs, has_gain, has_res, tf, ck):
    del te_ref
    it = iter(refs)
    x_ref = next(it)
    g_ref = next(it) if has_gain else None
    w1_ref, w3_ref, w2_ref, o_ref, xb_scr, a_scr, acc_scr = it
    i = pl.program_id(0)
    j = pl.program_id(1)
    nj = pl.num_programs(1)
    valid = i < nv_ref[0]

    @pl.when(valid)
    def _():
        @pl.when(j == 0)
        def _():
            x = x_ref[...]
            if has_gain:
                x = _rms(x, g_ref[...])
            xb_scr[...] = x.astype(xb_scr.dtype)

        xb = xb_scr[...]
        for c in range(tf // ck):
            h1 = _dot(xb, w1_ref[0, :, c * ck:(c + 1) * ck])
            h3 = _dot(xb, w3_ref[0, :, c * ck:(c + 1) * ck])
            a_scr[:, c * ck:(c + 1) * ck] = (_silu(h1) * h3).astype(a_scr.dtype)
        part = _dot(a_scr[...], w2_ref[0])

        @pl.when(j == 0)
        def _():
            acc_scr[...] = part

        @pl.when(j > 0)
        def _():
            acc_scr[...] += part

        @pl.when(j == nj - 1)
        def _():
            out = acc_scr[...]
            if has_res:
                out = x_ref[...] + out
            o_ref[...] = out

    @pl.when(jnp.logical_and(jnp.logical_not(valid), j == nj - 1))
    def _():
        o_ref[...] = jnp.zeros_like(o_ref)


def _ffn(x, w1, w3, w2, tile_expert, n_valid, gain=None, res=False, tm=TILE_ROWS, n_f=2,
         ck=F_CHUNK):
    p, d = x.shape
    f = w1.shape[2]
    tm = min(tm, p)
    tf = f // n_f
    n_tiles = p // tm
    in_specs = [pl.BlockSpec((tm, d), lambda i, j, te, nv: (i, 0))]
    args = [x]
    if gain is not None:
        in_specs.append(pl.BlockSpec((1, d), lambda i, j, te, nv: (0, 0)))
        args.append(gain.reshape(1, d))
    in_specs += [
        pl.BlockSpec((1, d, tf), lambda i, j, te, nv: (te[i], 0, j)),
        pl.BlockSpec((1, d, tf), lambda i, j, te, nv: (te[i], 0, j)),
        pl.BlockSpec((1, tf, d), lambda i, j, te, nv: (te[i], j, 0)),
    ]
    args += [w1, w3, w2]
    grid_spec = pltpu.PrefetchScalarGridSpec(
        num_scalar_prefetch=2,
        grid=(n_tiles, n_f),
        in_specs=in_specs,
        out_specs=pl.BlockSpec((tm, d), lambda i, j, te, nv: (i, 0)),
        scratch_shapes=[pltpu.VMEM((tm, d), w1.dtype), pltpu.VMEM((tm, tf), w1.dtype),
                        pltpu.VMEM((tm, d), F32)],
    )
    return pl.pallas_call(
        functools.partial(_ffn_kernel, has_gain=gain is not None, has_res=res, tf=tf, ck=ck),
        out_shape=jax.ShapeDtypeStruct((p, d), F32),
        grid_spec=grid_spec,
        compiler_params=_cparams(("arbitrary", "arbitrary")),
        name="swiglu",
    )(tile_expert, n_valid, *args)


def _hgrn_gates(fz, loglb, l1mlb, oneml):
    e = _exp_neg(jnp.abs(fz))
    ope = 1.0 + e
    log_sig = jnp.minimum(fz, 0.0) - jnp.log(ope)
    cc = l1mlb + log_sig
    log_f = jnp.maximum(loglb, cc) + jnp.log(1.0 + _exp_neg(jnp.abs(loglb - cc)))
    k = oneml * (jnp.where(fz > 0.0, e, 1.0) / ope)
    return log_f, k


def _level_reference(b, w):
    n = b.shape[0]
    if w >= 4:
        return jnp.concatenate(
            [jnp.broadcast_to(b[j + w:j + w + 1, :], (2 * w, b.shape[1]))
             for j in range(0, n, 2 * w)], axis=0)
    b3 = b.reshape(n // SUBLANES, SUBLANES, b.shape[1])
    ahead = lambda s: pltpu.roll(b3, SUBLANES - s, 1)
    pos = lax.broadcasted_iota(I32, (1, SUBLANES, 1), 1) & (2 * w - 1)
    if w == 2:
        r3 = jnp.where(pos == 0, ahead(2),
                       jnp.where(pos == 1, ahead(1),
                                 jnp.where(pos == 2, b3, pltpu.roll(b3, 1, 1))))
    else:
        r3 = jnp.where(pos == 0, ahead(1), b3)
    return r3.reshape(b.shape)


def _lru_gates(xc, wra, bra, wix, bix, c8):
    r = _sigmoid(_dot(xc, wra) + bra)
    ig = _sigmoid(_dot(xc, wix) + bix)
    log_a = c8 * r
    a = jnp.exp(log_a)
    one_m_a2 = -jnp.tanh(log_a) * (a * a + 1.0)
    u = jnp.sqrt(one_m_a2) * (ig * xc)
    return a, u


def _mixer_block_kernel(x_ref, g1_ref, win_ref, lbp_ref, hgg_ref, cw_ref, cb_ref, wra_ref,
                        bra_ref, wix_ref, bix_ref, c8_ref, cm_ref, wout_ref, *rest, tl, n_cast):
    cast_src = rest[:n_cast]
    xo_ref, hs_ref, ls_ref, cs_ref = rest[n_cast:n_cast + 4]
    cast_dst = rest[n_cast + 4:2 * n_cast + 4]
    (st_scr, h_scr, xpad_scr, hb_scr, q_scr, lf_scr, k_scr, v_scr, g_scr, gl_scr, a_scr, u_scr,
     m_scr) = rest[2 * n_cast + 4:]
    l = pl.program_id(1)
    nl = pl.num_programs(1)
    for src, dst in zip(cast_src, cast_dst):
        dst[...] = src[...].astype(dst.dtype)

    @pl.when(l == 0)
    def _():
        st_scr[...] = jnp.zeros_like(st_scr)
        h_scr[...] = jnp.zeros_like(h_scr)
        xpad_scr[0:8, :] = jnp.zeros((8, LRU_WIDTH), F32)

    hb_scr[...] = _rms(x_ref[0], g1_ref[...]).astype(BF16)
    hb = hb_scr[...]
    zg = lambda j: _dot(hb, win_ref[:, j * HG_WIDTH:(j + 1) * HG_WIDTH])
    q_scr[...] = _silu(zg(0))
    log_f, kk = _hgrn_gates(zg(1), lbp_ref[0:1, :], lbp_ref[1:2, :], lbp_ref[2:3, :])
    lf_scr[...] = log_f
    k_scr[...] = kk
    v_scr[...] = zg(2)
    g_scr[...] = _silu(zg(3))
    xpad_scr[8:8 + tl, :] = zg(4)
    gl_scr[...] = _gelu_tanh(zg(5))
    xc = cb_ref[...]
    for j in range(CONV_K):
        xc = xc + xpad_scr[5 + j:5 + j + tl, :] * cw_ref[j:j + 1, :]
    tail = xpad_scr[tl + 5:tl + 8, :]
    xpad_scr[5:8, :] = tail
    a, u = _lru_gates(xc, wra_ref[...], bra_ref[...], wix_ref[...], bix_ref[...], c8_ref[...])
    a_scr[...] = a
    u_scr[...] = u

    row = lax.broadcasted_iota(I32, (CHUNK, 1), 0)
    ti = lax.broadcasted_iota(I32, (CHUNK, CHUNK), 0)
    si = lax.broadcasted_iota(I32, (CHUNK, CHUNK), 1)
    t3 = lax.broadcasted_iota(I32, (1, DIAG, 1), 1)
    sub3 = lax.broadcasted_iota(I32, (1, SUBLANES, 1), 1)
    nblk = CHUNK // DIAG
    ngrp = CHUNK // SUBLANES
    levels = []
    w = CHUNK // 2
    while w >= DIAG:
        sh = (2 * w).bit_length() - 1
        pairs = ((ti >> sh) == (si >> sh)) & ((ti & (2 * w - 1)) >= w) & ((si & (2 * w - 1)) < w)
        levels.append((w, (row & (2 * w - 1)) >= w, pairs))
        w //= 2

    def chunk_body(c, carry):
        r0 = pl.multiple_of(c * CHUNK, CHUNK)
        rows = pl.ds(r0, CHUNK)

        for gi in range(LRU_WIDTH // LANES):
            cs = slice(gi * LANES, (gi + 1) * LANES)
            aa = a_scr[rows, cs].reshape(ngrp, SUBLANES, LANES)
            uu = u_scr[rows, cs].reshape(ngrp, SUBLANES, LANES)
            d = 1
            while d < SUBLANES:
                keep = sub3 >= d
                a_sh = pltpu.roll(aa, d, 1)
                u_sh = pltpu.roll(uu, d, 1)
                uu = jnp.where(keep, aa * u_sh + uu, uu)
                aa = jnp.where(keep, aa * a_sh, aa)
                d *= 2
            hprev = h_scr[:, cs]
            groups = []
            for j in range(ngrp):
                hj = uu[j] + aa[j] * hprev
                groups.append(hj)
                hprev = hj[SUBLANES - 1:SUBLANES, :]
            hs = jnp.concatenate(groups, axis=0)
            h_scr[:, cs] = hprev
            m_scr[rows, HG_WIDTH + gi * LANES:HG_WIDTH + (gi + 1) * LANES] = (
                hs * gl_scr[rows, cs]).astype(BF16)

        q = q_scr[rows, :]
        k = k_scr[rows, :]
        v = v_scr[rows, :]
        f1, f2, f3 = _split3(lf_scr[rows, :])
        cm = cm_ref[...]
        b = _dot(cm, f1) + _dot(cm, f2) + _dot(cm, f3)

        for hh in range(HG_HEADS):
            cs = slice(hh * HG_D, (hh + 1) * HG_D)
            qh, kh, vh, bh = q[:, cs], k[:, cs], v[:, cs], b[:, cs]
            vb = vh.astype(BF16)
            amat = jnp.zeros((CHUNK, CHUNK), F32)
            for w, upper, pairs in levels:
                dec = _exp_neg(jnp.abs(bh - _level_reference(bh, w)))
                y = (jnp.where(upper, qh, kh) * dec).astype(BF16)
                amat = jnp.where(pairs, _dot_nt(y, y), amat)
            o = _dot(amat.astype(BF16), vb)
            st = st_scr[hh]
            o = o + _dot_nt((qh * jnp.exp(bh)).astype(BF16), st.astype(BF16))
            if DIAG == 1:
                o = o + jnp.sum(qh * kh, axis=-1, keepdims=True) * vh
            else:
                q3 = qh.reshape(nblk, DIAG, HG_D)
                k3 = kh.reshape(nblk, DIAG, HG_D)
                v3 = vh.reshape(nblk, DIAG, HG_D)
                b3 = bh.reshape(nblk, DIAG, HG_D)
                od = jnp.zeros((nblk, DIAG, HG_D), F32)
                for s in range(DIAG):
                    dec = jnp.exp(jnp.minimum(b3 - b3[:, s:s + 1, :], 0.0))
                    p = q3 * (k3[:, s:s + 1, :] * dec)
                    rs = jnp.sum(p, axis=-1, keepdims=True)
                    rs = jnp.where(t3 >= s, rs, 0.0)
                    od = od + rs * v3[:, s:s + 1, :]
                o = o + od.reshape(CHUNK, HG_D)
            bl = bh[CHUNK - 1:CHUNK, :]
            kdec = (kh * jnp.exp(bl - bh)).astype(BF16)
            st_scr[hh] = st * jnp.exp(bl) + _dot_tn(vb, kdec)
            o = o * lax.rsqrt(jnp.mean(o * o, axis=-1, keepdims=True) + EPS)
            o = o * hgg_ref[:, cs]
            o = o * g_scr[rows, cs]
            m_scr[rows, cs] = o.astype(BF16)
        return carry

    lax.fori_loop(0, tl // CHUNK, chunk_body, 0, unroll=8)
    xo_ref[0] = x_ref[0] + _dot(m_scr[...], wout_ref[...])

    @pl.when(l == nl - 1)
    def _():
        for hh in range(HG_HEADS):
            hs_ref[0, hh] = st_scr[hh].T
        ls_ref[0] = h_scr[...]
        cs_ref[0] = xpad_scr[5:8, :]


def _mixer_block(x, gain, w_in, w_out, layer, mp, cast=(), tl=512):
    bsz, seq, d = x.shape
    tl = min(tl, seq)
    nl = seq // tl
    steps = bsz * nl
    assert all(c.shape[0] % (steps * 2 * SUBLANES) == 0 for c in cast)
    full = lambda shape: pl.BlockSpec(shape, lambda b, l: (0,) * len(shape))
    stacked = lambda shape: pl.BlockSpec((None,) + shape, lambda b, l: (layer, 0, 0))
    slab = lambda c: pl.BlockSpec((c.shape[0] // steps, c.shape[1]), lambda b, l: (b * nl + l, 0))
    out_shapes = [
        jax.ShapeDtypeStruct((bsz, seq, d), F32),
        jax.ShapeDtypeStruct((bsz, HG_HEADS, HG_D, HG_D), F32),
        jax.ShapeDtypeStruct((bsz, 1, LRU_WIDTH), F32),
        jax.ShapeDtypeStruct((bsz, CONV_K - 1, LRU_WIDTH), F32),
    ] + [jax.ShapeDtypeStruct(c.shape, BF16) for c in cast]
    half = lambda dt: pltpu.VMEM((tl, HG_WIDTH), dt)
    xo, hg, lru, conv, *rounded = pl.pallas_call(
        functools.partial(_mixer_block_kernel, tl=tl, n_cast=len(cast)),
        out_shape=out_shapes,
        grid=(bsz, nl),
        in_specs=[
            pl.BlockSpec((1, tl, d), lambda b, l: (b, l, 0)),
            full((1, d)), stacked((d, IN_WIDTH)),
            full((3, HG_WIDTH)), full((1, HG_WIDTH)), full((CONV_K, LRU_WIDTH)),
            full((1, LRU_WIDTH)), full((LRU_WIDTH, LRU_WIDTH)), full((1, LRU_WIDTH)),
            full((LRU_WIDTH, LRU_WIDTH)), full((1, LRU_WIDTH)), full((1, LRU_WIDTH)),
            full(mp["cm"].shape), stacked((d, d)),
        ] + [slab(c) for c in cast],
        out_specs=[
            pl.BlockSpec((1, tl, d), lambda b, l: (b, l, 0)),
            pl.BlockSpec((1, HG_HEADS, HG_D, HG_D), lambda b, l: (b, 0, 0, 0)),
            pl.BlockSpec((1, 1, LRU_WIDTH), lambda b, l: (b, 0, 0)),
            pl.BlockSpec((1, CONV_K - 1, LRU_WIDTH), lambda b, l: (b, 0, 0)),
        ] + [slab(c) for c in cast],
        scratch_shapes=[
            pltpu.VMEM((HG_HEADS, HG_D, HG_D), F32),
            pltpu.VMEM((1, LRU_WIDTH), F32),
            pltpu.VMEM((tl + 8, LRU_WIDTH), F32),
            pltpu.VMEM((tl, d), BF16),
            half(F32), half(F32), half(F32), half(F32), half(F32), half(F32), half(F32),
            half(F32),
            pltpu.VMEM((tl, d), BF16),
        ],
        compiler_params=_cparams(("parallel", "arbitrary")),
        name="mixer_block",
    )(x, gain.reshape(1, d), w_in, mp["lbp"], mp["hgg"], mp["cw"], mp["cb"], mp["wra"],
      mp["bra"], mp["wix"], mp["bix"], mp["c8"], mp["cm"], w_out, *cast)
    return xo, hg, lru.reshape(bsz, LRU_WIDTH), conv, rounded


def _mixer_step_kernel(z_ref, sh_ref, sl_ref, sc_ref, lbp_ref, hgg_ref, cw_ref, cb_ref, wra_ref,
                       bra_ref, wix_ref, bix_ref, c8_ref, *rest, bb, aliased):
    m_ref, hn_ref, ln_ref, cn_ref, o_scr = rest[1:] if aliased else rest
    z = z_ref[...]
    xr = z[:, 4 * HG_WIDTH:4 * HG_WIDTH + LRU_WIDTH]
    gr = z[:, 4 * HG_WIDTH + LRU_WIDTH:IN_WIDTH]
    buf = sc_ref[...]
    xc = cb_ref[...]
    for j in range(CONV_K - 1):
        xc = xc + buf[:, j * LRU_WIDTH:(j + 1) * LRU_WIDTH] * cw_ref[j:j + 1, :]
    xc = xc + xr * cw_ref[CONV_K - 1:CONV_K, :]
    cn_ref[:, 0:(CONV_K - 2) * LRU_WIDTH] = buf[:, LRU_WIDTH:(CONV_K - 1) * LRU_WIDTH]
    cn_ref[:, (CONV_K - 2) * LRU_WIDTH:(CONV_K - 1) * LRU_WIDTH] = xr
    a, u = _lru_gates(xc, wra_ref[...], bra_ref[...], wix_ref[...], bix_ref[...], c8_ref[...])
    hnew = u + a * sl_ref[...]
    ln_ref[...] = hnew
    m_ref[:, HG_WIDTH:D_MODEL] = (hnew * _gelu_tanh(gr)).astype(m_ref.dtype)

    q = _silu(z[:, 0:HG_WIDTH])
    log_f, k = _hgrn_gates(z[:, HG_WIDTH:2 * HG_WIDTH], lbp_ref[0:1, :], lbp_ref[1:2, :],
                           lbp_ref[2:3, :])
    f = jnp.exp(log_f)
    v = z[:, 2 * HG_WIDTH:3 * HG_WIDTH]
    gate = _silu(z[:, 3 * HG_WIDTH:4 * HG_WIDTH])
    zpad = jnp.zeros((HG_D - bb, HG_D), F32)
    for hh in range(HG_HEADS):
        cs = slice(hh * HG_D, (hh + 1) * HG_D)
        ft = jnp.concatenate([f[:, cs], zpad], axis=0).T
        kt = jnp.concatenate([k[:, cs], zpad], axis=0).T
        qt = jnp.concatenate([q[:, cs], zpad], axis=0).T
        for j in range(bb):
            fcol = jnp.broadcast_to(ft[:, j:j + 1], (HG_D, HG_D))
            kcol = jnp.broadcast_to(kt[:, j:j + 1], (HG_D, HG_D))
            qcol = jnp.broadcast_to(qt[:, j:j + 1], (HG_D, HG_D))
            vrow = jnp.broadcast_to(v[j:j + 1, cs], (HG_D, HG_D))
            sn = fcol * sh_ref[j, hh] + kcol * vrow
            hn_ref[j, hh] = sn
            o_scr[j:j + 1, cs] = jnp.sum(qcol * sn, axis=0, keepdims=True)
    for hh in range(HG_HEADS):
        cs = slice(hh * HG_D, (hh + 1) * HG_D)
        o = o_scr[:, cs]
        o = o * lax.rsqrt(jnp.mean(o * o, axis=-1, keepdims=True) + EPS)
        o = o * hgg_ref[:, cs]
        o = o * gate[:, cs]
        m_ref[:, cs] = o.astype(m_ref.dtype)


def _mixer_step(z, s_hg_all, layer, s_lru, s_conv, mp, hg_buf=None, bb=16):
    bsz = z.shape[0]
    bb = min(bb, bsz)
    cw3 = (CONV_K - 1) * LRU_WIDTH
    full = lambda shape: pl.BlockSpec(shape, lambda i: (0,) * len(shape))
    state_spec = pl.BlockSpec((None, bb, HG_HEADS, HG_D, HG_D), lambda i: (layer, i, 0, 0, 0))
    out_shapes = (
        jax.ShapeDtypeStruct((bsz, D_MODEL), mp["wra"].dtype),
        jax.ShapeDtypeStruct(s_hg_all.shape, F32),
        jax.ShapeDtypeStruct((bsz, LRU_WIDTH), F32),
        jax.ShapeDtypeStruct((bsz, cw3), F32),
    )
    in_specs = [
        pl.BlockSpec((bb, IN_WIDTH), lambda i: (i, 0)),
        state_spec,
        pl.BlockSpec((bb, LRU_WIDTH), lambda i: (i, 0)),
        pl.BlockSpec((bb, cw3), lambda i: (i, 0)),
        full((3, HG_WIDTH)), full((1, HG_WIDTH)), full((CONV_K, LRU_WIDTH)),
        full((1, LRU_WIDTH)), full((LRU_WIDTH, LRU_WIDTH)), full((1, LRU_WIDTH)),
        full((LRU_WIDTH, LRU_WIDTH)), full((1, LRU_WIDTH)), full((1, LRU_WIDTH)),
    ]
    args = [z, s_hg_all, s_lru, s_conv.reshape(bsz, cw3), mp["lbp"], mp["hgg"], mp["cw"],
            mp["cb"], mp["wra"], mp["bra"], mp["wix"], mp["bix"], mp["c8"]]
    aliases = {}
    if hg_buf is not None:
        aliases = {len(args): 1}
        in_specs.append(pl.BlockSpec(memory_space=pl.ANY))
        args.append(hg_buf)
    merged, hg, lru, conv = pl.pallas_call(
        functools.partial(_mixer_step_kernel, bb=bb, aliased=hg_buf is not None),
        out_shape=out_shapes,
        grid=(bsz // bb,),
        in_specs=in_specs,
        out_specs=(
            pl.BlockSpec((bb, D_MODEL), lambda i: (i, 0)),
            state_spec,
            pl.BlockSpec((bb, LRU_WIDTH), lambda i: (i, 0)),
            pl.BlockSpec((bb, cw3), lambda i: (i, 0)),
        ),
        scratch_shapes=[pltpu.VMEM((bb, HG_WIDTH), F32)],
        input_output_aliases=aliases,
        compiler_params=_cparams(("parallel",)),
        name="mixer_step",
    )(*args)
    return merged, hg, lru, conv.reshape(bsz, CONV_K - 1, LRU_WIDTH)


def _router_kernel(x_ref, g_ref, rw_ref, tril_ref, *refs, n_cast):
    cast_src = refs[:n_cast]
    info_ref, idx_ref, cnt_ref, zero_ref = refs[n_cast:n_cast + 4]
    cast_dst = refs[n_cast + 4:2 * n_cast + 4]
    carry_scr = refs[-1]
    i = pl.program_id(0)

    @pl.when(i == 0)
    def _():
        carry_scr[...] = jnp.zeros_like(carry_scr)

    zero_ref[...] = jnp.zeros_like(zero_ref)
    for src, dst in zip(cast_src, cast_dst):
        dst[...] = src[...].astype(dst.dtype)

    h = _rms(x_ref[...], g_ref[...])
    logits = jnp.dot(h, rw_ref[...], preferred_element_type=F32, precision=lax.Precision.HIGHEST)
    lane = lax.broadcasted_iota(I32, logits.shape, 1)
    neg = jnp.float32(-jnp.inf)
    logits = jnp.where(lane < N_EXPERTS, logits, neg)
    m1 = jnp.max(logits, axis=-1, keepdims=True)
    i1 = jnp.min(jnp.where(logits == m1, lane, LANES), axis=-1, keepdims=True)
    l2 = jnp.where(lane == i1, neg, logits)
    m2 = jnp.max(l2, axis=-1, keepdims=True)
    i2 = jnp.min(jnp.where(l2 == m2, lane, LANES), axis=-1, keepdims=True)
    e = jnp.exp(m2 - m1)
    p1 = 1.0 / (1.0 + e)
    p2 = e / (1.0 + e)
    oh1 = (lane == i1).astype(F32)
    oh2 = (lane == i2).astype(F32)
    sel = oh1 + oh2
    before = _dot(tril_ref[...], sel.astype(BF16)) + carry_scr[...]
    r1 = jnp.sum(oh1 * before, axis=-1, keepdims=True)
    r2 = jnp.sum(oh2 * before, axis=-1, keepdims=True)
    carry_scr[...] += jnp.sum(sel, axis=0, keepdims=True)
    info = jnp.where(lane == 0, i1.astype(F32), 0.0)
    info = jnp.where(lane == 1, i2.astype(F32), info)
    info = jnp.where(lane == 2, r1, info)
    info = jnp.where(lane == 3, r2, info)
    info = jnp.where(lane == 4, p1, info)
    info = jnp.where(lane == 5, p2, info)
    info_ref[...] = info
    idx_ref[...] = info.T[0:SUBLANES, :].astype(I32)
    cnt_ref[...] = carry_scr[...]


def _router(x, gain, rw_pad, n_rows, cast=(), tm=512):
    t, d = x.shape
    tm = min(tm, t)
    steps = t // tm
    zrows = n_rows // steps
    assert zrows * steps == n_rows and zrows % SUBLANES == 0
    assert all(c.shape[0] % (steps * 2 * SUBLANES) == 0 for c in cast)
    slab = lambda c: pl.BlockSpec((c.shape[0] // steps, c.shape[1]), lambda i: (i, 0))
    tril = jnp.tril(jnp.ones((tm, tm), BF16), -1)
    return pl.pallas_call(
        functools.partial(_router_kernel, n_cast=len(cast)),
        out_shape=[jax.ShapeDtypeStruct((t, LANES), F32),
                   jax.ShapeDtypeStruct((SUBLANES, t), I32),
                   jax.ShapeDtypeStruct((1, LANES), F32),
                   jax.ShapeDtypeStruct((n_rows, d), F32)]
        + [jax.ShapeDtypeStruct(c.shape, BF16) for c in cast],
        grid=(steps,),
        in_specs=[
            pl.BlockSpec((tm, d), lambda i: (i, 0)),
            pl.BlockSpec((1, d), lambda i: (0, 0)),
            pl.BlockSpec((d, LANES), lambda i: (0, 0)),
            pl.BlockSpec((tm, tm), lambda i: (0, 0)),
        ] + [slab(c) for c in cast],
        out_specs=[pl.BlockSpec((tm, LANES), lambda i: (i, 0)),
                   pl.BlockSpec((SUBLANES, tm), lambda i: (0, i)),
                   pl.BlockSpec((1, LANES), lambda i: (0, 0)),
                   pl.BlockSpec((zrows, d), lambda i: (i, 0))] + [slab(c) for c in cast],
        scratch_shapes=[pltpu.VMEM((1, LANES), F32)],
        compiler_params=_cparams(("arbitrary",)),
        name="router",
    )(x, gain.reshape(1, d), rw_pad, tril, *cast)


def _row_copy(src, dst, s, d, sem):
    return pltpu.make_async_copy(src.at[pl.ds(s, 1)], dst.at[pl.ds(d, 1)], sem)


def _tile_dest(dest, t, td):
    nb = t // td
    dest3 = jnp.concatenate([dest[0].reshape(nb, td), dest[1].reshape(nb, td)], axis=1)
    return dest3.reshape(nb, 1, 2 * td)


def _dispatch_kernel(dest_ref, x_ref, g_ref, xs_in_ref, xs_ref, pk_scr, sem, *, td):
    del xs_in_ref
    pk_scr[...] = _rms(x_ref[...], g_ref[...])

    def issue(r, c):
        _row_copy(pk_scr, xs_ref, r, dest_ref[0, 0, r], sem).start(priority=0)
        _row_copy(pk_scr, xs_ref, r, dest_ref[0, 0, td + r], sem).start(priority=1)
        return c

    lax.fori_loop(0, td, issue, 0, unroll=8)
    for _ in range(2):
        pltpu.make_async_copy(pk_scr, xs_ref.at[pl.ds(0, td)], sem).wait()


def _dispatch(x, gain, dest, xs0, td=512):
    t, d = x.shape
    td = min(td, t)
    n_rows = xs0.shape[0]
    return pl.pallas_call(
        functools.partial(_dispatch_kernel, td=td),
        out_shape=jax.ShapeDtypeStruct((n_rows, d), F32),
        grid=(t // td,),
        in_specs=[
            pl.BlockSpec((1, 1, 2 * td), lambda i: (i, 0, 0), memory_space=pltpu.SMEM),
            pl.BlockSpec((td, d), lambda i: (i, 0)),
            pl.BlockSpec((1, d), lambda i: (0, 0)),
            pl.BlockSpec(memory_space=pl.ANY),
        ],
        out_specs=pl.BlockSpec(memory_space=pl.ANY),
        scratch_shapes=[pltpu.VMEM((td, d), F32), pltpu.SemaphoreType.DMA(())],
        input_output_aliases={3: 0},
        compiler_params=pltpu.CompilerParams(dimension_semantics=("arbitrary",),
                                             vmem_limit_bytes=VMEM_LIMIT_BYTES,
                                             has_side_effects=True),
        name="dispatch",
    )(_tile_dest(dest, t, td), x, gain.reshape(1, d), xs0)


def _combine_kernel(dest_ref, nxt_ref, x_ref, info_ref, g_ref, ys_ref, o_ref, y_scr, sems, *, td):
    i = pl.program_id(0)
    nb = pl.num_programs(0)

    def fetch(idx_ref, slot):
        def issue(r, c):
            _row_copy(ys_ref, y_scr.at[slot, 0], idx_ref[0, 0, r], r,
                      sems.at[slot]).start(priority=0)
            _row_copy(ys_ref, y_scr.at[slot, 1], idx_ref[0, 0, td + r], r,
                      sems.at[slot]).start(priority=1)
            return c

        lax.fori_loop(0, td, issue, 0, unroll=8)

    def finish(slot):
        for j in range(2):
            pltpu.make_async_copy(ys_ref.at[pl.ds(0, td)], y_scr.at[slot, j],
                                  sems.at[slot]).wait()
        p1 = info_ref[:, 4:5]
        p2 = info_ref[:, 5:6]
        x = x_ref[...] + (p1 * y_scr[slot, 0] + p2 * y_scr[slot, 1])
        o_ref[...] = _rms(x, g_ref[...])

    @pl.when(i == 0)
    def _():
        fetch(dest_ref, 0)

    for slot in range(2):
        @pl.when(i % 2 == slot)
        def _(slot=slot):
            @pl.when(i + 1 < nb)
            def _():
                fetch(nxt_ref, 1 - slot)

            finish(slot)


def _combine(x, ys, dest, info, gain, td=256):
    t, d = x.shape
    td = min(td, t)
    nb = t // td
    row = pl.BlockSpec((td, d), lambda i: (i, 0))
    tiled = _tile_dest(dest, t, td)
    return pl.pallas_call(
        functools.partial(_combine_kernel, td=td),
        out_shape=jax.ShapeDtypeStruct((t, d), F32),
        grid=(nb,),
        in_specs=[
            pl.BlockSpec((1, 1, 2 * td), lambda i: (i, 0, 0), memory_space=pltpu.SMEM),
            pl.BlockSpec((1, 1, 2 * td), lambda i: (jnp.minimum(i + 1, nb - 1), 0, 0),
                         memory_space=pltpu.SMEM),
            row,
            pl.BlockSpec((td, LANES), lambda i: (i, 0)),
            pl.BlockSpec((1, d), lambda i: (0, 0)),
            pl.BlockSpec(memory_space=pl.ANY),
        ],
        out_specs=row,
        scratch_shapes=[pltpu.VMEM((2, 2, td, d), F32), pltpu.SemaphoreType.DMA((2,))],
        compiler_params=_cparams(("arbitrary",)),
        name="combine",
    )(tiled, tiled, x, info, gain.reshape(1, d), ys)


def _rows2d(w):
    return w.reshape(-1, w.shape[-1])


def _moe(x, norm_g, rw_pad, experts, final_g, tm_e):
    t, d = x.shape
    n_tiles = (2 * t + N_EXPERTS * (tm_e - 1)) // tm_e
    n_rows = n_tiles * tm_e
    todo = [k for k in range(len(experts["f32"])) if k not in experts["bf16"]]
    info, idx, cnt, xs0, *rounded = _router(x, norm_g, rw_pad, n_rows,
                                            cast=[_rows2d(experts["f32"][k]) for k in todo])
    for k, r in zip(todo, rounded):
        experts["bf16"][k] = r.reshape(experts["f32"][k].shape)
    w1, w3, w2 = (experts["bf16"][k] for k in range(3))
    e1, e2, r1, r2 = idx[0], idx[1], idx[2], idx[3]
    counts = cnt[0, :N_EXPERTS].astype(I32)
    padded = ((counts + tm_e - 1) // tm_e) * tm_e
    ends = jnp.cumsum(padded)
    starts = ends - padded
    dest = jnp.stack([starts[e1] + r1, starts[e2] + r2])
    tile_start = jnp.arange(n_tiles, dtype=I32) * tm_e
    tile_expert = jnp.minimum(
        jnp.sum((tile_start[:, None] >= ends[None, :]).astype(I32), axis=1), N_EXPERTS - 1)
    n_valid = (ends[-1] // tm_e).reshape(1).astype(I32)
    xs = _dispatch(x, norm_g, dest, xs0)
    ys = _ffn(xs, w1, w3, w2, tile_expert.astype(I32), n_valid, tm=tm_e, n_f=1)
    return _combine(x, ys, dest, info, final_g)


def _block_diag(w):
    n, c, d = w.shape
    eye = jnp.eye(n, dtype=w.dtype)
    return (w[:, :, None, :] * eye[:, None, :, None]).reshape(n * c, n * d)


def _cum_matrix():
    t = jnp.arange(CHUNK)
    return (t[None, :] <= t[:, None]).astype(BF16)


def _mixer_params(l, lb, hg_norm_g, conv_w, conv_b, w_ra, b_ra, w_ix, b_ix, lru_lambda, wdtype):
    row = lambda a: a.reshape(1, -1).astype(F32)
    return {
        "lbp": jnp.stack([jnp.log(lb), jnp.log1p(-lb), 1.0 - lb]).astype(F32),
        "hgg": row(hg_norm_g[l]),
        "cw": conv_w[l].astype(F32),
        "cb": row(conv_b[l]),
        "wra": _block_diag(w_ra[l].astype(F32)).astype(wdtype),
        "bra": row(b_ra[l]),
        "wix": _block_diag(w_ix[l].astype(F32)).astype(wdtype),
        "bix": row(b_ix[l]),
        "c8": row(-LRU_C * jax.nn.softplus(-lru_lambda[l].astype(F32))),
        "cm": _cum_matrix(),
    }


def _trunk(x, seq_shape, states, prm):
    bsz, seq = seq_shape
    one = jnp.ones((1,), I32)
    new_hg, new_lru, new_conv = [], [], []
    hg_stack = None if states is None else states[0]
    for l in range(2):
        mp = prm["mixer"][l]
        if states is None:
            ex = prm["experts"]
            todo = [k for k in (l + 1,) if k not in ex["bf16"]]
            x, s1, s2, s3, rounded = _mixer_block(
                x.reshape(bsz, seq, D_MODEL), prm["norm1_g"][l], prm["w_in"], prm["w_out"], l,
                mp, cast=[_rows2d(ex["f32"][k]) for k in todo])
            for k, r in zip(todo, rounded):
                ex["bf16"][k] = r.reshape(ex["f32"][k].shape)
            x = x.reshape(bsz * seq, D_MODEL)
            new_hg.append(s1)
        else:
            z = _linear(x, prm["w_in"], l, gain=prm["norm1_g"][l])
            merged, hg_stack, s2, s3 = _mixer_step(z, states[0], l, states[1][l], states[2][l],
                                                   mp, hg_buf=hg_stack)
            x = _linear(merged, prm["w_out"], l, res=x)
        new_lru.append(s2)
        new_conv.append(s3)
        if l == 0:
            n_tiles = max(x.shape[0] // TILE_ROWS, 1)
            n_f = 1 if prm["ffn_w1"].dtype == BF16 else prm["ffn_w1"].shape[2] // F_CHUNK
            x = _ffn(x, prm["ffn_w1"], prm["ffn_w3"], prm["ffn_w2"], jnp.zeros((n_tiles,), I32),
                     one * n_tiles, gain=prm["norm2_g"][0], res=True, tm=TILE_ROWS, n_f=n_f,
                     ck=F_CHUNK)
        else:
            x = _moe(x, prm["norm2_g"][1], prm["rw_pad"], prm["experts"], prm["final_norm_g"],
                     tm_e=TILE_ROWS if states is None else LANES)
    hg_all = jnp.stack(new_hg) if states is None else hg_stack
    return x, hg_all, jnp.stack(new_lru), jnp.stack(new_conv)


def kernel(x_prompt, x_sample, state_hgrn, state_lru, state_conv, norm1_g, w_in, lower_bounds,
           hg_norm_g, conv_w, conv_b, w_ra, b_ra, w_ix, b_ix, lru_lambda, w_out, norm2_g,
           ffn_w1, ffn_w3, ffn_w2, router_w, moe_w1, moe_w3, moe_w2, final_norm_g):
    lb_all = jnp.cumsum(jax.nn.softmax(lower_bounds.astype(F32), axis=0), axis=0)
    lb_all = lb_all - lb_all[0]
    experts = {
        "experts": {"f32": [moe_w1[0].astype(F32), moe_w3[0].astype(F32),
                            moe_w2[0].astype(F32)], "bf16": {}},
        "rw_pad": jnp.pad(router_w[0].astype(F32), ((0, 0), (0, LANES - N_EXPERTS))),
        "norm1_g": norm1_g, "norm2_g": norm2_g, "final_norm_g": final_norm_g,
    }

    def params(wdtype):
        return dict(
            experts,
            mixer=[_mixer_params(l, lb_all[l], hg_norm_g, conv_w, conv_b, w_ra, b_ra, w_ix, b_ix,
                                 lru_lambda, wdtype) for l in range(2)],
            w_in=w_in.astype(wdtype), w_out=w_out.astype(wdtype),
            ffn_w1=ffn_w1.astype(wdtype), ffn_w3=ffn_w3.astype(wdtype),
            ffn_w2=ffn_w2.astype(wdtype))

    bp, sp, d = x_prompt.shape
    bs = x_sample.shape[0]
    y_p, hg_p, lru_p, conv_p = _trunk(x_prompt.reshape(bp * sp, d), (bp, sp), None, params(BF16))
    y_s, hg_s, lru_s, conv_s = _trunk(x_sample.reshape(bs, d), (bs, 1),
                                      (state_hgrn, state_lru, state_conv), params(F32))
    return (y_p.reshape(bp, sp, d), y_s.reshape(bs, 1, d), hg_p, lru_p, conv_p, hg_s, lru_s,
            conv_s)
```

```python
import functools

import jax
import jax.numpy as jnp
from jax import lax
from jax.experimental import pallas as pl
from jax.experimental.pallas import tpu as pltpu

F32 = jnp.float32
BF16 = jnp.bfloat16
I32 = jnp.int32

D_MODEL = 1024
HG_WIDTH = 512
HG_HEADS = 4
HG_D = 128
LRU_WIDTH = 512
LRU_BLOCKS = 8
LRU_C = 8.0
CONV_K = 4
IN_WIDTH = 4 * HG_WIDTH + 2 * LRU_WIDTH
N_EXPERTS = 8
EPS = 1e-6

CHUNK = 64
DIAG = 1
LANES = 128
SUBLANES = 8
TILE_ROWS = 512
F_CHUNK = 256
VMEM_LIMIT_BYTES = 56 * 1024 * 1024


def _cparams(sem):
    return pltpu.CompilerParams(dimension_semantics=sem, vmem_limit_bytes=VMEM_LIMIT_BYTES)


def _exp_neg(x):
    return jnp.exp2(x * -1.4426950408889634)


def _sigmoid(x):
    return 1.0 / (1.0 + _exp_neg(x))


def _silu(x):
    return x * _sigmoid(x)


def _gelu_tanh(x):
    c = 0.7978845608028654
    return 0.5 * x * (1.0 + jnp.tanh(c * (x + 0.044715 * (x * x * x))))


def _rms(x, g):
    return x * lax.rsqrt(jnp.mean(x * x, axis=-1, keepdims=True) + EPS) * g


def _dot(a, b):
    if b.dtype == F32:
        return jnp.dot(a.astype(F32), b, preferred_element_type=F32,
                       precision=lax.Precision.HIGHEST)
    return jnp.dot(a.astype(BF16), b, preferred_element_type=F32)


def _dot_nt(a, b):
    return lax.dot_general(a, b, (((1,), (1,)), ((), ())), preferred_element_type=F32)


def _dot_tn(a, b):
    return lax.dot_general(a, b, (((0,), (0,)), ((), ())), preferred_element_type=F32)


def _split3(x):
    x1 = x.astype(BF16)
    r1 = x - x1.astype(F32)
    x2 = r1.astype(BF16)
    x3 = (r1 - x2.astype(F32)).astype(BF16)
    return x1, x2, x3


def _linear_kernel(*refs, has_gain, has_res):
    it = iter(refs)
    x_ref = next(it)
    g_ref = next(it) if has_gain else None
    w_ref = next(it)
    r_ref = next(it) if has_res else None
    o_ref = next(it)
    h_scr = next(it)

    @pl.when(pl.program_id(1) == 0)
    def _():
        x = x_ref[...].astype(F32)
        if has_gain:
            x = _rms(x, g_ref[...])
        h_scr[...] = x.astype(h_scr.dtype)

    acc = _dot(h_scr[...], w_ref[...])
    if has_res:
        acc = acc + r_ref[...]
    o_ref[...] = acc.astype(o_ref.dtype)


def _linear(x, w, layer, gain=None, res=None, tm=1024, tn=1024):
    m, k = x.shape
    n = w.shape[2]
    tm = min(tm, m)
    tn = min(tn, n)
    in_specs = [pl.BlockSpec((tm, k), lambda i, j: (i, 0))]
    args = [x]
    if gain is not None:
        in_specs.append(pl.BlockSpec((1, k), lambda i, j: (0, 0)))
        args.append(gain.reshape(1, k))
    in_specs.append(pl.BlockSpec((None, k, tn), lambda i, j: (layer, 0, j)))
    args.append(w)
    if res is not None:
        in_specs.append(pl.BlockSpec((tm, tn), lambda i, j: (i, j)))
        args.append(res)
    return pl.pallas_call(
        functools.partial(_linear_kernel, has_gain=gain is not None, has_res=res is not None),
        out_shape=jax.ShapeDtypeStruct((m, n), F32),
        grid=(m // tm, n // tn),
        in_specs=in_specs,
        out_specs=pl.BlockSpec((tm, tn), lambda i, j: (i, j)),
        scratch_shapes=[pltpu.VMEM((tm, k), w.dtype)],
        compiler_params=_cparams(("parallel", "arbitrary")),
        name="linear",
    )(*args)


def _ffn_kernel(te_ref, nv_ref, *refs, has_gain, has_res, tf, ck):
    del te_ref
    it = iter(refs)
    x_ref = next(it)
    g_ref = next(it) if has_gain else None
    w1_ref, w3_ref, w2_ref, o_ref, xb_scr, a_scr, acc_scr = it
    i = pl.program_id(0)
    j = pl.program_id(1)
    nj = pl.num_programs(1)
    valid = i < nv_ref[0]

    @pl.when(valid)
    def _():
        @pl.when(j == 0)
        def _():
            x = x_ref[...]
            if has_gain:
                x = _rms(x, g_ref[...])
            xb_scr[...] = x.astype(xb_scr.dtype)

        xb = xb_scr[...]
        for c in range(tf // ck):
            h1 = _dot(xb, w1_ref[0, :, c * ck:(c + 1) * ck])
            h3 = _dot(xb, w3_ref[0, :, c * ck:(c + 1) * ck])
            a_scr[:, c * ck:(c + 1) * ck] = (_silu(h1) * h3).astype(a_scr.dtype)
        part = _dot(a_scr[...], w2_ref[0])

        @pl.when(j == 0)
        def _():
            acc_scr[...] = part

        @pl.when(j > 0)
        def _():
            acc_scr[...] += part

        @pl.when(j == nj - 1)
        def _():
            out = acc_scr[...]
            if has_res:
                out = x_ref[...] + out
            o_ref[...] = out

    @pl.when(jnp.logical_and(jnp.logical_not(valid), j == nj - 1))
    def _():
        o_ref[...] = jnp.zeros_like(o_ref)


def _ffn(x, w1, w3, w2, tile_expert, n_valid, gain=None, res=False, tm=TILE_ROWS, n_f=2,
         ck=F_CHUNK):
    p, d = x.shape
    f = w1.shape[2]
    tm = min(tm, p)
    tf = f // n_f
    n_tiles = p // tm
    in_specs = [pl.BlockSpec((tm, d), lambda i, j, te, nv: (i, 0))]
    args = [x]
    if gain is not None:
        in_specs.append(pl.BlockSpec((1, d), lambda i, j, te, nv: (0, 0)))
        args.append(gain.reshape(1, d))
    in_specs += [
        pl.BlockSpec((1, d, tf), lambda i, j, te, nv: (te[i], 0, j)),
        pl.BlockSpec((1, d, tf), lambda i, j, te, nv: (te[i], 0, j)),
        pl.BlockSpec((1, tf, d), lambda i, j, te, nv: (te[i], j, 0)),
    ]
    args += [w1, w3, w2]
    grid_spec = pltpu.PrefetchScalarGridSpec(
        num_scalar_prefetch=2,
        grid=(n_tiles, n_f),
        in_specs=in_specs,
        out_specs=pl.BlockSpec((tm, d), lambda i, j, te, nv: (i, 0)),
        scratch_shapes=[pltpu.VMEM((tm, d), w1.dtype), pltpu.VMEM((tm, tf), w1.dtype),
                        pltpu.VMEM((tm, d), F32)],
    )
    return pl.pallas_call(
        functools.partial(_ffn_kernel, has_gain=gain is not None, has_res=res, tf=tf, ck=ck),
        out_shape=jax.ShapeDtypeStruct((p, d), F32),
        grid_spec=grid_spec,
        compiler_params=_cparams(("arbitrary", "arbitrary")),
        name="swiglu",
    )(tile_expert, n_valid, *args)


def _hgrn_gates(fz, loglb, l1mlb, oneml):
    e = _exp_neg(jnp.abs(fz))
    ope = 1.0 + e
    log_sig = jnp.minimum(fz, 0.0) - jnp.log(ope)
    cc = l1mlb + log_sig
    log_f = jnp.maximum(loglb, cc) + jnp.log(1.0 + _exp_neg(jnp.abs(loglb - cc)))
    k = oneml * (jnp.where(fz > 0.0, e, 1.0) / ope)
    return log_f, k


def _level_reference(b, w):
    n = b.shape[0]
    if w >= 4:
        return jnp.concatenate(
            [jnp.broadcast_to(b[j + w:j + w + 1, :], (2 * w, b.shape[1]))
             for j in range(0, n, 2 * w)], axis=0)
    b3 = b.reshape(n // SUBLANES, SUBLANES, b.shape[1])
    ahead = lambda s: pltpu.roll(b3, SUBLANES - s, 1)
    pos = lax.broadcasted_iota(I32, (1, SUBLANES, 1), 1) & (2 * w - 1)
    if w == 2:
        r3 = jnp.where(pos == 0, ahead(2),
                       jnp.where(pos == 1, ahead(1),
                                 jnp.where(pos == 2, b3, pltpu.roll(b3, 1, 1))))
    else:
        r3 = jnp.where(pos == 0, ahead(1), b3)
    return r3.reshape(b.shape)


def _lru_gates(xc, wra, bra, wix, bix, c8):
    r = _sigmoid(_dot(xc, wra) + bra)
    ig = _sigmoid(_dot(xc, wix) + bix)
    log_a = c8 * r
    a = jnp.exp(log_a)
    one_m_a2 = -jnp.tanh(log_a) * (a * a + 1.0)
    u = jnp.sqrt(one_m_a2) * (ig * xc)
    return a, u


def _mixer_block_kernel(x_ref, g1_ref, win_ref, lbp_ref, hgg_ref, cw_ref, cb_ref, wra_ref,
                        bra_ref, wix_ref, bix_ref, c8_ref, cm_ref, wout_ref, *rest, tl, n_cast):
    cast_src = rest[:n_cast]
    xo_ref, hs_ref, ls_ref, cs_ref = rest[n_cast:n_cast + 4]
    cast_dst = rest[n_cast + 4:2 * n_cast + 4]
    (st_scr, h_scr, xpad_scr, hb_scr, q_scr, lf_scr, k_scr, v_scr, g_scr, gl_scr, a_scr, u_scr,
     m_scr) = rest[2 * n_cast + 4:]
    l = pl.program_id(1)
    nl = pl.num_programs(1)
    for src, dst in zip(cast_src, cast_dst):
        dst[...] = src[...].astype(dst.dtype)

    @pl.when(l == 0)
    def _():
        st_scr[...] = jnp.zeros_like(st_scr)
        h_scr[...] = jnp.zeros_like(h_scr)
        xpad_scr[0:8, :] = jnp.zeros((8, LRU_WIDTH), F32)

    hb_scr[...] = _rms(x_ref[0], g1_ref[...]).astype(BF16)
    hb = hb_scr[...]
    zg = lambda j: _dot(hb, win_ref[:, j * HG_WIDTH:(j + 1) * HG_WIDTH])
    q_scr[...] = _silu(zg(0))
    log_f, kk = _hgrn_gates(zg(1), lbp_ref[0:1, :], lbp_ref[1:2, :], lbp_ref[2:3, :])
    lf_scr[...] = log_f
    k_scr[...] = kk
    v_scr[...] = zg(2)
    g_scr[...] = _silu(zg(3))
    xpad_scr[8:8 + tl, :] = zg(4)
    gl_scr[...] = _gelu_tanh(zg(5))
    xc = cb_ref[...]
    for j in range(CONV_K):
        xc = xc + xpad_scr[5 + j:5 + j + tl, :] * cw_ref[j:j + 1, :]
    tail = xpad_scr[tl + 5:tl + 8, :]
    xpad_scr[5:8, :] = tail
    a, u = _lru_gates(xc, wra_ref[...], bra_ref[...], wix_ref[...], bix_ref[...], c8_ref[...])
    a_scr[...] = a
    u_scr[...] = u

    row = lax.broadcasted_iota(I32, (CHUNK, 1), 0)
    ti = lax.broadcasted_iota(I32, (CHUNK, CHUNK), 0)
    si = lax.broadcasted_iota(I32, (CHUNK, CHUNK), 1)
    t3 = lax.broadcasted_iota(I32, (1, DIAG, 1), 1)
    sub3 = lax.broadcasted_iota(I32, (1, SUBLANES, 1), 1)
    nblk = CHUNK // DIAG
    ngrp = CHUNK // SUBLANES
    levels = []
    w = CHUNK // 2
    while w >= DIAG:
        sh = (2 * w).bit_length() - 1
        pairs = ((ti >> sh) == (si >> sh)) & ((ti & (2 * w - 1)) >= w) & ((si & (2 * w - 1)) < w)
        levels.append((w, (row & (2 * w - 1)) >= w, pairs))
        w //= 2

    def chunk_body(c, carry):
        r0 = pl.multiple_of(c * CHUNK, CHUNK)
        rows = pl.ds(r0, CHUNK)

        for gi in range(LRU_WIDTH // LANES):
            cs = slice(gi * LANES, (gi + 1) * LANES)
            aa = a_scr[rows, cs].reshape(ngrp, SUBLANES, LANES)
            uu = u_scr[rows, cs].reshape(ngrp, SUBLANES, LANES)
            d = 1
            while d < SUBLANES:
                keep = sub3 >= d
                a_sh = pltpu.roll(aa, d, 1)
                u_sh = pltpu.roll(uu, d, 1)
                uu = jnp.where(keep, aa * u_sh + uu, uu)
                aa = jnp.where(keep, aa * a_sh, aa)
                d *= 2
            hprev = h_scr[:, cs]
            groups = []
            for j in range(ngrp):
                hj = uu[j] + aa[j] * hprev
                groups.append(hj)
                hprev = hj[SUBLANES - 1:SUBLANES, :]
            hs = jnp.concatenate(groups, axis=0)
            h_scr[:, cs] = hprev
            m_scr[rows, HG_WIDTH + gi * LANES:HG_WIDTH + (gi + 1) * LANES] = (
                hs * gl_scr[rows, cs]).astype(BF16)

        q = q_scr[rows, :]
        k = k_scr[rows, :]
        v = v_scr[rows, :]
        f1, f2, f3 = _split3(lf_scr[rows, :])
        cm = cm_ref[...]
        b = _dot(cm, f1) + _dot(cm, f2) + _dot(cm, f3)

        for hh in range(HG_HEADS):
            cs = slice(hh * HG_D, (hh + 1) * HG_D)
            qh, kh, vh, bh = q[:, cs], k[:, cs], v[:, cs], b[:, cs]
            vb = vh.astype(BF16)
            amat = jnp.zeros((CHUNK, CHUNK), F32)
            for w, upper, pairs in levels:
                dec = _exp_neg(jnp.abs(bh - _level_reference(bh, w)))
                y = (jnp.where(upper, qh, kh) * dec).astype(BF16)
                amat = jnp.where(pairs, _dot_nt(y, y), amat)
            o = _dot(amat.astype(BF16), vb)
            st = st_scr[hh]
            o = o + _dot_nt((qh * jnp.exp(bh)).astype(BF16), st.astype(BF16))
            if DIAG == 1:
                o = o + jnp.sum(qh * kh, axis=-1, keepdims=True) * vh
            else:
                q3 = qh.reshape(nblk, DIAG, HG_D)
                k3 = kh.reshape(nblk, DIAG, HG_D)
                v3 = vh.reshape(nblk, DIAG, HG_D)
                b3 = bh.reshape(nblk, DIAG, HG_D)
                od = jnp.zeros((nblk, DIAG, HG_D), F32)
                for s in range(DIAG):
                    dec = jnp.exp(jnp.minimum(b3 - b3[:, s:s + 1, :], 0.0))
                    p = q3 * (k3[:, s:s + 1, :] * dec)
                    rs = jnp.sum(p, axis=-1, keepdims=True)
                    rs = jnp.where(t3 >= s, rs, 0.0)
                    od = od + rs * v3[:, s:s + 1, :]
                o = o + od.reshape(CHUNK, HG_D)
            bl = bh[CHUNK - 1:CHUNK, :]
            kdec = (kh * jnp.exp(bl - bh)).astype(BF16)
            st_scr[hh] = st * jnp.exp(bl) + _dot_tn(vb, kdec)
            o = o * lax.rsqrt(jnp.mean(o * o, axis=-1, keepdims=True) + EPS)
            o = o * hgg_ref[:, cs]
            o = o * g_scr[rows, cs]
            m_scr[rows, cs] = o.astype(BF16)
        return carry

    lax.fori_loop(0, tl // CHUNK, chunk_body, 0, unroll=8)
    xo_ref[0] = x_ref[0] + _dot(m_scr[...], wout_ref[...])

    @pl.when(l == nl - 1)
    def _():
        for hh in range(HG_HEADS):
            hs_ref[0, hh] = st_scr[hh].T
        ls_ref[0] = h_scr[...]
        cs_ref[0] = xpad_scr[5:8, :]


def _slab_view(w, steps):
    rows = steps * 2 * SUBLANES
    assert w.size % (rows * LANES) == 0
    return w.reshape(rows, w.size // rows)


def _mixer_steps(x, tl=512):
    return x.shape[0] * (x.shape[1] // min(tl, x.shape[1]))


def _mixer_block(x, gain, w_in, w_out, layer, mp, cast=(), tl=512):
    bsz, seq, d = x.shape
    tl = min(tl, seq)
    nl = seq // tl
    steps = bsz * nl
    cast_dtypes = [dt for _, dt in cast]
    cast = [c for c, _ in cast]
    assert all(c.shape[0] % (steps * 2 * SUBLANES) == 0 for c in cast)
    full = lambda shape: pl.BlockSpec(shape, lambda b, l: (0,) * len(shape))
    stacked = lambda shape: pl.BlockSpec((None,) + shape, lambda b, l: (layer, 0, 0))
    slab = lambda c: pl.BlockSpec((c.shape[0] // steps, c.shape[1]), lambda b, l: (b * nl + l, 0))
    out_shapes = [
        jax.ShapeDtypeStruct((bsz, seq, d), F32),
        jax.ShapeDtypeStruct((bsz, HG_HEADS, HG_D, HG_D), F32),
        jax.ShapeDtypeStruct((bsz, 1, LRU_WIDTH), F32),
        jax.ShapeDtypeStruct((bsz, CONV_K - 1, LRU_WIDTH), F32),
    ] + [jax.ShapeDtypeStruct(c.shape, dt) for c, dt in zip(cast, cast_dtypes)]
    half = lambda dt: pltpu.VMEM((tl, HG_WIDTH), dt)
    xo, hg, lru, conv, *rounded = pl.pallas_call(
        functools.partial(_mixer_block_kernel, tl=tl, n_cast=len(cast)),
        out_shape=out_shapes,
        grid=(bsz, nl),
        in_specs=[
            pl.BlockSpec((1, tl, d), lambda b, l: (b, l, 0)),
            full((1, d)), stacked((d, IN_WIDTH)),
            full((3, HG_WIDTH)), full((1, HG_WIDTH)), full((CONV_K, LRU_WIDTH)),
            full((1, LRU_WIDTH)), full((LRU_WIDTH, LRU_WIDTH)), full((1, LRU_WIDTH)),
            full((LRU_WIDTH, LRU_WIDTH)), full((1, LRU_WIDTH)), full((1, LRU_WIDTH)),
            full(mp["cm"].shape), stacked((d, d)),
        ] + [slab(c) for c in cast],
        out_specs=[
            pl.BlockSpec((1, tl, d), lambda b, l: (b, l, 0)),
            pl.BlockSpec((1, HG_HEADS, HG_D, HG_D), lambda b, l: (b, 0, 0, 0)),
            pl.BlockSpec((1, 1, LRU_WIDTH), lambda b, l: (b, 0, 0)),
            pl.BlockSpec((1, CONV_K - 1, LRU_WIDTH), lambda b, l: (b, 0, 0)),
        ] + [slab(c) for c in cast],
        scratch_shapes=[
            pltpu.VMEM((HG_HEADS, HG_D, HG_D), F32),
            pltpu.VMEM((1, LRU_WIDTH), F32),
            pltpu.VMEM((tl + 8, LRU_WIDTH), F32),
            pltpu.VMEM((tl, d), BF16),
            half(F32), half(F32), half(F32), half(F32), half(F32), half(F32), half(F32),
            half(F32),
            pltpu.VMEM((tl, d), BF16),
        ],
        compiler_params=_cparams(("parallel", "arbitrary")),
        name="mixer_block",
    )(x, gain.reshape(1, d), w_in, mp["lbp"], mp["hgg"], mp["cw"], mp["cb"], mp["wra"],
      mp["bra"], mp["wix"], mp["bix"], mp["c8"], mp["cm"], w_out, *cast)
    return xo, hg, lru.reshape(bsz, LRU_WIDTH), conv, rounded


def _mixer_step_kernel(z_ref, sh_ref, sl_ref, sc_ref, lbp_ref, hgg_ref, cw_ref, cb_ref, wra_ref,
                       bra_ref, wix_ref, bix_ref, c8_ref, *rest, bb, aliased):
    m_ref, hn_ref, ln_ref, cn_ref, o_scr = rest[1:] if aliased else rest
    z = z_ref[...]
    xr = z[:, 4 * HG_WIDTH:4 * HG_WIDTH + LRU_WIDTH]
    gr = z[:, 4 * HG_WIDTH + LRU_WIDTH:IN_WIDTH]
    buf = sc_ref[...]
    xc = cb_ref[...]
    for j in range(CONV_K - 1):
        xc = xc + buf[:, j * LRU_WIDTH:(j + 1) * LRU_WIDTH] * cw_ref[j:j + 1, :]
    xc = xc + xr * cw_ref[CONV_K - 1:CONV_K, :]
    cn_ref[:, 0:(CONV_K - 2) * LRU_WIDTH] = buf[:, LRU_WIDTH:(CONV_K - 1) * LRU_WIDTH]
    cn_ref[:, (CONV_K - 2) * LRU_WIDTH:(CONV_K - 1) * LRU_WIDTH] = xr
    a, u = _lru_gates(xc, wra_ref[...], bra_ref[...], wix_ref[...], bix_ref[...], c8_ref[...])
    hnew = u + a * sl_ref[...]
    ln_ref[...] = hnew
    m_ref[:, HG_WIDTH:D_MODEL] = (hnew * _gelu_tanh(gr)).astype(m_ref.dtype)

    q = _silu(z[:, 0:HG_WIDTH])
    log_f, k = _hgrn_gates(z[:, HG_WIDTH:2 * HG_WIDTH], lbp_ref[0:1, :], lbp_ref[1:2, :],
                           lbp_ref[2:3, :])
    f = jnp.exp(log_f)
    v = z[:, 2 * HG_WIDTH:3 * HG_WIDTH]
    gate = _silu(z[:, 3 * HG_WIDTH:4 * HG_WIDTH])
    zpad = jnp.zeros((HG_D - bb, HG_D), F32)
    for hh in range(HG_HEADS):
        cs = slice(hh * HG_D, (hh + 1) * HG_D)
        ft = jnp.concatenate([f[:, cs], zpad], axis=0).T
        kt = jnp.concatenate([k[:, cs], zpad], axis=0).T
        qt = jnp.concatenate([q[:, cs], zpad], axis=0).T
        for j in range(bb):
            fcol = jnp.broadcast_to(ft[:, j:j + 1], (HG_D, HG_D))
            kcol = jnp.broadcast_to(kt[:, j:j + 1], (HG_D, HG_D))
            qcol = jnp.broadcast_to(qt[:, j:j + 1], (HG_D, HG_D))
            vrow = jnp.broadcast_to(v[j:j + 1, cs], (HG_D, HG_D))
            sn = fcol * sh_ref[j, hh] + kcol * vrow
            hn_ref[j, hh] = sn
            o_scr[j:j + 1, cs] = jnp.sum(qcol * sn, axis=0, keepdims=True)
    for hh in range(HG_HEADS):
        cs = slice(hh * HG_D, (hh + 1) * HG_D)
        o = o_scr[:, cs]
        o = o * lax.rsqrt(jnp.mean(o * o, axis=-1, keepdims=True) + EPS)
        o = o * hgg_ref[:, cs]
        o = o * gate[:, cs]
        m_ref[:, cs] = o.astype(m_ref.dtype)


def _mixer_step(z, s_hg_all, layer, s_lru, s_conv, mp, hg_buf=None, bb=16):
    bsz = z.shape[0]
    bb = min(bb, bsz)
    cw3 = (CONV_K - 1) * LRU_WIDTH
    full = lambda shape: pl.BlockSpec(shape, lambda i: (0,) * len(shape))
    state_spec = pl.BlockSpec((None, bb, HG_HEADS, HG_D, HG_D), lambda i: (layer, i, 0, 0, 0))
    out_shapes = (
        jax.ShapeDtypeStruct((bsz, D_MODEL), mp["wra"].dtype),
        jax.ShapeDtypeStruct(s_hg_all.shape, F32),
        jax.ShapeDtypeStruct((bsz, LRU_WIDTH), F32),
        jax.ShapeDtypeStruct((bsz, cw3), F32),
    )
    in_specs = [
        pl.BlockSpec((bb, IN_WIDTH), lambda i: (i, 0)),
        state_spec,
        pl.BlockSpec((bb, LRU_WIDTH), lambda i: (i, 0)),
        pl.BlockSpec((bb, cw3), lambda i: (i, 0)),
        full((3, HG_WIDTH)), full((1, HG_WIDTH)), full((CONV_K, LRU_WIDTH)),
        full((1, LRU_WIDTH)), full((LRU_WIDTH, LRU_WIDTH)), full((1, LRU_WIDTH)),
        full((LRU_WIDTH, LRU_WIDTH)), full((1, LRU_WIDTH)), full((1, LRU_WIDTH)),
    ]
    args = [z, s_hg_all, s_lru, s_conv.reshape(bsz, cw3), mp["lbp"], mp["hgg"], mp["cw"],
            mp["cb"], mp["wra"], mp["bra"], mp["wix"], mp["bix"], mp["c8"]]
    aliases = {}
    if hg_buf is not None:
        aliases = {len(args): 1}
        in_specs.append(pl.BlockSpec(memory_space=pl.ANY))
        args.append(hg_buf)
    merged, hg, lru, conv = pl.pallas_call(
        functools.partial(_mixer_step_kernel, bb=bb, aliased=hg_buf is not None),
        out_shape=out_shapes,
        grid=(bsz // bb,),
        in_specs=in_specs,
        out_specs=(
            pl.BlockSpec((bb, D_MODEL), lambda i: (i, 0)),
            state_spec,
            pl.BlockSpec((bb, LRU_WIDTH), lambda i: (i, 0)),
            pl.BlockSpec((bb, cw3), lambda i: (i, 0)),
        ),
        scratch_shapes=[pltpu.VMEM((bb, HG_WIDTH), F32)],
        input_output_aliases=aliases,
        compiler_params=_cparams(("parallel",)),
        name="mixer_step",
    )(*args)
    return merged, hg, lru, conv.reshape(bsz, CONV_K - 1, LRU_WIDTH)


def _router_kernel(x_ref, g_ref, rw_ref, tril_ref, *refs, n_cast):
    cast_src = refs[:n_cast]
    info_ref, idx_ref, cnt_ref, zero_ref = refs[n_cast:n_cast + 4]
    cast_dst = refs[n_cast + 4:2 * n_cast + 4]
    carry_scr = refs[-1]
    i = pl.program_id(0)

    @pl.when(i == 0)
    def _():
        carry_scr[...] = jnp.zeros_like(carry_scr)

    zero_ref[...] = jnp.zeros_like(zero_ref)
    for src, dst in zip(cast_src, cast_dst):
        dst[...] = src[...].astype(dst.dtype)

    h = _rms(x_ref[...], g_ref[...])
    logits = jnp.dot(h, rw_ref[...], preferred_element_type=F32, precision=lax.Precision.HIGHEST)
    lane = lax.broadcasted_iota(I32, logits.shape, 1)
    neg = jnp.float32(-jnp.inf)
    logits = jnp.where(lane < N_EXPERTS, logits, neg)
    m1 = jnp.max(logits, axis=-1, keepdims=True)
    i1 = jnp.min(jnp.where(logits == m1, lane, LANES), axis=-1, keepdims=True)
    l2 = jnp.where(lane == i1, neg, logits)
    m2 = jnp.max(l2, axis=-1, keepdims=True)
    i2 = jnp.min(jnp.where(l2 == m2, lane, LANES), axis=-1, keepdims=True)
    e = jnp.exp(m2 - m1)
    p1 = 1.0 / (1.0 + e)
    p2 = e / (1.0 + e)
    oh1 = (lane == i1).astype(F32)
    oh2 = (lane == i2).astype(F32)
    sel = oh1 + oh2
    before = _dot(tril_ref[...], sel.astype(BF16)) + carry_scr[...]
    r1 = jnp.sum(oh1 * before, axis=-1, keepdims=True)
    r2 = jnp.sum(oh2 * before, axis=-1, keepdims=True)
    carry_scr[...] += jnp.sum(sel, axis=0, keepdims=True)
    info = jnp.where(lane == 0, i1.astype(F32), 0.0)
    info = jnp.where(lane == 1, i2.astype(F32), info)
    info = jnp.where(lane == 2, r1, info)
    info = jnp.where(lane == 3, r2, info)
    info = jnp.where(lane == 4, p1, info)
    info = jnp.where(lane == 5, p2, info)
    info_ref[...] = info
    idx_ref[...] = info.T[0:SUBLANES, :].astype(I32)
    cnt_ref[...] = carry_scr[...]


def _router(x, gain, rw_pad, n_rows, cast=(), tm=512):
    t, d = x.shape
    tm = min(tm, t)
    steps = t // tm
    zrows = n_rows // steps
    assert zrows * steps == n_rows and zrows % SUBLANES == 0
    assert all(c.shape[0] % (steps * 2 * SUBLANES) == 0 for c in cast)
    slab = lambda c: pl.BlockSpec((c.shape[0] // steps, c.shape[1]), lambda i: (i, 0))
    tril = jnp.tril(jnp.ones((tm, tm), BF16), -1)
    return pl.pallas_call(
        functools.partial(_router_kernel, n_cast=len(cast)),
        out_shape=[jax.ShapeDtypeStruct((t, LANES), F32),
                   jax.ShapeDtypeStruct((SUBLANES, t), I32),
                   jax.ShapeDtypeStruct((1, LANES), F32),
                   jax.ShapeDtypeStruct((n_rows, d), F32)]
        + [jax.ShapeDtypeStruct(c.shape, BF16) for c in cast],
        grid=(steps,),
        in_specs=[
            pl.BlockSpec((tm, d), lambda i: (i, 0)),
            pl.BlockSpec((1, d), lambda i: (0, 0)),
            pl.BlockSpec((d, LANES), lambda i: (0, 0)),
            pl.BlockSpec((tm, tm), lambda i: (0, 0)),
        ] + [slab(c) for c in cast],
        out_specs=[pl.BlockSpec((tm, LANES), lambda i: (i, 0)),
                   pl.BlockSpec((SUBLANES, tm), lambda i: (0, i)),
                   pl.BlockSpec((1, LANES), lambda i: (0, 0)),
                   pl.BlockSpec((zrows, d), lambda i: (i, 0))] + [slab(c) for c in cast],
        scratch_shapes=[pltpu.VMEM((1, LANES), F32)],
        compiler_params=_cparams(("arbitrary",)),
        name="router",
    )(x, gain.reshape(1, d), rw_pad, tril, *cast)


def _row_copy(src, dst, s, d, sem):
    return pltpu.make_async_copy(src.at[pl.ds(s, 1)], dst.at[pl.ds(d, 1)], sem)


def _tile_dest(dest, t, td):
    nb = t // td
    dest3 = jnp.concatenate([dest[0].reshape(nb, td), dest[1].reshape(nb, td)], axis=1)
    return dest3.reshape(nb, 1, 2 * td)


def _dispatch_kernel(dest_ref, x_ref, g_ref, xs_in_ref, xs_ref, pk_scr, sem, *, td):
    del xs_in_ref
    pk_scr[...] = _rms(x_ref[...], g_ref[...])

    def issue(r, c):
        _row_copy(pk_scr, xs_ref, r, dest_ref[0, 0, r], sem).start(priority=0)
        _row_copy(pk_scr, xs_ref, r, dest_ref[0, 0, td + r], sem).start(priority=1)
        return c

    lax.fori_loop(0, td, issue, 0, unroll=8)
    for _ in range(2):
        pltpu.make_async_copy(pk_scr, xs_ref.at[pl.ds(0, td)], sem).wait()


def _dispatch(x, gain, dest, xs0, td=512):
    t, d = x.shape
    td = min(td, t)
    n_rows = xs0.shape[0]
    return pl.pallas_call(
        functools.partial(_dispatch_kernel, td=td),
        out_shape=jax.ShapeDtypeStruct((n_rows, d), F32),
        grid=(t // td,),
        in_specs=[
            pl.BlockSpec((1, 1, 2 * td), lambda i: (i, 0, 0), memory_space=pltpu.SMEM),
            pl.BlockSpec((td, d), lambda i: (i, 0)),
            pl.BlockSpec((1, d), lambda i: (0, 0)),
            pl.BlockSpec(memory_space=pl.ANY),
        ],
        out_specs=pl.BlockSpec(memory_space=pl.ANY),
        scratch_shapes=[pltpu.VMEM((td, d), F32), pltpu.SemaphoreType.DMA(())],
        input_output_aliases={3: 0},
        compiler_params=pltpu.CompilerParams(dimension_semantics=("arbitrary",),
                                             vmem_limit_bytes=VMEM_LIMIT_BYTES,
                                             has_side_effects=True),
        name="dispatch",
    )(_tile_dest(dest, t, td), x, gain.reshape(1, d), xs0)


def _combine_kernel(dest_ref, nxt_ref, x_ref, info_ref, g_ref, ys_ref, o_ref, y_scr, sems, *, td):
    i = pl.program_id(0)
    nb = pl.num_programs(0)

    def fetch(idx_ref, slot):
        def issue(r, c):
            _row_copy(ys_ref, y_scr.at[slot, 0], idx_ref[0, 0, r], r,
                      sems.at[slot]).start(priority=0)
            _row_copy(ys_ref, y_scr.at[slot, 1], idx_ref[0, 0, td + r], r,
                      sems.at[slot]).start(priority=1)
            return c

        lax.fori_loop(0, td, issue, 0, unroll=8)

    def finish(slot):
        for j in range(2):
            pltpu.make_async_copy(ys_ref.at[pl.ds(0, td)], y_scr.at[slot, j],
                                  sems.at[slot]).wait()
        p1 = info_ref[:, 4:5]
        p2 = info_ref[:, 5:6]
        x = x_ref[...] + (p1 * y_scr[slot, 0] + p2 * y_scr[slot, 1])
        o_ref[...] = _rms(x, g_ref[...])

    @pl.when(i == 0)
    def _():
        fetch(dest_ref, 0)

    for slot in range(2):
        @pl.when(i % 2 == slot)
        def _(slot=slot):
            @pl.when(i + 1 < nb)
            def _():
                fetch(nxt_ref, 1 - slot)

            finish(slot)


def _combine(x, ys, dest, info, gain, td=256):
    t, d = x.shape
    td = min(td, t)
    nb = t // td
    row = pl.BlockSpec((td, d), lambda i: (i, 0))
    tiled = _tile_dest(dest, t, td)
    return pl.pallas_call(
        functools.partial(_combine_kernel, td=td),
        out_shape=jax.ShapeDtypeStruct((t, d), F32),
        grid=(nb,),
        in_specs=[
            pl.BlockSpec((1, 1, 2 * td), lambda i: (i, 0, 0), memory_space=pltpu.SMEM),
            pl.BlockSpec((1, 1, 2 * td), lambda i: (jnp.minimum(i + 1, nb - 1), 0, 0),
                         memory_space=pltpu.SMEM),
            row,
            pl.BlockSpec((td, LANES), lambda i: (i, 0)),
            pl.BlockSpec((1, d), lambda i: (0, 0)),
            pl.BlockSpec(memory_space=pl.ANY),
        ],
        out_specs=row,
        scratch_shapes=[pltpu.VMEM((2, 2, td, d), F32), pltpu.SemaphoreType.DMA((2,))],
        compiler_params=_cparams(("arbitrary",)),
        name="combine",
    )(tiled, tiled, x, info, gain.reshape(1, d), ys)


def _rows2d(w):
    return w.reshape(-1, w.shape[-1])


def _moe(x, norm_g, rw_pad, experts, shared, final_g, tm_e):
    t, d = x.shape
    n_tiles = (2 * t + N_EXPERTS * (tm_e - 1)) // tm_e
    n_rows = n_tiles * tm_e
    todo = [(name, w) for name, w in experts if name not in shared]
    info, idx, cnt, xs0, *rounded = _router(x, norm_g, rw_pad, n_rows,
                                            cast=[_rows2d(w) for _, w in todo])
    for (name, w), r in zip(todo, rounded):
        shared[name] = r.reshape(w.shape)
    w1, w3, w2 = (shared[name] for name, _ in experts)
    e1, e2, r1, r2 = idx[0], idx[1], idx[2], idx[3]
    counts = cnt[0, :N_EXPERTS].astype(I32)
    padded = ((counts + tm_e - 1) // tm_e) * tm_e
    ends = jnp.cumsum(padded)
    starts = ends - padded
    dest = jnp.stack([starts[e1] + r1, starts[e2] + r2])
    tile_start = jnp.arange(n_tiles, dtype=I32) * tm_e
    tile_expert = jnp.minimum(
        jnp.sum((tile_start[:, None] >= ends[None, :]).astype(I32), axis=1), N_EXPERTS - 1)
    n_valid = (ends[-1] // tm_e).reshape(1).astype(I32)
    xs = _dispatch(x, norm_g, dest, xs0)
    ys = _ffn(xs, w1, w3, w2, tile_expert.astype(I32), n_valid, tm=tm_e, n_f=1)
    return _combine(x, ys, dest, info, final_g)


def _block_diag(w):
    n, c, d = w.shape
    eye = jnp.eye(n, dtype=w.dtype)
    return (w[:, :, None, :] * eye[:, None, :, None]).reshape(n * c, n * d)


def _cum_matrix():
    t = jnp.arange(CHUNK)
    return (t[None, :] <= t[:, None]).astype(BF16)


def _mixer_params(l, lb, hg_norm_g, conv_w, conv_b, w_ra, b_ra, w_ix, b_ix, lru_lambda, wdtype):
    row = lambda a: a.reshape(1, -1).astype(F32)
    return {
        "lbp": jnp.stack([jnp.log(lb), jnp.log1p(-lb), 1.0 - lb]).astype(F32),
        "hgg": row(hg_norm_g[l]),
        "cw": conv_w[l].astype(F32),
        "cb": row(conv_b[l]),
        "wra": _block_diag(w_ra[l].astype(F32)).astype(wdtype),
        "bra": row(b_ra[l]),
        "wix": _block_diag(w_ix[l].astype(F32)).astype(wdtype),
        "bix": row(b_ix[l]),
        "c8": row(-LRU_C * jax.nn.softplus(-lru_lambda[l].astype(F32))),
        "cm": _cum_matrix(),
    }


def _trunk(x, seq_shape, states, prm):
    bsz, seq = seq_shape
    one = jnp.ones((1,), I32)
    new_hg, new_lru, new_conv = [], [], []
    shared = prm["shared"]
    hg_stack = None if states is None else shared.get("hg_seed", states[0])
    for l in range(2):
        mp = prm["mixer"][l]
        if states is None:
            x3 = x.reshape(bsz, seq, D_MODEL)
            jobs = [j for j in prm["side"][l] if j[0] not in shared]
            steps = _mixer_steps(x3)
            if l == 0:
                w_in, w_out, layer = prm["w_in"], prm["w_out"], 0
            else:
                w_in, w_out, layer = shared["w_in"], shared["w_out"], 1
            x, s1, s2, s3, done = _mixer_block(
                x3, prm["norm1_g"][l], w_in, w_out, layer, mp,
                cast=[(_slab_view(a, steps), dt) for _, a, dt in jobs])
            for (name, a, _), r in zip(jobs, done):
                shared[name] = r.reshape(a.shape)
            x = x.reshape(bsz * seq, D_MODEL)
            new_hg.append(s1)
        else:
            z = _linear(x, prm["w_in"], l, gain=prm["norm1_g"][l])
            merged, hg_stack, s2, s3 = _mixer_step(z, states[0], l, states[1][l], states[2][l],
                                                   mp, hg_buf=hg_stack)
            x = _linear(merged, prm["w_out"], l, res=x)
        new_lru.append(s2)
        new_conv.append(s3)
        if l == 0:
            n_tiles = max(x.shape[0] // TILE_ROWS, 1)
            if states is None:
                f1, f3, f2 = shared["ffn_w1"], shared["ffn_w3"], shared["ffn_w2"]
            else:
                f1, f3, f2 = prm["ffn_w1"], prm["ffn_w3"], prm["ffn_w2"]
            n_f = 1 if f1.dtype == BF16 else f1.shape[2] // F_CHUNK
            x = _ffn(x, f1, f3, f2, jnp.zeros((n_tiles,), I32), one * n_tiles,
                     gain=prm["norm2_g"][0], res=True, tm=TILE_ROWS, n_f=n_f, ck=F_CHUNK)
        else:
            x = _moe(x, prm["norm2_g"][1], prm["rw_pad"], prm["experts"], shared,
                     prm["final_norm_g"], tm_e=TILE_ROWS if states is None else LANES)
    hg_all = jnp.stack(new_hg) if states is None else hg_stack
    return x, hg_all, jnp.stack(new_lru), jnp.stack(new_conv)


def kernel(x_prompt, x_sample, state_hgrn, state_lru, state_conv, norm1_g, w_in, lower_bounds,
           hg_norm_g, conv_w, conv_b, w_ra, b_ra, w_ix, b_ix, lru_lambda, w_out, norm2_g,
           ffn_w1, ffn_w3, ffn_w2, router_w, moe_w1, moe_w3, moe_w2, final_norm_g):
    lb_all = jnp.cumsum(jax.nn.softmax(lower_bounds.astype(F32), axis=0), axis=0)
    lb_all = lb_all - lb_all[0]
    f32 = lambda a: a.astype(F32)
    common = {
        "experts": [("moe_w1", f32(moe_w1[0])), ("moe_w3", f32(moe_w3[0])),
                    ("moe_w2", f32(moe_w2[0]))],
        "side": {0: [("moe_w3", f32(moe_w3[0]), BF16), ("ffn_w1", f32(ffn_w1), BF16),
                     ("ffn_w3", f32(ffn_w3), BF16), ("ffn_w2", f32(ffn_w2), BF16),
                     ("w_in", f32(w_in), BF16), ("w_out", f32(w_out), BF16)],
                 1: [("moe_w2", f32(moe_w2[0]), BF16), ("hg_seed", f32(state_hgrn), F32)]},
        "shared": {},
        "rw_pad": jnp.pad(router_w[0].astype(F32), ((0, 0), (0, LANES - N_EXPERTS))),
        "norm1_g": norm1_g, "norm2_g": norm2_g, "final_norm_g": final_norm_g,
    }
    mixer = lambda wdtype: [_mixer_params(l, lb_all[l], hg_norm_g, conv_w, conv_b, w_ra, b_ra,
                                          w_ix, b_ix, lru_lambda, wdtype) for l in range(2)]
    prompt = dict(common, mixer=mixer(BF16), w_in=w_in[:1].astype(BF16),
                  w_out=w_out[:1].astype(BF16))

    bp, sp, d = x_prompt.shape
    bs = x_sample.shape[0]
    y_p, hg_p, lru_p, conv_p = _trunk(x_prompt.reshape(bp * sp, d), (bp, sp), None, prompt)
    sample = dict(common, mixer=mixer(F32), w_in=f32(w_in), w_out=f32(w_out),
                  ffn_w1=f32(ffn_w1), ffn_w3=f32(ffn_w3), ffn_w2=f32(ffn_w2))
    y_s, hg_s, lru_s, conv_s = _trunk(x_sample.reshape(bs, d), (bs, 1),
                                      (state_hgrn, state_lru, state_conv), sample)
    return (y_p.reshape(bp, sp, d), y_s.reshape(bs, 1, d), hg_p, lru_p, conv_p, hg_s, lru_s,
            conv_s)
```

```python
import functools

import jax
import jax.numpy as jnp
from jax import lax
from jax.experimental import pallas as pl
from jax.experimental.pallas import tpu as pltpu

F32 = jnp.float32
BF16 = jnp.bfloat16
I32 = jnp.int32

D_MODEL = 1024
HG_WIDTH = 512
HG_HEADS = 4
HG_D = 128
LRU_WIDTH = 512
LRU_BLOCKS = 8
LRU_C = 8.0
CONV_K = 4
IN_WIDTH = 4 * HG_WIDTH + 2 * LRU_WIDTH
N_EXPERTS = 8
EPS = 1e-6

CHUNK = 64
DIAG = 1
LANES = 128
SUBLANES = 8
TILE_ROWS = 512
F_CHUNK = 256
VMEM_LIMIT_BYTES = 56 * 1024 * 1024


def _cparams(sem):
    return pltpu.CompilerParams(dimension_semantics=sem, vmem_limit_bytes=VMEM_LIMIT_BYTES)


def _exp_neg(x):
    return jnp.exp2(x * -1.4426950408889634)


def _sigmoid(x):
    return 1.0 / (1.0 + _exp_neg(x))


def _silu(x):
    return x * _sigmoid(x)


def _gelu_tanh(x):
    c = 0.7978845608028654
    return 0.5 * x * (1.0 + jnp.tanh(c * (x + 0.044715 * (x * x * x))))


def _rms(x, g):
    return x * lax.rsqrt(jnp.mean(x * x, axis=-1, keepdims=True) + EPS) * g


def _dot(a, b):
    if b.dtype == F32:
        return jnp.dot(a.astype(F32), b, preferred_element_type=F32,
                       precision=lax.Precision.HIGHEST)
    return jnp.dot(a.astype(BF16), b, preferred_element_type=F32)


def _dot_nt(a, b):
    return lax.dot_general(a, b, (((1,), (1,)), ((), ())), preferred_element_type=F32)


def _dot_tn(a, b):
    return lax.dot_general(a, b, (((0,), (0,)), ((), ())), preferred_element_type=F32)


def _split3(x):
    x1 = x.astype(BF16)
    r1 = x - x1.astype(F32)
    x2 = r1.astype(BF16)
    x3 = (r1 - x2.astype(F32)).astype(BF16)
    return x1, x2, x3


def _linear_kernel(*refs, has_gain, has_res):
    it = iter(refs)
    x_ref = next(it)
    g_ref = next(it) if has_gain else None
    w_ref = next(it)
    r_ref = next(it) if has_res else None
    o_ref = next(it)
    h_scr = next(it)

    @pl.when(pl.program_id(1) == 0)
    def _():
        x = x_ref[...].astype(F32)
        if has_gain:
            x = _rms(x, g_ref[...])
        h_scr[...] = x.astype(h_scr.dtype)

    acc = _dot(h_scr[...], w_ref[...])
    if has_res:
        acc = acc + r_ref[...]
    o_ref[...] = acc.astype(o_ref.dtype)


def _linear(x, w, layer, gain=None, res=None, tm=1024, tn=1024):
    m, k = x.shape
    n = w.shape[2]
    tm = min(tm, m)
    tn = min(tn, n)
    in_specs = [pl.BlockSpec((tm, k), lambda i, j: (i, 0))]
    args = [x]
    if gain is not None:
        in_specs.append(pl.BlockSpec((1, k), lambda i, j: (0, 0)))
        args.append(gain.reshape(1, k))
    in_specs.append(pl.BlockSpec((None, k, tn), lambda i, j: (layer, 0, j)))
    args.append(w)
    if res is not None:
        in_specs.append(pl.BlockSpec((tm, tn), lambda i, j: (i, j)))
        args.append(res)
    return pl.pallas_call(
        functools.partial(_linear_kernel, has_gain=gain is not None, has_res=res is not None),
        out_shape=jax.ShapeDtypeStruct((m, n), F32),
        grid=(m // tm, n // tn),
        in_specs=in_specs,
        out_specs=pl.BlockSpec((tm, tn), lambda i, j: (i, j)),
        scratch_shapes=[pltpu.VMEM((tm, k), w.dtype)],
        compiler_params=_cparams(("parallel", "arbitrary")),
        name="linear",
    )(*args)


def _ffn_kernel(te_ref, nv_ref, *refs, has_gain, has_res, tf, ck):
    del te_ref
    it = iter(refs)
    x_ref = next(it)
    g_ref = next(it) if has_gain else None
    w1_ref, w3_ref, w2_ref, o_ref, xb_scr, a_scr, acc_scr = it
    i = pl.program_id(0)
    j = pl.program_id(1)
    nj = pl.num_programs(1)
    valid = i < nv_ref[0]

    @pl.when(valid)
    def _():
        @pl.when(j == 0)
        def _():
            x = x_ref[...]
            if has_gain:
                x = _rms(x, g_ref[...])
            xb_scr[...] = x.astype(xb_scr.dtype)

        xb = xb_scr[...]
        for c in range(tf // ck):
            h1 = _dot(xb, w1_ref[0, :, c * ck:(c + 1) * ck])
            h3 = _dot(xb, w3_ref[0, :, c * ck:(c + 1) * ck])
            a_scr[:, c * ck:(c + 1) * ck] = (_silu(h1) * h3).astype(a_scr.dtype)
        part = _dot(a_scr[...], w2_ref[0])

        @pl.when(j == 0)
        def _():
            acc_scr[...] = part

        @pl.when(j > 0)
        def _():
            acc_scr[...] += part

        @pl.when(j == nj - 1)
        def _():
            out = acc_scr[...]
            if has_res:
                out = x_ref[...] + out
            o_ref[...] = out

    @pl.when(jnp.logical_and(jnp.logical_not(valid), j == nj - 1))
    def _():
        o_ref[...] = jnp.zeros_like(o_ref)


def _ffn(x, w1, w3, w2, tile_expert, n_valid, gain=None, res=False, tm=TILE_ROWS, n_f=2,
         ck=F_CHUNK):
    p, d = x.shape
    f = w1.shape[2]
    tm = min(tm, p)
    tf = f // n_f
    n_tiles = p // tm
    in_specs = [pl.BlockSpec((tm, d), lambda i, j, te, nv: (i, 0))]
    args = [x]
    if gain is not None:
        in_specs.append(pl.BlockSpec((1, d), lambda i, j, te, nv: (0, 0)))
        args.append(gain.reshape(1, d))
    in_specs += [
        pl.BlockSpec((1, d, tf), lambda i, j, te, nv: (te[i], 0, j)),
        pl.BlockSpec((1, d, tf), lambda i, j, te, nv: (te[i], 0, j)),
        pl.BlockSpec((1, tf, d), lambda i, j, te, nv: (te[i], j, 0)),
    ]
    args += [w1, w3, w2]
    grid_spec = pltpu.PrefetchScalarGridSpec(
        num_scalar_prefetch=2,
        grid=(n_tiles, n_f),
        in_specs=in_specs,
        out_specs=pl.BlockSpec((tm, d), lambda i, j, te, nv: (i, 0)),
        scratch_shapes=[pltpu.VMEM((tm, d), w1.dtype), pltpu.VMEM((tm, tf), w1.dtype),
                        pltpu.VMEM((tm, d), F32)],
    )
    return pl.pallas_call(
        functools.partial(_ffn_kernel, has_gain=gain is not None, has_res=res, tf=tf, ck=ck),
        out_shape=jax.ShapeDtypeStruct((p, d), F32),
        grid_spec=grid_spec,
        compiler_params=_cparams(("arbitrary", "arbitrary")),
        name="swiglu",
    )(tile_expert, n_valid, *args)


def _hgrn_gates(fz, loglb, l1mlb, oneml):
    e = _exp_neg(jnp.abs(fz))
    ope = 1.0 + e
    log_sig = jnp.minimum(fz, 0.0) - jnp.log(ope)
    cc = l1mlb + log_sig
    log_f = jnp.maximum(loglb, cc) + jnp.log(1.0 + _exp_neg(jnp.abs(loglb - cc)))
    k = oneml * (jnp.where(fz > 0.0, e, 1.0) / ope)
    return log_f, k


def _level_reference(b, w):
    n = b.shape[0]
    if w >= 4:
        return jnp.concatenate(
            [jnp.broadcast_to(b[j + w:j + w + 1, :], (2 * w, b.shape[1]))
             for j in range(0, n, 2 * w)], axis=0)
    b3 = b.reshape(n // SUBLANES, SUBLANES, b.shape[1])
    ahead = lambda s: pltpu.roll(b3, SUBLANES - s, 1)
    pos = lax.broadcasted_iota(I32, (1, SUBLANES, 1), 1) & (2 * w - 1)
    if w == 2:
        r3 = jnp.where(pos == 0, ahead(2),
                       jnp.where(pos == 1, ahead(1),
                                 jnp.where(pos == 2, b3, pltpu.roll(b3, 1, 1))))
    else:
        r3 = jnp.where(pos == 0, ahead(1), b3)
    return r3.reshape(b.shape)


def _lru_gates(xc, wra, bra, wix, bix, c8):
    r = _sigmoid(_dot(xc, wra) + bra)
    ig = _sigmoid(_dot(xc, wix) + bix)
    log_a = c8 * r
    a = jnp.exp(log_a)
    one_m_a2 = -jnp.tanh(log_a) * (a * a + 1.0)
    u = jnp.sqrt(one_m_a2) * (ig * xc)
    return a, u


def _mixer_block_kernel(x_ref, g1_ref, win_ref, lbp_ref, hgg_ref, cw_ref, cb_ref, wra_ref,
                        bra_ref, wix_ref, bix_ref, c8_ref, cm_ref, wout_ref, *rest, tl, n_cast):
    cast_src = rest[:n_cast]
    xo_ref, hs_ref, ls_ref, cs_ref = rest[n_cast:n_cast + 4]
    cast_dst = rest[n_cast + 4:2 * n_cast + 4]
    (st_scr, h_scr, xpad_scr, hb_scr, q_scr, lf_scr, k_scr, v_scr, g_scr, gl_scr, a_scr, u_scr,
     m_scr) = rest[2 * n_cast + 4:]
    l = pl.program_id(1)
    nl = pl.num_programs(1)
    for src, dst in zip(cast_src, cast_dst):
        dst[...] = src[...].astype(dst.dtype)

    @pl.when(l == 0)
    def _():
        st_scr[...] = jnp.zeros_like(st_scr)
        h_scr[...] = jnp.zeros_like(h_scr)
        xpad_scr[0:8, :] = jnp.zeros((8, LRU_WIDTH), F32)

    hb_scr[...] = _rms(x_ref[0], g1_ref[...]).astype(BF16)
    hb = hb_scr[...]
    zg = lambda j: _dot(hb, win_ref[:, j * HG_WIDTH:(j + 1) * HG_WIDTH])
    q_scr[...] = _silu(zg(0))
    log_f, kk = _hgrn_gates(zg(1), lbp_ref[0:1, :], lbp_ref[1:2, :], lbp_ref[2:3, :])
    lf_scr[...] = log_f
    k_scr[...] = kk
    v_scr[...] = zg(2)
    g_scr[...] = _silu(zg(3))
    xpad_scr[8:8 + tl, :] = zg(4)
    gl_scr[...] = _gelu_tanh(zg(5))
    xc = cb_ref[...]
    for j in range(CONV_K):
        xc = xc + xpad_scr[5 + j:5 + j + tl, :] * cw_ref[j:j + 1, :]
    tail = xpad_scr[tl + 5:tl + 8, :]
    xpad_scr[5:8, :] = tail
    a, u = _lru_gates(xc, wra_ref[...], bra_ref[...], wix_ref[...], bix_ref[...], c8_ref[...])
    a_scr[...] = a
    u_scr[...] = u

    row = lax.broadcasted_iota(I32, (CHUNK, 1), 0)
    ti = lax.broadcasted_iota(I32, (CHUNK, CHUNK), 0)
    si = lax.broadcasted_iota(I32, (CHUNK, CHUNK), 1)
    t3 = lax.broadcasted_iota(I32, (1, DIAG, 1), 1)
    sub3 = lax.broadcasted_iota(I32, (1, SUBLANES, 1), 1)
    nblk = CHUNK // DIAG
    ngrp = CHUNK // SUBLANES
    levels = []
    w = CHUNK // 2
    while w >= DIAG:
        sh = (2 * w).bit_length() - 1
        pairs = ((ti >> sh) == (si >> sh)) & ((ti & (2 * w - 1)) >= w) & ((si & (2 * w - 1)) < w)
        levels.append((w, (row & (2 * w - 1)) >= w, pairs))
        w //= 2

    def chunk_body(c, carry):
        r0 = pl.multiple_of(c * CHUNK, CHUNK)
        rows = pl.ds(r0, CHUNK)

        for gi in range(LRU_WIDTH // LANES):
            cs = slice(gi * LANES, (gi + 1) * LANES)
            aa = a_scr[rows, cs].reshape(ngrp, SUBLANES, LANES)
            uu = u_scr[rows, cs].reshape(ngrp, SUBLANES, LANES)
            d = 1
            while d < SUBLANES:
                keep = sub3 >= d
                a_sh = pltpu.roll(aa, d, 1)
                u_sh = pltpu.roll(uu, d, 1)
                uu = jnp.where(keep, aa * u_sh + uu, uu)
                aa = jnp.where(keep, aa * a_sh, aa)
                d *= 2
            hprev = h_scr[:, cs]
            groups = []
            for j in range(ngrp):
                hj = uu[j] + aa[j] * hprev
                groups.append(hj)
                hprev = hj[SUBLANES - 1:SUBLANES, :]
            hs = jnp.concatenate(groups, axis=0)
            h_scr[:, cs] = hprev
            m_scr[rows, HG_WIDTH + gi * LANES:HG_WIDTH + (gi + 1) * LANES] = (
                hs * gl_scr[rows, cs]).astype(BF16)

        q = q_scr[rows, :]
        k = k_scr[rows, :]
        v = v_scr[rows, :]
        f1, f2, f3 = _split3(lf_scr[rows, :])
        cm = cm_ref[...]
        b = _dot(cm, f1) + _dot(cm, f2) + _dot(cm, f3)

        for hh in range(HG_HEADS):
            cs = slice(hh * HG_D, (hh + 1) * HG_D)
            qh, kh, vh, bh = q[:, cs], k[:, cs], v[:, cs], b[:, cs]
            vb = vh.astype(BF16)
            amat = jnp.zeros((CHUNK, CHUNK), F32)
            for w, upper, pairs in levels:
                dec = _exp_neg(jnp.abs(bh - _level_reference(bh, w)))
                y = (jnp.where(upper, qh, kh) * dec).astype(BF16)
                amat = jnp.where(pairs, _dot_nt(y, y), amat)
            o = _dot(amat.astype(BF16), vb)
            st = st_scr[hh]
            o = o + _dot_nt((qh * jnp.exp(bh)).astype(BF16), st.astype(BF16))
            if DIAG == 1:
                o = o + jnp.sum(qh * kh, axis=-1, keepdims=True) * vh
            else:
                q3 = qh.reshape(nblk, DIAG, HG_D)
                k3 = kh.reshape(nblk, DIAG, HG_D)
                v3 = vh.reshape(nblk, DIAG, HG_D)
                b3 = bh.reshape(nblk, DIAG, HG_D)
                od = jnp.zeros((nblk, DIAG, HG_D), F32)
                for s in range(DIAG):
                    dec = jnp.exp(jnp.minimum(b3 - b3[:, s:s + 1, :], 0.0))
                    p = q3 * (k3[:, s:s + 1, :] * dec)
                    rs = jnp.sum(p, axis=-1, keepdims=True)
                    rs = jnp.where(t3 >= s, rs, 0.0)
                    od = od + rs * v3[:, s:s + 1, :]
                o = o + od.reshape(CHUNK, HG_D)
            bl = bh[CHUNK - 1:CHUNK, :]
            kdec = (kh * jnp.exp(bl - bh)).astype(BF16)
            st_scr[hh] = st * jnp.exp(bl) + _dot_tn(vb, kdec)
            o = o * lax.rsqrt(jnp.mean(o * o, axis=-1, keepdims=True) + EPS)
            o = o * hgg_ref[:, cs]
            o = o * g_scr[rows, cs]
            m_scr[rows, cs] = o.astype(BF16)
        return carry

    lax.fori_loop(0, tl // CHUNK, chunk_body, 0, unroll=8)
    xo_ref[0] = x_ref[0] + _dot(m_scr[...], wout_ref[...])

    @pl.when(l == nl - 1)
    def _():
        for hh in range(HG_HEADS):
            hs_ref[0, hh] = st_scr[hh].T
        ls_ref[0] = h_scr[...]
        cs_ref[0] = xpad_scr[5:8, :]


def _slab_spec(c, steps, step_of):
    share = 1
    while (c.shape[0] * share) % (steps * 2 * SUBLANES):
        share *= 2
    assert steps % share == 0
    rows = c.shape[0] * share // steps
    return pl.BlockSpec((rows, c.shape[1]), lambda *ids: (step_of(*ids) // share, 0))


def _mixer_block(x, gain, w_in, w_out, layer, mp, cast=(), tl=512):
    bsz, seq, d = x.shape
    tl = min(tl, seq)
    nl = seq // tl
    steps = bsz * nl
    cast_dtypes = [dt for _, dt in cast]
    cast = [c for c, _ in cast]
    full = lambda shape: pl.BlockSpec(shape, lambda b, l: (0,) * len(shape))
    stacked = lambda shape: pl.BlockSpec((None,) + shape, lambda b, l: (layer, 0, 0))
    slab = lambda c: _slab_spec(c, steps, lambda b, l: b * nl + l)
    out_shapes = [
        jax.ShapeDtypeStruct((bsz, seq, d), F32),
        jax.ShapeDtypeStruct((bsz, HG_HEADS, HG_D, HG_D), F32),
        jax.ShapeDtypeStruct((bsz, 1, LRU_WIDTH), F32),
        jax.ShapeDtypeStruct((bsz, CONV_K - 1, LRU_WIDTH), F32),
    ] + [jax.ShapeDtypeStruct(c.shape, dt) for c, dt in zip(cast, cast_dtypes)]
    half = lambda dt: pltpu.VMEM((tl, HG_WIDTH), dt)
    xo, hg, lru, conv, *rounded = pl.pallas_call(
        functools.partial(_mixer_block_kernel, tl=tl, n_cast=len(cast)),
        out_shape=out_shapes,
        grid=(bsz, nl),
        in_specs=[
            pl.BlockSpec((1, tl, d), lambda b, l: (b, l, 0)),
            full((1, d)), stacked((d, IN_WIDTH)),
            full((3, HG_WIDTH)), full((1, HG_WIDTH)), full((CONV_K, LRU_WIDTH)),
            full((1, LRU_WIDTH)), full((LRU_WIDTH, LRU_WIDTH)), full((1, LRU_WIDTH)),
            full((LRU_WIDTH, LRU_WIDTH)), full((1, LRU_WIDTH)), full((1, LRU_WIDTH)),
            full(mp["cm"].shape), stacked((d, d)),
        ] + [slab(c) for c in cast],
        out_specs=[
            pl.BlockSpec((1, tl, d), lambda b, l: (b, l, 0)),
            pl.BlockSpec((1, HG_HEADS, HG_D, HG_D), lambda b, l: (b, 0, 0, 0)),
            pl.BlockSpec((1, 1, LRU_WIDTH), lambda b, l: (b, 0, 0)),
            pl.BlockSpec((1, CONV_K - 1, LRU_WIDTH), lambda b, l: (b, 0, 0)),
        ] + [slab(c) for c in cast],
        scratch_shapes=[
            pltpu.VMEM((HG_HEADS, HG_D, HG_D), F32),
            pltpu.VMEM((1, LRU_WIDTH), F32),
            pltpu.VMEM((tl + 8, LRU_WIDTH), F32),
            pltpu.VMEM((tl, d), BF16),
            half(F32), half(F32), half(F32), half(F32), half(F32), half(F32), half(F32),
            half(F32),
            pltpu.VMEM((tl, d), BF16),
        ],
        compiler_params=_cparams(("parallel", "arbitrary")),
        name="mixer_block",
    )(x, gain.reshape(1, d), w_in, mp["lbp"], mp["hgg"], mp["cw"], mp["cb"], mp["wra"],
      mp["bra"], mp["wix"], mp["bix"], mp["c8"], mp["cm"], w_out, *cast)
    return xo, hg, lru.reshape(bsz, LRU_WIDTH), conv, rounded


def _mixer_step_kernel(z_ref, sh_ref, sl_ref, sc_ref, lbp_ref, hgg_ref, cw_ref, cb_ref, wra_ref,
                       bra_ref, wix_ref, bix_ref, c8_ref, *rest, bb, aliased):
    m_ref, hn_ref, ln_ref, cn_ref, o_scr = rest[1:] if aliased else rest
    z = z_ref[...]
    xr = z[:, 4 * HG_WIDTH:4 * HG_WIDTH + LRU_WIDTH]
    gr = z[:, 4 * HG_WIDTH + LRU_WIDTH:IN_WIDTH]
    buf = sc_ref[...]
    xc = cb_ref[...]
    for j in range(CONV_K - 1):
        xc = xc + buf[:, j * LRU_WIDTH:(j + 1) * LRU_WIDTH] * cw_ref[j:j + 1, :]
    xc = xc + xr * cw_ref[CONV_K - 1:CONV_K, :]
    cn_ref[:, 0:(CONV_K - 2) * LRU_WIDTH] = buf[:, LRU_WIDTH:(CONV_K - 1) * LRU_WIDTH]
    cn_ref[:, (CONV_K - 2) * LRU_WIDTH:(CONV_K - 1) * LRU_WIDTH] = xr
    a, u = _lru_gates(xc, wra_ref[...], bra_ref[...], wix_ref[...], bix_ref[...], c8_ref[...])
    hnew = u + a * sl_ref[...]
    ln_ref[...] = hnew
    m_ref[:, HG_WIDTH:D_MODEL] = (hnew * _gelu_tanh(gr)).astype(m_ref.dtype)

    q = _silu(z[:, 0:HG_WIDTH])
    log_f, k = _hgrn_gates(z[:, HG_WIDTH:2 * HG_WIDTH], lbp_ref[0:1, :], lbp_ref[1:2, :],
                           lbp_ref[2:3, :])
    f = jnp.exp(log_f)
    v = z[:, 2 * HG_WIDTH:3 * HG_WIDTH]
    gate = _silu(z[:, 3 * HG_WIDTH:4 * HG_WIDTH])
    zpad = jnp.zeros((HG_D - bb, HG_D), F32)
    for hh in range(HG_HEADS):
        cs = slice(hh * HG_D, (hh + 1) * HG_D)
        ft = jnp.concatenate([f[:, cs], zpad], axis=0).T
        kt = jnp.concatenate([k[:, cs], zpad], axis=0).T
        qt = jnp.concatenate([q[:, cs], zpad], axis=0).T
        for j in range(bb):
            fcol = jnp.broadcast_to(ft[:, j:j + 1], (HG_D, HG_D))
            kcol = jnp.broadcast_to(kt[:, j:j + 1], (HG_D, HG_D))
            qcol = jnp.broadcast_to(qt[:, j:j + 1], (HG_D, HG_D))
            vrow = jnp.broadcast_to(v[j:j + 1, cs], (HG_D, HG_D))
            sn = fcol * sh_ref[j, hh] + kcol * vrow
            hn_ref[j, hh] = sn
            o_scr[j:j + 1, cs] = jnp.sum(qcol * sn, axis=0, keepdims=True)
    for hh in range(HG_HEADS):
        cs = slice(hh * HG_D, (hh + 1) * HG_D)
        o = o_scr[:, cs]
        o = o * lax.rsqrt(jnp.mean(o * o, axis=-1, keepdims=True) + EPS)
        o = o * hgg_ref[:, cs]
        o = o * gate[:, cs]
        m_ref[:, cs] = o.astype(m_ref.dtype)


def _mixer_step(z, s_hg_all, layer, s_lru, s_conv, mp, hg_buf=None, bb=16):
    bsz = z.shape[0]
    bb = min(bb, bsz)
    cw3 = (CONV_K - 1) * LRU_WIDTH
    full = lambda shape: pl.BlockSpec(shape, lambda i: (0,) * len(shape))
    state_spec = pl.BlockSpec((None, bb, HG_HEADS, HG_D, HG_D), lambda i: (layer, i, 0, 0, 0))
    out_shapes = (
        jax.ShapeDtypeStruct((bsz, D_MODEL), mp["wra"].dtype),
        jax.ShapeDtypeStruct(s_hg_all.shape, F32),
        jax.ShapeDtypeStruct((bsz, LRU_WIDTH), F32),
        jax.ShapeDtypeStruct((bsz, cw3), F32),
    )
    in_specs = [
        pl.BlockSpec((bb, IN_WIDTH), lambda i: (i, 0)),
        state_spec,
        pl.BlockSpec((bb, LRU_WIDTH), lambda i: (i, 0)),
        pl.BlockSpec((bb, cw3), lambda i: (i, 0)),
        full((3, HG_WIDTH)), full((1, HG_WIDTH)), full((CONV_K, LRU_WIDTH)),
        full((1, LRU_WIDTH)), full((LRU_WIDTH, LRU_WIDTH)), full((1, LRU_WIDTH)),
        full((LRU_WIDTH, LRU_WIDTH)), full((1, LRU_WIDTH)), full((1, LRU_WIDTH)),
    ]
    args = [z, s_hg_all, s_lru, s_conv.reshape(bsz, cw3), mp["lbp"], mp["hgg"], mp["cw"],
            mp["cb"], mp["wra"], mp["bra"], mp["wix"], mp["bix"], mp["c8"]]
    aliases = {}
    if hg_buf is not None:
        aliases = {len(args): 1}
        in_specs.append(pl.BlockSpec(memory_space=pl.ANY))
        args.append(hg_buf)
    merged, hg, lru, conv = pl.pallas_call(
        functools.partial(_mixer_step_kernel, bb=bb, aliased=hg_buf is not None),
        out_shape=out_shapes,
        grid=(bsz // bb,),
        in_specs=in_specs,
        out_specs=(
            pl.BlockSpec((bb, D_MODEL), lambda i: (i, 0)),
            state_spec,
            pl.BlockSpec((bb, LRU_WIDTH), lambda i: (i, 0)),
            pl.BlockSpec((bb, cw3), lambda i: (i, 0)),
        ),
        scratch_shapes=[pltpu.VMEM((bb, HG_WIDTH), F32)],
        input_output_aliases=aliases,
        compiler_params=_cparams(("parallel",)),
        name="mixer_step",
    )(*args)
    return merged, hg, lru, conv.reshape(bsz, CONV_K - 1, LRU_WIDTH)


def _router_kernel(x_ref, g_ref, rw_ref, tril_ref, *refs, n_cast):
    cast_src = refs[:n_cast]
    info_ref, idx_ref, cnt_ref, zero_ref = refs[n_cast:n_cast + 4]
    cast_dst = refs[n_cast + 4:2 * n_cast + 4]
    carry_scr = refs[-1]
    i = pl.program_id(0)

    @pl.when(i == 0)
    def _():
        carry_scr[...] = jnp.zeros_like(carry_scr)

    zero_ref[...] = jnp.zeros_like(zero_ref)
    for src, dst in zip(cast_src, cast_dst):
        dst[...] = src[...].astype(dst.dtype)

    h = _rms(x_ref[...], g_ref[...])
    logits = jnp.dot(h, rw_ref[...], preferred_element_type=F32, precision=lax.Precision.HIGHEST)
    lane = lax.broadcasted_iota(I32, logits.shape, 1)
    neg = jnp.float32(-jnp.inf)
    logits = jnp.where(lane < N_EXPERTS, logits, neg)
    m1 = jnp.max(logits, axis=-1, keepdims=True)
    i1 = jnp.min(jnp.where(logits == m1, lane, LANES), axis=-1, keepdims=True)
    l2 = jnp.where(lane == i1, neg, logits)
    m2 = jnp.max(l2, axis=-1, keepdims=True)
    i2 = jnp.min(jnp.where(l2 == m2, lane, LANES), axis=-1, keepdims=True)
    e = jnp.exp(m2 - m1)
    p1 = 1.0 / (1.0 + e)
    p2 = e / (1.0 + e)
    oh1 = (lane == i1).astype(F32)
    oh2 = (lane == i2).astype(F32)
    sel = oh1 + oh2
    before = _dot(tril_ref[...], sel.astype(BF16)) + carry_scr[...]
    r1 = jnp.sum(oh1 * before, axis=-1, keepdims=True)
    r2 = jnp.sum(oh2 * before, axis=-1, keepdims=True)
    carry_scr[...] += jnp.sum(sel, axis=0, keepdims=True)
    info = jnp.where(lane == 0, i1.astype(F32), 0.0)
    info = jnp.where(lane == 1, i2.astype(F32), info)
    info = jnp.where(lane == 2, r1, info)
    info = jnp.where(lane == 3, r2, info)
    info = jnp.where(lane == 4, p1, info)
    info = jnp.where(lane == 5, p2, info)
    info_ref[...] = info
    idx_ref[...] = info.T[0:SUBLANES, :].astype(I32)
    cnt_ref[...] = carry_scr[...]


def _router(x, gain, rw_pad, n_rows, cast=(), tm=512):
    t, d = x.shape
    tm = min(tm, t)
    steps = t // tm
    zrows = n_rows // steps
    assert zrows * steps == n_rows and zrows % SUBLANES == 0
    slab = lambda c: _slab_spec(c, steps, lambda i: i)
    tril = jnp.tril(jnp.ones((tm, tm), BF16), -1)
    return pl.pallas_call(
        functools.partial(_router_kernel, n_cast=len(cast)),
        out_shape=[jax.ShapeDtypeStruct((t, LANES), F32),
                   jax.ShapeDtypeStruct((SUBLANES, t), I32),
                   jax.ShapeDtypeStruct((1, LANES), F32),
                   jax.ShapeDtypeStruct((n_rows, d), F32)]
        + [jax.ShapeDtypeStruct(c.shape, BF16) for c in cast],
        grid=(steps,),
        in_specs=[
            pl.BlockSpec((tm, d), lambda i: (i, 0)),
            pl.BlockSpec((1, d), lambda i: (0, 0)),
            pl.BlockSpec((d, LANES), lambda i: (0, 0)),
            pl.BlockSpec((tm, tm), lambda i: (0, 0)),
        ] + [slab(c) for c in cast],
        out_specs=[pl.BlockSpec((tm, LANES), lambda i: (i, 0)),
                   pl.BlockSpec((SUBLANES, tm), lambda i: (0, i)),
                   pl.BlockSpec((1, LANES), lambda i: (0, 0)),
                   pl.BlockSpec((zrows, d), lambda i: (i, 0))] + [slab(c) for c in cast],
        scratch_shapes=[pltpu.VMEM((1, LANES), F32)],
        compiler_params=_cparams(("arbitrary",)),
        name="router",
    )(x, gain.reshape(1, d), rw_pad, tril, *cast)


def _row_copy(src, dst, s, d, sem):
    return pltpu.make_async_copy(src.at[pl.ds(s, 1)], dst.at[pl.ds(d, 1)], sem)


def _tile_dest(dest, t, td):
    nb = t // td
    dest3 = jnp.concatenate([dest[0].reshape(nb, td), dest[1].reshape(nb, td)], axis=1)
    return dest3.reshape(nb, 1, 2 * td)


def _dispatch_kernel(dest_ref, x_ref, g_ref, xs_in_ref, xs_ref, pk_scr, sem, *, td):
    del xs_in_ref
    pk_scr[...] = _rms(x_ref[...], g_ref[...])

    def issue(r, c):
        _row_copy(pk_scr, xs_ref, r, dest_ref[0, 0, r], sem).start(priority=0)
        _row_copy(pk_scr, xs_ref, r, dest_ref[0, 0, td + r], sem).start(priority=1)
        return c

    lax.fori_loop(0, td, issue, 0, unroll=8)
    for _ in range(2):
        pltpu.make_async_copy(pk_scr, xs_ref.at[pl.ds(0, td)], sem).wait()


def _dispatch(x, gain, dest, xs0, td=512):
    t, d = x.shape
    td = min(td, t)
    n_rows = xs0.shape[0]
    return pl.pallas_call(
        functools.partial(_dispatch_kernel, td=td),
        out_shape=jax.ShapeDtypeStruct((n_rows, d), F32),
        grid=(t // td,),
        in_specs=[
            pl.BlockSpec((1, 1, 2 * td), lambda i: (i, 0, 0), memory_space=pltpu.SMEM),
            pl.BlockSpec((td, d), lambda i: (i, 0)),
            pl.BlockSpec((1, d), lambda i: (0, 0)),
            pl.BlockSpec(memory_space=pl.ANY),
        ],
        out_specs=pl.BlockSpec(memory_space=pl.ANY),
        scratch_shapes=[pltpu.VMEM((td, d), F32), pltpu.SemaphoreType.DMA(())],
        input_output_aliases={3: 0},
        compiler_params=pltpu.CompilerParams(dimension_semantics=("arbitrary",),
                                             vmem_limit_bytes=VMEM_LIMIT_BYTES,
                                             has_side_effects=True),
        name="dispatch",
    )(_tile_dest(dest, t, td), x, gain.reshape(1, d), xs0)


def _combine_kernel(dest_ref, nxt_ref, x_ref, info_ref, g_ref, ys_ref, o_ref, y_scr, sems, *, td):
    i = pl.program_id(0)
    nb = pl.num_programs(0)

    def fetch(idx_ref, slot):
        def issue(r, c):
            _row_copy(ys_ref, y_scr.at[slot, 0], idx_ref[0, 0, r], r,
                      sems.at[slot]).start(priority=0)
            _row_copy(ys_ref, y_scr.at[slot, 1], idx_ref[0, 0, td + r], r,
                      sems.at[slot]).start(priority=1)
            return c

        lax.fori_loop(0, td, issue, 0, unroll=8)

    def finish(slot):
        for j in range(2):
            pltpu.make_async_copy(ys_ref.at[pl.ds(0, td)], y_scr.at[slot, j],
                                  sems.at[slot]).wait()
        p1 = info_ref[:, 4:5]
        p2 = info_ref[:, 5:6]
        x = x_ref[...] + (p1 * y_scr[slot, 0] + p2 * y_scr[slot, 1])
        o_ref[...] = _rms(x, g_ref[...])

    @pl.when(i == 0)
    def _():
        fetch(dest_ref, 0)

    for slot in range(2):
        @pl.when(i % 2 == slot)
        def _(slot=slot):
            @pl.when(i + 1 < nb)
            def _():
                fetch(nxt_ref, 1 - slot)

            finish(slot)


def _combine(x, ys, dest, info, gain, td=256):
    t, d = x.shape
    td = min(td, t)
    nb = t // td
    row = pl.BlockSpec((td, d), lambda i: (i, 0))
    tiled = _tile_dest(dest, t, td)
    return pl.pallas_call(
        functools.partial(_combine_kernel, td=td),
        out_shape=jax.ShapeDtypeStruct((t, d), F32),
        grid=(nb,),
        in_specs=[
            pl.BlockSpec((1, 1, 2 * td), lambda i: (i, 0, 0), memory_space=pltpu.SMEM),
            pl.BlockSpec((1, 1, 2 * td), lambda i: (jnp.minimum(i + 1, nb - 1), 0, 0),
                         memory_space=pltpu.SMEM),
            row,
            pl.BlockSpec((td, LANES), lambda i: (i, 0)),
            pl.BlockSpec((1, d), lambda i: (0, 0)),
            pl.BlockSpec(memory_space=pl.ANY),
        ],
        out_specs=row,
        scratch_shapes=[pltpu.VMEM((2, 2, td, d), F32), pltpu.SemaphoreType.DMA((2,))],
        compiler_params=_cparams(("arbitrary",)),
        name="combine",
    )(tiled, tiled, x, info, gain.reshape(1, d), ys)


def _rows2d(w):
    return w.reshape(-1, w.shape[-1])


def _moe(x, norm_g, rw_pad, experts, shared, final_g, tm_e):
    t, d = x.shape
    n_tiles = (2 * t + N_EXPERTS * (tm_e - 1)) // tm_e
    n_rows = n_tiles * tm_e
    todo = [(name, w) for name, w in experts if name not in shared]
    info, idx, cnt, xs0, *rounded = _router(x, norm_g, rw_pad, n_rows,
                                            cast=[_rows2d(w) for _, w in todo])
    for (name, w), r in zip(todo, rounded):
        shared[name] = r.reshape(w.shape)
    w1, w3, w2 = (shared[name] for name, _ in experts)
    e1, e2, r1, r2 = idx[0], idx[1], idx[2], idx[3]
    counts = cnt[0, :N_EXPERTS].astype(I32)
    padded = ((counts + tm_e - 1) // tm_e) * tm_e
    ends = jnp.cumsum(padded)
    starts = ends - padded
    dest = jnp.stack([starts[e1] + r1, starts[e2] + r2])
    tile_start = jnp.arange(n_tiles, dtype=I32) * tm_e
    tile_expert = jnp.minimum(
        jnp.sum((tile_start[:, None] >= ends[None, :]).astype(I32), axis=1), N_EXPERTS - 1)
    n_valid = (ends[-1] // tm_e).reshape(1).astype(I32)
    xs = _dispatch(x, norm_g, dest, xs0)
    ys = _ffn(xs, w1, w3, w2, tile_expert.astype(I32), n_valid, tm=tm_e, n_f=1)
    return _combine(x, ys, dest, info, final_g)


def _block_diag(w):
    n, c, d = w.shape
    eye = jnp.eye(n, dtype=w.dtype)
    return (w[:, :, None, :] * eye[:, None, :, None]).reshape(n * c, n * d)


def _cum_matrix():
    t = jnp.arange(CHUNK)
    return (t[None, :] <= t[:, None]).astype(BF16)


def _mixer_params(l, lb, hg_norm_g, conv_w, conv_b, w_ra, b_ra, w_ix, b_ix, lru_lambda, wdtype):
    row = lambda a: a.reshape(1, -1).astype(F32)
    return {
        "lbp": jnp.stack([jnp.log(lb), jnp.log1p(-lb), 1.0 - lb]).astype(F32),
        "hgg": row(hg_norm_g[l]),
        "cw": conv_w[l].astype(F32),
        "cb": row(conv_b[l]),
        "wra": _block_diag(w_ra[l].astype(F32)).astype(wdtype),
        "bra": row(b_ra[l]),
        "wix": _block_diag(w_ix[l].astype(F32)).astype(wdtype),
        "bix": row(b_ix[l]),
        "c8": row(-LRU_C * jax.nn.softplus(-lru_lambda[l].astype(F32))),
        "cm": _cum_matrix(),
    }


def _trunk(x, seq_shape, states, prm):
    bsz, seq = seq_shape
    one = jnp.ones((1,), I32)
    new_hg, new_lru, new_conv = [], [], []
    shared = prm["shared"]
    hg_stack = None if states is None else shared.get("hg_seed", states[0])
    for l in range(2):
        mp = prm["mixer"][l]
        if states is None:
            x3 = x.reshape(bsz, seq, D_MODEL)
            jobs = [j for j in prm["side"][l] if j[0] not in shared]
            if l == 0:
                w_in, w_out, layer = prm["w_in"], prm["w_out"], 0
            else:
                w_in, w_out, layer = shared["w_in"], shared["w_out"], 1
            x, s1, s2, s3, done = _mixer_block(
                x3, prm["norm1_g"][l], w_in, w_out, layer, mp,
                cast=[(_rows2d(a), dt) for _, a, dt in jobs])
            for (name, a, _), r in zip(jobs, done):
                shared[name] = r.reshape(a.shape)
            x = x.reshape(bsz * seq, D_MODEL)
            new_hg.append(s1)
        else:
            z = _linear(x, prm["w_in"], l, gain=prm["norm1_g"][l])
            merged, hg_stack, s2, s3 = _mixer_step(z, states[0], l, states[1][l], states[2][l],
                                                   mp, hg_buf=hg_stack)
            x = _linear(merged, prm["w_out"], l, res=x)
        new_lru.append(s2)
        new_conv.append(s3)
        if l == 0:
            n_tiles = max(x.shape[0] // TILE_ROWS, 1)
            if states is None:
                f1, f3, f2 = shared["ffn_w1"], shared["ffn_w3"], shared["ffn_w2"]
            else:
                f1, f3, f2 = prm["ffn_w1"], prm["ffn_w3"], prm["ffn_w2"]
            n_f = 1 if f1.dtype == BF16 else f1.shape[2] // F_CHUNK
            x = _ffn(x, f1, f3, f2, jnp.zeros((n_tiles,), I32), one * n_tiles,
                     gain=prm["norm2_g"][0], res=True, tm=TILE_ROWS, n_f=n_f, ck=F_CHUNK)
        else:
            x = _moe(x, prm["norm2_g"][1], prm["rw_pad"], prm["experts"], shared,
                     prm["final_norm_g"], tm_e=TILE_ROWS if states is None else LANES)
    hg_all = jnp.stack(new_hg) if states is None else hg_stack
    return x, hg_all, jnp.stack(new_lru), jnp.stack(new_conv)


def kernel(x_prompt, x_sample, state_hgrn, state_lru, state_conv, norm1_g, w_in, lower_bounds,
           hg_norm_g, conv_w, conv_b, w_ra, b_ra, w_ix, b_ix, lru_lambda, w_out, norm2_g,
           ffn_w1, ffn_w3, ffn_w2, router_w, moe_w1, moe_w3, moe_w2, final_norm_g):
    lb_all = jnp.cumsum(jax.nn.softmax(lower_bounds.astype(F32), axis=0), axis=0)
    lb_all = lb_all - lb_all[0]
    f32 = lambda a: a.astype(F32)
    common = {
        "experts": [("moe_w1", f32(moe_w1[0])), ("moe_w3", f32(moe_w3[0])),
                    ("moe_w2", f32(moe_w2[0]))],
        "side": {0: [("moe_w3", f32(moe_w3[0]), BF16), ("ffn_w1", f32(ffn_w1), BF16),
                     ("ffn_w3", f32(ffn_w3), BF16), ("ffn_w2", f32(ffn_w2), BF16),
                     ("w_in", f32(w_in), BF16), ("w_out", f32(w_out), BF16)],
                 1: [("moe_w2", f32(moe_w2[0]), BF16), ("hg_seed", f32(state_hgrn), F32)]},
        "shared": {},
        "rw_pad": jnp.pad(router_w[0].astype(F32), ((0, 0), (0, LANES - N_EXPERTS))),
        "norm1_g": norm1_g, "norm2_g": norm2_g, "final_norm_g": final_norm_g,
    }
    mixer = lambda wdtype: [_mixer_params(l, lb_all[l], hg_norm_g, conv_w, conv_b, w_ra, b_ra,
                                          w_ix, b_ix, lru_lambda, wdtype) for l in range(2)]
    prompt = dict(common, mixer=mixer(BF16), w_in=w_in[:1].astype(BF16),
                  w_out=w_out[:1].astype(BF16))

    bp, sp, d = x_prompt.shape
    bs = x_sample.shape[0]
    y_p, hg_p, lru_p, conv_p = _trunk(x_prompt.reshape(bp * sp, d), (bp, sp), None, prompt)
    sample = dict(common, mixer=mixer(F32), w_in=f32(w_in), w_out=f32(w_out),
                  ffn_w1=f32(ffn_w1), ffn_w3=f32(ffn_w3), ffn_w2=f32(ffn_w2))
    y_s, hg_s, lru_s, conv_s = _trunk(x_sample.reshape(bs, d), (bs, 1),
                                      (state_hgrn, state_lru, state_conv), sample)
    return (y_p.reshape(bp, sp, d), y_s.reshape(bs, 1, d), hg_p, lru_p, conv_p, hg_s, lru_s,
            conv_s)
```

```python
import functools

import jax
import jax.numpy as jnp
from jax import lax
from jax.experimental import pallas as pl
from jax.experimental.pallas import tpu as pltpu

F32 = jnp.float32
BF16 = jnp.bfloat16
I32 = jnp.int32

D_MODEL = 1024
HG_WIDTH = 512
HG_HEADS = 4
HG_D = 128
LRU_WIDTH = 512
LRU_BLOCKS = 8
LRU_C = 8.0
CONV_K = 4
IN_WIDTH = 4 * HG_WIDTH + 2 * LRU_WIDTH
N_EXPERTS = 8
EPS = 1e-6

CHUNK = 64
DIAG = 1
LANES = 128
SUBLANES = 8
TILE_ROWS = 512
F_CHUNK = 256
VMEM_LIMIT_BYTES = 56 * 1024 * 1024


def _cparams(sem):
    return pltpu.CompilerParams(dimension_semantics=sem, vmem_limit_bytes=VMEM_LIMIT_BYTES)


def _exp_neg(x):
    return jnp.exp2(x * -1.4426950408889634)


def _sigmoid(x):
    return 1.0 / (1.0 + _exp_neg(x))


def _silu(x):
    return x * _sigmoid(x)


def _gelu_tanh(x):
    c = 0.7978845608028654
    return 0.5 * x * (1.0 + jnp.tanh(c * (x + 0.044715 * (x * x * x))))


def _rms(x, g):
    return x * lax.rsqrt(jnp.mean(x * x, axis=-1, keepdims=True) + EPS) * g


def _dot(a, b):
    if b.dtype == F32:
        return jnp.dot(a.astype(F32), b, preferred_element_type=F32,
                       precision=lax.Precision.HIGHEST)
    return jnp.dot(a.astype(BF16), b, preferred_element_type=F32)


def _dot_nt(a, b):
    return lax.dot_general(a, b, (((1,), (1,)), ((), ())), preferred_element_type=F32)


def _dot_tn(a, b):
    return lax.dot_general(a, b, (((0,), (0,)), ((), ())), preferred_element_type=F32)


def _split3(x):
    x1 = x.astype(BF16)
    r1 = x - x1.astype(F32)
    x2 = r1.astype(BF16)
    x3 = (r1 - x2.astype(F32)).astype(BF16)
    return x1, x2, x3


def _linear_kernel(*refs, has_gain, has_res):
    it = iter(refs)
    x_ref = next(it)
    g_ref = next(it) if has_gain else None
    w_ref = next(it)
    r_ref = next(it) if has_res else None
    o_ref = next(it)
    h_scr = next(it)

    @pl.when(pl.program_id(1) == 0)
    def _():
        x = x_ref[...].astype(F32)
        if has_gain:
            x = _rms(x, g_ref[...])
        h_scr[...] = x.astype(h_scr.dtype)

    acc = _dot(h_scr[...], w_ref[...])
    if has_res:
        acc = acc + r_ref[...]
    o_ref[...] = acc.astype(o_ref.dtype)


def _linear(x, w, layer, gain=None, res=None, tm=1024, tn=1024):
    m, k = x.shape
    n = w.shape[2]
    tm = min(tm, m)
    tn = min(tn, n)
    in_specs = [pl.BlockSpec((tm, k), lambda i, j: (i, 0))]
    args = [x]
    if gain is not None:
        in_specs.append(pl.BlockSpec((1, k), lambda i, j: (0, 0)))
        args.append(gain.reshape(1, k))
    in_specs.append(pl.BlockSpec((None, k, tn), lambda i, j: (layer, 0, j)))
    args.append(w)
    if res is not None:
        in_specs.append(pl.BlockSpec((tm, tn), lambda i, j: (i, j)))
        args.append(res)
    return pl.pallas_call(
        functools.partial(_linear_kernel, has_gain=gain is not None, has_res=res is not None),
        out_shape=jax.ShapeDtypeStruct((m, n), F32),
        grid=(m // tm, n // tn),
        in_specs=in_specs,
        out_specs=pl.BlockSpec((tm, tn), lambda i, j: (i, j)),
        scratch_shapes=[pltpu.VMEM((tm, k), w.dtype)],
        compiler_params=_cparams(("parallel", "arbitrary")),
        name="linear",
    )(*args)


def _ffn_kernel(te_ref, nv_ref, *refs, has_gain, has_res, tf, ck):
    del te_ref
    it = iter(refs)
    x_ref = next(it)
    g_ref = next(it) if has_gain else None
    w1_ref, w3_ref, w2_ref, o_ref, xb_scr, a_scr, acc_scr = it
    i = pl.program_id(0)
    j = pl.program_id(1)
    nj = pl.num_programs(1)
    valid = i < nv_ref[0]

    @pl.when(valid)
    def _():
        @pl.when(j == 0)
        def _():
            x = x_ref[...]
            if has_gain:
                x = _rms(x, g_ref[...])
            xb_scr[...] = x.astype(xb_scr.dtype)

        xb = xb_scr[...]
        for c in range(tf // ck):
            h1 = _dot(xb, w1_ref[0, :, c * ck:(c + 1) * ck])
            h3 = _dot(xb, w3_ref[0, :, c * ck:(c + 1) * ck])
            a_scr[:, c * ck:(c + 1) * ck] = (_silu(h1) * h3).astype(a_scr.dtype)
        part = _dot(a_scr[...], w2_ref[0])

        @pl.when(j == 0)
        def _():
            acc_scr[...] = part

        @pl.when(j > 0)
        def _():
            acc_scr[...] += part

        @pl.when(j == nj - 1)
        def _():
            out = acc_scr[...]
            if has_res:
                out = x_ref[...] + out
            o_ref[...] = out

    @pl.when(jnp.logical_and(jnp.logical_not(valid), j == nj - 1))
    def _():
        o_ref[...] = jnp.zeros_like(o_ref)


def _ffn(x, w1, w3, w2, tile_expert, n_valid, gain=None, res=False, tm=TILE_ROWS, n_f=2,
         ck=F_CHUNK):
    p, d = x.shape
    f = w1.shape[2]
    tm = min(tm, p)
    tf = f // n_f
    n_tiles = p // tm
    in_specs = [pl.BlockSpec((tm, d), lambda i, j, te, nv: (i, 0))]
    args = [x]
    if gain is not None:
        in_specs.append(pl.BlockSpec((1, d), lambda i, j, te, nv: (0, 0)))
        args.append(gain.reshape(1, d))
    in_specs += [
        pl.BlockSpec((1, d, tf), lambda i, j, te, nv: (te[i], 0, j)),
        pl.BlockSpec((1, d, tf), lambda i, j, te, nv: (te[i], 0, j)),
        pl.BlockSpec((1, tf, d), lambda i, j, te, nv: (te[i], j, 0)),
    ]
    args += [w1, w3, w2]
    grid_spec = pltpu.PrefetchScalarGridSpec(
        num_scalar_prefetch=2,
        grid=(n_tiles, n_f),
        in_specs=in_specs,
        out_specs=pl.BlockSpec((tm, d), lambda i, j, te, nv: (i, 0)),
        scratch_shapes=[pltpu.VMEM((tm, d), w1.dtype), pltpu.VMEM((tm, tf), w1.dtype),
                        pltpu.VMEM((tm, d), F32)],
    )
    return pl.pallas_call(
        functools.partial(_ffn_kernel, has_gain=gain is not None, has_res=res, tf=tf, ck=ck),
        out_shape=jax.ShapeDtypeStruct((p, d), F32),
        grid_spec=grid_spec,
        compiler_params=_cparams(("arbitrary", "arbitrary")),
        name="swiglu",
    )(tile_expert, n_valid, *args)


def _hgrn_gates(fz, loglb, l1mlb, oneml):
    e = _exp_neg(jnp.abs(fz))
    ope = 1.0 + e
    log_sig = jnp.minimum(fz, 0.0) - jnp.log(ope)
    cc = l1mlb + log_sig
    log_f = jnp.maximum(loglb, cc) + jnp.log(1.0 + _exp_neg(jnp.abs(loglb - cc)))
    k = oneml * (jnp.where(fz > 0.0, e, 1.0) / ope)
    return log_f, k


def _level_reference(b, w):
    n = b.shape[0]
    if w >= 4:
        return jnp.concatenate(
            [jnp.broadcast_to(b[j + w:j + w + 1, :], (2 * w, b.shape[1]))
             for j in range(0, n, 2 * w)], axis=0)
    b3 = b.reshape(n // SUBLANES, SUBLANES, b.shape[1])
    ahead = lambda s: pltpu.roll(b3, SUBLANES - s, 1)
    pos = lax.broadcasted_iota(I32, (1, SUBLANES, 1), 1) & (2 * w - 1)
    if w == 2:
        r3 = jnp.where(pos == 0, ahead(2),
                       jnp.where(pos == 1, ahead(1),
                                 jnp.where(pos == 2, b3, pltpu.roll(b3, 1, 1))))
    else:
        r3 = jnp.where(pos == 0, ahead(1), b3)
    return r3.reshape(b.shape)


def _lru_gates(xc, wra, bra, wix, bix, c8):
    r = _sigmoid(_dot(xc, wra) + bra)
    ig = _sigmoid(_dot(xc, wix) + bix)
    log_a = c8 * r
    a = jnp.exp(log_a)
    one_m_a2 = -jnp.tanh(log_a) * (a * a + 1.0)
    u = jnp.sqrt(one_m_a2) * (ig * xc)
    return a, u


def _mixer_block_kernel(x_ref, g1_ref, win_ref, lbp_ref, hgg_ref, cw_ref, cb_ref, wra_ref,
                        bra_ref, wix_ref, bix_ref, c8_ref, cm_ref, wout_ref, *rest, tl, n_cast):
    cast_src = rest[:n_cast]
    xo_ref, hs_ref, ls_ref, cs_ref = rest[n_cast:n_cast + 4]
    cast_dst = rest[n_cast + 4:2 * n_cast + 4]
    (st_scr, h_scr, xpad_scr, hb_scr, q_scr, lf_scr, k_scr, v_scr, g_scr, gl_scr, a_scr, u_scr,
     m_scr) = rest[2 * n_cast + 4:]
    l = pl.program_id(1)
    nl = pl.num_programs(1)
    for src, dst in zip(cast_src, cast_dst):
        dst[...] = src[...].astype(dst.dtype)

    @pl.when(l == 0)
    def _():
        st_scr[...] = jnp.zeros_like(st_scr)
        h_scr[...] = jnp.zeros_like(h_scr)
        xpad_scr[0:8, :] = jnp.zeros((8, LRU_WIDTH), F32)

    hb_scr[...] = _rms(x_ref[0], g1_ref[...]).astype(BF16)
    hb = hb_scr[...]
    zg = lambda j: _dot(hb, win_ref[:, j * HG_WIDTH:(j + 1) * HG_WIDTH])
    q_scr[...] = _silu(zg(0))
    log_f, kk = _hgrn_gates(zg(1), lbp_ref[0:1, :], lbp_ref[1:2, :], lbp_ref[2:3, :])
    lf_scr[...] = log_f
    k_scr[...] = kk
    v_scr[...] = zg(2)
    g_scr[...] = _silu(zg(3))
    xpad_scr[8:8 + tl, :] = zg(4)
    gl_scr[...] = _gelu_tanh(zg(5))
    xc = cb_ref[...]
    for j in range(CONV_K):
        xc = xc + xpad_scr[5 + j:5 + j + tl, :] * cw_ref[j:j + 1, :]
    tail = xpad_scr[tl + 5:tl + 8, :]
    xpad_scr[5:8, :] = tail
    a, u = _lru_gates(xc, wra_ref[...], bra_ref[...], wix_ref[...], bix_ref[...], c8_ref[...])
    a_scr[...] = a
    u_scr[...] = u

    row = lax.broadcasted_iota(I32, (CHUNK, 1), 0)
    ti = lax.broadcasted_iota(I32, (CHUNK, CHUNK), 0)
    si = lax.broadcasted_iota(I32, (CHUNK, CHUNK), 1)
    t3 = lax.broadcasted_iota(I32, (1, DIAG, 1), 1)
    sub3 = lax.broadcasted_iota(I32, (1, SUBLANES, 1), 1)
    nblk = CHUNK // DIAG
    ngrp = CHUNK // SUBLANES
    levels = []
    w = CHUNK // 2
    while w >= DIAG:
        sh = (2 * w).bit_length() - 1
        pairs = ((ti >> sh) == (si >> sh)) & ((ti & (2 * w - 1)) >= w) & ((si & (2 * w - 1)) < w)
        levels.append((w, (row & (2 * w - 1)) >= w, pairs))
        w //= 2

    def chunk_body(c, carry):
        r0 = pl.multiple_of(c * CHUNK, CHUNK)
        rows = pl.ds(r0, CHUNK)

        for gi in range(LRU_WIDTH // LANES):
            cs = slice(gi * LANES, (gi + 1) * LANES)
            aa = a_scr[rows, cs].reshape(ngrp, SUBLANES, LANES)
            uu = u_scr[rows, cs].reshape(ngrp, SUBLANES, LANES)
            d = 1
            while d < SUBLANES:
                keep = sub3 >= d
                a_sh = pltpu.roll(aa, d, 1)
                u_sh = pltpu.roll(uu, d, 1)
                uu = jnp.where(keep, aa * u_sh + uu, uu)
                aa = jnp.where(keep, aa * a_sh, aa)
                d *= 2
            hprev = h_scr[:, cs]
            groups = []
            for j in range(ngrp):
                hj = uu[j] + aa[j] * hprev
                groups.append(hj)
                hprev = hj[SUBLANES - 1:SUBLANES, :]
            hs = jnp.concatenate(groups, axis=0)
            h_scr[:, cs] = hprev
            m_scr[rows, HG_WIDTH + gi * LANES:HG_WIDTH + (gi + 1) * LANES] = (
                hs * gl_scr[rows, cs]).astype(BF16)

        q = q_scr[rows, :]
        k = k_scr[rows, :]
        v = v_scr[rows, :]
        f1, f2, f3 = _split3(lf_scr[rows, :])
        cm = cm_ref[...]
        b = _dot(cm, f1) + _dot(cm, f2) + _dot(cm, f3)

        for hh in range(HG_HEADS):
            cs = slice(hh * HG_D, (hh + 1) * HG_D)
            qh, kh, vh, bh = q[:, cs], k[:, cs], v[:, cs], b[:, cs]
            vb = vh.astype(BF16)
            amat = jnp.zeros((CHUNK, CHUNK), F32)
            for w, upper, pairs in levels:
                dec = _exp_neg(jnp.abs(bh - _level_reference(bh, w)))
                y = (jnp.where(upper, qh, kh) * dec).astype(BF16)
                amat = jnp.where(pairs, _dot_nt(y, y), amat)
            o = _dot(amat.astype(BF16), vb)
            st = st_scr[hh]
            o = o + _dot_nt((qh * jnp.exp(bh)).astype(BF16), st.astype(BF16))
            if DIAG == 1:
                o = o + jnp.sum(qh * kh, axis=-1, keepdims=True) * vh
            else:
                q3 = qh.reshape(nblk, DIAG, HG_D)
                k3 = kh.reshape(nblk, DIAG, HG_D)
                v3 = vh.reshape(nblk, DIAG, HG_D)
                b3 = bh.reshape(nblk, DIAG, HG_D)
                od = jnp.zeros((nblk, DIAG, HG_D), F32)
                for s in range(DIAG):
                    dec = jnp.exp(jnp.minimum(b3 - b3[:, s:s + 1, :], 0.0))
                    p = q3 * (k3[:, s:s + 1, :] * dec)
                    rs = jnp.sum(p, axis=-1, keepdims=True)
                    rs = jnp.where(t3 >= s, rs, 0.0)
                    od = od + rs * v3[:, s:s + 1, :]
                o = o + od.reshape(CHUNK, HG_D)
            bl = bh[CHUNK - 1:CHUNK, :]
            kdec = (kh * jnp.exp(bl - bh)).astype(BF16)
            st_scr[hh] = st * jnp.exp(bl) + _dot_tn(vb, kdec)
            o = o * lax.rsqrt(jnp.mean(o * o, axis=-1, keepdims=True) + EPS)
            o = o * hgg_ref[:, cs]
            o = o * g_scr[rows, cs]
            m_scr[rows, cs] = o.astype(BF16)
        return carry

    lax.fori_loop(0, tl // CHUNK, chunk_body, 0, unroll=8)
    xo_ref[0] = x_ref[0] + _dot(m_scr[...], wout_ref[...])

    @pl.when(l == nl - 1)
    def _():
        for hh in range(HG_HEADS):
            hs_ref[0, hh] = st_scr[hh].T
        ls_ref[0] = h_scr[...]
        cs_ref[0] = xpad_scr[5:8, :]


def _slab_spec(c, steps, step_of):
    share = 1
    while (c.shape[0] * share) % (steps * 2 * SUBLANES):
        share *= 2
    assert steps % share == 0
    rows = c.shape[0] * share // steps
    return pl.BlockSpec((rows, c.shape[1]), lambda *ids: (step_of(*ids) // share, 0))


def _mixer_block(x, gain, w_in, w_out, layer, mp, cast=(), tl=512):
    bsz, seq, d = x.shape
    tl = min(tl, seq)
    nl = seq // tl
    steps = bsz * nl
    cast_dtypes = [dt for _, dt in cast]
    cast = [c for c, _ in cast]
    full = lambda shape: pl.BlockSpec(shape, lambda b, l: (0,) * len(shape))
    stacked = lambda shape: pl.BlockSpec((None,) + shape, lambda b, l: (layer, 0, 0))
    slab = lambda c: _slab_spec(c, steps, lambda b, l: b * nl + l)
    out_shapes = [
        jax.ShapeDtypeStruct((bsz, seq, d), F32),
        jax.ShapeDtypeStruct((bsz, HG_HEADS, HG_D, HG_D), F32),
        jax.ShapeDtypeStruct((bsz, 1, LRU_WIDTH), F32),
        jax.ShapeDtypeStruct((bsz, CONV_K - 1, LRU_WIDTH), F32),
    ] + [jax.ShapeDtypeStruct(c.shape, dt) for c, dt in zip(cast, cast_dtypes)]
    half = lambda dt: pltpu.VMEM((tl, HG_WIDTH), dt)
    xo, hg, lru, conv, *rounded = pl.pallas_call(
        functools.partial(_mixer_block_kernel, tl=tl, n_cast=len(cast)),
        out_shape=out_shapes,
        grid=(bsz, nl),
        in_specs=[
            pl.BlockSpec((1, tl, d), lambda b, l: (b, l, 0)),
            full((1, d)), stacked((d, IN_WIDTH)),
            full((3, HG_WIDTH)), full((1, HG_WIDTH)), full((CONV_K, LRU_WIDTH)),
            full((1, LRU_WIDTH)), full((LRU_WIDTH, LRU_WIDTH)), full((1, LRU_WIDTH)),
            full((LRU_WIDTH, LRU_WIDTH)), full((1, LRU_WIDTH)), full((1, LRU_WIDTH)),
            full(mp["cm"].shape), stacked((d, d)),
        ] + [slab(c) for c in cast],
        out_specs=[
            pl.BlockSpec((1, tl, d), lambda b, l: (b, l, 0)),
            pl.BlockSpec((1, HG_HEADS, HG_D, HG_D), lambda b, l: (b, 0, 0, 0)),
            pl.BlockSpec((1, 1, LRU_WIDTH), lambda b, l: (b, 0, 0)),
            pl.BlockSpec((1, CONV_K - 1, LRU_WIDTH), lambda b, l: (b, 0, 0)),
        ] + [slab(c) for c in cast],
        scratch_shapes=[
            pltpu.VMEM((HG_HEADS, HG_D, HG_D), F32),
            pltpu.VMEM((1, LRU_WIDTH), F32),
            pltpu.VMEM((tl + 8, LRU_WIDTH), F32),
            pltpu.VMEM((tl, d), BF16),
            half(F32), half(F32), half(F32), half(F32), half(F32), half(F32), half(F32),
            half(F32),
            pltpu.VMEM((tl, d), BF16),
        ],
        compiler_params=_cparams(("parallel", "arbitrary")),
        name="mixer_block",
    )(x, gain.reshape(1, d), w_in, mp["lbp"], mp["hgg"], mp["cw"], mp["cb"], mp["wra"],
      mp["bra"], mp["wix"], mp["bix"], mp["c8"], mp["cm"], w_out, *cast)
    return xo, hg, lru.reshape(bsz, LRU_WIDTH), conv, rounded


def _mixer_step_kernel(z_ref, sh_ref, sl_ref, sc_ref, lbp_ref, hgg_ref, cw_ref, cb_ref, wra_ref,
                       bra_ref, wix_ref, bix_ref, c8_ref, *rest, bb, aliased):
    m_ref, hn_ref, ln_ref, cn_ref, o_scr = rest[1:] if aliased else rest
    z = z_ref[...]
    xr = z[:, 4 * HG_WIDTH:4 * HG_WIDTH + LRU_WIDTH]
    gr = z[:, 4 * HG_WIDTH + LRU_WIDTH:IN_WIDTH]
    buf = sc_ref[...]
    xc = cb_ref[...]
    for j in range(CONV_K - 1):
        xc = xc + buf[:, j * LRU_WIDTH:(j + 1) * LRU_WIDTH] * cw_ref[j:j + 1, :]
    xc = xc + xr * cw_ref[CONV_K - 1:CONV_K, :]
    cn_ref[:, 0:(CONV_K - 2) * LRU_WIDTH] = buf[:, LRU_WIDTH:(CONV_K - 1) * LRU_WIDTH]
    cn_ref[:, (CONV_K - 2) * LRU_WIDTH:(CONV_K - 1) * LRU_WIDTH] = xr
    a, u = _lru_gates(xc, wra_ref[...], bra_ref[...], wix_ref[...], bix_ref[...], c8_ref[...])
    hnew = u + a * sl_ref[...]
    ln_ref[...] = hnew
    m_ref[:, HG_WIDTH:D_MODEL] = (hnew * _gelu_tanh(gr)).astype(m_ref.dtype)

    q = _silu(z[:, 0:HG_WIDTH])
    log_f, k = _hgrn_gates(z[:, HG_WIDTH:2 * HG_WIDTH], lbp_ref[0:1, :], lbp_ref[1:2, :],
                           lbp_ref[2:3, :])
    f = jnp.exp(log_f)
    v = z[:, 2 * HG_WIDTH:3 * HG_WIDTH]
    gate = _silu(z[:, 3 * HG_WIDTH:4 * HG_WIDTH])
    zpad = jnp.zeros((HG_D - bb, HG_D), F32)
    for hh in range(HG_HEADS):
        cs = slice(hh * HG_D, (hh + 1) * HG_D)
        ft = jnp.concatenate([f[:, cs], zpad], axis=0).T
        kt = jnp.concatenate([k[:, cs], zpad], axis=0).T
        qt = jnp.concatenate([q[:, cs], zpad], axis=0).T
        for j in range(bb):
            fcol = jnp.broadcast_to(ft[:, j:j + 1], (HG_D, HG_D))
            kcol = jnp.broadcast_to(kt[:, j:j + 1], (HG_D, HG_D))
            qcol = jnp.broadcast_to(qt[:, j:j + 1], (HG_D, HG_D))
            vrow = jnp.broadcast_to(v[j:j + 1, cs], (HG_D, HG_D))
            sn = fcol * sh_ref[j, hh] + kcol * vrow
            hn_ref[j, hh] = sn
            o_scr[j:j + 1, cs] = jnp.sum(qcol * sn, axis=0, keepdims=True)
    for hh in range(HG_HEADS):
        cs = slice(hh * HG_D, (hh + 1) * HG_D)
        o = o_scr[:, cs]
        o = o * lax.rsqrt(jnp.mean(o * o, axis=-1, keepdims=True) + EPS)
        o = o * hgg_ref[:, cs]
        o = o * gate[:, cs]
        m_ref[:, cs] = o.astype(m_ref.dtype)


def _mixer_step(z, s_hg_all, layer, s_lru, s_conv, mp, hg_buf=None, bb=16):
    bsz = z.shape[0]
    bb = min(bb, bsz)
    cw3 = (CONV_K - 1) * LRU_WIDTH
    full = lambda shape: pl.BlockSpec(shape, lambda i: (0,) * len(shape))
    state_spec = pl.BlockSpec((None, bb, HG_HEADS, HG_D, HG_D), lambda i: (layer, i, 0, 0, 0))
    out_shapes = (
        jax.ShapeDtypeStruct((bsz, D_MODEL), mp["wra"].dtype),
        jax.ShapeDtypeStruct(s_hg_all.shape, F32),
        jax.ShapeDtypeStruct((bsz, LRU_WIDTH), F32),
        jax.ShapeDtypeStruct((bsz, cw3), F32),
    )
    in_specs = [
        pl.BlockSpec((bb, IN_WIDTH), lambda i: (i, 0)),
        state_spec,
        pl.BlockSpec((bb, LRU_WIDTH), lambda i: (i, 0)),
        pl.BlockSpec((bb, cw3), lambda i: (i, 0)),
        full((3, HG_WIDTH)), full((1, HG_WIDTH)), full((CONV_K, LRU_WIDTH)),
        full((1, LRU_WIDTH)), full((LRU_WIDTH, LRU_WIDTH)), full((1, LRU_WIDTH)),
        full((LRU_WIDTH, LRU_WIDTH)), full((1, LRU_WIDTH)), full((1, LRU_WIDTH)),
    ]
    args = [z, s_hg_all, s_lru, s_conv.reshape(bsz, cw3), mp["lbp"], mp["hgg"], mp["cw"],
            mp["cb"], mp["wra"], mp["bra"], mp["wix"], mp["bix"], mp["c8"]]
    aliases = {}
    if hg_buf is not None:
        aliases = {len(args): 1}
        in_specs.append(pl.BlockSpec(memory_space=pl.ANY))
        args.append(hg_buf)
    merged, hg, lru, conv = pl.pallas_call(
        functools.partial(_mixer_step_kernel, bb=bb, aliased=hg_buf is not None),
        out_shape=out_shapes,
        grid=(bsz // bb,),
        in_specs=in_specs,
        out_specs=(
            pl.BlockSpec((bb, D_MODEL), lambda i: (i, 0)),
            state_spec,
            pl.BlockSpec((bb, LRU_WIDTH), lambda i: (i, 0)),
            pl.BlockSpec((bb, cw3), lambda i: (i, 0)),
        ),
        scratch_shapes=[pltpu.VMEM((bb, HG_WIDTH), F32)],
        input_output_aliases=aliases,
        compiler_params=_cparams(("parallel",)),
        name="mixer_step",
    )(*args)
    return merged, hg, lru, conv.reshape(bsz, CONV_K - 1, LRU_WIDTH)


def _router_kernel(x_ref, g_ref, rw_ref, tril_ref, *refs, n_cast):
    cast_src = refs[:n_cast]
    info_ref, idx_ref, cnt_ref, zero_ref = refs[n_cast:n_cast + 4]
    cast_dst = refs[n_cast + 4:2 * n_cast + 4]
    carry_scr = refs[-1]
    i = pl.program_id(0)

    @pl.when(i == 0)
    def _():
        carry_scr[...] = jnp.zeros_like(carry_scr)

    zero_ref[...] = jnp.zeros_like(zero_ref)
    for src, dst in zip(cast_src, cast_dst):
        dst[...] = src[...].astype(dst.dtype)

    h = _rms(x_ref[...], g_ref[...])
    logits = jnp.dot(h, rw_ref[...], preferred_element_type=F32, precision=lax.Precision.HIGHEST)
    lane = lax.broadcasted_iota(I32, logits.shape, 1)
    neg = jnp.float32(-jnp.inf)
    logits = jnp.where(lane < N_EXPERTS, logits, neg)
    m1 = jnp.max(logits, axis=-1, keepdims=True)
    i1 = jnp.min(jnp.where(logits == m1, lane, LANES), axis=-1, keepdims=True)
    l2 = jnp.where(lane == i1, neg, logits)
    m2 = jnp.max(l2, axis=-1, keepdims=True)
    i2 = jnp.min(jnp.where(l2 == m2, lane, LANES), axis=-1, keepdims=True)
    e = jnp.exp(m2 - m1)
    p1 = 1.0 / (1.0 + e)
    p2 = e / (1.0 + e)
    oh1 = (lane == i1).astype(F32)
    oh2 = (lane == i2).astype(F32)
    sel = oh1 + oh2
    before = _dot(tril_ref[...], sel.astype(BF16)) + carry_scr[...]
    r1 = jnp.sum(oh1 * before, axis=-1, keepdims=True)
    r2 = jnp.sum(oh2 * before, axis=-1, keepdims=True)
    carry_scr[...] += jnp.sum(sel, axis=0, keepdims=True)
    info = jnp.where(lane == 0, i1.astype(F32), 0.0)
    info = jnp.where(lane == 1, i2.astype(F32), info)
    info = jnp.where(lane == 2, r1, info)
    info = jnp.where(lane == 3, r2, info)
    info = jnp.where(lane == 4, p1, info)
    info = jnp.where(lane == 5, p2, info)
    info_ref[...] = info
    idx_ref[...] = info.T[0:SUBLANES, :].astype(I32)
    cnt_ref[...] = carry_scr[...]


def _router(x, gain, rw_pad, n_rows, cast=(), tm=512):
    t, d = x.shape
    tm = min(tm, t)
    steps = t // tm
    zrows = n_rows // steps
    assert zrows * steps == n_rows and zrows % SUBLANES == 0
    slab = lambda c: _slab_spec(c, steps, lambda i: i)
    tril = jnp.tril(jnp.ones((tm, tm), BF16), -1)
    return pl.pallas_call(
        functools.partial(_router_kernel, n_cast=len(cast)),
        out_shape=[jax.ShapeDtypeStruct((t, LANES), F32),
                   jax.ShapeDtypeStruct((SUBLANES, t), I32),
                   jax.ShapeDtypeStruct((1, LANES), F32),
                   jax.ShapeDtypeStruct((n_rows, d), F32)]
        + [jax.ShapeDtypeStruct(c.shape, BF16) for c in cast],
        grid=(steps,),
        in_specs=[
            pl.BlockSpec((tm, d), lambda i: (i, 0)),
            pl.BlockSpec((1, d), lambda i: (0, 0)),
            pl.BlockSpec((d, LANES), lambda i: (0, 0)),
            pl.BlockSpec((tm, tm), lambda i: (0, 0)),
        ] + [slab(c) for c in cast],
        out_specs=[pl.BlockSpec((tm, LANES), lambda i: (i, 0)),
                   pl.BlockSpec((SUBLANES, tm), lambda i: (0, i)),
                   pl.BlockSpec((1, LANES), lambda i: (0, 0)),
                   pl.BlockSpec((zrows, d), lambda i: (i, 0))] + [slab(c) for c in cast],
        scratch_shapes=[pltpu.VMEM((1, LANES), F32)],
        compiler_params=_cparams(("arbitrary",)),
        name="router",
    )(x, gain.reshape(1, d), rw_pad, tril, *cast)


def _row_copy(src, dst, s, d, sem):
    return pltpu.make_async_copy(src.at[pl.ds(s, 1)], dst.at[pl.ds(d, 1)], sem)


def _tile_dest(dest, t, td):
    nb = t // td
    dest3 = jnp.concatenate([dest[0].reshape(nb, td), dest[1].reshape(nb, td)], axis=1)
    return dest3.reshape(nb, 1, 2 * td)


def _dispatch_kernel(dest_ref, x_ref, g_ref, xs_in_ref, xs_ref, pk_scr, sem, *, td):
    del xs_in_ref
    pk_scr[...] = _rms(x_ref[...], g_ref[...])

    def issue(r, c):
        _row_copy(pk_scr, xs_ref, r, dest_ref[0, 0, r], sem).start(priority=0)
        _row_copy(pk_scr, xs_ref, r, dest_ref[0, 0, td + r], sem).start(priority=1)
        return c

    lax.fori_loop(0, td, issue, 0, unroll=8)
    for _ in range(2):
        pltpu.make_async_copy(pk_scr, xs_ref.at[pl.ds(0, td)], sem).wait()


def _dispatch(x, gain, dest, xs0, td=512):
    t, d = x.shape
    td = min(td, t)
    n_rows = xs0.shape[0]
    return pl.pallas_call(
        functools.partial(_dispatch_kernel, td=td),
        out_shape=jax.ShapeDtypeStruct((n_rows, d), F32),
        grid=(t // td,),
        in_specs=[
            pl.BlockSpec((1, 1, 2 * td), lambda i: (i, 0, 0), memory_space=pltpu.SMEM),
            pl.BlockSpec((td, d), lambda i: (i, 0)),
            pl.BlockSpec((1, d), lambda i: (0, 0)),
            pl.BlockSpec(memory_space=pl.ANY),
        ],
        out_specs=pl.BlockSpec(memory_space=pl.ANY),
        scratch_shapes=[pltpu.VMEM((td, d), F32), pltpu.SemaphoreType.DMA(())],
        input_output_aliases={3: 0},
        compiler_params=pltpu.CompilerParams(dimension_semantics=("arbitrary",),
                                             vmem_limit_bytes=VMEM_LIMIT_BYTES,
                                             has_side_effects=True),
        name="dispatch",
    )(_tile_dest(dest, t, td), x, gain.reshape(1, d), xs0)


def _combine_kernel(dest_ref, nxt_ref, x_ref, info_ref, g_ref, ys_ref, o_ref, y_scr, sems, *, td):
    i = pl.program_id(0)
    nb = pl.num_programs(0)

    def fetch(idx_ref, slot):
        def issue(r, c):
            _row_copy(ys_ref, y_scr.at[slot, 0], idx_ref[0, 0, r], r,
                      sems.at[slot]).start(priority=0)
            _row_copy(ys_ref, y_scr.at[slot, 1], idx_ref[0, 0, td + r], r,
                      sems.at[slot]).start(priority=1)
            return c

        lax.fori_loop(0, td, issue, 0, unroll=8)

    def finish(slot):
        for j in range(2):
            pltpu.make_async_copy(ys_ref.at[pl.ds(0, td)], y_scr.at[slot, j],
                                  sems.at[slot]).wait()
        p1 = info_ref[:, 4:5]
        p2 = info_ref[:, 5:6]
        x = x_ref[...] + (p1 * y_scr[slot, 0] + p2 * y_scr[slot, 1])
        o_ref[...] = _rms(x, g_ref[...])

    @pl.when(i == 0)
    def _():
        fetch(dest_ref, 0)

    for slot in range(2):
        @pl.when(i % 2 == slot)
        def _(slot=slot):
            @pl.when(i + 1 < nb)
            def _():
                fetch(nxt_ref, 1 - slot)

            finish(slot)


def _combine(x, ys, dest, info, gain, td=256):
    t, d = x.shape
    td = min(td, t)
    nb = t // td
    row = pl.BlockSpec((td, d), lambda i: (i, 0))
    tiled = _tile_dest(dest, t, td)
    return pl.pallas_call(
        functools.partial(_combine_kernel, td=td),
        out_shape=jax.ShapeDtypeStruct((t, d), F32),
        grid=(nb,),
        in_specs=[
            pl.BlockSpec((1, 1, 2 * td), lambda i: (i, 0, 0), memory_space=pltpu.SMEM),
            pl.BlockSpec((1, 1, 2 * td), lambda i: (jnp.minimum(i + 1, nb - 1), 0, 0),
                         memory_space=pltpu.SMEM),
            row,
            pl.BlockSpec((td, LANES), lambda i: (i, 0)),
            pl.BlockSpec((1, d), lambda i: (0, 0)),
            pl.BlockSpec(memory_space=pl.ANY),
        ],
        out_specs=row,
        scratch_shapes=[pltpu.VMEM((2, 2, td, d), F32), pltpu.SemaphoreType.DMA((2,))],
        compiler_params=_cparams(("arbitrary",)),
        name="combine",
    )(tiled, tiled, x, info, gain.reshape(1, d), ys)


def _rows2d(w):
    return w.reshape(-1, w.shape[-1])


def _moe(xs_in, norm_g, rw_pad, experts, shared, final_g, tm_e=TILE_ROWS):
    d = xs_in[0].shape[1]
    n_tiles = (2 * sum(x.shape[0] for x in xs_in) + N_EXPERTS * (tm_e - 1)) // tm_e
    n_rows = n_tiles * tm_e
    routed = []
    xs0 = None
    for g, x in enumerate(xs_in):
        todo = [(name, w) for name, w in experts if name not in shared]
        steps = x.shape[0] // min(TILE_ROWS, x.shape[0])
        info, idx, cnt, zeros, *rounded = _router(
            x, norm_g, rw_pad, n_rows if g == 0 else steps * SUBLANES,
            cast=[_rows2d(w) for _, w in todo])
        for (name, w), r in zip(todo, rounded):
            shared[name] = r.reshape(w.shape)
        xs0 = zeros if g == 0 else xs0
        routed.append((info, idx, cnt[0, :N_EXPERTS].astype(I32)))
    w1, w3, w2 = (shared[name] for name, _ in experts)
    counts = sum(c for _, _, c in routed)
    padded = ((counts + tm_e - 1) // tm_e) * tm_e
    ends = jnp.cumsum(padded)
    starts = ends - padded
    tile_start = jnp.arange(n_tiles, dtype=I32) * tm_e
    tile_expert = jnp.minimum(
        jnp.sum((tile_start[:, None] >= ends[None, :]).astype(I32), axis=1), N_EXPERTS - 1)
    n_valid = (ends[-1] // tm_e).reshape(1).astype(I32)
    dests = []
    base = starts
    for info, idx, cnt in routed:
        e1, e2, r1, r2 = idx[0], idx[1], idx[2], idx[3]
        dests.append(jnp.stack([base[e1] + r1, base[e2] + r2]))
        base = base + cnt
    rows = xs0
    for x, dest in zip(xs_in, dests):
        rows = _dispatch(x, norm_g, dest, rows)
    ys = _ffn(rows, w1, w3, w2, tile_expert.astype(I32), n_valid, tm=tm_e, n_f=1)
    return [_combine(x, ys, dest, info, final_g)
            for x, dest, (info, _, _) in zip(xs_in, dests, routed)]


def _block_diag(w):
    n, c, d = w.shape
    eye = jnp.eye(n, dtype=w.dtype)
    return (w[:, :, None, :] * eye[:, None, :, None]).reshape(n * c, n * d)


def _cum_matrix():
    t = jnp.arange(CHUNK)
    return (t[None, :] <= t[:, None]).astype(BF16)


def _mixer_params(l, lb, hg_norm_g, conv_w, conv_b, w_ra, b_ra, w_ix, b_ix, lru_lambda, wdtype):
    row = lambda a: a.reshape(1, -1).astype(F32)
    return {
        "lbp": jnp.stack([jnp.log(lb), jnp.log1p(-lb), 1.0 - lb]).astype(F32),
        "hgg": row(hg_norm_g[l]),
        "cw": conv_w[l].astype(F32),
        "cb": row(conv_b[l]),
        "wra": _block_diag(w_ra[l].astype(F32)).astype(wdtype),
        "bra": row(b_ra[l]),
        "wix": _block_diag(w_ix[l].astype(F32)).astype(wdtype),
        "bix": row(b_ix[l]),
        "c8": row(-LRU_C * jax.nn.softplus(-lru_lambda[l].astype(F32))),
        "cm": _cum_matrix(),
    }


def _trunk(x, seq_shape, states, prm):
    bsz, seq = seq_shape
    one = jnp.ones((1,), I32)
    new_hg, new_lru, new_conv = [], [], []
    shared = prm["shared"]
    hg_stack = None if states is None else shared.get("hg_seed", states[0])
    for l in range(2):
        mp = prm["mixer"][l]
        if states is None:
            x3 = x.reshape(bsz, seq, D_MODEL)
            jobs = [j for j in prm["side"][l] if j[0] not in shared]
            if l == 0:
                w_in, w_out, layer = prm["w_in"], prm["w_out"], 0
            else:
                w_in, w_out, layer = shared["w_in"], shared["w_out"], 1
            x, s1, s2, s3, done = _mixer_block(
                x3, prm["norm1_g"][l], w_in, w_out, layer, mp,
                cast=[(_rows2d(a), dt) for _, a, dt in jobs])
            for (name, a, _), r in zip(jobs, done):
                shared[name] = r.reshape(a.shape)
            x = x.reshape(bsz * seq, D_MODEL)
            new_hg.append(s1)
        else:
            z = _linear(x, prm["w_in"], l, gain=prm["norm1_g"][l])
            merged, hg_stack, s2, s3 = _mixer_step(z, states[0], l, states[1][l], states[2][l],
                                                   mp, hg_buf=hg_stack)
            x = _linear(merged, prm["w_out"], l, res=x)
        new_lru.append(s2)
        new_conv.append(s3)
        if l == 0:
            n_tiles = max(x.shape[0] // TILE_ROWS, 1)
            if states is None:
                f1, f3, f2 = shared["ffn_w1"], shared["ffn_w3"], shared["ffn_w2"]
            else:
                f1, f3, f2 = prm["ffn_w1"], prm["ffn_w3"], prm["ffn_w2"]
            n_f = 1 if f1.dtype == BF16 else f1.shape[2] // F_CHUNK
            x = _ffn(x, f1, f3, f2, jnp.zeros((n_tiles,), I32), one * n_tiles,
                     gain=prm["norm2_g"][0], res=True, tm=TILE_ROWS, n_f=n_f, ck=F_CHUNK)
    hg_all = jnp.stack(new_hg) if states is None else hg_stack
    return x, hg_all, jnp.stack(new_lru), jnp.stack(new_conv)


def kernel(x_prompt, x_sample, state_hgrn, state_lru, state_conv, norm1_g, w_in, lower_bounds,
           hg_norm_g, conv_w, conv_b, w_ra, b_ra, w_ix, b_ix, lru_lambda, w_out, norm2_g,
           ffn_w1, ffn_w3, ffn_w2, router_w, moe_w1, moe_w3, moe_w2, final_norm_g):
    lb_all = jnp.cumsum(jax.nn.softmax(lower_bounds.astype(F32), axis=0), axis=0)
    lb_all = lb_all - lb_all[0]
    f32 = lambda a: a.astype(F32)
    common = {
        "experts": [("moe_w1", f32(moe_w1[0])), ("moe_w3", f32(moe_w3[0])),
                    ("moe_w2", f32(moe_w2[0]))],
        "side": {0: [("moe_w3", f32(moe_w3[0]), BF16), ("ffn_w1", f32(ffn_w1), BF16),
                     ("ffn_w3", f32(ffn_w3), BF16), ("ffn_w2", f32(ffn_w2), BF16),
                     ("w_in", f32(w_in), BF16), ("w_out", f32(w_out), BF16)],
                 1: [("moe_w2", f32(moe_w2[0]), BF16), ("hg_seed", f32(state_hgrn), F32)]},
        "shared": {},
        "rw_pad": jnp.pad(router_w[0].astype(F32), ((0, 0), (0, LANES - N_EXPERTS))),
        "norm1_g": norm1_g, "norm2_g": norm2_g, "final_norm_g": final_norm_g,
    }
    mixer = lambda wdtype: [_mixer_params(l, lb_all[l], hg_norm_g, conv_w, conv_b, w_ra, b_ra,
                                          w_ix, b_ix, lru_lambda, wdtype) for l in range(2)]
    prompt = dict(common, mixer=mixer(BF16), w_in=w_in[:1].astype(BF16),
                  w_out=w_out[:1].astype(BF16))

    bp, sp, d = x_prompt.shape
    bs = x_sample.shape[0]
    x_p, hg_p, lru_p, conv_p = _trunk(x_prompt.reshape(bp * sp, d), (bp, sp), None, prompt)
    sample = dict(common, mixer=mixer(F32), w_in=f32(w_in), w_out=f32(w_out),
                  ffn_w1=f32(ffn_w1), ffn_w3=f32(ffn_w3), ffn_w2=f32(ffn_w2))
    x_s, hg_s, lru_s, conv_s = _trunk(x_sample.reshape(bs, d), (bs, 1),
                                      (state_hgrn, state_lru, state_conv), sample)
    y_p, y_s = _moe([x_p, x_s], norm2_g[1], common["rw_pad"], common["experts"],
                    common["shared"], final_norm_g)
    return (y_p.reshape(bp, sp, d), y_s.reshape(bs, 1, d), hg_p, lru_p, conv_p, hg_s, lru_s,
            conv_s)
```

```python
import functools

import jax
import jax.numpy as jnp
from jax import lax
from jax.experimental import pallas as pl
from jax.experimental.pallas import tpu as pltpu

F32 = jnp.float32
BF16 = jnp.bfloat16
I32 = jnp.int32

D_MODEL = 1024
HG_WIDTH = 512
HG_HEADS = 4
HG_D = 128
LRU_WIDTH = 512
LRU_BLOCKS = 8
LRU_C = 8.0
CONV_K = 4
IN_WIDTH = 4 * HG_WIDTH + 2 * LRU_WIDTH
N_EXPERTS = 8
EPS = 1e-6

CHUNK = 128
DIAG = 1
LANES = 128
SUBLANES = 8
TILE_ROWS = 512
F_CHUNK = 256
VMEM_LIMIT_BYTES = 56 * 1024 * 1024


def _cparams(sem):
    return pltpu.CompilerParams(dimension_semantics=sem, vmem_limit_bytes=VMEM_LIMIT_BYTES)


def _exp_neg(x):
    return jnp.exp2(x * -1.4426950408889634)


def _sigmoid(x):
    return 1.0 / (1.0 + _exp_neg(x))


def _silu(x):
    return x * _sigmoid(x)


def _gelu_tanh(x):
    c = 0.7978845608028654
    return 0.5 * x * (1.0 + jnp.tanh(c * (x + 0.044715 * (x * x * x))))


def _rms(x, g):
    return x * lax.rsqrt(jnp.mean(x * x, axis=-1, keepdims=True) + EPS) * g


def _dot(a, b):
    if b.dtype == F32:
        return jnp.dot(a.astype(F32), b, preferred_element_type=F32,
                       precision=lax.Precision.HIGHEST)
    return jnp.dot(a.astype(BF16), b, preferred_element_type=F32)


def _dot_nt(a, b):
    return lax.dot_general(a, b, (((1,), (1,)), ((), ())), preferred_element_type=F32)


def _dot_tn(a, b):
    return lax.dot_general(a, b, (((0,), (0,)), ((), ())), preferred_element_type=F32)


def _split3(x):
    x1 = x.astype(BF16)
    r1 = x - x1.astype(F32)
    x2 = r1.astype(BF16)
    x3 = (r1 - x2.astype(F32)).astype(BF16)
    return x1, x2, x3


def _linear_kernel(*refs, has_gain, has_res):
    it = iter(refs)
    x_ref = next(it)
    g_ref = next(it) if has_gain else None
    w_ref = next(it)
    r_ref = next(it) if has_res else None
    o_ref = next(it)
    h_scr = next(it)

    @pl.when(pl.program_id(1) == 0)
    def _():
        x = x_ref[...].astype(F32)
        if has_gain:
            x = _rms(x, g_ref[...])
        h_scr[...] = x.astype(h_scr.dtype)

    acc = _dot(h_scr[...], w_ref[...])
    if has_res:
        acc = acc + r_ref[...]
    o_ref[...] = acc.astype(o_ref.dtype)


def _linear(x, w, layer, gain=None, res=None, tm=1024, tn=1024):
    m, k = x.shape
    n = w.shape[2]
    tm = min(tm, m)
    tn = min(tn, n)
    in_specs = [pl.BlockSpec((tm, k), lambda i, j: (i, 0))]
    args = [x]
    if gain is not None:
        in_specs.append(pl.BlockSpec((1, k), lambda i, j: (0, 0)))
        args.append(gain.reshape(1, k))
    in_specs.append(pl.BlockSpec((None, k, tn), lambda i, j: (layer, 0, j)))
    args.append(w)
    if res is not None:
        in_specs.append(pl.BlockSpec((tm, tn), lambda i, j: (i, j)))
        args.append(res)
    return pl.pallas_call(
        functools.partial(_linear_kernel, has_gain=gain is not None, has_res=res is not None),
        out_shape=jax.ShapeDtypeStruct((m, n), F32),
        grid=(m // tm, n // tn),
        in_specs=in_specs,
        out_specs=pl.BlockSpec((tm, tn), lambda i, j: (i, j)),
        scratch_shapes=[pltpu.VMEM((tm, k), w.dtype)],
        compiler_params=_cparams(("parallel", "arbitrary")),
        name="linear",
    )(*args)


def _ffn_kernel(te_ref, nv_ref, *refs, has_gain, has_res, tf, ck):
    del te_ref
    it = iter(refs)
    x_ref = next(it)
    g_ref = next(it) if has_gain else None
    w1_ref, w3_ref, w2_ref, o_ref, xb_scr, a_scr, acc_scr = it
    i = pl.program_id(0)
    j = pl.program_id(1)
    nj = pl.num_programs(1)
    valid = i < nv_ref[0]

    @pl.when(valid)
    def _():
        @pl.when(j == 0)
        def _():
            x = x_ref[...]
            if has_gain:
                x = _rms(x, g_ref[...])
            xb_scr[...] = x.astype(xb_scr.dtype)

        xb = xb_scr[...]
        for c in range(tf // ck):
            h1 = _dot(xb, w1_ref[0, :, c * ck:(c + 1) * ck])
            h3 = _dot(xb, w3_ref[0, :, c * ck:(c + 1) * ck])
            a_scr[:, c * ck:(c + 1) * ck] = (_silu(h1) * h3).astype(a_scr.dtype)
        part = _dot(a_scr[...], w2_ref[0])

        @pl.when(j == 0)
        def _():
            acc_scr[...] = part

        @pl.when(j > 0)
        def _():
            acc_scr[...] += part

        @pl.when(j == nj - 1)
        def _():
            out = acc_scr[...]
            if has_res:
                out = x_ref[...] + out
            o_ref[...] = out

    @pl.when(jnp.logical_and(jnp.logical_not(valid), j == nj - 1))
    def _():
        o_ref[...] = jnp.zeros_like(o_ref)


def _ffn(x, w1, w3, w2, tile_expert, n_valid, gain=None, res=False, tm=TILE_ROWS, n_f=2,
         ck=F_CHUNK):
    p, d = x.shape
    f = w1.shape[2]
    tm = min(tm, p)
    tf = f // n_f
    n_tiles = p // tm
    in_specs = [pl.BlockSpec((tm, d), lambda i, j, te, nv: (i, 0))]
    args = [x]
    if gain is not None:
        in_specs.append(pl.BlockSpec((1, d), lambda i, j, te, nv: (0, 0)))
        args.append(gain.reshape(1, d))
    in_specs += [
        pl.BlockSpec((1, d, tf), lambda i, j, te, nv: (te[i], 0, j)),
        pl.BlockSpec((1, d, tf), lambda i, j, te, nv: (te[i], 0, j)),
        pl.BlockSpec((1, tf, d), lambda i, j, te, nv: (te[i], j, 0)),
    ]
    args += [w1, w3, w2]
    grid_spec = pltpu.PrefetchScalarGridSpec(
        num_scalar_prefetch=2,
        grid=(n_tiles, n_f),
        in_specs=in_specs,
        out_specs=pl.BlockSpec((tm, d), lambda i, j, te, nv: (i, 0)),
        scratch_shapes=[pltpu.VMEM((tm, d), w1.dtype), pltpu.VMEM((tm, tf), w1.dtype),
                        pltpu.VMEM((tm, d), F32)],
    )
    return pl.pallas_call(
        functools.partial(_ffn_kernel, has_gain=gain is not None, has_res=res, tf=tf, ck=ck),
        out_shape=jax.ShapeDtypeStruct((p, d), F32),
        grid_spec=grid_spec,
        compiler_params=_cparams(("arbitrary", "arbitrary")),
        name="swiglu",
    )(tile_expert, n_valid, *args)


def _hgrn_gates(fz, loglb, l1mlb, oneml):
    e = _exp_neg(jnp.abs(fz))
    ope = 1.0 + e
    log_sig = jnp.minimum(fz, 0.0) - jnp.log(ope)
    cc = l1mlb + log_sig
    log_f = jnp.maximum(loglb, cc) + jnp.log(1.0 + _exp_neg(jnp.abs(loglb - cc)))
    k = oneml * (jnp.where(fz > 0.0, e, 1.0) / ope)
    return log_f, k


def _level_reference(b, w):
    n = b.shape[0]
    if w >= 4:
        return jnp.concatenate(
            [jnp.broadcast_to(b[j + w:j + w + 1, :], (2 * w, b.shape[1]))
             for j in range(0, n, 2 * w)], axis=0)
    b3 = b.reshape(n // SUBLANES, SUBLANES, b.shape[1])
    ahead = lambda s: pltpu.roll(b3, SUBLANES - s, 1)
    pos = lax.broadcasted_iota(I32, (1, SUBLANES, 1), 1) & (2 * w - 1)
    if w == 2:
        r3 = jnp.where(pos == 0, ahead(2),
                       jnp.where(pos == 1, ahead(1),
                                 jnp.where(pos == 2, b3, pltpu.roll(b3, 1, 1))))
    else:
        r3 = jnp.where(pos == 0, ahead(1), b3)
    return r3.reshape(b.shape)


def _lru_gates(xc, wra, bra, wix, bix, c8):
    r = _sigmoid(_dot(xc, wra) + bra)
    ig = _sigmoid(_dot(xc, wix) + bix)
    log_a = c8 * r
    a = jnp.exp(log_a)
    one_m_a2 = -jnp.tanh(log_a) * (a * a + 1.0)
    u = jnp.sqrt(one_m_a2) * (ig * xc)
    return a, u


def _mixer_block_kernel(x_ref, g1_ref, win_ref, lbp_ref, hgg_ref, cw_ref, cb_ref, wra_ref,
                        bra_ref, wix_ref, bix_ref, c8_ref, cm_ref, wout_ref, *rest, tl, n_cast):
    cast_src = rest[:n_cast]
    xo_ref, hs_ref, ls_ref, cs_ref = rest[n_cast:n_cast + 4]
    cast_dst = rest[n_cast + 4:2 * n_cast + 4]
    (st_scr, h_scr, xpad_scr, hb_scr, q_scr, lf_scr, k_scr, v_scr, g_scr, gl_scr, a_scr, u_scr,
     m_scr) = rest[2 * n_cast + 4:]
    l = pl.program_id(1)
    nl = pl.num_programs(1)
    for src, dst in zip(cast_src, cast_dst):
        dst[...] = src[...].astype(dst.dtype)

    @pl.when(l == 0)
    def _():
        st_scr[...] = jnp.zeros_like(st_scr)
        h_scr[...] = jnp.zeros_like(h_scr)
        xpad_scr[0:8, :] = jnp.zeros((8, LRU_WIDTH), F32)

    hb_scr[...] = _rms(x_ref[0], g1_ref[...]).astype(BF16)
    hb = hb_scr[...]
    zg = lambda j: _dot(hb, win_ref[:, j * HG_WIDTH:(j + 1) * HG_WIDTH])
    q_scr[...] = _silu(zg(0))
    log_f, kk = _hgrn_gates(zg(1), lbp_ref[0:1, :], lbp_ref[1:2, :], lbp_ref[2:3, :])
    lf_scr[...] = log_f
    k_scr[...] = kk
    v_scr[...] = zg(2)
    g_scr[...] = _silu(zg(3))
    xpad_scr[8:8 + tl, :] = zg(4)
    gl_scr[...] = _gelu_tanh(zg(5))
    xc = cb_ref[...]
    for j in range(CONV_K):
        xc = xc + xpad_scr[5 + j:5 + j + tl, :] * cw_ref[j:j + 1, :]
    tail = xpad_scr[tl + 5:tl + 8, :]
    xpad_scr[5:8, :] = tail
    a, u = _lru_gates(xc, wra_ref[...], bra_ref[...], wix_ref[...], bix_ref[...], c8_ref[...])
    a_scr[...] = a
    u_scr[...] = u

    row = lax.broadcasted_iota(I32, (CHUNK, 1), 0)
    ti = lax.broadcasted_iota(I32, (CHUNK, CHUNK), 0)
    si = lax.broadcasted_iota(I32, (CHUNK, CHUNK), 1)
    t3 = lax.broadcasted_iota(I32, (1, DIAG, 1), 1)
    sub3 = lax.broadcasted_iota(I32, (1, SUBLANES, 1), 1)
    nblk = CHUNK // DIAG
    ngrp = CHUNK // SUBLANES
    levels = []
    w = CHUNK // 2
    while w >= DIAG:
        sh = (2 * w).bit_length() - 1
        pairs = ((ti >> sh) == (si >> sh)) & ((ti & (2 * w - 1)) >= w) & ((si & (2 * w - 1)) < w)
        levels.append((w, (row & (2 * w - 1)) >= w, pairs))
        w //= 2

    def chunk_body(c, carry):
        r0 = pl.multiple_of(c * CHUNK, CHUNK)
        rows = pl.ds(r0, CHUNK)

        for gi in range(LRU_WIDTH // LANES):
            cs = slice(gi * LANES, (gi + 1) * LANES)
            aa = a_scr[rows, cs].reshape(ngrp, SUBLANES, LANES)
            uu = u_scr[rows, cs].reshape(ngrp, SUBLANES, LANES)
            d = 1
            while d < SUBLANES:
                keep = sub3 >= d
                a_sh = pltpu.roll(aa, d, 1)
                u_sh = pltpu.roll(uu, d, 1)
                uu = jnp.where(keep, aa * u_sh + uu, uu)
                aa = jnp.where(keep, aa * a_sh, aa)
                d *= 2
            hprev = h_scr[:, cs]
            groups = []
            for j in range(ngrp):
                hj = uu[j] + aa[j] * hprev
                groups.append(hj)
                hprev = hj[SUBLANES - 1:SUBLANES, :]
            hs = jnp.concatenate(groups, axis=0)
            h_scr[:, cs] = hprev
            m_scr[rows, HG_WIDTH + gi * LANES:HG_WIDTH + (gi + 1) * LANES] = (
                hs * gl_scr[rows, cs]).astype(BF16)

        q = q_scr[rows, :]
        k = k_scr[rows, :]
        v = v_scr[rows, :]
        f1, f2, f3 = _split3(lf_scr[rows, :])
        cm = cm_ref[...]
        b = _dot(cm, f1) + _dot(cm, f2) + _dot(cm, f3)

        for hh in range(HG_HEADS):
            cs = slice(hh * HG_D, (hh + 1) * HG_D)
            qh, kh, vh, bh = q[:, cs], k[:, cs], v[:, cs], b[:, cs]
            vb = vh.astype(BF16)
            amat = jnp.zeros((CHUNK, CHUNK), F32)
            for w, upper, pairs in levels:
                dec = _exp_neg(jnp.abs(bh - _level_reference(bh, w)))
                y = (jnp.where(upper, qh, kh) * dec).astype(BF16)
                amat = jnp.where(pairs, _dot_nt(y, y), amat)
            o = _dot(amat.astype(BF16), vb)
            st = st_scr[hh]
            o = o + _dot_nt((qh * jnp.exp(bh)).astype(BF16), st.astype(BF16))
            if DIAG == 1:
                o = o + jnp.sum(qh * kh, axis=-1, keepdims=True) * vh
            else:
                q3 = qh.reshape(nblk, DIAG, HG_D)
                k3 = kh.reshape(nblk, DIAG, HG_D)
                v3 = vh.reshape(nblk, DIAG, HG_D)
                b3 = bh.reshape(nblk, DIAG, HG_D)
                od = jnp.zeros((nblk, DIAG, HG_D), F32)
                for s in range(DIAG):
                    dec = jnp.exp(jnp.minimum(b3 - b3[:, s:s + 1, :], 0.0))
                    p = q3 * (k3[:, s:s + 1, :] * dec)
                    rs = jnp.sum(p, axis=-1, keepdims=True)
                    rs = jnp.where(t3 >= s, rs, 0.0)
                    od = od + rs * v3[:, s:s + 1, :]
                o = o + od.reshape(CHUNK, HG_D)
            bl = bh[CHUNK - 1:CHUNK, :]
            kdec = (kh * jnp.exp(bl - bh)).astype(BF16)
            st_scr[hh] = st * jnp.exp(bl) + _dot_tn(vb, kdec)
            o = o * lax.rsqrt(jnp.mean(o * o, axis=-1, keepdims=True) + EPS)
            o = o * hgg_ref[:, cs]
            o = o * g_scr[rows, cs]
            m_scr[rows, cs] = o.astype(BF16)
        return carry

    lax.fori_loop(0, tl // CHUNK, chunk_body, 0, unroll=tl // CHUNK)
    xo_ref[0] = x_ref[0] + _dot(m_scr[...], wout_ref[...])

    @pl.when(l == nl - 1)
    def _():
        for hh in range(HG_HEADS):
            hs_ref[0, hh] = st_scr[hh].T
        ls_ref[0] = h_scr[...]
        cs_ref[0] = xpad_scr[5:8, :]


def _slab_spec(c, steps, step_of):
    share = 1
    while (c.shape[0] * share) % (steps * 2 * SUBLANES):
        share *= 2
    assert steps % share == 0
    rows = c.shape[0] * share // steps
    return pl.BlockSpec((rows, c.shape[1]), lambda *ids: (step_of(*ids) // share, 0))


def _mixer_block(x, gain, w_in, w_out, layer, mp, cast=(), tl=512):
    bsz, seq, d = x.shape
    tl = min(tl, seq)
    nl = seq // tl
    steps = bsz * nl
    cast_dtypes = [dt for _, dt in cast]
    cast = [c for c, _ in cast]
    full = lambda shape: pl.BlockSpec(shape, lambda b, l: (0,) * len(shape))
    stacked = lambda shape: pl.BlockSpec((None,) + shape, lambda b, l: (layer, 0, 0))
    slab = lambda c: _slab_spec(c, steps, lambda b, l: b * nl + l)
    out_shapes = [
        jax.ShapeDtypeStruct((bsz, seq, d), F32),
        jax.ShapeDtypeStruct((bsz, HG_HEADS, HG_D, HG_D), F32),
        jax.ShapeDtypeStruct((bsz, 1, LRU_WIDTH), F32),
        jax.ShapeDtypeStruct((bsz, CONV_K - 1, LRU_WIDTH), F32),
    ] + [jax.ShapeDtypeStruct(c.shape, dt) for c, dt in zip(cast, cast_dtypes)]
    half = lambda dt: pltpu.VMEM((tl, HG_WIDTH), dt)
    xo, hg, lru, conv, *rounded = pl.pallas_call(
        functools.partial(_mixer_block_kernel, tl=tl, n_cast=len(cast)),
        out_shape=out_shapes,
        grid=(bsz, nl),
        in_specs=[
            pl.BlockSpec((1, tl, d), lambda b, l: (b, l, 0)),
            full((1, d)), stacked((d, IN_WIDTH)),
            full((3, HG_WIDTH)), full((1, HG_WIDTH)), full((CONV_K, LRU_WIDTH)),
            full((1, LRU_WIDTH)), full((LRU_WIDTH, LRU_WIDTH)), full((1, LRU_WIDTH)),
            full((LRU_WIDTH, LRU_WIDTH)), full((1, LRU_WIDTH)), full((1, LRU_WIDTH)),
            full(mp["cm"].shape), stacked((d, d)),
        ] + [slab(c) for c in cast],
        out_specs=[
            pl.BlockSpec((1, tl, d), lambda b, l: (b, l, 0)),
            pl.BlockSpec((1, HG_HEADS, HG_D, HG_D), lambda b, l: (b, 0, 0, 0)),
            pl.BlockSpec((1, 1, LRU_WIDTH), lambda b, l: (b, 0, 0)),
            pl.BlockSpec((1, CONV_K - 1, LRU_WIDTH), lambda b, l: (b, 0, 0)),
        ] + [slab(c) for c in cast],
        scratch_shapes=[
            pltpu.VMEM((HG_HEADS, HG_D, HG_D), F32),
            pltpu.VMEM((1, LRU_WIDTH), F32),
            pltpu.VMEM((tl + 8, LRU_WIDTH), F32),
            pltpu.VMEM((tl, d), BF16),
            half(F32), half(F32), half(F32), half(F32), half(F32), half(F32), half(F32),
            half(F32),
            pltpu.VMEM((tl, d), BF16),
        ],
        compiler_params=_cparams(("parallel", "arbitrary")),
        name="mixer_block",
    )(x, gain.reshape(1, d), w_in, mp["lbp"], mp["hgg"], mp["cw"], mp["cb"], mp["wra"],
      mp["bra"], mp["wix"], mp["bix"], mp["c8"], mp["cm"], w_out, *cast)
    return xo, hg, lru.reshape(bsz, LRU_WIDTH), conv, rounded


def _mixer_step_kernel(z_ref, sh_ref, sl_ref, sc_ref, lbp_ref, hgg_ref, cw_ref, cb_ref, wra_ref,
                       bra_ref, wix_ref, bix_ref, c8_ref, *rest, bb, aliased):
    m_ref, hn_ref, ln_ref, cn_ref, o_scr = rest[1:] if aliased else rest
    z = z_ref[...]
    xr = z[:, 4 * HG_WIDTH:4 * HG_WIDTH + LRU_WIDTH]
    gr = z[:, 4 * HG_WIDTH + LRU_WIDTH:IN_WIDTH]
    buf = sc_ref[...]
    xc = cb_ref[...]
    for j in range(CONV_K - 1):
        xc = xc + buf[:, j * LRU_WIDTH:(j + 1) * LRU_WIDTH] * cw_ref[j:j + 1, :]
    xc = xc + xr * cw_ref[CONV_K - 1:CONV_K, :]
    cn_ref[:, 0:(CONV_K - 2) * LRU_WIDTH] = buf[:, LRU_WIDTH:(CONV_K - 1) * LRU_WIDTH]
    cn_ref[:, (CONV_K - 2) * LRU_WIDTH:(CONV_K - 1) * LRU_WIDTH] = xr
    a, u = _lru_gates(xc, wra_ref[...], bra_ref[...], wix_ref[...], bix_ref[...], c8_ref[...])
    hnew = u + a * sl_ref[...]
    ln_ref[...] = hnew
    m_ref[:, HG_WIDTH:D_MODEL] = (hnew * _gelu_tanh(gr)).astype(m_ref.dtype)

    q = _silu(z[:, 0:HG_WIDTH])
    log_f, k = _hgrn_gates(z[:, HG_WIDTH:2 * HG_WIDTH], lbp_ref[0:1, :], lbp_ref[1:2, :],
                           lbp_ref[2:3, :])
    f = jnp.exp(log_f)
    v = z[:, 2 * HG_WIDTH:3 * HG_WIDTH]
    gate = _silu(z[:, 3 * HG_WIDTH:4 * HG_WIDTH])
    zpad = jnp.zeros((HG_D - bb, HG_D), F32)
    for hh in range(HG_HEADS):
        cs = slice(hh * HG_D, (hh + 1) * HG_D)
        ft = jnp.concatenate([f[:, cs], zpad], axis=0).T
        kt = jnp.concatenate([k[:, cs], zpad], axis=0).T
        qt = jnp.concatenate([q[:, cs], zpad], axis=0).T
        for j in range(bb):
            fcol = jnp.broadcast_to(ft[:, j:j + 1], (HG_D, HG_D))
            kcol = jnp.broadcast_to(kt[:, j:j + 1], (HG_D, HG_D))
            qcol = jnp.broadcast_to(qt[:, j:j + 1], (HG_D, HG_D))
            vrow = jnp.broadcast_to(v[j:j + 1, cs], (HG_D, HG_D))
            sn = fcol * sh_ref[j, hh] + kcol * vrow
            hn_ref[j, hh] = sn
            o_scr[j:j + 1, cs] = jnp.sum(qcol * sn, axis=0, keepdims=True)
    for hh in range(HG_HEADS):
        cs = slice(hh * HG_D, (hh + 1) * HG_D)
        o = o_scr[:, cs]
        o = o * lax.rsqrt(jnp.mean(o * o, axis=-1, keepdims=True) + EPS)
        o = o * hgg_ref[:, cs]
        o = o * gate[:, cs]
        m_ref[:, cs] = o.astype(m_ref.dtype)


def _mixer_step(z, s_hg_all, layer, s_lru, s_conv, mp, hg_buf=None, bb=16):
    bsz = z.shape[0]
    bb = min(bb, bsz)
    cw3 = (CONV_K - 1) * LRU_WIDTH
    full = lambda shape: pl.BlockSpec(shape, lambda i: (0,) * len(shape))
    state_spec = pl.BlockSpec((None, bb, HG_HEADS, HG_D, HG_D), lambda i: (layer, i, 0, 0, 0))
    out_shapes = (
        jax.ShapeDtypeStruct((bsz, D_MODEL), mp["wra"].dtype),
        jax.ShapeDtypeStruct(s_hg_all.shape, F32),
        jax.ShapeDtypeStruct((bsz, LRU_WIDTH), F32),
        jax.ShapeDtypeStruct((bsz, cw3), F32),
    )
    in_specs = [
        pl.BlockSpec((bb, IN_WIDTH), lambda i: (i, 0)),
        state_spec,
        pl.BlockSpec((bb, LRU_WIDTH), lambda i: (i, 0)),
        pl.BlockSpec((bb, cw3), lambda i: (i, 0)),
        full((3, HG_WIDTH)), full((1, HG_WIDTH)), full((CONV_K, LRU_WIDTH)),
        full((1, LRU_WIDTH)), full((LRU_WIDTH, LRU_WIDTH)), full((1, LRU_WIDTH)),
        full((LRU_WIDTH, LRU_WIDTH)), full((1, LRU_WIDTH)), full((1, LRU_WIDTH)),
    ]
    args = [z, s_hg_all, s_lru, s_conv.reshape(bsz, cw3), mp["lbp"], mp["hgg"], mp["cw"],
            mp["cb"], mp["wra"], mp["bra"], mp["wix"], mp["bix"], mp["c8"]]
    aliases = {}
    if hg_buf is not None:
        aliases = {len(args): 1}
        in_specs.append(pl.BlockSpec(memory_space=pl.ANY))
        args.append(hg_buf)
    merged, hg, lru, conv = pl.pallas_call(
        functools.partial(_mixer_step_kernel, bb=bb, aliased=hg_buf is not None),
        out_shape=out_shapes,
        grid=(bsz // bb,),
        in_specs=in_specs,
        out_specs=(
            pl.BlockSpec((bb, D_MODEL), lambda i: (i, 0)),
            state_spec,
            pl.BlockSpec((bb, LRU_WIDTH), lambda i: (i, 0)),
            pl.BlockSpec((bb, cw3), lambda i: (i, 0)),
        ),
        scratch_shapes=[pltpu.VMEM((bb, HG_WIDTH), F32)],
        input_output_aliases=aliases,
        compiler_params=_cparams(("parallel",)),
        name="mixer_step",
    )(*args)
    return merged, hg, lru, conv.reshape(bsz, CONV_K - 1, LRU_WIDTH)


def _router_kernel(x_ref, g_ref, rw_ref, tril_ref, *refs, n_cast):
    cast_src = refs[:n_cast]
    info_ref, idx_ref, cnt_ref, zero_ref = refs[n_cast:n_cast + 4]
    cast_dst = refs[n_cast + 4:2 * n_cast + 4]
    carry_scr = refs[-1]
    i = pl.program_id(0)

    @pl.when(i == 0)
    def _():
        carry_scr[...] = jnp.zeros_like(carry_scr)

    zero_ref[...] = jnp.zeros_like(zero_ref)
    for src, dst in zip(cast_src, cast_dst):
        dst[...] = src[...].astype(dst.dtype)

    h = _rms(x_ref[...], g_ref[...])
    logits = jnp.dot(h, rw_ref[...], preferred_element_type=F32, precision=lax.Precision.HIGHEST)
    lane = lax.broadcasted_iota(I32, logits.shape, 1)
    neg = jnp.float32(-jnp.inf)
    logits = jnp.where(lane < N_EXPERTS, logits, neg)
    m1 = jnp.max(logits, axis=-1, keepdims=True)
    i1 = jnp.min(jnp.where(logits == m1, lane, LANES), axis=-1, keepdims=True)
    l2 = jnp.where(lane == i1, neg, logits)
    m2 = jnp.max(l2, axis=-1, keepdims=True)
    i2 = jnp.min(jnp.where(l2 == m2, lane, LANES), axis=-1, keepdims=True)
    e = jnp.exp(m2 - m1)
    p1 = 1.0 / (1.0 + e)
    p2 = e / (1.0 + e)
    oh1 = (lane == i1).astype(F32)
    oh2 = (lane == i2).astype(F32)
    sel = oh1 + oh2
    before = _dot(tril_ref[...], sel.astype(BF16)) + carry_scr[...]
    r1 = jnp.sum(oh1 * before, axis=-1, keepdims=True)
    r2 = jnp.sum(oh2 * before, axis=-1, keepdims=True)
    carry_scr[...] += jnp.sum(sel, axis=0, keepdims=True)
    info = jnp.where(lane == 0, i1.astype(F32), 0.0)
    info = jnp.where(lane == 1, i2.astype(F32), info)
    info = jnp.where(lane == 2, r1, info)
    info = jnp.where(lane == 3, r2, info)
    info = jnp.where(lane == 4, p1, info)
    info = jnp.where(lane == 5, p2, info)
    info_ref[...] = info
    idx_ref[...] = info.T[0:SUBLANES, :].astype(I32)
    cnt_ref[...] = carry_scr[...]


def _router(x, gain, rw_pad, n_rows, cast=(), tm=512):
    t, d = x.shape
    tm = min(tm, t)
    steps = t // tm
    zrows = n_rows // steps
    assert zrows * steps == n_rows and zrows % SUBLANES == 0
    slab = lambda c: _slab_spec(c, steps, lambda i: i)
    tril = jnp.tril(jnp.ones((tm, tm), BF16), -1)
    return pl.pallas_call(
        functools.partial(_router_kernel, n_cast=len(cast)),
        out_shape=[jax.ShapeDtypeStruct((t, LANES), F32),
                   jax.ShapeDtypeStruct((SUBLANES, t), I32),
                   jax.ShapeDtypeStruct((1, LANES), F32),
                   jax.ShapeDtypeStruct((n_rows, d), F32)]
        + [jax.ShapeDtypeStruct(c.shape, BF16) for c in cast],
        grid=(steps,),
        in_specs=[
            pl.BlockSpec((tm, d), lambda i: (i, 0)),
            pl.BlockSpec((1, d), lambda i: (0, 0)),
            pl.BlockSpec((d, LANES), lambda i: (0, 0)),
            pl.BlockSpec((tm, tm), lambda i: (0, 0)),
        ] + [slab(c) for c in cast],
        out_specs=[pl.BlockSpec((tm, LANES), lambda i: (i, 0)),
                   pl.BlockSpec((SUBLANES, tm), lambda i: (0, i)),
                   pl.BlockSpec((1, LANES), lambda i: (0, 0)),
                   pl.BlockSpec((zrows, d), lambda i: (i, 0))] + [slab(c) for c in cast],
        scratch_shapes=[pltpu.VMEM((1, LANES), F32)],
        compiler_params=_cparams(("arbitrary",)),
        name="router",
    )(x, gain.reshape(1, d), rw_pad, tril, *cast)


def _row_copy(src, dst, s, d, sem):
    return pltpu.make_async_copy(src.at[pl.ds(s, 1)], dst.at[pl.ds(d, 1)], sem)


def _tile_dest(dest, t, td):
    nb = t // td
    dest3 = jnp.concatenate([dest[0].reshape(nb, td), dest[1].reshape(nb, td)], axis=1)
    return dest3.reshape(nb, 1, 2 * td)


def _dispatch_kernel(dest_ref, x_ref, g_ref, xs_in_ref, xs_ref, pk_scr, sem, *, td):
    del xs_in_ref
    pk_scr[...] = _rms(x_ref[...], g_ref[...])

    def issue(r, c):
        _row_copy(pk_scr, xs_ref, r, dest_ref[0, 0, r], sem).start(priority=0)
        _row_copy(pk_scr, xs_ref, r, dest_ref[0, 0, td + r], sem).start(priority=1)
        return c

    lax.fori_loop(0, td, issue, 0, unroll=8)
    for _ in range(2):
        pltpu.make_async_copy(pk_scr, xs_ref.at[pl.ds(0, td)], sem).wait()


def _dispatch(x, gain, dest, xs0, td=512):
    t, d = x.shape
    td = min(td, t)
    n_rows = xs0.shape[0]
    return pl.pallas_call(
        functools.partial(_dispatch_kernel, td=td),
        out_shape=jax.ShapeDtypeStruct((n_rows, d), F32),
        grid=(t // td,),
        in_specs=[
            pl.BlockSpec((1, 1, 2 * td), lambda i: (i, 0, 0), memory_space=pltpu.SMEM),
            pl.BlockSpec((td, d), lambda i: (i, 0)),
            pl.BlockSpec((1, d), lambda i: (0, 0)),
            pl.BlockSpec(memory_space=pl.ANY),
        ],
        out_specs=pl.BlockSpec(memory_space=pl.ANY),
        scratch_shapes=[pltpu.VMEM((td, d), F32), pltpu.SemaphoreType.DMA(())],
        input_output_aliases={3: 0},
        compiler_params=pltpu.CompilerParams(dimension_semantics=("arbitrary",),
                                             vmem_limit_bytes=VMEM_LIMIT_BYTES,
                                             has_side_effects=True),
        name="dispatch",
    )(_tile_dest(dest, t, td), x, gain.reshape(1, d), xs0)


def _combine_kernel(dest_ref, nxt_ref, x_ref, info_ref, g_ref, ys_ref, o_ref, y_scr, sems, *, td):
    i = pl.program_id(0)
    nb = pl.num_programs(0)

    def fetch(idx_ref, slot):
        def issue(r, c):
            _row_copy(ys_ref, y_scr.at[slot, 0], idx_ref[0, 0, r], r,
                      sems.at[slot]).start(priority=0)
            _row_copy(ys_ref, y_scr.at[slot, 1], idx_ref[0, 0, td + r], r,
                      sems.at[slot]).start(priority=1)
            return c

        lax.fori_loop(0, td, issue, 0, unroll=8)

    def finish(slot):
        for j in range(2):
            pltpu.make_async_copy(ys_ref.at[pl.ds(0, td)], y_scr.at[slot, j],
                                  sems.at[slot]).wait()
        p1 = info_ref[:, 4:5]
        p2 = info_ref[:, 5:6]
        x = x_ref[...] + (p1 * y_scr[slot, 0] + p2 * y_scr[slot, 1])
        o_ref[...] = _rms(x, g_ref[...])

    @pl.when(i == 0)
    def _():
        fetch(dest_ref, 0)

    for slot in range(2):
        @pl.when(i % 2 == slot)
        def _(slot=slot):
            @pl.when(i + 1 < nb)
            def _():
                fetch(nxt_ref, 1 - slot)

            finish(slot)


def _combine(x, ys, dest, info, gain, td=256):
    t, d = x.shape
    td = min(td, t)
    nb = t // td
    row = pl.BlockSpec((td, d), lambda i: (i, 0))
    tiled = _tile_dest(dest, t, td)
    return pl.pallas_call(
        functools.partial(_combine_kernel, td=td),
        out_shape=jax.ShapeDtypeStruct((t, d), F32),
        grid=(nb,),
        in_specs=[
            pl.BlockSpec((1, 1, 2 * td), lambda i: (i, 0, 0), memory_space=pltpu.SMEM),
            pl.BlockSpec((1, 1, 2 * td), lambda i: (jnp.minimum(i + 1, nb - 1), 0, 0),
                         memory_space=pltpu.SMEM),
            row,
            pl.BlockSpec((td, LANES), lambda i: (i, 0)),
            pl.BlockSpec((1, d), lambda i: (0, 0)),
            pl.BlockSpec(memory_space=pl.ANY),
        ],
        out_specs=row,
        scratch_shapes=[pltpu.VMEM((2, 2, td, d), F32), pltpu.SemaphoreType.DMA((2,))],
        compiler_params=_cparams(("arbitrary",)),
        name="combine",
    )(tiled, tiled, x, info, gain.reshape(1, d), ys)


def _rows2d(w):
    return w.reshape(-1, w.shape[-1])


def _moe(xs_in, norm_g, rw_pad, experts, shared, final_g, tm_e=TILE_ROWS):
    d = xs_in[0].shape[1]
    n_tiles = (2 * sum(x.shape[0] for x in xs_in) + N_EXPERTS * (tm_e - 1)) // tm_e
    n_rows = n_tiles * tm_e
    routed = []
    xs0 = None
    for g, x in enumerate(xs_in):
        todo = [(name, w) for name, w in experts if name not in shared]
        steps = x.shape[0] // min(TILE_ROWS, x.shape[0])
        info, idx, cnt, zeros, *rounded = _router(
            x, norm_g, rw_pad, n_rows if g == 0 else steps * SUBLANES,
            cast=[_rows2d(w) for _, w in todo])
        for (name, w), r in zip(todo, rounded):
            shared[name] = r.reshape(w.shape)
        xs0 = zeros if g == 0 else xs0
        routed.append((info, idx, cnt[0, :N_EXPERTS].astype(I32)))
    w1, w3, w2 = (shared[name] for name, _ in experts)
    counts = sum(c for _, _, c in routed)
    padded = ((counts + tm_e - 1) // tm_e) * tm_e
    ends = jnp.cumsum(padded)
    starts = ends - padded
    tile_start = jnp.arange(n_tiles, dtype=I32) * tm_e
    tile_expert = jnp.minimum(
        jnp.sum((tile_start[:, None] >= ends[None, :]).astype(I32), axis=1), N_EXPERTS - 1)
    n_valid = (ends[-1] // tm_e).reshape(1).astype(I32)
    dests = []
    base = starts
    for info, idx, cnt in routed:
        e1, e2, r1, r2 = idx[0], idx[1], idx[2], idx[3]
        dests.append(jnp.stack([base[e1] + r1, base[e2] + r2]))
        base = base + cnt
    rows = xs0
    for x, dest in zip(xs_in, dests):
        rows = _dispatch(x, norm_g, dest, rows)
    ys = _ffn(rows, w1, w3, w2, tile_expert.astype(I32), n_valid, tm=tm_e, n_f=1)
    return [_combine(x, ys, dest, info, final_g)
            for x, dest, (info, _, _) in zip(xs_in, dests, routed)]


def _block_diag(w):
    n, c, d = w.shape
    eye = jnp.eye(n, dtype=w.dtype)
    return (w[:, :, None, :] * eye[:, None, :, None]).reshape(n * c, n * d)


def _cum_matrix():
    t = jnp.arange(CHUNK)
    return (t[None, :] <= t[:, None]).astype(BF16)


def _mixer_params(l, lb, hg_norm_g, conv_w, conv_b, w_ra, b_ra, w_ix, b_ix, lru_lambda, wdtype):
    row = lambda a: a.reshape(1, -1).astype(F32)
    return {
        "lbp": jnp.stack([jnp.log(lb), jnp.log1p(-lb), 1.0 - lb]).astype(F32),
        "hgg": row(hg_norm_g[l]),
        "cw": conv_w[l].astype(F32),
        "cb": row(conv_b[l]),
        "wra": _block_diag(w_ra[l].astype(F32)).astype(wdtype),
        "bra": row(b_ra[l]),
        "wix": _block_diag(w_ix[l].astype(F32)).astype(wdtype),
        "bix": row(b_ix[l]),
        "c8": row(-LRU_C * jax.nn.softplus(-lru_lambda[l].astype(F32))),
        "cm": _cum_matrix(),
    }


def _trunk(x, seq_shape, states, prm):
    bsz, seq = seq_shape
    one = jnp.ones((1,), I32)
    new_hg, new_lru, new_conv = [], [], []
    shared = prm["shared"]
    hg_stack = None if states is None else shared.get("hg_seed", states[0])
    for l in range(2):
        mp = prm["mixer"][l]
        if states is None:
            x3 = x.reshape(bsz, seq, D_MODEL)
            jobs = [j for j in prm["side"][l] if j[0] not in shared]
            if l == 0:
                w_in, w_out, layer = prm["w_in"], prm["w_out"], 0
            else:
                w_in, w_out, layer = shared["w_in"], shared["w_out"], 1
            x, s1, s2, s3, done = _mixer_block(
                x3, prm["norm1_g"][l], w_in, w_out, layer, mp,
                cast=[(_rows2d(a), dt) for _, a, dt in jobs])
            for (name, a, _), r in zip(jobs, done):
                shared[name] = r.reshape(a.shape)
            x = x.reshape(bsz * seq, D_MODEL)
            new_hg.append(s1)
        else:
            z = _linear(x, prm["w_in"], l, gain=prm["norm1_g"][l])
            merged, hg_stack, s2, s3 = _mixer_step(z, states[0], l, states[1][l], states[2][l],
                                                   mp, hg_buf=hg_stack)
            x = _linear(merged, prm["w_out"], l, res=x)
        new_lru.append(s2)
        new_conv.append(s3)
        if l == 0:
            n_tiles = max(x.shape[0] // TILE_ROWS, 1)
            if states is None:
                f1, f3, f2 = shared["ffn_w1"], shared["ffn_w3"], shared["ffn_w2"]
            else:
                f1, f3, f2 = prm["ffn_w1"], prm["ffn_w3"], prm["ffn_w2"]
            n_f = 1 if f1.dtype == BF16 else f1.shape[2] // F_CHUNK
            x = _ffn(x, f1, f3, f2, jnp.zeros((n_tiles,), I32), one * n_tiles,
                     gain=prm["norm2_g"][0], res=True, tm=TILE_ROWS, n_f=n_f, ck=F_CHUNK)
    hg_all = jnp.stack(new_hg) if states is None else hg_stack
    return x, hg_all, jnp.stack(new_lru), jnp.stack(new_conv)


def kernel(x_prompt, x_sample, state_hgrn, state_lru, state_conv, norm1_g, w_in, lower_bounds,
           hg_norm_g, conv_w, conv_b, w_ra, b_ra, w_ix, b_ix, lru_lambda, w_out, norm2_g,
           ffn_w1, ffn_w3, ffn_w2, router_w, moe_w1, moe_w3, moe_w2, final_norm_g):
    lb_all = jnp.cumsum(jax.nn.softmax(lower_bounds.astype(F32), axis=0), axis=0)
    lb_all = lb_all - lb_all[0]
    f32 = lambda a: a.astype(F32)
    common = {
        "experts": [("moe_w1", f32(moe_w1[0])), ("moe_w3", f32(moe_w3[0])),
                    ("moe_w2", f32(moe_w2[0]))],
        "side": {0: [("moe_w3", f32(moe_w3[0]), BF16), ("ffn_w1", f32(ffn_w1), BF16),
                     ("ffn_w3", f32(ffn_w3), BF16), ("ffn_w2", f32(ffn_w2), BF16),
                     ("w_in", f32(w_in), BF16), ("w_out", f32(w_out), BF16)],
                 1: [("moe_w2", f32(moe_w2[0]), BF16), ("hg_seed", f32(state_hgrn), F32)]},
        "shared": {},
        "rw_pad": jnp.pad(router_w[0].astype(F32), ((0, 0), (0, LANES - N_EXPERTS))),
        "norm1_g": norm1_g, "norm2_g": norm2_g, "final_norm_g": final_norm_g,
    }
    mixer = lambda wdtype: [_mixer_params(l, lb_all[l], hg_norm_g, conv_w, conv_b, w_ra, b_ra,
                                          w_ix, b_ix, lru_lambda, wdtype) for l in range(2)]
    prompt = dict(common, mixer=mixer(BF16), w_in=w_in[:1].astype(BF16),
                  w_out=w_out[:1].astype(BF16))

    bp, sp, d = x_prompt.shape
    bs = x_sample.shape[0]
    x_p, hg_p, lru_p, conv_p = _trunk(x_prompt.reshape(bp * sp, d), (bp, sp), None, prompt)
    sample = dict(common, mixer=mixer(F32), w_in=f32(w_in), w_out=f32(w_out),
                  ffn_w1=f32(ffn_w1), ffn_w3=f32(ffn_w3), ffn_w2=f32(ffn_w2))
    x_s, hg_s, lru_s, conv_s = _trunk(x_sample.reshape(bs, d), (bs, 1),
                                      (state_hgrn, state_lru, state_conv), sample)
    y_p, y_s = _moe([x_p, x_s], norm2_g[1], common["rw_pad"], common["experts"],
                    common["shared"], final_norm_g)
    return (y_p.reshape(bp, sp, d), y_s.reshape(bs, 1, d), hg_p, lru_p, conv_p, hg_s, lru_s,
            conv_s)
```

```python
import functools

import jax
import jax.numpy as jnp
from jax import lax
from jax.experimental import pallas as pl
from jax.experimental.pallas import tpu as pltpu

F32 = jnp.float32
BF16 = jnp.bfloat16
I32 = jnp.int32

D_MODEL = 1024
HG_WIDTH = 512
HG_HEADS = 4
HG_D = 128
LRU_WIDTH = 512
LRU_BLOCKS = 8
LRU_C = 8.0
CONV_K = 4
IN_WIDTH = 4 * HG_WIDTH + 2 * LRU_WIDTH
N_EXPERTS = 8
EPS = 1e-6

CHUNK = 128
DIAG = 1
LANES = 128
SUBLANES = 8
TILE_ROWS = 512
F_CHUNK = 256
VMEM_LIMIT_BYTES = 56 * 1024 * 1024


def _cparams(sem):
    return pltpu.CompilerParams(dimension_semantics=sem, vmem_limit_bytes=VMEM_LIMIT_BYTES)


def _exp_neg(x):
    return jnp.exp2(x * -1.4426950408889634)


def _sigmoid(x):
    return 1.0 / (1.0 + _exp_neg(x))


def _silu(x):
    return x * _sigmoid(x)


def _gelu_tanh(x):
    c = 0.7978845608028654
    return 0.5 * x * (1.0 + jnp.tanh(c * (x + 0.044715 * (x * x * x))))


def _rms(x, g):
    return x * lax.rsqrt(jnp.mean(x * x, axis=-1, keepdims=True) + EPS) * g


def _dot(a, b):
    if b.dtype == F32:
        return jnp.dot(a.astype(F32), b, preferred_element_type=F32,
                       precision=lax.Precision.HIGHEST)
    return jnp.dot(a.astype(BF16), b, preferred_element_type=F32)


def _dot_nt(a, b):
    return lax.dot_general(a, b, (((1,), (1,)), ((), ())), preferred_element_type=F32)


def _dot_tn(a, b):
    return lax.dot_general(a, b, (((0,), (0,)), ((), ())), preferred_element_type=F32)


def _split3(x):
    x1 = x.astype(BF16)
    r1 = x - x1.astype(F32)
    x2 = r1.astype(BF16)
    x3 = (r1 - x2.astype(F32)).astype(BF16)
    return x1, x2, x3


def _linear_kernel(*refs, has_gain, has_res):
    it = iter(refs)
    x_ref = next(it)
    g_ref = next(it) if has_gain else None
    w_ref = next(it)
    r_ref = next(it) if has_res else None
    o_ref = next(it)
    h_scr = next(it)

    @pl.when(pl.program_id(1) == 0)
    def _():
        x = x_ref[...].astype(F32)
        if has_gain:
            x = _rms(x, g_ref[...])
        h_scr[...] = x.astype(h_scr.dtype)

    acc = _dot(h_scr[...], w_ref[...])
    if has_res:
        acc = acc + r_ref[...]
    o_ref[...] = acc.astype(o_ref.dtype)


def _linear(x, w, layer, gain=None, res=None, tm=1024, tn=1024):
    m, k = x.shape
    n = w.shape[2]
    tm = min(tm, m)
    tn = min(tn, n)
    in_specs = [pl.BlockSpec((tm, k), lambda i, j: (i, 0))]
    args = [x]
    if gain is not None:
        in_specs.append(pl.BlockSpec((1, k), lambda i, j: (0, 0)))
        args.append(gain.reshape(1, k))
    in_specs.append(pl.BlockSpec((None, k, tn), lambda i, j: (layer, 0, j)))
    args.append(w)
    if res is not None:
        in_specs.append(pl.BlockSpec((tm, tn), lambda i, j: (i, j)))
        args.append(res)
    return pl.pallas_call(
        functools.partial(_linear_kernel, has_gain=gain is not None, has_res=res is not None),
        out_shape=jax.ShapeDtypeStruct((m, n), F32),
        grid=(m // tm, n // tn),
        in_specs=in_specs,
        out_specs=pl.BlockSpec((tm, tn), lambda i, j: (i, j)),
        scratch_shapes=[pltpu.VMEM((tm, k), w.dtype)],
        compiler_params=_cparams(("parallel", "arbitrary")),
        name="linear",
    )(*args)


def _ffn_kernel(te_ref, nv_ref, *refs, has_gain, has_res, tf, ck):
    del te_ref
    it = iter(refs)
    x_ref = next(it)
    g_ref = next(it) if has_gain else None
    w1_ref, w3_ref, w2_ref, o_ref, xb_scr, a_scr, acc_scr = it
    i = pl.program_id(0)
    j = pl.program_id(1)
    nj = pl.num_programs(1)
    valid = i < nv_ref[0]

    @pl.when(valid)
    def _():
        @pl.when(j == 0)
        def _():
            x = x_ref[...]
            if has_gain:
                x = _rms(x, g_ref[...])
            xb_scr[...] = x.astype(xb_scr.dtype)

        xb = xb_scr[...]
        for c in range(tf // ck):
            h1 = _dot(xb, w1_ref[0, :, c * ck:(c + 1) * ck])
            h3 = _dot(xb, w3_ref[0, :, c * ck:(c + 1) * ck])
            a_scr[:, c * ck:(c + 1) * ck] = (_silu(h1) * h3).astype(a_scr.dtype)
        part = _dot(a_scr[...], w2_ref[0])

        @pl.when(j == 0)
        def _():
            acc_scr[...] = part

        @pl.when(j > 0)
        def _():
            acc_scr[...] += part

        @pl.when(j == nj - 1)
        def _():
            out = acc_scr[...]
            if has_res:
                out = x_ref[...] + out
            o_ref[...] = out

    @pl.when(jnp.logical_and(jnp.logical_not(valid), j == nj - 1))
    def _():
        o_ref[...] = jnp.zeros_like(o_ref)


def _ffn(x, w1, w3, w2, tile_expert, n_valid, gain=None, res=False, tm=TILE_ROWS, n_f=2,
         ck=F_CHUNK):
    p, d = x.shape
    f = w1.shape[2]
    tm = min(tm, p)
    tf = f // n_f
    n_tiles = p // tm
    in_specs = [pl.BlockSpec((tm, d), lambda i, j, te, nv: (i, 0))]
    args = [x]
    if gain is not None:
        in_specs.append(pl.BlockSpec((1, d), lambda i, j, te, nv: (0, 0)))
        args.append(gain.reshape(1, d))
    in_specs += [
        pl.BlockSpec((1, d, tf), lambda i, j, te, nv: (te[i], 0, j)),
        pl.BlockSpec((1, d, tf), lambda i, j, te, nv: (te[i], 0, j)),
        pl.BlockSpec((1, tf, d), lambda i, j, te, nv: (te[i], j, 0)),
    ]
    args += [w1, w3, w2]
    grid_spec = pltpu.PrefetchScalarGridSpec(
        num_scalar_prefetch=2,
        grid=(n_tiles, n_f),
        in_specs=in_specs,
        out_specs=pl.BlockSpec((tm, d), lambda i, j, te, nv: (i, 0)),
        scratch_shapes=[pltpu.VMEM((tm, d), w1.dtype), pltpu.VMEM((tm, tf), w1.dtype),
                        pltpu.VMEM((tm, d), F32)],
    )
    return pl.pallas_call(
        functools.partial(_ffn_kernel, has_gain=gain is not None, has_res=res, tf=tf, ck=ck),
        out_shape=jax.ShapeDtypeStruct((p, d), F32),
        grid_spec=grid_spec,
        compiler_params=_cparams(("arbitrary", "arbitrary")),
        name="swiglu",
    )(tile_expert, n_valid, *args)


def _hgrn_gates(fz, loglb, l1mlb, oneml):
    e = _exp_neg(jnp.abs(fz))
    ope = 1.0 + e
    log_sig = jnp.minimum(fz, 0.0) - jnp.log(ope)
    cc = l1mlb + log_sig
    log_f = jnp.maximum(loglb, cc) + jnp.log(1.0 + _exp_neg(jnp.abs(loglb - cc)))
    k = oneml * (jnp.where(fz > 0.0, e, 1.0) / ope)
    return log_f, k


def _level_reference(b, w):
    n = b.shape[0]
    if w >= 4:
        return jnp.concatenate(
            [jnp.broadcast_to(b[j + w:j + w + 1, :], (2 * w, b.shape[1]))
             for j in range(0, n, 2 * w)], axis=0)
    b3 = b.reshape(n // SUBLANES, SUBLANES, b.shape[1])
    ahead = lambda s: pltpu.roll(b3, SUBLANES - s, 1)
    pos = lax.broadcasted_iota(I32, (1, SUBLANES, 1), 1) & (2 * w - 1)
    if w == 2:
        r3 = jnp.where(pos == 0, ahead(2),
                       jnp.where(pos == 1, ahead(1),
                                 jnp.where(pos == 2, b3, pltpu.roll(b3, 1, 1))))
    else:
        r3 = jnp.where(pos == 0, ahead(1), b3)
    return r3.reshape(b.shape)


def _lru_gates(xc, wra, bra, wix, bix, c8):
    r = _sigmoid(_dot(xc, wra) + bra)
    ig = _sigmoid(_dot(xc, wix) + bix)
    log_a = c8 * r
    a = jnp.exp(log_a)
    one_m_a2 = -jnp.tanh(log_a) * (a * a + 1.0)
    u = jnp.sqrt(one_m_a2) * (ig * xc)
    return a, u


def _mixer_block_kernel(x_ref, g1_ref, win_ref, lbp_ref, hgg_ref, cw_ref, cb_ref, wra_ref,
                        bra_ref, wix_ref, bix_ref, c8_ref, cm_ref, wout_ref, *rest, tl, n_cast):
    cast_src = rest[:n_cast]
    xo_ref, hs_ref, ls_ref, cs_ref = rest[n_cast:n_cast + 4]
    cast_dst = rest[n_cast + 4:2 * n_cast + 4]
    (st_scr, h_scr, xpad_scr, hb_scr, q_scr, lf_scr, k_scr, v_scr, g_scr, gl_scr, a_scr, u_scr,
     m_scr) = rest[2 * n_cast + 4:]
    l = pl.program_id(1)
    nl = pl.num_programs(1)
    for src, dst in zip(cast_src, cast_dst):
        dst[...] = src[...].astype(dst.dtype)

    @pl.when(l == 0)
    def _():
        st_scr[...] = jnp.zeros_like(st_scr)
        h_scr[...] = jnp.zeros_like(h_scr)
        xpad_scr[0:8, :] = jnp.zeros((8, LRU_WIDTH), F32)

    hb_scr[...] = _rms(x_ref[0], g1_ref[...]).astype(BF16)
    hb = hb_scr[...]
    zg = lambda j: _dot(hb, win_ref[:, j * HG_WIDTH:(j + 1) * HG_WIDTH])
    q_scr[...] = _silu(zg(0))
    log_f, kk = _hgrn_gates(zg(1), lbp_ref[0:1, :], lbp_ref[1:2, :], lbp_ref[2:3, :])
    lf_scr[...] = log_f
    k_scr[...] = kk
    v_scr[...] = zg(2)
    g_scr[...] = _silu(zg(3))
    xpad_scr[8:8 + tl, :] = zg(4)
    gl_scr[...] = _gelu_tanh(zg(5))
    xc = cb_ref[...]
    for j in range(CONV_K):
        xc = xc + xpad_scr[5 + j:5 + j + tl, :] * cw_ref[j:j + 1, :]
    tail = xpad_scr[tl + 5:tl + 8, :]
    xpad_scr[5:8, :] = tail
    a, u = _lru_gates(xc, wra_ref[...], bra_ref[...], wix_ref[...], bix_ref[...], c8_ref[...])
    a_scr[...] = a
    u_scr[...] = u

    row = lax.broadcasted_iota(I32, (CHUNK, 1), 0)
    ti = lax.broadcasted_iota(I32, (CHUNK, CHUNK), 0)
    si = lax.broadcasted_iota(I32, (CHUNK, CHUNK), 1)
    t3 = lax.broadcasted_iota(I32, (1, DIAG, 1), 1)
    sub3 = lax.broadcasted_iota(I32, (1, SUBLANES, 1), 1)
    nblk = CHUNK // DIAG
    ngrp = CHUNK // SUBLANES
    levels = []
    w = CHUNK // 2
    while w >= DIAG:
        sh = (2 * w).bit_length() - 1
        pairs = ((ti >> sh) == (si >> sh)) & ((ti & (2 * w - 1)) >= w) & ((si & (2 * w - 1)) < w)
        levels.append((w, (row & (2 * w - 1)) >= w, pairs))
        w //= 2

    def chunk_body(c, carry):
        r0 = pl.multiple_of(c * CHUNK, CHUNK)
        rows = pl.ds(r0, CHUNK)

        for gi in range(LRU_WIDTH // LANES):
            cs = slice(gi * LANES, (gi + 1) * LANES)
            aa = a_scr[rows, cs].reshape(ngrp, SUBLANES, LANES)
            uu = u_scr[rows, cs].reshape(ngrp, SUBLANES, LANES)
            d = 1
            while d < SUBLANES:
                keep = sub3 >= d
                a_sh = pltpu.roll(aa, d, 1)
                u_sh = pltpu.roll(uu, d, 1)
                uu = jnp.where(keep, aa * u_sh + uu, uu)
                aa = jnp.where(keep, aa * a_sh, aa)
                d *= 2
            hprev = h_scr[:, cs]
            groups = []
            for j in range(ngrp):
                hj = uu[j] + aa[j] * hprev
                groups.append(hj)
                hprev = hj[SUBLANES - 1:SUBLANES, :]
            hs = jnp.concatenate(groups, axis=0)
            h_scr[:, cs] = hprev
            m_scr[rows, HG_WIDTH + gi * LANES:HG_WIDTH + (gi + 1) * LANES] = (
                hs * gl_scr[rows, cs]).astype(BF16)

        q = q_scr[rows, :]
        k = k_scr[rows, :]
        v = v_scr[rows, :]
        f1, f2, f3 = _split3(lf_scr[rows, :])
        cm = cm_ref[...]
        b = _dot(cm, f1) + _dot(cm, f2) + _dot(cm, f3)

        for hh in range(HG_HEADS):
            cs = slice(hh * HG_D, (hh + 1) * HG_D)
            qh, kh, vh, bh = q[:, cs], k[:, cs], v[:, cs], b[:, cs]
            vb = vh.astype(BF16)
            amat = jnp.zeros((CHUNK, CHUNK), F32)
            for w, upper, pairs in levels:
                dec = _exp_neg(jnp.abs(bh - _level_reference(bh, w)))
                y = (jnp.where(upper, qh, kh) * dec).astype(BF16)
                amat = jnp.where(pairs, _dot_nt(y, y), amat)
            o = _dot(amat.astype(BF16), vb)
            st = st_scr[hh]
            o = o + _dot_nt((qh * jnp.exp(bh)).astype(BF16), st.astype(BF16))
            if DIAG == 1:
                o = o + jnp.sum(qh * kh, axis=-1, keepdims=True) * vh
            else:
                q3 = qh.reshape(nblk, DIAG, HG_D)
                k3 = kh.reshape(nblk, DIAG, HG_D)
                v3 = vh.reshape(nblk, DIAG, HG_D)
                b3 = bh.reshape(nblk, DIAG, HG_D)
                od = jnp.zeros((nblk, DIAG, HG_D), F32)
                for s in range(DIAG):
                    dec = jnp.exp(jnp.minimum(b3 - b3[:, s:s + 1, :], 0.0))
                    p = q3 * (k3[:, s:s + 1, :] * dec)
                    rs = jnp.sum(p, axis=-1, keepdims=True)
                    rs = jnp.where(t3 >= s, rs, 0.0)
                    od = od + rs * v3[:, s:s + 1, :]
                o = o + od.reshape(CHUNK, HG_D)
            bl = bh[CHUNK - 1:CHUNK, :]
            kdec = (kh * jnp.exp(bl - bh)).astype(BF16)
            st_scr[hh] = st * jnp.exp(bl) + _dot_tn(vb, kdec)
            o = o * lax.rsqrt(jnp.mean(o * o, axis=-1, keepdims=True) + EPS)
            o = o * hgg_ref[:, cs]
            o = o * g_scr[rows, cs]
            m_scr[rows, cs] = o.astype(BF16)
        return carry

    lax.fori_loop(0, tl // CHUNK, chunk_body, 0, unroll=tl // CHUNK)
    xo_ref[0] = x_ref[0] + _dot(m_scr[...], wout_ref[...])

    @pl.when(l == nl - 1)
    def _():
        for hh in range(HG_HEADS):
            hs_ref[0, hh] = st_scr[hh].T
        ls_ref[0] = h_scr[...]
        cs_ref[0] = xpad_scr[5:8, :]


def _slab_spec(c, steps, step_of):
    share = 1
    while (c.shape[0] * share) % (steps * 2 * SUBLANES):
        share *= 2
    assert steps % share == 0
    rows = c.shape[0] * share // steps
    return pl.BlockSpec((rows, c.shape[1]), lambda *ids: (step_of(*ids) // share, 0))


def _mixer_block(x, gain, w_in, w_out, layer, mp, cast=(), tl=512):
    bsz, seq, d = x.shape
    tl = min(tl, seq)
    nl = seq // tl
    steps = bsz * nl
    cast_dtypes = [dt for _, dt in cast]
    cast = [c for c, _ in cast]
    full = lambda shape: pl.BlockSpec(shape, lambda b, l: (0,) * len(shape))
    stacked = lambda shape: pl.BlockSpec((None,) + shape, lambda b, l: (layer, 0, 0))
    slab = lambda c: _slab_spec(c, steps, lambda b, l: b * nl + l)
    out_shapes = [
        jax.ShapeDtypeStruct((bsz, seq, d), F32),
        jax.ShapeDtypeStruct((bsz, HG_HEADS, HG_D, HG_D), F32),
        jax.ShapeDtypeStruct((bsz, 1, LRU_WIDTH), F32),
        jax.ShapeDtypeStruct((bsz, CONV_K - 1, LRU_WIDTH), F32),
    ] + [jax.ShapeDtypeStruct(c.shape, dt) for c, dt in zip(cast, cast_dtypes)]
    half = lambda dt: pltpu.VMEM((tl, HG_WIDTH), dt)
    xo, hg, lru, conv, *rounded = pl.pallas_call(
        functools.partial(_mixer_block_kernel, tl=tl, n_cast=len(cast)),
        out_shape=out_shapes,
        grid=(bsz, nl),
        in_specs=[
            pl.BlockSpec((1, tl, d), lambda b, l: (b, l, 0)),
            full((1, d)), stacked((d, IN_WIDTH)),
            full((3, HG_WIDTH)), full((1, HG_WIDTH)), full((CONV_K, LRU_WIDTH)),
            full((1, LRU_WIDTH)), full((LRU_WIDTH, LRU_WIDTH)), full((1, LRU_WIDTH)),
            full((LRU_WIDTH, LRU_WIDTH)), full((1, LRU_WIDTH)), full((1, LRU_WIDTH)),
            full(mp["cm"].shape), stacked((d, d)),
        ] + [slab(c) for c in cast],
        out_specs=[
            pl.BlockSpec((1, tl, d), lambda b, l: (b, l, 0)),
            pl.BlockSpec((1, HG_HEADS, HG_D, HG_D), lambda b, l: (b, 0, 0, 0)),
            pl.BlockSpec((1, 1, LRU_WIDTH), lambda b, l: (b, 0, 0)),
            pl.BlockSpec((1, CONV_K - 1, LRU_WIDTH), lambda b, l: (b, 0, 0)),
        ] + [slab(c) for c in cast],
        scratch_shapes=[
            pltpu.VMEM((HG_HEADS, HG_D, HG_D), F32),
            pltpu.VMEM((1, LRU_WIDTH), F32),
            pltpu.VMEM((tl + 8, LRU_WIDTH), F32),
            pltpu.VMEM((tl, d), BF16),
            half(F32), half(F32), half(F32), half(F32), half(F32), half(F32), half(F32),
            half(F32),
            pltpu.VMEM((tl, d), BF16),
        ],
        compiler_params=_cparams(("parallel", "arbitrary")),
        name="mixer_block",
    )(x, gain.reshape(1, d), w_in, mp["lbp"], mp["hgg"], mp["cw"], mp["cb"], mp["wra"],
      mp["bra"], mp["wix"], mp["bix"], mp["c8"], mp["cm"], w_out, *cast)
    return xo, hg, lru.reshape(bsz, LRU_WIDTH), conv, rounded


def _mixer_step_kernel(z_ref, sh_ref, sl_ref, sc_ref, lbp_ref, hgg_ref, cw_ref, cb_ref, wra_ref,
                       bra_ref, wix_ref, bix_ref, c8_ref, *rest, bb, aliased):
    m_ref, hn_ref, ln_ref, cn_ref, o_scr = rest[1:] if aliased else rest
    z = z_ref[...]
    xr = z[:, 4 * HG_WIDTH:4 * HG_WIDTH + LRU_WIDTH]
    gr = z[:, 4 * HG_WIDTH + LRU_WIDTH:IN_WIDTH]
    buf = sc_ref[...]
    xc = cb_ref[...]
    for j in range(CONV_K - 1):
        xc = xc + buf[:, j * LRU_WIDTH:(j + 1) * LRU_WIDTH] * cw_ref[j:j + 1, :]
    xc = xc + xr * cw_ref[CONV_K - 1:CONV_K, :]
    cn_ref[:, 0:(CONV_K - 2) * LRU_WIDTH] = buf[:, LRU_WIDTH:(CONV_K - 1) * LRU_WIDTH]
    cn_ref[:, (CONV_K - 2) * LRU_WIDTH:(CONV_K - 1) * LRU_WIDTH] = xr
    a, u = _lru_gates(xc, wra_ref[...], bra_ref[...], wix_ref[...], bix_ref[...], c8_ref[...])
    hnew = u + a * sl_ref[...]
    ln_ref[...] = hnew
    m_ref[:, HG_WIDTH:D_MODEL] = (hnew * _gelu_tanh(gr)).astype(m_ref.dtype)

    q = _silu(z[:, 0:HG_WIDTH])
    log_f, k = _hgrn_gates(z[:, HG_WIDTH:2 * HG_WIDTH], lbp_ref[0:1, :], lbp_ref[1:2, :],
                           lbp_ref[2:3, :])
    f = jnp.exp(log_f)
    v = z[:, 2 * HG_WIDTH:3 * HG_WIDTH]
    gate = _silu(z[:, 3 * HG_WIDTH:4 * HG_WIDTH])
    zpad = jnp.zeros((HG_D - bb, HG_D), F32)
    for hh in range(HG_HEADS):
        cs = slice(hh * HG_D, (hh + 1) * HG_D)
        ft = jnp.concatenate([f[:, cs], zpad], axis=0).T
        kt = jnp.concatenate([k[:, cs], zpad], axis=0).T
        qt = jnp.concatenate([q[:, cs], zpad], axis=0).T
        for j in range(bb):
            fcol = jnp.broadcast_to(ft[:, j:j + 1], (HG_D, HG_D))
            kcol = jnp.broadcast_to(kt[:, j:j + 1], (HG_D, HG_D))
            qcol = jnp.broadcast_to(qt[:, j:j + 1], (HG_D, HG_D))
            vrow = jnp.broadcast_to(v[j:j + 1, cs], (HG_D, HG_D))
            sn = fcol * sh_ref[j, hh] + kcol * vrow
            hn_ref[j, hh] = sn
            o_scr[j:j + 1, cs] = jnp.sum(qcol * sn, axis=0, keepdims=True)
    for hh in range(HG_HEADS):
        cs = slice(hh * HG_D, (hh + 1) * HG_D)
        o = o_scr[:, cs]
        o = o * lax.rsqrt(jnp.mean(o * o, axis=-1, keepdims=True) + EPS)
        o = o * hgg_ref[:, cs]
        o = o * gate[:, cs]
        m_ref[:, cs] = o.astype(m_ref.dtype)


def _mixer_step(z, s_hg_all, layer, s_lru, s_conv, mp, hg_buf=None, bb=16):
    bsz = z.shape[0]
    bb = min(bb, bsz)
    cw3 = (CONV_K - 1) * LRU_WIDTH
    full = lambda shape: pl.BlockSpec(shape, lambda i: (0,) * len(shape))
    state_spec = pl.BlockSpec((None, bb, HG_HEADS, HG_D, HG_D), lambda i: (layer, i, 0, 0, 0))
    out_shapes = (
        jax.ShapeDtypeStruct((bsz, D_MODEL), mp["wra"].dtype),
        jax.ShapeDtypeStruct(s_hg_all.shape, F32),
        jax.ShapeDtypeStruct((bsz, LRU_WIDTH), F32),
        jax.ShapeDtypeStruct((bsz, cw3), F32),
    )
    in_specs = [
        pl.BlockSpec((bb, IN_WIDTH), lambda i: (i, 0)),
        state_spec,
        pl.BlockSpec((bb, LRU_WIDTH), lambda i: (i, 0)),
        pl.BlockSpec((bb, cw3), lambda i: (i, 0)),
        full((3, HG_WIDTH)), full((1, HG_WIDTH)), full((CONV_K, LRU_WIDTH)),
        full((1, LRU_WIDTH)), full((LRU_WIDTH, LRU_WIDTH)), full((1, LRU_WIDTH)),
        full((LRU_WIDTH, LRU_WIDTH)), full((1, LRU_WIDTH)), full((1, LRU_WIDTH)),
    ]
    args = [z, s_hg_all, s_lru, s_conv.reshape(bsz, cw3), mp["lbp"], mp["hgg"], mp["cw"],
            mp["cb"], mp["wra"], mp["bra"], mp["wix"], mp["bix"], mp["c8"]]
    aliases = {}
    if hg_buf is not None:
        aliases = {len(args): 1}
        in_specs.append(pl.BlockSpec(memory_space=pl.ANY))
        args.append(hg_buf)
    merged, hg, lru, conv = pl.pallas_call(
        functools.partial(_mixer_step_kernel, bb=bb, aliased=hg_buf is not None),
        out_shape=out_shapes,
        grid=(bsz // bb,),
        in_specs=in_specs,
        out_specs=(
            pl.BlockSpec((bb, D_MODEL), lambda i: (i, 0)),
            state_spec,
            pl.BlockSpec((bb, LRU_WIDTH), lambda i: (i, 0)),
            pl.BlockSpec((bb, cw3), lambda i: (i, 0)),
        ),
        scratch_shapes=[pltpu.VMEM((bb, HG_WIDTH), F32)],
        input_output_aliases=aliases,
        compiler_params=_cparams(("parallel",)),
        name="mixer_step",
    )(*args)
    return merged, hg, lru, conv.reshape(bsz, CONV_K - 1, LRU_WIDTH)


def _router_kernel(x_ref, g_ref, rw_ref, tril_ref, *refs, n_cast, exact_logits):
    cast_src = refs[:n_cast]
    info_ref, idx_ref, cnt_ref, zero_ref = refs[n_cast:n_cast + 4]
    cast_dst = refs[n_cast + 4:2 * n_cast + 4]
    carry_scr = refs[-1]
    i = pl.program_id(0)

    @pl.when(i == 0)
    def _():
        carry_scr[...] = jnp.zeros_like(carry_scr)

    zero_ref[...] = jnp.zeros_like(zero_ref)
    for src, dst in zip(cast_src, cast_dst):
        dst[...] = src[...].astype(dst.dtype)

    h = _rms(x_ref[...], g_ref[...])
    rw = rw_ref[...]
    if exact_logits:
        logits = jnp.dot(h, rw, preferred_element_type=F32, precision=lax.Precision.HIGHEST)
    else:
        h1 = h.astype(BF16)
        h2 = (h - h1.astype(F32)).astype(BF16)
        r1 = rw.astype(BF16)
        r2 = (rw - r1.astype(F32)).astype(BF16)
        logits = _dot(h1, r1) + (_dot(h1, r2) + _dot(h2, r1))
    lane = lax.broadcasted_iota(I32, logits.shape, 1)
    neg = jnp.float32(-jnp.inf)
    logits = jnp.where(lane < N_EXPERTS, logits, neg)
    m1 = jnp.max(logits, axis=-1, keepdims=True)
    i1 = jnp.min(jnp.where(logits == m1, lane, LANES), axis=-1, keepdims=True)
    l2 = jnp.where(lane == i1, neg, logits)
    m2 = jnp.max(l2, axis=-1, keepdims=True)
    i2 = jnp.min(jnp.where(l2 == m2, lane, LANES), axis=-1, keepdims=True)
    e = jnp.exp(m2 - m1)
    p1 = 1.0 / (1.0 + e)
    p2 = e / (1.0 + e)
    oh1 = (lane == i1).astype(F32)
    oh2 = (lane == i2).astype(F32)
    sel = oh1 + oh2
    before = _dot(tril_ref[...], sel.astype(BF16)) + carry_scr[...]
    r1 = jnp.sum(oh1 * before, axis=-1, keepdims=True)
    r2 = jnp.sum(oh2 * before, axis=-1, keepdims=True)
    carry_scr[...] += jnp.sum(sel, axis=0, keepdims=True)
    info = jnp.where(lane == 0, i1.astype(F32), 0.0)
    info = jnp.where(lane == 1, i2.astype(F32), info)
    info = jnp.where(lane == 2, r1, info)
    info = jnp.where(lane == 3, r2, info)
    info = jnp.where(lane == 4, p1, info)
    info = jnp.where(lane == 5, p2, info)
    info_ref[...] = info
    idx_ref[...] = info.T[0:SUBLANES, :].astype(I32)
    cnt_ref[...] = carry_scr[...]


def _router(x, gain, rw_pad, n_rows, cast=(), exact_logits=True, tm=512):
    t, d = x.shape
    tm = min(tm, t)
    steps = t // tm
    zrows = n_rows // steps
    assert zrows * steps == n_rows and zrows % SUBLANES == 0
    slab = lambda c: _slab_spec(c, steps, lambda i: i)
    tril = jnp.tril(jnp.ones((tm, tm), BF16), -1)
    return pl.pallas_call(
        functools.partial(_router_kernel, n_cast=len(cast), exact_logits=exact_logits),
        out_shape=[jax.ShapeDtypeStruct((t, LANES), F32),
                   jax.ShapeDtypeStruct((SUBLANES, t), I32),
                   jax.ShapeDtypeStruct((1, LANES), F32),
                   jax.ShapeDtypeStruct((n_rows, d), F32)]
        + [jax.ShapeDtypeStruct(c.shape, BF16) for c in cast],
        grid=(steps,),
        in_specs=[
            pl.BlockSpec((tm, d), lambda i: (i, 0)),
            pl.BlockSpec((1, d), lambda i: (0, 0)),
            pl.BlockSpec((d, LANES), lambda i: (0, 0)),
            pl.BlockSpec((tm, tm), lambda i: (0, 0)),
        ] + [slab(c) for c in cast],
        out_specs=[pl.BlockSpec((tm, LANES), lambda i: (i, 0)),
                   pl.BlockSpec((SUBLANES, tm), lambda i: (0, i)),
                   pl.BlockSpec((1, LANES), lambda i: (0, 0)),
                   pl.BlockSpec((zrows, d), lambda i: (i, 0))] + [slab(c) for c in cast],
        scratch_shapes=[pltpu.VMEM((1, LANES), F32)],
        compiler_params=_cparams(("arbitrary",)),
        name="router",
    )(x, gain.reshape(1, d), rw_pad, tril, *cast)


def _row_copy(src, dst, s, d, sem):
    return pltpu.make_async_copy(src.at[pl.ds(s, 1)], dst.at[pl.ds(d, 1)], sem)


def _tile_dest(dest, t, td):
    nb = t // td
    dest3 = jnp.concatenate([dest[0].reshape(nb, td), dest[1].reshape(nb, td)], axis=1)
    return dest3.reshape(nb, 1, 2 * td)


def _dispatch_kernel(dest_ref, x_ref, g_ref, xs_in_ref, xs_ref, pk_scr, sem, *, td):
    del xs_in_ref
    pk_scr[...] = _rms(x_ref[...], g_ref[...])

    def issue(r, c):
        _row_copy(pk_scr, xs_ref, r, dest_ref[0, 0, r], sem).start(priority=0)
        _row_copy(pk_scr, xs_ref, r, dest_ref[0, 0, td + r], sem).start(priority=1)
        return c

    lax.fori_loop(0, td, issue, 0, unroll=8)
    for _ in range(2):
        pltpu.make_async_copy(pk_scr, xs_ref.at[pl.ds(0, td)], sem).wait()


def _dispatch(x, gain, dest, xs0, td=512):
    t, d = x.shape
    td = min(td, t)
    n_rows = xs0.shape[0]
    return pl.pallas_call(
        functools.partial(_dispatch_kernel, td=td),
        out_shape=jax.ShapeDtypeStruct((n_rows, d), F32),
        grid=(t // td,),
        in_specs=[
            pl.BlockSpec((1, 1, 2 * td), lambda i: (i, 0, 0), memory_space=pltpu.SMEM),
            pl.BlockSpec((td, d), lambda i: (i, 0)),
            pl.BlockSpec((1, d), lambda i: (0, 0)),
            pl.BlockSpec(memory_space=pl.ANY),
        ],
        out_specs=pl.BlockSpec(memory_space=pl.ANY),
        scratch_shapes=[pltpu.VMEM((td, d), F32), pltpu.SemaphoreType.DMA(())],
        input_output_aliases={3: 0},
        compiler_params=pltpu.CompilerParams(dimension_semantics=("arbitrary",),
                                             vmem_limit_bytes=VMEM_LIMIT_BYTES,
                                             has_side_effects=True),
        name="dispatch",
    )(_tile_dest(dest, t, td), x, gain.reshape(1, d), xs0)


def _combine_kernel(dest_ref, nxt_ref, x_ref, info_ref, g_ref, ys_ref, o_ref, y_scr, sems, *, td):
    i = pl.program_id(0)
    nb = pl.num_programs(0)

    def fetch(idx_ref, slot):
        def issue(r, c):
            _row_copy(ys_ref, y_scr.at[slot, 0], idx_ref[0, 0, r], r,
                      sems.at[slot]).start(priority=0)
            _row_copy(ys_ref, y_scr.at[slot, 1], idx_ref[0, 0, td + r], r,
                      sems.at[slot]).start(priority=1)
            return c

        lax.fori_loop(0, td, issue, 0, unroll=8)

    def finish(slot):
        for j in range(2):
            pltpu.make_async_copy(ys_ref.at[pl.ds(0, td)], y_scr.at[slot, j],
                                  sems.at[slot]).wait()
        p1 = info_ref[:, 4:5]
        p2 = info_ref[:, 5:6]
        x = x_ref[...] + (p1 * y_scr[slot, 0] + p2 * y_scr[slot, 1])
        o_ref[...] = _rms(x, g_ref[...])

    @pl.when(i == 0)
    def _():
        fetch(dest_ref, 0)

    for slot in range(2):
        @pl.when(i % 2 == slot)
        def _(slot=slot):
            @pl.when(i + 1 < nb)
            def _():
                fetch(nxt_ref, 1 - slot)

            finish(slot)


def _combine(x, ys, dest, info, gain, td=256):
    t, d = x.shape
    td = min(td, t)
    nb = t // td
    row = pl.BlockSpec((td, d), lambda i: (i, 0))
    tiled = _tile_dest(dest, t, td)
    return pl.pallas_call(
        functools.partial(_combine_kernel, td=td),
        out_shape=jax.ShapeDtypeStruct((t, d), F32),
        grid=(nb,),
        in_specs=[
            pl.BlockSpec((1, 1, 2 * td), lambda i: (i, 0, 0), memory_space=pltpu.SMEM),
            pl.BlockSpec((1, 1, 2 * td), lambda i: (jnp.minimum(i + 1, nb - 1), 0, 0),
                         memory_space=pltpu.SMEM),
            row,
            pl.BlockSpec((td, LANES), lambda i: (i, 0)),
            pl.BlockSpec((1, d), lambda i: (0, 0)),
            pl.BlockSpec(memory_space=pl.ANY),
        ],
        out_specs=row,
        scratch_shapes=[pltpu.VMEM((2, 2, td, d), F32), pltpu.SemaphoreType.DMA((2,))],
        compiler_params=_cparams(("arbitrary",)),
        name="combine",
    )(tiled, tiled, x, info, gain.reshape(1, d), ys)


def _rows2d(w):
    return w.reshape(-1, w.shape[-1])


def _moe(xs_in, norm_g, rw_pad, experts, shared, final_g, tm_e=TILE_ROWS):
    d = xs_in[0].shape[1]
    n_tiles = (2 * sum(x.shape[0] for x in xs_in) + N_EXPERTS * (tm_e - 1)) // tm_e
    n_rows = n_tiles * tm_e
    routed = []
    xs0 = None
    for g, x in enumerate(xs_in):
        todo = [(name, w) for name, w in experts if name not in shared]
        steps = x.shape[0] // min(TILE_ROWS, x.shape[0])
        info, idx, cnt, zeros, *rounded = _router(
            x, norm_g, rw_pad, n_rows if g == 0 else steps * SUBLANES,
            cast=[_rows2d(w) for _, w in todo], exact_logits=x.shape[0] <= TILE_ROWS)
        for (name, w), r in zip(todo, rounded):
            shared[name] = r.reshape(w.shape)
        xs0 = zeros if g == 0 else xs0
        routed.append((info, idx, cnt[0, :N_EXPERTS].astype(I32)))
    w1, w3, w2 = (shared[name] for name, _ in experts)
    counts = sum(c for _, _, c in routed)
    padded = ((counts + tm_e - 1) // tm_e) * tm_e
    ends = jnp.cumsum(padded)
    starts = ends - padded
    tile_start = jnp.arange(n_tiles, dtype=I32) * tm_e
    tile_expert = jnp.minimum(
        jnp.sum((tile_start[:, None] >= ends[None, :]).astype(I32), axis=1), N_EXPERTS - 1)
    n_valid = (ends[-1] // tm_e).reshape(1).astype(I32)
    dests = []
    base = starts
    for info, idx, cnt in routed:
        e1, e2, r1, r2 = idx[0], idx[1], idx[2], idx[3]
        dests.append(jnp.stack([base[e1] + r1, base[e2] + r2]))
        base = base + cnt
    rows = xs0
    for x, dest in zip(xs_in, dests):
        rows = _dispatch(x, norm_g, dest, rows)
    ys = _ffn(rows, w1, w3, w2, tile_expert.astype(I32), n_valid, tm=tm_e, n_f=1)
    return [_combine(x, ys, dest, info, final_g)
            for x, dest, (info, _, _) in zip(xs_in, dests, routed)]


def _block_diag(w):
    n, c, d = w.shape
    eye = jnp.eye(n, dtype=w.dtype)
    return (w[:, :, None, :] * eye[:, None, :, None]).reshape(n * c, n * d)


def _cum_matrix():
    t = jnp.arange(CHUNK)
    return (t[None, :] <= t[:, None]).astype(BF16)


def _mixer_params(l, lb, hg_norm_g, conv_w, conv_b, w_ra, b_ra, w_ix, b_ix, lru_lambda, wdtype):
    row = lambda a: a.reshape(1, -1).astype(F32)
    return {
        "lbp": jnp.stack([jnp.log(lb), jnp.log1p(-lb), 1.0 - lb]).astype(F32),
        "hgg": row(hg_norm_g[l]),
        "cw": conv_w[l].astype(F32),
        "cb": row(conv_b[l]),
        "wra": _block_diag(w_ra[l].astype(F32)).astype(wdtype),
        "bra": row(b_ra[l]),
        "wix": _block_diag(w_ix[l].astype(F32)).astype(wdtype),
        "bix": row(b_ix[l]),
        "c8": row(-LRU_C * jax.nn.softplus(-lru_lambda[l].astype(F32))),
        "cm": _cum_matrix(),
    }


def _trunk(x, seq_shape, states, prm):
    bsz, seq = seq_shape
    one = jnp.ones((1,), I32)
    new_hg, new_lru, new_conv = [], [], []
    shared = prm["shared"]
    hg_stack = None if states is None else shared.get("hg_seed", states[0])
    for l in range(2):
        mp = prm["mixer"][l]
        if states is None:
            x3 = x.reshape(bsz, seq, D_MODEL)
            jobs = [j for j in prm["side"][l] if j[0] not in shared]
            if l == 0:
                w_in, w_out, layer = prm["w_in"], prm["w_out"], 0
            else:
                w_in, w_out, layer = shared["w_in"], shared["w_out"], 1
            x, s1, s2, s3, done = _mixer_block(
                x3, prm["norm1_g"][l], w_in, w_out, layer, mp,
                cast=[(_rows2d(a), dt) for _, a, dt in jobs])
            for (name, a, _), r in zip(jobs, done):
                shared[name] = r.reshape(a.shape)
            x = x.reshape(bsz * seq, D_MODEL)
            new_hg.append(s1)
        else:
            z = _linear(x, prm["w_in"], l, gain=prm["norm1_g"][l])
            merged, hg_stack, s2, s3 = _mixer_step(z, states[0], l, states[1][l], states[2][l],
                                                   mp, hg_buf=hg_stack)
            x = _linear(merged, prm["w_out"], l, res=x)
        new_lru.append(s2)
        new_conv.append(s3)
        if l == 0:
            n_tiles = max(x.shape[0] // TILE_ROWS, 1)
            if states is None:
                f1, f3, f2 = shared["ffn_w1"], shared["ffn_w3"], shared["ffn_w2"]
            else:
                f1, f3, f2 = prm["ffn_w1"], prm["ffn_w3"], prm["ffn_w2"]
            n_f = 1 if f1.dtype == BF16 else f1.shape[2] // F_CHUNK
            x = _ffn(x, f1, f3, f2, jnp.zeros((n_tiles,), I32), one * n_tiles,
                     gain=prm["norm2_g"][0], res=True, tm=TILE_ROWS, n_f=n_f, ck=F_CHUNK)
    hg_all = jnp.stack(new_hg) if states is None else hg_stack
    return x, hg_all, jnp.stack(new_lru), jnp.stack(new_conv)


def kernel(x_prompt, x_sample, state_hgrn, state_lru, state_conv, norm1_g, w_in, lower_bounds,
           hg_norm_g, conv_w, conv_b, w_ra, b_ra, w_ix, b_ix, lru_lambda, w_out, norm2_g,
           ffn_w1, ffn_w3, ffn_w2, router_w, moe_w1, moe_w3, moe_w2, final_norm_g):
    lb_all = jnp.cumsum(jax.nn.softmax(lower_bounds.astype(F32), axis=0), axis=0)
    lb_all = lb_all - lb_all[0]
    f32 = lambda a: a.astype(F32)
    common = {
        "experts": [("moe_w1", f32(moe_w1[0])), ("moe_w3", f32(moe_w3[0])),
                    ("moe_w2", f32(moe_w2[0]))],
        "side": {0: [("moe_w3", f32(moe_w3[0]), BF16), ("ffn_w1", f32(ffn_w1), BF16),
                     ("ffn_w3", f32(ffn_w3), BF16), ("ffn_w2", f32(ffn_w2), BF16),
                     ("w_in", f32(w_in), BF16), ("w_out", f32(w_out), BF16)],
                 1: [("moe_w2", f32(moe_w2[0]), BF16), ("hg_seed", f32(state_hgrn), F32)]},
        "shared": {},
        "rw_pad": jnp.pad(router_w[0].astype(F32), ((0, 0), (0, LANES - N_EXPERTS))),
        "norm1_g": norm1_g, "norm2_g": norm2_g, "final_norm_g": final_norm_g,
    }
    mixer = lambda wdtype: [_mixer_params(l, lb_all[l], hg_norm_g, conv_w, conv_b, w_ra, b_ra,
                                          w_ix, b_ix, lru_lambda, wdtype) for l in range(2)]
    prompt = dict(common, mixer=mixer(BF16), w_in=w_in[:1].astype(BF16),
                  w_out=w_out[:1].astype(BF16))

    bp, sp, d = x_prompt.shape
    bs = x_sample.shape[0]
    x_p, hg_p, lru_p, conv_p = _trunk(x_prompt.reshape(bp * sp, d), (bp, sp), None, prompt)
    sample = dict(common, mixer=mixer(F32), w_in=f32(w_in), w_out=f32(w_out),
                  ffn_w1=f32(ffn_w1), ffn_w3=f32(ffn_w3), ffn_w2=f32(ffn_w2))
    x_s, hg_s, lru_s, conv_s = _trunk(x_sample.reshape(bs, d), (bs, 1),
                                      (state_hgrn, state_lru, state_conv), sample)
    y_p, y_s = _moe([x_p, x_s], norm2_g[1], common["rw_pad"], common["experts"],
                    common["shared"], final_norm_g)
    return (y_p.reshape(bp, sp, d), y_s.reshape(bs, 1, d), hg_p, lru_p, conv_p, hg_s, lru_s,
            conv_s)
```

```python
import functools

import jax
import jax.numpy as jnp
from jax import lax
from jax.experimental import pallas as pl
from jax.experimental.pallas import tpu as pltpu

F32 = jnp.float32
BF16 = jnp.bfloat16
I32 = jnp.int32

D_MODEL = 1024
HG_WIDTH = 512
HG_HEADS = 4
HG_D = 128
LRU_WIDTH = 512
LRU_BLOCKS = 8
LRU_C = 8.0
CONV_K = 4
IN_WIDTH = 4 * HG_WIDTH + 2 * LRU_WIDTH
N_EXPERTS = 8
EPS = 1e-6

CHUNK = 128
DIAG = 1
LANES = 128
SUBLANES = 8
TILE_ROWS = 512
F_CHUNK = 256
VMEM_LIMIT_BYTES = 56 * 1024 * 1024


def _cparams(sem):
    return pltpu.CompilerParams(dimension_semantics=sem, vmem_limit_bytes=VMEM_LIMIT_BYTES)


def _exp_neg(x):
    return jnp.exp2(x * -1.4426950408889634)


def _sigmoid(x):
    return 1.0 / (1.0 + _exp_neg(x))


def _silu(x):
    return x * _sigmoid(x)


def _gelu_tanh(x):
    c = 0.7978845608028654
    return 0.5 * x * (1.0 + jnp.tanh(c * (x + 0.044715 * (x * x * x))))


def _rms(x, g):
    return x * lax.rsqrt(jnp.mean(x * x, axis=-1, keepdims=True) + EPS) * g


def _dot(a, b):
    if b.dtype == F32:
        return jnp.dot(a.astype(F32), b, preferred_element_type=F32,
                       precision=lax.Precision.HIGHEST)
    return jnp.dot(a.astype(BF16), b, preferred_element_type=F32)


def _dot_nt(a, b):
    return lax.dot_general(a, b, (((1,), (1,)), ((), ())), preferred_element_type=F32)


def _dot_tn(a, b):
    return lax.dot_general(a, b, (((0,), (0,)), ((), ())), preferred_element_type=F32)


def _split3(x):
    x1 = x.astype(BF16)
    r1 = x - x1.astype(F32)
    x2 = r1.astype(BF16)
    x3 = (r1 - x2.astype(F32)).astype(BF16)
    return x1, x2, x3


def _linear_kernel(*refs, has_gain, has_res):
    it = iter(refs)
    x_ref = next(it)
    g_ref = next(it) if has_gain else None
    w_ref = next(it)
    r_ref = next(it) if has_res else None
    o_ref = next(it)
    h_scr = next(it)

    @pl.when(pl.program_id(1) == 0)
    def _():
        x = x_ref[...].astype(F32)
        if has_gain:
            x = _rms(x, g_ref[...])
        h_scr[...] = x.astype(h_scr.dtype)

    acc = _dot(h_scr[...], w_ref[...])
    if has_res:
        acc = acc + r_ref[...]
    o_ref[...] = acc.astype(o_ref.dtype)


def _linear(x, w, layer, gain=None, res=None, tm=1024, tn=1024):
    m, k = x.shape
    n = w.shape[2]
    tm = min(tm, m)
    tn = min(tn, n)
    in_specs = [pl.BlockSpec((tm, k), lambda i, j: (i, 0))]
    args = [x]
    if gain is not None:
        in_specs.append(pl.BlockSpec((1, k), lambda i, j: (0, 0)))
        args.append(gain.reshape(1, k))
    in_specs.append(pl.BlockSpec((None, k, tn), lambda i, j: (layer, 0, j)))
    args.append(w)
    if res is not None:
        in_specs.append(pl.BlockSpec((tm, tn), lambda i, j: (i, j)))
        args.append(res)
    return pl.pallas_call(
        functools.partial(_linear_kernel, has_gain=gain is not None, has_res=res is not None),
        out_shape=jax.ShapeDtypeStruct((m, n), F32),
        grid=(m // tm, n // tn),
        in_specs=in_specs,
        out_specs=pl.BlockSpec((tm, tn), lambda i, j: (i, j)),
        scratch_shapes=[pltpu.VMEM((tm, k), w.dtype)],
        compiler_params=_cparams(("parallel", "arbitrary")),
        name="linear",
    )(*args)


def _ffn_kernel(te_ref, nv_ref, *refs, has_gain, has_res, has_fill, tf, ck):
    del te_ref
    it = iter(refs)
    x_ref = next(it)
    g_ref = next(it) if has_gain else None
    w1_ref, w3_ref, w2_ref, o_ref = next(it), next(it), next(it), next(it)
    fill_ref = next(it) if has_fill else None
    xb_scr, a_scr, acc_scr = it
    i = pl.program_id(0)
    j = pl.program_id(1)
    nj = pl.num_programs(1)
    valid = i < nv_ref[0]
    if has_fill:
        fill_ref[...] = jnp.zeros_like(fill_ref)

    @pl.when(valid)
    def _():
        @pl.when(j == 0)
        def _():
            x = x_ref[...]
            if has_gain:
                x = _rms(x, g_ref[...])
            xb_scr[...] = x.astype(xb_scr.dtype)

        xb = xb_scr[...]
        for c in range(tf // ck):
            h1 = _dot(xb, w1_ref[0, :, c * ck:(c + 1) * ck])
            h3 = _dot(xb, w3_ref[0, :, c * ck:(c + 1) * ck])
            a_scr[:, c * ck:(c + 1) * ck] = (_silu(h1) * h3).astype(a_scr.dtype)
        part = _dot(a_scr[...], w2_ref[0])

        @pl.when(j == 0)
        def _():
            acc_scr[...] = part

        @pl.when(j > 0)
        def _():
            acc_scr[...] += part

        @pl.when(j == nj - 1)
        def _():
            out = acc_scr[...]
            if has_res:
                out = x_ref[...] + out
            o_ref[...] = out

    @pl.when(jnp.logical_and(jnp.logical_not(valid), j == nj - 1))
    def _():
        o_ref[...] = jnp.zeros_like(o_ref)


def _ffn(x, w1, w3, w2, tile_expert, n_valid, gain=None, res=False, fill_rows=0, tm=TILE_ROWS,
         n_f=2, ck=F_CHUNK):
    p, d = x.shape
    f = w1.shape[2]
    tm = min(tm, p)
    tf = f // n_f
    n_tiles = p // tm
    out_specs = [pl.BlockSpec((tm, d), lambda i, j, te, nv: (i, 0))]
    out_shape = [jax.ShapeDtypeStruct((p, d), F32)]
    if fill_rows:
        assert n_f == 1 and fill_rows % (n_tiles * SUBLANES) == 0
        out_specs.append(pl.BlockSpec((fill_rows // n_tiles, d), lambda i, j, te, nv: (i, 0)))
        out_shape.append(jax.ShapeDtypeStruct((fill_rows, d), F32))
    in_specs = [pl.BlockSpec((tm, d), lambda i, j, te, nv: (i, 0))]
    args = [x]
    if gain is not None:
        in_specs.append(pl.BlockSpec((1, d), lambda i, j, te, nv: (0, 0)))
        args.append(gain.reshape(1, d))
    in_specs += [
        pl.BlockSpec((1, d, tf), lambda i, j, te, nv: (te[i], 0, j)),
        pl.BlockSpec((1, d, tf), lambda i, j, te, nv: (te[i], 0, j)),
        pl.BlockSpec((1, tf, d), lambda i, j, te, nv: (te[i], j, 0)),
    ]
    args += [w1, w3, w2]
    grid_spec = pltpu.PrefetchScalarGridSpec(
        num_scalar_prefetch=2,
        grid=(n_tiles, n_f),
        in_specs=in_specs,
        out_specs=out_specs,
        scratch_shapes=[pltpu.VMEM((tm, d), w1.dtype), pltpu.VMEM((tm, tf), w1.dtype),
                        pltpu.VMEM((tm, d), F32)],
    )
    out = pl.pallas_call(
        functools.partial(_ffn_kernel, has_gain=gain is not None, has_res=res,
                          has_fill=bool(fill_rows), tf=tf, ck=ck),
        out_shape=out_shape,
        grid_spec=grid_spec,
        compiler_params=_cparams(("arbitrary", "arbitrary")),
        name="swiglu",
    )(tile_expert, n_valid, *args)
    return out if fill_rows else out[0]


def _hgrn_gates(fz, loglb, l1mlb, oneml):
    e = _exp_neg(jnp.abs(fz))
    ope = 1.0 + e
    log_sig = jnp.minimum(fz, 0.0) - jnp.log(ope)
    cc = l1mlb + log_sig
    log_f = jnp.maximum(loglb, cc) + jnp.log(1.0 + _exp_neg(jnp.abs(loglb - cc)))
    k = oneml * (jnp.where(fz > 0.0, e, 1.0) / ope)
    return log_f, k


def _level_reference(b, w):
    n = b.shape[0]
    if w >= 4:
        return jnp.concatenate(
            [jnp.broadcast_to(b[j + w:j + w + 1, :], (2 * w, b.shape[1]))
             for j in range(0, n, 2 * w)], axis=0)
    b3 = b.reshape(n // SUBLANES, SUBLANES, b.shape[1])
    ahead = lambda s: pltpu.roll(b3, SUBLANES - s, 1)
    pos = lax.broadcasted_iota(I32, (1, SUBLANES, 1), 1) & (2 * w - 1)
    if w == 2:
        r3 = jnp.where(pos == 0, ahead(2),
                       jnp.where(pos == 1, ahead(1),
                                 jnp.where(pos == 2, b3, pltpu.roll(b3, 1, 1))))
    else:
        r3 = jnp.where(pos == 0, ahead(1), b3)
    return r3.reshape(b.shape)


def _lru_gates(xc, wra, bra, wix, bix, c8):
    r = _sigmoid(_dot(xc, wra) + bra)
    ig = _sigmoid(_dot(xc, wix) + bix)
    log_a = c8 * r
    a = jnp.exp(log_a)
    one_m_a2 = -jnp.tanh(log_a) * (a * a + 1.0)
    u = jnp.sqrt(one_m_a2) * (ig * xc)
    return a, u


def _mixer_block_kernel(x_ref, g1_ref, win_ref, lbp_ref, hgg_ref, cw_ref, cb_ref, wra_ref,
                        bra_ref, wix_ref, bix_ref, c8_ref, cm_ref, wout_ref, *rest, tl, n_cast):
    cast_src = rest[:n_cast]
    xo_ref, hs_ref, ls_ref, cs_ref = rest[n_cast:n_cast + 4]
    cast_dst = rest[n_cast + 4:2 * n_cast + 4]
    (st_scr, h_scr, xpad_scr, hb_scr, q_scr, lf_scr, k_scr, v_scr, g_scr, gl_scr, a_scr, u_scr,
     m_scr) = rest[2 * n_cast + 4:]
    l = pl.program_id(1)
    nl = pl.num_programs(1)
    for src, dst in zip(cast_src, cast_dst):
        dst[...] = src[...].astype(dst.dtype)

    @pl.when(l == 0)
    def _():
        st_scr[...] = jnp.zeros_like(st_scr)
        h_scr[...] = jnp.zeros_like(h_scr)
        xpad_scr[0:8, :] = jnp.zeros((8, LRU_WIDTH), F32)

    hb_scr[...] = _rms(x_ref[0], g1_ref[...]).astype(BF16)
    hb = hb_scr[...]
    zg = lambda j: _dot(hb, win_ref[:, j * HG_WIDTH:(j + 1) * HG_WIDTH])
    q_scr[...] = _silu(zg(0))
    log_f, kk = _hgrn_gates(zg(1), lbp_ref[0:1, :], lbp_ref[1:2, :], lbp_ref[2:3, :])
    lf_scr[...] = log_f
    k_scr[...] = kk
    v_scr[...] = zg(2)
    g_scr[...] = _silu(zg(3))
    xpad_scr[8:8 + tl, :] = zg(4)
    gl_scr[...] = _gelu_tanh(zg(5))
    xc = cb_ref[...]
    for j in range(CONV_K):
        xc = xc + xpad_scr[5 + j:5 + j + tl, :] * cw_ref[j:j + 1, :]
    tail = xpad_scr[tl + 5:tl + 8, :]
    xpad_scr[5:8, :] = tail
    a, u = _lru_gates(xc, wra_ref[...], bra_ref[...], wix_ref[...], bix_ref[...], c8_ref[...])
    a_scr[...] = a
    u_scr[...] = u

    row = lax.broadcasted_iota(I32, (CHUNK, 1), 0)
    ti = lax.broadcasted_iota(I32, (CHUNK, CHUNK), 0)
    si = lax.broadcasted_iota(I32, (CHUNK, CHUNK), 1)
    t3 = lax.broadcasted_iota(I32, (1, DIAG, 1), 1)
    sub3 = lax.broadcasted_iota(I32, (1, SUBLANES, 1), 1)
    nblk = CHUNK // DIAG
    ngrp = CHUNK // SUBLANES
    levels = []
    w = CHUNK // 2
    while w >= DIAG:
        sh = (2 * w).bit_length() - 1
        pairs = ((ti >> sh) == (si >> sh)) & ((ti & (2 * w - 1)) >= w) & ((si & (2 * w - 1)) < w)
        levels.append((w, (row & (2 * w - 1)) >= w, pairs))
        w //= 2

    def chunk_body(c, carry):
        r0 = pl.multiple_of(c * CHUNK, CHUNK)
        rows = pl.ds(r0, CHUNK)

        for gi in range(LRU_WIDTH // LANES):
            cs = slice(gi * LANES, (gi + 1) * LANES)
            aa = a_scr[rows, cs].reshape(ngrp, SUBLANES, LANES)
            uu = u_scr[rows, cs].reshape(ngrp, SUBLANES, LANES)
            d = 1
            while d < SUBLANES:
                keep = sub3 >= d
                a_sh = pltpu.roll(aa, d, 1)
                u_sh = pltpu.roll(uu, d, 1)
                uu = jnp.where(keep, aa * u_sh + uu, uu)
                aa = jnp.where(keep, aa * a_sh, aa)
                d *= 2
            hprev = h_scr[:, cs]
            groups = []
            for j in range(ngrp):
                hj = uu[j] + aa[j] * hprev
                groups.append(hj)
                hprev = hj[SUBLANES - 1:SUBLANES, :]
            hs = jnp.concatenate(groups, axis=0)
            h_scr[:, cs] = hprev
            m_scr[rows, HG_WIDTH + gi * LANES:HG_WIDTH + (gi + 1) * LANES] = (
                hs * gl_scr[rows, cs]).astype(BF16)

        q = q_scr[rows, :]
        k = k_scr[rows, :]
        v = v_scr[rows, :]
        f1, f2, f3 = _split3(lf_scr[rows, :])
        cm = cm_ref[...]
        b = _dot(cm, f1) + _dot(cm, f2) + _dot(cm, f3)

        for hh in range(HG_HEADS):
            cs = slice(hh * HG_D, (hh + 1) * HG_D)
            qh, kh, vh, bh = q[:, cs], k[:, cs], v[:, cs], b[:, cs]
            vb = vh.astype(BF16)
            amat = jnp.zeros((CHUNK, CHUNK), F32)
            for w, upper, pairs in levels:
                dec = _exp_neg(jnp.abs(bh - _level_reference(bh, w)))
                y = (jnp.where(upper, qh, kh) * dec).astype(BF16)
                amat = jnp.where(pairs, _dot_nt(y, y), amat)
            o = _dot(amat.astype(BF16), vb)
            st = st_scr[hh]
            o = o + _dot_nt((qh * jnp.exp(bh)).astype(BF16), st.astype(BF16))
            if DIAG == 1:
                o = o + jnp.sum(qh * kh, axis=-1, keepdims=True) * vh
            else:
                q3 = qh.reshape(nblk, DIAG, HG_D)
                k3 = kh.reshape(nblk, DIAG, HG_D)
                v3 = vh.reshape(nblk, DIAG, HG_D)
                b3 = bh.reshape(nblk, DIAG, HG_D)
                od = jnp.zeros((nblk, DIAG, HG_D), F32)
                for s in range(DIAG):
                    dec = jnp.exp(jnp.minimum(b3 - b3[:, s:s + 1, :], 0.0))
                    p = q3 * (k3[:, s:s + 1, :] * dec)
                    rs = jnp.sum(p, axis=-1, keepdims=True)
                    rs = jnp.where(t3 >= s, rs, 0.0)
                    od = od + rs * v3[:, s:s + 1, :]
                o = o + od.reshape(CHUNK, HG_D)
            bl = bh[CHUNK - 1:CHUNK, :]
            kdec = (kh * jnp.exp(bl - bh)).astype(BF16)
            st_scr[hh] = st * jnp.exp(bl) + _dot_tn(vb, kdec)
            o = o * lax.rsqrt(jnp.mean(o * o, axis=-1, keepdims=True) + EPS)
            o = o * hgg_ref[:, cs]
            o = o * g_scr[rows, cs]
            m_scr[rows, cs] = o.astype(BF16)
        return carry

    lax.fori_loop(0, tl // CHUNK, chunk_body, 0, unroll=tl // CHUNK)
    xo_ref[0] = x_ref[0] + _dot(m_scr[...], wout_ref[...])

    @pl.when(l == nl - 1)
    def _():
        for hh in range(HG_HEADS):
            hs_ref[0, hh] = st_scr[hh].T
        ls_ref[0] = h_scr[...]
        cs_ref[0] = xpad_scr[5:8, :]


def _slab_spec(c, steps, step_of):
    share = 1
    while (c.shape[0] * share) % (steps * 2 * SUBLANES):
        share *= 2
    assert steps % share == 0
    rows = c.shape[0] * share // steps
    return pl.BlockSpec((rows, c.shape[1]), lambda *ids: (step_of(*ids) // share, 0))


def _mixer_block(x, gain, w_in, w_out, layer, mp, cast=(), tl=512):
    bsz, seq, d = x.shape
    tl = min(tl, seq)
    nl = seq // tl
    steps = bsz * nl
    cast_dtypes = [dt for _, dt in cast]
    cast = [c for c, _ in cast]
    full = lambda shape: pl.BlockSpec(shape, lambda b, l: (0,) * len(shape))
    stacked = lambda shape: pl.BlockSpec((None,) + shape, lambda b, l: (layer, 0, 0))
    slab = lambda c: _slab_spec(c, steps, lambda b, l: b * nl + l)
    out_shapes = [
        jax.ShapeDtypeStruct((bsz, seq, d), F32),
        jax.ShapeDtypeStruct((bsz, HG_HEADS, HG_D, HG_D), F32),
        jax.ShapeDtypeStruct((bsz, 1, LRU_WIDTH), F32),
        jax.ShapeDtypeStruct((bsz, CONV_K - 1, LRU_WIDTH), F32),
    ] + [jax.ShapeDtypeStruct(c.shape, dt) for c, dt in zip(cast, cast_dtypes)]
    half = lambda dt: pltpu.VMEM((tl, HG_WIDTH), dt)
    xo, hg, lru, conv, *rounded = pl.pallas_call(
        functools.partial(_mixer_block_kernel, tl=tl, n_cast=len(cast)),
        out_shape=out_shapes,
        grid=(bsz, nl),
        in_specs=[
            pl.BlockSpec((1, tl, d), lambda b, l: (b, l, 0)),
            full((1, d)), stacked((d, IN_WIDTH)),
            full((3, HG_WIDTH)), full((1, HG_WIDTH)), full((CONV_K, LRU_WIDTH)),
            full((1, LRU_WIDTH)), full((LRU_WIDTH, LRU_WIDTH)), full((1, LRU_WIDTH)),
            full((LRU_WIDTH, LRU_WIDTH)), full((1, LRU_WIDTH)), full((1, LRU_WIDTH)),
            full(mp["cm"].shape), stacked((d, d)),
        ] + [slab(c) for c in cast],
        out_specs=[
            pl.BlockSpec((1, tl, d), lambda b, l: (b, l, 0)),
            pl.BlockSpec((1, HG_HEADS, HG_D, HG_D), lambda b, l: (b, 0, 0, 0)),
            pl.BlockSpec((1, 1, LRU_WIDTH), lambda b, l: (b, 0, 0)),
            pl.BlockSpec((1, CONV_K - 1, LRU_WIDTH), lambda b, l: (b, 0, 0)),
        ] + [slab(c) for c in cast],
        scratch_shapes=[
            pltpu.VMEM((HG_HEADS, HG_D, HG_D), F32),
            pltpu.VMEM((1, LRU_WIDTH), F32),
            pltpu.VMEM((tl + 8, LRU_WIDTH), F32),
            pltpu.VMEM((tl, d), BF16),
            half(F32), half(F32), half(F32), half(F32), half(F32), half(F32), half(F32),
            half(F32),
            pltpu.VMEM((tl, d), BF16),
        ],
        compiler_params=_cparams(("parallel", "arbitrary")),
        name="mixer_block",
    )(x, gain.reshape(1, d), w_in, mp["lbp"], mp["hgg"], mp["cw"], mp["cb"], mp["wra"],
      mp["bra"], mp["wix"], mp["bix"], mp["c8"], mp["cm"], w_out, *cast)
    return xo, hg, lru.reshape(bsz, LRU_WIDTH), conv, rounded


def _mixer_step_kernel(z_ref, sh_ref, sl_ref, sc_ref, lbp_ref, hgg_ref, cw_ref, cb_ref, wra_ref,
                       bra_ref, wix_ref, bix_ref, c8_ref, *rest, bb, aliased):
    m_ref, hn_ref, ln_ref, cn_ref, o_scr = rest[1:] if aliased else rest
    z = z_ref[...]
    xr = z[:, 4 * HG_WIDTH:4 * HG_WIDTH + LRU_WIDTH]
    gr = z[:, 4 * HG_WIDTH + LRU_WIDTH:IN_WIDTH]
    buf = sc_ref[...]
    xc = cb_ref[...]
    for j in range(CONV_K - 1):
        xc = xc + buf[:, j * LRU_WIDTH:(j + 1) * LRU_WIDTH] * cw_ref[j:j + 1, :]
    xc = xc + xr * cw_ref[CONV_K - 1:CONV_K, :]
    cn_ref[:, 0:(CONV_K - 2) * LRU_WIDTH] = buf[:, LRU_WIDTH:(CONV_K - 1) * LRU_WIDTH]
    cn_ref[:, (CONV_K - 2) * LRU_WIDTH:(CONV_K - 1) * LRU_WIDTH] = xr
    a, u = _lru_gates(xc, wra_ref[...], bra_ref[...], wix_ref[...], bix_ref[...], c8_ref[...])
    hnew = u + a * sl_ref[...]
    ln_ref[...] = hnew
    m_ref[:, HG_WIDTH:D_MODEL] = (hnew * _gelu_tanh(gr)).astype(m_ref.dtype)

    q = _silu(z[:, 0:HG_WIDTH])
    log_f, k = _hgrn_gates(z[:, HG_WIDTH:2 * HG_WIDTH], lbp_ref[0:1, :], lbp_ref[1:2, :],
                           lbp_ref[2:3, :])
    f = jnp.exp(log_f)
    v = z[:, 2 * HG_WIDTH:3 * HG_WIDTH]
    gate = _silu(z[:, 3 * HG_WIDTH:4 * HG_WIDTH])
    zpad = jnp.zeros((HG_D - bb, HG_D), F32)
    for hh in range(HG_HEADS):
        cs = slice(hh * HG_D, (hh + 1) * HG_D)
        ft = jnp.concatenate([f[:, cs], zpad], axis=0).T
        kt = jnp.concatenate([k[:, cs], zpad], axis=0).T
        qt = jnp.concatenate([q[:, cs], zpad], axis=0).T
        for j in range(bb):
            fcol = jnp.broadcast_to(ft[:, j:j + 1], (HG_D, HG_D))
            kcol = jnp.broadcast_to(kt[:, j:j + 1], (HG_D, HG_D))
            qcol = jnp.broadcast_to(qt[:, j:j + 1], (HG_D, HG_D))
            vrow = jnp.broadcast_to(v[j:j + 1, cs], (HG_D, HG_D))
            sn = fcol * sh_ref[j, hh] + kcol * vrow
            hn_ref[j, hh] = sn
            o_scr[j:j + 1, cs] = jnp.sum(qcol * sn, axis=0, keepdims=True)
    for hh in range(HG_HEADS):
        cs = slice(hh * HG_D, (hh + 1) * HG_D)
        o = o_scr[:, cs]
        o = o * lax.rsqrt(jnp.mean(o * o, axis=-1, keepdims=True) + EPS)
        o = o * hgg_ref[:, cs]
        o = o * gate[:, cs]
        m_ref[:, cs] = o.astype(m_ref.dtype)


def _mixer_step(z, s_hg_all, layer, s_lru, s_conv, mp, hg_buf=None, bb=16):
    bsz = z.shape[0]
    bb = min(bb, bsz)
    cw3 = (CONV_K - 1) * LRU_WIDTH
    full = lambda shape: pl.BlockSpec(shape, lambda i: (0,) * len(shape))
    state_spec = pl.BlockSpec((None, bb, HG_HEADS, HG_D, HG_D), lambda i: (layer, i, 0, 0, 0))
    out_shapes = (
        jax.ShapeDtypeStruct((bsz, D_MODEL), mp["wra"].dtype),
        jax.ShapeDtypeStruct(s_hg_all.shape, F32),
        jax.ShapeDtypeStruct((bsz, LRU_WIDTH), F32),
        jax.ShapeDtypeStruct((bsz, cw3), F32),
    )
    in_specs = [
        pl.BlockSpec((bb, IN_WIDTH), lambda i: (i, 0)),
        state_spec,
        pl.BlockSpec((bb, LRU_WIDTH), lambda i: (i, 0)),
        pl.BlockSpec((bb, cw3), lambda i: (i, 0)),
        full((3, HG_WIDTH)), full((1, HG_WIDTH)), full((CONV_K, LRU_WIDTH)),
        full((1, LRU_WIDTH)), full((LRU_WIDTH, LRU_WIDTH)), full((1, LRU_WIDTH)),
        full((LRU_WIDTH, LRU_WIDTH)), full((1, LRU_WIDTH)), full((1, LRU_WIDTH)),
    ]
    args = [z, s_hg_all, s_lru, s_conv.reshape(bsz, cw3), mp["lbp"], mp["hgg"], mp["cw"],
            mp["cb"], mp["wra"], mp["bra"], mp["wix"], mp["bix"], mp["c8"]]
    aliases = {}
    if hg_buf is not None:
        aliases = {len(args): 1}
        in_specs.append(pl.BlockSpec(memory_space=pl.ANY))
        args.append(hg_buf)
    merged, hg, lru, conv = pl.pallas_call(
        functools.partial(_mixer_step_kernel, bb=bb, aliased=hg_buf is not None),
        out_shape=out_shapes,
        grid=(bsz // bb,),
        in_specs=in_specs,
        out_specs=(
            pl.BlockSpec((bb, D_MODEL), lambda i: (i, 0)),
            state_spec,
            pl.BlockSpec((bb, LRU_WIDTH), lambda i: (i, 0)),
            pl.BlockSpec((bb, cw3), lambda i: (i, 0)),
        ),
        scratch_shapes=[pltpu.VMEM((bb, HG_WIDTH), F32)],
        input_output_aliases=aliases,
        compiler_params=_cparams(("parallel",)),
        name="mixer_step",
    )(*args)
    return merged, hg, lru, conv.reshape(bsz, CONV_K - 1, LRU_WIDTH)


def _router_kernel(x_ref, g_ref, rw_ref, tril_ref, *refs, n_cast, exact_logits):
    cast_src = refs[:n_cast]
    info_ref, idx_ref, cnt_ref, zero_ref = refs[n_cast:n_cast + 4]
    cast_dst = refs[n_cast + 4:2 * n_cast + 4]
    carry_scr = refs[-1]
    i = pl.program_id(0)

    @pl.when(i == 0)
    def _():
        carry_scr[...] = jnp.zeros_like(carry_scr)

    zero_ref[...] = jnp.zeros_like(zero_ref)
    for src, dst in zip(cast_src, cast_dst):
        dst[...] = src[...].astype(dst.dtype)

    h = _rms(x_ref[...], g_ref[...])
    rw = rw_ref[...]
    if exact_logits:
        logits = jnp.dot(h, rw, preferred_element_type=F32, precision=lax.Precision.HIGHEST)
    else:
        h1 = h.astype(BF16)
        h2 = (h - h1.astype(F32)).astype(BF16)
        r1 = rw.astype(BF16)
        r2 = (rw - r1.astype(F32)).astype(BF16)
        logits = _dot(h1, r1) + (_dot(h1, r2) + _dot(h2, r1))
    lane = lax.broadcasted_iota(I32, logits.shape, 1)
    neg = jnp.float32(-jnp.inf)
    logits = jnp.where(lane < N_EXPERTS, logits, neg)
    m1 = jnp.max(logits, axis=-1, keepdims=True)
    i1 = jnp.min(jnp.where(logits == m1, lane, LANES), axis=-1, keepdims=True)
    l2 = jnp.where(lane == i1, neg, logits)
    m2 = jnp.max(l2, axis=-1, keepdims=True)
    i2 = jnp.min(jnp.where(l2 == m2, lane, LANES), axis=-1, keepdims=True)
    e = jnp.exp(m2 - m1)
    p1 = 1.0 / (1.0 + e)
    p2 = e / (1.0 + e)
    oh1 = (lane == i1).astype(F32)
    oh2 = (lane == i2).astype(F32)
    sel = oh1 + oh2
    before = _dot(tril_ref[...], sel.astype(BF16)) + carry_scr[...]
    r1 = jnp.sum(oh1 * before, axis=-1, keepdims=True)
    r2 = jnp.sum(oh2 * before, axis=-1, keepdims=True)
    carry_scr[...] += jnp.sum(sel, axis=0, keepdims=True)
    info = jnp.where(lane == 0, i1.astype(F32), 0.0)
    info = jnp.where(lane == 1, i2.astype(F32), info)
    info = jnp.where(lane == 2, r1, info)
    info = jnp.where(lane == 3, r2, info)
    info = jnp.where(lane == 4, p1, info)
    info = jnp.where(lane == 5, p2, info)
    info_ref[...] = info
    idx_ref[...] = info.T[0:SUBLANES, :].astype(I32)
    cnt_ref[...] = carry_scr[...]


def _router(x, gain, rw_pad, n_rows, cast=(), exact_logits=True, tm=512):
    t, d = x.shape
    tm = min(tm, t)
    steps = t // tm
    zrows = n_rows // steps
    assert zrows * steps == n_rows and zrows % SUBLANES == 0
    slab = lambda c: _slab_spec(c, steps, lambda i: i)
    tril = jnp.tril(jnp.ones((tm, tm), BF16), -1)
    return pl.pallas_call(
        functools.partial(_router_kernel, n_cast=len(cast), exact_logits=exact_logits),
        out_shape=[jax.ShapeDtypeStruct((t, LANES), F32),
                   jax.ShapeDtypeStruct((SUBLANES, t), I32),
                   jax.ShapeDtypeStruct((1, LANES), F32),
                   jax.ShapeDtypeStruct((n_rows, d), F32)]
        + [jax.ShapeDtypeStruct(c.shape, BF16) for c in cast],
        grid=(steps,),
        in_specs=[
            pl.BlockSpec((tm, d), lambda i: (i, 0)),
            pl.BlockSpec((1, d), lambda i: (0, 0)),
            pl.BlockSpec((d, LANES), lambda i: (0, 0)),
            pl.BlockSpec((tm, tm), lambda i: (0, 0)),
        ] + [slab(c) for c in cast],
        out_specs=[pl.BlockSpec((tm, LANES), lambda i: (i, 0)),
                   pl.BlockSpec((SUBLANES, tm), lambda i: (0, i)),
                   pl.BlockSpec((1, LANES), lambda i: (0, 0)),
                   pl.BlockSpec((zrows, d), lambda i: (i, 0))] + [slab(c) for c in cast],
        scratch_shapes=[pltpu.VMEM((1, LANES), F32)],
        compiler_params=_cparams(("arbitrary",)),
        name="router",
    )(x, gain.reshape(1, d), rw_pad, tril, *cast)


def _row_copy(src, dst, s, d, sem):
    return pltpu.make_async_copy(src.at[pl.ds(s, 1)], dst.at[pl.ds(d, 1)], sem)


def _tile_dest(dest, t, td):
    nb = t // td
    dest3 = jnp.concatenate([dest[0].reshape(nb, td), dest[1].reshape(nb, td)], axis=1)
    return dest3.reshape(nb, 1, 2 * td)


def _dispatch_kernel(dest_ref, x_ref, g_ref, xs_in_ref, xs_ref, pk_scr, sem, *, td):
    del xs_in_ref
    pk_scr[...] = _rms(x_ref[...], g_ref[...])

    def issue(r, c):
        _row_copy(pk_scr, xs_ref, r, dest_ref[0, 0, r], sem).start(priority=0)
        _row_copy(pk_scr, xs_ref, r, dest_ref[0, 0, td + r], sem).start(priority=1)
        return c

    lax.fori_loop(0, td, issue, 0, unroll=8)
    for _ in range(2):
        pltpu.make_async_copy(pk_scr, xs_ref.at[pl.ds(0, td)], sem).wait()


def _dispatch(x, gain, dest, xs0, td=512):
    t, d = x.shape
    td = min(td, t)
    n_rows = xs0.shape[0]
    return pl.pallas_call(
        functools.partial(_dispatch_kernel, td=td),
        out_shape=jax.ShapeDtypeStruct((n_rows, d), F32),
        grid=(t // td,),
        in_specs=[
            pl.BlockSpec((1, 1, 2 * td), lambda i: (i, 0, 0), memory_space=pltpu.SMEM),
            pl.BlockSpec((td, d), lambda i: (i, 0)),
            pl.BlockSpec((1, d), lambda i: (0, 0)),
            pl.BlockSpec(memory_space=pl.ANY),
        ],
        out_specs=pl.BlockSpec(memory_space=pl.ANY),
        scratch_shapes=[pltpu.VMEM((td, d), F32), pltpu.SemaphoreType.DMA(())],
        input_output_aliases={3: 0},
        compiler_params=pltpu.CompilerParams(dimension_semantics=("arbitrary",),
                                             vmem_limit_bytes=VMEM_LIMIT_BYTES,
                                             has_side_effects=True),
        name="dispatch",
    )(_tile_dest(dest, t, td), x, gain.reshape(1, d), xs0)


def _combine_kernel(dest_ref, nxt_ref, x_ref, info_ref, g_ref, ys_ref, o_ref, y_scr, sems, *, td):
    i = pl.program_id(0)
    nb = pl.num_programs(0)

    def fetch(idx_ref, slot):
        def issue(r, c):
            _row_copy(ys_ref, y_scr.at[slot, 0], idx_ref[0, 0, r], r,
                      sems.at[slot]).start(priority=0)
            _row_copy(ys_ref, y_scr.at[slot, 1], idx_ref[0, 0, td + r], r,
                      sems.at[slot]).start(priority=1)
            return c

        lax.fori_loop(0, td, issue, 0, unroll=8)

    def finish(slot):
        for j in range(2):
            pltpu.make_async_copy(ys_ref.at[pl.ds(0, td)], y_scr.at[slot, j],
                                  sems.at[slot]).wait()
        p1 = info_ref[:, 4:5]
        p2 = info_ref[:, 5:6]
        x = x_ref[...] + (p1 * y_scr[slot, 0] + p2 * y_scr[slot, 1])
        o_ref[...] = _rms(x, g_ref[...])

    @pl.when(i == 0)
    def _():
        fetch(dest_ref, 0)

    for slot in range(2):
        @pl.when(i % 2 == slot)
        def _(slot=slot):
            @pl.when(i + 1 < nb)
            def _():
                fetch(nxt_ref, 1 - slot)

            finish(slot)


def _combine(x, ys, dest, info, gain, td=256):
    t, d = x.shape
    td = min(td, t)
    nb = t // td
    row = pl.BlockSpec((td, d), lambda i: (i, 0))
    tiled = _tile_dest(dest, t, td)
    return pl.pallas_call(
        functools.partial(_combine_kernel, td=td),
        out_shape=jax.ShapeDtypeStruct((t, d), F32),
        grid=(nb,),
        in_specs=[
            pl.BlockSpec((1, 1, 2 * td), lambda i: (i, 0, 0), memory_space=pltpu.SMEM),
            pl.BlockSpec((1, 1, 2 * td), lambda i: (jnp.minimum(i + 1, nb - 1), 0, 0),
                         memory_space=pltpu.SMEM),
            row,
            pl.BlockSpec((td, LANES), lambda i: (i, 0)),
            pl.BlockSpec((1, d), lambda i: (0, 0)),
            pl.BlockSpec(memory_space=pl.ANY),
        ],
        out_specs=row,
        scratch_shapes=[pltpu.VMEM((2, 2, td, d), F32), pltpu.SemaphoreType.DMA((2,))],
        compiler_params=_cparams(("arbitrary",)),
        name="combine",
    )(tiled, tiled, x, info, gain.reshape(1, d), ys)


def _rows2d(w):
    return w.reshape(-1, w.shape[-1])


def _moe_rows(group_sizes, tm_e=TILE_ROWS):
    return (2 * sum(group_sizes) + N_EXPERTS * (tm_e - 1)) // tm_e * tm_e


def _moe(xs_in, norm_g, rw_pad, experts, shared, final_g, tm_e=TILE_ROWS):
    n_rows = _moe_rows([x.shape[0] for x in xs_in], tm_e)
    n_tiles = n_rows // tm_e
    routed = []
    for x in xs_in:
        todo = [(name, w) for name, w in experts if name not in shared]
        steps = x.shape[0] // min(TILE_ROWS, x.shape[0])
        fill = steps * SUBLANES if "moe_rows" in shared else n_rows
        info, idx, cnt, zeros, *rounded = _router(
            x, norm_g, rw_pad, fill, cast=[_rows2d(w) for _, w in todo],
            exact_logits=x.shape[0] <= TILE_ROWS)
        for (name, w), r in zip(todo, rounded):
            shared[name] = r.reshape(w.shape)
        shared.setdefault("moe_rows", zeros)
        routed.append((info, idx, cnt[0, :N_EXPERTS].astype(I32)))
    xs0 = shared["moe_rows"]
    assert xs0.shape[0] == n_rows
    w1, w3, w2 = (shared[name] for name, _ in experts)
    counts = sum(c for _, _, c in routed)
    padded = ((counts + tm_e - 1) // tm_e) * tm_e
    ends = jnp.cumsum(padded)
    starts = ends - padded
    tile_start = jnp.arange(n_tiles, dtype=I32) * tm_e
    tile_expert = jnp.minimum(
        jnp.sum((tile_start[:, None] >= ends[None, :]).astype(I32), axis=1), N_EXPERTS - 1)
    n_valid = (ends[-1] // tm_e).reshape(1).astype(I32)
    dests = []
    base = starts
    for info, idx, cnt in routed:
        e1, e2, r1, r2 = idx[0], idx[1], idx[2], idx[3]
        dests.append(jnp.stack([base[e1] + r1, base[e2] + r2]))
        base = base + cnt
    rows = xs0
    for x, dest in zip(xs_in, dests):
        rows = _dispatch(x, norm_g, dest, rows)
    ys = _ffn(rows, w1, w3, w2, tile_expert.astype(I32), n_valid, tm=tm_e, n_f=1)
    return [_combine(x, ys, dest, info, final_g)
            for x, dest, (info, _, _) in zip(xs_in, dests, routed)]


def _block_diag(w):
    n, c, d = w.shape
    eye = jnp.eye(n, dtype=w.dtype)
    return (w[:, :, None, :] * eye[:, None, :, None]).reshape(n * c, n * d)


def _cum_matrix():
    t = jnp.arange(CHUNK)
    return (t[None, :] <= t[:, None]).astype(BF16)


def _mixer_params(l, lb, hg_norm_g, conv_w, conv_b, w_ra, b_ra, w_ix, b_ix, lru_lambda, wdtype):
    row = lambda a: a.reshape(1, -1).astype(F32)
    return {
        "lbp": jnp.stack([jnp.log(lb), jnp.log1p(-lb), 1.0 - lb]).astype(F32),
        "hgg": row(hg_norm_g[l]),
        "cw": conv_w[l].astype(F32),
        "cb": row(conv_b[l]),
        "wra": _block_diag(w_ra[l].astype(F32)).astype(wdtype),
        "bra": row(b_ra[l]),
        "wix": _block_diag(w_ix[l].astype(F32)).astype(wdtype),
        "bix": row(b_ix[l]),
        "c8": row(-LRU_C * jax.nn.softplus(-lru_lambda[l].astype(F32))),
        "cm": _cum_matrix(),
    }


def _trunk(x, seq_shape, states, prm):
    bsz, seq = seq_shape
    one = jnp.ones((1,), I32)
    new_hg, new_lru, new_conv = [], [], []
    shared = prm["shared"]
    hg_stack = None if states is None else shared.get("hg_seed", states[0])
    for l in range(2):
        mp = prm["mixer"][l]
        if states is None:
            x3 = x.reshape(bsz, seq, D_MODEL)
            jobs = [j for j in prm["side"][l] if j[0] not in shared]
            if l == 0:
                w_in, w_out, layer = prm["w_in"], prm["w_out"], 0
            else:
                w_in, w_out, layer = shared["w_in"], shared["w_out"], 1
            x, s1, s2, s3, done = _mixer_block(
                x3, prm["norm1_g"][l], w_in, w_out, layer, mp,
                cast=[(_rows2d(a), dt) for _, a, dt in jobs])
            for (name, a, _), r in zip(jobs, done):
                shared[name] = r.reshape(a.shape)
            x = x.reshape(bsz * seq, D_MODEL)
            new_hg.append(s1)
        else:
            z = _linear(x, prm["w_in"], l, gain=prm["norm1_g"][l])
            merged, hg_stack, s2, s3 = _mixer_step(z, states[0], l, states[1][l], states[2][l],
                                                   mp, hg_buf=hg_stack)
            x = _linear(merged, prm["w_out"], l, res=x)
        new_lru.append(s2)
        new_conv.append(s3)
        if l == 0:
            n_tiles = max(x.shape[0] // TILE_ROWS, 1)
            if states is None:
                f1, f3, f2 = shared["ffn_w1"], shared["ffn_w3"], shared["ffn_w2"]
            else:
                f1, f3, f2 = prm["ffn_w1"], prm["ffn_w3"], prm["ffn_w2"]
            n_f = 1 if f1.dtype == BF16 else f1.shape[2] // F_CHUNK
            fill = prm["moe_rows"] if states is None else 0
            x = _ffn(x, f1, f3, f2, jnp.zeros((n_tiles,), I32), one * n_tiles,
                     gain=prm["norm2_g"][0], res=True, fill_rows=fill, tm=TILE_ROWS, n_f=n_f,
                     ck=F_CHUNK)
            if fill:
                x, shared["moe_rows"] = x
    hg_all = jnp.stack(new_hg) if states is None else hg_stack
    return x, hg_all, jnp.stack(new_lru), jnp.stack(new_conv)


def kernel(x_prompt, x_sample, state_hgrn, state_lru, state_conv, norm1_g, w_in, lower_bounds,
           hg_norm_g, conv_w, conv_b, w_ra, b_ra, w_ix, b_ix, lru_lambda, w_out, norm2_g,
           ffn_w1, ffn_w3, ffn_w2, router_w, moe_w1, moe_w3, moe_w2, final_norm_g):
    lb_all = jnp.cumsum(jax.nn.softmax(lower_bounds.astype(F32), axis=0), axis=0)
    lb_all = lb_all - lb_all[0]
    f32 = lambda a: a.astype(F32)
    common = {
        "experts": [("moe_w1", f32(moe_w1[0])), ("moe_w3", f32(moe_w3[0])),
                    ("moe_w2", f32(moe_w2[0]))],
        "side": {0: [("moe_w3", f32(moe_w3[0]), BF16), ("ffn_w1", f32(ffn_w1), BF16),
                     ("ffn_w3", f32(ffn_w3), BF16), ("ffn_w2", f32(ffn_w2), BF16),
                     ("w_in", f32(w_in), BF16), ("w_out", f32(w_out), BF16)],
                 1: [("moe_w2", f32(moe_w2[0]), BF16), ("hg_seed", f32(state_hgrn), F32)]},
        "shared": {},
        "moe_rows": _moe_rows([x_prompt.shape[0] * x_prompt.shape[1], x_sample.shape[0]]),
        "rw_pad": jnp.pad(router_w[0].astype(F32), ((0, 0), (0, LANES - N_EXPERTS))),
        "norm1_g": norm1_g, "norm2_g": norm2_g, "final_norm_g": final_norm_g,
    }
    mixer = lambda wdtype: [_mixer_params(l, lb_all[l], hg_norm_g, conv_w, conv_b, w_ra, b_ra,
                                          w_ix, b_ix, lru_lambda, wdtype) for l in range(2)]
    prompt = dict(common, mixer=mixer(BF16), w_in=w_in[:1].astype(BF16),
                  w_out=w_out[:1].astype(BF16))

    bp, sp, d = x_prompt.shape
    bs = x_sample.shape[0]
    x_p, hg_p, lru_p, conv_p = _trunk(x_prompt.reshape(bp * sp, d), (bp, sp), None, prompt)
    sample = dict(common, mixer=mixer(F32), w_in=f32(w_in), w_out=f32(w_out),
                  ffn_w1=f32(ffn_w1), ffn_w3=f32(ffn_w3), ffn_w2=f32(ffn_w2))
    x_s, hg_s, lru_s, conv_s = _trunk(x_sample.reshape(bs, d), (bs, 1),
                                      (state_hgrn, state_lru, state_conv), sample)
    y_p, y_s = _moe([x_p, x_s], norm2_g[1], common["rw_pad"], common["experts"],
                    common["shared"], final_norm_g)
    return (y_p.reshape(bp, sp, d), y_s.reshape(bs, 1, d), hg_p, lru_p, conv_p, hg_s, lru_s,
            conv_s)
```

```python
import functools

import jax
import jax.numpy as jnp
from jax import lax
from jax.experimental import pallas as pl
from jax.experimental.pallas import tpu as pltpu

F32 = jnp.float32
BF16 = jnp.bfloat16
I32 = jnp.int32

D_MODEL = 1024
HG_WIDTH = 512
HG_HEADS = 4
HG_D = 128
LRU_WIDTH = 512
LRU_BLOCKS = 8
LRU_C = 8.0
CONV_K = 4
IN_WIDTH = 4 * HG_WIDTH + 2 * LRU_WIDTH
N_EXPERTS = 8
EPS = 1e-6

CHUNK = 128
DIAG = 1
LANES = 128
SUBLANES = 8
TILE_ROWS = 512
F_CHUNK = 256
VMEM_LIMIT_BYTES = 56 * 1024 * 1024


def _cparams(sem):
    return pltpu.CompilerParams(dimension_semantics=sem, vmem_limit_bytes=VMEM_LIMIT_BYTES)


def _exp_neg(x):
    return jnp.exp2(x * -1.4426950408889634)


def _sigmoid(x):
    return 1.0 / (1.0 + _exp_neg(x))


def _silu(x):
    return x * _sigmoid(x)


def _gelu_tanh(x):
    c = 0.7978845608028654
    return 0.5 * x * (1.0 + jnp.tanh(c * (x + 0.044715 * (x * x * x))))


def _rms(x, g):
    return x * lax.rsqrt(jnp.mean(x * x, axis=-1, keepdims=True) + EPS) * g


def _dot(a, b):
    if b.dtype == F32:
        return jnp.dot(a.astype(F32), b, preferred_element_type=F32,
                       precision=lax.Precision.HIGHEST)
    return jnp.dot(a.astype(BF16), b, preferred_element_type=F32)


def _dot_nt(a, b):
    return lax.dot_general(a, b, (((1,), (1,)), ((), ())), preferred_element_type=F32)


def _dot_tn(a, b):
    return lax.dot_general(a, b, (((0,), (0,)), ((), ())), preferred_element_type=F32)


def _split3(x):
    x1 = x.astype(BF16)
    r1 = x - x1.astype(F32)
    x2 = r1.astype(BF16)
    x3 = (r1 - x2.astype(F32)).astype(BF16)
    return x1, x2, x3


def _linear_kernel(*refs, has_gain, has_res):
    it = iter(refs)
    x_ref = next(it)
    g_ref = next(it) if has_gain else None
    w_ref = next(it)
    r_ref = next(it) if has_res else None
    o_ref = next(it)
    h_scr = next(it)

    @pl.when(pl.program_id(1) == 0)
    def _():
        x = x_ref[...].astype(F32)
        if has_gain:
            x = _rms(x, g_ref[...])
        h_scr[...] = x.astype(h_scr.dtype)

    acc = _dot(h_scr[...], w_ref[...])
    if has_res:
        acc = acc + r_ref[...]
    o_ref[...] = acc.astype(o_ref.dtype)


def _linear(x, w, layer, gain=None, res=None, tm=1024, tn=1024):
    m, k = x.shape
    n = w.shape[2]
    tm = min(tm, m)
    tn = min(tn, n)
    in_specs = [pl.BlockSpec((tm, k), lambda i, j: (i, 0))]
    args = [x]
    if gain is not None:
        in_specs.append(pl.BlockSpec((1, k), lambda i, j: (0, 0)))
        args.append(gain.reshape(1, k))
    in_specs.append(pl.BlockSpec((None, k, tn), lambda i, j: (layer, 0, j)))
    args.append(w)
    if res is not None:
        in_specs.append(pl.BlockSpec((tm, tn), lambda i, j: (i, j)))
        args.append(res)
    return pl.pallas_call(
        functools.partial(_linear_kernel, has_gain=gain is not None, has_res=res is not None),
        out_shape=jax.ShapeDtypeStruct((m, n), F32),
        grid=(m // tm, n // tn),
        in_specs=in_specs,
        out_specs=pl.BlockSpec((tm, tn), lambda i, j: (i, j)),
        scratch_shapes=[pltpu.VMEM((tm, k), w.dtype)],
        compiler_params=_cparams(("parallel", "arbitrary")),
        name="linear",
    )(*args)


def _ffn_kernel(te_ref, nv_ref, *refs, has_gain, has_res, has_fill, tf, ck):
    del te_ref
    it = iter(refs)
    x_ref = next(it)
    g_ref = next(it) if has_gain else None
    w1_ref, w3_ref, w2_ref, o_ref = next(it), next(it), next(it), next(it)
    fill_ref = next(it) if has_fill else None
    xb_scr, a_scr, acc_scr = it
    i = pl.program_id(0)
    j = pl.program_id(1)
    nj = pl.num_programs(1)
    valid = i < nv_ref[0]
    if has_fill:
        fill_ref[...] = jnp.zeros_like(fill_ref)

    @pl.when(valid)
    def _():
        @pl.when(j == 0)
        def _():
            x = x_ref[...]
            if has_gain:
                x = _rms(x, g_ref[...])
            xb_scr[...] = x.astype(xb_scr.dtype)

        xb = xb_scr[...]
        for c in range(tf // ck):
            h1 = _dot(xb, w1_ref[0, :, c * ck:(c + 1) * ck])
            h3 = _dot(xb, w3_ref[0, :, c * ck:(c + 1) * ck])
            a_scr[:, c * ck:(c + 1) * ck] = (_silu(h1) * h3).astype(a_scr.dtype)
        part = _dot(a_scr[...], w2_ref[0])

        @pl.when(j == 0)
        def _():
            acc_scr[...] = part

        @pl.when(j > 0)
        def _():
            acc_scr[...] += part

        @pl.when(j == nj - 1)
        def _():
            out = acc_scr[...]
            if has_res:
                out = x_ref[...] + out
            o_ref[...] = out

    @pl.when(jnp.logical_and(jnp.logical_not(valid), j == nj - 1))
    def _():
        o_ref[...] = jnp.zeros_like(o_ref)


def _ffn(x, w1, w3, w2, tile_expert, n_valid, gain=None, res=False, fill_rows=0, tm=TILE_ROWS,
         n_f=2, ck=F_CHUNK):
    p, d = x.shape
    f = w1.shape[2]
    tm = min(tm, p)
    tf = f // n_f
    n_tiles = p // tm
    out_specs = [pl.BlockSpec((tm, d), lambda i, j, te, nv: (i, 0))]
    out_shape = [jax.ShapeDtypeStruct((p, d), F32)]
    if fill_rows:
        assert n_f == 1 and fill_rows % (n_tiles * SUBLANES) == 0
        out_specs.append(pl.BlockSpec((fill_rows // n_tiles, d), lambda i, j, te, nv: (i, 0)))
        out_shape.append(jax.ShapeDtypeStruct((fill_rows, d), F32))
    in_specs = [pl.BlockSpec((tm, d), lambda i, j, te, nv: (i, 0))]
    args = [x]
    if gain is not None:
        in_specs.append(pl.BlockSpec((1, d), lambda i, j, te, nv: (0, 0)))
        args.append(gain.reshape(1, d))
    in_specs += [
        pl.BlockSpec((1, d, tf), lambda i, j, te, nv: (te[i], 0, j)),
        pl.BlockSpec((1, d, tf), lambda i, j, te, nv: (te[i], 0, j)),
        pl.BlockSpec((1, tf, d), lambda i, j, te, nv: (te[i], j, 0)),
    ]
    args += [w1, w3, w2]
    grid_spec = pltpu.PrefetchScalarGridSpec(
        num_scalar_prefetch=2,
        grid=(n_tiles, n_f),
        in_specs=in_specs,
        out_specs=out_specs,
        scratch_shapes=[pltpu.VMEM((tm, d), w1.dtype), pltpu.VMEM((tm, tf), w1.dtype),
                        pltpu.VMEM((tm, d), F32)],
    )
    out = pl.pallas_call(
        functools.partial(_ffn_kernel, has_gain=gain is not None, has_res=res,
                          has_fill=bool(fill_rows), tf=tf, ck=ck),
        out_shape=out_shape,
        grid_spec=grid_spec,
        compiler_params=_cparams(("arbitrary", "arbitrary")),
        name="swiglu",
    )(tile_expert, n_valid, *args)
    return out if fill_rows else out[0]


def _hgrn_gates(fz, loglb, l1mlb, oneml):
    e = _exp_neg(jnp.abs(fz))
    ope = 1.0 + e
    log_sig = jnp.minimum(fz, 0.0) - jnp.log(ope)
    cc = l1mlb + log_sig
    log_f = jnp.maximum(loglb, cc) + jnp.log(1.0 + _exp_neg(jnp.abs(loglb - cc)))
    k = oneml * (jnp.where(fz > 0.0, e, 1.0) / ope)
    return log_f, k


def _level_reference(b, w):
    n = b.shape[0]
    if w >= 4:
        return jnp.concatenate(
            [jnp.broadcast_to(b[j + w:j + w + 1, :], (2 * w, b.shape[1]))
             for j in range(0, n, 2 * w)], axis=0)
    b3 = b.reshape(n // SUBLANES, SUBLANES, b.shape[1])
    ahead = lambda s: pltpu.roll(b3, SUBLANES - s, 1)
    pos = lax.broadcasted_iota(I32, (1, SUBLANES, 1), 1) & (2 * w - 1)
    if w == 2:
        r3 = jnp.where(pos == 0, ahead(2),
                       jnp.where(pos == 1, ahead(1),
                                 jnp.where(pos == 2, b3, pltpu.roll(b3, 1, 1))))
    else:
        r3 = jnp.where(pos == 0, ahead(1), b3)
    return r3.reshape(b.shape)


def _lru_gates(xc, wra, bra, wix, bix, c8):
    r = _sigmoid(_dot(xc, wra) + bra)
    ig = _sigmoid(_dot(xc, wix) + bix)
    log_a = c8 * r
    a = jnp.exp(log_a)
    one_m_a2 = -jnp.tanh(log_a) * (a * a + 1.0)
    u = jnp.sqrt(one_m_a2) * (ig * xc)
    return a, u


def _mixer_block_kernel(x_ref, g1_ref, win_ref, lbp_ref, hgg_ref, cw_ref, cb_ref, wra_ref,
                        bra_ref, wix_ref, bix_ref, c8_ref, cm_ref, wout_ref, *rest, tl, n_cast):
    cast_src = rest[:n_cast]
    xo_ref, hs_ref, ls_ref, cs_ref = rest[n_cast:n_cast + 4]
    cast_dst = rest[n_cast + 4:2 * n_cast + 4]
    (st_scr, h_scr, xpad_scr, hb_scr, q_scr, lf_scr, k_scr, v_scr, g_scr, gl_scr, a_scr, u_scr,
     m_scr) = rest[2 * n_cast + 4:]
    l = pl.program_id(1)
    nl = pl.num_programs(1)
    for src, dst in zip(cast_src, cast_dst):
        dst[...] = src[...].astype(dst.dtype)

    @pl.when(l == 0)
    def _():
        st_scr[...] = jnp.zeros_like(st_scr)
        h_scr[...] = jnp.zeros_like(h_scr)
        xpad_scr[0:8, :] = jnp.zeros((8, LRU_WIDTH), F32)

    hb_scr[...] = _rms(x_ref[0], g1_ref[...]).astype(BF16)
    hb = hb_scr[...]
    zg = lambda j: _dot(hb, win_ref[:, j * HG_WIDTH:(j + 1) * HG_WIDTH])
    q_scr[...] = _silu(zg(0))
    log_f, kk = _hgrn_gates(zg(1), lbp_ref[0:1, :], lbp_ref[1:2, :], lbp_ref[2:3, :])
    lf_scr[...] = log_f
    k_scr[...] = kk
    v_scr[...] = zg(2)
    g_scr[...] = _silu(zg(3))
    xpad_scr[8:8 + tl, :] = zg(4)
    gl_scr[...] = _gelu_tanh(zg(5))
    xc = cb_ref[...]
    for j in range(CONV_K):
        xc = xc + xpad_scr[5 + j:5 + j + tl, :] * cw_ref[j:j + 1, :]
    tail = xpad_scr[tl + 5:tl + 8, :]
    xpad_scr[5:8, :] = tail
    a, u = _lru_gates(xc, wra_ref[...], bra_ref[...], wix_ref[...], bix_ref[...], c8_ref[...])
    a_scr[...] = a
    u_scr[...] = u

    row = lax.broadcasted_iota(I32, (CHUNK, 1), 0)
    ti = lax.broadcasted_iota(I32, (CHUNK, CHUNK), 0)
    si = lax.broadcasted_iota(I32, (CHUNK, CHUNK), 1)
    t3 = lax.broadcasted_iota(I32, (1, DIAG, 1), 1)
    sub3 = lax.broadcasted_iota(I32, (1, SUBLANES, 1), 1)
    nblk = CHUNK // DIAG
    ngrp = CHUNK // SUBLANES
    levels = []
    w = CHUNK // 2
    while w >= DIAG:
        sh = (2 * w).bit_length() - 1
        pairs = ((ti >> sh) == (si >> sh)) & ((ti & (2 * w - 1)) >= w) & ((si & (2 * w - 1)) < w)
        levels.append((w, (row & (2 * w - 1)) >= w, pairs))
        w //= 2

    def chunk_body(c, carry):
        r0 = pl.multiple_of(c * CHUNK, CHUNK)
        rows = pl.ds(r0, CHUNK)

        for gi in range(LRU_WIDTH // LANES):
            cs = slice(gi * LANES, (gi + 1) * LANES)
            aa = a_scr[rows, cs].reshape(ngrp, SUBLANES, LANES)
            uu = u_scr[rows, cs].reshape(ngrp, SUBLANES, LANES)
            d = 1
            while d < SUBLANES:
                keep = sub3 >= d
                a_sh = pltpu.roll(aa, d, 1)
                u_sh = pltpu.roll(uu, d, 1)
                uu = jnp.where(keep, aa * u_sh + uu, uu)
                aa = jnp.where(keep, aa * a_sh, aa)
                d *= 2
            hprev = h_scr[:, cs]
            groups = []
            for j in range(ngrp):
                hj = uu[j] + aa[j] * hprev
                groups.append(hj)
                hprev = hj[SUBLANES - 1:SUBLANES, :]
            hs = jnp.concatenate(groups, axis=0)
            h_scr[:, cs] = hprev
            m_scr[rows, HG_WIDTH + gi * LANES:HG_WIDTH + (gi + 1) * LANES] = (
                hs * gl_scr[rows, cs]).astype(BF16)

        q = q_scr[rows, :]
        k = k_scr[rows, :]
        v = v_scr[rows, :]
        f1, f2, f3 = _split3(lf_scr[rows, :])
        cm = cm_ref[...]
        b = _dot(cm, f1) + _dot(cm, f2) + _dot(cm, f3)

        for hh in range(HG_HEADS):
            cs = slice(hh * HG_D, (hh + 1) * HG_D)
            qh, kh, vh, bh = q[:, cs], k[:, cs], v[:, cs], b[:, cs]
            vb = vh.astype(BF16)
            amat = jnp.zeros((CHUNK, CHUNK), F32)
            for w, upper, pairs in levels:
                dec = _exp_neg(jnp.abs(bh - _level_reference(bh, w)))
                y = (jnp.where(upper, qh, kh) * dec).astype(BF16)
                amat = jnp.where(pairs, _dot_nt(y, y), amat)
            o = _dot(amat.astype(BF16), vb)
            st = st_scr[hh]
            o = o + _dot_nt((qh * jnp.exp(bh)).astype(BF16), st.astype(BF16))
            if DIAG == 1:
                o = o + jnp.sum(qh * kh, axis=-1, keepdims=True) * vh
            else:
                q3 = qh.reshape(nblk, DIAG, HG_D)
                k3 = kh.reshape(nblk, DIAG, HG_D)
                v3 = vh.reshape(nblk, DIAG, HG_D)
                b3 = bh.reshape(nblk, DIAG, HG_D)
                od = jnp.zeros((nblk, DIAG, HG_D), F32)
                for s in range(DIAG):
                    dec = jnp.exp(jnp.minimum(b3 - b3[:, s:s + 1, :], 0.0))
                    p = q3 * (k3[:, s:s + 1, :] * dec)
                    rs = jnp.sum(p, axis=-1, keepdims=True)
                    rs = jnp.where(t3 >= s, rs, 0.0)
                    od = od + rs * v3[:, s:s + 1, :]
                o = o + od.reshape(CHUNK, HG_D)
            bl = bh[CHUNK - 1:CHUNK, :]
            kdec = (kh * jnp.exp(bl - bh)).astype(BF16)
            st_scr[hh] = st * jnp.exp(bl) + _dot_tn(vb, kdec)
            o = o * lax.rsqrt(jnp.mean(o * o, axis=-1, keepdims=True) + EPS)
            o = o * hgg_ref[:, cs]
            o = o * g_scr[rows, cs]
            m_scr[rows, cs] = o.astype(BF16)
        return carry

    lax.fori_loop(0, tl // CHUNK, chunk_body, 0, unroll=tl // CHUNK)
    xo_ref[0] = x_ref[0] + _dot(m_scr[...], wout_ref[...])

    @pl.when(l == nl - 1)
    def _():
        for hh in range(HG_HEADS):
            hs_ref[0, hh] = st_scr[hh].T
        ls_ref[0] = h_scr[...]
        cs_ref[0] = xpad_scr[5:8, :]


def _slab_spec(c, steps, step_of):
    share = 1
    while (c.shape[0] * share) % (steps * 2 * SUBLANES):
        share *= 2
    assert steps % share == 0
    rows = c.shape[0] * share // steps
    return pl.BlockSpec((rows, c.shape[1]), lambda *ids: (step_of(*ids) // share, 0))


def _mixer_block(x, gain, w_in, w_out, layer, mp, cast=(), tl=512):
    bsz, seq, d = x.shape
    tl = min(tl, seq)
    nl = seq // tl
    steps = bsz * nl
    cast_dtypes = [dt for _, dt in cast]
    cast = [c for c, _ in cast]
    full = lambda shape: pl.BlockSpec(shape, lambda b, l: (0,) * len(shape))
    stacked = lambda shape: pl.BlockSpec((None,) + shape, lambda b, l: (layer, 0, 0))
    slab = lambda c: _slab_spec(c, steps, lambda b, l: b * nl + l)
    out_shapes = [
        jax.ShapeDtypeStruct((bsz, seq, d), F32),
        jax.ShapeDtypeStruct((bsz, HG_HEADS, HG_D, HG_D), F32),
        jax.ShapeDtypeStruct((bsz, 1, LRU_WIDTH), F32),
        jax.ShapeDtypeStruct((bsz, CONV_K - 1, LRU_WIDTH), F32),
    ] + [jax.ShapeDtypeStruct(c.shape, dt) for c, dt in zip(cast, cast_dtypes)]
    half = lambda dt: pltpu.VMEM((tl, HG_WIDTH), dt)
    xo, hg, lru, conv, *rounded = pl.pallas_call(
        functools.partial(_mixer_block_kernel, tl=tl, n_cast=len(cast)),
        out_shape=out_shapes,
        grid=(bsz, nl),
        in_specs=[
            pl.BlockSpec((1, tl, d), lambda b, l: (b, l, 0)),
            full((1, d)), stacked((d, IN_WIDTH)),
            full((3, HG_WIDTH)), full((1, HG_WIDTH)), full((CONV_K, LRU_WIDTH)),
            full((1, LRU_WIDTH)), full((LRU_WIDTH, LRU_WIDTH)), full((1, LRU_WIDTH)),
            full((LRU_WIDTH, LRU_WIDTH)), full((1, LRU_WIDTH)), full((1, LRU_WIDTH)),
            full(mp["cm"].shape), stacked((d, d)),
        ] + [slab(c) for c in cast],
        out_specs=[
            pl.BlockSpec((1, tl, d), lambda b, l: (b, l, 0)),
            pl.BlockSpec((1, HG_HEADS, HG_D, HG_D), lambda b, l: (b, 0, 0, 0)),
            pl.BlockSpec((1, 1, LRU_WIDTH), lambda b, l: (b, 0, 0)),
            pl.BlockSpec((1, CONV_K - 1, LRU_WIDTH), lambda b, l: (b, 0, 0)),
        ] + [slab(c) for c in cast],
        scratch_shapes=[
            pltpu.VMEM((HG_HEADS, HG_D, HG_D), F32),
            pltpu.VMEM((1, LRU_WIDTH), F32),
            pltpu.VMEM((tl + 8, LRU_WIDTH), F32),
            pltpu.VMEM((tl, d), BF16),
            half(F32), half(F32), half(F32), half(F32), half(F32), half(F32), half(F32),
            half(F32),
            pltpu.VMEM((tl, d), BF16),
        ],
        compiler_params=_cparams(("parallel", "arbitrary")),
        name="mixer_block",
    )(x, gain.reshape(1, d), w_in, mp["lbp"], mp["hgg"], mp["cw"], mp["cb"], mp["wra"],
      mp["bra"], mp["wix"], mp["bix"], mp["c8"], mp["cm"], w_out, *cast)
    return xo, hg, lru.reshape(bsz, LRU_WIDTH), conv, rounded


def _mixer_step_kernel(z_ref, sh_ref, sl_ref, sc_ref, lbp_ref, hgg_ref, cw_ref, cb_ref, wra_ref,
                       bra_ref, wix_ref, bix_ref, c8_ref, *rest, bb, aliased):
    m_ref, hn_ref, ln_ref, cn_ref, o_scr = rest[1:] if aliased else rest
    z = z_ref[...]
    xr = z[:, 4 * HG_WIDTH:4 * HG_WIDTH + LRU_WIDTH]
    gr = z[:, 4 * HG_WIDTH + LRU_WIDTH:IN_WIDTH]
    buf = sc_ref[...]
    xc = cb_ref[...]
    for j in range(CONV_K - 1):
        xc = xc + buf[:, j * LRU_WIDTH:(j + 1) * LRU_WIDTH] * cw_ref[j:j + 1, :]
    xc = xc + xr * cw_ref[CONV_K - 1:CONV_K, :]
    cn_ref[:, 0:(CONV_K - 2) * LRU_WIDTH] = buf[:, LRU_WIDTH:(CONV_K - 1) * LRU_WIDTH]
    cn_ref[:, (CONV_K - 2) * LRU_WIDTH:(CONV_K - 1) * LRU_WIDTH] = xr
    a, u = _lru_gates(xc, wra_ref[...], bra_ref[...], wix_ref[...], bix_ref[...], c8_ref[...])
    hnew = u + a * sl_ref[...]
    ln_ref[...] = hnew
    m_ref[:, HG_WIDTH:D_MODEL] = (hnew * _gelu_tanh(gr)).astype(m_ref.dtype)

    q = _silu(z[:, 0:HG_WIDTH])
    log_f, k = _hgrn_gates(z[:, HG_WIDTH:2 * HG_WIDTH], lbp_ref[0:1, :], lbp_ref[1:2, :],
                           lbp_ref[2:3, :])
    f = jnp.exp(log_f)
    v = z[:, 2 * HG_WIDTH:3 * HG_WIDTH]
    gate = _silu(z[:, 3 * HG_WIDTH:4 * HG_WIDTH])
    zpad = jnp.zeros((HG_D - bb, HG_D), F32)
    for hh in range(HG_HEADS):
        cs = slice(hh * HG_D, (hh + 1) * HG_D)
        ft = jnp.concatenate([f[:, cs], zpad], axis=0).T
        kt = jnp.concatenate([k[:, cs], zpad], axis=0).T
        qt = jnp.concatenate([q[:, cs], zpad], axis=0).T
        for j in range(bb):
            fcol = jnp.broadcast_to(ft[:, j:j + 1], (HG_D, HG_D))
            kcol = jnp.broadcast_to(kt[:, j:j + 1], (HG_D, HG_D))
            qcol = jnp.broadcast_to(qt[:, j:j + 1], (HG_D, HG_D))
            vrow = jnp.broadcast_to(v[j:j + 1, cs], (HG_D, HG_D))
            sn = fcol * sh_ref[j, hh] + kcol * vrow
            hn_ref[j, hh] = sn
            o_scr[j:j + 1, cs] = jnp.sum(qcol * sn, axis=0, keepdims=True)
    for hh in range(HG_HEADS):
        cs = slice(hh * HG_D, (hh + 1) * HG_D)
        o = o_scr[:, cs]
        o = o * lax.rsqrt(jnp.mean(o * o, axis=-1, keepdims=True) + EPS)
        o = o * hgg_ref[:, cs]
        o = o * gate[:, cs]
        m_ref[:, cs] = o.astype(m_ref.dtype)


def _mixer_step(z, s_hg_all, layer, s_lru, s_conv, mp, hg_buf=None, bb=16):
    bsz = z.shape[0]
    bb = min(bb, bsz)
    cw3 = (CONV_K - 1) * LRU_WIDTH
    full = lambda shape: pl.BlockSpec(shape, lambda i: (0,) * len(shape))
    state_spec = pl.BlockSpec((None, bb, HG_HEADS, HG_D, HG_D), lambda i: (layer, i, 0, 0, 0))
    out_shapes = (
        jax.ShapeDtypeStruct((bsz, D_MODEL), mp["wra"].dtype),
        jax.ShapeDtypeStruct(s_hg_all.shape, F32),
        jax.ShapeDtypeStruct((bsz, LRU_WIDTH), F32),
        jax.ShapeDtypeStruct((bsz, cw3), F32),
    )
    in_specs = [
        pl.BlockSpec((bb, IN_WIDTH), lambda i: (i, 0)),
        state_spec,
        pl.BlockSpec((bb, LRU_WIDTH), lambda i: (i, 0)),
        pl.BlockSpec((bb, cw3), lambda i: (i, 0)),
        full((3, HG_WIDTH)), full((1, HG_WIDTH)), full((CONV_K, LRU_WIDTH)),
        full((1, LRU_WIDTH)), full((LRU_WIDTH, LRU_WIDTH)), full((1, LRU_WIDTH)),
        full((LRU_WIDTH, LRU_WIDTH)), full((1, LRU_WIDTH)), full((1, LRU_WIDTH)),
    ]
    args = [z, s_hg_all, s_lru, s_conv.reshape(bsz, cw3), mp["lbp"], mp["hgg"], mp["cw"],
            mp["cb"], mp["wra"], mp["bra"], mp["wix"], mp["bix"], mp["c8"]]
    aliases = {}
    if hg_buf is not None:
        aliases = {len(args): 1}
        in_specs.append(pl.BlockSpec(memory_space=pl.ANY))
        args.append(hg_buf)
    merged, hg, lru, conv = pl.pallas_call(
        functools.partial(_mixer_step_kernel, bb=bb, aliased=hg_buf is not None),
        out_shape=out_shapes,
        grid=(bsz // bb,),
        in_specs=in_specs,
        out_specs=(
            pl.BlockSpec((bb, D_MODEL), lambda i: (i, 0)),
            state_spec,
            pl.BlockSpec((bb, LRU_WIDTH), lambda i: (i, 0)),
            pl.BlockSpec((bb, cw3), lambda i: (i, 0)),
        ),
        scratch_shapes=[pltpu.VMEM((bb, HG_WIDTH), F32)],
        input_output_aliases=aliases,
        compiler_params=_cparams(("parallel",)),
        name="mixer_step",
    )(*args)
    return merged, hg, lru, conv.reshape(bsz, CONV_K - 1, LRU_WIDTH)


def _router_kernel(x_ref, g_ref, rw_ref, tril_ref, *refs, n_cast, exact_logits):
    cast_src = refs[:n_cast]
    info_ref, idx_ref, cnt_ref, zero_ref = refs[n_cast:n_cast + 4]
    cast_dst = refs[n_cast + 4:2 * n_cast + 4]
    carry_scr = refs[-1]
    i = pl.program_id(0)

    @pl.when(i == 0)
    def _():
        carry_scr[...] = jnp.zeros_like(carry_scr)

    zero_ref[...] = jnp.zeros_like(zero_ref)
    for src, dst in zip(cast_src, cast_dst):
        dst[...] = src[...].astype(dst.dtype)

    h = _rms(x_ref[...], g_ref[...])
    rw = rw_ref[...]
    if exact_logits:
        logits = jnp.dot(h, rw, preferred_element_type=F32, precision=lax.Precision.HIGHEST)
    else:
        h1 = h.astype(BF16)
        h2 = (h - h1.astype(F32)).astype(BF16)
        r1 = rw.astype(BF16)
        r2 = (rw - r1.astype(F32)).astype(BF16)
        logits = _dot(h1, r1) + (_dot(h1, r2) + _dot(h2, r1))
    lane = lax.broadcasted_iota(I32, logits.shape, 1)
    neg = jnp.float32(-jnp.inf)
    logits = jnp.where(lane < N_EXPERTS, logits, neg)
    m1 = jnp.max(logits, axis=-1, keepdims=True)
    i1 = jnp.min(jnp.where(logits == m1, lane, LANES), axis=-1, keepdims=True)
    l2 = jnp.where(lane == i1, neg, logits)
    m2 = jnp.max(l2, axis=-1, keepdims=True)
    i2 = jnp.min(jnp.where(l2 == m2, lane, LANES), axis=-1, keepdims=True)
    e = jnp.exp(m2 - m1)
    p1 = 1.0 / (1.0 + e)
    p2 = e / (1.0 + e)
    oh1 = (lane == i1).astype(F32)
    oh2 = (lane == i2).astype(F32)
    sel = oh1 + oh2
    before = _dot(tril_ref[...], sel.astype(BF16)) + carry_scr[...]
    r1 = jnp.sum(oh1 * before, axis=-1, keepdims=True)
    r2 = jnp.sum(oh2 * before, axis=-1, keepdims=True)
    carry_scr[...] += jnp.sum(sel, axis=0, keepdims=True)
    info = jnp.where(lane == 0, i1.astype(F32), 0.0)
    info = jnp.where(lane == 1, i2.astype(F32), info)
    info = jnp.where(lane == 2, r1, info)
    info = jnp.where(lane == 3, r2, info)
    info = jnp.where(lane == 4, p1, info)
    info = jnp.where(lane == 5, p2, info)
    info_ref[...] = info
    idx_ref[...] = info.T[0:SUBLANES, :].astype(I32)
    cnt_ref[...] = carry_scr[...]


def _router(x, gain, rw_pad, n_rows, cast=(), exact_logits=True, tm=512):
    t, d = x.shape
    tm = min(tm, t)
    steps = t // tm
    zrows = n_rows // steps
    assert zrows * steps == n_rows and zrows % SUBLANES == 0
    slab = lambda c: _slab_spec(c, steps, lambda i: i)
    tril = jnp.tril(jnp.ones((tm, tm), BF16), -1)
    return pl.pallas_call(
        functools.partial(_router_kernel, n_cast=len(cast), exact_logits=exact_logits),
        out_shape=[jax.ShapeDtypeStruct((t, LANES), F32),
                   jax.ShapeDtypeStruct((SUBLANES, t), I32),
                   jax.ShapeDtypeStruct((1, LANES), F32),
                   jax.ShapeDtypeStruct((n_rows, d), F32)]
        + [jax.ShapeDtypeStruct(c.shape, BF16) for c in cast],
        grid=(steps,),
        in_specs=[
            pl.BlockSpec((tm, d), lambda i: (i, 0)),
            pl.BlockSpec((1, d), lambda i: (0, 0)),
            pl.BlockSpec((d, LANES), lambda i: (0, 0)),
            pl.BlockSpec((tm, tm), lambda i: (0, 0)),
        ] + [slab(c) for c in cast],
        out_specs=[pl.BlockSpec((tm, LANES), lambda i: (i, 0)),
                   pl.BlockSpec((SUBLANES, tm), lambda i: (0, i)),
                   pl.BlockSpec((1, LANES), lambda i: (0, 0)),
                   pl.BlockSpec((zrows, d), lambda i: (i, 0))] + [slab(c) for c in cast],
        scratch_shapes=[pltpu.VMEM((1, LANES), F32)],
        compiler_params=_cparams(("arbitrary",)),
        name="router",
    )(x, gain.reshape(1, d), rw_pad, tril, *cast)


def _row_copy(src, dst, s, d, sem):
    return pltpu.make_async_copy(src.at[pl.ds(s, 1)], dst.at[pl.ds(d, 1)], sem)


def _tile_dest(dest, t, td):
    nb = t // td
    dest3 = jnp.concatenate([dest[0].reshape(nb, td), dest[1].reshape(nb, td)], axis=1)
    return dest3.reshape(nb, 1, 2 * td)


def _dispatch_kernel(dest_ref, x_ref, g_ref, xs_in_ref, xs_ref, pk_scr, sem, *, td):
    del xs_in_ref
    pk_scr[...] = _rms(x_ref[...], g_ref[...])

    def issue(r, c):
        _row_copy(pk_scr, xs_ref, r, dest_ref[0, 0, r], sem).start(priority=0)
        _row_copy(pk_scr, xs_ref, r, dest_ref[0, 0, td + r], sem).start(priority=1)
        return c

    lax.fori_loop(0, td, issue, 0, unroll=8)
    for _ in range(2):
        pltpu.make_async_copy(pk_scr, xs_ref.at[pl.ds(0, td)], sem).wait()


def _dispatch(x, gain, dest, xs0, td=1024):
    t, d = x.shape
    td = min(td, t)
    n_rows = xs0.shape[0]
    return pl.pallas_call(
        functools.partial(_dispatch_kernel, td=td),
        out_shape=jax.ShapeDtypeStruct((n_rows, d), F32),
        grid=(t // td,),
        in_specs=[
            pl.BlockSpec((1, 1, 2 * td), lambda i: (i, 0, 0), memory_space=pltpu.SMEM),
            pl.BlockSpec((td, d), lambda i: (i, 0)),
            pl.BlockSpec((1, d), lambda i: (0, 0)),
            pl.BlockSpec(memory_space=pl.ANY),
        ],
        out_specs=pl.BlockSpec(memory_space=pl.ANY),
        scratch_shapes=[pltpu.VMEM((td, d), F32), pltpu.SemaphoreType.DMA(())],
        input_output_aliases={3: 0},
        compiler_params=pltpu.CompilerParams(dimension_semantics=("arbitrary",),
                                             vmem_limit_bytes=VMEM_LIMIT_BYTES,
                                             has_side_effects=True),
        name="dispatch",
    )(_tile_dest(dest, t, td), x, gain.reshape(1, d), xs0)


def _combine_kernel(dest_ref, nxt_ref, x_ref, info_ref, g_ref, ys_ref, o_ref, y_scr, sems, *, td):
    i = pl.program_id(0)
    nb = pl.num_programs(0)

    def fetch(idx_ref, slot):
        def issue(r, c):
            _row_copy(ys_ref, y_scr.at[slot, 0], idx_ref[0, 0, r], r,
                      sems.at[slot]).start(priority=0)
            _row_copy(ys_ref, y_scr.at[slot, 1], idx_ref[0, 0, td + r], r,
                      sems.at[slot]).start(priority=1)
            return c

        lax.fori_loop(0, td, issue, 0, unroll=8)

    def finish(slot):
        for j in range(2):
            pltpu.make_async_copy(ys_ref.at[pl.ds(0, td)], y_scr.at[slot, j],
                                  sems.at[slot]).wait()
        p1 = info_ref[:, 4:5]
        p2 = info_ref[:, 5:6]
        x = x_ref[...] + (p1 * y_scr[slot, 0] + p2 * y_scr[slot, 1])
        o_ref[...] = _rms(x, g_ref[...])

    @pl.when(i == 0)
    def _():
        fetch(dest_ref, 0)

    for slot in range(2):
        @pl.when(i % 2 == slot)
        def _(slot=slot):
            @pl.when(i + 1 < nb)
            def _():
                fetch(nxt_ref, 1 - slot)

            finish(slot)


def _combine(x, ys, dest, info, gain, td=512):
    t, d = x.shape
    td = min(td, t)
    nb = t // td
    row = pl.BlockSpec((td, d), lambda i: (i, 0))
    tiled = _tile_dest(dest, t, td)
    return pl.pallas_call(
        functools.partial(_combine_kernel, td=td),
        out_shape=jax.ShapeDtypeStruct((t, d), F32),
        grid=(nb,),
        in_specs=[
            pl.BlockSpec((1, 1, 2 * td), lambda i: (i, 0, 0), memory_space=pltpu.SMEM),
            pl.BlockSpec((1, 1, 2 * td), lambda i: (jnp.minimum(i + 1, nb - 1), 0, 0),
                         memory_space=pltpu.SMEM),
            row,
            pl.BlockSpec((td, LANES), lambda i: (i, 0)),
            pl.BlockSpec((1, d), lambda i: (0, 0)),
            pl.BlockSpec(memory_space=pl.ANY),
        ],
        out_specs=row,
        scratch_shapes=[pltpu.VMEM((2, 2, td, d), F32), pltpu.SemaphoreType.DMA((2,))],
        compiler_params=_cparams(("arbitrary",)),
        name="combine",
    )(tiled, tiled, x, info, gain.reshape(1, d), ys)


def _rows2d(w):
    return w.reshape(-1, w.shape[-1])


def _moe_rows(group_sizes, tm_e=TILE_ROWS):
    return (2 * sum(group_sizes) + N_EXPERTS * (tm_e - 1)) // tm_e * tm_e


def _moe(xs_in, norm_g, rw_pad, experts, shared, final_g, tm_e=TILE_ROWS):
    n_rows = _moe_rows([x.shape[0] for x in xs_in], tm_e)
    n_tiles = n_rows // tm_e
    routed = []
    for x in xs_in:
        todo = [(name, w) for name, w in experts if name not in shared]
        steps = x.shape[0] // min(TILE_ROWS, x.shape[0])
        fill = steps * SUBLANES if "moe_rows" in shared else n_rows
        info, idx, cnt, zeros, *rounded = _router(
            x, norm_g, rw_pad, fill, cast=[_rows2d(w) for _, w in todo],
            exact_logits=x.shape[0] <= TILE_ROWS)
        for (name, w), r in zip(todo, rounded):
            shared[name] = r.reshape(w.shape)
        shared.setdefault("moe_rows", zeros)
        routed.append((info, idx, cnt[0, :N_EXPERTS].astype(I32)))
    xs0 = shared["moe_rows"]
    assert xs0.shape[0] == n_rows
    w1, w3, w2 = (shared[name] for name, _ in experts)
    counts = sum(c for _, _, c in routed)
    padded = ((counts + tm_e - 1) // tm_e) * tm_e
    ends = jnp.cumsum(padded)
    starts = ends - padded
    tile_start = jnp.arange(n_tiles, dtype=I32) * tm_e
    tile_expert = jnp.minimum(
        jnp.sum((tile_start[:, None] >= ends[None, :]).astype(I32), axis=1), N_EXPERTS - 1)
    n_valid = (ends[-1] // tm_e).reshape(1).astype(I32)
    dests = []
    base = starts
    for info, idx, cnt in routed:
        e1, e2, r1, r2 = idx[0], idx[1], idx[2], idx[3]
        dests.append(jnp.stack([base[e1] + r1, base[e2] + r2]))
        base = base + cnt
    rows = xs0
    for x, dest in zip(xs_in, dests):
        rows = _dispatch(x, norm_g, dest, rows)
    ys = _ffn(rows, w1, w3, w2, tile_expert.astype(I32), n_valid, tm=tm_e, n_f=1)
    return [_combine(x, ys, dest, info, final_g)
            for x, dest, (info, _, _) in zip(xs_in, dests, routed)]


def _block_diag(w):
    n, c, d = w.shape
    eye = jnp.eye(n, dtype=w.dtype)
    return (w[:, :, None, :] * eye[:, None, :, None]).reshape(n * c, n * d)


def _cum_matrix():
    t = jnp.arange(CHUNK)
    return (t[None, :] <= t[:, None]).astype(BF16)


def _mixer_params(l, lb, hg_norm_g, conv_w, conv_b, w_ra, b_ra, w_ix, b_ix, lru_lambda, wdtype):
    row = lambda a: a.reshape(1, -1).astype(F32)
    return {
        "lbp": jnp.stack([jnp.log(lb), jnp.log1p(-lb), 1.0 - lb]).astype(F32),
        "hgg": row(hg_norm_g[l]),
        "cw": conv_w[l].astype(F32),
        "cb": row(conv_b[l]),
        "wra": _block_diag(w_ra[l].astype(F32)).astype(wdtype),
        "bra": row(b_ra[l]),
        "wix": _block_diag(w_ix[l].astype(F32)).astype(wdtype),
        "bix": row(b_ix[l]),
        "c8": row(-LRU_C * jax.nn.softplus(-lru_lambda[l].astype(F32))),
        "cm": _cum_matrix(),
    }


def _trunk(x, seq_shape, states, prm):
    bsz, seq = seq_shape
    one = jnp.ones((1,), I32)
    new_hg, new_lru, new_conv = [], [], []
    shared = prm["shared"]
    hg_stack = None if states is None else shared.get("hg_seed", states[0])
    for l in range(2):
        mp = prm["mixer"][l]
        if states is None:
            x3 = x.reshape(bsz, seq, D_MODEL)
            jobs = [j for j in prm["side"][l] if j[0] not in shared]
            if l == 0:
                w_in, w_out, layer = prm["w_in"], prm["w_out"], 0
            else:
                w_in, w_out, layer = shared["w_in"], shared["w_out"], 1
            x, s1, s2, s3, done = _mixer_block(
                x3, prm["norm1_g"][l], w_in, w_out, layer, mp,
                cast=[(_rows2d(a), dt) for _, a, dt in jobs])
            for (name, a, _), r in zip(jobs, done):
                shared[name] = r.reshape(a.shape)
            x = x.reshape(bsz * seq, D_MODEL)
            new_hg.append(s1)
        else:
            z = _linear(x, prm["w_in"], l, gain=prm["norm1_g"][l])
            merged, hg_stack, s2, s3 = _mixer_step(z, states[0], l, states[1][l], states[2][l],
                                                   mp, hg_buf=hg_stack)
            x = _linear(merged, prm["w_out"], l, res=x)
        new_lru.append(s2)
        new_conv.append(s3)
        if l == 0:
            n_tiles = max(x.shape[0] // TILE_ROWS, 1)
            if states is None:
                f1, f3, f2 = shared["ffn_w1"], shared["ffn_w3"], shared["ffn_w2"]
            else:
                f1, f3, f2 = prm["ffn_w1"], prm["ffn_w3"], prm["ffn_w2"]
            n_f = 1 if f1.dtype == BF16 else f1.shape[2] // F_CHUNK
            fill = prm["moe_rows"] if states is None else 0
            x = _ffn(x, f1, f3, f2, jnp.zeros((n_tiles,), I32), one * n_tiles,
                     gain=prm["norm2_g"][0], res=True, fill_rows=fill, tm=TILE_ROWS, n_f=n_f,
                     ck=F_CHUNK)
            if fill:
                x, shared["moe_rows"] = x
    hg_all = jnp.stack(new_hg) if states is None else hg_stack
    return x, hg_all, jnp.stack(new_lru), jnp.stack(new_conv)


def kernel(x_prompt, x_sample, state_hgrn, state_lru, state_conv, norm1_g, w_in, lower_bounds,
           hg_norm_g, conv_w, conv_b, w_ra, b_ra, w_ix, b_ix, lru_lambda, w_out, norm2_g,
           ffn_w1, ffn_w3, ffn_w2, router_w, moe_w1, moe_w3, moe_w2, final_norm_g):
    lb_all = jnp.cumsum(jax.nn.softmax(lower_bounds.astype(F32), axis=0), axis=0)
    lb_all = lb_all - lb_all[0]
    f32 = lambda a: a.astype(F32)
    common = {
        "experts": [("moe_w1", f32(moe_w1[0])), ("moe_w3", f32(moe_w3[0])),
                    ("moe_w2", f32(moe_w2[0]))],
        "side": {0: [("moe_w3", f32(moe_w3[0]), BF16), ("ffn_w1", f32(ffn_w1), BF16),
                     ("ffn_w3", f32(ffn_w3), BF16), ("ffn_w2", f32(ffn_w2), BF16),
                     ("w_in", f32(w_in), BF16), ("w_out", f32(w_out), BF16)],
                 1: [("moe_w2", f32(moe_w2[0]), BF16), ("hg_seed", f32(state_hgrn), F32)]},
        "shared": {},
        "moe_rows": _moe_rows([x_prompt.shape[0] * x_prompt.shape[1], x_sample.shape[0]]),
        "rw_pad": jnp.pad(router_w[0].astype(F32), ((0, 0), (0, LANES - N_EXPERTS))),
        "norm1_g": norm1_g, "norm2_g": norm2_g, "final_norm_g": final_norm_g,
    }
    mixer = lambda wdtype: [_mixer_params(l, lb_all[l], hg_norm_g, conv_w, conv_b, w_ra, b_ra,
                                          w_ix, b_ix, lru_lambda, wdtype) for l in range(2)]
    prompt = dict(common, mixer=mixer(BF16), w_in=w_in[:1].astype(BF16),
                  w_out=w_out[:1].astype(BF16))

    bp, sp, d = x_prompt.shape
    bs = x_sample.shape[0]
    x_p, hg_p, lru_p, conv_p = _trunk(x_prompt.reshape(bp * sp, d), (bp, sp), None, prompt)
    sample = dict(common, mixer=mixer(F32), w_in=f32(w_in), w_out=f32(w_out),
                  ffn_w1=f32(ffn_w1), ffn_w3=f32(ffn_w3), ffn_w2=f32(ffn_w2))
    x_s, hg_s, lru_s, conv_s = _trunk(x_sample.reshape(bs, d), (bs, 1),
                                      (state_hgrn, state_lru, state_conv), sample)
    y_p, y_s = _moe([x_p, x_s], norm2_g[1], common["rw_pad"], common["experts"],
                    common["shared"], final_norm_g)
    return (y_p.reshape(bp, sp, d), y_s.reshape(bs, 1, d), hg_p, lru_p, conv_p, hg_s, lru_s,
            conv_s)
```

```python
import functools

import jax
import jax.numpy as jnp
from jax import lax
from jax.experimental import pallas as pl
from jax.experimental.pallas import tpu as pltpu

F32 = jnp.float32
BF16 = jnp.bfloat16
I32 = jnp.int32

D_MODEL = 1024
HG_WIDTH = 512
HG_HEADS = 4
HG_D = 128
LRU_WIDTH = 512
LRU_BLOCKS = 8
LRU_C = 8.0
CONV_K = 4
IN_WIDTH = 4 * HG_WIDTH + 2 * LRU_WIDTH
N_EXPERTS = 8
EPS = 1e-6

CHUNK = 128
DIAG = 1
LANES = 128
SUBLANES = 8
TILE_ROWS = 512
F_CHUNK = 256
VMEM_LIMIT_BYTES = 56 * 1024 * 1024


def _cparams(sem):
    return pltpu.CompilerParams(dimension_semantics=sem, vmem_limit_bytes=VMEM_LIMIT_BYTES)


def _exp_neg(x):
    return jnp.exp2(x * -1.4426950408889634)


def _sigmoid(x):
    return 1.0 / (1.0 + _exp_neg(x))


def _silu(x):
    return x * _sigmoid(x)


def _gelu_tanh(x):
    c = 0.7978845608028654
    return 0.5 * x * (1.0 + jnp.tanh(c * (x + 0.044715 * (x * x * x))))


def _rms(x, g):
    return x * lax.rsqrt(jnp.mean(x * x, axis=-1, keepdims=True) + EPS) * g


def _dot(a, b):
    if b.dtype == F32:
        return jnp.dot(a.astype(F32), b, preferred_element_type=F32,
                       precision=lax.Precision.HIGHEST)
    return jnp.dot(a.astype(BF16), b, preferred_element_type=F32)


def _dot_nt(a, b):
    return lax.dot_general(a, b, (((1,), (1,)), ((), ())), preferred_element_type=F32)


def _dot_tn(a, b):
    return lax.dot_general(a, b, (((0,), (0,)), ((), ())), preferred_element_type=F32)


def _split3(x):
    x1 = x.astype(BF16)
    r1 = x - x1.astype(F32)
    x2 = r1.astype(BF16)
    x3 = (r1 - x2.astype(F32)).astype(BF16)
    return x1, x2, x3


def _linear_kernel(*refs, has_gain, has_res):
    it = iter(refs)
    x_ref = next(it)
    g_ref = next(it) if has_gain else None
    w_ref = next(it)
    r_ref = next(it) if has_res else None
    o_ref = next(it)
    h_scr = next(it)

    @pl.when(pl.program_id(1) == 0)
    def _():
        x = x_ref[...].astype(F32)
        if has_gain:
            x = _rms(x, g_ref[...])
        h_scr[...] = x.astype(h_scr.dtype)

    acc = _dot(h_scr[...], w_ref[...])
    if has_res:
        acc = acc + r_ref[...]
    o_ref[...] = acc.astype(o_ref.dtype)


def _linear(x, w, layer, gain=None, res=None, tm=1024, tn=1024):
    m, k = x.shape
    n = w.shape[2]
    tm = min(tm, m)
    tn = min(tn, n)
    in_specs = [pl.BlockSpec((tm, k), lambda i, j: (i, 0))]
    args = [x]
    if gain is not None:
        in_specs.append(pl.BlockSpec((1, k), lambda i, j: (0, 0)))
        args.append(gain.reshape(1, k))
    in_specs.append(pl.BlockSpec((None, k, tn), lambda i, j: (layer, 0, j)))
    args.append(w)
    if res is not None:
        in_specs.append(pl.BlockSpec((tm, tn), lambda i, j: (i, j)))
        args.append(res)
    return pl.pallas_call(
        functools.partial(_linear_kernel, has_gain=gain is not None, has_res=res is not None),
        out_shape=jax.ShapeDtypeStruct((m, n), F32),
        grid=(m // tm, n // tn),
        in_specs=in_specs,
        out_specs=pl.BlockSpec((tm, tn), lambda i, j: (i, j)),
        scratch_shapes=[pltpu.VMEM((tm, k), w.dtype)],
        compiler_params=_cparams(("parallel", "arbitrary")),
        name="linear",
    )(*args)


def _ffn_kernel(te_ref, nv_ref, *refs, has_gain, has_res, has_fill, tf, ck):
    del te_ref
    it = iter(refs)
    x_ref = next(it)
    g_ref = next(it) if has_gain else None
    w1_ref, w3_ref, w2_ref, o_ref = next(it), next(it), next(it), next(it)
    fill_ref = next(it) if has_fill else None
    xb_scr, a_scr, acc_scr = it
    i = pl.program_id(0)
    j = pl.program_id(1)
    nj = pl.num_programs(1)
    valid = i < nv_ref[0]
    if has_fill:
        fill_ref[...] = jnp.zeros_like(fill_ref)

    @pl.when(valid)
    def _():
        @pl.when(j == 0)
        def _():
            x = x_ref[...]
            if has_gain:
                x = _rms(x, g_ref[...])
            xb_scr[...] = x.astype(xb_scr.dtype)

        xb = xb_scr[...]
        for c in range(tf // ck):
            h1 = _dot(xb, w1_ref[0, :, c * ck:(c + 1) * ck])
            h3 = _dot(xb, w3_ref[0, :, c * ck:(c + 1) * ck])
            a_scr[:, c * ck:(c + 1) * ck] = (_silu(h1) * h3).astype(a_scr.dtype)
        part = _dot(a_scr[...], w2_ref[0])

        @pl.when(j == 0)
        def _():
            acc_scr[...] = part

        @pl.when(j > 0)
        def _():
            acc_scr[...] += part

        @pl.when(j == nj - 1)
        def _():
            out = acc_scr[...]
            if has_res:
                out = x_ref[...] + out
            o_ref[...] = out

    @pl.when(jnp.logical_and(jnp.logical_not(valid), j == nj - 1))
    def _():
        o_ref[...] = jnp.zeros_like(o_ref)


def _ffn(x, w1, w3, w2, tile_expert, n_valid, gain=None, res=False, fill_rows=0, tm=TILE_ROWS,
         n_f=2, ck=F_CHUNK):
    p, d = x.shape
    f = w1.shape[2]
    tm = min(tm, p)
    tf = f // n_f
    n_tiles = p // tm
    out_specs = [pl.BlockSpec((tm, d), lambda i, j, te, nv: (i, 0))]
    out_shape = [jax.ShapeDtypeStruct((p, d), F32)]
    if fill_rows:
        assert n_f == 1 and fill_rows % (n_tiles * SUBLANES) == 0
        out_specs.append(pl.BlockSpec((fill_rows // n_tiles, d), lambda i, j, te, nv: (i, 0)))
        out_shape.append(jax.ShapeDtypeStruct((fill_rows, d), F32))
    in_specs = [pl.BlockSpec((tm, d), lambda i, j, te, nv: (i, 0))]
    args = [x]
    if gain is not None:
        in_specs.append(pl.BlockSpec((1, d), lambda i, j, te, nv: (0, 0)))
        args.append(gain.reshape(1, d))
    in_specs += [
        pl.BlockSpec((1, d, tf), lambda i, j, te, nv: (te[i], 0, j)),
        pl.BlockSpec((1, d, tf), lambda i, j, te, nv: (te[i], 0, j)),
        pl.BlockSpec((1, tf, d), lambda i, j, te, nv: (te[i], j, 0)),
    ]
    args += [w1, w3, w2]
    grid_spec = pltpu.PrefetchScalarGridSpec(
        num_scalar_prefetch=2,
        grid=(n_tiles, n_f),
        in_specs=in_specs,
        out_specs=out_specs,
        scratch_shapes=[pltpu.VMEM((tm, d), w1.dtype), pltpu.VMEM((tm, tf), w1.dtype),
                        pltpu.VMEM((tm, d), F32)],
    )
    out = pl.pallas_call(
        functools.partial(_ffn_kernel, has_gain=gain is not None, has_res=res,
                          has_fill=bool(fill_rows), tf=tf, ck=ck),
        out_shape=out_shape,
        grid_spec=grid_spec,
        compiler_params=_cparams(("arbitrary", "arbitrary")),
        name="swiglu",
    )(tile_expert, n_valid, *args)
    return out if fill_rows else out[0]


def _hgrn_gates(fz, loglb, l1mlb, oneml):
    e = _exp_neg(jnp.abs(fz))
    ope = 1.0 + e
    log_sig = jnp.minimum(fz, 0.0) - jnp.log(ope)
    cc = l1mlb + log_sig
    log_f = jnp.maximum(loglb, cc) + jnp.log(1.0 + _exp_neg(jnp.abs(loglb - cc)))
    k = oneml * (jnp.where(fz > 0.0, e, 1.0) / ope)
    return log_f, k


def _level_reference(b, w):
    n = b.shape[0]
    if w >= 4:
        return jnp.concatenate(
            [jnp.broadcast_to(b[j + w:j + w + 1, :], (2 * w, b.shape[1]))
             for j in range(0, n, 2 * w)], axis=0)
    b3 = b.reshape(n // SUBLANES, SUBLANES, b.shape[1])
    ahead = lambda s: pltpu.roll(b3, SUBLANES - s, 1)
    pos = lax.broadcasted_iota(I32, (1, SUBLANES, 1), 1) & (2 * w - 1)
    if w == 2:
        r3 = jnp.where(pos == 0, ahead(2),
                       jnp.where(pos == 1, ahead(1),
                                 jnp.where(pos == 2, b3, pltpu.roll(b3, 1, 1))))
    else:
        r3 = jnp.where(pos == 0, ahead(1), b3)
    return r3.reshape(b.shape)


def _lru_gates(xc, wra, bra, wix, bix, c8):
    r = _sigmoid(_dot(xc, wra) + bra)
    ig = _sigmoid(_dot(xc, wix) + bix)
    log_a = c8 * r
    a = jnp.exp(log_a)
    one_m_a2 = -jnp.tanh(log_a) * (a * a + 1.0)
    u = jnp.sqrt(one_m_a2) * (ig * xc)
    return a, u


def _mixer_block_kernel(x_ref, g1_ref, win_ref, lbp_ref, hgg_ref, cw_ref, cb_ref, wra_ref,
                        bra_ref, wix_ref, bix_ref, c8_ref, cm_ref, wout_ref, *rest, tl, n_cast):
    cast_src = rest[:n_cast]
    xo_ref, hs_ref, ls_ref, cs_ref = rest[n_cast:n_cast + 4]
    cast_dst = rest[n_cast + 4:2 * n_cast + 4]
    (st_scr, h_scr, xpad_scr, hb_scr, q_scr, lf_scr, k_scr, v_scr, g_scr, gl_scr, a_scr, u_scr,
     m_scr) = rest[2 * n_cast + 4:]
    l = pl.program_id(1)
    nl = pl.num_programs(1)
    for src, dst in zip(cast_src, cast_dst):
        dst[...] = src[...].astype(dst.dtype)

    @pl.when(l == 0)
    def _():
        st_scr[...] = jnp.zeros_like(st_scr)
        h_scr[...] = jnp.zeros_like(h_scr)
        xpad_scr[0:8, :] = jnp.zeros((8, LRU_WIDTH), F32)

    hb_scr[...] = _rms(x_ref[0], g1_ref[...]).astype(BF16)
    hb = hb_scr[...]
    zg = lambda j: _dot(hb, win_ref[:, j * HG_WIDTH:(j + 1) * HG_WIDTH])
    q_scr[...] = _silu(zg(0))
    log_f, kk = _hgrn_gates(zg(1), lbp_ref[0:1, :], lbp_ref[1:2, :], lbp_ref[2:3, :])
    lf_scr[...] = log_f
    k_scr[...] = kk
    v_scr[...] = zg(2)
    g_scr[...] = _silu(zg(3))
    xpad_scr[8:8 + tl, :] = zg(4)
    gl_scr[...] = _gelu_tanh(zg(5))
    xc = cb_ref[...]
    for j in range(CONV_K):
        xc = xc + xpad_scr[5 + j:5 + j + tl, :] * cw_ref[j:j + 1, :]
    tail = xpad_scr[tl + 5:tl + 8, :]
    xpad_scr[5:8, :] = tail
    a, u = _lru_gates(xc, wra_ref[...], bra_ref[...], wix_ref[...], bix_ref[...], c8_ref[...])
    a_scr[...] = a
    u_scr[...] = u

    row = lax.broadcasted_iota(I32, (CHUNK, 1), 0)
    ti = lax.broadcasted_iota(I32, (CHUNK, CHUNK), 0)
    si = lax.broadcasted_iota(I32, (CHUNK, CHUNK), 1)
    t3 = lax.broadcasted_iota(I32, (1, DIAG, 1), 1)
    sub3 = lax.broadcasted_iota(I32, (1, SUBLANES, 1), 1)
    nblk = CHUNK // DIAG
    ngrp = CHUNK // SUBLANES
    levels = []
    w = CHUNK // 2
    while w >= DIAG:
        sh = (2 * w).bit_length() - 1
        pairs = ((ti >> sh) == (si >> sh)) & ((ti & (2 * w - 1)) >= w) & ((si & (2 * w - 1)) < w)
        levels.append((w, (row & (2 * w - 1)) >= w, pairs))
        w //= 2

    def chunk_body(c, carry):
        r0 = pl.multiple_of(c * CHUNK, CHUNK)
        rows = pl.ds(r0, CHUNK)

        for gi in range(LRU_WIDTH // LANES):
            cs = slice(gi * LANES, (gi + 1) * LANES)
            aa = a_scr[rows, cs].reshape(ngrp, SUBLANES, LANES)
            uu = u_scr[rows, cs].reshape(ngrp, SUBLANES, LANES)
            d = 1
            while d < SUBLANES:
                keep = sub3 >= d
                a_sh = pltpu.roll(aa, d, 1)
                u_sh = pltpu.roll(uu, d, 1)
                uu = jnp.where(keep, aa * u_sh + uu, uu)
                aa = jnp.where(keep, aa * a_sh, aa)
                d *= 2
            hprev = h_scr[:, cs]
            groups = []
            for j in range(ngrp):
                hj = uu[j] + aa[j] * hprev
                groups.append(hj)
                hprev = hj[SUBLANES - 1:SUBLANES, :]
            hs = jnp.concatenate(groups, axis=0)
            h_scr[:, cs] = hprev
            m_scr[rows, HG_WIDTH + gi * LANES:HG_WIDTH + (gi + 1) * LANES] = (
                hs * gl_scr[rows, cs]).astype(BF16)

        q = q_scr[rows, :]
        k = k_scr[rows, :]
        v = v_scr[rows, :]
        f1, f2, f3 = _split3(lf_scr[rows, :])
        cm = cm_ref[...]
        b = _dot(cm, f1) + _dot(cm, f2) + _dot(cm, f3)

        for hh in range(HG_HEADS):
            cs = slice(hh * HG_D, (hh + 1) * HG_D)
            qh, kh, vh, bh = q[:, cs], k[:, cs], v[:, cs], b[:, cs]
            vb = vh.astype(BF16)
            amat = jnp.zeros((CHUNK, CHUNK), F32)
            for w, upper, pairs in levels:
                dec = _exp_neg(jnp.abs(bh - _level_reference(bh, w)))
                y = (jnp.where(upper, qh, kh) * dec).astype(BF16)
                amat = jnp.where(pairs, _dot_nt(y, y), amat)
            o = _dot(amat.astype(BF16), vb)
            st = st_scr[hh]
            o = o + _dot_nt((qh * jnp.exp(bh)).astype(BF16), st.astype(BF16))
            if DIAG == 1:
                o = o + jnp.sum(qh * kh, axis=-1, keepdims=True) * vh
            else:
                q3 = qh.reshape(nblk, DIAG, HG_D)
                k3 = kh.reshape(nblk, DIAG, HG_D)
                v3 = vh.reshape(nblk, DIAG, HG_D)
                b3 = bh.reshape(nblk, DIAG, HG_D)
                od = jnp.zeros((nblk, DIAG, HG_D), F32)
                for s in range(DIAG):
                    dec = jnp.exp(jnp.minimum(b3 - b3[:, s:s + 1, :], 0.0))
                    p = q3 * (k3[:, s:s + 1, :] * dec)
                    rs = jnp.sum(p, axis=-1, keepdims=True)
                    rs = jnp.where(t3 >= s, rs, 0.0)
                    od = od + rs * v3[:, s:s + 1, :]
                o = o + od.reshape(CHUNK, HG_D)
            bl = bh[CHUNK - 1:CHUNK, :]
            kdec = (kh * jnp.exp(bl - bh)).astype(BF16)
            st_scr[hh] = st * jnp.exp(bl) + _dot_tn(vb, kdec)
            o = o * lax.rsqrt(jnp.mean(o * o, axis=-1, keepdims=True) + EPS)
            o = o * hgg_ref[:, cs]
            o = o * g_scr[rows, cs]
            m_scr[rows, cs] = o.astype(BF16)
        return carry

    lax.fori_loop(0, tl // CHUNK, chunk_body, 0, unroll=tl // CHUNK)
    xo_ref[0] = x_ref[0] + _dot(m_scr[...], wout_ref[...])

    @pl.when(l == nl - 1)
    def _():
        for hh in range(HG_HEADS):
            hs_ref[0, hh] = st_scr[hh].T
        ls_ref[0] = h_scr[...]
        cs_ref[0] = xpad_scr[5:8, :]


def _slab_spec(c, steps, step_of):
    share = 1
    while (c.shape[0] * share) % (steps * 2 * SUBLANES):
        share *= 2
    assert steps % share == 0
    rows = c.shape[0] * share // steps
    return pl.BlockSpec((rows, c.shape[1]), lambda *ids: (step_of(*ids) // share, 0))


def _mixer_block(x, gain, w_in, w_out, layer, mp, cast=(), tl=512):
    bsz, seq, d = x.shape
    tl = min(tl, seq)
    nl = seq // tl
    steps = bsz * nl
    cast_dtypes = [dt for _, dt in cast]
    cast = [c for c, _ in cast]
    full = lambda shape: pl.BlockSpec(shape, lambda b, l: (0,) * len(shape))
    stacked = lambda shape: pl.BlockSpec((None,) + shape, lambda b, l: (layer, 0, 0))
    slab = lambda c: _slab_spec(c, steps, lambda b, l: b * nl + l)
    out_shapes = [
        jax.ShapeDtypeStruct((bsz, seq, d), F32),
        jax.ShapeDtypeStruct((bsz, HG_HEADS, HG_D, HG_D), F32),
        jax.ShapeDtypeStruct((bsz, 1, LRU_WIDTH), F32),
        jax.ShapeDtypeStruct((bsz, CONV_K - 1, LRU_WIDTH), F32),
    ] + [jax.ShapeDtypeStruct(c.shape, dt) for c, dt in zip(cast, cast_dtypes)]
    half = lambda dt: pltpu.VMEM((tl, HG_WIDTH), dt)
    xo, hg, lru, conv, *rounded = pl.pallas_call(
        functools.partial(_mixer_block_kernel, tl=tl, n_cast=len(cast)),
        out_shape=out_shapes,
        grid=(bsz, nl),
        in_specs=[
            pl.BlockSpec((1, tl, d), lambda b, l: (b, l, 0)),
            full((1, d)), stacked((d, IN_WIDTH)),
            full((3, HG_WIDTH)), full((1, HG_WIDTH)), full((CONV_K, LRU_WIDTH)),
            full((1, LRU_WIDTH)), full((LRU_WIDTH, LRU_WIDTH)), full((1, LRU_WIDTH)),
            full((LRU_WIDTH, LRU_WIDTH)), full((1, LRU_WIDTH)), full((1, LRU_WIDTH)),
            full(mp["cm"].shape), stacked((d, d)),
        ] + [slab(c) for c in cast],
        out_specs=[
            pl.BlockSpec((1, tl, d), lambda b, l: (b, l, 0)),
            pl.BlockSpec((1, HG_HEADS, HG_D, HG_D), lambda b, l: (b, 0, 0, 0)),
            pl.BlockSpec((1, 1, LRU_WIDTH), lambda b, l: (b, 0, 0)),
            pl.BlockSpec((1, CONV_K - 1, LRU_WIDTH), lambda b, l: (b, 0, 0)),
        ] + [slab(c) for c in cast],
        scratch_shapes=[
            pltpu.VMEM((HG_HEADS, HG_D, HG_D), F32),
            pltpu.VMEM((1, LRU_WIDTH), F32),
            pltpu.VMEM((tl + 8, LRU_WIDTH), F32),
            pltpu.VMEM((tl, d), BF16),
            half(F32), half(F32), half(F32), half(F32), half(F32), half(F32), half(F32),
            half(F32),
            pltpu.VMEM((tl, d), BF16),
        ],
        compiler_params=_cparams(("parallel", "arbitrary")),
        name="mixer_block",
    )(x, gain.reshape(1, d), w_in, mp["lbp"], mp["hgg"], mp["cw"], mp["cb"], mp["wra"],
      mp["bra"], mp["wix"], mp["bix"], mp["c8"], mp["cm"], w_out, *cast)
    return xo, hg, lru.reshape(bsz, LRU_WIDTH), conv, rounded


def _mixer_step_kernel(z_ref, sh_ref, sl_ref, sc_ref, lbp_ref, hgg_ref, cw_ref, cb_ref, wra_ref,
                       bra_ref, wix_ref, bix_ref, c8_ref, *rest, bb, aliased):
    m_ref, hn_ref, ln_ref, cn_ref, o_scr = rest[1:] if aliased else rest
    z = z_ref[...]
    xr = z[:, 4 * HG_WIDTH:4 * HG_WIDTH + LRU_WIDTH]
    gr = z[:, 4 * HG_WIDTH + LRU_WIDTH:IN_WIDTH]
    buf = sc_ref[...]
    xc = cb_ref[...]
    for j in range(CONV_K - 1):
        xc = xc + buf[:, j * LRU_WIDTH:(j + 1) * LRU_WIDTH] * cw_ref[j:j + 1, :]
    xc = xc + xr * cw_ref[CONV_K - 1:CONV_K, :]
    cn_ref[:, 0:(CONV_K - 2) * LRU_WIDTH] = buf[:, LRU_WIDTH:(CONV_K - 1) * LRU_WIDTH]
    cn_ref[:, (CONV_K - 2) * LRU_WIDTH:(CONV_K - 1) * LRU_WIDTH] = xr
    a, u = _lru_gates(xc, wra_ref[...], bra_ref[...], wix_ref[...], bix_ref[...], c8_ref[...])
    hnew = u + a * sl_ref[...]
    ln_ref[...] = hnew
    m_ref[:, HG_WIDTH:D_MODEL] = (hnew * _gelu_tanh(gr)).astype(m_ref.dtype)

    q = _silu(z[:, 0:HG_WIDTH])
    log_f, k = _hgrn_gates(z[:, HG_WIDTH:2 * HG_WIDTH], lbp_ref[0:1, :], lbp_ref[1:2, :],
                           lbp_ref[2:3, :])
    f = jnp.exp(log_f)
    v = z[:, 2 * HG_WIDTH:3 * HG_WIDTH]
    gate = _silu(z[:, 3 * HG_WIDTH:4 * HG_WIDTH])
    zpad = jnp.zeros((HG_D - bb, HG_D), F32)
    for hh in range(HG_HEADS):
        cs = slice(hh * HG_D, (hh + 1) * HG_D)
        ft = jnp.concatenate([f[:, cs], zpad], axis=0).T
        kt = jnp.concatenate([k[:, cs], zpad], axis=0).T
        qt = jnp.concatenate([q[:, cs], zpad], axis=0).T
        for j in range(bb):
            fcol = jnp.broadcast_to(ft[:, j:j + 1], (HG_D, HG_D))
            kcol = jnp.broadcast_to(kt[:, j:j + 1], (HG_D, HG_D))
            qcol = jnp.broadcast_to(qt[:, j:j + 1], (HG_D, HG_D))
            vrow = jnp.broadcast_to(v[j:j + 1, cs], (HG_D, HG_D))
            sn = fcol * sh_ref[j, hh] + kcol * vrow
            hn_ref[j, hh] = sn
            o_scr[j:j + 1, cs] = jnp.sum(qcol * sn, axis=0, keepdims=True)
    for hh in range(HG_HEADS):
        cs = slice(hh * HG_D, (hh + 1) * HG_D)
        o = o_scr[:, cs]
        o = o * lax.rsqrt(jnp.mean(o * o, axis=-1, keepdims=True) + EPS)
        o = o * hgg_ref[:, cs]
        o = o * gate[:, cs]
        m_ref[:, cs] = o.astype(m_ref.dtype)


def _mixer_step(z, s_hg_all, layer, s_lru, s_conv, mp, hg_buf=None, bb=16):
    bsz = z.shape[0]
    bb = min(bb, bsz)
    cw3 = (CONV_K - 1) * LRU_WIDTH
    full = lambda shape: pl.BlockSpec(shape, lambda i: (0,) * len(shape))
    state_spec = pl.BlockSpec((None, bb, HG_HEADS, HG_D, HG_D), lambda i: (layer, i, 0, 0, 0))
    out_shapes = (
        jax.ShapeDtypeStruct((bsz, D_MODEL), mp["wra"].dtype),
        jax.ShapeDtypeStruct(s_hg_all.shape, F32),
        jax.ShapeDtypeStruct((bsz, LRU_WIDTH), F32),
        jax.ShapeDtypeStruct((bsz, cw3), F32),
    )
    in_specs = [
        pl.BlockSpec((bb, IN_WIDTH), lambda i: (i, 0)),
        state_spec,
        pl.BlockSpec((bb, LRU_WIDTH), lambda i: (i, 0)),
        pl.BlockSpec((bb, cw3), lambda i: (i, 0)),
        full((3, HG_WIDTH)), full((1, HG_WIDTH)), full((CONV_K, LRU_WIDTH)),
        full((1, LRU_WIDTH)), full((LRU_WIDTH, LRU_WIDTH)), full((1, LRU_WIDTH)),
        full((LRU_WIDTH, LRU_WIDTH)), full((1, LRU_WIDTH)), full((1, LRU_WIDTH)),
    ]
    args = [z, s_hg_all, s_lru, s_conv.reshape(bsz, cw3), mp["lbp"], mp["hgg"], mp["cw"],
            mp["cb"], mp["wra"], mp["bra"], mp["wix"], mp["bix"], mp["c8"]]
    aliases = {}
    if hg_buf is not None:
        aliases = {len(args): 1}
        in_specs.append(pl.BlockSpec(memory_space=pl.ANY))
        args.append(hg_buf)
    merged, hg, lru, conv = pl.pallas_call(
        functools.partial(_mixer_step_kernel, bb=bb, aliased=hg_buf is not None),
        out_shape=out_shapes,
        grid=(bsz // bb,),
        in_specs=in_specs,
        out_specs=(
            pl.BlockSpec((bb, D_MODEL), lambda i: (i, 0)),
            state_spec,
            pl.BlockSpec((bb, LRU_WIDTH), lambda i: (i, 0)),
            pl.BlockSpec((bb, cw3), lambda i: (i, 0)),
        ),
        scratch_shapes=[pltpu.VMEM((bb, HG_WIDTH), F32)],
        input_output_aliases=aliases,
        compiler_params=_cparams(("parallel",)),
        name="mixer_step",
    )(*args)
    return merged, hg, lru, conv.reshape(bsz, CONV_K - 1, LRU_WIDTH)


def _router_kernel(x_ref, g_ref, rw_ref, tril_ref, *refs, n_cast, exact_logits):
    cast_src = refs[:n_cast]
    info_ref, idx_ref, cnt_ref, zero_ref = refs[n_cast:n_cast + 4]
    cast_dst = refs[n_cast + 4:2 * n_cast + 4]
    carry_scr = refs[-1]
    i = pl.program_id(0)

    @pl.when(i == 0)
    def _():
        carry_scr[...] = jnp.zeros_like(carry_scr)

    zero_ref[...] = jnp.zeros_like(zero_ref)
    for src, dst in zip(cast_src, cast_dst):
        dst[...] = src[...].astype(dst.dtype)

    h = _rms(x_ref[...], g_ref[...])
    rw = rw_ref[...]
    if exact_logits:
        logits = jnp.dot(h, rw, preferred_element_type=F32, precision=lax.Precision.HIGHEST)
    else:
        h1 = h.astype(BF16)
        h2 = (h - h1.astype(F32)).astype(BF16)
        r1 = rw.astype(BF16)
        r2 = (rw - r1.astype(F32)).astype(BF16)
        logits = _dot(h1, r1) + (_dot(h1, r2) + _dot(h2, r1))
    lane = lax.broadcasted_iota(I32, logits.shape, 1)
    neg = jnp.float32(-jnp.inf)
    logits = jnp.where(lane < N_EXPERTS, logits, neg)
    m1 = jnp.max(logits, axis=-1, keepdims=True)
    i1 = jnp.min(jnp.where(logits == m1, lane, LANES), axis=-1, keepdims=True)
    l2 = jnp.where(lane == i1, neg, logits)
    m2 = jnp.max(l2, axis=-1, keepdims=True)
    i2 = jnp.min(jnp.where(l2 == m2, lane, LANES), axis=-1, keepdims=True)
    e = jnp.exp(m2 - m1)
    p1 = 1.0 / (1.0 + e)
    p2 = e / (1.0 + e)
    oh1 = (lane == i1).astype(F32)
    oh2 = (lane == i2).astype(F32)
    sel = oh1 + oh2
    before = _dot(tril_ref[...], sel.astype(BF16)) + carry_scr[...]
    r1 = jnp.sum(oh1 * before, axis=-1, keepdims=True)
    r2 = jnp.sum(oh2 * before, axis=-1, keepdims=True)
    carry_scr[...] += jnp.sum(sel, axis=0, keepdims=True)
    info = jnp.where(lane == 0, i1.astype(F32), 0.0)
    info = jnp.where(lane == 1, i2.astype(F32), info)
    info = jnp.where(lane == 2, r1, info)
    info = jnp.where(lane == 3, r2, info)
    info = jnp.where(lane == 4, p1, info)
    info = jnp.where(lane == 5, p2, info)
    info_ref[...] = info
    idx_ref[...] = info.T[0:SUBLANES, :].astype(I32)
    cnt_ref[...] = carry_scr[...]


def _router(x, gain, rw_pad, n_rows, cast=(), exact_logits=True, tm=512):
    t, d = x.shape
    tm = min(tm, t)
    steps = t // tm
    zrows = n_rows // steps
    assert zrows * steps == n_rows and zrows % SUBLANES == 0
    slab = lambda c: _slab_spec(c, steps, lambda i: i)
    tril = jnp.tril(jnp.ones((tm, tm), BF16), -1)
    return pl.pallas_call(
        functools.partial(_router_kernel, n_cast=len(cast), exact_logits=exact_logits),
        out_shape=[jax.ShapeDtypeStruct((t, LANES), F32),
                   jax.ShapeDtypeStruct((SUBLANES, t), I32),
                   jax.ShapeDtypeStruct((1, LANES), F32),
                   jax.ShapeDtypeStruct((n_rows, d), F32)]
        + [jax.ShapeDtypeStruct(c.shape, BF16) for c in cast],
        grid=(steps,),
        in_specs=[
            pl.BlockSpec((tm, d), lambda i: (i, 0)),
            pl.BlockSpec((1, d), lambda i: (0, 0)),
            pl.BlockSpec((d, LANES), lambda i: (0, 0)),
            pl.BlockSpec((tm, tm), lambda i: (0, 0)),
        ] + [slab(c) for c in cast],
        out_specs=[pl.BlockSpec((tm, LANES), lambda i: (i, 0)),
                   pl.BlockSpec((SUBLANES, tm), lambda i: (0, i)),
                   pl.BlockSpec((1, LANES), lambda i: (0, 0)),
                   pl.BlockSpec((zrows, d), lambda i: (i, 0))] + [slab(c) for c in cast],
        scratch_shapes=[pltpu.VMEM((1, LANES), F32)],
        compiler_params=_cparams(("arbitrary",)),
        name="router",
    )(x, gain.reshape(1, d), rw_pad, tril, *cast)


def _row_copy(src, dst, s, d, sem):
    return pltpu.make_async_copy(src.at[pl.ds(s, 1)], dst.at[pl.ds(d, 1)], sem)


def _tile_dest(dest, t, td):
    nb = t // td
    dest3 = jnp.concatenate([dest[0].reshape(nb, td), dest[1].reshape(nb, td)], axis=1)
    return dest3.reshape(nb, 1, 2 * td)


def _dispatch_kernel(dest_ref, x_ref, g_ref, xs_in_ref, xs_ref, pk_scr, sem, *, td):
    del xs_in_ref
    pk_scr[...] = _rms(x_ref[...], g_ref[...])

    def issue(r, c):
        _row_copy(pk_scr, xs_ref, r, dest_ref[0, 0, r], sem).start(priority=0)
        _row_copy(pk_scr, xs_ref, r, dest_ref[0, 0, td + r], sem).start(priority=1)
        return c

    lax.fori_loop(0, td, issue, 0, unroll=8)
    for _ in range(2):
        pltpu.make_async_copy(pk_scr, xs_ref.at[pl.ds(0, td)], sem).wait()


def _dispatch(x, gain, dest, xs0, td=2048):
    t, d = x.shape
    td = min(td, t)
    n_rows = xs0.shape[0]
    return pl.pallas_call(
        functools.partial(_dispatch_kernel, td=td),
        out_shape=jax.ShapeDtypeStruct((n_rows, d), F32),
        grid=(t // td,),
        in_specs=[
            pl.BlockSpec((1, 1, 2 * td), lambda i: (i, 0, 0), memory_space=pltpu.SMEM),
            pl.BlockSpec((td, d), lambda i: (i, 0)),
            pl.BlockSpec((1, d), lambda i: (0, 0)),
            pl.BlockSpec(memory_space=pl.ANY),
        ],
        out_specs=pl.BlockSpec(memory_space=pl.ANY),
        scratch_shapes=[pltpu.VMEM((td, d), F32), pltpu.SemaphoreType.DMA(())],
        input_output_aliases={3: 0},
        compiler_params=pltpu.CompilerParams(dimension_semantics=("arbitrary",),
                                             vmem_limit_bytes=VMEM_LIMIT_BYTES,
                                             has_side_effects=True),
        name="dispatch",
    )(_tile_dest(dest, t, td), x, gain.reshape(1, d), xs0)


def _combine_kernel(dest_ref, nxt_ref, x_ref, info_ref, g_ref, ys_ref, o_ref, y_scr, sems, *, td):
    i = pl.program_id(0)
    nb = pl.num_programs(0)

    def fetch(idx_ref, slot):
        def issue(r, c):
            _row_copy(ys_ref, y_scr.at[slot, 0], idx_ref[0, 0, r], r,
                      sems.at[slot]).start(priority=0)
            _row_copy(ys_ref, y_scr.at[slot, 1], idx_ref[0, 0, td + r], r,
                      sems.at[slot]).start(priority=1)
            return c

        lax.fori_loop(0, td, issue, 0, unroll=8)

    def finish(slot):
        for j in range(2):
            pltpu.make_async_copy(ys_ref.at[pl.ds(0, td)], y_scr.at[slot, j],
                                  sems.at[slot]).wait()
        p1 = info_ref[:, 4:5]
        p2 = info_ref[:, 5:6]
        x = x_ref[...] + (p1 * y_scr[slot, 0] + p2 * y_scr[slot, 1])
        o_ref[...] = _rms(x, g_ref[...])

    @pl.when(i == 0)
    def _():
        fetch(dest_ref, 0)

    for slot in range(2):
        @pl.when(i % 2 == slot)
        def _(slot=slot):
            @pl.when(i + 1 < nb)
            def _():
                fetch(nxt_ref, 1 - slot)

            finish(slot)


def _combine(x, ys, dest, info, gain, td=1024):
    t, d = x.shape
    td = min(td, t)
    nb = t // td
    row = pl.BlockSpec((td, d), lambda i: (i, 0))
    tiled = _tile_dest(dest, t, td)
    return pl.pallas_call(
        functools.partial(_combine_kernel, td=td),
        out_shape=jax.ShapeDtypeStruct((t, d), F32),
        grid=(nb,),
        in_specs=[
            pl.BlockSpec((1, 1, 2 * td), lambda i: (i, 0, 0), memory_space=pltpu.SMEM),
            pl.BlockSpec((1, 1, 2 * td), lambda i: (jnp.minimum(i + 1, nb - 1), 0, 0),
                         memory_space=pltpu.SMEM),
            row,
            pl.BlockSpec((td, LANES), lambda i: (i, 0)),
            pl.BlockSpec((1, d), lambda i: (0, 0)),
            pl.BlockSpec(memory_space=pl.ANY),
        ],
        out_specs=row,
        scratch_shapes=[pltpu.VMEM((2, 2, td, d), F32), pltpu.SemaphoreType.DMA((2,))],
        compiler_params=_cparams(("arbitrary",)),
        name="combine",
    )(tiled, tiled, x, info, gain.reshape(1, d), ys)


def _rows2d(w):
    return w.reshape(-1, w.shape[-1])


def _moe_rows(group_sizes, tm_e=TILE_ROWS):
    return (2 * sum(group_sizes) + N_EXPERTS * (tm_e - 1)) // tm_e * tm_e


def _moe(xs_in, norm_g, rw_pad, experts, shared, final_g, tm_e=TILE_ROWS):
    n_rows = _moe_rows([x.shape[0] for x in xs_in], tm_e)
    n_tiles = n_rows // tm_e
    routed = []
    for x in xs_in:
        todo = [(name, w) for name, w in experts if name not in shared]
        steps = x.shape[0] // min(TILE_ROWS, x.shape[0])
        fill = steps * SUBLANES if "moe_rows" in shared else n_rows
        info, idx, cnt, zeros, *rounded = _router(
            x, norm_g, rw_pad, fill, cast=[_rows2d(w) for _, w in todo],
            exact_logits=x.shape[0] <= TILE_ROWS)
        for (name, w), r in zip(todo, rounded):
            shared[name] = r.reshape(w.shape)
        shared.setdefault("moe_rows", zeros)
        routed.append((info, idx, cnt[0, :N_EXPERTS].astype(I32)))
    xs0 = shared["moe_rows"]
    assert xs0.shape[0] == n_rows
    w1, w3, w2 = (shared[name] for name, _ in experts)
    counts = sum(c for _, _, c in routed)
    padded = ((counts + tm_e - 1) // tm_e) * tm_e
    ends = jnp.cumsum(padded)
    starts = ends - padded
    tile_start = jnp.arange(n_tiles, dtype=I32) * tm_e
    tile_expert = jnp.minimum(
        jnp.sum((tile_start[:, None] >= ends[None, :]).astype(I32), axis=1), N_EXPERTS - 1)
    n_valid = (ends[-1] // tm_e).reshape(1).astype(I32)
    dests = []
    base = starts
    for info, idx, cnt in routed:
        e1, e2, r1, r2 = idx[0], idx[1], idx[2], idx[3]
        dests.append(jnp.stack([base[e1] + r1, base[e2] + r2]))
        base = base + cnt
    rows = xs0
    for x, dest in zip(xs_in, dests):
        rows = _dispatch(x, norm_g, dest, rows)
    ys = _ffn(rows, w1, w3, w2, tile_expert.astype(I32), n_valid, tm=tm_e, n_f=1)
    return [_combine(x, ys, dest, info, final_g)
            for x, dest, (info, _, _) in zip(xs_in, dests, routed)]


def _block_diag(w):
    n, c, d = w.shape
    eye = jnp.eye(n, dtype=w.dtype)
    return (w[:, :, None, :] * eye[:, None, :, None]).reshape(n * c, n * d)


def _cum_matrix():
    t = jnp.arange(CHUNK)
    return (t[None, :] <= t[:, None]).astype(BF16)


def _mixer_params(l, lb, hg_norm_g, conv_w, conv_b, w_ra, b_ra, w_ix, b_ix, lru_lambda, wdtype):
    row = lambda a: a.reshape(1, -1).astype(F32)
    return {
        "lbp": jnp.stack([jnp.log(lb), jnp.log1p(-lb), 1.0 - lb]).astype(F32),
        "hgg": row(hg_norm_g[l]),
        "cw": conv_w[l].astype(F32),
        "cb": row(conv_b[l]),
        "wra": _block_diag(w_ra[l].astype(F32)).astype(wdtype),
        "bra": row(b_ra[l]),
        "wix": _block_diag(w_ix[l].astype(F32)).astype(wdtype),
        "bix": row(b_ix[l]),
        "c8": row(-LRU_C * jax.nn.softplus(-lru_lambda[l].astype(F32))),
        "cm": _cum_matrix(),
    }


def _trunk(x, seq_shape, states, prm):
    bsz, seq = seq_shape
    one = jnp.ones((1,), I32)
    new_hg, new_lru, new_conv = [], [], []
    shared = prm["shared"]
    hg_stack = None if states is None else shared.get("hg_seed", states[0])
    for l in range(2):
        mp = prm["mixer"][l]
        if states is None:
            x3 = x.reshape(bsz, seq, D_MODEL)
            jobs = [j for j in prm["side"][l] if j[0] not in shared]
            if l == 0:
                w_in, w_out, layer = prm["w_in"], prm["w_out"], 0
            else:
                w_in, w_out, layer = shared["w_in"], shared["w_out"], 1
            x, s1, s2, s3, done = _mixer_block(
                x3, prm["norm1_g"][l], w_in, w_out, layer, mp,
                cast=[(_rows2d(a), dt) for _, a, dt in jobs])
            for (name, a, _), r in zip(jobs, done):
                shared[name] = r.reshape(a.shape)
            x = x.reshape(bsz * seq, D_MODEL)
            new_hg.append(s1)
        else:
            z = _linear(x, prm["w_in"], l, gain=prm["norm1_g"][l])
            merged, hg_stack, s2, s3 = _mixer_step(z, states[0], l, states[1][l], states[2][l],
                                                   mp, hg_buf=hg_stack)
            x = _linear(merged, prm["w_out"], l, res=x)
        new_lru.append(s2)
        new_conv.append(s3)
        if l == 0:
            n_tiles = max(x.shape[0] // TILE_ROWS, 1)
            if states is None:
                f1, f3, f2 = shared["ffn_w1"], shared["ffn_w3"], shared["ffn_w2"]
            else:
                f1, f3, f2 = prm["ffn_w1"], prm["ffn_w3"], prm["ffn_w2"]
            n_f = 1 if f1.dtype == BF16 else f1.shape[2] // F_CHUNK
            fill = prm["moe_rows"] if states is None else 0
            x = _ffn(x, f1, f3, f2, jnp.zeros((n_tiles,), I32), one * n_tiles,
                     gain=prm["norm2_g"][0], res=True, fill_rows=fill, tm=TILE_ROWS, n_f=n_f,
                     ck=F_CHUNK)
            if fill:
                x, shared["moe_rows"] = x
    hg_all = jnp.stack(new_hg) if states is None else hg_stack
    return x, hg_all, jnp.stack(new_lru), jnp.stack(new_conv)


def kernel(x_prompt, x_sample, state_hgrn, state_lru, state_conv, norm1_g, w_in, lower_bounds,
           hg_norm_g, conv_w, conv_b, w_ra, b_ra, w_ix, b_ix, lru_lambda, w_out, norm2_g,
           ffn_w1, ffn_w3, ffn_w2, router_w, moe_w1, moe_w3, moe_w2, final_norm_g):
    lb_all = jnp.cumsum(jax.nn.softmax(lower_bounds.astype(F32), axis=0), axis=0)
    lb_all = lb_all - lb_all[0]
    f32 = lambda a: a.astype(F32)
    common = {
        "experts": [("moe_w1", f32(moe_w1[0])), ("moe_w3", f32(moe_w3[0])),
                    ("moe_w2", f32(moe_w2[0]))],
        "side": {0: [("moe_w3", f32(moe_w3[0]), BF16), ("ffn_w1", f32(ffn_w1), BF16),
                     ("ffn_w3", f32(ffn_w3), BF16), ("ffn_w2", f32(ffn_w2), BF16),
                     ("w_in", f32(w_in), BF16), ("w_out", f32(w_out), BF16)],
                 1: [("moe_w2", f32(moe_w2[0]), BF16), ("hg_seed", f32(state_hgrn), F32)]},
        "shared": {},
        "moe_rows": _moe_rows([x_prompt.shape[0] * x_prompt.shape[1], x_sample.shape[0]]),
        "rw_pad": jnp.pad(router_w[0].astype(F32), ((0, 0), (0, LANES - N_EXPERTS))),
        "norm1_g": norm1_g, "norm2_g": norm2_g, "final_norm_g": final_norm_g,
    }
    mixer = lambda wdtype: [_mixer_params(l, lb_all[l], hg_norm_g, conv_w, conv_b, w_ra, b_ra,
                                          w_ix, b_ix, lru_lambda, wdtype) for l in range(2)]
    prompt = dict(common, mixer=mixer(BF16), w_in=w_in[:1].astype(BF16),
                  w_out=w_out[:1].astype(BF16))

    bp, sp, d = x_prompt.shape
    bs = x_sample.shape[0]
    x_p, hg_p, lru_p, conv_p = _trunk(x_prompt.reshape(bp * sp, d), (bp, sp), None, prompt)
    sample = dict(common, mixer=mixer(F32), w_in=f32(w_in), w_out=f32(w_out),
                  ffn_w1=f32(ffn_w1), ffn_w3=f32(ffn_w3), ffn_w2=f32(ffn_w2))
    x_s, hg_s, lru_s, conv_s = _trunk(x_sample.reshape(bs, d), (bs, 1),
                                      (state_hgrn, state_lru, state_conv), sample)
    y_p, y_s = _moe([x_p, x_s], norm2_g[1], common["rw_pad"], common["experts"],
                    common["shared"], final_norm_g)
    return (y_p.reshape(bp, sp, d), y_s.reshape(bs, 1, d), hg_p, lru_p, conv_p, hg_s, lru_s,
            conv_s)
```

```python
import functools

import jax
import jax.numpy as jnp
from jax import lax
from jax.experimental import pallas as pl
from jax.experimental.pallas import tpu as pltpu

F32 = jnp.float32
BF16 = jnp.bfloat16
I32 = jnp.int32

D_MODEL = 1024
HG_WIDTH = 512
HG_HEADS = 4
HG_D = 128
LRU_WIDTH = 512
LRU_BLOCKS = 8
LRU_C = 8.0
CONV_K = 4
IN_WIDTH = 4 * HG_WIDTH + 2 * LRU_WIDTH
N_EXPERTS = 8
EPS = 1e-6

CHUNK = 128
DIAG = 1
LANES = 128
SUBLANES = 8
TILE_ROWS = 512
F_CHUNK = 256
VMEM_LIMIT_BYTES = 56 * 1024 * 1024


def _cparams(sem):
    return pltpu.CompilerParams(dimension_semantics=sem, vmem_limit_bytes=VMEM_LIMIT_BYTES)


def _exp_neg(x):
    return jnp.exp2(x * -1.4426950408889634)


def _sigmoid(x):
    return 1.0 / (1.0 + _exp_neg(x))


def _silu(x):
    return x * _sigmoid(x)


def _gelu_tanh(x):
    c = 0.7978845608028654
    return 0.5 * x * (1.0 + jnp.tanh(c * (x + 0.044715 * (x * x * x))))


def _rms(x, g):
    return x * lax.rsqrt(jnp.mean(x * x, axis=-1, keepdims=True) + EPS) * g


def _dot(a, b):
    if b.dtype == F32:
        return jnp.dot(a.astype(F32), b, preferred_element_type=F32,
                       precision=lax.Precision.HIGHEST)
    return jnp.dot(a.astype(BF16), b, preferred_element_type=F32)


def _dot_nt(a, b):
    return lax.dot_general(a, b, (((1,), (1,)), ((), ())), preferred_element_type=F32)


def _dot_tn(a, b):
    return lax.dot_general(a, b, (((0,), (0,)), ((), ())), preferred_element_type=F32)


def _split3(x):
    x1 = x.astype(BF16)
    r1 = x - x1.astype(F32)
    x2 = r1.astype(BF16)
    x3 = (r1 - x2.astype(F32)).astype(BF16)
    return x1, x2, x3


def _linear_kernel(*refs, has_gain, has_res):
    it = iter(refs)
    x_ref = next(it)
    g_ref = next(it) if has_gain else None
    w_ref = next(it)
    r_ref = next(it) if has_res else None
    o_ref = next(it)
    h_scr = next(it)

    @pl.when(pl.program_id(1) == 0)
    def _():
        x = x_ref[...].astype(F32)
        if has_gain:
            x = _rms(x, g_ref[...])
        h_scr[...] = x.astype(h_scr.dtype)

    acc = _dot(h_scr[...], w_ref[...])
    if has_res:
        acc = acc + r_ref[...]
    o_ref[...] = acc.astype(o_ref.dtype)


def _linear(x, w, layer, gain=None, res=None, tm=1024, tn=1024):
    m, k = x.shape
    n = w.shape[2]
    tm = min(tm, m)
    tn = min(tn, n)
    in_specs = [pl.BlockSpec((tm, k), lambda i, j: (i, 0))]
    args = [x]
    if gain is not None:
        in_specs.append(pl.BlockSpec((1, k), lambda i, j: (0, 0)))
        args.append(gain.reshape(1, k))
    in_specs.append(pl.BlockSpec((None, k, tn), lambda i, j: (layer, 0, j)))
    args.append(w)
    if res is not None:
        in_specs.append(pl.BlockSpec((tm, tn), lambda i, j: (i, j)))
        args.append(res)
    return pl.pallas_call(
        functools.partial(_linear_kernel, has_gain=gain is not None, has_res=res is not None),
        out_shape=jax.ShapeDtypeStruct((m, n), F32),
        grid=(m // tm, n // tn),
        in_specs=in_specs,
        out_specs=pl.BlockSpec((tm, tn), lambda i, j: (i, j)),
        scratch_shapes=[pltpu.VMEM((tm, k), w.dtype)],
        compiler_params=_cparams(("parallel", "arbitrary")),
        name="linear",
    )(*args)


def _ffn_kernel(te_ref, nv_ref, *refs, has_gain, has_res, has_fill, tf, ck):
    del te_ref
    it = iter(refs)
    x_ref = next(it)
    g_ref = next(it) if has_gain else None
    w1_ref, w3_ref, w2_ref, o_ref = next(it), next(it), next(it), next(it)
    fill_ref = next(it) if has_fill else None
    xb_scr, a_scr, acc_scr = it
    i = pl.program_id(0)
    j = pl.program_id(1)
    nj = pl.num_programs(1)
    valid = i < nv_ref[0]
    if has_fill:
        fill_ref[...] = jnp.zeros_like(fill_ref)

    @pl.when(valid)
    def _():
        @pl.when(j == 0)
        def _():
            x = x_ref[...]
            if has_gain:
                x = _rms(x, g_ref[...])
            xb_scr[...] = x.astype(xb_scr.dtype)

        xb = xb_scr[...]
        for c in range(tf // ck):
            h1 = _dot(xb, w1_ref[0, :, c * ck:(c + 1) * ck])
            h3 = _dot(xb, w3_ref[0, :, c * ck:(c + 1) * ck])
            a_scr[:, c * ck:(c + 1) * ck] = (_silu(h1) * h3).astype(a_scr.dtype)
        part = _dot(a_scr[...], w2_ref[0])

        @pl.when(j == 0)
        def _():
            acc_scr[...] = part

        @pl.when(j > 0)
        def _():
            acc_scr[...] += part

        @pl.when(j == nj - 1)
        def _():
            out = acc_scr[...]
            if has_res:
                out = x_ref[...] + out
            o_ref[...] = out

    @pl.when(jnp.logical_and(jnp.logical_not(valid), j == nj - 1))
    def _():
        o_ref[...] = jnp.zeros_like(o_ref)


def _ffn(x, w1, w3, w2, tile_expert, n_valid, gain=None, res=False, fill_rows=0, tm=TILE_ROWS,
         n_f=2, ck=F_CHUNK):
    p, d = x.shape
    f = w1.shape[2]
    tm = min(tm, p)
    tf = f // n_f
    n_tiles = p // tm
    out_specs = [pl.BlockSpec((tm, d), lambda i, j, te, nv: (i, 0))]
    out_shape = [jax.ShapeDtypeStruct((p, d), F32)]
    if fill_rows:
        assert n_f == 1 and fill_rows % (n_tiles * SUBLANES) == 0
        out_specs.append(pl.BlockSpec((fill_rows // n_tiles, d), lambda i, j, te, nv: (i, 0)))
        out_shape.append(jax.ShapeDtypeStruct((fill_rows, d), F32))
    in_specs = [pl.BlockSpec((tm, d), lambda i, j, te, nv: (i, 0))]
    args = [x]
    if gain is not None:
        in_specs.append(pl.BlockSpec((1, d), lambda i, j, te, nv: (0, 0)))
        args.append(gain.reshape(1, d))
    in_specs += [
        pl.BlockSpec((1, d, tf), lambda i, j, te, nv: (te[i], 0, j)),
        pl.BlockSpec((1, d, tf), lambda i, j, te, nv: (te[i], 0, j)),
        pl.BlockSpec((1, tf, d), lambda i, j, te, nv: (te[i], j, 0)),
    ]
    args += [w1, w3, w2]
    grid_spec = pltpu.PrefetchScalarGridSpec(
        num_scalar_prefetch=2,
        grid=(n_tiles, n_f),
        in_specs=in_specs,
        out_specs=out_specs,
        scratch_shapes=[pltpu.VMEM((tm, d), w1.dtype), pltpu.VMEM((tm, tf), w1.dtype),
                        pltpu.VMEM((tm, d), F32)],
    )
    out = pl.pallas_call(
        functools.partial(_ffn_kernel, has_gain=gain is not None, has_res=res,
                          has_fill=bool(fill_rows), tf=tf, ck=ck),
        out_shape=out_shape,
        grid_spec=grid_spec,
        compiler_params=_cparams(("arbitrary", "arbitrary")),
        name="swiglu",
    )(tile_expert, n_valid, *args)
    return out if fill_rows else out[0]


def _hgrn_gates(fz, loglb, l1mlb, oneml):
    e = _exp_neg(jnp.abs(fz))
    ope = 1.0 + e
    log_sig = jnp.minimum(fz, 0.0) - jnp.log(ope)
    cc = l1mlb + log_sig
    log_f = jnp.maximum(loglb, cc) + jnp.log(1.0 + _exp_neg(jnp.abs(loglb - cc)))
    k = oneml * (jnp.where(fz > 0.0, e, 1.0) / ope)
    return log_f, k


def _level_reference(b, w):
    n = b.shape[0]
    if w >= 4:
        return jnp.concatenate(
            [jnp.broadcast_to(b[j + w:j + w + 1, :], (2 * w, b.shape[1]))
             for j in range(0, n, 2 * w)], axis=0)
    b3 = b.reshape(n // SUBLANES, SUBLANES, b.shape[1])
    ahead = lambda s: pltpu.roll(b3, SUBLANES - s, 1)
    pos = lax.broadcasted_iota(I32, (1, SUBLANES, 1), 1) & (2 * w - 1)
    if w == 2:
        r3 = jnp.where(pos == 0, ahead(2),
                       jnp.where(pos == 1, ahead(1),
                                 jnp.where(pos == 2, b3, pltpu.roll(b3, 1, 1))))
    else:
        r3 = jnp.where(pos == 0, ahead(1), b3)
    return r3.reshape(b.shape)


def _lru_gates(xc, wra, bra, wix, bix, c8):
    r = _sigmoid(_dot(xc, wra) + bra)
    ig = _sigmoid(_dot(xc, wix) + bix)
    log_a = c8 * r
    a = jnp.exp(log_a)
    one_m_a2 = -jnp.tanh(log_a) * (a * a + 1.0)
    u = jnp.sqrt(one_m_a2) * (ig * xc)
    return a, u


def _mixer_block_kernel(x_ref, g1_ref, win_ref, lbp_ref, hgg_ref, cw_ref, cb_ref, wra_ref,
                        bra_ref, wix_ref, bix_ref, c8_ref, cm_ref, wout_ref, *rest, tl, n_cast):
    cast_src = rest[:n_cast]
    xo_ref, hs_ref, ls_ref, cs_ref = rest[n_cast:n_cast + 4]
    cast_dst = rest[n_cast + 4:2 * n_cast + 4]
    (st_scr, h_scr, xpad_scr, hb_scr, q_scr, lf_scr, k_scr, v_scr, g_scr, gl_scr, a_scr, u_scr,
     m_scr) = rest[2 * n_cast + 4:]
    l = pl.program_id(1)
    nl = pl.num_programs(1)
    for src, dst in zip(cast_src, cast_dst):
        dst[...] = src[...].astype(dst.dtype)

    @pl.when(l == 0)
    def _():
        st_scr[...] = jnp.zeros_like(st_scr)
        h_scr[...] = jnp.zeros_like(h_scr)
        xpad_scr[0:8, :] = jnp.zeros((8, LRU_WIDTH), F32)

    hb_scr[...] = _rms(x_ref[0], g1_ref[...]).astype(BF16)
    hb = hb_scr[...]
    zg = lambda j: _dot(hb, win_ref[:, j * HG_WIDTH:(j + 1) * HG_WIDTH])
    q_scr[...] = _silu(zg(0))
    log_f, kk = _hgrn_gates(zg(1), lbp_ref[0:1, :], lbp_ref[1:2, :], lbp_ref[2:3, :])
    lf_scr[...] = log_f
    k_scr[...] = kk
    v_scr[...] = zg(2)
    g_scr[...] = _silu(zg(3))
    xpad_scr[8:8 + tl, :] = zg(4)
    gl_scr[...] = _gelu_tanh(zg(5))
    xc = cb_ref[...]
    for j in range(CONV_K):
        xc = xc + xpad_scr[5 + j:5 + j + tl, :] * cw_ref[j:j + 1, :]
    tail = xpad_scr[tl + 5:tl + 8, :]
    xpad_scr[5:8, :] = tail
    a, u = _lru_gates(xc, wra_ref[...], bra_ref[...], wix_ref[...], bix_ref[...], c8_ref[...])
    a_scr[...] = a
    u_scr[...] = u

    row = lax.broadcasted_iota(I32, (CHUNK, 1), 0)
    ti = lax.broadcasted_iota(I32, (CHUNK, CHUNK), 0)
    si = lax.broadcasted_iota(I32, (CHUNK, CHUNK), 1)
    t3 = lax.broadcasted_iota(I32, (1, DIAG, 1), 1)
    sub3 = lax.broadcasted_iota(I32, (1, SUBLANES, 1), 1)
    nblk = CHUNK // DIAG
    ngrp = CHUNK // SUBLANES
    levels = []
    w = CHUNK // 2
    while w >= DIAG:
        sh = (2 * w).bit_length() - 1
        pairs = ((ti >> sh) == (si >> sh)) & ((ti & (2 * w - 1)) >= w) & ((si & (2 * w - 1)) < w)
        levels.append((w, (row & (2 * w - 1)) >= w, pairs))
        w //= 2

    def chunk_body(c, carry):
        r0 = pl.multiple_of(c * CHUNK, CHUNK)
        rows = pl.ds(r0, CHUNK)

        for gi in range(LRU_WIDTH // LANES):
            cs = slice(gi * LANES, (gi + 1) * LANES)
            aa = a_scr[rows, cs].reshape(ngrp, SUBLANES, LANES)
            uu = u_scr[rows, cs].reshape(ngrp, SUBLANES, LANES)
            d = 1
            while d < SUBLANES:
                keep = sub3 >= d
                a_sh = pltpu.roll(aa, d, 1)
                u_sh = pltpu.roll(uu, d, 1)
                uu = jnp.where(keep, aa * u_sh + uu, uu)
                aa = jnp.where(keep, aa * a_sh, aa)
                d *= 2
            hprev = h_scr[:, cs]
            groups = []
            for j in range(ngrp):
                hj = uu[j] + aa[j] * hprev
                groups.append(hj)
                hprev = hj[SUBLANES - 1:SUBLANES, :]
            hs = jnp.concatenate(groups, axis=0)
            h_scr[:, cs] = hprev
            m_scr[rows, HG_WIDTH + gi * LANES:HG_WIDTH + (gi + 1) * LANES] = (
                hs * gl_scr[rows, cs]).astype(BF16)

        q = q_scr[rows, :]
        k = k_scr[rows, :]
        v = v_scr[rows, :]
        f1, f2, f3 = _split3(lf_scr[rows, :])
        cm = cm_ref[...]
        b = _dot(cm, f1) + _dot(cm, f2) + _dot(cm, f3)

        for hh in range(HG_HEADS):
            cs = slice(hh * HG_D, (hh + 1) * HG_D)
            qh, kh, vh, bh = q[:, cs], k[:, cs], v[:, cs], b[:, cs]
            vb = vh.astype(BF16)
            amat = jnp.zeros((CHUNK, CHUNK), F32)
            for w, upper, pairs in levels:
                dec = _exp_neg(jnp.abs(bh - _level_reference(bh, w)))
                y = (jnp.where(upper, qh, kh) * dec).astype(BF16)
                amat = jnp.where(pairs, _dot_nt(y, y), amat)
            o = _dot(amat.astype(BF16), vb)
            st = st_scr[hh]
            o = o + _dot_nt((qh * jnp.exp(bh)).astype(BF16), st.astype(BF16))
            if DIAG == 1:
                o = o + jnp.sum(qh * kh, axis=-1, keepdims=True) * vh
            else:
                q3 = qh.reshape(nblk, DIAG, HG_D)
                k3 = kh.reshape(nblk, DIAG, HG_D)
                v3 = vh.reshape(nblk, DIAG, HG_D)
                b3 = bh.reshape(nblk, DIAG, HG_D)
                od = jnp.zeros((nblk, DIAG, HG_D), F32)
                for s in range(DIAG):
                    dec = jnp.exp(jnp.minimum(b3 - b3[:, s:s + 1, :], 0.0))
                    p = q3 * (k3[:, s:s + 1, :] * dec)
                    rs = jnp.sum(p, axis=-1, keepdims=True)
                    rs = jnp.where(t3 >= s, rs, 0.0)
                    od = od + rs * v3[:, s:s + 1, :]
                o = o + od.reshape(CHUNK, HG_D)
            bl = bh[CHUNK - 1:CHUNK, :]
            kdec = (kh * jnp.exp(bl - bh)).astype(BF16)
            st_scr[hh] = st * jnp.exp(bl) + _dot_tn(vb, kdec)
            o = o * lax.rsqrt(jnp.mean(o * o, axis=-1, keepdims=True) + EPS)
            o = o * hgg_ref[:, cs]
            o = o * g_scr[rows, cs]
            m_scr[rows, cs] = o.astype(BF16)
        return carry

    lax.fori_loop(0, tl // CHUNK, chunk_body, 0, unroll=tl // CHUNK)
    xo_ref[0] = x_ref[0] + _dot(m_scr[...], wout_ref[...])

    @pl.when(l == nl - 1)
    def _():
        for hh in range(HG_HEADS):
            hs_ref[0, hh] = st_scr[hh].T
        ls_ref[0] = h_scr[...]
        cs_ref[0] = xpad_scr[5:8, :]


def _slab_spec(c, steps, step_of):
    share = 1
    while (c.shape[0] * share) % (steps * 2 * SUBLANES):
        share *= 2
    assert steps % share == 0
    rows = c.shape[0] * share // steps
    return pl.BlockSpec((rows, c.shape[1]), lambda *ids: (step_of(*ids) // share, 0))


def _mixer_block(x, gain, w_in, w_out, layer, mp, cast=(), tl=512):
    bsz, seq, d = x.shape
    tl = min(tl, seq)
    nl = seq // tl
    steps = bsz * nl
    cast_dtypes = [dt for _, dt in cast]
    cast = [c for c, _ in cast]
    full = lambda shape: pl.BlockSpec(shape, lambda b, l: (0,) * len(shape))
    stacked = lambda shape: pl.BlockSpec((None,) + shape, lambda b, l: (layer, 0, 0))
    slab = lambda c: _slab_spec(c, steps, lambda b, l: b * nl + l)
    out_shapes = [
        jax.ShapeDtypeStruct((bsz, seq, d), F32),
        jax.ShapeDtypeStruct((bsz, HG_HEADS, HG_D, HG_D), F32),
        jax.ShapeDtypeStruct((bsz, 1, LRU_WIDTH), F32),
        jax.ShapeDtypeStruct((bsz, CONV_K - 1, LRU_WIDTH), F32),
    ] + [jax.ShapeDtypeStruct(c.shape, dt) for c, dt in zip(cast, cast_dtypes)]
    half = lambda dt: pltpu.VMEM((tl, HG_WIDTH), dt)
    xo, hg, lru, conv, *rounded = pl.pallas_call(
        functools.partial(_mixer_block_kernel, tl=tl, n_cast=len(cast)),
        out_shape=out_shapes,
        grid=(bsz, nl),
        in_specs=[
            pl.BlockSpec((1, tl, d), lambda b, l: (b, l, 0)),
            full((1, d)), stacked((d, IN_WIDTH)),
            full((3, HG_WIDTH)), full((1, HG_WIDTH)), full((CONV_K, LRU_WIDTH)),
            full((1, LRU_WIDTH)), full((LRU_WIDTH, LRU_WIDTH)), full((1, LRU_WIDTH)),
            full((LRU_WIDTH, LRU_WIDTH)), full((1, LRU_WIDTH)), full((1, LRU_WIDTH)),
            full(mp["cm"].shape), stacked((d, d)),
        ] + [slab(c) for c in cast],
        out_specs=[
            pl.BlockSpec((1, tl, d), lambda b, l: (b, l, 0)),
            pl.BlockSpec((1, HG_HEADS, HG_D, HG_D), lambda b, l: (b, 0, 0, 0)),
            pl.BlockSpec((1, 1, LRU_WIDTH), lambda b, l: (b, 0, 0)),
            pl.BlockSpec((1, CONV_K - 1, LRU_WIDTH), lambda b, l: (b, 0, 0)),
        ] + [slab(c) for c in cast],
        scratch_shapes=[
            pltpu.VMEM((HG_HEADS, HG_D, HG_D), F32),
            pltpu.VMEM((1, LRU_WIDTH), F32),
            pltpu.VMEM((tl + 8, LRU_WIDTH), F32),
            pltpu.VMEM((tl, d), BF16),
            half(F32), half(F32), half(F32), half(F32), half(F32), half(F32), half(F32),
            half(F32),
            pltpu.VMEM((tl, d), BF16),
        ],
        compiler_params=_cparams(("parallel", "arbitrary")),
        name="mixer_block",
    )(x, gain.reshape(1, d), w_in, mp["lbp"], mp["hgg"], mp["cw"], mp["cb"], mp["wra"],
      mp["bra"], mp["wix"], mp["bix"], mp["c8"], mp["cm"], w_out, *cast)
    return xo, hg, lru.reshape(bsz, LRU_WIDTH), conv, rounded


def _mixer_step_kernel(z_ref, sh_ref, sl_ref, sc_ref, lbp_ref, hgg_ref, cw_ref, cb_ref, wra_ref,
                       bra_ref, wix_ref, bix_ref, c8_ref, *rest, bb, aliased):
    m_ref, hn_ref, ln_ref, cn_ref, o_scr = rest[1:] if aliased else rest
    z = z_ref[...]
    xr = z[:, 4 * HG_WIDTH:4 * HG_WIDTH + LRU_WIDTH]
    gr = z[:, 4 * HG_WIDTH + LRU_WIDTH:IN_WIDTH]
    buf = sc_ref[...]
    xc = cb_ref[...]
    for j in range(CONV_K - 1):
        xc = xc + buf[:, j * LRU_WIDTH:(j + 1) * LRU_WIDTH] * cw_ref[j:j + 1, :]
    xc = xc + xr * cw_ref[CONV_K - 1:CONV_K, :]
    cn_ref[:, 0:(CONV_K - 2) * LRU_WIDTH] = buf[:, LRU_WIDTH:(CONV_K - 1) * LRU_WIDTH]
    cn_ref[:, (CONV_K - 2) * LRU_WIDTH:(CONV_K - 1) * LRU_WIDTH] = xr
    a, u = _lru_gates(xc, wra_ref[...], bra_ref[...], wix_ref[...], bix_ref[...], c8_ref[...])
    hnew = u + a * sl_ref[...]
    ln_ref[...] = hnew
    m_ref[:, HG_WIDTH:D_MODEL] = (hnew * _gelu_tanh(gr)).astype(m_ref.dtype)

    q = _silu(z[:, 0:HG_WIDTH])
    log_f, k = _hgrn_gates(z[:, HG_WIDTH:2 * HG_WIDTH], lbp_ref[0:1, :], lbp_ref[1:2, :],
                           lbp_ref[2:3, :])
    f = jnp.exp(log_f)
    v = z[:, 2 * HG_WIDTH:3 * HG_WIDTH]
    gate = _silu(z[:, 3 * HG_WIDTH:4 * HG_WIDTH])
    zpad = jnp.zeros((HG_D - bb, HG_D), F32)
    for hh in range(HG_HEADS):
        cs = slice(hh * HG_D, (hh + 1) * HG_D)
        ft = jnp.concatenate([f[:, cs], zpad], axis=0).T
        kt = jnp.concatenate([k[:, cs], zpad], axis=0).T
        qt = jnp.concatenate([q[:, cs], zpad], axis=0).T
        for j in range(bb):
            fcol = jnp.broadcast_to(ft[:, j:j + 1], (HG_D, HG_D))
            kcol = jnp.broadcast_to(kt[:, j:j + 1], (HG_D, HG_D))
            qcol = jnp.broadcast_to(qt[:, j:j + 1], (HG_D, HG_D))
            vrow = jnp.broadcast_to(v[j:j + 1, cs], (HG_D, HG_D))
            sn = fcol * sh_ref[j, hh] + kcol * vrow
            hn_ref[j, hh] = sn
            o_scr[j:j + 1, cs] = jnp.sum(qcol * sn, axis=0, keepdims=True)
    for hh in range(HG_HEADS):
        cs = slice(hh * HG_D, (hh + 1) * HG_D)
        o = o_scr[:, cs]
        o = o * lax.rsqrt(jnp.mean(o * o, axis=-1, keepdims=True) + EPS)
        o = o * hgg_ref[:, cs]
        o = o * gate[:, cs]
        m_ref[:, cs] = o.astype(m_ref.dtype)


def _mixer_step(z, s_hg_all, layer, s_lru, s_conv, mp, hg_buf=None, bb=16):
    bsz = z.shape[0]
    bb = min(bb, bsz)
    cw3 = (CONV_K - 1) * LRU_WIDTH
    full = lambda shape: pl.BlockSpec(shape, lambda i: (0,) * len(shape))
    state_spec = pl.BlockSpec((None, bb, HG_HEADS, HG_D, HG_D), lambda i: (layer, i, 0, 0, 0))
    out_shapes = (
        jax.ShapeDtypeStruct((bsz, D_MODEL), mp["wra"].dtype),
        jax.ShapeDtypeStruct(s_hg_all.shape, F32),
        jax.ShapeDtypeStruct((bsz, LRU_WIDTH), F32),
        jax.ShapeDtypeStruct((bsz, cw3), F32),
    )
    in_specs = [
        pl.BlockSpec((bb, IN_WIDTH), lambda i: (i, 0)),
        state_spec,
        pl.BlockSpec((bb, LRU_WIDTH), lambda i: (i, 0)),
        pl.BlockSpec((bb, cw3), lambda i: (i, 0)),
        full((3, HG_WIDTH)), full((1, HG_WIDTH)), full((CONV_K, LRU_WIDTH)),
        full((1, LRU_WIDTH)), full((LRU_WIDTH, LRU_WIDTH)), full((1, LRU_WIDTH)),
        full((LRU_WIDTH, LRU_WIDTH)), full((1, LRU_WIDTH)), full((1, LRU_WIDTH)),
    ]
    args = [z, s_hg_all, s_lru, s_conv.reshape(bsz, cw3), mp["lbp"], mp["hgg"], mp["cw"],
            mp["cb"], mp["wra"], mp["bra"], mp["wix"], mp["bix"], mp["c8"]]
    aliases = {}
    if hg_buf is not None:
        aliases = {len(args): 1}
        in_specs.append(pl.BlockSpec(memory_space=pl.ANY))
        args.append(hg_buf)
    merged, hg, lru, conv = pl.pallas_call(
        functools.partial(_mixer_step_kernel, bb=bb, aliased=hg_buf is not None),
        out_shape=out_shapes,
        grid=(bsz // bb,),
        in_specs=in_specs,
        out_specs=(
            pl.BlockSpec((bb, D_MODEL), lambda i: (i, 0)),
            state_spec,
            pl.BlockSpec((bb, LRU_WIDTH), lambda i: (i, 0)),
            pl.BlockSpec((bb, cw3), lambda i: (i, 0)),
        ),
        scratch_shapes=[pltpu.VMEM((bb, HG_WIDTH), F32)],
        input_output_aliases=aliases,
        compiler_params=_cparams(("parallel",)),
        name="mixer_step",
    )(*args)
    return merged, hg, lru, conv.reshape(bsz, CONV_K - 1, LRU_WIDTH)


def _router_kernel(x_ref, g_ref, rw_ref, tril_ref, *refs, n_cast, exact_logits):
    cast_src = refs[:n_cast]
    info_ref, idx_ref, cnt_ref, zero_ref = refs[n_cast:n_cast + 4]
    cast_dst = refs[n_cast + 4:2 * n_cast + 4]
    carry_scr = refs[-1]
    i = pl.program_id(0)

    @pl.when(i == 0)
    def _():
        carry_scr[...] = jnp.zeros_like(carry_scr)

    zero_ref[...] = jnp.zeros_like(zero_ref)
    for src, dst in zip(cast_src, cast_dst):
        dst[...] = src[...].astype(dst.dtype)

    h = _rms(x_ref[...], g_ref[...])
    rw = rw_ref[...]
    if exact_logits:
        logits = jnp.dot(h, rw, preferred_element_type=F32, precision=lax.Precision.HIGHEST)
    else:
        h1 = h.astype(BF16)
        h2 = (h - h1.astype(F32)).astype(BF16)
        r1 = rw.astype(BF16)
        r2 = (rw - r1.astype(F32)).astype(BF16)
        logits = _dot(h1, r1) + (_dot(h1, r2) + _dot(h2, r1))
    lane = lax.broadcasted_iota(I32, logits.shape, 1)
    neg = jnp.float32(-jnp.inf)
    logits = jnp.where(lane < N_EXPERTS, logits, neg)
    m1 = jnp.max(logits, axis=-1, keepdims=True)
    i1 = jnp.min(jnp.where(logits == m1, lane, LANES), axis=-1, keepdims=True)
    l2 = jnp.where(lane == i1, neg, logits)
    m2 = jnp.max(l2, axis=-1, keepdims=True)
    i2 = jnp.min(jnp.where(l2 == m2, lane, LANES), axis=-1, keepdims=True)
    e = jnp.exp(m2 - m1)
    p1 = 1.0 / (1.0 + e)
    p2 = e / (1.0 + e)
    oh1 = (lane == i1).astype(F32)
    oh2 = (lane == i2).astype(F32)
    sel = oh1 + oh2
    before = _dot(tril_ref[...], sel.astype(BF16)) + carry_scr[...]
    r1 = jnp.sum(oh1 * before, axis=-1, keepdims=True)
    r2 = jnp.sum(oh2 * before, axis=-1, keepdims=True)
    carry_scr[...] += jnp.sum(sel, axis=0, keepdims=True)
    info = jnp.where(lane == 0, i1.astype(F32), 0.0)
    info = jnp.where(lane == 1, i2.astype(F32), info)
    info = jnp.where(lane == 2, r1, info)
    info = jnp.where(lane == 3, r2, info)
    info = jnp.where(lane == 4, p1, info)
    info = jnp.where(lane == 5, p2, info)
    info_ref[...] = info
    idx_ref[...] = info.T[0:SUBLANES, :].astype(I32)
    cnt_ref[...] = carry_scr[...]


def _router(x, gain, rw_pad, n_rows, cast=(), exact_logits=True, tm=512):
    t, d = x.shape
    tm = min(tm, t)
    steps = t // tm
    zrows = n_rows // steps
    assert zrows * steps == n_rows and zrows % SUBLANES == 0
    slab = lambda c: _slab_spec(c, steps, lambda i: i)
    tril = jnp.tril(jnp.ones((tm, tm), BF16), -1)
    return pl.pallas_call(
        functools.partial(_router_kernel, n_cast=len(cast), exact_logits=exact_logits),
        out_shape=[jax.ShapeDtypeStruct((t, LANES), F32),
                   jax.ShapeDtypeStruct((SUBLANES, t), I32),
                   jax.ShapeDtypeStruct((1, LANES), F32),
                   jax.ShapeDtypeStruct((n_rows, d), F32)]
        + [jax.ShapeDtypeStruct(c.shape, BF16) for c in cast],
        grid=(steps,),
        in_specs=[
            pl.BlockSpec((tm, d), lambda i: (i, 0)),
            pl.BlockSpec((1, d), lambda i: (0, 0)),
            pl.BlockSpec((d, LANES), lambda i: (0, 0)),
            pl.BlockSpec((tm, tm), lambda i: (0, 0)),
        ] + [slab(c) for c in cast],
        out_specs=[pl.BlockSpec((tm, LANES), lambda i: (i, 0)),
                   pl.BlockSpec((SUBLANES, tm), lambda i: (0, i)),
                   pl.BlockSpec((1, LANES), lambda i: (0, 0)),
                   pl.BlockSpec((zrows, d), lambda i: (i, 0))] + [slab(c) for c in cast],
        scratch_shapes=[pltpu.VMEM((1, LANES), F32)],
        compiler_params=_cparams(("arbitrary",)),
        name="router",
    )(x, gain.reshape(1, d), rw_pad, tril, *cast)


def _row_copy(src, dst, s, d, sem):
    return pltpu.make_async_copy(src.at[pl.ds(s, 1)], dst.at[pl.ds(d, 1)], sem)


def _tile_dest(dest, t, td):
    nb = t // td
    dest3 = jnp.concatenate([dest[0].reshape(nb, td), dest[1].reshape(nb, td)], axis=1)
    return dest3.reshape(nb, 1, 2 * td)


def _dispatch_kernel(dest_ref, x_ref, g_ref, xs_in_ref, xs_ref, pk_scr, sems, *, td):
    del xs_in_ref
    i = pl.program_id(0)
    nb = pl.num_programs(0)

    def drain(slot):
        for _ in range(2):
            pltpu.make_async_copy(pk_scr.at[slot], xs_ref.at[pl.ds(0, td)], sems.at[slot]).wait()

    for slot in range(2):
        @pl.when(i % 2 == slot)
        def _(slot=slot):
            pk_scr[slot] = _rms(x_ref[...], g_ref[...])

            def issue(r, c):
                _row_copy(pk_scr.at[slot], xs_ref, r, dest_ref[0, 0, r],
                          sems.at[slot]).start(priority=0)
                _row_copy(pk_scr.at[slot], xs_ref, r, dest_ref[0, 0, td + r],
                          sems.at[slot]).start(priority=1)
                return c

            lax.fori_loop(0, td, issue, 0, unroll=8)

            @pl.when(i > 0)
            def _():
                drain(1 - slot)

            @pl.when(i == nb - 1)
            def _():
                drain(slot)


def _dispatch(x, gain, dest, xs0, td=2048):
    t, d = x.shape
    td = min(td, t)
    n_rows = xs0.shape[0]
    return pl.pallas_call(
        functools.partial(_dispatch_kernel, td=td),
        out_shape=jax.ShapeDtypeStruct((n_rows, d), F32),
        grid=(t // td,),
        in_specs=[
            pl.BlockSpec((1, 1, 2 * td), lambda i: (i, 0, 0), memory_space=pltpu.SMEM),
            pl.BlockSpec((td, d), lambda i: (i, 0)),
            pl.BlockSpec((1, d), lambda i: (0, 0)),
            pl.BlockSpec(memory_space=pl.ANY),
        ],
        out_specs=pl.BlockSpec(memory_space=pl.ANY),
        scratch_shapes=[pltpu.VMEM((2, td, d), F32), pltpu.SemaphoreType.DMA((2,))],
        input_output_aliases={3: 0},
        compiler_params=pltpu.CompilerParams(dimension_semantics=("arbitrary",),
                                             vmem_limit_bytes=VMEM_LIMIT_BYTES,
                                             has_side_effects=True),
        name="dispatch",
    )(_tile_dest(dest, t, td), x, gain.reshape(1, d), xs0)


def _combine_kernel(dest_ref, nxt_ref, x_ref, info_ref, g_ref, ys_ref, o_ref, y_scr, sems, *, td):
    i = pl.program_id(0)
    nb = pl.num_programs(0)

    def fetch(idx_ref, slot):
        def issue(r, c):
            _row_copy(ys_ref, y_scr.at[slot, 0], idx_ref[0, 0, r], r,
                      sems.at[slot]).start(priority=0)
            _row_copy(ys_ref, y_scr.at[slot, 1], idx_ref[0, 0, td + r], r,
                      sems.at[slot]).start(priority=1)
            return c

        lax.fori_loop(0, td, issue, 0, unroll=8)

    def finish(slot):
        for j in range(2):
            pltpu.make_async_copy(ys_ref.at[pl.ds(0, td)], y_scr.at[slot, j],
                                  sems.at[slot]).wait()
        p1 = info_ref[:, 4:5]
        p2 = info_ref[:, 5:6]
        x = x_ref[...] + (p1 * y_scr[slot, 0] + p2 * y_scr[slot, 1])
        o_ref[...] = _rms(x, g_ref[...])

    @pl.when(i == 0)
    def _():
        fetch(dest_ref, 0)

    for slot in range(2):
        @pl.when(i % 2 == slot)
        def _(slot=slot):
            @pl.when(i + 1 < nb)
            def _():
                fetch(nxt_ref, 1 - slot)

            finish(slot)


def _combine(x, ys, dest, info, gain, td=1024):
    t, d = x.shape
    td = min(td, t)
    nb = t // td
    row = pl.BlockSpec((td, d), lambda i: (i, 0))
    tiled = _tile_dest(dest, t, td)
    return pl.pallas_call(
        functools.partial(_combine_kernel, td=td),
        out_shape=jax.ShapeDtypeStruct((t, d), F32),
        grid=(nb,),
        in_specs=[
            pl.BlockSpec((1, 1, 2 * td), lambda i: (i, 0, 0), memory_space=pltpu.SMEM),
            pl.BlockSpec((1, 1, 2 * td), lambda i: (jnp.minimum(i + 1, nb - 1), 0, 0),
                         memory_space=pltpu.SMEM),
            row,
            pl.BlockSpec((td, LANES), lambda i: (i, 0)),
            pl.BlockSpec((1, d), lambda i: (0, 0)),
            pl.BlockSpec(memory_space=pl.ANY),
        ],
        out_specs=row,
        scratch_shapes=[pltpu.VMEM((2, 2, td, d), F32), pltpu.SemaphoreType.DMA((2,))],
        compiler_params=_cparams(("arbitrary",)),
        name="combine",
    )(tiled, tiled, x, info, gain.reshape(1, d), ys)


def _rows2d(w):
    return w.reshape(-1, w.shape[-1])


def _moe_rows(group_sizes, tm_e=TILE_ROWS):
    return (2 * sum(group_sizes) + N_EXPERTS * (tm_e - 1)) // tm_e * tm_e


def _moe(xs_in, norm_g, rw_pad, experts, shared, final_g, tm_e=TILE_ROWS):
    n_rows = _moe_rows([x.shape[0] for x in xs_in], tm_e)
    n_tiles = n_rows // tm_e
    routed = []
    for x in xs_in:
        todo = [(name, w) for name, w in experts if name not in shared]
        steps = x.shape[0] // min(TILE_ROWS, x.shape[0])
        fill = steps * SUBLANES if "moe_rows" in shared else n_rows
        info, idx, cnt, zeros, *rounded = _router(
            x, norm_g, rw_pad, fill, cast=[_rows2d(w) for _, w in todo],
            exact_logits=x.shape[0] <= TILE_ROWS)
        for (name, w), r in zip(todo, rounded):
            shared[name] = r.reshape(w.shape)
        shared.setdefault("moe_rows", zeros)
        routed.append((info, idx, cnt[0, :N_EXPERTS].astype(I32)))
    xs0 = shared["moe_rows"]
    assert xs0.shape[0] == n_rows
    w1, w3, w2 = (shared[name] for name, _ in experts)
    counts = sum(c for _, _, c in routed)
    padded = ((counts + tm_e - 1) // tm_e) * tm_e
    ends = jnp.cumsum(padded)
    starts = ends - padded
    tile_start = jnp.arange(n_tiles, dtype=I32) * tm_e
    tile_expert = jnp.minimum(
        jnp.sum((tile_start[:, None] >= ends[None, :]).astype(I32), axis=1), N_EXPERTS - 1)
    n_valid = (ends[-1] // tm_e).reshape(1).astype(I32)
    dests = []
    base = starts
    for info, idx, cnt in routed:
        e1, e2, r1, r2 = idx[0], idx[1], idx[2], idx[3]
        dests.append(jnp.stack([base[e1] + r1, base[e2] + r2]))
        base = base + cnt
    rows = xs0
    for x, dest in zip(xs_in, dests):
        rows = _dispatch(x, norm_g, dest, rows)
    ys = _ffn(rows, w1, w3, w2, tile_expert.astype(I32), n_valid, tm=tm_e, n_f=1)
    return [_combine(x, ys, dest, info, final_g)
            for x, dest, (info, _, _) in zip(xs_in, dests, routed)]


def _block_diag(w):
    n, c, d = w.shape
    eye = jnp.eye(n, dtype=w.dtype)
    return (w[:, :, None, :] * eye[:, None, :, None]).reshape(n * c, n * d)


def _cum_matrix():
    t = jnp.arange(CHUNK)
    return (t[None, :] <= t[:, None]).astype(BF16)


def _mixer_params(l, lb, hg_norm_g, conv_w, conv_b, w_ra, b_ra, w_ix, b_ix, lru_lambda, wdtype):
    row = lambda a: a.reshape(1, -1).astype(F32)
    return {
        "lbp": jnp.stack([jnp.log(lb), jnp.log1p(-lb), 1.0 - lb]).astype(F32),
        "hgg": row(hg_norm_g[l]),
        "cw": conv_w[l].astype(F32),
        "cb": row(conv_b[l]),
        "wra": _block_diag(w_ra[l].astype(F32)).astype(wdtype),
        "bra": row(b_ra[l]),
        "wix": _block_diag(w_ix[l].astype(F32)).astype(wdtype),
        "bix": row(b_ix[l]),
        "c8": row(-LRU_C * jax.nn.softplus(-lru_lambda[l].astype(F32))),
        "cm": _cum_matrix(),
    }


def _trunk(x, seq_shape, states, prm):
    bsz, seq = seq_shape
    one = jnp.ones((1,), I32)
    new_hg, new_lru, new_conv = [], [], []
    shared = prm["shared"]
    hg_stack = None if states is None else shared.get("hg_seed", states[0])
    for l in range(2):
        mp = prm["mixer"][l]
        if states is None:
            x3 = x.reshape(bsz, seq, D_MODEL)
            jobs = [j for j in prm["side"][l] if j[0] not in shared]
            if l == 0:
                w_in, w_out, layer = prm["w_in"], prm["w_out"], 0
            else:
                w_in, w_out, layer = shared["w_in"], shared["w_out"], 1
            x, s1, s2, s3, done = _mixer_block(
                x3, prm["norm1_g"][l], w_in, w_out, layer, mp,
                cast=[(_rows2d(a), dt) for _, a, dt in jobs])
            for (name, a, _), r in zip(jobs, done):
                shared[name] = r.reshape(a.shape)
            x = x.reshape(bsz * seq, D_MODEL)
            new_hg.append(s1)
        else:
            z = _linear(x, prm["w_in"], l, gain=prm["norm1_g"][l])
            merged, hg_stack, s2, s3 = _mixer_step(z, states[0], l, states[1][l], states[2][l],
                                                   mp, hg_buf=hg_stack)
            x = _linear(merged, prm["w_out"], l, res=x)
        new_lru.append(s2)
        new_conv.append(s3)
        if l == 0:
            n_tiles = max(x.shape[0] // TILE_ROWS, 1)
            if states is None:
                f1, f3, f2 = shared["ffn_w1"], shared["ffn_w3"], shared["ffn_w2"]
            else:
                f1, f3, f2 = prm["ffn_w1"], prm["ffn_w3"], prm["ffn_w2"]
            n_f = 1 if f1.dtype == BF16 else f1.shape[2] // F_CHUNK
            fill = prm["moe_rows"] if states is None else 0
            x = _ffn(x, f1, f3, f2, jnp.zeros((n_tiles,), I32), one * n_tiles,
                     gain=prm["norm2_g"][0], res=True, fill_rows=fill, tm=TILE_ROWS, n_f=n_f,
                     ck=F_CHUNK)
            if fill:
                x, shared["moe_rows"] = x
    hg_all = jnp.stack(new_hg) if states is None else hg_stack
    return x, hg_all, jnp.stack(new_lru), jnp.stack(new_conv)


def kernel(x_prompt, x_sample, state_hgrn, state_lru, state_conv, norm1_g, w_in, lower_bounds,
           hg_norm_g, conv_w, conv_b, w_ra, b_ra, w_ix, b_ix, lru_lambda, w_out, norm2_g,
           ffn_w1, ffn_w3, ffn_w2, router_w, moe_w1, moe_w3, moe_w2, final_norm_g):
    lb_all = jnp.cumsum(jax.nn.softmax(lower_bounds.astype(F32), axis=0), axis=0)
    lb_all = lb_all - lb_all[0]
    f32 = lambda a: a.astype(F32)
    common = {
        "experts": [("moe_w1", f32(moe_w1[0])), ("moe_w3", f32(moe_w3[0])),
                    ("moe_w2", f32(moe_w2[0]))],
        "side": {0: [("moe_w3", f32(moe_w3[0]), BF16), ("ffn_w1", f32(ffn_w1), BF16),
                     ("ffn_w3", f32(ffn_w3), BF16), ("ffn_w2", f32(ffn_w2), BF16),
                     ("w_in", f32(w_in), BF16), ("w_out", f32(w_out), BF16)],
                 1: [("moe_w2", f32(moe_w2[0]), BF16), ("hg_seed", f32(state_hgrn), F32)]},
        "shared": {},
        "moe_rows": _moe_rows([x_prompt.shape[0] * x_prompt.shape[1], x_sample.shape[0]]),
        "rw_pad": jnp.pad(router_w[0].astype(F32), ((0, 0), (0, LANES - N_EXPERTS))),
        "norm1_g": norm1_g, "norm2_g": norm2_g, "final_norm_g": final_norm_g,
    }
    mixer = lambda wdtype: [_mixer_params(l, lb_all[l], hg_norm_g, conv_w, conv_b, w_ra, b_ra,
                                          w_ix, b_ix, lru_lambda, wdtype) for l in range(2)]
    prompt = dict(common, mixer=mixer(BF16), w_in=w_in[:1].astype(BF16),
                  w_out=w_out[:1].astype(BF16))

    bp, sp, d = x_prompt.shape
    bs = x_sample.shape[0]
    x_p, hg_p, lru_p, conv_p = _trunk(x_prompt.reshape(bp * sp, d), (bp, sp), None, prompt)
    sample = dict(common, mixer=mixer(F32), w_in=f32(w_in), w_out=f32(w_out),
                  ffn_w1=f32(ffn_w1), ffn_w3=f32(ffn_w3), ffn_w2=f32(ffn_w2))
    x_s, hg_s, lru_s, conv_s = _trunk(x_sample.reshape(bs, d), (bs, 1),
                                      (state_hgrn, state_lru, state_conv), sample)
    y_p, y_s = _moe([x_p, x_s], norm2_g[1], common["rw_pad"], common["experts"],
                    common["shared"], final_norm_g)
    return (y_p.reshape(bp, sp, d), y_s.reshape(bs, 1, d), hg_p, lru_p, conv_p, hg_s, lru_s,
            conv_s)
```
